```python
import jax, jax.numpy as jnp
from jax import lax
import numpy as np

D_MODEL = 1024
BATCH = 8
SEQ = 2048
DEPTH = 2

MLA_HEADS = 8
QK_NOPE_DIM = 64
QK_ROPE_DIM = 32
QK_HEAD_DIM = QK_NOPE_DIM + QK_ROPE_DIM
V_HEAD_DIM = 64
Q_LORA_RANK = 384
KV_LORA_RANK = 256
ROPE_THETA = 10000.0
Q_BLOCK = 128
RWKV_HEAD_DIM = 64
RWKV_HEADS = 4
RWKV_WIDTH = RWKV_HEADS * RWKV_HEAD_DIM
DECAY_LORA = 64
AAA_LORA = 64
GATE_LORA = 128
MV_LORA = 32
GN_EPS = 64e-5
CONV_WIDTH = 256
CONV_K = 3
D_FF = 4 * D_MODEL
N_BRANCH = 3
NORM_EPS = 1e-6

GATE_COLS = N_BRANCH * D_MODEL
MLA_COLS = Q_LORA_RANK + KV_LORA_RANK + QK_ROPE_DIM
RWKV_COLS = 3 * RWKV_WIDTH + DECAY_LORA + AAA_LORA + GATE_LORA
CONV_COLS = 3 * CONV_WIDTH
IN_COLS = GATE_COLS + MLA_COLS + RWKV_COLS + CONV_COLS

kernel_name = 'hybrid_mla_rwkv7_shortconv_block'


def _split(x, sizes):
    idx = np.cumsum(sizes)[:-1].tolist()
    return jnp.split(x, idx, axis=-1)


def rms_norm(x, g, eps=NORM_EPS):
    xf = x.astype(jnp.float32)
    y = xf * lax.rsqrt(jnp.mean(xf * xf, axis=-1, keepdims=True) + eps)
    return (y * g.astype(jnp.float32)).astype(x.dtype)


def token_shift(x):
    return jnp.pad(x, ((0, 0), (1, 0), (0, 0)))[:, :-1]


def rope_tables(positions):
    freqs = ROPE_THETA ** (-(jnp.arange(QK_ROPE_DIM // 2, dtype=jnp.float32) * 2.0 / QK_ROPE_DIM))
    ang = positions.astype(jnp.float32)[..., None] * freqs
    return jnp.cos(ang)[:, :, None, :], jnp.sin(ang)[:, :, None, :]


def apply_rope(x, cos, sin):
    x1, x2 = x[..., :QK_ROPE_DIM // 2], x[..., QK_ROPE_DIM // 2:]
    cos, sin = cos.astype(x.dtype), sin.astype(x.dtype)
    return jnp.concatenate([x1 * cos - x2 * sin, x1 * sin + x2 * cos], axis=-1)


def causal_block_attention(q, k, v):
    scale = QK_HEAD_DIM ** -0.5
    qh, kh, vh = (jnp.swapaxes(t, 1, 2) for t in (q, k, v))
    seq = qh.shape[2]
    outs = []
    for start in range(0, seq, Q_BLOCK):
        stop = start + Q_BLOCK
        s = jnp.einsum('bhqd,bhkd->bhqk', qh[:, :, start:stop], kh[:, :, :stop]).astype(jnp.float32) * scale
        causal = (start + jnp.arange(Q_BLOCK))[:, None] >= jnp.arange(stop)[None, :]
        p = jax.nn.softmax(jnp.where(causal, s, -jnp.inf), axis=-1)
        outs.append(jnp.einsum('bhqk,bhkd->bhqd', p.astype(vh.dtype), vh[:, :, :stop]))
    return jnp.swapaxes(jnp.concatenate(outs, axis=2), 1, 2)


def mla_branch(cols, positions, q_a_norm, wq_b, kv_a_norm, wkv_b, q_norm, k_norm):
    bsz, seq, _ = cols.shape
    c_q, c_kv, k_pe = _split(cols, [Q_LORA_RANK, KV_LORA_RANK, QK_ROPE_DIM])
    q = (rms_norm(c_q, q_a_norm) @ wq_b).reshape(bsz, seq, MLA_HEADS, QK_HEAD_DIM)
    kv = (rms_norm(c_kv, kv_a_norm) @ wkv_b).reshape(bsz, seq, MLA_HEADS, QK_NOPE_DIM + V_HEAD_DIM)
    k_nope, v = kv[..., :QK_NOPE_DIM], kv[..., QK_NOPE_DIM:]
    k_pe = jnp.broadcast_to(k_pe[:, :, None, :], (bsz, seq, MLA_HEADS, QK_ROPE_DIM))
    k = jnp.concatenate([k_nope, k_pe], axis=-1)
    q = rms_norm(q, q_norm)
    k = rms_norm(k, k_norm)
    cos, sin = rope_tables(positions)
    q = jnp.concatenate([q[..., :QK_NOPE_DIM], apply_rope(q[..., QK_NOPE_DIM:], cos, sin)], axis=-1)
    k = jnp.concatenate([k[..., :QK_NOPE_DIM], apply_rope(k[..., QK_NOPE_DIM:], cos, sin)], axis=-1)
    o = causal_block_attention(q, k, v)
    return o.reshape(bsz, seq, MLA_HEADS * V_HEAD_DIM)


def wkv7_scan(r, w, k, v, a, b):
    def step(state, inp):
        r_t, w_t, k_t, v_t, a_t, b_t = inp
        sa = jnp.einsum('bhvk,bhk->bhv', state, a_t)
        state = state * w_t[:, :, None, :] + sa[..., None] * b_t[:, :, None, :] + v_t[..., None] * k_t[:, :, None, :]
        return state, jnp.einsum('bhvk,bhk->bhv', state, r_t)
    bsz, _, heads, n = r.shape
    xs = tuple(jnp.moveaxis(t, 1, 0) for t in (r, w, k, v, a, b))
    s0 = jnp.zeros((bsz, heads, n, n), jnp.float32)
    _, ys = lax.scan(step, s0, xs)
    return jnp.moveaxis(ys, 0, 1)


def rwkv7_branch(cols, h, v_first, mu, w0, w2, a0, a2, g2, k_k, k_a, r_k, ln_w, ln_b, vres):
    bsz, seq, _ = cols.shape
    f32 = jnp.float32
    cols = cols + (token_shift(cols) - cols) * mu
    r, k, v, xw, xa, xg = _split(cols, [RWKV_WIDTH, RWKV_WIDTH, RWKV_WIDTH, DECAY_LORA, AAA_LORA, GATE_LORA])
    log_w = -jax.nn.softplus(-(w0 + jnp.tanh(xw) @ w2).astype(f32)) - 0.5
    decay = jnp.exp(-jnp.exp(log_w))
    a = jax.nn.sigmoid(a0 + xa @ a2)
    g = jax.nn.sigmoid(xg) @ g2
    if vres is None:
        v_first = v
    else:
        v1, v_mu, v0, v2 = vres
        xv = h @ v1
        xv = xv + (token_shift(xv) - xv) * v_mu
        v = v + (v_first - v) * jax.nn.sigmoid(v0 + xv @ v2)
    heads = lambda t: t.reshape(bsz, seq, RWKV_HEADS, RWKV_HEAD_DIM).astype(f32)
    kk = heads(k * k_k)
    kk = kk / jnp.maximum(jnp.linalg.norm(kk, axis=-1, keepdims=True), 1e-12)
    k = k * (1 + (a - 1) * k_a)
    rh, kh, vh, ah = heads(r), heads(k), heads(v), heads(a)
    y = wkv7_scan(rh, heads(decay), kh, vh, -kk, kk * ah)
    mean = jnp.mean(y, axis=-1, keepdims=True)
    var = jnp.mean(jnp.square(y - mean), axis=-1, keepdims=True)
    y = ((y - mean) * lax.rsqrt(var + GN_EPS)).reshape(bsz, seq, RWKV_WIDTH)
    y = y * ln_w.astype(f32) + ln_b.astype(f32)
    bonus = jnp.sum(rh * kh * r_k.astype(f32), axis=-1, keepdims=True) * vh
    y = (y + bonus.reshape(bsz, seq, RWKV_WIDTH)).astype(cols.dtype)
    return y * g, v_first


def short_conv_branch(cols, conv_w):
    b_gate, c_gate, xc = _split(cols, [CONV_WIDTH, CONV_WIDTH, CONV_WIDTH])
    u = c_gate * xc
    y = lax.conv_general_dilated(
        u, conv_w.astype(u.dtype)[:, None, :], window_strides=(1,), padding=[(CONV_K - 1, 0)],
        dimension_numbers=('NWC', 'WIO', 'NWC'), feature_group_count=CONV_WIDTH)
    return b_gate * y


def _fwd_setup_inputs(seed: int = 0) -> dict:
    key = jax.random.key(seed)
    ks = iter(jax.random.split(key, 64))
    nrm = lambda shape, scale: jax.random.normal(next(ks), shape, jnp.float32) * scale
    gain = lambda shape: 1.0 + nrm(shape, 0.05)
    unif = lambda shape, lo, hi: jax.random.uniform(next(ks), shape, jnp.float32, lo, hi)
    L, Lv = DEPTH, DEPTH - 1
    x = nrm((BATCH, SEQ, D_MODEL), 1.0)
    offsets = jax.random.randint(next(ks), (BATCH, 1), 0, 4096, dtype=jnp.int32)
    positions = offsets + jnp.arange(SEQ, dtype=jnp.int32)[None, :]
    return {
        'x': x,
        'positions': positions,
        'attn_norm': gain((L, D_MODEL)),
        'w_in': nrm((L, D_MODEL, IN_COLS), D_MODEL ** -0.5),
        'mla_q_a_norm': gain((L, Q_LORA_RANK)),
        'mla_wq_b': nrm((L, Q_LORA_RANK, MLA_HEADS * QK_HEAD_DIM), Q_LORA_RANK ** -0.5),
        'mla_kv_a_norm': gain((L, KV_LORA_RANK)),
        'mla_wkv_b': nrm((L, KV_LORA_RANK, MLA_HEADS * (QK_NOPE_DIM + V_HEAD_DIM)), KV_LORA_RANK ** -0.5),
        'mla_q_norm': gain((L, QK_HEAD_DIM)),
        'mla_k_norm': gain((L, QK_HEAD_DIM)),
        'mla_w_o': nrm((L, MLA_HEADS * V_HEAD_DIM, D_MODEL), (MLA_HEADS * V_HEAD_DIM) ** -0.5),
        'rwkv_mu': unif((L, RWKV_COLS), 0.0, 1.0),
        'rwkv_w0': unif((L, RWKV_WIDTH), -6.0, -1.0),
        'rwkv_w2': nrm((L, DECAY_LORA, RWKV_WIDTH), 0.5 * DECAY_LORA ** -0.5),
        'rwkv_a0': nrm((L, RWKV_WIDTH), 0.1),
        'rwkv_a2': nrm((L, AAA_LORA, RWKV_WIDTH), 0.5 * AAA_LORA ** -0.5),
        'rwkv_g2': nrm((L, GATE_LORA, RWKV_WIDTH), GATE_LORA ** -0.5),
        'rwkv_k_k': 0.85 + nrm((L, RWKV_WIDTH), 0.05),
        'rwkv_k_a': gain((L, RWKV_WIDTH)),
        'rwkv_r_k': nrm((L, RWKV_HEADS, RWKV_HEAD_DIM), 0.1),
        'rwkv_ln_w': gain((L, RWKV_WIDTH)),
        'rwkv_ln_b': nrm((L, RWKV_WIDTH), 0.02),
        'rwkv_w_o': nrm((L, RWKV_WIDTH, D_MODEL), RWKV_WIDTH ** -0.5),
        'rwkv_v1': nrm((Lv, D_MODEL, MV_LORA), D_MODEL ** -0.5),
        'rwkv_v_mu': unif((Lv, MV_LORA), 0.0, 1.0),
        'rwkv_v0': nrm((Lv, RWKV_WIDTH), 0.1),
        'rwkv_v2': nrm((Lv, MV_LORA, RWKV_WIDTH), 0.5 * MV_LORA ** -0.5),
        'conv_w': nrm((L, CONV_K, CONV_WIDTH), CONV_K ** -0.5),
        'conv_w_o': nrm((L, CONV_WIDTH, D_MODEL), CONV_WIDTH ** -0.5),
        'w_out': nrm((L, D_MODEL, D_MODEL), D_MODEL ** -0.5),
        'mlp_norm': gain((L, D_MODEL)),
        'w_up': nrm((L, D_MODEL, D_FF), D_MODEL ** -0.5),
        'w_down': nrm((L, D_FF, D_MODEL), D_FF ** -0.5),
    }


def _fwd_reference(x, positions, attn_norm, w_in, mla_q_a_norm, mla_wq_b, mla_kv_a_norm, mla_wkv_b,
              mla_q_norm, mla_k_norm, mla_w_o, rwkv_mu, rwkv_w0, rwkv_w2, rwkv_a0, rwkv_a2, rwkv_g2,
              rwkv_k_k, rwkv_k_a, rwkv_r_k, rwkv_ln_w, rwkv_ln_b, rwkv_w_o, rwkv_v1, rwkv_v_mu,
              rwkv_v0, rwkv_v2, conv_w, conv_w_o, w_out, mlp_norm, w_up, w_down):
    bsz, seq, _ = x.shape
    v_first = None
    for l in range(DEPTH):
        h = rms_norm(x, attn_norm[l])
        proj = h @ w_in[l]
        gate_cols, mla_cols, rwkv_cols, conv_cols = _split(proj, [GATE_COLS, MLA_COLS, RWKV_COLS, CONV_COLS])
        o_a = mla_branch(mla_cols, positions, mla_q_a_norm[l], mla_wq_b[l], mla_kv_a_norm[l],
                         mla_wkv_b[l], mla_q_norm[l], mla_k_norm[l]) @ mla_w_o[l]
        vres = None if l == 0 else (rwkv_v1[l - 1], rwkv_v_mu[l - 1], rwkv_v0[l - 1], rwkv_v2[l - 1])
        o_b, v_first = rwkv7_branch(rwkv_cols, h, v_first, rwkv_mu[l], rwkv_w0[l], rwkv_w2[l], rwkv_a0[l],
                                    rwkv_a2[l], rwkv_g2[l], rwkv_k_k[l], rwkv_k_a[l], rwkv_r_k[l],
                                    rwkv_ln_w[l], rwkv_ln_b[l], vres)
        o_b = o_b @ rwkv_w_o[l]
        o_c = short_conv_branch(conv_cols, conv_w[l]) @ conv_w_o[l]
        g = jax.nn.sigmoid(gate_cols).reshape(bsz, seq, N_BRANCH, D_MODEL)
        merged = g[:, :, 0] * o_a + g[:, :, 1] * o_b + g[:, :, 2] * o_c
        x = x + merged @ w_out[l]
        h2 = rms_norm(x, mlp_norm[l])
        x = x + jnp.square(jax.nn.relu(h2 @ w_up[l])) @ w_down[l]
    return x


import jax as _jax
import jax.numpy as _jnp

TWIN_FORMAT = 'train_step'
FWD_PARAMS = ['x', 'positions', 'attn_norm', 'w_in', 'mla_q_a_norm', 'mla_wq_b', 'mla_kv_a_norm', 'mla_wkv_b', 'mla_q_norm', 'mla_k_norm', 'mla_w_o', 'rwkv_mu', 'rwkv_w0', 'rwkv_w2', 'rwkv_a0', 'rwkv_a2', 'rwkv_g2', 'rwkv_k_k', 'rwkv_k_a', 'rwkv_r_k', 'rwkv_ln_w', 'rwkv_ln_b', 'rwkv_w_o', 'rwkv_v1', 'rwkv_v_mu', 'rwkv_v0', 'rwkv_v2', 'conv_w', 'conv_w_o', 'w_out', 'mlp_norm', 'w_up', 'w_down']
TWIN_WEIGHTS = ['attn_norm', 'w_in', 'mla_q_a_norm', 'mla_wq_b', 'mla_kv_a_norm', 'mla_wkv_b', 'mla_q_norm', 'mla_k_norm', 'mla_w_o', 'rwkv_mu', 'rwkv_w0', 'rwkv_w2', 'rwkv_a0', 'rwkv_a2', 'rwkv_g2', 'rwkv_k_k', 'rwkv_k_a', 'rwkv_r_k', 'rwkv_ln_w', 'rwkv_ln_b', 'rwkv_w_o', 'rwkv_v1', 'rwkv_v_mu', 'rwkv_v0', 'rwkv_v2', 'conv_w', 'conv_w_o', 'w_out', 'mlp_norm', 'w_up', 'w_down']
TWIN_DIFF_INPUT = 'x'
TWIN_INPUTS = ['x', 'positions', 'attn_norm', 'w_in', 'mla_q_a_norm', 'mla_wq_b', 'mla_kv_a_norm', 'mla_wkv_b', 'mla_q_norm', 'mla_k_norm', 'mla_w_o', 'rwkv_mu', 'rwkv_w0', 'rwkv_w2', 'rwkv_a0', 'rwkv_a2', 'rwkv_g2', 'rwkv_k_k', 'rwkv_k_a', 'rwkv_r_k', 'rwkv_ln_w', 'rwkv_ln_b', 'rwkv_w_o', 'rwkv_v1', 'rwkv_v_mu', 'rwkv_v0', 'rwkv_v2', 'conv_w', 'conv_w_o', 'w_out', 'mlp_norm', 'w_up', 'w_down', 'loss_target', 'm_attn_norm', 'm_w_in', 'm_mla_q_a_norm', 'm_mla_wq_b', 'm_mla_kv_a_norm', 'm_mla_wkv_b', 'm_mla_q_norm', 'm_mla_k_norm', 'm_mla_w_o', 'm_rwkv_mu', 'm_rwkv_w0', 'm_rwkv_w2', 'm_rwkv_a0', 'm_rwkv_a2', 'm_rwkv_g2', 'm_rwkv_k_k', 'm_rwkv_k_a', 'm_rwkv_r_k', 'm_rwkv_ln_w', 'm_rwkv_ln_b', 'm_rwkv_w_o', 'm_rwkv_v1', 'm_rwkv_v_mu', 'm_rwkv_v0', 'm_rwkv_v2', 'm_conv_w', 'm_conv_w_o', 'm_w_out', 'm_mlp_norm', 'm_w_up', 'm_w_down', 'v_attn_norm', 'v_w_in', 'v_mla_q_a_norm', 'v_mla_wq_b', 'v_mla_kv_a_norm', 'v_mla_wkv_b', 'v_mla_q_norm', 'v_mla_k_norm', 'v_mla_w_o', 'v_rwkv_mu', 'v_rwkv_w0', 'v_rwkv_w2', 'v_rwkv_a0', 'v_rwkv_a2', 'v_rwkv_g2', 'v_rwkv_k_k', 'v_rwkv_k_a', 'v_rwkv_r_k', 'v_rwkv_ln_w', 'v_rwkv_ln_b', 'v_rwkv_w_o', 'v_rwkv_v1', 'v_rwkv_v_mu', 'v_rwkv_v0', 'v_rwkv_v2', 'v_conv_w', 'v_conv_w_o', 'v_w_out', 'v_mlp_norm', 'v_w_up', 'v_w_down']
TWIN_OUTPUTS = ['loss', 'grad_x', 'grad_attn_norm', 'grad_w_in', 'grad_mla_q_a_norm', 'grad_mla_wq_b', 'grad_mla_kv_a_norm', 'grad_mla_wkv_b', 'grad_mla_q_norm', 'grad_mla_k_norm', 'grad_mla_w_o', 'grad_rwkv_mu', 'grad_rwkv_w0', 'grad_rwkv_w2', 'grad_rwkv_a0', 'grad_rwkv_a2', 'grad_rwkv_g2', 'grad_rwkv_k_k', 'grad_rwkv_k_a', 'grad_rwkv_r_k', 'grad_rwkv_ln_w', 'grad_rwkv_ln_b', 'grad_rwkv_w_o', 'grad_rwkv_v1', 'grad_rwkv_v_mu', 'grad_rwkv_v0', 'grad_rwkv_v2', 'grad_conv_w', 'grad_conv_w_o', 'grad_w_out', 'grad_mlp_norm', 'grad_w_up', 'grad_w_down', 'delta_attn_norm', 'delta_w_in', 'delta_mla_q_a_norm', 'delta_mla_wq_b', 'delta_mla_kv_a_norm', 'delta_mla_wkv_b', 'delta_mla_q_norm', 'delta_mla_k_norm', 'delta_mla_w_o', 'delta_rwkv_mu', 'delta_rwkv_w0', 'delta_rwkv_w2', 'delta_rwkv_a0', 'delta_rwkv_a2', 'delta_rwkv_g2', 'delta_rwkv_k_k', 'delta_rwkv_k_a', 'delta_rwkv_r_k', 'delta_rwkv_ln_w', 'delta_rwkv_ln_b', 'delta_rwkv_w_o', 'delta_rwkv_v1', 'delta_rwkv_v_mu', 'delta_rwkv_v0', 'delta_rwkv_v2', 'delta_conv_w', 'delta_conv_w_o', 'delta_w_out', 'delta_mlp_norm', 'delta_w_up', 'delta_w_down', 'new_m_attn_norm', 'new_m_w_in', 'new_m_mla_q_a_norm', 'new_m_mla_wq_b', 'new_m_mla_kv_a_norm', 'new_m_mla_wkv_b', 'new_m_mla_q_norm', 'new_m_mla_k_norm', 'new_m_mla_w_o', 'new_m_rwkv_mu', 'new_m_rwkv_w0', 'new_m_rwkv_w2', 'new_m_rwkv_a0', 'new_m_rwkv_a2', 'new_m_rwkv_g2', 'new_m_rwkv_k_k', 'new_m_rwkv_k_a', 'new_m_rwkv_r_k', 'new_m_rwkv_ln_w', 'new_m_rwkv_ln_b', 'new_m_rwkv_w_o', 'new_m_rwkv_v1', 'new_m_rwkv_v_mu', 'new_m_rwkv_v0', 'new_m_rwkv_v2', 'new_m_conv_w', 'new_m_conv_w_o', 'new_m_w_out', 'new_m_mlp_norm', 'new_m_w_up', 'new_m_w_down', 'new_v_attn_norm', 'new_v_w_in', 'new_v_mla_q_a_norm', 'new_v_mla_wq_b', 'new_v_mla_kv_a_norm', 'new_v_mla_wkv_b', 'new_v_mla_q_norm', 'new_v_mla_k_norm', 'new_v_mla_w_o', 'new_v_rwkv_mu', 'new_v_rwkv_w0', 'new_v_rwkv_w2', 'new_v_rwkv_a0', 'new_v_rwkv_a2', 'new_v_rwkv_g2', 'new_v_rwkv_k_k', 'new_v_rwkv_k_a', 'new_v_rwkv_r_k', 'new_v_rwkv_ln_w', 'new_v_rwkv_ln_b', 'new_v_rwkv_w_o', 'new_v_rwkv_v1', 'new_v_rwkv_v_mu', 'new_v_rwkv_v0', 'new_v_rwkv_v2', 'new_v_conv_w', 'new_v_conv_w_o', 'new_v_w_out', 'new_v_mlp_norm', 'new_v_w_up', 'new_v_w_down']
TWIN_LEAF_KINDS = {'loss': 'loss', 'grad_x': 'grad_x', 'grad_attn_norm': 'grad_w', 'grad_w_in': 'grad_w', 'grad_mla_q_a_norm': 'grad_w', 'grad_mla_wq_b': 'grad_w', 'grad_mla_kv_a_norm': 'grad_w', 'grad_mla_wkv_b': 'grad_w', 'grad_mla_q_norm': 'grad_w', 'grad_mla_k_norm': 'grad_w', 'grad_mla_w_o': 'grad_w', 'grad_rwkv_mu': 'grad_w', 'grad_rwkv_w0': 'grad_w', 'grad_rwkv_w2': 'grad_w', 'grad_rwkv_a0': 'grad_w', 'grad_rwkv_a2': 'grad_w', 'grad_rwkv_g2': 'grad_w', 'grad_rwkv_k_k': 'grad_w', 'grad_rwkv_k_a': 'grad_w', 'grad_rwkv_r_k': 'grad_w', 'grad_rwkv_ln_w': 'grad_w', 'grad_rwkv_ln_b': 'grad_w', 'grad_rwkv_w_o': 'grad_w', 'grad_rwkv_v1': 'grad_w', 'grad_rwkv_v_mu': 'grad_w', 'grad_rwkv_v0': 'grad_w', 'grad_rwkv_v2': 'grad_w', 'grad_conv_w': 'grad_w', 'grad_conv_w_o': 'grad_w', 'grad_w_out': 'grad_w', 'grad_mlp_norm': 'grad_w', 'grad_w_up': 'grad_w', 'grad_w_down': 'grad_w', 'delta_attn_norm': 'delta_w', 'delta_w_in': 'delta_w', 'delta_mla_q_a_norm': 'delta_w', 'delta_mla_wq_b': 'delta_w', 'delta_mla_kv_a_norm': 'delta_w', 'delta_mla_wkv_b': 'delta_w', 'delta_mla_q_norm': 'delta_w', 'delta_mla_k_norm': 'delta_w', 'delta_mla_w_o': 'delta_w', 'delta_rwkv_mu': 'delta_w', 'delta_rwkv_w0': 'delta_w', 'delta_rwkv_w2': 'delta_w', 'delta_rwkv_a0': 'delta_w', 'delta_rwkv_a2': 'delta_w', 'delta_rwkv_g2': 'delta_w', 'delta_rwkv_k_k': 'delta_w', 'delta_rwkv_k_a': 'delta_w', 'delta_rwkv_r_k': 'delta_w', 'delta_rwkv_ln_w': 'delta_w', 'delta_rwkv_ln_b': 'delta_w', 'delta_rwkv_w_o': 'delta_w', 'delta_rwkv_v1': 'delta_w', 'delta_rwkv_v_mu': 'delta_w', 'delta_rwkv_v0': 'delta_w', 'delta_rwkv_v2': 'delta_w', 'delta_conv_w': 'delta_w', 'delta_conv_w_o': 'delta_w', 'delta_w_out': 'delta_w', 'delta_mlp_norm': 'delta_w', 'delta_w_up': 'delta_w', 'delta_w_down': 'delta_w', 'new_m_attn_norm': 'new_m', 'new_m_w_in': 'new_m', 'new_m_mla_q_a_norm': 'new_m', 'new_m_mla_wq_b': 'new_m', 'new_m_mla_kv_a_norm': 'new_m', 'new_m_mla_wkv_b': 'new_m', 'new_m_mla_q_norm': 'new_m', 'new_m_mla_k_norm': 'new_m', 'new_m_mla_w_o': 'new_m', 'new_m_rwkv_mu': 'new_m', 'new_m_rwkv_w0': 'new_m', 'new_m_rwkv_w2': 'new_m', 'new_m_rwkv_a0': 'new_m', 'new_m_rwkv_a2': 'new_m', 'new_m_rwkv_g2': 'new_m', 'new_m_rwkv_k_k': 'new_m', 'new_m_rwkv_k_a': 'new_m', 'new_m_rwkv_r_k': 'new_m', 'new_m_rwkv_ln_w': 'new_m', 'new_m_rwkv_ln_b': 'new_m', 'new_m_rwkv_w_o': 'new_m', 'new_m_rwkv_v1': 'new_m', 'new_m_rwkv_v_mu': 'new_m', 'new_m_rwkv_v0': 'new_m', 'new_m_rwkv_v2': 'new_m', 'new_m_conv_w': 'new_m', 'new_m_conv_w_o': 'new_m', 'new_m_w_out': 'new_m', 'new_m_mlp_norm': 'new_m', 'new_m_w_up': 'new_m', 'new_m_w_down': 'new_m', 'new_v_attn_norm': 'new_v', 'new_v_w_in': 'new_v', 'new_v_mla_q_a_norm': 'new_v', 'new_v_mla_wq_b': 'new_v', 'new_v_mla_kv_a_norm': 'new_v', 'new_v_mla_wkv_b': 'new_v', 'new_v_mla_q_norm': 'new_v', 'new_v_mla_k_norm': 'new_v', 'new_v_mla_w_o': 'new_v', 'new_v_rwkv_mu': 'new_v', 'new_v_rwkv_w0': 'new_v', 'new_v_rwkv_w2': 'new_v', 'new_v_rwkv_a0': 'new_v', 'new_v_rwkv_a2': 'new_v', 'new_v_rwkv_g2': 'new_v', 'new_v_rwkv_k_k': 'new_v', 'new_v_rwkv_k_a': 'new_v', 'new_v_rwkv_r_k': 'new_v', 'new_v_rwkv_ln_w': 'new_v', 'new_v_rwkv_ln_b': 'new_v', 'new_v_rwkv_w_o': 'new_v', 'new_v_rwkv_v1': 'new_v', 'new_v_rwkv_v_mu': 'new_v', 'new_v_rwkv_v0': 'new_v', 'new_v_rwkv_v2': 'new_v', 'new_v_conv_w': 'new_v', 'new_v_conv_w_o': 'new_v', 'new_v_w_out': 'new_v', 'new_v_mlp_norm': 'new_v', 'new_v_w_up': 'new_v', 'new_v_w_down': 'new_v'}


def _forward(args):
    return _fwd_reference(*[args[k] for k in FWD_PARAMS])


def _output_shape():
    out = _jax.eval_shape(lambda: _forward(_fwd_setup_inputs(0)))
    return out.shape, out.dtype

N_MICROBATCH = 1
ADAM_LR = 0.001
ADAM_B1 = 0.9
ADAM_B2 = 0.999
ADAM_EPS = 1e-08
ADAM_WD = 0.01
ADAM_STEP = 10
PER_EXAMPLE_BATCH_AXIS = {'x': 0, 'positions': 0, 'loss_target': 0}
SHARED_INPUTS = []
_WEIGHT_DTYPES = {'attn_norm': _jnp.float32, 'w_in': _jnp.float32, 'mla_q_a_norm': _jnp.float32, 'mla_wq_b': _jnp.float32, 'mla_kv_a_norm': _jnp.float32, 'mla_wkv_b': _jnp.float32, 'mla_q_norm': _jnp.float32, 'mla_k_norm': _jnp.float32, 'mla_w_o': _jnp.float32, 'rwkv_mu': _jnp.float32, 'rwkv_w0': _jnp.float32, 'rwkv_w2': _jnp.float32, 'rwkv_a0': _jnp.float32, 'rwkv_a2': _jnp.float32, 'rwkv_g2': _jnp.float32, 'rwkv_k_k': _jnp.float32, 'rwkv_k_a': _jnp.float32, 'rwkv_r_k': _jnp.float32, 'rwkv_ln_w': _jnp.float32, 'rwkv_ln_b': _jnp.float32, 'rwkv_w_o': _jnp.float32, 'rwkv_v1': _jnp.float32, 'rwkv_v_mu': _jnp.float32, 'rwkv_v0': _jnp.float32, 'rwkv_v2': _jnp.float32, 'conv_w': _jnp.float32, 'conv_w_o': _jnp.float32, 'w_out': _jnp.float32, 'mlp_norm': _jnp.float32, 'w_up': _jnp.float32, 'w_down': _jnp.float32}
MOMENT_SCALE = {'attn_norm': 1.440126e+01, 'w_in': 1.022836e+00, 'mla_q_a_norm': 1.257688e-01, 'mla_wq_b': 9.545057e-02, 'mla_kv_a_norm': 5.248304e+00, 'mla_wkv_b': 2.023366e+00, 'mla_q_norm': 4.146874e-01, 'mla_k_norm': 4.142001e-01, 'mla_w_o': 1.915658e+00, 'rwkv_mu': 1.747050e+00, 'rwkv_w0': 3.842915e-01, 'rwkv_w2': 7.701179e-02, 'rwkv_a0': 2.730769e-01, 'rwkv_a2': 2.120788e-01, 'rwkv_g2': 5.105265e+00, 'rwkv_k_k': 3.124523e+00, 'rwkv_k_a': 3.464707e+00, 'rwkv_r_k': 2.747943e+00, 'rwkv_ln_w': 8.112393e+00, 'rwkv_ln_b': 1.068628e+01, 'rwkv_w_o': 5.311784e-01, 'rwkv_v1': 4.058649e-01, 'rwkv_v_mu': 3.118157e-01, 'rwkv_v0': 9.494678e-01, 'rwkv_v2': 2.820132e-01, 'conv_w': 1.050866e+01, 'conv_w_o': 6.941539e-01, 'w_out': 1.903739e+00, 'mlp_norm': 4.953803e+01, 'w_up': 1.931551e+00, 'w_down': 8.225006e+00}


def _to_microbatches(a, axis):
    t = _jnp.moveaxis(a, axis, 0)
    t = t.reshape((N_MICROBATCH, t.shape[0] // N_MICROBATCH) + t.shape[1:])
    return _jnp.moveaxis(t, 1, axis + 1)


def setup_inputs(seed: int = 0) -> dict:
    inp = _fwd_setup_inputs(seed)
    key = _jax.random.fold_in(_jax.random.key(seed), 7919)
    shape, _ = _output_shape()
    out = dict(inp)
    out["loss_target"] = _jax.random.normal(_jax.random.fold_in(key, 0), shape, _jnp.float32)
    for i, name in enumerate(TWIN_WEIGHTS):
        w = inp[name].astype(_jnp.float32)
        if MOMENT_SCALE is None:
            s = _jnp.sqrt(_jnp.mean(_jnp.square(w)) + 1e-30)
        else:
            s = MOMENT_SCALE[name]
        km, kv = _jax.random.split(_jax.random.fold_in(key, i + 1))
        out[name] = w
        out["m_" + name] = s * _jax.random.normal(km, w.shape, _jnp.float32)
        out["v_" + name] = (s * s) * _jax.random.uniform(kv, w.shape, _jnp.float32, 0.5, 1.5)
    if N_MICROBATCH > 1:
        for name, axis in PER_EXAMPLE_BATCH_AXIS.items():
            out[name] = _to_microbatches(out[name], axis)
    return {'x': out['x'], 'positions': out['positions'], 'attn_norm': out['attn_norm'], 'w_in': out['w_in'], 'mla_q_a_norm': out['mla_q_a_norm'], 'mla_wq_b': out['mla_wq_b'], 'mla_kv_a_norm': out['mla_kv_a_norm'], 'mla_wkv_b': out['mla_wkv_b'], 'mla_q_norm': out['mla_q_norm'], 'mla_k_norm': out['mla_k_norm'], 'mla_w_o': out['mla_w_o'], 'rwkv_mu': out['rwkv_mu'], 'rwkv_w0': out['rwkv_w0'], 'rwkv_w2': out['rwkv_w2'], 'rwkv_a0': out['rwkv_a0'], 'rwkv_a2': out['rwkv_a2'], 'rwkv_g2': out['rwkv_g2'], 'rwkv_k_k': out['rwkv_k_k'], 'rwkv_k_a': out['rwkv_k_a'], 'rwkv_r_k': out['rwkv_r_k'], 'rwkv_ln_w': out['rwkv_ln_w'], 'rwkv_ln_b': out['rwkv_ln_b'], 'rwkv_w_o': out['rwkv_w_o'], 'rwkv_v1': out['rwkv_v1'], 'rwkv_v_mu': out['rwkv_v_mu'], 'rwkv_v0': out['rwkv_v0'], 'rwkv_v2': out['rwkv_v2'], 'conv_w': out['conv_w'], 'conv_w_o': out['conv_w_o'], 'w_out': out['w_out'], 'mlp_norm': out['mlp_norm'], 'w_up': out['w_up'], 'w_down': out['w_down'], 'loss_target': out['loss_target'], 'm_attn_norm': out['m_attn_norm'], 'm_w_in': out['m_w_in'], 'm_mla_q_a_norm': out['m_mla_q_a_norm'], 'm_mla_wq_b': out['m_mla_wq_b'], 'm_mla_kv_a_norm': out['m_mla_kv_a_norm'], 'm_mla_wkv_b': out['m_mla_wkv_b'], 'm_mla_q_norm': out['m_mla_q_norm'], 'm_mla_k_norm': out['m_mla_k_norm'], 'm_mla_w_o': out['m_mla_w_o'], 'm_rwkv_mu': out['m_rwkv_mu'], 'm_rwkv_w0': out['m_rwkv_w0'], 'm_rwkv_w2': out['m_rwkv_w2'], 'm_rwkv_a0': out['m_rwkv_a0'], 'm_rwkv_a2': out['m_rwkv_a2'], 'm_rwkv_g2': out['m_rwkv_g2'], 'm_rwkv_k_k': out['m_rwkv_k_k'], 'm_rwkv_k_a': out['m_rwkv_k_a'], 'm_rwkv_r_k': out['m_rwkv_r_k'], 'm_rwkv_ln_w': out['m_rwkv_ln_w'], 'm_rwkv_ln_b': out['m_rwkv_ln_b'], 'm_rwkv_w_o': out['m_rwkv_w_o'], 'm_rwkv_v1': out['m_rwkv_v1'], 'm_rwkv_v_mu': out['m_rwkv_v_mu'], 'm_rwkv_v0': out['m_rwkv_v0'], 'm_rwkv_v2': out['m_rwkv_v2'], 'm_conv_w': out['m_conv_w'], 'm_conv_w_o': out['m_conv_w_o'], 'm_w_out': out['m_w_out'], 'm_mlp_norm': out['m_mlp_norm'], 'm_w_up': out['m_w_up'], 'm_w_down': out['m_w_down'], 'v_attn_norm': out['v_attn_norm'], 'v_w_in': out['v_w_in'], 'v_mla_q_a_norm': out['v_mla_q_a_norm'], 'v_mla_wq_b': out['v_mla_wq_b'], 'v_mla_kv_a_norm': out['v_mla_kv_a_norm'], 'v_mla_wkv_b': out['v_mla_wkv_b'], 'v_mla_q_norm': out['v_mla_q_norm'], 'v_mla_k_norm': out['v_mla_k_norm'], 'v_mla_w_o': out['v_mla_w_o'], 'v_rwkv_mu': out['v_rwkv_mu'], 'v_rwkv_w0': out['v_rwkv_w0'], 'v_rwkv_w2': out['v_rwkv_w2'], 'v_rwkv_a0': out['v_rwkv_a0'], 'v_rwkv_a2': out['v_rwkv_a2'], 'v_rwkv_g2': out['v_rwkv_g2'], 'v_rwkv_k_k': out['v_rwkv_k_k'], 'v_rwkv_k_a': out['v_rwkv_k_a'], 'v_rwkv_r_k': out['v_rwkv_r_k'], 'v_rwkv_ln_w': out['v_rwkv_ln_w'], 'v_rwkv_ln_b': out['v_rwkv_ln_b'], 'v_rwkv_w_o': out['v_rwkv_w_o'], 'v_rwkv_v1': out['v_rwkv_v1'], 'v_rwkv_v_mu': out['v_rwkv_v_mu'], 'v_rwkv_v0': out['v_rwkv_v0'], 'v_rwkv_v2': out['v_rwkv_v2'], 'v_conv_w': out['v_conv_w'], 'v_conv_w_o': out['v_conv_w_o'], 'v_w_out': out['v_w_out'], 'v_mlp_norm': out['v_mlp_norm'], 'v_w_up': out['v_w_up'], 'v_w_down': out['v_w_down']}


def _loss(weights, diff, rest, loss_target):
    with _jax.named_scope("forward"):
        args = {**rest, TWIN_DIFF_INPUT: diff, **{k: w.astype(_WEIGHT_DTYPES[k]) for k, w in weights.items()}}
        y = _forward(args)
    with _jax.named_scope("loss_head"):
        err = _jnp.square(y.astype(_jnp.float32) - loss_target)
        return 0.5 * _jnp.sum(_jnp.mean(err, axis=-1)) if err.ndim else 0.5 * err


def _adamw(w, g, m, v):
    m = ADAM_B1 * m + (1.0 - ADAM_B1) * g
    v = ADAM_B2 * v + (1.0 - ADAM_B2) * _jnp.square(g)
    m_hat = m / (1.0 - ADAM_B1 ** ADAM_STEP)
    v_hat = v / (1.0 - ADAM_B2 ** ADAM_STEP)
    delta = -ADAM_LR * (m_hat / (_jnp.sqrt(v_hat) + ADAM_EPS) + ADAM_WD * w)
    return delta, m, v


def reference(x, positions, attn_norm, w_in, mla_q_a_norm, mla_wq_b, mla_kv_a_norm, mla_wkv_b, mla_q_norm, mla_k_norm, mla_w_o, rwkv_mu, rwkv_w0, rwkv_w2, rwkv_a0, rwkv_a2, rwkv_g2, rwkv_k_k, rwkv_k_a, rwkv_r_k, rwkv_ln_w, rwkv_ln_b, rwkv_w_o, rwkv_v1, rwkv_v_mu, rwkv_v0, rwkv_v2, conv_w, conv_w_o, w_out, mlp_norm, w_up, w_down, loss_target, m_attn_norm, m_w_in, m_mla_q_a_norm, m_mla_wq_b, m_mla_kv_a_norm, m_mla_wkv_b, m_mla_q_norm, m_mla_k_norm, m_mla_w_o, m_rwkv_mu, m_rwkv_w0, m_rwkv_w2, m_rwkv_a0, m_rwkv_a2, m_rwkv_g2, m_rwkv_k_k, m_rwkv_k_a, m_rwkv_r_k, m_rwkv_ln_w, m_rwkv_ln_b, m_rwkv_w_o, m_rwkv_v1, m_rwkv_v_mu, m_rwkv_v0, m_rwkv_v2, m_conv_w, m_conv_w_o, m_w_out, m_mlp_norm, m_w_up, m_w_down, v_attn_norm, v_w_in, v_mla_q_a_norm, v_mla_wq_b, v_mla_kv_a_norm, v_mla_wkv_b, v_mla_q_norm, v_mla_k_norm, v_mla_w_o, v_rwkv_mu, v_rwkv_w0, v_rwkv_w2, v_rwkv_a0, v_rwkv_a2, v_rwkv_g2, v_rwkv_k_k, v_rwkv_k_a, v_rwkv_r_k, v_rwkv_ln_w, v_rwkv_ln_b, v_rwkv_w_o, v_rwkv_v1, v_rwkv_v_mu, v_rwkv_v0, v_rwkv_v2, v_conv_w, v_conv_w_o, v_w_out, v_mlp_norm, v_w_up, v_w_down):
    given = dict(x=x, positions=positions, attn_norm=attn_norm, w_in=w_in, mla_q_a_norm=mla_q_a_norm, mla_wq_b=mla_wq_b, mla_kv_a_norm=mla_kv_a_norm, mla_wkv_b=mla_wkv_b, mla_q_norm=mla_q_norm, mla_k_norm=mla_k_norm, mla_w_o=mla_w_o, rwkv_mu=rwkv_mu, rwkv_w0=rwkv_w0, rwkv_w2=rwkv_w2, rwkv_a0=rwkv_a0, rwkv_a2=rwkv_a2, rwkv_g2=rwkv_g2, rwkv_k_k=rwkv_k_k, rwkv_k_a=rwkv_k_a, rwkv_r_k=rwkv_r_k, rwkv_ln_w=rwkv_ln_w, rwkv_ln_b=rwkv_ln_b, rwkv_w_o=rwkv_w_o, rwkv_v1=rwkv_v1, rwkv_v_mu=rwkv_v_mu, rwkv_v0=rwkv_v0, rwkv_v2=rwkv_v2, conv_w=conv_w, conv_w_o=conv_w_o, w_out=w_out, mlp_norm=mlp_norm, w_up=w_up, w_down=w_down, loss_target=loss_target, m_attn_norm=m_attn_norm, m_w_in=m_w_in, m_mla_q_a_norm=m_mla_q_a_norm, m_mla_wq_b=m_mla_wq_b, m_mla_kv_a_norm=m_mla_kv_a_norm, m_mla_wkv_b=m_mla_wkv_b, m_mla_q_norm=m_mla_q_norm, m_mla_k_norm=m_mla_k_norm, m_mla_w_o=m_mla_w_o, m_rwkv_mu=m_rwkv_mu, m_rwkv_w0=m_rwkv_w0, m_rwkv_w2=m_rwkv_w2, m_rwkv_a0=m_rwkv_a0, m_rwkv_a2=m_rwkv_a2, m_rwkv_g2=m_rwkv_g2, m_rwkv_k_k=m_rwkv_k_k, m_rwkv_k_a=m_rwkv_k_a, m_rwkv_r_k=m_rwkv_r_k, m_rwkv_ln_w=m_rwkv_ln_w, m_rwkv_ln_b=m_rwkv_ln_b, m_rwkv_w_o=m_rwkv_w_o, m_rwkv_v1=m_rwkv_v1, m_rwkv_v_mu=m_rwkv_v_mu, m_rwkv_v0=m_rwkv_v0, m_rwkv_v2=m_rwkv_v2, m_conv_w=m_conv_w, m_conv_w_o=m_conv_w_o, m_w_out=m_w_out, m_mlp_norm=m_mlp_norm, m_w_up=m_w_up, m_w_down=m_w_down, v_attn_norm=v_attn_norm, v_w_in=v_w_in, v_mla_q_a_norm=v_mla_q_a_norm, v_mla_wq_b=v_mla_wq_b, v_mla_kv_a_norm=v_mla_kv_a_norm, v_mla_wkv_b=v_mla_wkv_b, v_mla_q_norm=v_mla_q_norm, v_mla_k_norm=v_mla_k_norm, v_mla_w_o=v_mla_w_o, v_rwkv_mu=v_rwkv_mu, v_rwkv_w0=v_rwkv_w0, v_rwkv_w2=v_rwkv_w2, v_rwkv_a0=v_rwkv_a0, v_rwkv_a2=v_rwkv_a2, v_rwkv_g2=v_rwkv_g2, v_rwkv_k_k=v_rwkv_k_k, v_rwkv_k_a=v_rwkv_k_a, v_rwkv_r_k=v_rwkv_r_k, v_rwkv_ln_w=v_rwkv_ln_w, v_rwkv_ln_b=v_rwkv_ln_b, v_rwkv_w_o=v_rwkv_w_o, v_rwkv_v1=v_rwkv_v1, v_rwkv_v_mu=v_rwkv_v_mu, v_rwkv_v0=v_rwkv_v0, v_rwkv_v2=v_rwkv_v2, v_conv_w=v_conv_w, v_conv_w_o=v_conv_w_o, v_w_out=v_w_out, v_mlp_norm=v_mlp_norm, v_w_up=v_w_up, v_w_down=v_w_down)
    weights = {n: given[n] for n in TWIN_WEIGHTS}
    shared = {n: given[n] for n in SHARED_INPUTS}
    per_example = {n: given[n] for n in ['x', 'positions']}
    grad_fn = _jax.value_and_grad(_loss, argnums=(0, 1))

    def one_microbatch(ex, loss_target):
        ex = dict(ex)
        diff = ex.pop(TWIN_DIFF_INPUT)
        return grad_fn(weights, diff, {**shared, **ex}, loss_target)

    if N_MICROBATCH == 1:
        loss, (grad_w, grad_x) = one_microbatch(per_example, given["loss_target"])
    else:
        def body(carry, xs):
            loss_sum, grad_sum = carry
            l_k, (gw_k, gx_k) = one_microbatch(xs[0], xs[1])
            with _jax.named_scope("update"):
                return (loss_sum + l_k, _jax.tree.map(_jnp.add, grad_sum, gw_k)), gx_k

        init = (_jnp.zeros((), _jnp.float32), _jax.tree.map(_jnp.zeros_like, weights))
        (loss, grad_w), grad_x = _jax.lax.scan(body, init, (per_example, given["loss_target"]))
    with _jax.named_scope("update"):
        delta_w, new_m, new_v = {}, {}, {}
        for n in TWIN_WEIGHTS:
            delta_w[n], new_m[n], new_v[n] = _adamw(weights[n], grad_w[n], given["m_" + n], given["v_" + n])
    return (loss, grad_x, *[grad_w[n] for n in TWIN_WEIGHTS], *[delta_w[n] for n in TWIN_WEIGHTS],
            *[new_m[n] for n in TWIN_WEIGHTS], *[new_v[n] for n in TWIN_WEIGHTS])
```

```python
import functools

import numpy as np
import jax
import jax.numpy as jnp
from jax import lax
from jax.experimental import pallas as pl
from jax.experimental.pallas import tpu as pltpu

F32 = jnp.float32
BF16 = jnp.bfloat16

N_DEV = 8
LANES = 128
D_MODEL = 1024
DEPTH = 2
MLA_HEADS = 8
QK_NOPE = 64
QK_ROPE = 32
QK_HEAD = QK_NOPE + QK_ROPE
V_HEAD = 64
Q_LORA = 384
KV_LORA = 256
ROPE_THETA = 10000.0
RW_HEADS = 4
RW_N = 64
RW_WIDTH = RW_HEADS * RW_N
MV_LORA = 32
GN_EPS = 64e-5
CONV_WIDTH = 256
D_FF = 4 * D_MODEL
NORM_EPS = 1e-6
ADAM_LR = 0.001
ADAM_B1 = 0.9
ADAM_B2 = 0.999
ADAM_EPS = 1e-08
ADAM_WD = 0.01
ADAM_STEP = 10

VMEM_LIMIT = 56 * 1024 * 1024
MESH = pl.DeviceIdType.MESH

WEIGHT_NAMES = ['attn_norm', 'w_in', 'mla_q_a_norm', 'mla_wq_b', 'mla_kv_a_norm', 'mla_wkv_b', 'mla_q_norm',
                'mla_k_norm', 'mla_w_o', 'rwkv_mu', 'rwkv_w0', 'rwkv_w2', 'rwkv_a0', 'rwkv_a2', 'rwkv_g2',
                'rwkv_k_k', 'rwkv_k_a', 'rwkv_r_k', 'rwkv_ln_w', 'rwkv_ln_b', 'rwkv_w_o', 'rwkv_v1',
                'rwkv_v_mu', 'rwkv_v0', 'rwkv_v2', 'conv_w', 'conv_w_o', 'w_out', 'mlp_norm', 'w_up', 'w_down']

SHARDED = {
    'w_in': (2, (1024, 5536), 1), 'mla_wq_b': (2, (384, 768), 1), 'mla_wkv_b': (2, (256, 1024), 1),
    'mla_w_o': (2, (512, 1024), 1), 'rwkv_w2': (2, (64, 256), 1), 'rwkv_a2': (2, (64, 256), 1),
    'rwkv_g2': (2, (128, 256), 1), 'rwkv_w_o': (2, (256, 1024), 1), 'conv_w': (2, (3, 256), 1),
    'conv_w_o': (2, (256, 1024), 1), 'w_out': (2, (1024, 1024), 0), 'w_up': (2, (1024, 4096), 1),
    'w_down': (2, (4096, 1024), 0), 'rwkv_v1': (1, (1024, 32), 0), 'rwkv_v2': (1, (32, 256), 1),
}
SMALL_NAMES = [n for n in WEIGHT_NAMES if n not in SHARDED]
PACK_ROW_MULT = 512


def _cparams(sem=None, **kw):
    if sem is not None:
        kw['dimension_semantics'] = sem
    return pltpu.CompilerParams(vmem_limit_bytes=VMEM_LIMIT, **kw)


def _pick(n, cands):
    for c in cands:
        if n % c == 0:
            return c
    raise ValueError(f'no tile for {n}')


def _mm_nn(a, b, add=None, name='mm_nn'):
    M, K = a.shape
    N = b.shape[1]
    tm = _pick(M, (512, 256, 128))
    tn = _pick(N, (512, 384, 256, 128))
    tk = _pick(K, (1024, 512, 384, 256, 128))
    nk = K // tk
    has_add = add is not None

    def body(*refs):
        if has_add:
            a_ref, b_ref, add_ref, o_ref, acc_ref = refs
        else:
            a_ref, b_ref, o_ref, acc_ref = refs
        kk = pl.program_id(2)
        part = jnp.dot(a_ref[...].astype(BF16), b_ref[...].astype(BF16), preferred_element_type=F32)

        @pl.when(kk == 0)
        def _():
            acc_ref[...] = part

        @pl.when(kk > 0)
        def _():
            acc_ref[...] += part

        @pl.when(kk == nk - 1)
        def _():
            if has_add:
                o_ref[...] = acc_ref[...] + add_ref[...]
            else:
                o_ref[...] = acc_ref[...]

    in_specs = [pl.BlockSpec((tm, tk), lambda i, j, k: (i, k)), pl.BlockSpec((tk, tn), lambda i, j, k: (k, j))]
    args = [a, b]
    if has_add:
        in_specs.append(pl.BlockSpec((tm, tn), lambda i, j, k: (i, j)))
        args.append(add)
    return pl.pallas_call(
        body, name=name, grid=(M // tm, N // tn, nk), in_specs=in_specs,
        out_specs=pl.BlockSpec((tm, tn), lambda i, j, k: (i, j)),
        out_shape=jax.ShapeDtypeStruct((M, N), F32),
        scratch_shapes=[pltpu.VMEM((tm, tn), F32)],
        compiler_params=_cparams(('parallel', 'parallel', 'arbitrary')),
    )(*args)


def _mm_nt(a, b, add=None, name='mm_nt'):
    M, N = a.shape
    K = b.shape[0]
    tm = _pick(M, (512, 256, 128))
    tk = _pick(K, (512, 384, 256, 128))
    tn = _pick(N, (1024, 512, 384, 256, 128))
    nn = N // tn
    has_add = add is not None

    def body(*refs):
        if has_add:
            a_ref, b_ref, add_ref, o_ref, acc_ref = refs
        else:
            a_ref, b_ref, o_ref, acc_ref = refs
        kk = pl.program_id(2)
        part = lax.dot_general(a_ref[...].astype(BF16), b_ref[...].astype(BF16), (((1,), (1,)), ((), ())),
                               preferred_element_type=F32)

        @pl.when(kk == 0)
        def _():
            acc_ref[...] = part

        @pl.when(kk > 0)
        def _():
            acc_ref[...] += part

        @pl.when(kk == nn - 1)
        def _():
            if has_add:
                o_ref[...] = acc_ref[...] + add_ref[...]
            else:
                o_ref[...] = acc_ref[...]

    in_specs = [pl.BlockSpec((tm, tn), lambda i, j, k: (i, k)), pl.BlockSpec((tk, tn), lambda i, j, k: (j, k))]
    args = [a, b]
    if has_add:
        in_specs.append(pl.BlockSpec((tm, tk), lambda i, j, k: (i, j)))
        args.append(add)
    return pl.pallas_call(
        body, name=name, grid=(M // tm, K // tk, nn), in_specs=in_specs,
        out_specs=pl.BlockSpec((tm, tk), lambda i, j, k: (i, j)),
        out_shape=jax.ShapeDtypeStruct((M, K), F32),
        scratch_shapes=[pltpu.VMEM((tm, tk), F32)],
        compiler_params=_cparams(('parallel', 'parallel', 'arbitrary')),
    )(*args)


def _mm_tn(a, b, name='mm_tn'):
    M, K = a.shape
    N = b.shape[1]
    tm = _pick(M, (512, 256, 128))
    tk = _pick(K, (512, 384, 256, 128))
    tn = _pick(N, (512, 384, 256, 128))
    nm = M // tm

    def body(a_ref, b_ref, o_ref, acc_ref):
        mm = pl.program_id(2)
        part = lax.dot_general(a_ref[...].astype(BF16), b_ref[...].astype(BF16), (((0,), (0,)), ((), ())),
                               preferred_element_type=F32)

        @pl.when(mm == 0)
        def _():
            acc_ref[...] = part

        @pl.when(mm > 0)
        def _():
            acc_ref[...] += part

        @pl.when(mm == nm - 1)
        def _():
            o_ref[...] = acc_ref[...]

    return pl.pallas_call(
        body, name=name, grid=(K // tk, N // tn, nm),
        in_specs=[pl.BlockSpec((tm, tk), lambda i, j, m: (m, i)), pl.BlockSpec((tm, tn), lambda i, j, m: (m, j))],
        out_specs=pl.BlockSpec((tk, tn), lambda i, j, m: (i, j)),
        out_shape=jax.ShapeDtypeStruct((K, N), F32),
        scratch_shapes=[pltpu.VMEM((tk, tn), F32)],
        compiler_params=_cparams(('parallel', 'parallel', 'arbitrary')),
    )(a, b)


@functools.partial(jax.custom_vjp, nondiff_argnums=(4,))
def _linear_add(a, wb, wc, add, name):
    return _mm_nn(a, wb, add, name=name + '_f')


def _linear_add_fwd(a, wb, wc, add, name):
    return _mm_nn(a, wb, add, name=name + '_f'), (a, wb)


def _linear_add_bwd(name, res, dy):
    a, wb = res
    return _mm_nt(dy, wb, name=name + '_da'), None, _mm_tn(a, dy, name=name + '_dw'), dy


_linear_add.defvjp(_linear_add_fwd, _linear_add_bwd)


@functools.partial(jax.custom_vjp, nondiff_argnums=(3,))
def _multi_linear(a, wbs, wcs, name):
    return tuple(_mm_nn(a, wb, name=f'{name}_f{i}') for i, wb in enumerate(wbs))


def _multi_linear_fwd(a, wbs, wcs, name):
    return _multi_linear(a, wbs, wcs, name), (a, wbs)


def _multi_linear_bwd(name, res, dys):
    a, wbs = res
    da = None
    for i, (dy, wb) in enumerate(zip(dys, wbs)):
        da = _mm_nt(dy, wb, add=da, name=f'{name}_da{i}')
    dws = tuple(_mm_tn(a, dy, name=f'{name}_dw{i}') for i, dy in enumerate(dys))
    return da, None, dws


_multi_linear.defvjp(_multi_linear_fwd, _multi_linear_bwd)


class W:
    def __init__(self, b, c):
        self.b, self.c = b, c

    def map(self, fn):
        return W(fn(self.b), fn(self.c))


def _wcat(ws, axis):
    return W(jnp.concatenate([w.b for w in ws], axis), jnp.concatenate([w.c for w in ws], axis))


def linear(a, w, add=None, name='lin'):
    if add is None:
        return _multi_linear(a, (w.b,), (w.c,), name)[0]
    return _linear_add(a, w.b, w.c, add, name)


def multi_linear(a, ws, name):
    return _multi_linear(a, tuple(w.b for w in ws), tuple(w.c for w in ws), name)


def ROW(diff=True, pieces=None):
    return ('row', diff, pieces)


def FULL(diff=True):
    return ('full', diff, None)


def _load_args(refs, specs):
    args, amap = [], []
    for i, (ref, (kind, diff, pieces)) in enumerate(zip(refs, specs)):
        if pieces is None:
            args.append(ref[...])
            amap.append((i, None))
        else:
            for (s, w) in pieces:
                args.append(ref[:, s:s + w])
                amap.append((i, (s, w)))
    return args, amap


def _stage_in_specs(ins, specs, tb):
    out = []
    for a, (kind, _, _) in zip(ins, specs):
        if kind == 'row':
            out.append(pl.BlockSpec((tb, a.shape[1]), lambda i: (i, 0)))
        else:
            out.append(pl.BlockSpec(a.shape, lambda i: (0, 0)))
    return out


def _stage_fwd(fn, ins, specs, out_widths, name, tb):
    T = [a for a, s in zip(ins, specs) if s[0] == 'row'][0].shape[0]
    tb = min(tb, T)
    n_in = len(ins)

    def body(*refs):
        args, _ = _load_args(refs[:n_in], specs)
        outs = fn(*args)
        for o_ref, o in zip(refs[n_in:], outs):
            o_ref[...] = o

    return pl.pallas_call(
        body, name=name + '_f', grid=(T // tb,), in_specs=_stage_in_specs(ins, specs, tb),
        out_specs=[pl.BlockSpec((tb, w), lambda i: (i, 0)) for w in out_widths],
        out_shape=[jax.ShapeDtypeStruct((T, w), F32) for w in out_widths],
        compiler_params=_cparams(('parallel',)),
    )(*ins)


def _stage_bwd(fn, ins, specs, out_widths, douts, name, tb):
    T = [a for a, s in zip(ins, specs) if s[0] == 'row'][0].shape[0]
    tb = min(tb, T)
    n_in, n_out = len(ins), len(out_widths)
    diff_inputs = [i for i, s in enumerate(specs) if s[1]]

    def body(*refs):
        in_refs, dout_refs, g_refs = refs[:n_in], refs[n_in:n_in + n_out], refs[n_in + n_out:]
        args, amap = _load_args(in_refs, specs)
        didx = [j for j, (i, _) in enumerate(amap) if specs[i][1]]

        def f(*dv):
            full = list(args)
            for j, v in zip(didx, dv):
                full[j] = v
            return tuple(fn(*full))

        _, vjp = jax.vjp(f, *[args[j] for j in didx])
        gs = vjp(tuple(d[...] for d in dout_refs))
        gmap = {j: g for j, g in zip(didx, gs)}
        first = pl.program_id(0) == 0
        for g_ref, i in zip(g_refs, diff_inputs):
            kind, _, pieces = specs[i]
            js = [j for j, (ii, _) in enumerate(amap) if ii == i]
            if kind == 'row':
                if pieces is None:
                    g_ref[...] = gmap[js[0]]
                else:
                    if sum(w for _, w in pieces) != ins[i].shape[1]:
                        g_ref[...] = jnp.zeros(g_ref.shape, F32)
                    for j in js:
                        s, w = amap[j][1]
                        g_ref[:, s:s + w] = gmap[j]
            else:
                @pl.when(first)
                def _(g_ref=g_ref):
                    g_ref[...] = jnp.zeros(g_ref.shape, F32)

                g_ref[...] += gmap[js[0]]

    in_specs = _stage_in_specs(ins, specs, tb) + [pl.BlockSpec((tb, w), lambda i: (i, 0)) for w in out_widths]
    out_specs, out_shape = [], []
    for i in diff_inputs:
        a = ins[i]
        if specs[i][0] == 'row':
            out_specs.append(pl.BlockSpec((tb, a.shape[1]), lambda i: (i, 0)))
        else:
            out_specs.append(pl.BlockSpec(a.shape, lambda i: (0, 0)))
        out_shape.append(jax.ShapeDtypeStruct(a.shape, F32))
    return pl.pallas_call(
        body, name=name + '_b', grid=(T // tb,), in_specs=in_specs, out_specs=out_specs, out_shape=out_shape,
        compiler_params=_cparams(('arbitrary',)),
    )(*ins, *douts)


def stage_op(fn, specs, out_widths, name, tb=256):
    n = len(specs)
    diff_inputs = [i for i, s in enumerate(specs) if s[1]]

    @jax.custom_vjp
    def op(*ins):
        return tuple(_stage_fwd(fn, ins, specs, out_widths, name, tb))

    def op_fwd(*ins):
        return op(*ins), ins

    def op_bwd(ins, douts):
        gs = _stage_bwd(fn, ins, specs, out_widths, douts, name, tb)
        res = [None] * n
        for i, g in zip(diff_inputs, gs):
            res[i] = g
        return tuple(res)

    op.defvjp(op_fwd, op_bwd)
    return op


@jax.custom_vjp
def bdot(x, w):
    return jnp.dot(x.astype(BF16), w.astype(BF16), preferred_element_type=F32)


def _bdot_fwd(x, w):
    return bdot(x, w), (x, w)


def _bdot_bwd(res, dy):
    x, w = res
    dyb = dy.astype(BF16)
    dx = lax.dot_general(dyb, w.astype(BF16), (((1,), (1,)), ((), ())), preferred_element_type=F32)
    dw = lax.dot_general(x.astype(BF16), dyb, (((0,), (0,)), ((), ())), preferred_element_type=F32)
    return dx, dw


bdot.defvjp(_bdot_fwd, _bdot_bwd)


def _sdot_raw(x, c):
    hi = x.astype(BF16)
    r1 = x - hi.astype(F32)
    mid = r1.astype(BF16)
    lo = (r1 - mid.astype(F32)).astype(BF16)
    d = lambda u: jnp.dot(u, c, preferred_element_type=F32)
    return d(hi) + d(mid) + d(lo)


@jax.custom_vjp
def sdot(x, c, ct):
    return _sdot_raw(x, c)


def _sdot_fwd(x, c, ct):
    return _sdot_raw(x, c), (c, ct)


def _sdot_bwd(res, dy):
    c, ct = res
    return _sdot_raw(dy, ct), None, None


sdot.defvjp(_sdot_fwd, _sdot_bwd)


def _sigmoid(x):
    return 1.0 / (1.0 + jnp.exp(-x))


def _rms(x, g):
    return x * lax.rsqrt(jnp.mean(x * x, axis=-1, keepdims=True) + NORM_EPS) * g


def rmsnorm(x, g, name):
    op = stage_op(lambda xv, gv: (_rms(xv, gv),), [ROW(), FULL()], [x.shape[1]], name)
    return op(x, g.reshape(1, -1))[0]


def _shift_down(x, rows):
    return jnp.where(rows == 0, 0.0, pltpu.roll(x, 1, 0))


def _shift_up(x, rows, T):
    return jnp.where(rows == T - 1, 0.0, pltpu.roll(x, T - 1, 0))


def _tshift_fwd_call(x, mu, name):
    T, C = x.shape

    def body(x_ref, mu_ref, o_ref):
        xv = x_ref[...]
        rows = lax.broadcasted_iota(jnp.int32, xv.shape, 0)
        o_ref[...] = xv + (_shift_down(xv, rows) - xv) * mu_ref[...]

    return pl.pallas_call(
        body, name=name + '_f', grid=(C // LANES,),
        in_specs=[pl.BlockSpec((T, LANES), lambda j: (0, j)), pl.BlockSpec((1, LANES), lambda j: (0, j))],
        out_specs=pl.BlockSpec((T, LANES), lambda j: (0, j)), out_shape=jax.ShapeDtypeStruct((T, C), F32),
        compiler_params=_cparams(('parallel',)),
    )(x, mu)


def _tshift_bwd_call(x, mu, dy, name):
    T, C = x.shape

    def body(x_ref, mu_ref, dy_ref, dx_ref, dmu_ref):
        xv, d = x_ref[...], dy_ref[...]
        rows = lax.broadcasted_iota(jnp.int32, xv.shape, 0)
        z = d * mu_ref[...]
        dx_ref[...] = d - z + _shift_up(z, rows, T)
        dmu_ref[...] = jnp.sum(d * (_shift_down(xv, rows) - xv), axis=0, keepdims=True)

    return pl.pallas_call(
        body, name=name + '_b', grid=(C // LANES,),
        in_specs=[pl.BlockSpec((T, LANES), lambda j: (0, j)), pl.BlockSpec((1, LANES), lambda j: (0, j)),
                  pl.BlockSpec((T, LANES), lambda j: (0, j))],
        out_specs=[pl.BlockSpec((T, LANES), lambda j: (0, j)), pl.BlockSpec((1, LANES), lambda j: (0, j))],
        out_shape=[jax.ShapeDtypeStruct((T, C), F32), jax.ShapeDtypeStruct((1, C), F32)],
        compiler_params=_cparams(('parallel',)),
    )(x, mu, dy)


@functools.partial(jax.custom_vjp, nondiff_argnums=(2,))
def token_shift_mix(x, mu, name):
    return _tshift_fwd_call(x, mu, name)


def _tsm_fwd(x, mu, name):
    return _tshift_fwd_call(x, mu, name), (x, mu)


def _tsm_bwd(name, res, dy):
    x, mu = res
    dx, dmu = _tshift_bwd_call(x, mu, dy, name)
    return dx, dmu


token_shift_mix.defvjp(_tsm_fwd, _tsm_bwd)


def _conv_specs(T):
    nb = CONV_WIDTH // LANES
    return [pl.BlockSpec((T, LANES), lambda j: (0, j)), pl.BlockSpec((T, LANES), lambda j: (0, nb + j)),
            pl.BlockSpec((T, LANES), lambda j: (0, 2 * nb + j)), pl.BlockSpec((3, LANES), lambda j: (0, j))]


def _conv_fwd_call(cv, w, name):
    T = cv.shape[0]

    def body(b_ref, c_ref, x_ref, w_ref, o_ref):
        u = c_ref[...] * x_ref[...]
        rows = lax.broadcasted_iota(jnp.int32, u.shape, 0)
        u1 = _shift_down(u, rows)
        u2 = _shift_down(u1, rows)
        o_ref[...] = b_ref[...] * (w_ref[0:1, :] * u2 + w_ref[1:2, :] * u1 + w_ref[2:3, :] * u)

    return pl.pallas_call(
        body, name=name + '_f', grid=(CONV_WIDTH // LANES,), in_specs=_conv_specs(T),
        out_specs=pl.BlockSpec((T, LANES), lambda j: (0, j)),
        out_shape=jax.ShapeDtypeStruct((T, CONV_WIDTH), F32), compiler_params=_cparams(('parallel',)),
    )(cv, cv, cv, w)


def _conv_bwd_call(cv, w, do, name):
    T = cv.shape[0]

    def body(b_ref, c_ref, x_ref, w_ref, do_ref, db_ref, dc_ref, dx_ref, dw_ref):
        c, x, d = c_ref[...], x_ref[...], do_ref[...]
        u = c * x
        rows = lax.broadcasted_iota(jnp.int32, u.shape, 0)
        u1 = _shift_down(u, rows)
        u2 = _shift_down(u1, rows)
        w0, w1, w2 = w_ref[0:1, :], w_ref[1:2, :], w_ref[2:3, :]
        db_ref[...] = d * (w0 * u2 + w1 * u1 + w2 * u)
        dy = d * b_ref[...]
        dy1 = _shift_up(dy, rows, T)
        dy2 = _shift_up(dy1, rows, T)
        du = w2 * dy + w1 * dy1 + w0 * dy2
        dc_ref[...] = du * x
        dx_ref[...] = du * c
        dw_ref[0:1, :] = jnp.sum(dy * u2, axis=0, keepdims=True)
        dw_ref[1:2, :] = jnp.sum(dy * u1, axis=0, keepdims=True)
        dw_ref[2:3, :] = jnp.sum(dy * u, axis=0, keepdims=True)

    blk = pl.BlockSpec((T, LANES), lambda j: (0, j))
    sh = jax.ShapeDtypeStruct((T, CONV_WIDTH), F32)
    return pl.pallas_call(
        body, name=name + '_b', grid=(CONV_WIDTH // LANES,), in_specs=_conv_specs(T) + [blk],
        out_specs=[blk, blk, blk, pl.BlockSpec((3, LANES), lambda j: (0, j))],
        out_shape=[sh, sh, sh, jax.ShapeDtypeStruct((3, CONV_WIDTH), F32)],
        compiler_params=_cparams(('parallel',)),
    )(cv, cv, cv, w, do)


@functools.partial(jax.custom_vjp, nondiff_argnums=(2,))
def short_conv(cv, w, name):
    return _conv_fwd_call(cv, w, name)


def _sc_fwd(cv, w, name):
    return _conv_fwd_call(cv, w, name), (cv, w)


def _sc_bwd(name, res, do):
    cv, w = res
    db, dc, dx, dw = _conv_bwd_call(cv, w, do, name)
    return jnp.concatenate([db, dc, dx], axis=1), dw


short_conv.defvjp(_sc_fwd, _sc_bwd)


ATT_SCALE = QK_HEAD ** -0.5
NPAIR = MLA_HEADS // 2


def _att_bq(T):
    return min(256, T)


def _att_masks(pair, j):
    lane = lax.broadcasted_iota(jnp.int32, (1, LANES), 1)
    mask_n = (lane // QK_NOPE) == j
    mask_r = (lane // (QK_ROPE // 2)) == (2 * pair + j)
    return mask_n, mask_r


def _att_probs(qcat, kcat, row0, stop):
    s = lax.dot_general(qcat, kcat, (((1,), (1,)), ((), ())), preferred_element_type=F32) * ATT_SCALE
    r = row0 + lax.broadcasted_iota(jnp.int32, s.shape, 0)
    c = lax.broadcasted_iota(jnp.int32, s.shape, 1)
    s = jnp.where(c <= r, s, -jnp.inf)
    e = jnp.exp(s - jnp.max(s, axis=-1, keepdims=True))
    return e / jnp.sum(e, axis=-1, keepdims=True)


def _att_in_specs(T):
    blk = lambda f: pl.BlockSpec((T, LANES), f)
    return [blk(lambda p: (0, p)), blk(lambda p: (0, 0)), blk(lambda p: (0, 0)),
            blk(lambda p: (0, p)), blk(lambda p: (0, 0)), blk(lambda p: (0, 0)), blk(lambda p: (0, p))]


def _att_fwd_call(qn, q1, q2, kn, k1, k2, v, name):
    T = qn.shape[0]
    bq = _att_bq(T)

    def body(qn_ref, q1_ref, q2_ref, kn_ref, k1_ref, k2_ref, v_ref, o_ref):
        pair = pl.program_id(0)
        for i in range(T // bq):
            r0, stop = i * bq, (i + 1) * bq
            kcat = jnp.concatenate([kn_ref[0:stop, :], k1_ref[0:stop, :], k2_ref[0:stop, :]], axis=1).astype(BF16)
            vb = v_ref[0:stop, :].astype(BF16)
            outs = []
            for j in range(2):
                mask_n, mask_r = _att_masks(pair, j)
                qcat = jnp.concatenate([jnp.where(mask_n, qn_ref[r0:stop, :], 0.0),
                                        jnp.where(mask_r, q1_ref[r0:stop, :], 0.0),
                                        jnp.where(mask_r, q2_ref[r0:stop, :], 0.0)], axis=1).astype(BF16)
                p = _att_probs(qcat, kcat, r0, stop)
                outs.append(jnp.dot(p.astype(BF16), vb, preferred_element_type=F32))
            mask_n0, _ = _att_masks(pair, 0)
            o_ref[r0:stop, :] = jnp.where(mask_n0, outs[0], outs[1])

    return pl.pallas_call(
        body, name=name + '_f', grid=(NPAIR,), in_specs=_att_in_specs(T),
        out_specs=pl.BlockSpec((T, LANES), lambda p: (0, p)),
        out_shape=jax.ShapeDtypeStruct((T, MLA_HEADS * V_HEAD), F32), compiler_params=_cparams(('parallel',)),
    )(qn, q1, q2, kn, k1, k2, v)


def _att_bwd_call(qn, q1, q2, kn, k1, k2, v, o, do, name):
    T = qn.shape[0]
    bq = _att_bq(T)

    def body(qn_ref, q1_ref, q2_ref, kn_ref, k1_ref, k2_ref, v_ref, o_ref, do_ref,
             dqn_ref, dq1_ref, dq2_ref, dkn_ref, dk1_ref, dk2_ref, dv_ref, dk_acc, dv_acc):
        pair = pl.program_id(0)

        @pl.when(pair == 0)
        def _():
            dq1_ref[...] = jnp.zeros(dq1_ref.shape, F32)
            dq2_ref[...] = jnp.zeros(dq2_ref.shape, F32)
            dk1_ref[...] = jnp.zeros(dk1_ref.shape, F32)
            dk2_ref[...] = jnp.zeros(dk2_ref.shape, F32)

        dk_acc[...] = jnp.zeros(dk_acc.shape, F32)
        dv_acc[...] = jnp.zeros(dv_acc.shape, F32)
        for i in range(T // bq):
            r0, stop = i * bq, (i + 1) * bq
            kcat = jnp.concatenate([kn_ref[0:stop, :], k1_ref[0:stop, :], k2_ref[0:stop, :]], axis=1).astype(BF16)
            vb = v_ref[0:stop, :].astype(BF16)
            dqn = jnp.zeros((bq, LANES), F32)
            for j in range(2):
                mask_n, mask_r = _att_masks(pair, j)
                qcat = jnp.concatenate([jnp.where(mask_n, qn_ref[r0:stop, :], 0.0),
                                        jnp.where(mask_r, q1_ref[r0:stop, :], 0.0),
                                        jnp.where(mask_r, q2_ref[r0:stop, :], 0.0)], axis=1).astype(BF16)
                p = _att_probs(qcat, kcat, r0, stop)
                dom = jnp.where(mask_n, do_ref[r0:stop, :], 0.0)
                delta = jnp.sum(dom * o_ref[r0:stop, :], axis=-1, keepdims=True)
                domb = dom.astype(BF16)
                dp = lax.dot_general(domb, vb, (((1,), (1,)), ((), ())), preferred_element_type=F32)
                ds = (p * (dp - delta) * ATT_SCALE).astype(BF16)
                dqc = jnp.dot(ds, kcat, preferred_element_type=F32)
                dqn = dqn + jnp.where(mask_n, dqc[:, 0:LANES], 0.0)
                dq1_ref[r0:stop, :] += jnp.where(mask_r, dqc[:, LANES:2 * LANES], 0.0)
                dq2_ref[r0:stop, :] += jnp.where(mask_r, dqc[:, 2 * LANES:3 * LANES], 0.0)
                dk_acc[0:stop, :] += lax.dot_general(ds, qcat, (((0,), (0,)), ((), ())),
                                                     preferred_element_type=F32)
                dv_acc[0:stop, :] += lax.dot_general(p.astype(BF16), domb, (((0,), (0,)), ((), ())),
                                                     preferred_element_type=F32)
            dqn_ref[r0:stop, :] = dqn
        dkn_ref[...] = dk_acc[:, 0:LANES]
        dk1_ref[...] += dk_acc[:, LANES:2 * LANES]
        dk2_ref[...] += dk_acc[:, 2 * LANES:3 * LANES]
        dv_ref[...] = dv_acc[...]

    per_pair = pl.BlockSpec((T, LANES), lambda p: (0, p))
    shared = pl.BlockSpec((T, LANES), lambda p: (0, 0))
    wide = jax.ShapeDtypeStruct((T, MLA_HEADS * QK_NOPE), F32)
    narrow = jax.ShapeDtypeStruct((T, LANES), F32)
    return pl.pallas_call(
        body, name=name + '_b', grid=(NPAIR,), in_specs=_att_in_specs(T) + [per_pair, per_pair],
        out_specs=[per_pair, shared, shared, per_pair, shared, shared, per_pair],
        out_shape=[wide, narrow, narrow, wide, narrow, narrow, wide],
        scratch_shapes=[pltpu.VMEM((T, 3 * LANES), F32), pltpu.VMEM((T, LANES), F32)],
        compiler_params=_cparams(('arbitrary',)),
    )(qn, q1, q2, kn, k1, k2, v, o, do)


@functools.partial(jax.custom_vjp, nondiff_argnums=(7,))
def attention(qn, q1, q2, kn, k1, k2, v, name):
    return _att_fwd_call(qn, q1, q2, kn, k1, k2, v, name)


def _attn_fwd(qn, q1, q2, kn, k1, k2, v, name):
    o = _att_fwd_call(qn, q1, q2, kn, k1, k2, v, name)
    return o, (qn, q1, q2, kn, k1, k2, v, o)


def _attn_bwd(name, res, do):
    return tuple(_att_bwd_call(*res, do, name))


attention.defvjp(_attn_fwd, _attn_bwd)


SCAN_CHUNK = 64


def _block_ones(n, seg):
    i = np.arange(n)
    return (i[:, None] // seg == i[None, :] // seg).astype(np.float32)


def _scan_consts():
    bm = jnp.asarray(_block_ones(RW_WIDTH, RW_N), BF16)
    i = np.arange(RW_WIDTH)
    dg = jnp.asarray((np.arange(RW_N)[:, None] == (i[None, :] % RW_N)).astype(np.float32))
    return bm, dg


def _scan_fwd_call(r, w, k, v, a, b, name):
    T = r.shape[0]
    tc = min(SCAN_CHUNK, T)
    bm, dg = _scan_consts()

    def body(r_ref, w_ref, k_ref, v_ref, a_ref, b_ref, bm_ref, dg_ref, y_ref, st_ref, s_ref):
        @pl.when(pl.program_id(0) == 0)
        def _():
            s_ref[...] = jnp.zeros(s_ref.shape, F32)

        bmv, dgv = bm_ref[...], dg_ref[...]
        seg = lambda x: _sdot_raw(x, bmv)

        def step(t, s):
            st_ref[t] = s
            sa = seg(s * a_ref[t])
            vcol = seg(dgv * v_ref[t])
            sn = s * w_ref[t] + sa * b_ref[t] + vcol * k_ref[t]
            y_ref[t] = jnp.sum(seg(sn * r_ref[t]) * dgv, axis=0, keepdims=True)
            return sn

        s_ref[...] = lax.fori_loop(0, tc, step, s_ref[...])

    vec = pl.BlockSpec((tc, 1, RW_WIDTH), lambda i: (i, 0, 0))
    return pl.pallas_call(
        body, name=name + '_f', grid=(T // tc,),
        in_specs=[vec] * 6 + [pl.BlockSpec((RW_WIDTH, RW_WIDTH), lambda i: (0, 0)),
                              pl.BlockSpec((RW_N, RW_WIDTH), lambda i: (0, 0))],
        out_specs=[vec, pl.BlockSpec((tc, RW_N, RW_WIDTH), lambda i: (i, 0, 0))],
        out_shape=[jax.ShapeDtypeStruct((T, 1, RW_WIDTH), F32), jax.ShapeDtypeStruct((T, RW_N, RW_WIDTH), F32)],
        scratch_shapes=[pltpu.VMEM((RW_N, RW_WIDTH), F32)],
        compiler_params=_cparams(('arbitrary',)),
    )(r, w, k, v, a, b, bm, dg)


def _scan_bwd_call(r, w, k, v, a, b, st, dy, name):
    T = r.shape[0]
    tc = min(SCAN_CHUNK, T)
    nt = T // tc
    bm, dg = _scan_consts()

    def body(r_ref, w_ref, k_ref, v_ref, a_ref, b_ref, st_ref, dy_ref, bm_ref, dg_ref,
             dr_ref, dw_ref, dk_ref, dv_ref, da_ref, db_ref, ds_ref):
        @pl.when(pl.program_id(0) == 0)
        def _():
            ds_ref[...] = jnp.zeros(ds_ref.shape, F32)

        bmv, dgv = bm_ref[...], dg_ref[...]
        seg = lambda x: _sdot_raw(x, bmv)
        colsum = lambda x: jnp.sum(x, axis=0, keepdims=True)

        def step(i, ds):
            t = tc - 1 - i
            sp = st_ref[t]
            rt, wt, kt, at, bt = r_ref[t], w_ref[t], k_ref[t], a_ref[t], b_ref[t]
            sa = seg(sp * at)
            vcol = seg(dgv * v_ref[t])
            dycol = seg(dgv * dy_ref[t])
            sn = sp * wt + sa * bt + vcol * kt
            ds = ds + dycol * rt
            dr_ref[t] = colsum(sn * dycol)
            dk_ref[t] = colsum(ds * vcol)
            db_ref[t] = colsum(ds * sa)
            dw_ref[t] = colsum(ds * sp)
            dv_ref[t] = colsum(seg(ds * kt) * dgv)
            dsa = seg(ds * bt)
            da_ref[t] = colsum(sp * dsa)
            return ds * wt + dsa * at

        ds_ref[...] = lax.fori_loop(0, tc, step, ds_ref[...])

    vec = pl.BlockSpec((tc, 1, RW_WIDTH), lambda i: (nt - 1 - i, 0, 0))
    vsh = jax.ShapeDtypeStruct((T, 1, RW_WIDTH), F32)
    return pl.pallas_call(
        body, name=name + '_b', grid=(nt,),
        in_specs=[vec] * 6 + [pl.BlockSpec((tc, RW_N, RW_WIDTH), lambda i: (nt - 1 - i, 0, 0)), vec,
                              pl.BlockSpec((RW_WIDTH, RW_WIDTH), lambda i: (0, 0)),
                              pl.BlockSpec((RW_N, RW_WIDTH), lambda i: (0, 0))],
        out_specs=[vec] * 6, out_shape=[vsh] * 6,
        scratch_shapes=[pltpu.VMEM((RW_N, RW_WIDTH), F32)],
        compiler_params=_cparams(('arbitrary',)),
    )(r, w, k, v, a, b, st, dy, bm, dg)


@functools.partial(jax.custom_vjp, nondiff_argnums=(6,))
def wkv7(r, w, k, v, a, b, name):
    return _scan_fwd_call(r, w, k, v, a, b, name)[0]


def _wkv7_fwd(r, w, k, v, a, b, name):
    y, st = _scan_fwd_call(r, w, k, v, a, b, name)
    return y, (r, w, k, v, a, b, st)


def _wkv7_bwd(name, res, dy):
    return tuple(_scan_bwd_call(*res, dy, name))


wkv7.defvjp(_wkv7_fwd, _wkv7_bwd)


def _np_bf16(a):
    return jnp.asarray(a, BF16)


def _mla_consts():
    seg_n = (np.arange(512)[:, None] // QK_NOPE == np.arange(LANES)[None, :]).astype(np.float32)
    seg_r = (np.arange(LANES)[:, None] // 16 == np.arange(LANES)[None, :]).astype(np.float32)
    e1 = np.zeros((LANES, LANES), np.float32)
    e2 = np.zeros((LANES, LANES), np.float32)
    for h in range(MLA_HEADS):
        for i in range(16):
            e1[i, h * 16 + i] = 1.0
            e2[16 + i, h * 16 + i] = 1.0
    mats = [seg_n, seg_n.T, seg_r, seg_r.T, e1, e1.T, e2, e2.T]
    return [_np_bf16(m) for m in mats]


def _qk_prep_fn(qn, q1, q2, kn, kx, cos, sin, gqn, gq1, gq2, gkn, gk1, gk2,
                seg_n, seg_nt, seg_r, seg_rt, e1, e1t, e2, e2t):
    def normrope(xn, x1, x2, gn, g1, g2):
        ss = sdot(xn * xn, seg_n, seg_nt) + sdot(x1 * x1, seg_r, seg_rt) + sdot(x2 * x2, seg_r, seg_rt)
        inv = lax.rsqrt(ss * (1.0 / QK_HEAD) + NORM_EPS)
        inv_n = sdot(inv, seg_nt, seg_n)
        inv_r = sdot(inv, seg_rt, seg_r)
        y1 = x1 * inv_r * g1
        y2 = x2 * inv_r * g2
        return xn * inv_n * gn, y1 * cos - y2 * sin, y1 * sin + y2 * cos

    k1 = sdot(kx, e1, e1t)
    k2 = sdot(kx, e2, e2t)
    return normrope(qn, q1, q2, gqn, gq1, gq2) + normrope(kn, k1, k2, gkn, gk1, gk2)


def _rwkv_prep_fn(vres):
    def fn(r, k, v, xg, xwa, kx, *rest):
        if vres:
            vfirst, w0, a0, k_k, k_a, w2p, a2p, g2, v0, v2p, bm = rest
        else:
            w0, a0, k_k, k_a, w2p, a2p, g2, bm = rest
        z = w0 + bdot(jnp.tanh(xwa), w2p)
        nz = -z
        softplus = jnp.maximum(nz, 0.0) + jnp.log(1.0 + jnp.exp(-jnp.abs(nz)))
        decay = jnp.exp(-jnp.exp(-softplus - 0.5))
        a = _sigmoid(a0 + bdot(xwa, a2p))
        g = bdot(_sigmoid(xg), g2)
        if vres:
            vv = v + (vfirst - v) * _sigmoid(v0 + bdot(kx, v2p))
        else:
            vv = v
        kkr = k * k_k
        kk = kkr / jnp.maximum(jnp.sqrt(sdot(kkr * kkr, bm, bm)), 1e-12)
        k2 = k * (1.0 + (a - 1.0) * k_a)
        return r * 1.0, decay, k2, vv, -kk, kk * a, g
    return fn


def _rwkv_post_fn(y, r, k2, vv, g, ln_w, ln_b, rk, bm):
    inv_n = 1.0 / RW_N
    mean = sdot(y, bm, bm) * inv_n
    yc = y - mean
    var = sdot(yc * yc, bm, bm) * inv_n
    yn = yc * lax.rsqrt(var + GN_EPS) * ln_w + ln_b
    bonus = sdot(r * k2 * rk, bm, bm) * vv
    return ((yn + bonus) * g,)


def _merge_fn(g0, g1, g2, oa, ob, oc):
    return (_sigmoid(g0) * oa + _sigmoid(g1) * ob + _sigmoid(g2) * oc,)


def _relu2_fn(u):
    r = jnp.maximum(u, 0.0)
    return (r * r,)


def _loss_call(y, target):
    T, C = y.shape
    tb = min(256, T)

    def body(y_ref, t_ref, dy_ref, part_ref):
        err = y_ref[...] - t_ref[...]
        dy_ref[...] = err * (1.0 / C)
        sq = jnp.sum(err * err, axis=0, keepdims=True)
        acc = sq[:, 0:LANES]
        for j in range(1, C // LANES):
            acc = acc + sq[:, j * LANES:(j + 1) * LANES]
        part_ref[...] = jnp.zeros(part_ref.shape, F32)
        part_ref[0:1, :] = acc * (0.5 / C)

    return pl.pallas_call(
        body, name='loss', grid=(T // tb,),
        in_specs=[pl.BlockSpec((tb, C), lambda i: (i, 0))] * 2,
        out_specs=[pl.BlockSpec((tb, C), lambda i: (i, 0)), pl.BlockSpec((8, LANES), lambda i: (i, 0))],
        out_shape=[jax.ShapeDtypeStruct((T, C), F32), jax.ShapeDtypeStruct((8 * (T // tb), LANES), F32)],
        compiler_params=_cparams(('parallel',)),
    )(y, target)


def _cols(w, lo, hi):
    return w.map(lambda t: t[:, lo:hi])


def _pad_rows(t, before, total):
    return jnp.pad(t, ((before, total - before - t.shape[0]), (0, 0)))


def _head_tile(g, lo, hi):
    return jnp.tile(g[lo:hi], MLA_HEADS).reshape(1, -1)


def _layer(l, x, v_first, wd, sp, cos, sin):
    T = x.shape[0]
    nm = f'l{l}'
    vres = l > 0
    w_in = wd['w_in']
    if vres:
        v1 = wd['rwkv_v1']
    else:
        v1 = W(jnp.zeros((D_MODEL, MV_LORA), BF16), jnp.zeros((D_MODEL, MV_LORA), F32))
    zpad = W(jnp.zeros((D_MODEL, 64), BF16), jnp.zeros((D_MODEL, 64), F32))
    w_rw = _wcat([_cols(w_in, 3744, 4512), _cols(w_in, 4640, 4768), _cols(w_in, 4512, 4640),
                  _cols(w_in, 3712, 3744), v1, zpad], 1)
    h = rmsnorm(x, sp['attn_norm'], nm + '_anorm')
    gates, cq, ckv, rw, cv = multi_linear(
        h, [_cols(w_in, 0, 3072), _cols(w_in, 3072, 3456), _cols(w_in, 3456, 3712), w_rw, _cols(w_in, 4768, 5536)],
        nm + '_win')

    v_mu = sp['rwkv_v_mu'] if vres else jnp.zeros((MV_LORA,), F32)
    mu_all = jnp.concatenate([sp['rwkv_mu'][0:768], sp['rwkv_mu'][896:1024], sp['rwkv_mu'][768:896],
                              jnp.zeros((QK_ROPE,), F32), v_mu, jnp.zeros((64,), F32)]).reshape(1, -1)
    rws = token_shift_mix(rw, mu_all, nm + '_shift')

    cqn = rmsnorm(cq, sp['mla_q_a_norm'], nm + '_qan')
    ckvn = rmsnorm(ckv, sp['mla_kv_a_norm'], nm + '_kvan')
    wq = wd['mla_wq_b'].map(lambda t: jnp.concatenate(
        [t.reshape(Q_LORA, MLA_HEADS, QK_HEAD)[:, :, 0:64].reshape(Q_LORA, 512),
         t.reshape(Q_LORA, MLA_HEADS, QK_HEAD)[:, :, 64:80].reshape(Q_LORA, 128),
         t.reshape(Q_LORA, MLA_HEADS, QK_HEAD)[:, :, 80:96].reshape(Q_LORA, 128)], axis=1))
    wkn = wd['mla_wkv_b'].map(lambda t: t.reshape(KV_LORA, MLA_HEADS, 128)[:, :, 0:64].reshape(KV_LORA, 512))
    wv = wd['mla_wkv_b'].map(lambda t: t.reshape(KV_LORA, MLA_HEADS, 128)[:, :, 64:128].reshape(KV_LORA, 512))
    q = linear(cqn, wq, name=nm + '_wq')
    kn, vv_att = multi_linear(ckvn, [wkn, wv], nm + '_wkv')
    gq, gk = sp['mla_q_norm'], sp['mla_k_norm']
    consts = _mla_consts()
    qk_specs = ([ROW(pieces=((0, 512), (512, 128), (640, 128))), ROW(), ROW(pieces=((1024, 128),)),
                 ROW(False), ROW(False)] + [FULL()] * 6 + [FULL(False)] * 8)
    qk_op = stage_op(_qk_prep_fn, qk_specs, [512, 128, 128, 512, 128, 128], nm + '_qkprep')
    Qn, Q1, Q2, Kn, K1, K2 = qk_op(q, kn, rws, cos, sin,
                                   _head_tile(gq, 0, 64), _head_tile(gq, 64, 80), _head_tile(gq, 80, 96),
                                   _head_tile(gk, 0, 64), _head_tile(gk, 64, 80), _head_tile(gk, 80, 96), *consts)
    o_att = attention(Qn, Q1, Q2, Kn, K1, K2, vv_att, nm + '_att')
    o_a = linear(o_att, wd['mla_w_o'], name=nm + '_wo')

    bm = _np_bf16(_block_ones(RW_WIDTH, RW_N))
    vec = lambda n: sp[n].reshape(1, -1)
    f32w = lambda n: wd[n].c + wd[n].b.astype(F32)
    w2p = _pad_rows(f32w('rwkv_w2'), 0, 128)
    a2p = _pad_rows(f32w('rwkv_a2'), 64, 128)
    g2 = f32w('rwkv_g2')
    rw_pieces = ((0, 256), (256, 256), (512, 256), (768, 128), (896, 128), (1024, 128))
    if vres:
        v2p = _pad_rows(f32w('rwkv_v2'), 32, 128)
        prep_specs = [ROW(pieces=rw_pieces), ROW()] + [FULL()] * 9 + [FULL(False)]
        prep_in = [rws, v_first, vec('rwkv_w0'), vec('rwkv_a0'), vec('rwkv_k_k'), vec('rwkv_k_a'), w2p, a2p, g2,
                   vec('rwkv_v0'), v2p, bm]
    else:
        prep_specs = [ROW(pieces=rw_pieces)] + [FULL()] * 7 + [FULL(False)]
        prep_in = [rws, vec('rwkv_w0'), vec('rwkv_a0'), vec('rwkv_k_k'), vec('rwkv_k_a'), w2p, a2p, g2, bm]
    prep_op = stage_op(_rwkv_prep_fn(vres), prep_specs, [256] * 7, nm + '_rwprep')
    r_, dec, k2, vv, an, bn, g = prep_op(*prep_in)
    if not vres:
        v_first = vv
    t3 = lambda t: t.reshape(T, 1, RW_WIDTH)
    y = wkv7(t3(r_), t3(dec), t3(k2), t3(vv), t3(an), t3(bn), nm + '_scan').reshape(T, RW_WIDTH)
    post_op = stage_op(_rwkv_post_fn, [ROW()] * 5 + [FULL()] * 3 + [FULL(False)], [256], nm + '_rwpost')
    yb = post_op(y, r_, k2, vv, g, vec('rwkv_ln_w'), vec('rwkv_ln_b'), sp['rwkv_r_k'].reshape(1, -1), bm)[0]
    o_b = linear(yb, wd['rwkv_w_o'], name=nm + '_rwo')

    oc_in = short_conv(cv, f32w('conv_w'), nm + '_conv')
    o_c = linear(oc_in, wd['conv_w_o'], name=nm + '_cwo')

    merge_op = stage_op(_merge_fn, [ROW(pieces=((0, 1024), (1024, 1024), (2048, 1024))), ROW(), ROW(), ROW()],
                        [D_MODEL], nm + '_merge')
    merged = merge_op(gates, o_a, o_b, o_c)[0]
    x2 = linear(merged, wd['w_out'], add=x, name=nm + '_wout')
    h2 = rmsnorm(x2, sp['mlp_norm'], nm + '_mnorm')
    u = linear(h2, wd['w_up'], name=nm + '_wup')
    act = stage_op(_relu2_fn, [ROW()], [D_FF], nm + '_relu2')(u)[0]
    x3 = linear(act, wd['w_down'], add=x2, name=nm + '_wdown')
    return x3, v_first


def _entries():
    out = []
    for name, (layers, shape, axis) in SHARDED.items():
        n = shape[0] * shape[1] // N_DEV
        for l in range(layers):
            if name == 'conv_w':
                out.append(('conv_w_hi', name, l, n))
                out.append(('conv_w_lo', name, l, n))
            else:
                out.append((name, name, l, n))
    return out


def _slot_size(n):
    return -(-n // LANES) * LANES


def _pack_rows():
    total = sum(_slot_size(n) for _, _, _, n in _entries())
    rows = -(-total // LANES)
    return -(-rows // PACK_ROW_MULT) * PACK_ROW_MULT


def _pack_flat(pieces):
    rows = _pack_rows()
    padded = []
    for p, (_, _, _, n) in zip(pieces, _entries()):
        pad = _slot_size(n) - n
        if pad:
            p = jnp.pad(p, [(0, 0)] * (p.ndim - 1) + [(0, pad)])
        padded.append(p)
    flat = jnp.concatenate(padded, axis=-1)
    tail = rows * LANES - flat.shape[-1]
    if tail:
        flat = jnp.pad(flat, [(0, 0)] * (flat.ndim - 1) + [(0, tail)])
    return flat.reshape(flat.shape[:-1] + (rows, LANES))


def _unpack_flat(buf):
    flat = buf.reshape(buf.shape[:-2] + (-1,))
    out, off = [], 0
    for _, _, _, n in _entries():
        out.append(flat[..., off:off + n])
        off += _slot_size(n)
    return out


def _pack_shards(shards, dtype, split_conv):
    pieces = []
    for slot, name, l, n in _entries():
        a = shards[name][l].reshape(-1)
        if slot == 'conv_w_hi':
            a = a.astype(BF16).astype(F32) if split_conv else a
        elif slot == 'conv_w_lo':
            a = (a - a.astype(BF16).astype(F32)) if split_conv else jnp.zeros_like(a)
        pieces.append(a.astype(dtype))
    return _pack_flat(pieces)


def _unpack_shards(buf):
    vals = _unpack_flat(buf)
    out = {}
    for (slot, name, l, n), v in zip(_entries(), vals):
        if slot == 'conv_w_lo':
            continue
        layers, shape, axis = SHARDED[name]
        sshape = (shape[0] // N_DEV, shape[1]) if axis == 0 else (shape[0], shape[1] // N_DEV)
        out.setdefault(name, []).append(v.reshape(sshape))
    return {k: jnp.stack(v) for k, v in out.items()}


def _to_full(blocks, shape, axis):
    if axis == 0:
        return blocks.reshape(shape)
    return blocks.reshape(N_DEV, shape[0], shape[1] // N_DEV).transpose(1, 0, 2).reshape(shape)


def _to_blocks(full, axis):
    r, c = full.shape
    if axis == 0:
        return full.reshape(N_DEV, -1)
    return full.reshape(r, N_DEV, c // N_DEV).transpose(1, 0, 2).reshape(N_DEV, -1)


def _unpack_gathered(gathered):
    vals = _unpack_flat(gathered)
    out, conv_hi = {}, {}
    for (slot, name, l, n), v in zip(_entries(), vals):
        layers, shape, axis = SHARDED[name]
        full = _to_full(v, shape, axis)
        if slot == 'conv_w_hi':
            conv_hi[l] = full
        elif slot == 'conv_w_lo':
            out[(name, l)] = conv_hi[l].astype(F32) + full.astype(F32)
        else:
            out[(name, l)] = full
    return out


def _pack_grads(grads):
    pieces = []
    for slot, name, l, n in _entries():
        blocks = _to_blocks(grads[(name, l)], SHARDED[name][2])
        if slot == 'conv_w_lo':
            blocks = jnp.zeros_like(blocks)
        pieces.append(blocks)
    return _pack_flat(pieces)


def _my_pos():
    return lax.axis_index('x'), lax.axis_index('y'), lax.axis_index('c')


def _flip(v, bit):
    return 1 - v if bit else v


def all_gather_blocks(x):
    rows = x.shape[0]

    def body(x_ref, out_ref, send_sems, recv_sems, local_sem):
        mx, my, mc = _my_pos()
        me, sibling = (mx, my, mc), (mx, my, 1 - mc)
        chips = [(1 - mx, my), (mx, 1 - my), (1 - mx, 1 - my)]

        def block(px, py, pc):
            return out_ref.at[4 * px + 2 * py + pc]

        def copy(k, blk, to, src=None):
            return pltpu.make_async_remote_copy(
                src_ref=block(*blk) if src is None else src, dst_ref=block(*blk),
                send_sem=send_sems.at[k], recv_sem=recv_sems.at[k], device_id=to, device_id_type=MESH)

        mine = pltpu.make_async_copy(x_ref, block(*me), local_sem)
        mine.start()
        first = [copy(0, me, sibling, src=x_ref)]
        first += [copy(1 + j, me, (*chip, mc), src=x_ref) for j, chip in enumerate(chips)]
        for cp in first:
            cp.start()
        passed = [copy(4 + j, (*chip, mc), sibling) for j, chip in enumerate(chips)]
        for j, chip in enumerate(chips):
            copy(1 + j, (*chip, mc), me).wait_recv()
            passed[j].start()
        copy(0, sibling, me).wait_recv()
        for j, chip in enumerate(chips):
            copy(4 + j, (*chip, 1 - mc), me).wait_recv()
        for cp in first + passed:
            cp.wait_send()
        mine.wait()

    return pl.pallas_call(
        body, name='all_gather_weights',
        out_shape=jax.ShapeDtypeStruct((N_DEV, rows, LANES), x.dtype),
        in_specs=[pl.BlockSpec(memory_space=pl.ANY)], out_specs=pl.BlockSpec(memory_space=pl.ANY),
        scratch_shapes=[pltpu.SemaphoreType.DMA((7,)), pltpu.SemaphoreType.DMA((7,)), pltpu.SemaphoreType.DMA],
    )(x)


def exchange_blocks(g):
    def body(g_ref, out_ref, send_sems, recv_sems, local_sem):
        mx, my, mc = _my_pos()
        me = 4 * mx + 2 * my + mc
        mine = pltpu.make_async_copy(g_ref.at[me], out_ref.at[me], local_sem)
        mine.start()
        copies, arrivals = [], []
        for k in range(1, N_DEV):
            peer = (_flip(mx, k & 4), _flip(my, k & 2), _flip(mc, k & 1))
            pidx = 4 * peer[0] + 2 * peer[1] + peer[2]
            copies.append(pltpu.make_async_remote_copy(
                src_ref=g_ref.at[pidx], dst_ref=out_ref.at[me], send_sem=send_sems.at[k - 1],
                recv_sem=recv_sems.at[k - 1], device_id=peer, device_id_type=MESH))
            arrivals.append(pltpu.make_async_remote_copy(
                src_ref=g_ref.at[me], dst_ref=out_ref.at[pidx], send_sem=send_sems.at[k - 1],
                recv_sem=recv_sems.at[k - 1], device_id=peer, device_id_type=MESH))
        for cp in copies:
            cp.start()
        for cp in arrivals:
            cp.wait_recv()
        for cp in copies:
            cp.wait_send()
        mine.wait()

    return pl.pallas_call(
        body, name='exchange_grads',
        out_shape=jax.ShapeDtypeStruct(g.shape, g.dtype),
        in_specs=[pl.BlockSpec(memory_space=pl.ANY)], out_specs=pl.BlockSpec(memory_space=pl.ANY),
        scratch_shapes=[pltpu.SemaphoreType.DMA((7,)), pltpu.SemaphoreType.DMA((7,)), pltpu.SemaphoreType.DMA],
    )(g)


def _adamw_math(w, g, m, v):
    m2 = ADAM_B1 * m + (1.0 - ADAM_B1) * g
    v2 = ADAM_B2 * v + (1.0 - ADAM_B2) * (g * g)
    m_hat = m2 / (1.0 - ADAM_B1 ** ADAM_STEP)
    v_hat = v2 / (1.0 - ADAM_B2 ** ADAM_STEP)
    delta = -ADAM_LR * (m_hat / (jnp.sqrt(v_hat) + ADAM_EPS) + ADAM_WD * w)
    return delta, m2, v2


def adamw_sharded(parts, w, m, v):
    rows = w.shape[0]
    rb = PACK_ROW_MULT

    def body(p_ref, w_ref, m_ref, v_ref, g_ref, d_ref, m2_ref, v2_ref):
        g = p_ref[0]
        for j in range(1, N_DEV):
            g = g + p_ref[j]
        delta, m2, v2 = _adamw_math(w_ref[...], g, m_ref[...], v_ref[...])
        g_ref[...] = g
        d_ref[...] = delta
        m2_ref[...] = m2
        v2_ref[...] = v2

    blk = pl.BlockSpec((rb, LANES), lambda i: (i, 0))
    sh = jax.ShapeDtypeStruct((rows, LANES), F32)
    return pl.pallas_call(
        body, name='adamw_sharded', grid=(rows // rb,),
        in_specs=[pl.BlockSpec((N_DEV, rb, LANES), lambda i: (0, i, 0)), blk, blk, blk],
        out_specs=[blk] * 4, out_shape=[sh] * 4, compiler_params=_cparams(('parallel',)),
    )(parts, w, m, v)


def allreduce_adamw_small(g, w, m, v):
    rows = g.shape[0]

    def body(g_ref, w_ref, m_ref, v_ref, gs_ref, d_ref, m2_ref, v2_ref, all_ref, send_sems, recv_sems):
        mx, my, mc = _my_pos()
        me, sibling = (mx, my, mc), (mx, my, 1 - mc)
        chips = [(1 - mx, my), (mx, 1 - my), (1 - mx, 1 - my)]

        def block(px, py, pc):
            return all_ref.at[4 * px + 2 * py + pc]

        def copy(k, blk, to, src=None):
            return pltpu.make_async_remote_copy(
                src_ref=block(*blk) if src is None else src, dst_ref=block(*blk),
                send_sem=send_sems.at[k], recv_sem=recv_sems.at[k], device_id=to, device_id_type=MESH)

        first = [copy(0, me, sibling, src=g_ref)]
        first += [copy(1 + j, me, (*chip, mc), src=g_ref) for j, chip in enumerate(chips)]
        for cp in first:
            cp.start()
        passed = [copy(4 + j, (*chip, mc), sibling) for j, chip in enumerate(chips)]
        for j, chip in enumerate(chips):
            copy(1 + j, (*chip, mc), me).wait_recv()
            passed[j].start()
        copy(0, sibling, me).wait_recv()
        for j, chip in enumerate(chips):
            copy(4 + j, (*chip, 1 - mc), me).wait_recv()
        for cp in first + passed:
            cp.wait_send()
        my_idx = 4 * mx + 2 * my + mc
        total = jnp.zeros((rows, LANES), F32)
        for j in range(N_DEV):
            total = total + jnp.where(my_idx == j, g_ref[...], all_ref[j])
        delta, m2, v2 = _adamw_math(w_ref[...], total, m_ref[...], v_ref[...])
        gs_ref[...] = total
        d_ref[...] = delta
        m2_ref[...] = m2
        v2_ref[...] = v2

    vm = pl.BlockSpec(memory_space=pltpu.VMEM)
    sh = jax.ShapeDtypeStruct((rows, LANES), F32)
    return pl.pallas_call(
        body, name='allreduce_adamw_small', in_specs=[vm] * 4, out_specs=[vm] * 4, out_shape=[sh] * 4,
        scratch_shapes=[pltpu.VMEM((N_DEV, rows, LANES), F32), pltpu.SemaphoreType.DMA((7,)),
                        pltpu.SemaphoreType.DMA((7,))],
    )(g, w, m, v)


def _small_pack(d):
    flat = jnp.concatenate([d[n].reshape(-1) for n in SMALL_NAMES])
    rows = -(-flat.shape[0] // (8 * LANES)) * 8
    return jnp.pad(flat, (0, rows * LANES - flat.shape[0])).reshape(rows, LANES)


def _small_unpack(buf, like):
    flat = buf.reshape(-1)
    out, off = {}, 0
    for n in SMALL_NAMES:
        sz = int(np.prod(like[n].shape))
        out[n] = flat[off:off + sz].reshape(like[n].shape)
        off += sz
    return out


def _rope_tables(positions):
    freqs = ROPE_THETA ** (-(jnp.arange(QK_ROPE // 2, dtype=F32) * 2.0 / QK_ROPE))
    ang = positions.astype(F32)[:, None] * freqs
    return jnp.tile(jnp.cos(ang), (1, MLA_HEADS)), jnp.tile(jnp.sin(ang), (1, MLA_HEADS))


def _forward(carriers, small, x, gathered, cos, sin):
    v_first = None
    for l in range(DEPTH):
        wd = {}
        for name, (layers, _, _) in SHARDED.items():
            ll = l if layers == DEPTH else l - 1
            if 0 <= ll < layers:
                wd[name] = W(gathered[(name, ll)], carriers[(name, ll)])
        sp = {}
        for n in SMALL_NAMES:
            ll = l if small[n].shape[0] == DEPTH else l - 1
            if 0 <= ll < small[n].shape[0]:
                sp[n] = small[n][ll]
        x, v_first = _layer(l, x, v_first, wd, sp, cos, sin)
    return x


def _train_step(x, positions, loss_target, weights, moms_m, moms_v):
    shards = {n: weights[n] for n in SHARDED}
    small = {n: weights[n] for n in SMALL_NAMES}

    gathered = _unpack_gathered(all_gather_blocks(_pack_shards(shards, BF16, True)))
    carriers = {}
    for (name, l), full in gathered.items():
        carriers[(name, l)] = jnp.zeros(SHARDED[name][1], F32)
        if name == 'conv_w':
            gathered[(name, l)] = full.astype(BF16)
            carriers[(name, l)] = full - full.astype(BF16).astype(F32)
    cos, sin = _rope_tables(positions[0])

    y, vjp = jax.vjp(lambda c, s, xx: _forward(c, s, xx, gathered, cos, sin), carriers, small, x[0])
    dy, parts = _loss_call(y, loss_target[0])
    loss = lax.psum(jnp.sum(parts), ('x', 'y', 'c'))
    g_full, g_small, g_x = vjp(dy)

    received = exchange_blocks(_pack_grads(g_full))
    g_sh, d_sh, m_sh, v_sh = (_unpack_shards(b) for b in adamw_sharded(
        received, _pack_shards(shards, F32, False), _pack_shards({n: moms_m[n] for n in SHARDED}, F32, False),
        _pack_shards({n: moms_v[n] for n in SHARDED}, F32, False)))
    g_sm, d_sm, m_sm, v_sm = (_small_unpack(b, small) for b in allreduce_adamw_small(
        _small_pack(g_small), _small_pack(small), _small_pack({n: moms_m[n] for n in SMALL_NAMES}),
        _small_pack({n: moms_v[n] for n in SMALL_NAMES})))

    pick = lambda sh, sm: [sh[n] if n in SHARDED else sm[n] for n in WEIGHT_NAMES]
    return (loss, g_x[None], *pick(g_sh, g_sm), *pick(d_sh, d_sm), *pick(m_sh, m_sm), *pick(v_sh, v_sm))


def kernel(x, positions, attn_norm, w_in, mla_q_a_norm, mla_wq_b, mla_kv_a_norm, mla_wkv_b, mla_q_norm, mla_k_norm, mla_w_o, rwkv_mu, rwkv_w0, rwkv_w2, rwkv_a0, rwkv_a2, rwkv_g2, rwkv_k_k, rwkv_k_a, rwkv_r_k, rwkv_ln_w, rwkv_ln_b, rwkv_w_o, rwkv_v1, rwkv_v_mu, rwkv_v0, rwkv_v2, conv_w, conv_w_o, w_out, mlp_norm, w_up, w_down, loss_target, m_attn_norm, m_w_in, m_mla_q_a_norm, m_mla_wq_b, m_mla_kv_a_norm, m_mla_wkv_b, m_mla_q_norm, m_mla_k_norm, m_mla_w_o, m_rwkv_mu, m_rwkv_w0, m_rwkv_w2, m_rwkv_a0, m_rwkv_a2, m_rwkv_g2, m_rwkv_k_k, m_rwkv_k_a, m_rwkv_r_k, m_rwkv_ln_w, m_rwkv_ln_b, m_rwkv_w_o, m_rwkv_v1, m_rwkv_v_mu, m_rwkv_v0, m_rwkv_v2, m_conv_w, m_conv_w_o, m_w_out, m_mlp_norm, m_w_up, m_w_down, v_attn_norm, v_w_in, v_mla_q_a_norm, v_mla_wq_b, v_mla_kv_a_norm, v_mla_wkv_b, v_mla_q_norm, v_mla_k_norm, v_mla_w_o, v_rwkv_mu, v_rwkv_w0, v_rwkv_w2, v_rwkv_a0, v_rwkv_a2, v_rwkv_g2, v_rwkv_k_k, v_rwkv_k_a, v_rwkv_r_k, v_rwkv_ln_w, v_rwkv_ln_b, v_rwkv_w_o, v_rwkv_v1, v_rwkv_v_mu, v_rwkv_v0, v_rwkv_v2, v_conv_w, v_conv_w_o, v_w_out, v_mlp_norm, v_w_up, v_w_down):
    args = locals()
    weights = {n: args[n] for n in WEIGHT_NAMES}
    moms_m = {n: args['m_' + n] for n in WEIGHT_NAMES}
    moms_v = {n: args['v_' + n] for n in WEIGHT_NAMES}
    return _train_step(x, positions, loss_target, weights, moms_m, moms_v)
```

```python
import functools

import numpy as np
import jax
import jax.numpy as jnp
from jax import lax
from jax.experimental import pallas as pl
from jax.experimental.pallas import tpu as pltpu

F32 = jnp.float32
BF16 = jnp.bfloat16

N_DEV = 8
LANES = 128
D_MODEL = 1024
DEPTH = 2
MLA_HEADS = 8
QK_NOPE = 64
QK_ROPE = 32
QK_HEAD = QK_NOPE + QK_ROPE
V_HEAD = 64
Q_LORA = 384
KV_LORA = 256
ROPE_THETA = 10000.0
RW_HEADS = 4
RW_N = 64
RW_WIDTH = RW_HEADS * RW_N
MV_LORA = 32
GN_EPS = 64e-5
CONV_WIDTH = 256
D_FF = 4 * D_MODEL
NORM_EPS = 1e-6
ADAM_LR = 0.001
ADAM_B1 = 0.9
ADAM_B2 = 0.999
ADAM_EPS = 1e-08
ADAM_WD = 0.01
ADAM_STEP = 10

VMEM_LIMIT = 56 * 1024 * 1024
MESH = pl.DeviceIdType.MESH

WEIGHT_NAMES = ['attn_norm', 'w_in', 'mla_q_a_norm', 'mla_wq_b', 'mla_kv_a_norm', 'mla_wkv_b', 'mla_q_norm',
                'mla_k_norm', 'mla_w_o', 'rwkv_mu', 'rwkv_w0', 'rwkv_w2', 'rwkv_a0', 'rwkv_a2', 'rwkv_g2',
                'rwkv_k_k', 'rwkv_k_a', 'rwkv_r_k', 'rwkv_ln_w', 'rwkv_ln_b', 'rwkv_w_o', 'rwkv_v1',
                'rwkv_v_mu', 'rwkv_v0', 'rwkv_v2', 'conv_w', 'conv_w_o', 'w_out', 'mlp_norm', 'w_up', 'w_down']

SHARDED = {
    'w_in': (2, (1024, 5536), 1), 'mla_wq_b': (2, (384, 768), 1), 'mla_wkv_b': (2, (256, 1024), 1),
    'mla_w_o': (2, (512, 1024), 1), 'rwkv_w2': (2, (64, 256), 1), 'rwkv_a2': (2, (64, 256), 1),
    'rwkv_g2': (2, (128, 256), 1), 'rwkv_w_o': (2, (256, 1024), 1), 'conv_w': (2, (3, 256), 1),
    'conv_w_o': (2, (256, 1024), 1), 'w_out': (2, (1024, 1024), 0), 'w_up': (2, (1024, 4096), 1),
    'w_down': (2, (4096, 1024), 0), 'rwkv_v1': (1, (1024, 32), 0), 'rwkv_v2': (1, (32, 256), 1),
}
SMALL_NAMES = [n for n in WEIGHT_NAMES if n not in SHARDED]
TRANSPOSED = ('w_in',)
WIN_SEGS = (('gates', 0, 3072), ('cq', 3072, 3456), ('ckv', 3456, 3712), ('kpe', 3712, 3744), ('rkv', 3744, 4512),
            ('xwa', 4512, 4640), ('xg', 4640, 4768), ('conv', 4768, 5536))
PACK_ROW_MULT = 512


def _cparams(sem=None, **kw):
    if sem is not None:
        kw['dimension_semantics'] = sem
    return pltpu.CompilerParams(vmem_limit_bytes=VMEM_LIMIT, **kw)


def _pick(n, cands):
    for c in cands:
        if n % c == 0:
            return c
    raise ValueError(f'no tile for {n}')


def _mm_nn(a, b, add=None, name='mm_nn'):
    M, K = a.shape
    N = b.shape[1]
    tm = _pick(M, (512, 256, 128))
    tn = _pick(N, (512, 384, 256, 128))
    tk = _pick(K, (1024, 512, 384, 256, 128))
    nk = K // tk
    has_add = add is not None

    def body(*refs):
        if has_add:
            a_ref, b_ref, add_ref, o_ref, acc_ref = refs
        else:
            a_ref, b_ref, o_ref, acc_ref = refs
        kk = pl.program_id(2)
        part = jnp.dot(a_ref[...].astype(BF16), b_ref[...].astype(BF16), preferred_element_type=F32)

        @pl.when(kk == 0)
        def _():
            acc_ref[...] = part

        @pl.when(kk > 0)
        def _():
            acc_ref[...] += part

        @pl.when(kk == nk - 1)
        def _():
            if has_add:
                o_ref[...] = acc_ref[...] + add_ref[...]
            else:
                o_ref[...] = acc_ref[...]

    in_specs = [pl.BlockSpec((tm, tk), lambda i, j, k: (i, k)), pl.BlockSpec((tk, tn), lambda i, j, k: (k, j))]
    args = [a, b]
    if has_add:
        in_specs.append(pl.BlockSpec((tm, tn), lambda i, j, k: (i, j)))
        args.append(add)
    return pl.pallas_call(
        body, name=name, grid=(M // tm, N // tn, nk), in_specs=in_specs,
        out_specs=pl.BlockSpec((tm, tn), lambda i, j, k: (i, j)),
        out_shape=jax.ShapeDtypeStruct((M, N), F32),
        scratch_shapes=[pltpu.VMEM((tm, tn), F32)],
        compiler_params=_cparams(('parallel', 'parallel', 'arbitrary')),
    )(*args)


def _mm_nt(a, b, add=None, name='mm_nt'):
    M, N = a.shape
    K = b.shape[0]
    tm = _pick(M, (512, 256, 128))
    tk = _pick(K, (512, 384, 256, 128))
    tn = _pick(N, (1024, 512, 384, 256, 128))
    nn = N // tn
    has_add = add is not None

    def body(*refs):
        if has_add:
            a_ref, b_ref, add_ref, o_ref, acc_ref = refs
        else:
            a_ref, b_ref, o_ref, acc_ref = refs
        kk = pl.program_id(2)
        part = lax.dot_general(a_ref[...].astype(BF16), b_ref[...].astype(BF16), (((1,), (1,)), ((), ())),
                               preferred_element_type=F32)

        @pl.when(kk == 0)
        def _():
            acc_ref[...] = part

        @pl.when(kk > 0)
        def _():
            acc_ref[...] += part

        @pl.when(kk == nn - 1)
        def _():
            if has_add:
                o_ref[...] = acc_ref[...] + add_ref[...]
            else:
                o_ref[...] = acc_ref[...]

    in_specs = [pl.BlockSpec((tm, tn), lambda i, j, k: (i, k)), pl.BlockSpec((tk, tn), lambda i, j, k: (j, k))]
    args = [a, b]
    if has_add:
        in_specs.append(pl.BlockSpec((tm, tk), lambda i, j, k: (i, j)))
        args.append(add)
    return pl.pallas_call(
        body, name=name, grid=(M // tm, K // tk, nn), in_specs=in_specs,
        out_specs=pl.BlockSpec((tm, tk), lambda i, j, k: (i, j)),
        out_shape=jax.ShapeDtypeStruct((M, K), F32),
        scratch_shapes=[pltpu.VMEM((tm, tk), F32)],
        compiler_params=_cparams(('parallel', 'parallel', 'arbitrary')),
    )(*args)


def _mm_tn(a, b, name='mm_tn'):
    M, K = a.shape
    N = b.shape[1]
    tm = _pick(M, (512, 256, 128))
    tk = _pick(K, (512, 384, 256, 128))
    tn = _pick(N, (512, 384, 256, 128))
    nm = M // tm

    def body(a_ref, b_ref, o_ref, acc_ref):
        mm = pl.program_id(2)
        part = lax.dot_general(a_ref[...].astype(BF16), b_ref[...].astype(BF16), (((0,), (0,)), ((), ())),
                               preferred_element_type=F32)

        @pl.when(mm == 0)
        def _():
            acc_ref[...] = part

        @pl.when(mm > 0)
        def _():
            acc_ref[...] += part

        @pl.when(mm == nm - 1)
        def _():
            o_ref[...] = acc_ref[...]

    return pl.pallas_call(
        body, name=name, grid=(K // tk, N // tn, nm),
        in_specs=[pl.BlockSpec((tm, tk), lambda i, j, m: (m, i)), pl.BlockSpec((tm, tn), lambda i, j, m: (m, j))],
        out_specs=pl.BlockSpec((tk, tn), lambda i, j, m: (i, j)),
        out_shape=jax.ShapeDtypeStruct((K, N), F32),
        scratch_shapes=[pltpu.VMEM((tk, tn), F32)],
        compiler_params=_cparams(('parallel', 'parallel', 'arbitrary')),
    )(a, b)


@functools.partial(jax.custom_vjp, nondiff_argnums=(4,))
def _linear_add(a, wb, wc, add, name):
    return _mm_nn(a, wb, add, name=name + '_f')


def _linear_add_fwd(a, wb, wc, add, name):
    return _mm_nn(a, wb, add, name=name + '_f'), (a, wb)


def _linear_add_bwd(name, res, dy):
    a, wb = res
    return _mm_nt(dy, wb, name=name + '_da'), None, _mm_tn(a, dy, name=name + '_dw'), dy


_linear_add.defvjp(_linear_add_fwd, _linear_add_bwd)


@functools.partial(jax.custom_vjp, nondiff_argnums=(3,))
def _multi_linear(a, wbs, wcs, name):
    return tuple(_mm_nn(a, wb, name=f'{name}_f{i}') for i, wb in enumerate(wbs))


def _multi_linear_fwd(a, wbs, wcs, name):
    return _multi_linear(a, wbs, wcs, name), (a, wbs)


def _multi_linear_bwd(name, res, dys):
    a, wbs = res
    da = None
    for i, (dy, wb) in enumerate(zip(dys, wbs)):
        da = _mm_nt(dy, wb, add=da, name=f'{name}_da{i}')
    dws = tuple(_mm_tn(a, dy, name=f'{name}_dw{i}') for i, dy in enumerate(dys))
    return da, None, dws


_multi_linear.defvjp(_multi_linear_fwd, _multi_linear_bwd)


class W:
    def __init__(self, b, c):
        self.b, self.c = b, c

    def map(self, fn):
        return W(fn(self.b), fn(self.c))


def _wcat(ws, axis):
    return W(jnp.concatenate([w.b for w in ws], axis), jnp.concatenate([w.c for w in ws], axis))


def linear(a, w, add=None, name='lin'):
    if add is None:
        return _multi_linear(a, (w.b,), (w.c,), name)[0]
    return _linear_add(a, w.b, w.c, add, name)


def multi_linear(a, ws, name):
    return _multi_linear(a, tuple(w.b for w in ws), tuple(w.c for w in ws), name)


@functools.partial(jax.custom_vjp, nondiff_argnums=(3,))
def _multi_linear_t(a, wbs, wcs, name):
    return tuple(_mm_nt(a, wb, name=f'{name}_f{i}') for i, wb in enumerate(wbs))


def _multi_linear_t_fwd(a, wbs, wcs, name):
    return _multi_linear_t(a, wbs, wcs, name), (a, wbs)


def _multi_linear_t_bwd(name, res, dys):
    a, wbs = res
    da = None
    for i, (dy, wb) in enumerate(zip(dys, wbs)):
        da = _mm_nn(dy, wb, add=da, name=f'{name}_da{i}')
    dws = tuple(_mm_tn(dy, a, name=f'{name}_dw{i}') for i, dy in enumerate(dys))
    return da, None, dws


_multi_linear_t.defvjp(_multi_linear_t_fwd, _multi_linear_t_bwd)


def multi_linear_t(a, ws, name):
    return _multi_linear_t(a, tuple(w.b for w in ws), tuple(w.c for w in ws), name)


def ROW(diff=True, pieces=None):
    return ('row', diff, pieces)


def FULL(diff=True):
    return ('full', diff, None)


def _load_args(refs, specs):
    args, amap = [], []
    for i, (ref, (kind, diff, pieces)) in enumerate(zip(refs, specs)):
        if pieces is None:
            args.append(ref[...])
            amap.append((i, None))
        else:
            for (s, w) in pieces:
                args.append(ref[:, s:s + w])
                amap.append((i, (s, w)))
    return args, amap


def _stage_in_specs(ins, specs, tb):
    out = []
    for a, (kind, _, _) in zip(ins, specs):
        if kind == 'row':
            out.append(pl.BlockSpec((tb, a.shape[1]), lambda i: (i, 0)))
        else:
            out.append(pl.BlockSpec(a.shape, lambda i: (0, 0)))
    return out


def _stage_fwd(fn, ins, specs, out_widths, name, tb):
    T = [a for a, s in zip(ins, specs) if s[0] == 'row'][0].shape[0]
    tb = min(tb, T)
    n_in = len(ins)

    def body(*refs):
        args, _ = _load_args(refs[:n_in], specs)
        outs = fn(*args)
        for o_ref, o in zip(refs[n_in:], outs):
            o_ref[...] = o

    return pl.pallas_call(
        body, name=name + '_f', grid=(T // tb,), in_specs=_stage_in_specs(ins, specs, tb),
        out_specs=[pl.BlockSpec((tb, w), lambda i: (i, 0)) for w in out_widths],
        out_shape=[jax.ShapeDtypeStruct((T, w), F32) for w in out_widths],
        compiler_params=_cparams(('parallel',)),
    )(*ins)


def _stage_bwd(fn, ins, specs, out_widths, douts, name, tb):
    T = [a for a, s in zip(ins, specs) if s[0] == 'row'][0].shape[0]
    tb = min(tb, T)
    n_in, n_out = len(ins), len(out_widths)
    diff_inputs = [i for i, s in enumerate(specs) if s[1]]

    def body(*refs):
        in_refs, dout_refs, g_refs = refs[:n_in], refs[n_in:n_in + n_out], refs[n_in + n_out:]
        args, amap = _load_args(in_refs, specs)
        didx = [j for j, (i, _) in enumerate(amap) if specs[i][1]]

        def f(*dv):
            full = list(args)
            for j, v in zip(didx, dv):
                full[j] = v
            return tuple(fn(*full))

        _, vjp = jax.vjp(f, *[args[j] for j in didx])
        gs = vjp(tuple(d[...] for d in dout_refs))
        gmap = {j: g for j, g in zip(didx, gs)}
        first = pl.program_id(0) == 0
        for g_ref, i in zip(g_refs, diff_inputs):
            kind, _, pieces = specs[i]
            js = [j for j, (ii, _) in enumerate(amap) if ii == i]
            if kind == 'row':
                if pieces is None:
                    g_ref[...] = gmap[js[0]]
                else:
                    if sum(w for _, w in pieces) != ins[i].shape[1]:
                        g_ref[...] = jnp.zeros(g_ref.shape, F32)
                    for j in js:
                        s, w = amap[j][1]
                        g_ref[:, s:s + w] = gmap[j]
            else:
                @pl.when(first)
                def _(g_ref=g_ref):
                    g_ref[...] = jnp.zeros(g_ref.shape, F32)

                g_ref[...] += gmap[js[0]]

    in_specs = _stage_in_specs(ins, specs, tb) + [pl.BlockSpec((tb, w), lambda i: (i, 0)) for w in out_widths]
    out_specs, out_shape = [], []
    for i in diff_inputs:
        a = ins[i]
        if specs[i][0] == 'row':
            out_specs.append(pl.BlockSpec((tb, a.shape[1]), lambda i: (i, 0)))
        else:
            out_specs.append(pl.BlockSpec(a.shape, lambda i: (0, 0)))
        out_shape.append(jax.ShapeDtypeStruct(a.shape, F32))
    return pl.pallas_call(
        body, name=name + '_b', grid=(T // tb,), in_specs=in_specs, out_specs=out_specs, out_shape=out_shape,
        compiler_params=_cparams(('arbitrary',)),
    )(*ins, *douts)


def stage_op(fn, specs, out_widths, name, tb=256):
    n = len(specs)
    diff_inputs = [i for i, s in enumerate(specs) if s[1]]

    @jax.custom_vjp
    def op(*ins):
        return tuple(_stage_fwd(fn, ins, specs, out_widths, name, tb))

    def op_fwd(*ins):
        return op(*ins), ins

    def op_bwd(ins, douts):
        gs = _stage_bwd(fn, ins, specs, out_widths, douts, name, tb)
        res = [None] * n
        for i, g in zip(diff_inputs, gs):
            res[i] = g
        return tuple(res)

    op.defvjp(op_fwd, op_bwd)
    return op


@jax.custom_vjp
def bdot(x, w):
    return jnp.dot(x.astype(BF16), w.astype(BF16), preferred_element_type=F32)


def _bdot_fwd(x, w):
    return bdot(x, w), (x, w)


def _bdot_bwd(res, dy):
    x, w = res
    dyb = dy.astype(BF16)
    dx = lax.dot_general(dyb, w.astype(BF16), (((1,), (1,)), ((), ())), preferred_element_type=F32)
    dw = lax.dot_general(x.astype(BF16), dyb, (((0,), (0,)), ((), ())), preferred_element_type=F32)
    return dx, dw


bdot.defvjp(_bdot_fwd, _bdot_bwd)


def _sdot_raw(x, c):
    hi = x.astype(BF16)
    r1 = x - hi.astype(F32)
    mid = r1.astype(BF16)
    lo = (r1 - mid.astype(F32)).astype(BF16)
    d = lambda u: jnp.dot(u, c, preferred_element_type=F32)
    return d(hi) + d(mid) + d(lo)


@jax.custom_vjp
def sdot(x, c, ct):
    return _sdot_raw(x, c)


def _sdot_fwd(x, c, ct):
    return _sdot_raw(x, c), (c, ct)


def _sdot_bwd(res, dy):
    c, ct = res
    return _sdot_raw(dy, ct), None, None


sdot.defvjp(_sdot_fwd, _sdot_bwd)


def _sigmoid(x):
    return 1.0 / (1.0 + jnp.exp(-x))


def _rms(x, g):
    return x * lax.rsqrt(jnp.mean(x * x, axis=-1, keepdims=True) + NORM_EPS) * g


def rmsnorm(x, g, name):
    op = stage_op(lambda xv, gv: (_rms(xv, gv),), [ROW(), FULL()], [x.shape[1]], name)
    return op(x, g.reshape(1, -1))[0]


def _shift_down(x, rows):
    return jnp.where(rows == 0, 0.0, pltpu.roll(x, 1, 0))


def _shift_up(x, rows, T):
    return jnp.where(rows == T - 1, 0.0, pltpu.roll(x, T - 1, 0))


def _tshift_fwd_call(x, mu, name):
    T, C = x.shape

    def body(x_ref, mu_ref, o_ref):
        xv = x_ref[...]
        rows = lax.broadcasted_iota(jnp.int32, xv.shape, 0)
        o_ref[...] = xv + (_shift_down(xv, rows) - xv) * mu_ref[...]

    return pl.pallas_call(
        body, name=name + '_f', grid=(C // LANES,),
        in_specs=[pl.BlockSpec((T, LANES), lambda j: (0, j)), pl.BlockSpec((1, LANES), lambda j: (0, j))],
        out_specs=pl.BlockSpec((T, LANES), lambda j: (0, j)), out_shape=jax.ShapeDtypeStruct((T, C), F32),
        compiler_params=_cparams(('parallel',)),
    )(x, mu)


def _tshift_bwd_call(x, mu, dy, name):
    T, C = x.shape

    def body(x_ref, mu_ref, dy_ref, dx_ref, dmu_ref):
        xv, d = x_ref[...], dy_ref[...]
        rows = lax.broadcasted_iota(jnp.int32, xv.shape, 0)
        z = d * mu_ref[...]
        dx_ref[...] = d - z + _shift_up(z, rows, T)
        dmu_ref[...] = jnp.sum(d * (_shift_down(xv, rows) - xv), axis=0, keepdims=True)

    return pl.pallas_call(
        body, name=name + '_b', grid=(C // LANES,),
        in_specs=[pl.BlockSpec((T, LANES), lambda j: (0, j)), pl.BlockSpec((1, LANES), lambda j: (0, j)),
                  pl.BlockSpec((T, LANES), lambda j: (0, j))],
        out_specs=[pl.BlockSpec((T, LANES), lambda j: (0, j)), pl.BlockSpec((1, LANES), lambda j: (0, j))],
        out_shape=[jax.ShapeDtypeStruct((T, C), F32), jax.ShapeDtypeStruct((1, C), F32)],
        compiler_params=_cparams(('parallel',)),
    )(x, mu, dy)


@functools.partial(jax.custom_vjp, nondiff_argnums=(2,))
def token_shift_mix(x, mu, name):
    return _tshift_fwd_call(x, mu, name)


def _tsm_fwd(x, mu, name):
    return _tshift_fwd_call(x, mu, name), (x, mu)


def _tsm_bwd(name, res, dy):
    x, mu = res
    dx, dmu = _tshift_bwd_call(x, mu, dy, name)
    return dx, dmu


token_shift_mix.defvjp(_tsm_fwd, _tsm_bwd)


def _conv_specs(T):
    nb = CONV_WIDTH // LANES
    return [pl.BlockSpec((T, LANES), lambda j: (0, j)), pl.BlockSpec((T, LANES), lambda j: (0, nb + j)),
            pl.BlockSpec((T, LANES), lambda j: (0, 2 * nb + j)), pl.BlockSpec((3, LANES), lambda j: (0, j))]


def _conv_fwd_call(cv, w, name):
    T = cv.shape[0]

    def body(b_ref, c_ref, x_ref, w_ref, o_ref):
        u = c_ref[...] * x_ref[...]
        rows = lax.broadcasted_iota(jnp.int32, u.shape, 0)
        u1 = _shift_down(u, rows)
        u2 = _shift_down(u1, rows)
        o_ref[...] = b_ref[...] * (w_ref[0:1, :] * u2 + w_ref[1:2, :] * u1 + w_ref[2:3, :] * u)

    return pl.pallas_call(
        body, name=name + '_f', grid=(CONV_WIDTH // LANES,), in_specs=_conv_specs(T),
        out_specs=pl.BlockSpec((T, LANES), lambda j: (0, j)),
        out_shape=jax.ShapeDtypeStruct((T, CONV_WIDTH), F32), compiler_params=_cparams(('parallel',)),
    )(cv, cv, cv, w)


def _conv_bwd_call(cv, w, do, name):
    T = cv.shape[0]

    def body(b_ref, c_ref, x_ref, w_ref, do_ref, db_ref, dc_ref, dx_ref, dw_ref):
        c, x, d = c_ref[...], x_ref[...], do_ref[...]
        u = c * x
        rows = lax.broadcasted_iota(jnp.int32, u.shape, 0)
        u1 = _shift_down(u, rows)
        u2 = _shift_down(u1, rows)
        w0, w1, w2 = w_ref[0:1, :], w_ref[1:2, :], w_ref[2:3, :]
        db_ref[...] = d * (w0 * u2 + w1 * u1 + w2 * u)
        dy = d * b_ref[...]
        dy1 = _shift_up(dy, rows, T)
        dy2 = _shift_up(dy1, rows, T)
        du = w2 * dy + w1 * dy1 + w0 * dy2
        dc_ref[...] = du * x
        dx_ref[...] = du * c
        dw_ref[0:1, :] = jnp.sum(dy * u2, axis=0, keepdims=True)
        dw_ref[1:2, :] = jnp.sum(dy * u1, axis=0, keepdims=True)
        dw_ref[2:3, :] = jnp.sum(dy * u, axis=0, keepdims=True)

    blk = pl.BlockSpec((T, LANES), lambda j: (0, j))
    sh = jax.ShapeDtypeStruct((T, CONV_WIDTH), F32)
    return pl.pallas_call(
        body, name=name + '_b', grid=(CONV_WIDTH // LANES,), in_specs=_conv_specs(T) + [blk],
        out_specs=[blk, blk, blk, pl.BlockSpec((3, LANES), lambda j: (0, j))],
        out_shape=[sh, sh, sh, jax.ShapeDtypeStruct((3, CONV_WIDTH), F32)],
        compiler_params=_cparams(('parallel',)),
    )(cv, cv, cv, w, do)


@functools.partial(jax.custom_vjp, nondiff_argnums=(2,))
def short_conv(cv, w, name):
    return _conv_fwd_call(cv, w, name)


def _sc_fwd(cv, w, name):
    return _conv_fwd_call(cv, w, name), (cv, w)


def _sc_bwd(name, res, do):
    cv, w = res
    db, dc, dx, dw = _conv_bwd_call(cv, w, do, name)
    return jnp.concatenate([db, dc, dx], axis=1), dw


short_conv.defvjp(_sc_fwd, _sc_bwd)


ATT_SCALE = QK_HEAD ** -0.5
NPAIR = MLA_HEADS // 2


def _att_bq(T):
    return min(256, T)


def _att_masks(pair, j):
    lane = lax.broadcasted_iota(jnp.int32, (1, LANES), 1)
    mask_n = (lane // QK_NOPE) == j
    mask_r = (lane // (QK_ROPE // 2)) == (2 * pair + j)
    return mask_n, mask_r


def _att_probs(qcat, kcat, row0, stop):
    s = lax.dot_general(qcat, kcat, (((1,), (1,)), ((), ())), preferred_element_type=F32) * ATT_SCALE
    r = row0 + lax.broadcasted_iota(jnp.int32, s.shape, 0)
    c = lax.broadcasted_iota(jnp.int32, s.shape, 1)
    s = jnp.where(c <= r, s, -jnp.inf)
    e = jnp.exp(s - jnp.max(s, axis=-1, keepdims=True))
    return e / jnp.sum(e, axis=-1, keepdims=True)


def _att_in_specs(T):
    blk = lambda f: pl.BlockSpec((T, LANES), f)
    return [blk(lambda p: (0, p)), blk(lambda p: (0, 0)), blk(lambda p: (0, 0)),
            blk(lambda p: (0, p)), blk(lambda p: (0, 0)), blk(lambda p: (0, 0)), blk(lambda p: (0, p))]


def _att_fwd_call(qn, q1, q2, kn, k1, k2, v, name):
    T = qn.shape[0]
    bq = _att_bq(T)

    def body(qn_ref, q1_ref, q2_ref, kn_ref, k1_ref, k2_ref, v_ref, o_ref):
        pair = pl.program_id(0)
        for i in range(T // bq):
            r0, stop = i * bq, (i + 1) * bq
            kcat = jnp.concatenate([kn_ref[0:stop, :], k1_ref[0:stop, :], k2_ref[0:stop, :]], axis=1).astype(BF16)
            vb = v_ref[0:stop, :].astype(BF16)
            outs = []
            for j in range(2):
                mask_n, mask_r = _att_masks(pair, j)
                qcat = jnp.concatenate([jnp.where(mask_n, qn_ref[r0:stop, :], 0.0),
                                        jnp.where(mask_r, q1_ref[r0:stop, :], 0.0),
                                        jnp.where(mask_r, q2_ref[r0:stop, :], 0.0)], axis=1).astype(BF16)
                p = _att_probs(qcat, kcat, r0, stop)
                outs.append(jnp.dot(p.astype(BF16), vb, preferred_element_type=F32))
            mask_n0, _ = _att_masks(pair, 0)
            o_ref[r0:stop, :] = jnp.where(mask_n0, outs[0], outs[1])

    return pl.pallas_call(
        body, name=name + '_f', grid=(NPAIR,), in_specs=_att_in_specs(T),
        out_specs=pl.BlockSpec((T, LANES), lambda p: (0, p)),
        out_shape=jax.ShapeDtypeStruct((T, MLA_HEADS * V_HEAD), F32), compiler_params=_cparams(('parallel',)),
    )(qn, q1, q2, kn, k1, k2, v)


def _att_bwd_call(qn, q1, q2, kn, k1, k2, v, o, do, name):
    T = qn.shape[0]
    bq = _att_bq(T)

    def body(qn_ref, q1_ref, q2_ref, kn_ref, k1_ref, k2_ref, v_ref, o_ref, do_ref,
             dqn_ref, dq1_ref, dq2_ref, dkn_ref, dk1_ref, dk2_ref, dv_ref, dk_acc, dv_acc):
        pair = pl.program_id(0)

        @pl.when(pair == 0)
        def _():
            dq1_ref[...] = jnp.zeros(dq1_ref.shape, F32)
            dq2_ref[...] = jnp.zeros(dq2_ref.shape, F32)
            dk1_ref[...] = jnp.zeros(dk1_ref.shape, F32)
            dk2_ref[...] = jnp.zeros(dk2_ref.shape, F32)

        dk_acc[...] = jnp.zeros(dk_acc.shape, F32)
        dv_acc[...] = jnp.zeros(dv_acc.shape, F32)
        for i in range(T // bq):
            r0, stop = i * bq, (i + 1) * bq
            kcat = jnp.concatenate([kn_ref[0:stop, :], k1_ref[0:stop, :], k2_ref[0:stop, :]], axis=1).astype(BF16)
            vb = v_ref[0:stop, :].astype(BF16)
            dqn = jnp.zeros((bq, LANES), F32)
            for j in range(2):
                mask_n, mask_r = _att_masks(pair, j)
                qcat = jnp.concatenate([jnp.where(mask_n, qn_ref[r0:stop, :], 0.0),
                                        jnp.where(mask_r, q1_ref[r0:stop, :], 0.0),
                                        jnp.where(mask_r, q2_ref[r0:stop, :], 0.0)], axis=1).astype(BF16)
                p = _att_probs(qcat, kcat, r0, stop)
                dom = jnp.where(mask_n, do_ref[r0:stop, :], 0.0)
                delta = jnp.sum(dom * o_ref[r0:stop, :], axis=-1, keepdims=True)
                domb = dom.astype(BF16)
                dp = lax.dot_general(domb, vb, (((1,), (1,)), ((), ())), preferred_element_type=F32)
                ds = (p * (dp - delta) * ATT_SCALE).astype(BF16)
                dqc = jnp.dot(ds, kcat, preferred_element_type=F32)
                dqn = dqn + jnp.where(mask_n, dqc[:, 0:LANES], 0.0)
                dq1_ref[r0:stop, :] += jnp.where(mask_r, dqc[:, LANES:2 * LANES], 0.0)
                dq2_ref[r0:stop, :] += jnp.where(mask_r, dqc[:, 2 * LANES:3 * LANES], 0.0)
                dk_acc[0:stop, :] += lax.dot_general(ds, qcat, (((0,), (0,)), ((), ())),
                                                     preferred_element_type=F32)
                dv_acc[0:stop, :] += lax.dot_general(p.astype(BF16), domb, (((0,), (0,)), ((), ())),
                                                     preferred_element_type=F32)
            dqn_ref[r0:stop, :] = dqn
        dkn_ref[...] = dk_acc[:, 0:LANES]
        dk1_ref[...] += dk_acc[:, LANES:2 * LANES]
        dk2_ref[...] += dk_acc[:, 2 * LANES:3 * LANES]
        dv_ref[...] = dv_acc[...]

    per_pair = pl.BlockSpec((T, LANES), lambda p: (0, p))
    shared = pl.BlockSpec((T, LANES), lambda p: (0, 0))
    wide = jax.ShapeDtypeStruct((T, MLA_HEADS * QK_NOPE), F32)
    narrow = jax.ShapeDtypeStruct((T, LANES), F32)
    return pl.pallas_call(
        body, name=name + '_b', grid=(NPAIR,), in_specs=_att_in_specs(T) + [per_pair, per_pair],
        out_specs=[per_pair, shared, shared, per_pair, shared, shared, per_pair],
        out_shape=[wide, narrow, narrow, wide, narrow, narrow, wide],
        scratch_shapes=[pltpu.VMEM((T, 3 * LANES), F32), pltpu.VMEM((T, LANES), F32)],
        compiler_params=_cparams(('arbitrary',)),
    )(qn, q1, q2, kn, k1, k2, v, o, do)


@functools.partial(jax.custom_vjp, nondiff_argnums=(7,))
def attention(qn, q1, q2, kn, k1, k2, v, name):
    return _att_fwd_call(qn, q1, q2, kn, k1, k2, v, name)


def _attn_fwd(qn, q1, q2, kn, k1, k2, v, name):
    o = _att_fwd_call(qn, q1, q2, kn, k1, k2, v, name)
    return o, (qn, q1, q2, kn, k1, k2, v, o)


def _attn_bwd(name, res, do):
    return tuple(_att_bwd_call(*res, do, name))


attention.defvjp(_attn_fwd, _attn_bwd)


SCAN_CHUNK = 64
SCAN_UNROLL = 4


def _block_ones(n, seg):
    i = np.arange(n)
    return (i[:, None] // seg == i[None, :] // seg).astype(np.float32)


def _scan_diag():
    i = np.arange(RW_WIDTH)
    return jnp.asarray((np.arange(RW_N)[:, None] == (i[None, :] % RW_N)).astype(np.float32))


def _head_rowsum(x):
    low = lax.broadcasted_iota(jnp.int32, (1, LANES), 1) < RW_N
    tiles = []
    for j in range(RW_WIDTH // LANES):
        xt = x[:, j * LANES:(j + 1) * LANES]
        x0 = jnp.where(low, xt, 0.0)
        s0 = jnp.sum(x0, axis=-1, keepdims=True)
        s1 = jnp.sum(xt - x0, axis=-1, keepdims=True)
        tiles.append(jnp.where(low, s0, s1))
    return jnp.concatenate(tiles, axis=1)


def _unrolled_loop(n, step, init):
    def body(i, carry):
        for j in range(SCAN_UNROLL):
            carry = step(i * SCAN_UNROLL + j, carry)
        return carry
    return lax.fori_loop(0, n // SCAN_UNROLL, body, init)


def _scan_fwd_call(r, w, k, v, a, b, name):
    T = r.shape[0]
    tc = min(SCAN_CHUNK, T)
    dg = _scan_diag()

    def body(r_ref, w_ref, k_ref, v_ref, a_ref, b_ref, dg_ref, y_ref, st_ref, s_ref):
        @pl.when(pl.program_id(0) == 0)
        def _():
            s_ref[...] = jnp.zeros(s_ref.shape, F32)

        dgv = dg_ref[...]
        readout = lambda s, t: jnp.sum(_head_rowsum(s * r_ref[t]) * dgv, axis=0, keepdims=True)

        def step(t, carry):
            s, vcol = carry
            st_ref[t] = s
            sa = _head_rowsum(s * a_ref[t])
            prev = jnp.maximum(t - 1, 0)
            y_ref[prev] = readout(s, prev)
            vcol_next = _head_rowsum(dgv * v_ref[jnp.minimum(t + 1, tc - 1)])
            sn = s * w_ref[t] + sa * b_ref[t] + vcol * k_ref[t]
            return sn, vcol_next

        s_end, _ = _unrolled_loop(tc, step, (s_ref[...], _head_rowsum(dgv * v_ref[0])))
        y_ref[tc - 1] = readout(s_end, tc - 1)
        s_ref[...] = s_end

    vec = pl.BlockSpec((tc, 1, RW_WIDTH), lambda i: (i, 0, 0))
    return pl.pallas_call(
        body, name=name + '_f', grid=(T // tc,),
        in_specs=[vec] * 6 + [pl.BlockSpec((RW_N, RW_WIDTH), lambda i: (0, 0))],
        out_specs=[vec, pl.BlockSpec((tc, RW_N, RW_WIDTH), lambda i: (i, 0, 0))],
        out_shape=[jax.ShapeDtypeStruct((T, 1, RW_WIDTH), F32), jax.ShapeDtypeStruct((T, RW_N, RW_WIDTH), F32)],
        scratch_shapes=[pltpu.VMEM((RW_N, RW_WIDTH), F32)],
        compiler_params=_cparams(('arbitrary',)),
    )(r, w, k, v, a, b, dg)


def _scan_bwd_call(r, w, k, v, a, b, st, dy, name):
    T = r.shape[0]
    tc = min(SCAN_CHUNK, T)
    nt = T // tc
    dg = _scan_diag()

    def body(r_ref, w_ref, k_ref, v_ref, a_ref, b_ref, st_ref, dy_ref, dg_ref,
             dr_ref, dw_ref, dk_ref, dv_ref, da_ref, db_ref, ds_ref):
        @pl.when(pl.program_id(0) == 0)
        def _():
            ds_ref[...] = jnp.zeros(ds_ref.shape, F32)

        dgv = dg_ref[...]
        colsum = lambda x: jnp.sum(x, axis=0, keepdims=True)

        def step(i, carry):
            ds, dycol = carry
            t = tc - 1 - i
            sp = st_ref[t]
            rt, wt, kt, at, bt = r_ref[t], w_ref[t], k_ref[t], a_ref[t], b_ref[t]
            ds = ds + dycol * rt
            dsa = _head_rowsum(ds * bt)
            sa = _head_rowsum(sp * at)
            vcol = _head_rowsum(dgv * v_ref[t])
            dycol_next = _head_rowsum(dgv * dy_ref[jnp.maximum(t - 1, 0)])
            sn = sp * wt + sa * bt + vcol * kt
            dr_ref[t] = colsum(sn * dycol)
            dk_ref[t] = colsum(ds * vcol)
            db_ref[t] = colsum(ds * sa)
            dw_ref[t] = colsum(ds * sp)
            dv_ref[t] = colsum(_head_rowsum(ds * kt) * dgv)
            da_ref[t] = colsum(sp * dsa)
            return ds * wt + dsa * at, dycol_next

        ds_end, _ = _unrolled_loop(tc, step, (ds_ref[...], _head_rowsum(dgv * dy_ref[tc - 1])))
        ds_ref[...] = ds_end

    vec = pl.BlockSpec((tc, 1, RW_WIDTH), lambda i: (nt - 1 - i, 0, 0))
    vsh = jax.ShapeDtypeStruct((T, 1, RW_WIDTH), F32)
    return pl.pallas_call(
        body, name=name + '_b', grid=(nt,),
        in_specs=[vec] * 6 + [pl.BlockSpec((tc, RW_N, RW_WIDTH), lambda i: (nt - 1 - i, 0, 0)), vec,
                              pl.BlockSpec((RW_N, RW_WIDTH), lambda i: (0, 0))],
        out_specs=[vec] * 6, out_shape=[vsh] * 6,
        scratch_shapes=[pltpu.VMEM((RW_N, RW_WIDTH), F32)],
        compiler_params=_cparams(('arbitrary',)),
    )(r, w, k, v, a, b, st, dy, dg)


@functools.partial(jax.custom_vjp, nondiff_argnums=(6,))
def wkv7(r, w, k, v, a, b, name):
    return _scan_fwd_call(r, w, k, v, a, b, name)[0]


def _wkv7_fwd(r, w, k, v, a, b, name):
    y, st = _scan_fwd_call(r, w, k, v, a, b, name)
    return y, (r, w, k, v, a, b, st)


def _wkv7_bwd(name, res, dy):
    return tuple(_scan_bwd_call(*res, dy, name))


wkv7.defvjp(_wkv7_fwd, _wkv7_bwd)


def _np_bf16(a):
    return jnp.asarray(a, BF16)


def _mla_consts():
    seg_n = (np.arange(512)[:, None] // QK_NOPE == np.arange(LANES)[None, :]).astype(np.float32)
    seg_r = (np.arange(LANES)[:, None] // 16 == np.arange(LANES)[None, :]).astype(np.float32)
    e1 = np.zeros((LANES, LANES), np.float32)
    e2 = np.zeros((LANES, LANES), np.float32)
    for h in range(MLA_HEADS):
        for i in range(16):
            e1[i, h * 16 + i] = 1.0
            e2[16 + i, h * 16 + i] = 1.0
    mats = [seg_n, seg_n.T, seg_r, seg_r.T, e1, e1.T, e2, e2.T]
    return [_np_bf16(m) for m in mats]


def _qk_prep_fn(qn, q1, q2, kn, kx, cos, sin, gqn, gq1, gq2, gkn, gk1, gk2,
                seg_n, seg_nt, seg_r, seg_rt, e1, e1t, e2, e2t):
    def normrope(xn, x1, x2, gn, g1, g2):
        ss = sdot(xn * xn, seg_n, seg_nt) + sdot(x1 * x1, seg_r, seg_rt) + sdot(x2 * x2, seg_r, seg_rt)
        inv = lax.rsqrt(ss * (1.0 / QK_HEAD) + NORM_EPS)
        inv_n = sdot(inv, seg_nt, seg_n)
        inv_r = sdot(inv, seg_rt, seg_r)
        y1 = x1 * inv_r * g1
        y2 = x2 * inv_r * g2
        return xn * inv_n * gn, y1 * cos - y2 * sin, y1 * sin + y2 * cos

    k1 = sdot(kx, e1, e1t)
    k2 = sdot(kx, e2, e2t)
    return normrope(qn, q1, q2, gqn, gq1, gq2) + normrope(kn, k1, k2, gkn, gk1, gk2)


def _rwkv_prep_fn(vres):
    def fn(r, k, v, xg, xwa, kx, *rest):
        if vres:
            vfirst, w0, a0, k_k, k_a, w2p, a2p, g2, v0, v2p, bm = rest
        else:
            w0, a0, k_k, k_a, w2p, a2p, g2, bm = rest
        z = w0 + bdot(jnp.tanh(xwa), w2p)
        nz = -z
        softplus = jnp.maximum(nz, 0.0) + jnp.log(1.0 + jnp.exp(-jnp.abs(nz)))
        decay = jnp.exp(-jnp.exp(-softplus - 0.5))
        a = _sigmoid(a0 + bdot(xwa, a2p))
        g = bdot(_sigmoid(xg), g2)
        if vres:
            vv = v + (vfirst - v) * _sigmoid(v0 + bdot(kx, v2p))
        else:
            vv = v
        kkr = k * k_k
        kk = kkr / jnp.maximum(jnp.sqrt(sdot(kkr * kkr, bm, bm)), 1e-12)
        k2 = k * (1.0 + (a - 1.0) * k_a)
        return r * 1.0, decay, k2, vv, -kk, kk * a, g
    return fn


def _rwkv_post_fn(y, r, k2, vv, g, ln_w, ln_b, rk, bm):
    inv_n = 1.0 / RW_N
    mean = sdot(y, bm, bm) * inv_n
    yc = y - mean
    var = sdot(yc * yc, bm, bm) * inv_n
    yn = yc * lax.rsqrt(var + GN_EPS) * ln_w + ln_b
    bonus = sdot(r * k2 * rk, bm, bm) * vv
    return ((yn + bonus) * g,)


def _merge_fn(g0, g1, g2, oa, ob, oc):
    return (_sigmoid(g0) * oa + _sigmoid(g1) * ob + _sigmoid(g2) * oc,)


def _relu2_fn(u):
    r = jnp.maximum(u, 0.0)
    return (r * r,)


def _loss_call(y, target):
    T, C = y.shape
    tb = min(256, T)

    def body(y_ref, t_ref, dy_ref, part_ref):
        err = y_ref[...] - t_ref[...]
        dy_ref[...] = err * (1.0 / C)
        sq = jnp.sum(err * err, axis=0, keepdims=True)
        acc = sq[:, 0:LANES]
        for j in range(1, C // LANES):
            acc = acc + sq[:, j * LANES:(j + 1) * LANES]
        part_ref[...] = jnp.zeros(part_ref.shape, F32)
        part_ref[0:1, :] = acc * (0.5 / C)

    return pl.pallas_call(
        body, name='loss', grid=(T // tb,),
        in_specs=[pl.BlockSpec((tb, C), lambda i: (i, 0))] * 2,
        out_specs=[pl.BlockSpec((tb, C), lambda i: (i, 0)), pl.BlockSpec((8, LANES), lambda i: (i, 0))],
        out_shape=[jax.ShapeDtypeStruct((T, C), F32), jax.ShapeDtypeStruct((8 * (T // tb), LANES), F32)],
        compiler_params=_cparams(('parallel',)),
    )(y, target)


def _cols(w, lo, hi):
    return w.map(lambda t: t[:, lo:hi])


def _pad_rows(t, before, total):
    return jnp.pad(t, ((before, total - before - t.shape[0]), (0, 0)))


def _head_tile(g, lo, hi):
    return jnp.tile(g[lo:hi], MLA_HEADS).reshape(1, -1)


def _layer(l, x, v_first, wd, sp, cos, sin):
    T = x.shape[0]
    nm = f'l{l}'
    vres = l > 0
    w_in = wd['w_in']
    if vres:
        v1t = wd['rwkv_v1'].map(lambda t: t.T)
    else:
        v1t = W(jnp.zeros((MV_LORA, D_MODEL), BF16), jnp.zeros((MV_LORA, D_MODEL), F32))
    zpad = W(jnp.zeros((64, D_MODEL), BF16), jnp.zeros((64, D_MODEL), F32))
    w_rw = _wcat([w_in['rkv'], w_in['xg'], w_in['xwa'], w_in['kpe'], v1t, zpad], 0)
    h = rmsnorm(x, sp['attn_norm'], nm + '_anorm')
    gates, cq, ckv, rw, cv = multi_linear_t(
        h, [w_in['gates'], w_in['cq'], w_in['ckv'], w_rw, w_in['conv']], nm + '_win')

    v_mu = sp['rwkv_v_mu'] if vres else jnp.zeros((MV_LORA,), F32)
    mu_all = jnp.concatenate([sp['rwkv_mu'][0:768], sp['rwkv_mu'][896:1024], sp['rwkv_mu'][768:896],
                              jnp.zeros((QK_ROPE,), F32), v_mu, jnp.zeros((64,), F32)]).reshape(1, -1)
    rws = token_shift_mix(rw, mu_all, nm + '_shift')

    cqn = rmsnorm(cq, sp['mla_q_a_norm'], nm + '_qan')
    ckvn = rmsnorm(ckv, sp['mla_kv_a_norm'], nm + '_kvan')
    wq = wd['mla_wq_b'].map(lambda t: jnp.concatenate(
        [t.reshape(Q_LORA, MLA_HEADS, QK_HEAD)[:, :, 0:64].reshape(Q_LORA, 512),
         t.reshape(Q_LORA, MLA_HEADS, QK_HEAD)[:, :, 64:80].reshape(Q_LORA, 128),
         t.reshape(Q_LORA, MLA_HEADS, QK_HEAD)[:, :, 80:96].reshape(Q_LORA, 128)], axis=1))
    wkn = wd['mla_wkv_b'].map(lambda t: t.reshape(KV_LORA, MLA_HEADS, 128)[:, :, 0:64].reshape(KV_LORA, 512))
    wv = wd['mla_wkv_b'].map(lambda t: t.reshape(KV_LORA, MLA_HEADS, 128)[:, :, 64:128].reshape(KV_LORA, 512))
    q = linear(cqn, wq, name=nm + '_wq')
    kn, vv_att = multi_linear(ckvn, [wkn, wv], nm + '_wkv')
    gq, gk = sp['mla_q_norm'], sp['mla_k_norm']
    consts = _mla_consts()
    qk_specs = ([ROW(pieces=((0, 512), (512, 128), (640, 128))), ROW(), ROW(pieces=((1024, 128),)),
                 ROW(False), ROW(False)] + [FULL()] * 6 + [FULL(False)] * 8)
    qk_op = stage_op(_qk_prep_fn, qk_specs, [512, 128, 128, 512, 128, 128], nm + '_qkprep')
    Qn, Q1, Q2, Kn, K1, K2 = qk_op(q, kn, rws, cos, sin,
                                   _head_tile(gq, 0, 64), _head_tile(gq, 64, 80), _head_tile(gq, 80, 96),
                                   _head_tile(gk, 0, 64), _head_tile(gk, 64, 80), _head_tile(gk, 80, 96), *consts)
    o_att = attention(Qn, Q1, Q2, Kn, K1, K2, vv_att, nm + '_att')
    o_a = linear(o_att, wd['mla_w_o'], name=nm + '_wo')

    bm = _np_bf16(_block_ones(RW_WIDTH, RW_N))
    vec = lambda n: sp[n].reshape(1, -1)
    f32w = lambda n: wd[n].c + wd[n].b.astype(F32)
    w2p = _pad_rows(f32w('rwkv_w2'), 0, 128)
    a2p = _pad_rows(f32w('rwkv_a2'), 64, 128)
    g2 = f32w('rwkv_g2')
    rw_pieces = ((0, 256), (256, 256), (512, 256), (768, 128), (896, 128), (1024, 128))
    if vres:
        v2p = _pad_rows(f32w('rwkv_v2'), 32, 128)
        prep_specs = [ROW(pieces=rw_pieces), ROW()] + [FULL()] * 9 + [FULL(False)]
        prep_in = [rws, v_first, vec('rwkv_w0'), vec('rwkv_a0'), vec('rwkv_k_k'), vec('rwkv_k_a'), w2p, a2p, g2,
                   vec('rwkv_v0'), v2p, bm]
    else:
        prep_specs = [ROW(pieces=rw_pieces)] + [FULL()] * 7 + [FULL(False)]
        prep_in = [rws, vec('rwkv_w0'), vec('rwkv_a0'), vec('rwkv_k_k'), vec('rwkv_k_a'), w2p, a2p, g2, bm]
    prep_op = stage_op(_rwkv_prep_fn(vres), prep_specs, [256] * 7, nm + '_rwprep')
    r_, dec, k2, vv, an, bn, g = prep_op(*prep_in)
    if not vres:
        v_first = vv
    t3 = lambda t: t.reshape(T, 1, RW_WIDTH)
    y = wkv7(t3(r_), t3(dec), t3(k2), t3(vv), t3(an), t3(bn), nm + '_scan').reshape(T, RW_WIDTH)
    post_op = stage_op(_rwkv_post_fn, [ROW()] * 5 + [FULL()] * 3 + [FULL(False)], [256], nm + '_rwpost')
    yb = post_op(y, r_, k2, vv, g, vec('rwkv_ln_w'), vec('rwkv_ln_b'), sp['rwkv_r_k'].reshape(1, -1), bm)[0]
    o_b = linear(yb, wd['rwkv_w_o'], name=nm + '_rwo')

    oc_in = short_conv(cv, f32w('conv_w'), nm + '_conv')
    o_c = linear(oc_in, wd['conv_w_o'], name=nm + '_cwo')

    merge_op = stage_op(_merge_fn, [ROW(pieces=((0, 1024), (1024, 1024), (2048, 1024))), ROW(), ROW(), ROW()],
                        [D_MODEL], nm + '_merge')
    merged = merge_op(gates, o_a, o_b, o_c)[0]
    x2 = linear(merged, wd['w_out'], add=x, name=nm + '_wout')
    h2 = rmsnorm(x2, sp['mlp_norm'], nm + '_mnorm')
    u = linear(h2, wd['w_up'], name=nm + '_wup')
    act = stage_op(_relu2_fn, [ROW()], [D_FF], nm + '_relu2')(u)[0]
    x3 = linear(act, wd['w_down'], add=x2, name=nm + '_wdown')
    return x3, v_first


def _entries():
    out = []
    for name, (layers, shape, axis) in SHARDED.items():
        n = shape[0] * shape[1] // N_DEV
        for l in range(layers):
            if name == 'conv_w':
                out.append(('conv_w_hi', name, l, n))
                out.append(('conv_w_lo', name, l, n))
            else:
                out.append((name, name, l, n))
    return out


def _slot_size(n):
    return -(-n // LANES) * LANES


def _pack_rows():
    total = sum(_slot_size(n) for _, _, _, n in _entries())
    rows = -(-total // LANES)
    return -(-rows // PACK_ROW_MULT) * PACK_ROW_MULT


def _pack_flat(pieces):
    rows = _pack_rows()
    padded = []
    for p, (_, _, _, n) in zip(pieces, _entries()):
        pad = _slot_size(n) - n
        if pad:
            p = jnp.pad(p, [(0, 0)] * (p.ndim - 1) + [(0, pad)])
        padded.append(p)
    flat = jnp.concatenate(padded, axis=-1)
    tail = rows * LANES - flat.shape[-1]
    if tail:
        flat = jnp.pad(flat, [(0, 0)] * (flat.ndim - 1) + [(0, tail)])
    return flat.reshape(flat.shape[:-1] + (rows, LANES))


def _unpack_flat(buf):
    flat = buf.reshape(buf.shape[:-2] + (-1,))
    out, off = [], 0
    for _, _, _, n in _entries():
        out.append(flat[..., off:off + n])
        off += _slot_size(n)
    return out


def _pack_shards(shards, dtype, split_conv):
    pieces = []
    for slot, name, l, n in _entries():
        a = shards[name][l]
        a = (a.T if name in TRANSPOSED else a).reshape(-1)
        if slot == 'conv_w_hi':
            a = a.astype(BF16).astype(F32) if split_conv else a
        elif slot == 'conv_w_lo':
            a = (a - a.astype(BF16).astype(F32)) if split_conv else jnp.zeros_like(a)
        pieces.append(a.astype(dtype))
    return _pack_flat(pieces)


def _unpack_shards(buf):
    vals = _unpack_flat(buf)
    out = {}
    for (slot, name, l, n), v in zip(_entries(), vals):
        if slot == 'conv_w_lo':
            continue
        layers, shape, axis = SHARDED[name]
        sshape = (shape[0] // N_DEV, shape[1]) if axis == 0 else (shape[0], shape[1] // N_DEV)
        v = v.reshape(sshape[::-1]).T if name in TRANSPOSED else v.reshape(sshape)
        out.setdefault(name, []).append(v)
    return {k: jnp.stack(v) for k, v in out.items()}


def _to_full(blocks, shape, axis):
    if axis == 0:
        return blocks.reshape(shape)
    return blocks.reshape(N_DEV, shape[0], shape[1] // N_DEV).transpose(1, 0, 2).reshape(shape)


def _to_blocks(full, axis):
    r, c = full.shape
    if axis == 0:
        return full.reshape(N_DEV, -1)
    return full.reshape(r, N_DEV, c // N_DEV).transpose(1, 0, 2).reshape(N_DEV, -1)


def _unpack_gathered(gathered):
    vals = _unpack_flat(gathered)
    out, conv_hi = {}, {}
    for (slot, name, l, n), v in zip(_entries(), vals):
        layers, shape, axis = SHARDED[name]
        if name in TRANSPOSED:
            out[(name, l)] = v.reshape(-1, 16, LANES)
            continue
        full = _to_full(v, shape, axis)
        if slot == 'conv_w_hi':
            conv_hi[l] = full
        elif slot == 'conv_w_lo':
            out[(name, l)] = conv_hi[l].astype(F32) + full.astype(F32)
        else:
            out[(name, l)] = full
    return out


def _pack_grads(grads):
    pieces = []
    for slot, name, l, n in _entries():
        if name in TRANSPOSED:
            blocks = jnp.concatenate(grads[(name, l)], axis=0).reshape(N_DEV, -1)
        else:
            blocks = _to_blocks(grads[(name, l)], SHARDED[name][2])
        if slot == 'conv_w_lo':
            blocks = jnp.zeros_like(blocks)
        pieces.append(blocks)
    return _pack_flat(pieces)


def _my_pos():
    return lax.axis_index('x'), lax.axis_index('y'), lax.axis_index('c')


def _flip(v, bit):
    return 1 - v if bit else v


def all_gather_blocks(x):
    rows = x.shape[0]

    def body(x_ref, out_ref, send_sems, recv_sems, local_sem):
        mx, my, mc = _my_pos()
        me, sibling = (mx, my, mc), (mx, my, 1 - mc)
        chips = [(1 - mx, my), (mx, 1 - my), (1 - mx, 1 - my)]

        def block(px, py, pc):
            return out_ref.at[4 * px + 2 * py + pc]

        def copy(k, blk, to, src=None):
            return pltpu.make_async_remote_copy(
                src_ref=block(*blk) if src is None else src, dst_ref=block(*blk),
                send_sem=send_sems.at[k], recv_sem=recv_sems.at[k], device_id=to, device_id_type=MESH)

        mine = pltpu.make_async_copy(x_ref, block(*me), local_sem)
        mine.start()
        first = [copy(0, me, sibling, src=x_ref)]
        first += [copy(1 + j, me, (*chip, mc), src=x_ref) for j, chip in enumerate(chips)]
        for cp in first:
            cp.start()
        passed = [copy(4 + j, (*chip, mc), sibling) for j, chip in enumerate(chips)]
        for j, chip in enumerate(chips):
            copy(1 + j, (*chip, mc), me).wait_recv()
            passed[j].start()
        copy(0, sibling, me).wait_recv()
        for j, chip in enumerate(chips):
            copy(4 + j, (*chip, 1 - mc), me).wait_recv()
        for cp in first + passed:
            cp.wait_send()
        mine.wait()

    return pl.pallas_call(
        body, name='all_gather_weights',
        out_shape=jax.ShapeDtypeStruct((N_DEV, rows, LANES), x.dtype),
        in_specs=[pl.BlockSpec(memory_space=pl.ANY)], out_specs=pl.BlockSpec(memory_space=pl.ANY),
        scratch_shapes=[pltpu.SemaphoreType.DMA((7,)), pltpu.SemaphoreType.DMA((7,)), pltpu.SemaphoreType.DMA],
    )(x)


def exchange_blocks(g):
    def body(g_ref, out_ref, send_sems, recv_sems, local_sem):
        mx, my, mc = _my_pos()
        me = 4 * mx + 2 * my + mc
        mine = pltpu.make_async_copy(g_ref.at[me], out_ref.at[me], local_sem)
        mine.start()
        copies, arrivals = [], []
        for k in range(1, N_DEV):
            peer = (_flip(mx, k & 4), _flip(my, k & 2), _flip(mc, k & 1))
            pidx = 4 * peer[0] + 2 * peer[1] + peer[2]
            copies.append(pltpu.make_async_remote_copy(
                src_ref=g_ref.at[pidx], dst_ref=out_ref.at[me], send_sem=send_sems.at[k - 1],
                recv_sem=recv_sems.at[k - 1], device_id=peer, device_id_type=MESH))
            arrivals.append(pltpu.make_async_remote_copy(
                src_ref=g_ref.at[me], dst_ref=out_ref.at[pidx], send_sem=send_sems.at[k - 1],
                recv_sem=recv_sems.at[k - 1], device_id=peer, device_id_type=MESH))
        for cp in copies:
            cp.start()
        for cp in arrivals:
            cp.wait_recv()
        for cp in copies:
            cp.wait_send()
        mine.wait()

    return pl.pallas_call(
        body, name='exchange_grads',
        out_shape=jax.ShapeDtypeStruct(g.shape, g.dtype),
        in_specs=[pl.BlockSpec(memory_space=pl.ANY)], out_specs=pl.BlockSpec(memory_space=pl.ANY),
        scratch_shapes=[pltpu.SemaphoreType.DMA((7,)), pltpu.SemaphoreType.DMA((7,)), pltpu.SemaphoreType.DMA],
    )(g)


def _adamw_math(w, g, m, v):
    m2 = ADAM_B1 * m + (1.0 - ADAM_B1) * g
    v2 = ADAM_B2 * v + (1.0 - ADAM_B2) * (g * g)
    m_hat = m2 / (1.0 - ADAM_B1 ** ADAM_STEP)
    v_hat = v2 / (1.0 - ADAM_B2 ** ADAM_STEP)
    delta = -ADAM_LR * (m_hat / (jnp.sqrt(v_hat) + ADAM_EPS) + ADAM_WD * w)
    return delta, m2, v2


def adamw_sharded(parts, w, m, v):
    rows = w.shape[0]
    rb = PACK_ROW_MULT

    def body(p_ref, w_ref, m_ref, v_ref, g_ref, d_ref, m2_ref, v2_ref):
        g = p_ref[0].astype(F32)
        for j in range(1, N_DEV):
            g = g + p_ref[j].astype(F32)
        delta, m2, v2 = _adamw_math(w_ref[...], g, m_ref[...], v_ref[...])
        g_ref[...] = g
        d_ref[...] = delta
        m2_ref[...] = m2
        v2_ref[...] = v2

    blk = pl.BlockSpec((rb, LANES), lambda i: (i, 0))
    sh = jax.ShapeDtypeStruct((rows, LANES), F32)
    return pl.pallas_call(
        body, name='adamw_sharded', grid=(rows // rb,),
        in_specs=[pl.BlockSpec((N_DEV, rb, LANES), lambda i: (0, i, 0)), blk, blk, blk],
        out_specs=[blk] * 4, out_shape=[sh] * 4, compiler_params=_cparams(('parallel',)),
    )(parts, w, m, v)


def allreduce_adamw_small(g, w, m, v):
    rows = g.shape[0]

    def body(g_ref, w_ref, m_ref, v_ref, gs_ref, d_ref, m2_ref, v2_ref, all_ref, send_sems, recv_sems):
        mx, my, mc = _my_pos()
        me, sibling = (mx, my, mc), (mx, my, 1 - mc)
        chips = [(1 - mx, my), (mx, 1 - my), (1 - mx, 1 - my)]

        def block(px, py, pc):
            return all_ref.at[4 * px + 2 * py + pc]

        def copy(k, blk, to, src=None):
            return pltpu.make_async_remote_copy(
                src_ref=block(*blk) if src is None else src, dst_ref=block(*blk),
                send_sem=send_sems.at[k], recv_sem=recv_sems.at[k], device_id=to, device_id_type=MESH)

        first = [copy(0, me, sibling, src=g_ref)]
        first += [copy(1 + j, me, (*chip, mc), src=g_ref) for j, chip in enumerate(chips)]
        for cp in first:
            cp.start()
        passed = [copy(4 + j, (*chip, mc), sibling) for j, chip in enumerate(chips)]
        for j, chip in enumerate(chips):
            copy(1 + j, (*chip, mc), me).wait_recv()
            passed[j].start()
        copy(0, sibling, me).wait_recv()
        for j, chip in enumerate(chips):
            copy(4 + j, (*chip, 1 - mc), me).wait_recv()
        for cp in first + passed:
            cp.wait_send()
        my_idx = 4 * mx + 2 * my + mc
        total = jnp.zeros((rows, LANES), F32)
        for j in range(N_DEV):
            total = total + jnp.where(my_idx == j, g_ref[...], all_ref[j])
        delta, m2, v2 = _adamw_math(w_ref[...], total, m_ref[...], v_ref[...])
        gs_ref[...] = total
        d_ref[...] = delta
        m2_ref[...] = m2
        v2_ref[...] = v2

    vm = pl.BlockSpec(memory_space=pltpu.VMEM)
    sh = jax.ShapeDtypeStruct((rows, LANES), F32)
    return pl.pallas_call(
        body, name='allreduce_adamw_small', in_specs=[vm] * 4, out_specs=[vm] * 4, out_shape=[sh] * 4,
        scratch_shapes=[pltpu.VMEM((N_DEV, rows, LANES), F32), pltpu.SemaphoreType.DMA((7,)),
                        pltpu.SemaphoreType.DMA((7,))],
    )(g, w, m, v)


def _small_pack(d):
    flat = jnp.concatenate([d[n].reshape(-1) for n in SMALL_NAMES])
    rows = -(-flat.shape[0] // (8 * LANES)) * 8
    return jnp.pad(flat, (0, rows * LANES - flat.shape[0])).reshape(rows, LANES)


def _small_unpack(buf, like):
    flat = buf.reshape(-1)
    out, off = {}, 0
    for n in SMALL_NAMES:
        sz = int(np.prod(like[n].shape))
        out[n] = flat[off:off + sz].reshape(like[n].shape)
        off += sz
    return out


def _rope_tables(positions):
    freqs = ROPE_THETA ** (-(jnp.arange(QK_ROPE // 2, dtype=F32) * 2.0 / QK_ROPE))
    ang = positions.astype(F32)[:, None] * freqs
    return jnp.tile(jnp.cos(ang), (1, MLA_HEADS)), jnp.tile(jnp.sin(ang), (1, MLA_HEADS))


def _forward(carriers, small, x, gathered, cos, sin):
    v_first = None
    for l in range(DEPTH):
        wd = {}
        for name, (layers, _, _) in SHARDED.items():
            ll = l if layers == DEPTH else l - 1
            if not 0 <= ll < layers:
                continue
            if name == 'w_in':
                tiles = gathered[(name, ll)]
                wd[name] = {seg: W(tiles[lo // 2:hi // 2].reshape(hi - lo, D_MODEL), c)
                            for (seg, lo, hi), c in zip(WIN_SEGS, carriers[(name, ll)])}
            else:
                wd[name] = W(gathered[(name, ll)], carriers[(name, ll)])
        sp = {}
        for n in SMALL_NAMES:
            ll = l if small[n].shape[0] == DEPTH else l - 1
            if 0 <= ll < small[n].shape[0]:
                sp[n] = small[n][ll]
        x, v_first = _layer(l, x, v_first, wd, sp, cos, sin)
    return x


def _train_step(x, positions, loss_target, weights, moms_m, moms_v):
    shards = {n: weights[n] for n in SHARDED}
    small = {n: weights[n] for n in SMALL_NAMES}

    gathered = _unpack_gathered(all_gather_blocks(_pack_shards(shards, BF16, True)))
    carriers = {}
    for (name, l), full in gathered.items():
        if name == 'w_in':
            carriers[(name, l)] = tuple(jnp.zeros((hi - lo, D_MODEL), F32) for _, lo, hi in WIN_SEGS)
            continue
        carriers[(name, l)] = jnp.zeros(SHARDED[name][1], F32)
        if name == 'conv_w':
            gathered[(name, l)] = full.astype(BF16)
            carriers[(name, l)] = full - full.astype(BF16).astype(F32)
    cos, sin = _rope_tables(positions[0])

    y, vjp = jax.vjp(lambda c, s, xx: _forward(c, s, xx, gathered, cos, sin), carriers, small, x[0])
    dy, parts = _loss_call(y, loss_target[0])
    loss = lax.psum(jnp.sum(parts), ('x', 'y', 'c'))
    g_full, g_small, g_x = vjp(dy)

    received = exchange_blocks(_pack_grads(g_full).astype(BF16))
    g_sh, d_sh, m_sh, v_sh = (_unpack_shards(b) for b in adamw_sharded(
        received, _pack_shards(shards, F32, False), _pack_shards({n: moms_m[n] for n in SHARDED}, F32, False),
        _pack_shards({n: moms_v[n] for n in SHARDED}, F32, False)))
    g_sm, d_sm, m_sm, v_sm = (_small_unpack(b, small) for b in allreduce_adamw_small(
        _small_pack(g_small), _small_pack(small), _small_pack({n: moms_m[n] for n in SMALL_NAMES}),
        _small_pack({n: moms_v[n] for n in SMALL_NAMES})))

    pick = lambda sh, sm: [sh[n] if n in SHARDED else sm[n] for n in WEIGHT_NAMES]
    return (loss, g_x[None], *pick(g_sh, g_sm), *pick(d_sh, d_sm), *pick(m_sh, m_sm), *pick(v_sh, v_sm))


def kernel(x, positions, attn_norm, w_in, mla_q_a_norm, mla_wq_b, mla_kv_a_norm, mla_wkv_b, mla_q_norm, mla_k_norm, mla_w_o, rwkv_mu, rwkv_w0, rwkv_w2, rwkv_a0, rwkv_a2, rwkv_g2, rwkv_k_k, rwkv_k_a, rwkv_r_k, rwkv_ln_w, rwkv_ln_b, rwkv_w_o, rwkv_v1, rwkv_v_mu, rwkv_v0, rwkv_v2, conv_w, conv_w_o, w_out, mlp_norm, w_up, w_down, loss_target, m_attn_norm, m_w_in, m_mla_q_a_norm, m_mla_wq_b, m_mla_kv_a_norm, m_mla_wkv_b, m_mla_q_norm, m_mla_k_norm, m_mla_w_o, m_rwkv_mu, m_rwkv_w0, m_rwkv_w2, m_rwkv_a0, m_rwkv_a2, m_rwkv_g2, m_rwkv_k_k, m_rwkv_k_a, m_rwkv_r_k, m_rwkv_ln_w, m_rwkv_ln_b, m_rwkv_w_o, m_rwkv_v1, m_rwkv_v_mu, m_rwkv_v0, m_rwkv_v2, m_conv_w, m_conv_w_o, m_w_out, m_mlp_norm, m_w_up, m_w_down, v_attn_norm, v_w_in, v_mla_q_a_norm, v_mla_wq_b, v_mla_kv_a_norm, v_mla_wkv_b, v_mla_q_norm, v_mla_k_norm, v_mla_w_o, v_rwkv_mu, v_rwkv_w0, v_rwkv_w2, v_rwkv_a0, v_rwkv_a2, v_rwkv_g2, v_rwkv_k_k, v_rwkv_k_a, v_rwkv_r_k, v_rwkv_ln_w, v_rwkv_ln_b, v_rwkv_w_o, v_rwkv_v1, v_rwkv_v_mu, v_rwkv_v0, v_rwkv_v2, v_conv_w, v_conv_w_o, v_w_out, v_mlp_norm, v_w_up, v_w_down):
    args = locals()
    weights = {n: args[n] for n in WEIGHT_NAMES}
    moms_m = {n: args['m_' + n] for n in WEIGHT_NAMES}
    moms_v = {n: args['v_' + n] for n in WEIGHT_NAMES}
    return _train_step(x, positions, loss_target, weights, moms_m, moms_v)
```

```python
import functools

import numpy as np
import jax
import jax.numpy as jnp
from jax import lax
from jax.experimental import pallas as pl
from jax.experimental.pallas import tpu as pltpu

F32 = jnp.float32
BF16 = jnp.bfloat16

N_DEV = 8
LANES = 128
D_MODEL = 1024
DEPTH = 2
MLA_HEADS = 8
QK_NOPE = 64
QK_ROPE = 32
QK_HEAD = QK_NOPE + QK_ROPE
V_HEAD = 64
Q_LORA = 384
KV_LORA = 256
ROPE_THETA = 10000.0
RW_HEADS = 4
RW_N = 64
RW_WIDTH = RW_HEADS * RW_N
MV_LORA = 32
GN_EPS = 64e-5
CONV_WIDTH = 256
D_FF = 4 * D_MODEL
NORM_EPS = 1e-6
ADAM_LR = 0.001
ADAM_B1 = 0.9
ADAM_B2 = 0.999
ADAM_EPS = 1e-08
ADAM_WD = 0.01
ADAM_STEP = 10

VMEM_LIMIT = 56 * 1024 * 1024
MESH = pl.DeviceIdType.MESH

WEIGHT_NAMES = ['attn_norm', 'w_in', 'mla_q_a_norm', 'mla_wq_b', 'mla_kv_a_norm', 'mla_wkv_b', 'mla_q_norm',
                'mla_k_norm', 'mla_w_o', 'rwkv_mu', 'rwkv_w0', 'rwkv_w2', 'rwkv_a0', 'rwkv_a2', 'rwkv_g2',
                'rwkv_k_k', 'rwkv_k_a', 'rwkv_r_k', 'rwkv_ln_w', 'rwkv_ln_b', 'rwkv_w_o', 'rwkv_v1',
                'rwkv_v_mu', 'rwkv_v0', 'rwkv_v2', 'conv_w', 'conv_w_o', 'w_out', 'mlp_norm', 'w_up', 'w_down']

SHARDED = {
    'w_in': (2, (1024, 5536), 1), 'mla_wq_b': (2, (384, 768), 1), 'mla_wkv_b': (2, (256, 1024), 1),
    'mla_w_o': (2, (512, 1024), 1), 'rwkv_w2': (2, (64, 256), 1), 'rwkv_a2': (2, (64, 256), 1),
    'rwkv_g2': (2, (128, 256), 1), 'rwkv_w_o': (2, (256, 1024), 1), 'conv_w': (2, (3, 256), 1),
    'conv_w_o': (2, (256, 1024), 1), 'w_out': (2, (1024, 1024), 0), 'w_up': (2, (1024, 4096), 1),
    'w_down': (2, (4096, 1024), 0), 'rwkv_v1': (1, (1024, 32), 0), 'rwkv_v2': (1, (32, 256), 1),
}
SMALL_NAMES = [n for n in WEIGHT_NAMES if n not in SHARDED]
TRANSPOSED = ('w_in',)
WIN_SEGS = (('gates', 0, 3072), ('cq', 3072, 3456), ('ckv', 3456, 3712), ('kpe', 3712, 3744), ('rkv', 3744, 4512),
            ('xwa', 4512, 4640), ('xg', 4640, 4768), ('conv', 4768, 5536))
PACK_ROW_MULT = 512


def _cparams(sem=None, **kw):
    if sem is not None:
        kw['dimension_semantics'] = sem
    return pltpu.CompilerParams(vmem_limit_bytes=VMEM_LIMIT, **kw)


def _pick(n, cands):
    for c in cands:
        if n % c == 0:
            return c
    raise ValueError(f'no tile for {n}')


def _mm_nn(a, b, add=None, name='mm_nn'):
    M, K = a.shape
    N = b.shape[1]
    tm = _pick(M, (1024, 512, 256, 128))
    tn = _pick(N, (512, 384, 256, 128))
    tk = _pick(K, (1024, 512, 384, 256, 128))
    nk = K // tk
    has_add = add is not None

    def body(*refs):
        if has_add:
            a_ref, b_ref, add_ref, o_ref, acc_ref = refs
        else:
            a_ref, b_ref, o_ref, acc_ref = refs
        kk = pl.program_id(2)
        part = jnp.dot(a_ref[...].astype(BF16), b_ref[...].astype(BF16), preferred_element_type=F32)

        @pl.when(kk == 0)
        def _():
            acc_ref[...] = part

        @pl.when(kk > 0)
        def _():
            acc_ref[...] += part

        @pl.when(kk == nk - 1)
        def _():
            if has_add:
                o_ref[...] = acc_ref[...] + add_ref[...]
            else:
                o_ref[...] = acc_ref[...]

    in_specs = [pl.BlockSpec((tm, tk), lambda i, j, k: (i, k)), pl.BlockSpec((tk, tn), lambda i, j, k: (k, j))]
    args = [a, b]
    if has_add:
        in_specs.append(pl.BlockSpec((tm, tn), lambda i, j, k: (i, j)))
        args.append(add)
    return pl.pallas_call(
        body, name=name, grid=(M // tm, N // tn, nk), in_specs=in_specs,
        out_specs=pl.BlockSpec((tm, tn), lambda i, j, k: (i, j)),
        out_shape=jax.ShapeDtypeStruct((M, N), F32),
        scratch_shapes=[pltpu.VMEM((tm, tn), F32)],
        compiler_params=_cparams(('parallel', 'parallel', 'arbitrary')),
    )(*args)


def _mm_nt(a, b, add=None, name='mm_nt'):
    M, N = a.shape
    K = b.shape[0]
    tm = _pick(M, (1024, 512, 256, 128))
    tk = _pick(K, (512, 384, 256, 128))
    tn = _pick(N, (1024, 512, 384, 256, 128))
    nn = N // tn
    has_add = add is not None

    def body(*refs):
        if has_add:
            a_ref, b_ref, add_ref, o_ref, acc_ref = refs
        else:
            a_ref, b_ref, o_ref, acc_ref = refs
        kk = pl.program_id(2)
        part = lax.dot_general(a_ref[...].astype(BF16), b_ref[...].astype(BF16), (((1,), (1,)), ((), ())),
                               preferred_element_type=F32)

        @pl.when(kk == 0)
        def _():
            acc_ref[...] = part

        @pl.when(kk > 0)
        def _():
            acc_ref[...] += part

        @pl.when(kk == nn - 1)
        def _():
            if has_add:
                o_ref[...] = acc_ref[...] + add_ref[...]
            else:
                o_ref[...] = acc_ref[...]

    in_specs = [pl.BlockSpec((tm, tn), lambda i, j, k: (i, k)), pl.BlockSpec((tk, tn), lambda i, j, k: (j, k))]
    args = [a, b]
    if has_add:
        in_specs.append(pl.BlockSpec((tm, tk), lambda i, j, k: (i, j)))
        args.append(add)
    return pl.pallas_call(
        body, name=name, grid=(M // tm, K // tk, nn), in_specs=in_specs,
        out_specs=pl.BlockSpec((tm, tk), lambda i, j, k: (i, j)),
        out_shape=jax.ShapeDtypeStruct((M, K), F32),
        scratch_shapes=[pltpu.VMEM((tm, tk), F32)],
        compiler_params=_cparams(('parallel', 'parallel', 'arbitrary')),
    )(*args)


def _mm_tn(a, b, name='mm_tn'):
    M, K = a.shape
    N = b.shape[1]
    tm = _pick(M, (1024, 512, 256, 128))
    tk = _pick(K, (512, 384, 256, 128))
    tn = _pick(N, (512, 384, 256, 128))
    nm = M // tm

    def body(a_ref, b_ref, o_ref, acc_ref):
        mm = pl.program_id(2)
        part = lax.dot_general(a_ref[...].astype(BF16), b_ref[...].astype(BF16), (((0,), (0,)), ((), ())),
                               preferred_element_type=F32)

        @pl.when(mm == 0)
        def _():
            acc_ref[...] = part

        @pl.when(mm > 0)
        def _():
            acc_ref[...] += part

        @pl.when(mm == nm - 1)
        def _():
            o_ref[...] = acc_ref[...]

    return pl.pallas_call(
        body, name=name, grid=(K // tk, N // tn, nm),
        in_specs=[pl.BlockSpec((tm, tk), lambda i, j, m: (m, i)), pl.BlockSpec((tm, tn), lambda i, j, m: (m, j))],
        out_specs=pl.BlockSpec((tk, tn), lambda i, j, m: (i, j)),
        out_shape=jax.ShapeDtypeStruct((K, N), F32),
        scratch_shapes=[pltpu.VMEM((tk, tn), F32)],
        compiler_params=_cparams(('parallel', 'parallel', 'arbitrary')),
    )(a, b)


@functools.partial(jax.custom_vjp, nondiff_argnums=(4,))
def _linear_add(a, wb, wc, add, name):
    return _mm_nn(a, wb, add, name=name + '_f')


def _linear_add_fwd(a, wb, wc, add, name):
    return _mm_nn(a, wb, add, name=name + '_f'), (a, wb)


def _linear_add_bwd(name, res, dy):
    a, wb = res
    return _mm_nt(dy, wb, name=name + '_da'), None, _mm_tn(a, dy, name=name + '_dw'), dy


_linear_add.defvjp(_linear_add_fwd, _linear_add_bwd)


@functools.partial(jax.custom_vjp, nondiff_argnums=(3,))
def _multi_linear(a, wbs, wcs, name):
    return tuple(_mm_nn(a, wb, name=f'{name}_f{i}') for i, wb in enumerate(wbs))


def _multi_linear_fwd(a, wbs, wcs, name):
    return _multi_linear(a, wbs, wcs, name), (a, wbs)


def _multi_linear_bwd(name, res, dys):
    a, wbs = res
    da = None
    for i, (dy, wb) in enumerate(zip(dys, wbs)):
        da = _mm_nt(dy, wb, add=da, name=f'{name}_da{i}')
    dws = tuple(_mm_tn(a, dy, name=f'{name}_dw{i}') for i, dy in enumerate(dys))
    return da, None, dws


_multi_linear.defvjp(_multi_linear_fwd, _multi_linear_bwd)


class W:
    def __init__(self, b, c):
        self.b, self.c = b, c

    def map(self, fn):
        return W(fn(self.b), fn(self.c))


def _wcat(ws, axis):
    return W(jnp.concatenate([w.b for w in ws], axis), jnp.concatenate([w.c for w in ws], axis))


def linear(a, w, add=None, name='lin'):
    if add is None:
        return _multi_linear(a, (w.b,), (w.c,), name)[0]
    return _linear_add(a, w.b, w.c, add, name)


def multi_linear(a, ws, name):
    return _multi_linear(a, tuple(w.b for w in ws), tuple(w.c for w in ws), name)


@functools.partial(jax.custom_vjp, nondiff_argnums=(3,))
def _multi_linear_t(a, wbs, wcs, name):
    return tuple(_mm_nt(a, wb, name=f'{name}_f{i}') for i, wb in enumerate(wbs))


def _multi_linear_t_fwd(a, wbs, wcs, name):
    return _multi_linear_t(a, wbs, wcs, name), (a, wbs)


def _multi_linear_t_bwd(name, res, dys):
    a, wbs = res
    da = None
    for i, (dy, wb) in enumerate(zip(dys, wbs)):
        da = _mm_nn(dy, wb, add=da, name=f'{name}_da{i}')
    dws = tuple(_mm_tn(dy, a, name=f'{name}_dw{i}') for i, dy in enumerate(dys))
    return da, None, dws


_multi_linear_t.defvjp(_multi_linear_t_fwd, _multi_linear_t_bwd)


def multi_linear_t(a, ws, name):
    return _multi_linear_t(a, tuple(w.b for w in ws), tuple(w.c for w in ws), name)


def ROW(diff=True, pieces=None):
    return ('row', diff, pieces)


def FULL(diff=True):
    return ('full', diff, None)


def _load_args(refs, specs):
    args, amap = [], []
    for i, (ref, (kind, diff, pieces)) in enumerate(zip(refs, specs)):
        if pieces is None:
            args.append(ref[...])
            amap.append((i, None))
        else:
            for (s, w) in pieces:
                args.append(ref[:, s:s + w])
                amap.append((i, (s, w)))
    return args, amap


def _stage_in_specs(ins, specs, tb):
    out = []
    for a, (kind, _, _) in zip(ins, specs):
        if kind == 'row':
            out.append(pl.BlockSpec((tb, a.shape[1]), lambda i: (i, 0)))
        else:
            out.append(pl.BlockSpec(a.shape, lambda i: (0, 0)))
    return out


def _stage_fwd(fn, ins, specs, out_widths, name, tb):
    T = [a for a, s in zip(ins, specs) if s[0] == 'row'][0].shape[0]
    tb = min(tb, T)
    n_in = len(ins)

    def body(*refs):
        args, _ = _load_args(refs[:n_in], specs)
        outs = fn(*args)
        for o_ref, o in zip(refs[n_in:], outs):
            o_ref[...] = o

    return pl.pallas_call(
        body, name=name + '_f', grid=(T // tb,), in_specs=_stage_in_specs(ins, specs, tb),
        out_specs=[pl.BlockSpec((tb, w), lambda i: (i, 0)) for w in out_widths],
        out_shape=[jax.ShapeDtypeStruct((T, w), F32) for w in out_widths],
        compiler_params=_cparams(('parallel',)),
    )(*ins)


def _stage_bwd(fn, ins, specs, out_widths, douts, name, tb):
    T = [a for a, s in zip(ins, specs) if s[0] == 'row'][0].shape[0]
    tb = min(tb, T)
    n_in, n_out = len(ins), len(out_widths)
    diff_inputs = [i for i, s in enumerate(specs) if s[1]]

    def body(*refs):
        in_refs, dout_refs, g_refs = refs[:n_in], refs[n_in:n_in + n_out], refs[n_in + n_out:]
        args, amap = _load_args(in_refs, specs)
        didx = [j for j, (i, _) in enumerate(amap) if specs[i][1]]

        def f(*dv):
            full = list(args)
            for j, v in zip(didx, dv):
                full[j] = v
            return tuple(fn(*full))

        _, vjp = jax.vjp(f, *[args[j] for j in didx])
        gs = vjp(tuple(d[...] for d in dout_refs))
        gmap = {j: g for j, g in zip(didx, gs)}
        first = pl.program_id(0) == 0
        for g_ref, i in zip(g_refs, diff_inputs):
            kind, _, pieces = specs[i]
            js = [j for j, (ii, _) in enumerate(amap) if ii == i]
            if kind == 'row':
                if pieces is None:
                    g_ref[...] = gmap[js[0]]
                else:
                    if sum(w for _, w in pieces) != ins[i].shape[1]:
                        g_ref[...] = jnp.zeros(g_ref.shape, F32)
                    for j in js:
                        s, w = amap[j][1]
                        g_ref[:, s:s + w] = gmap[j]
            else:
                @pl.when(first)
                def _(g_ref=g_ref):
                    g_ref[...] = jnp.zeros(g_ref.shape, F32)

                g_ref[...] += gmap[js[0]]

    in_specs = _stage_in_specs(ins, specs, tb) + [pl.BlockSpec((tb, w), lambda i: (i, 0)) for w in out_widths]
    out_specs, out_shape = [], []
    for i in diff_inputs:
        a = ins[i]
        if specs[i][0] == 'row':
            out_specs.append(pl.BlockSpec((tb, a.shape[1]), lambda i: (i, 0)))
        else:
            out_specs.append(pl.BlockSpec(a.shape, lambda i: (0, 0)))
        out_shape.append(jax.ShapeDtypeStruct(a.shape, F32))
    return pl.pallas_call(
        body, name=name + '_b', grid=(T // tb,), in_specs=in_specs, out_specs=out_specs, out_shape=out_shape,
        compiler_params=_cparams(('arbitrary',)),
    )(*ins, *douts)


def stage_op(fn, specs, out_widths, name, tb=256):
    n = len(specs)
    diff_inputs = [i for i, s in enumerate(specs) if s[1]]

    @jax.custom_vjp
    def op(*ins):
        return tuple(_stage_fwd(fn, ins, specs, out_widths, name, tb))

    def op_fwd(*ins):
        return op(*ins), ins

    def op_bwd(ins, douts):
        gs = _stage_bwd(fn, ins, specs, out_widths, douts, name, tb)
        res = [None] * n
        for i, g in zip(diff_inputs, gs):
            res[i] = g
        return tuple(res)

    op.defvjp(op_fwd, op_bwd)
    return op


@jax.custom_vjp
def bdot(x, w):
    return jnp.dot(x.astype(BF16), w.astype(BF16), preferred_element_type=F32)


def _bdot_fwd(x, w):
    return bdot(x, w), (x, w)


def _bdot_bwd(res, dy):
    x, w = res
    dyb = dy.astype(BF16)
    dx = lax.dot_general(dyb, w.astype(BF16), (((1,), (1,)), ((), ())), preferred_element_type=F32)
    dw = lax.dot_general(x.astype(BF16), dyb, (((0,), (0,)), ((), ())), preferred_element_type=F32)
    return dx, dw


bdot.defvjp(_bdot_fwd, _bdot_bwd)


def _sdot_raw(x, c):
    hi = x.astype(BF16)
    r1 = x - hi.astype(F32)
    mid = r1.astype(BF16)
    lo = (r1 - mid.astype(F32)).astype(BF16)
    d = lambda u: jnp.dot(u, c, preferred_element_type=F32)
    return d(hi) + d(mid) + d(lo)


@jax.custom_vjp
def sdot(x, c, ct):
    return _sdot_raw(x, c)


def _sdot_fwd(x, c, ct):
    return _sdot_raw(x, c), (c, ct)


def _sdot_bwd(res, dy):
    c, ct = res
    return _sdot_raw(dy, ct), None, None


sdot.defvjp(_sdot_fwd, _sdot_bwd)


def _sigmoid(x):
    return 1.0 / (1.0 + jnp.exp(-x))


def _rms(x, g):
    return x * lax.rsqrt(jnp.mean(x * x, axis=-1, keepdims=True) + NORM_EPS) * g


def rmsnorm(x, g, name):
    op = stage_op(lambda xv, gv: (_rms(xv, gv),), [ROW(), FULL()], [x.shape[1]], name)
    return op(x, g.reshape(1, -1))[0]


def _shift_down(x, rows):
    return jnp.where(rows == 0, 0.0, pltpu.roll(x, 1, 0))


def _shift_up(x, rows, T):
    return jnp.where(rows == T - 1, 0.0, pltpu.roll(x, T - 1, 0))


def _tshift_fwd_call(x, mu, name):
    T, C = x.shape

    def body(x_ref, mu_ref, o_ref):
        xv = x_ref[...]
        rows = lax.broadcasted_iota(jnp.int32, xv.shape, 0)
        o_ref[...] = xv + (_shift_down(xv, rows) - xv) * mu_ref[...]

    return pl.pallas_call(
        body, name=name + '_f', grid=(C // LANES,),
        in_specs=[pl.BlockSpec((T, LANES), lambda j: (0, j)), pl.BlockSpec((1, LANES), lambda j: (0, j))],
        out_specs=pl.BlockSpec((T, LANES), lambda j: (0, j)), out_shape=jax.ShapeDtypeStruct((T, C), F32),
        compiler_params=_cparams(('parallel',)),
    )(x, mu)


def _tshift_bwd_call(x, mu, dy, name):
    T, C = x.shape

    def body(x_ref, mu_ref, dy_ref, dx_ref, dmu_ref):
        xv, d = x_ref[...], dy_ref[...]
        rows = lax.broadcasted_iota(jnp.int32, xv.shape, 0)
        z = d * mu_ref[...]
        dx_ref[...] = d - z + _shift_up(z, rows, T)
        dmu_ref[...] = jnp.sum(d * (_shift_down(xv, rows) - xv), axis=0, keepdims=True)

    return pl.pallas_call(
        body, name=name + '_b', grid=(C // LANES,),
        in_specs=[pl.BlockSpec((T, LANES), lambda j: (0, j)), pl.BlockSpec((1, LANES), lambda j: (0, j)),
                  pl.BlockSpec((T, LANES), lambda j: (0, j))],
        out_specs=[pl.BlockSpec((T, LANES), lambda j: (0, j)), pl.BlockSpec((1, LANES), lambda j: (0, j))],
        out_shape=[jax.ShapeDtypeStruct((T, C), F32), jax.ShapeDtypeStruct((1, C), F32)],
        compiler_params=_cparams(('parallel',)),
    )(x, mu, dy)


@functools.partial(jax.custom_vjp, nondiff_argnums=(2,))
def token_shift_mix(x, mu, name):
    return _tshift_fwd_call(x, mu, name)


def _tsm_fwd(x, mu, name):
    return _tshift_fwd_call(x, mu, name), (x, mu)


def _tsm_bwd(name, res, dy):
    x, mu = res
    dx, dmu = _tshift_bwd_call(x, mu, dy, name)
    return dx, dmu


token_shift_mix.defvjp(_tsm_fwd, _tsm_bwd)


def _conv_specs(T):
    nb = CONV_WIDTH // LANES
    return [pl.BlockSpec((T, LANES), lambda j: (0, j)), pl.BlockSpec((T, LANES), lambda j: (0, nb + j)),
            pl.BlockSpec((T, LANES), lambda j: (0, 2 * nb + j)), pl.BlockSpec((3, LANES), lambda j: (0, j))]


def _conv_fwd_call(cv, w, name):
    T = cv.shape[0]

    def body(b_ref, c_ref, x_ref, w_ref, o_ref):
        u = c_ref[...] * x_ref[...]
        rows = lax.broadcasted_iota(jnp.int32, u.shape, 0)
        u1 = _shift_down(u, rows)
        u2 = _shift_down(u1, rows)
        o_ref[...] = b_ref[...] * (w_ref[0:1, :] * u2 + w_ref[1:2, :] * u1 + w_ref[2:3, :] * u)

    return pl.pallas_call(
        body, name=name + '_f', grid=(CONV_WIDTH // LANES,), in_specs=_conv_specs(T),
        out_specs=pl.BlockSpec((T, LANES), lambda j: (0, j)),
        out_shape=jax.ShapeDtypeStruct((T, CONV_WIDTH), F32), compiler_params=_cparams(('parallel',)),
    )(cv, cv, cv, w)


def _conv_bwd_call(cv, w, do, name):
    T = cv.shape[0]

    def body(b_ref, c_ref, x_ref, w_ref, do_ref, db_ref, dc_ref, dx_ref, dw_ref):
        c, x, d = c_ref[...], x_ref[...], do_ref[...]
        u = c * x
        rows = lax.broadcasted_iota(jnp.int32, u.shape, 0)
        u1 = _shift_down(u, rows)
        u2 = _shift_down(u1, rows)
        w0, w1, w2 = w_ref[0:1, :], w_ref[1:2, :], w_ref[2:3, :]
        db_ref[...] = d * (w0 * u2 + w1 * u1 + w2 * u)
        dy = d * b_ref[...]
        dy1 = _shift_up(dy, rows, T)
        dy2 = _shift_up(dy1, rows, T)
        du = w2 * dy + w1 * dy1 + w0 * dy2
        dc_ref[...] = du * x
        dx_ref[...] = du * c
        dw_ref[0:1, :] = jnp.sum(dy * u2, axis=0, keepdims=True)
        dw_ref[1:2, :] = jnp.sum(dy * u1, axis=0, keepdims=True)
        dw_ref[2:3, :] = jnp.sum(dy * u, axis=0, keepdims=True)

    blk = pl.BlockSpec((T, LANES), lambda j: (0, j))
    sh = jax.ShapeDtypeStruct((T, CONV_WIDTH), F32)
    return pl.pallas_call(
        body, name=name + '_b', grid=(CONV_WIDTH // LANES,), in_specs=_conv_specs(T) + [blk],
        out_specs=[blk, blk, blk, pl.BlockSpec((3, LANES), lambda j: (0, j))],
        out_shape=[sh, sh, sh, jax.ShapeDtypeStruct((3, CONV_WIDTH), F32)],
        compiler_params=_cparams(('parallel',)),
    )(cv, cv, cv, w, do)


@functools.partial(jax.custom_vjp, nondiff_argnums=(2,))
def short_conv(cv, w, name):
    return _conv_fwd_call(cv, w, name)


def _sc_fwd(cv, w, name):
    return _conv_fwd_call(cv, w, name), (cv, w)


def _sc_bwd(name, res, do):
    cv, w = res
    db, dc, dx, dw = _conv_bwd_call(cv, w, do, name)
    return jnp.concatenate([db, dc, dx], axis=1), dw


short_conv.defvjp(_sc_fwd, _sc_bwd)


ATT_SCALE = QK_HEAD ** -0.5
NPAIR = MLA_HEADS // 2


def _att_bq(T):
    return min(256, T)


def _att_masks(pair, j):
    lane = lax.broadcasted_iota(jnp.int32, (1, LANES), 1)
    mask_n = (lane // QK_NOPE) == j
    mask_r = (lane // (QK_ROPE // 2)) == (2 * pair + j)
    return mask_n, mask_r


def _att_probs(qcat, kcat, row0, stop):
    s = lax.dot_general(qcat, kcat, (((1,), (1,)), ((), ())), preferred_element_type=F32) * ATT_SCALE
    r = row0 + lax.broadcasted_iota(jnp.int32, s.shape, 0)
    c = lax.broadcasted_iota(jnp.int32, s.shape, 1)
    s = jnp.where(c <= r, s, -jnp.inf)
    e = jnp.exp(s - jnp.max(s, axis=-1, keepdims=True))
    return e / jnp.sum(e, axis=-1, keepdims=True)


def _att_in_specs(T):
    blk = lambda f: pl.BlockSpec((T, LANES), f)
    return [blk(lambda p: (0, p)), blk(lambda p: (0, 0)), blk(lambda p: (0, 0)),
            blk(lambda p: (0, p)), blk(lambda p: (0, 0)), blk(lambda p: (0, 0)), blk(lambda p: (0, p))]


def _att_fwd_call(qn, q1, q2, kn, k1, k2, v, name):
    T = qn.shape[0]
    bq = _att_bq(T)

    def body(qn_ref, q1_ref, q2_ref, kn_ref, k1_ref, k2_ref, v_ref, o_ref):
        pair = pl.program_id(0)
        for i in range(T // bq):
            r0, stop = i * bq, (i + 1) * bq
            kcat = jnp.concatenate([kn_ref[0:stop, :], k1_ref[0:stop, :], k2_ref[0:stop, :]], axis=1).astype(BF16)
            vb = v_ref[0:stop, :].astype(BF16)
            outs = []
            for j in range(2):
                mask_n, mask_r = _att_masks(pair, j)
                qcat = jnp.concatenate([jnp.where(mask_n, qn_ref[r0:stop, :], 0.0),
                                        jnp.where(mask_r, q1_ref[r0:stop, :], 0.0),
                                        jnp.where(mask_r, q2_ref[r0:stop, :], 0.0)], axis=1).astype(BF16)
                p = _att_probs(qcat, kcat, r0, stop)
                outs.append(jnp.dot(p.astype(BF16), vb, preferred_element_type=F32))
            mask_n0, _ = _att_masks(pair, 0)
            o_ref[r0:stop, :] = jnp.where(mask_n0, outs[0], outs[1])

    return pl.pallas_call(
        body, name=name + '_f', grid=(NPAIR,), in_specs=_att_in_specs(T),
        out_specs=pl.BlockSpec((T, LANES), lambda p: (0, p)),
        out_shape=jax.ShapeDtypeStruct((T, MLA_HEADS * V_HEAD), F32), compiler_params=_cparams(('parallel',)),
    )(qn, q1, q2, kn, k1, k2, v)


def _att_bwd_call(qn, q1, q2, kn, k1, k2, v, o, do, name):
    T = qn.shape[0]
    bq = _att_bq(T)

    def body(qn_ref, q1_ref, q2_ref, kn_ref, k1_ref, k2_ref, v_ref, o_ref, do_ref,
             dqn_ref, dq1_ref, dq2_ref, dkn_ref, dk1_ref, dk2_ref, dv_ref, dk_acc, dv_acc):
        pair = pl.program_id(0)

        @pl.when(pair == 0)
        def _():
            dq1_ref[...] = jnp.zeros(dq1_ref.shape, F32)
            dq2_ref[...] = jnp.zeros(dq2_ref.shape, F32)
            dk1_ref[...] = jnp.zeros(dk1_ref.shape, F32)
            dk2_ref[...] = jnp.zeros(dk2_ref.shape, F32)

        dk_acc[...] = jnp.zeros(dk_acc.shape, F32)
        dv_acc[...] = jnp.zeros(dv_acc.shape, F32)
        for i in range(T // bq):
            r0, stop = i * bq, (i + 1) * bq
            kcat = jnp.concatenate([kn_ref[0:stop, :], k1_ref[0:stop, :], k2_ref[0:stop, :]], axis=1).astype(BF16)
            vb = v_ref[0:stop, :].astype(BF16)
            dqn = jnp.zeros((bq, LANES), F32)
            for j in range(2):
                mask_n, mask_r = _att_masks(pair, j)
                qcat = jnp.concatenate([jnp.where(mask_n, qn_ref[r0:stop, :], 0.0),
                                        jnp.where(mask_r, q1_ref[r0:stop, :], 0.0),
                                        jnp.where(mask_r, q2_ref[r0:stop, :], 0.0)], axis=1).astype(BF16)
                p = _att_probs(qcat, kcat, r0, stop)
                dom = jnp.where(mask_n, do_ref[r0:stop, :], 0.0)
                delta = jnp.sum(dom * o_ref[r0:stop, :], axis=-1, keepdims=True)
                domb = dom.astype(BF16)
                dp = lax.dot_general(domb, vb, (((1,), (1,)), ((), ())), preferred_element_type=F32)
                ds = (p * (dp - delta) * ATT_SCALE).astype(BF16)
                dqc = jnp.dot(ds, kcat, preferred_element_type=F32)
                dqn = dqn + jnp.where(mask_n, dqc[:, 0:LANES], 0.0)
                dq1_ref[r0:stop, :] += jnp.where(mask_r, dqc[:, LANES:2 * LANES], 0.0)
                dq2_ref[r0:stop, :] += jnp.where(mask_r, dqc[:, 2 * LANES:3 * LANES], 0.0)
                dk_acc[0:stop, :] += lax.dot_general(ds, qcat, (((0,), (0,)), ((), ())),
                                                     preferred_element_type=F32)
                dv_acc[0:stop, :] += lax.dot_general(p.astype(BF16), domb, (((0,), (0,)), ((), ())),
                                                     preferred_element_type=F32)
            dqn_ref[r0:stop, :] = dqn
        dkn_ref[...] = dk_acc[:, 0:LANES]
        dk1_ref[...] += dk_acc[:, LANES:2 * LANES]
        dk2_ref[...] += dk_acc[:, 2 * LANES:3 * LANES]
        dv_ref[...] = dv_acc[...]

    per_pair = pl.BlockSpec((T, LANES), lambda p: (0, p))
    shared = pl.BlockSpec((T, LANES), lambda p: (0, 0))
    wide = jax.ShapeDtypeStruct((T, MLA_HEADS * QK_NOPE), F32)
    narrow = jax.ShapeDtypeStruct((T, LANES), F32)
    return pl.pallas_call(
        body, name=name + '_b', grid=(NPAIR,), in_specs=_att_in_specs(T) + [per_pair, per_pair],
        out_specs=[per_pair, shared, shared, per_pair, shared, shared, per_pair],
        out_shape=[wide, narrow, narrow, wide, narrow, narrow, wide],
        scratch_shapes=[pltpu.VMEM((T, 3 * LANES), F32), pltpu.VMEM((T, LANES), F32)],
        compiler_params=_cparams(('arbitrary',)),
    )(qn, q1, q2, kn, k1, k2, v, o, do)


@functools.partial(jax.custom_vjp, nondiff_argnums=(7,))
def attention(qn, q1, q2, kn, k1, k2, v, name):
    return _att_fwd_call(qn, q1, q2, kn, k1, k2, v, name)


def _attn_fwd(qn, q1, q2, kn, k1, k2, v, name):
    o = _att_fwd_call(qn, q1, q2, kn, k1, k2, v, name)
    return o, (qn, q1, q2, kn, k1, k2, v, o)


def _attn_bwd(name, res, do):
    return tuple(_att_bwd_call(*res, do, name))


attention.defvjp(_attn_fwd, _attn_bwd)


SCAN_CHUNK = 64
SCAN_UNROLL = 4


def _block_ones(n, seg):
    i = np.arange(n)
    return (i[:, None] // seg == i[None, :] // seg).astype(np.float32)


def _scan_diag():
    i = np.arange(RW_WIDTH)
    return jnp.asarray((np.arange(RW_N)[:, None] == (i[None, :] % RW_N)).astype(np.float32))


def _head_rowsum(x):
    low = lax.broadcasted_iota(jnp.int32, (1, LANES), 1) < RW_N
    tiles = []
    for j in range(RW_WIDTH // LANES):
        xt = x[:, j * LANES:(j + 1) * LANES]
        x0 = jnp.where(low, xt, 0.0)
        s0 = jnp.sum(x0, axis=-1, keepdims=True)
        s1 = jnp.sum(xt - x0, axis=-1, keepdims=True)
        tiles.append(jnp.where(low, s0, s1))
    return jnp.concatenate(tiles, axis=1)


def _unrolled_loop(n, step, init):
    def body(i, carry):
        for j in range(SCAN_UNROLL):
            carry = step(i * SCAN_UNROLL + j, carry)
        return carry
    return lax.fori_loop(0, n // SCAN_UNROLL, body, init)


def _scan_fwd_call(r, w, k, v, a, b, name):
    T = r.shape[0]
    tc = min(SCAN_CHUNK, T)
    dg = _scan_diag()

    def body(r_ref, w_ref, k_ref, v_ref, a_ref, b_ref, dg_ref, y_ref, st_ref, s_ref):
        @pl.when(pl.program_id(0) == 0)
        def _():
            s_ref[...] = jnp.zeros(s_ref.shape, F32)

        dgv = dg_ref[...]
        readout = lambda s, t: jnp.sum(_head_rowsum(s * r_ref[t]) * dgv, axis=0, keepdims=True)

        def step(t, carry):
            s, vcol = carry
            st_ref[t] = s
            sa = _head_rowsum(s * a_ref[t])
            prev = jnp.maximum(t - 1, 0)
            y_ref[prev] = readout(s, prev)
            vcol_next = _head_rowsum(dgv * v_ref[jnp.minimum(t + 1, tc - 1)])
            sn = s * w_ref[t] + sa * b_ref[t] + vcol * k_ref[t]
            return sn, vcol_next

        s_end, _ = _unrolled_loop(tc, step, (s_ref[...], _head_rowsum(dgv * v_ref[0])))
        y_ref[tc - 1] = readout(s_end, tc - 1)
        s_ref[...] = s_end

    vec = pl.BlockSpec((tc, 1, RW_WIDTH), lambda i: (i, 0, 0))
    return pl.pallas_call(
        body, name=name + '_f', grid=(T // tc,),
        in_specs=[vec] * 6 + [pl.BlockSpec((RW_N, RW_WIDTH), lambda i: (0, 0))],
        out_specs=[vec, pl.BlockSpec((tc, RW_N, RW_WIDTH), lambda i: (i, 0, 0))],
        out_shape=[jax.ShapeDtypeStruct((T, 1, RW_WIDTH), F32), jax.ShapeDtypeStruct((T, RW_N, RW_WIDTH), F32)],
        scratch_shapes=[pltpu.VMEM((RW_N, RW_WIDTH), F32)],
        compiler_params=_cparams(('arbitrary',)),
    )(r, w, k, v, a, b, dg)


def _scan_bwd_call(r, w, k, v, a, b, st, dy, name):
    T = r.shape[0]
    tc = min(SCAN_CHUNK, T)
    nt = T // tc
    dg = _scan_diag()

    def body(r_ref, w_ref, k_ref, v_ref, a_ref, b_ref, st_ref, dy_ref, dg_ref,
             dr_ref, dw_ref, dk_ref, dv_ref, da_ref, db_ref, ds_ref):
        @pl.when(pl.program_id(0) == 0)
        def _():
            ds_ref[...] = jnp.zeros(ds_ref.shape, F32)

        dgv = dg_ref[...]
        colsum = lambda x: jnp.sum(x, axis=0, keepdims=True)

        def step(i, carry):
            ds, dycol = carry
            t = tc - 1 - i
            sp = st_ref[t]
            rt, wt, kt, at, bt = r_ref[t], w_ref[t], k_ref[t], a_ref[t], b_ref[t]
            ds = ds + dycol * rt
            dsa = _head_rowsum(ds * bt)
            sa = _head_rowsum(sp * at)
            vcol = _head_rowsum(dgv * v_ref[t])
            dycol_next = _head_rowsum(dgv * dy_ref[jnp.maximum(t - 1, 0)])
            sn = sp * wt + sa * bt + vcol * kt
            dr_ref[t] = colsum(sn * dycol)
            dk_ref[t] = colsum(ds * vcol)
            db_ref[t] = colsum(ds * sa)
            dw_ref[t] = colsum(ds * sp)
            dv_ref[t] = colsum(_head_rowsum(ds * kt) * dgv)
            da_ref[t] = colsum(sp * dsa)
            return ds * wt + dsa * at, dycol_next

        ds_end, _ = _unrolled_loop(tc, step, (ds_ref[...], _head_rowsum(dgv * dy_ref[tc - 1])))
        ds_ref[...] = ds_end

    vec = pl.BlockSpec((tc, 1, RW_WIDTH), lambda i: (nt - 1 - i, 0, 0))
    vsh = jax.ShapeDtypeStruct((T, 1, RW_WIDTH), F32)
    return pl.pallas_call(
        body, name=name + '_b', grid=(nt,),
        in_specs=[vec] * 6 + [pl.BlockSpec((tc, RW_N, RW_WIDTH), lambda i: (nt - 1 - i, 0, 0)), vec,
                              pl.BlockSpec((RW_N, RW_WIDTH), lambda i: (0, 0))],
        out_specs=[vec] * 6, out_shape=[vsh] * 6,
        scratch_shapes=[pltpu.VMEM((RW_N, RW_WIDTH), F32)],
        compiler_params=_cparams(('arbitrary',)),
    )(r, w, k, v, a, b, st, dy, dg)


@functools.partial(jax.custom_vjp, nondiff_argnums=(6,))
def wkv7(r, w, k, v, a, b, name):
    return _scan_fwd_call(r, w, k, v, a, b, name)[0]


def _wkv7_fwd(r, w, k, v, a, b, name):
    y, st = _scan_fwd_call(r, w, k, v, a, b, name)
    return y, (r, w, k, v, a, b, st)


def _wkv7_bwd(name, res, dy):
    return tuple(_scan_bwd_call(*res, dy, name))


wkv7.defvjp(_wkv7_fwd, _wkv7_bwd)


def _np_bf16(a):
    return jnp.asarray(a, BF16)


def _mla_consts():
    seg_n = (np.arange(512)[:, None] // QK_NOPE == np.arange(LANES)[None, :]).astype(np.float32)
    seg_r = (np.arange(LANES)[:, None] // 16 == np.arange(LANES)[None, :]).astype(np.float32)
    e1 = np.zeros((LANES, LANES), np.float32)
    e2 = np.zeros((LANES, LANES), np.float32)
    for h in range(MLA_HEADS):
        for i in range(16):
            e1[i, h * 16 + i] = 1.0
            e2[16 + i, h * 16 + i] = 1.0
    mats = [seg_n, seg_n.T, seg_r, seg_r.T, e1, e1.T, e2, e2.T]
    return [_np_bf16(m) for m in mats]


def _qk_prep_fn(qn, q1, q2, kn, kx, cos, sin, gqn, gq1, gq2, gkn, gk1, gk2,
                seg_n, seg_nt, seg_r, seg_rt, e1, e1t, e2, e2t):
    def normrope(xn, x1, x2, gn, g1, g2):
        ss = sdot(xn * xn, seg_n, seg_nt) + sdot(x1 * x1, seg_r, seg_rt) + sdot(x2 * x2, seg_r, seg_rt)
        inv = lax.rsqrt(ss * (1.0 / QK_HEAD) + NORM_EPS)
        inv_n = sdot(inv, seg_nt, seg_n)
        inv_r = sdot(inv, seg_rt, seg_r)
        y1 = x1 * inv_r * g1
        y2 = x2 * inv_r * g2
        return xn * inv_n * gn, y1 * cos - y2 * sin, y1 * sin + y2 * cos

    k1 = sdot(kx, e1, e1t)
    k2 = sdot(kx, e2, e2t)
    return normrope(qn, q1, q2, gqn, gq1, gq2) + normrope(kn, k1, k2, gkn, gk1, gk2)


def _rwkv_prep_fn(vres):
    def fn(r, k, v, xg, xwa, kx, *rest):
        if vres:
            vfirst, w0, a0, k_k, k_a, w2p, a2p, g2, v0, v2p, bm = rest
        else:
            w0, a0, k_k, k_a, w2p, a2p, g2, bm = rest
        z = w0 + bdot(jnp.tanh(xwa), w2p)
        nz = -z
        softplus = jnp.maximum(nz, 0.0) + jnp.log(1.0 + jnp.exp(-jnp.abs(nz)))
        decay = jnp.exp(-jnp.exp(-softplus - 0.5))
        a = _sigmoid(a0 + bdot(xwa, a2p))
        g = bdot(_sigmoid(xg), g2)
        if vres:
            vv = v + (vfirst - v) * _sigmoid(v0 + bdot(kx, v2p))
        else:
            vv = v
        kkr = k * k_k
        kk = kkr / jnp.maximum(jnp.sqrt(sdot(kkr * kkr, bm, bm)), 1e-12)
        k2 = k * (1.0 + (a - 1.0) * k_a)
        return r * 1.0, decay, k2, vv, -kk, kk * a, g
    return fn


def _rwkv_post_fn(y, r, k2, vv, g, ln_w, ln_b, rk, bm):
    inv_n = 1.0 / RW_N
    mean = sdot(y, bm, bm) * inv_n
    yc = y - mean
    var = sdot(yc * yc, bm, bm) * inv_n
    yn = yc * lax.rsqrt(var + GN_EPS) * ln_w + ln_b
    bonus = sdot(r * k2 * rk, bm, bm) * vv
    return ((yn + bonus) * g,)


def _merge_fn(g0, g1, g2, oa, ob, oc):
    return (_sigmoid(g0) * oa + _sigmoid(g1) * ob + _sigmoid(g2) * oc,)


def _relu2_fn(u):
    r = jnp.maximum(u, 0.0)
    return (r * r,)


def _loss_call(y, target):
    T, C = y.shape
    tb = min(256, T)

    def body(y_ref, t_ref, dy_ref, part_ref):
        err = y_ref[...] - t_ref[...]
        dy_ref[...] = err * (1.0 / C)
        sq = jnp.sum(err * err, axis=0, keepdims=True)
        acc = sq[:, 0:LANES]
        for j in range(1, C // LANES):
            acc = acc + sq[:, j * LANES:(j + 1) * LANES]
        part_ref[...] = jnp.zeros(part_ref.shape, F32)
        part_ref[0:1, :] = acc * (0.5 / C)

    return pl.pallas_call(
        body, name='loss', grid=(T // tb,),
        in_specs=[pl.BlockSpec((tb, C), lambda i: (i, 0))] * 2,
        out_specs=[pl.BlockSpec((tb, C), lambda i: (i, 0)), pl.BlockSpec((8, LANES), lambda i: (i, 0))],
        out_shape=[jax.ShapeDtypeStruct((T, C), F32), jax.ShapeDtypeStruct((8 * (T // tb), LANES), F32)],
        compiler_params=_cparams(('parallel',)),
    )(y, target)


def _cols(w, lo, hi):
    return w.map(lambda t: t[:, lo:hi])


def _pad_rows(t, before, total):
    return jnp.pad(t, ((before, total - before - t.shape[0]), (0, 0)))


def _head_tile(g, lo, hi):
    return jnp.tile(g[lo:hi], MLA_HEADS).reshape(1, -1)


def _layer(l, x, v_first, wd, sp, cos, sin):
    T = x.shape[0]
    nm = f'l{l}'
    vres = l > 0
    w_in = wd['w_in']
    if vres:
        v1t = wd['rwkv_v1'].map(lambda t: t.T)
    else:
        v1t = W(jnp.zeros((MV_LORA, D_MODEL), BF16), jnp.zeros((MV_LORA, D_MODEL), F32))
    zpad = W(jnp.zeros((64, D_MODEL), BF16), jnp.zeros((64, D_MODEL), F32))
    w_rw = _wcat([w_in['rkv'], w_in['xg'], w_in['xwa'], w_in['kpe'], v1t, zpad], 0)
    h = rmsnorm(x, sp['attn_norm'], nm + '_anorm')
    gates, cq, ckv, rw, cv = multi_linear_t(
        h, [w_in['gates'], w_in['cq'], w_in['ckv'], w_rw, w_in['conv']], nm + '_win')

    v_mu = sp['rwkv_v_mu'] if vres else jnp.zeros((MV_LORA,), F32)
    mu_all = jnp.concatenate([sp['rwkv_mu'][0:768], sp['rwkv_mu'][896:1024], sp['rwkv_mu'][768:896],
                              jnp.zeros((QK_ROPE,), F32), v_mu, jnp.zeros((64,), F32)]).reshape(1, -1)
    rws = token_shift_mix(rw, mu_all, nm + '_shift')

    cqn = rmsnorm(cq, sp['mla_q_a_norm'], nm + '_qan')
    ckvn = rmsnorm(ckv, sp['mla_kv_a_norm'], nm + '_kvan')
    wq = wd['mla_wq_b'].map(lambda t: jnp.concatenate(
        [t.reshape(Q_LORA, MLA_HEADS, QK_HEAD)[:, :, 0:64].reshape(Q_LORA, 512),
         t.reshape(Q_LORA, MLA_HEADS, QK_HEAD)[:, :, 64:80].reshape(Q_LORA, 128),
         t.reshape(Q_LORA, MLA_HEADS, QK_HEAD)[:, :, 80:96].reshape(Q_LORA, 128)], axis=1))
    wkn = wd['mla_wkv_b'].map(lambda t: t.reshape(KV_LORA, MLA_HEADS, 128)[:, :, 0:64].reshape(KV_LORA, 512))
    wv = wd['mla_wkv_b'].map(lambda t: t.reshape(KV_LORA, MLA_HEADS, 128)[:, :, 64:128].reshape(KV_LORA, 512))
    q = linear(cqn, wq, name=nm + '_wq')
    kn, vv_att = multi_linear(ckvn, [wkn, wv], nm + '_wkv')
    gq, gk = sp['mla_q_norm'], sp['mla_k_norm']
    consts = _mla_consts()
    qk_specs = ([ROW(pieces=((0, 512), (512, 128), (640, 128))), ROW(), ROW(pieces=((1024, 128),)),
                 ROW(False), ROW(False)] + [FULL()] * 6 + [FULL(False)] * 8)
    qk_op = stage_op(_qk_prep_fn, qk_specs, [512, 128, 128, 512, 128, 128], nm + '_qkprep')
    Qn, Q1, Q2, Kn, K1, K2 = qk_op(q, kn, rws, cos, sin,
                                   _head_tile(gq, 0, 64), _head_tile(gq, 64, 80), _head_tile(gq, 80, 96),
                                   _head_tile(gk, 0, 64), _head_tile(gk, 64, 80), _head_tile(gk, 80, 96), *consts)
    o_att = attention(Qn, Q1, Q2, Kn, K1, K2, vv_att, nm + '_att')
    o_a = linear(o_att, wd['mla_w_o'], name=nm + '_wo')

    bm = _np_bf16(_block_ones(RW_WIDTH, RW_N))
    vec = lambda n: sp[n].reshape(1, -1)
    f32w = lambda n: wd[n].c + wd[n].b.astype(F32)
    w2p = _pad_rows(f32w('rwkv_w2'), 0, 128)
    a2p = _pad_rows(f32w('rwkv_a2'), 64, 128)
    g2 = f32w('rwkv_g2')
    rw_pieces = ((0, 256), (256, 256), (512, 256), (768, 128), (896, 128), (1024, 128))
    if vres:
        v2p = _pad_rows(f32w('rwkv_v2'), 32, 128)
        prep_specs = [ROW(pieces=rw_pieces), ROW()] + [FULL()] * 9 + [FULL(False)]
        prep_in = [rws, v_first, vec('rwkv_w0'), vec('rwkv_a0'), vec('rwkv_k_k'), vec('rwkv_k_a'), w2p, a2p, g2,
                   vec('rwkv_v0'), v2p, bm]
    else:
        prep_specs = [ROW(pieces=rw_pieces)] + [FULL()] * 7 + [FULL(False)]
        prep_in = [rws, vec('rwkv_w0'), vec('rwkv_a0'), vec('rwkv_k_k'), vec('rwkv_k_a'), w2p, a2p, g2, bm]
    prep_op = stage_op(_rwkv_prep_fn(vres), prep_specs, [256] * 7, nm + '_rwprep')
    r_, dec, k2, vv, an, bn, g = prep_op(*prep_in)
    if not vres:
        v_first = vv
    t3 = lambda t: t.reshape(T, 1, RW_WIDTH)
    y = wkv7(t3(r_), t3(dec), t3(k2), t3(vv), t3(an), t3(bn), nm + '_scan').reshape(T, RW_WIDTH)
    post_op = stage_op(_rwkv_post_fn, [ROW()] * 5 + [FULL()] * 3 + [FULL(False)], [256], nm + '_rwpost')
    yb = post_op(y, r_, k2, vv, g, vec('rwkv_ln_w'), vec('rwkv_ln_b'), sp['rwkv_r_k'].reshape(1, -1), bm)[0]
    o_b = linear(yb, wd['rwkv_w_o'], name=nm + '_rwo')

    oc_in = short_conv(cv, f32w('conv_w'), nm + '_conv')
    o_c = linear(oc_in, wd['conv_w_o'], name=nm + '_cwo')

    merge_op = stage_op(_merge_fn, [ROW(pieces=((0, 1024), (1024, 1024), (2048, 1024))), ROW(), ROW(), ROW()],
                        [D_MODEL], nm + '_merge')
    merged = merge_op(gates, o_a, o_b, o_c)[0]
    x2 = linear(merged, wd['w_out'], add=x, name=nm + '_wout')
    h2 = rmsnorm(x2, sp['mlp_norm'], nm + '_mnorm')
    u = linear(h2, wd['w_up'], name=nm + '_wup')
    act = stage_op(_relu2_fn, [ROW()], [D_FF], nm + '_relu2')(u)[0]
    x3 = linear(act, wd['w_down'], add=x2, name=nm + '_wdown')
    return x3, v_first


def _entries(ml):
    out = []
    for name, (layers, shape, axis) in SHARDED.items():
        l = ml if layers == DEPTH else ml - 1
        if not 0 <= l < layers:
            continue
        n = shape[0] * shape[1] // N_DEV
        if name == 'conv_w':
            out.append(('conv_w_hi', name, l, n))
            out.append(('conv_w_lo', name, l, n))
        else:
            out.append((name, name, l, n))
    return out


def _slot_size(n):
    return -(-n // LANES) * LANES


def _pack_rows(ml):
    total = sum(_slot_size(n) for _, _, _, n in _entries(ml))
    rows = -(-total // LANES)
    return -(-rows // PACK_ROW_MULT) * PACK_ROW_MULT


def _pack_flat(pieces, ml):
    rows = _pack_rows(ml)
    padded = []
    for p, (_, _, _, n) in zip(pieces, _entries(ml)):
        pad = _slot_size(n) - n
        if pad:
            p = jnp.pad(p, [(0, 0)] * (p.ndim - 1) + [(0, pad)])
        padded.append(p)
    flat = jnp.concatenate(padded, axis=-1)
    tail = rows * LANES - flat.shape[-1]
    if tail:
        flat = jnp.pad(flat, [(0, 0)] * (flat.ndim - 1) + [(0, tail)])
    return flat.reshape(flat.shape[:-1] + (rows, LANES))


def _unpack_flat(buf, ml):
    flat = buf.reshape(buf.shape[:-2] + (-1,))
    out, off = [], 0
    for _, _, _, n in _entries(ml):
        out.append(flat[..., off:off + n])
        off += _slot_size(n)
    return out


def _pack_shards(shards, ml, dtype, split_conv):
    pieces = []
    for slot, name, l, n in _entries(ml):
        a = shards[name][l]
        a = (a.T if name in TRANSPOSED else a).reshape(-1)
        if slot == 'conv_w_hi':
            a = a.astype(BF16).astype(F32) if split_conv else a
        elif slot == 'conv_w_lo':
            a = (a - a.astype(BF16).astype(F32)) if split_conv else jnp.zeros_like(a)
        pieces.append(a.astype(dtype))
    return _pack_flat(pieces, ml)


def _unpack_shards(buf, ml):
    out = {}
    for (slot, name, l, n), v in zip(_entries(ml), _unpack_flat(buf, ml)):
        if slot == 'conv_w_lo':
            continue
        layers, shape, axis = SHARDED[name]
        sshape = (shape[0] // N_DEV, shape[1]) if axis == 0 else (shape[0], shape[1] // N_DEV)
        out[(name, l)] = v.reshape(sshape[::-1]).T if name in TRANSPOSED else v.reshape(sshape)
    return out


def _to_full(blocks, shape, axis):
    if axis == 0:
        return blocks.reshape(shape)
    return blocks.reshape(N_DEV, shape[0], shape[1] // N_DEV).transpose(1, 0, 2).reshape(shape)


def _to_blocks(full, axis):
    r, c = full.shape
    if axis == 0:
        return full.reshape(N_DEV, -1)
    return full.reshape(r, N_DEV, c // N_DEV).transpose(1, 0, 2).reshape(N_DEV, -1)


def _unpack_gathered(gathered, ml):
    out, conv_hi = {}, None
    for (slot, name, l, n), v in zip(_entries(ml), _unpack_flat(gathered, ml)):
        layers, shape, axis = SHARDED[name]
        if name in TRANSPOSED:
            out[name] = v.reshape(-1, 16, LANES)
            continue
        full = _to_full(v, shape, axis)
        if slot == 'conv_w_hi':
            conv_hi = full
        elif slot == 'conv_w_lo':
            out[name] = conv_hi.astype(F32) + full.astype(F32)
        else:
            out[name] = full
    return out


def _pack_grads(grads, ml):
    pieces = []
    for slot, name, l, n in _entries(ml):
        if name in TRANSPOSED:
            blocks = jnp.concatenate(grads[name], axis=0).reshape(N_DEV, -1)
        else:
            blocks = _to_blocks(grads[name], SHARDED[name][2])
        if slot == 'conv_w_lo':
            blocks = jnp.zeros_like(blocks)
        pieces.append(blocks)
    return _pack_flat(pieces, ml)


def _my_pos():
    return lax.axis_index('x'), lax.axis_index('y'), lax.axis_index('c')


def _flip(v, bit):
    return 1 - v if bit else v


def all_gather_blocks(x):
    rows = x.shape[0]

    def body(x_ref, out_ref, send_sems, recv_sems, local_sem):
        mx, my, mc = _my_pos()
        me, sibling = (mx, my, mc), (mx, my, 1 - mc)
        chips = [(1 - mx, my), (mx, 1 - my), (1 - mx, 1 - my)]

        def block(px, py, pc):
            return out_ref.at[4 * px + 2 * py + pc]

        def copy(k, blk, to, src=None):
            return pltpu.make_async_remote_copy(
                src_ref=block(*blk) if src is None else src, dst_ref=block(*blk),
                send_sem=send_sems.at[k], recv_sem=recv_sems.at[k], device_id=to, device_id_type=MESH)

        mine = pltpu.make_async_copy(x_ref, block(*me), local_sem)
        mine.start()
        first = [copy(0, me, sibling, src=x_ref)]
        first += [copy(1 + j, me, (*chip, mc), src=x_ref) for j, chip in enumerate(chips)]
        for cp in first:
            cp.start()
        passed = [copy(4 + j, (*chip, mc), sibling) for j, chip in enumerate(chips)]
        for j, chip in enumerate(chips):
            copy(1 + j, (*chip, mc), me).wait_recv()
            passed[j].start()
        copy(0, sibling, me).wait_recv()
        for j, chip in enumerate(chips):
            copy(4 + j, (*chip, 1 - mc), me).wait_recv()
        for cp in first + passed:
            cp.wait_send()
        mine.wait()

    return pl.pallas_call(
        body, name='all_gather_weights',
        out_shape=jax.ShapeDtypeStruct((N_DEV, rows, LANES), x.dtype),
        in_specs=[pl.BlockSpec(memory_space=pl.ANY)], out_specs=pl.BlockSpec(memory_space=pl.ANY),
        scratch_shapes=[pltpu.SemaphoreType.DMA((7,)), pltpu.SemaphoreType.DMA((7,)), pltpu.SemaphoreType.DMA],
    )(x)


def exchange_blocks(g):
    def body(g_ref, out_ref, send_sems, recv_sems, local_sem):
        mx, my, mc = _my_pos()
        me = 4 * mx + 2 * my + mc
        mine = pltpu.make_async_copy(g_ref.at[me], out_ref.at[me], local_sem)
        mine.start()
        copies, arrivals = [], []
        for k in range(1, N_DEV):
            peer = (_flip(mx, k & 4), _flip(my, k & 2), _flip(mc, k & 1))
            pidx = 4 * peer[0] + 2 * peer[1] + peer[2]
            copies.append(pltpu.make_async_remote_copy(
                src_ref=g_ref.at[pidx], dst_ref=out_ref.at[me], send_sem=send_sems.at[k - 1],
                recv_sem=recv_sems.at[k - 1], device_id=peer, device_id_type=MESH))
            arrivals.append(pltpu.make_async_remote_copy(
                src_ref=g_ref.at[me], dst_ref=out_ref.at[pidx], send_sem=send_sems.at[k - 1],
                recv_sem=recv_sems.at[k - 1], device_id=peer, device_id_type=MESH))
        for cp in copies:
            cp.start()
        for cp in arrivals:
            cp.wait_recv()
        for cp in copies:
            cp.wait_send()
        mine.wait()

    return pl.pallas_call(
        body, name='exchange_grads',
        out_shape=jax.ShapeDtypeStruct(g.shape, g.dtype),
        in_specs=[pl.BlockSpec(memory_space=pl.ANY)], out_specs=pl.BlockSpec(memory_space=pl.ANY),
        scratch_shapes=[pltpu.SemaphoreType.DMA((7,)), pltpu.SemaphoreType.DMA((7,)), pltpu.SemaphoreType.DMA],
    )(g)


HBM_SPEC = pl.BlockSpec(memory_space=pltpu.HBM)
SEM_SPEC = pl.BlockSpec(memory_space=pltpu.SEMAPHORE)
DATAFLOW_EFFECT = pltpu.SideEffectType.DATAFLOW_SIDE_EFFECTING


def _direct_copies(src_ref, land_ref, send_sems, recv_sems, per_peer):
    mx, my, mc = _my_pos()
    me = 4 * mx + 2 * my + mc
    copies = []
    for k in range(1, N_DEV):
        peer = (_flip(mx, k & 4), _flip(my, k & 2), _flip(mc, k & 1))
        pidx = 4 * peer[0] + 2 * peer[1] + peer[2]
        copies.append(pltpu.make_async_remote_copy(
            src_ref=src_ref.at[pidx] if per_peer else src_ref, dst_ref=land_ref.at[me],
            send_sem=send_sems.at[k - 1], recv_sem=recv_sems.at[k - 1], device_id=peer, device_id_type=MESH))
    return copies


def send_start(src, per_peer, name):
    block = src.shape[1:] if per_peer else src.shape
    land_shape = (N_DEV,) + tuple(block)

    def body(src_ref, land_ref, send_sems, recv_sems, src_thru, land_thru, token):
        for cp in _direct_copies(src_ref, land_ref, send_sems, recv_sems, per_peer):
            cp.start()
        token[...] = jnp.zeros(token.shape, F32)

    send_sems, recv_sems, src_thru, land_thru, token = pl.pallas_call(
        body, name=name,
        out_shape=(pltpu.SemaphoreType.DMA((N_DEV - 1,)), pltpu.SemaphoreType.DMA((N_DEV - 1,)),
                   pltpu.HBM(src.shape, src.dtype), pltpu.HBM(land_shape, src.dtype),
                   jax.ShapeDtypeStruct((8, LANES), F32)),
        in_specs=(HBM_SPEC, HBM_SPEC),
        out_specs=(SEM_SPEC, SEM_SPEC, HBM_SPEC, HBM_SPEC, pl.BlockSpec(memory_space=pltpu.VMEM)),
        input_output_aliases={0: 2, 1: 3},
        compiler_params=pltpu.CompilerParams(has_side_effects=DATAFLOW_EFFECT),
    )(pltpu.with_memory_space_constraint(src, pltpu.HBM),
      pltpu.with_memory_space_constraint(lax.empty(land_shape, src.dtype), pltpu.HBM))
    return (send_sems, recv_sems, src_thru, land_thru), token[0, 0]


def send_wait(handles, after, per_peer, name):
    send_sems, recv_sems, src_thru, land_thru = handles

    def body(src_ref, land_ref, send_sems, recv_sems, after_ref, src_dead, got_ref):
        for cp in _direct_copies(src_ref, land_ref, send_sems, recv_sems, per_peer):
            cp.wait_send()
            cp.wait_recv()

    return pl.pallas_call(
        body, name=name,
        out_shape=(pltpu.HBM(src_thru.shape, src_thru.dtype), pltpu.HBM(land_thru.shape, land_thru.dtype)),
        in_specs=(HBM_SPEC, HBM_SPEC, SEM_SPEC, SEM_SPEC, pl.BlockSpec(memory_space=pl.ANY)),
        out_specs=(HBM_SPEC, HBM_SPEC), input_output_aliases={0: 0, 1: 1},
        compiler_params=pltpu.CompilerParams(has_side_effects=DATAFLOW_EFFECT),
    )(src_thru, land_thru, send_sems, recv_sems, after)[1]


def _adamw_math(w, g, m, v):
    m2 = ADAM_B1 * m + (1.0 - ADAM_B1) * g
    v2 = ADAM_B2 * v + (1.0 - ADAM_B2) * (g * g)
    m_hat = m2 / (1.0 - ADAM_B1 ** ADAM_STEP)
    v_hat = v2 / (1.0 - ADAM_B2 ** ADAM_STEP)
    delta = -ADAM_LR * (m_hat / (jnp.sqrt(v_hat) + ADAM_EPS) + ADAM_WD * w)
    return delta, m2, v2


def adamw_sharded(parts, w, m, v):
    rows = w.shape[0]
    rb = PACK_ROW_MULT

    def body(p_ref, w_ref, m_ref, v_ref, g_ref, d_ref, m2_ref, v2_ref):
        g = p_ref[0].astype(F32)
        for j in range(1, N_DEV):
            g = g + p_ref[j].astype(F32)
        delta, m2, v2 = _adamw_math(w_ref[...], g, m_ref[...], v_ref[...])
        g_ref[...] = g
        d_ref[...] = delta
        m2_ref[...] = m2
        v2_ref[...] = v2

    blk = pl.BlockSpec((rb, LANES), lambda i: (i, 0))
    sh = jax.ShapeDtypeStruct((rows, LANES), F32)
    return pl.pallas_call(
        body, name='adamw_sharded', grid=(rows // rb,),
        in_specs=[pl.BlockSpec((N_DEV, rb, LANES), lambda i: (0, i, 0)), blk, blk, blk],
        out_specs=[blk] * 4, out_shape=[sh] * 4, compiler_params=_cparams(('parallel',)),
    )(parts, w, m, v)


def allreduce_adamw_small(g, w, m, v):
    rows = g.shape[0]

    def body(g_ref, w_ref, m_ref, v_ref, gs_ref, d_ref, m2_ref, v2_ref, all_ref, send_sems, recv_sems):
        mx, my, mc = _my_pos()
        me, sibling = (mx, my, mc), (mx, my, 1 - mc)
        chips = [(1 - mx, my), (mx, 1 - my), (1 - mx, 1 - my)]

        def block(px, py, pc):
            return all_ref.at[4 * px + 2 * py + pc]

        def copy(k, blk, to, src=None):
            return pltpu.make_async_remote_copy(
                src_ref=block(*blk) if src is None else src, dst_ref=block(*blk),
                send_sem=send_sems.at[k], recv_sem=recv_sems.at[k], device_id=to, device_id_type=MESH)

        first = [copy(0, me, sibling, src=g_ref)]
        first += [copy(1 + j, me, (*chip, mc), src=g_ref) for j, chip in enumerate(chips)]
        for cp in first:
            cp.start()
        passed = [copy(4 + j, (*chip, mc), sibling) for j, chip in enumerate(chips)]
        for j, chip in enumerate(chips):
            copy(1 + j, (*chip, mc), me).wait_recv()
            passed[j].start()
        copy(0, sibling, me).wait_recv()
        for j, chip in enumerate(chips):
            copy(4 + j, (*chip, 1 - mc), me).wait_recv()
        for cp in first + passed:
            cp.wait_send()
        my_idx = 4 * mx + 2 * my + mc
        total = jnp.zeros((rows, LANES), F32)
        for j in range(N_DEV):
            total = total + jnp.where(my_idx == j, g_ref[...], all_ref[j])
        delta, m2, v2 = _adamw_math(w_ref[...], total, m_ref[...], v_ref[...])
        gs_ref[...] = total
        d_ref[...] = delta
        m2_ref[...] = m2
        v2_ref[...] = v2

    vm = pl.BlockSpec(memory_space=pltpu.VMEM)
    sh = jax.ShapeDtypeStruct((rows, LANES), F32)
    return pl.pallas_call(
        body, name='allreduce_adamw_small', in_specs=[vm] * 4, out_specs=[vm] * 4, out_shape=[sh] * 4,
        scratch_shapes=[pltpu.VMEM((N_DEV, rows, LANES), F32), pltpu.SemaphoreType.DMA((7,)),
                        pltpu.SemaphoreType.DMA((7,))],
    )(g, w, m, v)


def _small_pack(d):
    flat = jnp.concatenate([d[n].reshape(-1) for n in SMALL_NAMES])
    rows = -(-flat.shape[0] // (8 * LANES)) * 8
    return jnp.pad(flat, (0, rows * LANES - flat.shape[0])).reshape(rows, LANES)


def _small_unpack(buf, like):
    flat = buf.reshape(-1)
    out, off = {}, 0
    for n in SMALL_NAMES:
        sz = int(np.prod(like[n].shape))
        out[n] = flat[off:off + sz].reshape(like[n].shape)
        off += sz
    return out


def _rope_tables(positions):
    freqs = ROPE_THETA ** (-(jnp.arange(QK_ROPE // 2, dtype=F32) * 2.0 / QK_ROPE))
    ang = positions.astype(F32)[:, None] * freqs
    return jnp.tile(jnp.cos(ang), (1, MLA_HEADS)), jnp.tile(jnp.sin(ang), (1, MLA_HEADS))


def _layer_weights(gathered, carriers):
    wd = {}
    for name, full in gathered.items():
        if name == 'w_in':
            wd[name] = {seg: W(full[lo // 2:hi // 2].reshape(hi - lo, D_MODEL), c)
                        for (seg, lo, hi), c in zip(WIN_SEGS, carriers[name])}
        else:
            wd[name] = W(full, carriers[name])
    return wd


def _make_carriers(gathered):
    gathered, carriers = dict(gathered), {}
    for name, full in gathered.items():
        if name == 'w_in':
            carriers[name] = tuple(jnp.zeros((hi - lo, D_MODEL), F32) for _, lo, hi in WIN_SEGS)
        elif name == 'conv_w':
            gathered[name] = full.astype(BF16)
            carriers[name] = full - full.astype(BF16).astype(F32)
        else:
            carriers[name] = jnp.zeros(SHARDED[name][1], F32)
    return gathered, carriers


def _layer_small(small, ml):
    out = {}
    for n in SMALL_NAMES:
        l = ml if small[n].shape[0] == DEPTH else ml - 1
        if 0 <= l < small[n].shape[0]:
            out[n] = small[n][l]
    return out


def _train_step(x, positions, loss_target, weights, moms_m, moms_v):
    shards = {n: weights[n] for n in SHARDED}
    small = {n: weights[n] for n in SMALL_NAMES}
    me = 4 * lax.axis_index('x') + 2 * lax.axis_index('y') + lax.axis_index('c')
    cos, sin = _rope_tables(positions[0])

    own1 = _pack_shards(shards, 1, BF16, True)
    blocks0 = all_gather_blocks(_pack_shards(shards, 0, BF16, True))
    own1, blocks0 = lax.optimization_barrier((own1, blocks0))
    gather1, token = send_start(own1, False, 'gather_l1_start')
    gathered0, carriers0 = _make_carriers(_unpack_gathered(blocks0, 0))
    (x1, v_first), vjp0 = jax.vjp(
        lambda c, s, xx: _layer(0, xx, None, _layer_weights(gathered0, c), s, cos, sin),
        carriers0, _layer_small(small, 0), x[0] + token)

    blocks1 = send_wait(gather1, x1, False, 'gather_l1_wait')
    blocks1 = lax.dynamic_update_slice(blocks1, own1[None], (me, 0, 0))
    gathered1, carriers1 = _make_carriers(_unpack_gathered(blocks1, 1))
    y, vjp1 = jax.vjp(
        lambda c, s, xx, vf: _layer(1, xx, vf, _layer_weights(gathered1, c), s, cos, sin)[0],
        carriers1, _layer_small(small, 1), x1, v_first)

    dy, parts = _loss_call(y, loss_target[0])
    loss = lax.psum(jnp.sum(parts), ('x', 'y', 'c'))

    gw1, gs1, dx1, dvf = vjp1(dy)
    packed1 = _pack_grads(gw1, 1).astype(BF16)
    exchange1, token = send_start(packed1, True, 'grads_l1_start')
    gw0, gs0, g_x = vjp0((dx1 + token, dvf))
    parts1 = send_wait(exchange1, g_x, True, 'grads_l1_wait')
    parts1 = lax.dynamic_update_slice(parts1, lax.dynamic_slice_in_dim(packed1, me, 1, axis=0), (me, 0, 0))
    parts0 = exchange_blocks(_pack_grads(gw0, 0).astype(BF16))

    per_layer = []
    for ml, parts_ml in ((0, parts0), (1, parts1)):
        pack = lambda d: _pack_shards({n: d[n] for n in SHARDED}, ml, F32, False)
        per_layer.append([_unpack_shards(b, ml) for b in adamw_sharded(
            parts_ml, pack(weights), pack(moms_m), pack(moms_v))])
    sharded_out = []
    for i in range(4):
        both = {**per_layer[0][i], **per_layer[1][i]}
        sharded_out.append({n: jnp.stack([both[(n, l)] for l in range(SHARDED[n][0])]) for n in SHARDED})

    g_small = {}
    for n in SMALL_NAMES:
        per = [g[n] for g in (gs0, gs1) if n in g]
        g_small[n] = jnp.stack(per)
    small_out = [_small_unpack(b, small) for b in allreduce_adamw_small(
        _small_pack(g_small), _small_pack(small), _small_pack({n: moms_m[n] for n in SMALL_NAMES}),
        _small_pack({n: moms_v[n] for n in SMALL_NAMES}))]

    pick = lambda i: [sharded_out[i][n] if n in SHARDED else small_out[i][n] for n in WEIGHT_NAMES]
    return (loss, g_x[None], *pick(0), *pick(1), *pick(2), *pick(3))


def kernel(x, positions, attn_norm, w_in, mla_q_a_norm, mla_wq_b, mla_kv_a_norm, mla_wkv_b, mla_q_norm, mla_k_norm, mla_w_o, rwkv_mu, rwkv_w0, rwkv_w2, rwkv_a0, rwkv_a2, rwkv_g2, rwkv_k_k, rwkv_k_a, rwkv_r_k, rwkv_ln_w, rwkv_ln_b, rwkv_w_o, rwkv_v1, rwkv_v_mu, rwkv_v0, rwkv_v2, conv_w, conv_w_o, w_out, mlp_norm, w_up, w_down, loss_target, m_attn_norm, m_w_in, m_mla_q_a_norm, m_mla_wq_b, m_mla_kv_a_norm, m_mla_wkv_b, m_mla_q_norm, m_mla_k_norm, m_mla_w_o, m_rwkv_mu, m_rwkv_w0, m_rwkv_w2, m_rwkv_a0, m_rwkv_a2, m_rwkv_g2, m_rwkv_k_k, m_rwkv_k_a, m_rwkv_r_k, m_rwkv_ln_w, m_rwkv_ln_b, m_rwkv_w_o, m_rwkv_v1, m_rwkv_v_mu, m_rwkv_v0, m_rwkv_v2, m_conv_w, m_conv_w_o, m_w_out, m_mlp_norm, m_w_up, m_w_down, v_attn_norm, v_w_in, v_mla_q_a_norm, v_mla_wq_b, v_mla_kv_a_norm, v_mla_wkv_b, v_mla_q_norm, v_mla_k_norm, v_mla_w_o, v_rwkv_mu, v_rwkv_w0, v_rwkv_w2, v_rwkv_a0, v_rwkv_a2, v_rwkv_g2, v_rwkv_k_k, v_rwkv_k_a, v_rwkv_r_k, v_rwkv_ln_w, v_rwkv_ln_b, v_rwkv_w_o, v_rwkv_v1, v_rwkv_v_mu, v_rwkv_v0, v_rwkv_v2, v_conv_w, v_conv_w_o, v_w_out, v_mlp_norm, v_w_up, v_w_down):
    args = locals()
    weights = {n: args[n] for n in WEIGHT_NAMES}
    moms_m = {n: args['m_' + n] for n in WEIGHT_NAMES}
    moms_v = {n: args['v_' + n] for n in WEIGHT_NAMES}
    return _train_step(x, positions, loss_target, weights, moms_m, moms_v)
```

```python
import functools

import numpy as np
import jax
import jax.numpy as jnp
from jax import lax
from jax.experimental import pallas as pl
from jax.experimental.pallas import tpu as pltpu

F32 = jnp.float32
BF16 = jnp.bfloat16

N_DEV = 8
LANES = 128
D_MODEL = 1024
DEPTH = 2
MLA_HEADS = 8
QK_NOPE = 64
QK_ROPE = 32
QK_HEAD = QK_NOPE + QK_ROPE
V_HEAD = 64
Q_LORA = 384
KV_LORA = 256
ROPE_THETA = 10000.0
RW_HEADS = 4
RW_N = 64
RW_WIDTH = RW_HEADS * RW_N
MV_LORA = 32
GN_EPS = 64e-5
CONV_WIDTH = 256
D_FF = 4 * D_MODEL
NORM_EPS = 1e-6
ADAM_LR = 0.001
ADAM_B1 = 0.9
ADAM_B2 = 0.999
ADAM_EPS = 1e-08
ADAM_WD = 0.01
ADAM_STEP = 10

VMEM_LIMIT = 56 * 1024 * 1024
MESH = pl.DeviceIdType.MESH

WEIGHT_NAMES = ['attn_norm', 'w_in', 'mla_q_a_norm', 'mla_wq_b', 'mla_kv_a_norm', 'mla_wkv_b', 'mla_q_norm',
                'mla_k_norm', 'mla_w_o', 'rwkv_mu', 'rwkv_w0', 'rwkv_w2', 'rwkv_a0', 'rwkv_a2', 'rwkv_g2',
                'rwkv_k_k', 'rwkv_k_a', 'rwkv_r_k', 'rwkv_ln_w', 'rwkv_ln_b', 'rwkv_w_o', 'rwkv_v1',
                'rwkv_v_mu', 'rwkv_v0', 'rwkv_v2', 'conv_w', 'conv_w_o', 'w_out', 'mlp_norm', 'w_up', 'w_down']

SHARDED = {
    'w_in': (2, (1024, 5536), 1), 'mla_wq_b': (2, (384, 768), 1), 'mla_wkv_b': (2, (256, 1024), 1),
    'mla_w_o': (2, (512, 1024), 1), 'rwkv_w2': (2, (64, 256), 1), 'rwkv_a2': (2, (64, 256), 1),
    'rwkv_g2': (2, (128, 256), 1), 'rwkv_w_o': (2, (256, 1024), 1), 'conv_w': (2, (3, 256), 1),
    'conv_w_o': (2, (256, 1024), 1), 'w_out': (2, (1024, 1024), 0), 'w_up': (2, (1024, 4096), 1),
    'w_down': (2, (4096, 1024), 0), 'rwkv_v1': (1, (1024, 32), 0), 'rwkv_v2': (1, (32, 256), 1),
}
SMALL_NAMES = [n for n in WEIGHT_NAMES if n not in SHARDED]
TRANSPOSED = ('w_in',)
WIN_SEGS = (('gates', 0, 3072), ('cq', 3072, 3456), ('ckv', 3456, 3712), ('kpe', 3712, 3744), ('rkv', 3744, 4512),
            ('xwa', 4512, 4640), ('xg', 4640, 4768), ('conv', 4768, 5536))
PACK_ROW_MULT = 512


def _cparams(sem=None, **kw):
    if sem is not None:
        kw['dimension_semantics'] = sem
    return pltpu.CompilerParams(vmem_limit_bytes=VMEM_LIMIT, **kw)


def _pick(n, cands):
    for c in cands:
        if n % c == 0:
            return c
    raise ValueError(f'no tile for {n}')


def _mm_nn(a, b, add=None, name='mm_nn'):
    M, K = a.shape
    N = b.shape[1]
    tm = _pick(M, (1024, 512, 256, 128))
    tn = _pick(N, (512, 384, 256, 128))
    tk = _pick(K, (1024, 512, 384, 256, 128))
    nk = K // tk
    has_add = add is not None

    def body(*refs):
        if has_add:
            a_ref, b_ref, add_ref, o_ref, acc_ref = refs
        else:
            a_ref, b_ref, o_ref, acc_ref = refs
        kk = pl.program_id(2)
        part = jnp.dot(a_ref[...].astype(BF16), b_ref[...].astype(BF16), preferred_element_type=F32)

        @pl.when(kk == 0)
        def _():
            acc_ref[...] = part

        @pl.when(kk > 0)
        def _():
            acc_ref[...] += part

        @pl.when(kk == nk - 1)
        def _():
            if has_add:
                o_ref[...] = acc_ref[...] + add_ref[...]
            else:
                o_ref[...] = acc_ref[...]

    in_specs = [pl.BlockSpec((tm, tk), lambda i, j, k: (i, k)), pl.BlockSpec((tk, tn), lambda i, j, k: (k, j))]
    args = [a, b]
    if has_add:
        in_specs.append(pl.BlockSpec((tm, tn), lambda i, j, k: (i, j)))
        args.append(add)
    return pl.pallas_call(
        body, name=name, grid=(M // tm, N // tn, nk), in_specs=in_specs,
        out_specs=pl.BlockSpec((tm, tn), lambda i, j, k: (i, j)),
        out_shape=jax.ShapeDtypeStruct((M, N), F32),
        scratch_shapes=[pltpu.VMEM((tm, tn), F32)],
        compiler_params=_cparams(('parallel', 'parallel', 'arbitrary')),
    )(*args)


def _mm_nt(a, b, add=None, name='mm_nt'):
    M, N = a.shape
    K = b.shape[0]
    tm = _pick(M, (1024, 512, 256, 128))
    tk = _pick(K, (512, 384, 256, 128))
    tn = _pick(N, (1024, 512, 384, 256, 128))
    nn = N // tn
    has_add = add is not None

    def body(*refs):
        if has_add:
            a_ref, b_ref, add_ref, o_ref, acc_ref = refs
        else:
            a_ref, b_ref, o_ref, acc_ref = refs
        kk = pl.program_id(2)
        part = lax.dot_general(a_ref[...].astype(BF16), b_ref[...].astype(BF16), (((1,), (1,)), ((), ())),
                               preferred_element_type=F32)

        @pl.when(kk == 0)
        def _():
            acc_ref[...] = part

        @pl.when(kk > 0)
        def _():
            acc_ref[...] += part

        @pl.when(kk == nn - 1)
        def _():
            if has_add:
                o_ref[...] = acc_ref[...] + add_ref[...]
            else:
                o_ref[...] = acc_ref[...]

    in_specs = [pl.BlockSpec((tm, tn), lambda i, j, k: (i, k)), pl.BlockSpec((tk, tn), lambda i, j, k: (j, k))]
    args = [a, b]
    if has_add:
        in_specs.append(pl.BlockSpec((tm, tk), lambda i, j, k: (i, j)))
        args.append(add)
    return pl.pallas_call(
        body, name=name, grid=(M // tm, K // tk, nn), in_specs=in_specs,
        out_specs=pl.BlockSpec((tm, tk), lambda i, j, k: (i, j)),
        out_shape=jax.ShapeDtypeStruct((M, K), F32),
        scratch_shapes=[pltpu.VMEM((tm, tk), F32)],
        compiler_params=_cparams(('parallel', 'parallel', 'arbitrary')),
    )(*args)


def _mm_tn(a, b, name='mm_tn'):
    M, K = a.shape
    N = b.shape[1]
    tm = _pick(M, (1024, 512, 256, 128))
    tk = _pick(K, (512, 384, 256, 128))
    tn = _pick(N, (512, 384, 256, 128))
    nm = M // tm

    def body(a_ref, b_ref, o_ref, acc_ref):
        mm = pl.program_id(2)
        part = lax.dot_general(a_ref[...].astype(BF16), b_ref[...].astype(BF16), (((0,), (0,)), ((), ())),
                               preferred_element_type=F32)

        @pl.when(mm == 0)
        def _():
            acc_ref[...] = part

        @pl.when(mm > 0)
        def _():
            acc_ref[...] += part

        @pl.when(mm == nm - 1)
        def _():
            o_ref[...] = acc_ref[...]

    return pl.pallas_call(
        body, name=name, grid=(K // tk, N // tn, nm),
        in_specs=[pl.BlockSpec((tm, tk), lambda i, j, m: (m, i)), pl.BlockSpec((tm, tn), lambda i, j, m: (m, j))],
        out_specs=pl.BlockSpec((tk, tn), lambda i, j, m: (i, j)),
        out_shape=jax.ShapeDtypeStruct((K, N), F32),
        scratch_shapes=[pltpu.VMEM((tk, tn), F32)],
        compiler_params=_cparams(('parallel', 'parallel', 'arbitrary')),
    )(a, b)


@functools.partial(jax.custom_vjp, nondiff_argnums=(4,))
def _linear_add(a, wb, wc, add, name):
    return _mm_nn(a, wb, add, name=name + '_f')


def _linear_add_fwd(a, wb, wc, add, name):
    return _mm_nn(a, wb, add, name=name + '_f'), (a, wb)


def _linear_add_bwd(name, res, dy):
    a, wb = res
    return _mm_nt(dy, wb, name=name + '_da'), None, _mm_tn(a, dy, name=name + '_dw'), dy


_linear_add.defvjp(_linear_add_fwd, _linear_add_bwd)


@functools.partial(jax.custom_vjp, nondiff_argnums=(3,))
def _multi_linear(a, wbs, wcs, name):
    return tuple(_mm_nn(a, wb, name=f'{name}_f{i}') for i, wb in enumerate(wbs))


def _multi_linear_fwd(a, wbs, wcs, name):
    return _multi_linear(a, wbs, wcs, name), (a, wbs)


def _multi_linear_bwd(name, res, dys):
    a, wbs = res
    da = None
    for i, (dy, wb) in enumerate(zip(dys, wbs)):
        da = _mm_nt(dy, wb, add=da, name=f'{name}_da{i}')
    dws = tuple(_mm_tn(a, dy, name=f'{name}_dw{i}') for i, dy in enumerate(dys))
    return da, None, dws


_multi_linear.defvjp(_multi_linear_fwd, _multi_linear_bwd)


class W:
    def __init__(self, b, c):
        self.b, self.c = b, c

    def map(self, fn):
        return W(fn(self.b), fn(self.c))


def _wcat(ws, axis):
    return W(jnp.concatenate([w.b for w in ws], axis), jnp.concatenate([w.c for w in ws], axis))


def linear(a, w, add=None, name='lin'):
    if add is None:
        return _multi_linear(a, (w.b,), (w.c,), name)[0]
    return _linear_add(a, w.b, w.c, add, name)


def multi_linear(a, ws, name):
    return _multi_linear(a, tuple(w.b for w in ws), tuple(w.c for w in ws), name)


@functools.partial(jax.custom_vjp, nondiff_argnums=(3,))
def _multi_linear_t(a, wbs, wcs, name):
    return tuple(_mm_nt(a, wb, name=f'{name}_f{i}') for i, wb in enumerate(wbs))


def _multi_linear_t_fwd(a, wbs, wcs, name):
    return _multi_linear_t(a, wbs, wcs, name), (a, wbs)


def _multi_linear_t_bwd(name, res, dys):
    a, wbs = res
    da = None
    for i, (dy, wb) in enumerate(zip(dys, wbs)):
        da = _mm_nn(dy, wb, add=da, name=f'{name}_da{i}')
    dws = tuple(_mm_tn(dy, a, name=f'{name}_dw{i}') for i, dy in enumerate(dys))
    return da, None, dws


_multi_linear_t.defvjp(_multi_linear_t_fwd, _multi_linear_t_bwd)


def multi_linear_t(a, ws, name):
    return _multi_linear_t(a, tuple(w.b for w in ws), tuple(w.c for w in ws), name)


def ROW(diff=True, pieces=None):
    return ('row', diff, pieces)


def FULL(diff=True):
    return ('full', diff, None)


def _load_args(refs, specs):
    args, amap = [], []
    for i, (ref, (kind, diff, pieces)) in enumerate(zip(refs, specs)):
        if pieces is None:
            args.append(ref[...])
            amap.append((i, None))
        else:
            for (s, w) in pieces:
                args.append(ref[:, s:s + w])
                amap.append((i, (s, w)))
    return args, amap


def _stage_in_specs(ins, specs, tb):
    out = []
    for a, (kind, _, _) in zip(ins, specs):
        if kind == 'row':
            out.append(pl.BlockSpec((tb, a.shape[1]), lambda i: (i, 0)))
        else:
            out.append(pl.BlockSpec(a.shape, lambda i: (0, 0)))
    return out


def _stage_fwd(fn, ins, specs, out_widths, name, tb):
    T = [a for a, s in zip(ins, specs) if s[0] == 'row'][0].shape[0]
    tb = min(tb, T)
    n_in = len(ins)

    def body(*refs):
        args, _ = _load_args(refs[:n_in], specs)
        outs = fn(*args)
        for o_ref, o in zip(refs[n_in:], outs):
            o_ref[...] = o

    return pl.pallas_call(
        body, name=name + '_f', grid=(T // tb,), in_specs=_stage_in_specs(ins, specs, tb),
        out_specs=[pl.BlockSpec((tb, w), lambda i: (i, 0)) for w in out_widths],
        out_shape=[jax.ShapeDtypeStruct((T, w), F32) for w in out_widths],
        compiler_params=_cparams(('parallel',)),
    )(*ins)


def _stage_bwd(fn, ins, specs, out_widths, douts, name, tb):
    T = [a for a, s in zip(ins, specs) if s[0] == 'row'][0].shape[0]
    tb = min(tb, T)
    n_in, n_out = len(ins), len(out_widths)
    diff_inputs = [i for i, s in enumerate(specs) if s[1]]

    def body(*refs):
        in_refs, dout_refs, g_refs = refs[:n_in], refs[n_in:n_in + n_out], refs[n_in + n_out:]
        args, amap = _load_args(in_refs, specs)
        didx = [j for j, (i, _) in enumerate(amap) if specs[i][1]]

        def f(*dv):
            full = list(args)
            for j, v in zip(didx, dv):
                full[j] = v
            return tuple(fn(*full))

        _, vjp = jax.vjp(f, *[args[j] for j in didx])
        gs = vjp(tuple(d[...] for d in dout_refs))
        gmap = {j: g for j, g in zip(didx, gs)}
        first = pl.program_id(0) == 0
        for g_ref, i in zip(g_refs, diff_inputs):
            kind, _, pieces = specs[i]
            js = [j for j, (ii, _) in enumerate(amap) if ii == i]
            if kind == 'row':
                if pieces is None:
                    g_ref[...] = gmap[js[0]]
                else:
                    if sum(w for _, w in pieces) != ins[i].shape[1]:
                        g_ref[...] = jnp.zeros(g_ref.shape, F32)
                    for j in js:
                        s, w = amap[j][1]
                        g_ref[:, s:s + w] = gmap[j]
            else:
                @pl.when(first)
                def _(g_ref=g_ref):
                    g_ref[...] = jnp.zeros(g_ref.shape, F32)

                g_ref[...] += gmap[js[0]]

    in_specs = _stage_in_specs(ins, specs, tb) + [pl.BlockSpec((tb, w), lambda i: (i, 0)) for w in out_widths]
    out_specs, out_shape = [], []
    for i in diff_inputs:
        a = ins[i]
        if specs[i][0] == 'row':
            out_specs.append(pl.BlockSpec((tb, a.shape[1]), lambda i: (i, 0)))
        else:
            out_specs.append(pl.BlockSpec(a.shape, lambda i: (0, 0)))
        out_shape.append(jax.ShapeDtypeStruct(a.shape, F32))
    return pl.pallas_call(
        body, name=name + '_b', grid=(T // tb,), in_specs=in_specs, out_specs=out_specs, out_shape=out_shape,
        compiler_params=_cparams(('arbitrary',)),
    )(*ins, *douts)


def stage_op(fn, specs, out_widths, name, tb=256):
    n = len(specs)
    diff_inputs = [i for i, s in enumerate(specs) if s[1]]

    @jax.custom_vjp
    def op(*ins):
        return tuple(_stage_fwd(fn, ins, specs, out_widths, name, tb))

    def op_fwd(*ins):
        return op(*ins), ins

    def op_bwd(ins, douts):
        gs = _stage_bwd(fn, ins, specs, out_widths, douts, name, tb)
        res = [None] * n
        for i, g in zip(diff_inputs, gs):
            res[i] = g
        return tuple(res)

    op.defvjp(op_fwd, op_bwd)
    return op


@jax.custom_vjp
def bdot(x, w):
    return jnp.dot(x.astype(BF16), w.astype(BF16), preferred_element_type=F32)


def _bdot_fwd(x, w):
    return bdot(x, w), (x, w)


def _bdot_bwd(res, dy):
    x, w = res
    dyb = dy.astype(BF16)
    dx = lax.dot_general(dyb, w.astype(BF16), (((1,), (1,)), ((), ())), preferred_element_type=F32)
    dw = lax.dot_general(x.astype(BF16), dyb, (((0,), (0,)), ((), ())), preferred_element_type=F32)
    return dx, dw


bdot.defvjp(_bdot_fwd, _bdot_bwd)


def _sdot_raw(x, c):
    hi = x.astype(BF16)
    r1 = x - hi.astype(F32)
    mid = r1.astype(BF16)
    lo = (r1 - mid.astype(F32)).astype(BF16)
    d = lambda u: jnp.dot(u, c, preferred_element_type=F32)
    return d(hi) + d(mid) + d(lo)


@jax.custom_vjp
def sdot(x, c, ct):
    return _sdot_raw(x, c)


def _sdot_fwd(x, c, ct):
    return _sdot_raw(x, c), (c, ct)


def _sdot_bwd(res, dy):
    c, ct = res
    return _sdot_raw(dy, ct), None, None


sdot.defvjp(_sdot_fwd, _sdot_bwd)


def _sigmoid(x):
    return 1.0 / (1.0 + jnp.exp(-x))


def _rms(x, g):
    return x * lax.rsqrt(jnp.mean(x * x, axis=-1, keepdims=True) + NORM_EPS) * g


def rmsnorm(x, g, name):
    op = stage_op(lambda xv, gv: (_rms(xv, gv),), [ROW(), FULL()], [x.shape[1]], name)
    return op(x, g.reshape(1, -1))[0]


def _shift_down(x, rows):
    return jnp.where(rows == 0, 0.0, pltpu.roll(x, 1, 0))


def _shift_up(x, rows, T):
    return jnp.where(rows == T - 1, 0.0, pltpu.roll(x, T - 1, 0))


def _tshift_fwd_call(x, mu, name):
    T, C = x.shape

    def body(x_ref, mu_ref, o_ref):
        xv = x_ref[...]
        rows = lax.broadcasted_iota(jnp.int32, xv.shape, 0)
        o_ref[...] = xv + (_shift_down(xv, rows) - xv) * mu_ref[...]

    return pl.pallas_call(
        body, name=name + '_f', grid=(C // LANES,),
        in_specs=[pl.BlockSpec((T, LANES), lambda j: (0, j)), pl.BlockSpec((1, LANES), lambda j: (0, j))],
        out_specs=pl.BlockSpec((T, LANES), lambda j: (0, j)), out_shape=jax.ShapeDtypeStruct((T, C), F32),
        compiler_params=_cparams(('parallel',)),
    )(x, mu)


def _tshift_bwd_call(x, mu, dy, name):
    T, C = x.shape

    def body(x_ref, mu_ref, dy_ref, dx_ref, dmu_ref):
        xv, d = x_ref[...], dy_ref[...]
        rows = lax.broadcasted_iota(jnp.int32, xv.shape, 0)
        z = d * mu_ref[...]
        dx_ref[...] = d - z + _shift_up(z, rows, T)
        dmu_ref[...] = jnp.sum(d * (_shift_down(xv, rows) - xv), axis=0, keepdims=True)

    return pl.pallas_call(
        body, name=name + '_b', grid=(C // LANES,),
        in_specs=[pl.BlockSpec((T, LANES), lambda j: (0, j)), pl.BlockSpec((1, LANES), lambda j: (0, j)),
                  pl.BlockSpec((T, LANES), lambda j: (0, j))],
        out_specs=[pl.BlockSpec((T, LANES), lambda j: (0, j)), pl.BlockSpec((1, LANES), lambda j: (0, j))],
        out_shape=[jax.ShapeDtypeStruct((T, C), F32), jax.ShapeDtypeStruct((1, C), F32)],
        compiler_params=_cparams(('parallel',)),
    )(x, mu, dy)


@functools.partial(jax.custom_vjp, nondiff_argnums=(2,))
def token_shift_mix(x, mu, name):
    return _tshift_fwd_call(x, mu, name)


def _tsm_fwd(x, mu, name):
    return _tshift_fwd_call(x, mu, name), (x, mu)


def _tsm_bwd(name, res, dy):
    x, mu = res
    dx, dmu = _tshift_bwd_call(x, mu, dy, name)
    return dx, dmu


token_shift_mix.defvjp(_tsm_fwd, _tsm_bwd)


def _conv_specs(T):
    nb = CONV_WIDTH // LANES
    return [pl.BlockSpec((T, LANES), lambda j: (0, j)), pl.BlockSpec((T, LANES), lambda j: (0, nb + j)),
            pl.BlockSpec((T, LANES), lambda j: (0, 2 * nb + j)), pl.BlockSpec((3, LANES), lambda j: (0, j))]


def _conv_fwd_call(cv, w, name):
    T = cv.shape[0]

    def body(b_ref, c_ref, x_ref, w_ref, o_ref):
        u = c_ref[...] * x_ref[...]
        rows = lax.broadcasted_iota(jnp.int32, u.shape, 0)
        u1 = _shift_down(u, rows)
        u2 = _shift_down(u1, rows)
        o_ref[...] = b_ref[...] * (w_ref[0:1, :] * u2 + w_ref[1:2, :] * u1 + w_ref[2:3, :] * u)

    return pl.pallas_call(
        body, name=name + '_f', grid=(CONV_WIDTH // LANES,), in_specs=_conv_specs(T),
        out_specs=pl.BlockSpec((T, LANES), lambda j: (0, j)),
        out_shape=jax.ShapeDtypeStruct((T, CONV_WIDTH), F32), compiler_params=_cparams(('parallel',)),
    )(cv, cv, cv, w)


def _conv_bwd_call(cv, w, do, name):
    T = cv.shape[0]

    def body(b_ref, c_ref, x_ref, w_ref, do_ref, db_ref, dc_ref, dx_ref, dw_ref):
        c, x, d = c_ref[...], x_ref[...], do_ref[...]
        u = c * x
        rows = lax.broadcasted_iota(jnp.int32, u.shape, 0)
        u1 = _shift_down(u, rows)
        u2 = _shift_down(u1, rows)
        w0, w1, w2 = w_ref[0:1, :], w_ref[1:2, :], w_ref[2:3, :]
        db_ref[...] = d * (w0 * u2 + w1 * u1 + w2 * u)
        dy = d * b_ref[...]
        dy1 = _shift_up(dy, rows, T)
        dy2 = _shift_up(dy1, rows, T)
        du = w2 * dy + w1 * dy1 + w0 * dy2
        dc_ref[...] = du * x
        dx_ref[...] = du * c
        dw_ref[0:1, :] = jnp.sum(dy * u2, axis=0, keepdims=True)
        dw_ref[1:2, :] = jnp.sum(dy * u1, axis=0, keepdims=True)
        dw_ref[2:3, :] = jnp.sum(dy * u, axis=0, keepdims=True)

    blk = pl.BlockSpec((T, LANES), lambda j: (0, j))
    sh = jax.ShapeDtypeStruct((T, CONV_WIDTH), F32)
    return pl.pallas_call(
        body, name=name + '_b', grid=(CONV_WIDTH // LANES,), in_specs=_conv_specs(T) + [blk],
        out_specs=[blk, blk, blk, pl.BlockSpec((3, LANES), lambda j: (0, j))],
        out_shape=[sh, sh, sh, jax.ShapeDtypeStruct((3, CONV_WIDTH), F32)],
        compiler_params=_cparams(('parallel',)),
    )(cv, cv, cv, w, do)


@functools.partial(jax.custom_vjp, nondiff_argnums=(2,))
def short_conv(cv, w, name):
    return _conv_fwd_call(cv, w, name)


def _sc_fwd(cv, w, name):
    return _conv_fwd_call(cv, w, name), (cv, w)


def _sc_bwd(name, res, do):
    cv, w = res
    db, dc, dx, dw = _conv_bwd_call(cv, w, do, name)
    return jnp.concatenate([db, dc, dx], axis=1), dw


short_conv.defvjp(_sc_fwd, _sc_bwd)


ATT_SCALE = QK_HEAD ** -0.5
NPAIR = MLA_HEADS // 2


def _att_bq(T):
    return min(256, T)


def _att_masks(pair, j):
    lane = lax.broadcasted_iota(jnp.int32, (1, LANES), 1)
    mask_n = (lane // QK_NOPE) == j
    mask_r = (lane // (QK_ROPE // 2)) == (2 * pair + j)
    return mask_n, mask_r


def _att_probs(qcat, kcat, row0, stop):
    s = lax.dot_general(qcat, kcat, (((1,), (1,)), ((), ())), preferred_element_type=F32) * ATT_SCALE
    r = row0 + lax.broadcasted_iota(jnp.int32, s.shape, 0)
    c = lax.broadcasted_iota(jnp.int32, s.shape, 1)
    s = jnp.where(c <= r, s, -jnp.inf)
    e = jnp.exp(s - jnp.max(s, axis=-1, keepdims=True))
    return e / jnp.sum(e, axis=-1, keepdims=True)


def _att_in_specs(T):
    blk = lambda f: pl.BlockSpec((T, LANES), f)
    return [blk(lambda p: (0, p)), blk(lambda p: (0, 0)), blk(lambda p: (0, 0)),
            blk(lambda p: (0, p)), blk(lambda p: (0, 0)), blk(lambda p: (0, 0)), blk(lambda p: (0, p))]


def _att_fwd_call(qn, q1, q2, kn, k1, k2, v, name):
    T = qn.shape[0]
    bq = _att_bq(T)

    def body(qn_ref, q1_ref, q2_ref, kn_ref, k1_ref, k2_ref, v_ref, o_ref):
        pair = pl.program_id(0)
        for i in range(T // bq):
            r0, stop = i * bq, (i + 1) * bq
            kcat = jnp.concatenate([kn_ref[0:stop, :], k1_ref[0:stop, :], k2_ref[0:stop, :]], axis=1).astype(BF16)
            vb = v_ref[0:stop, :].astype(BF16)
            outs = []
            for j in range(2):
                mask_n, mask_r = _att_masks(pair, j)
                qcat = jnp.concatenate([jnp.where(mask_n, qn_ref[r0:stop, :], 0.0),
                                        jnp.where(mask_r, q1_ref[r0:stop, :], 0.0),
                                        jnp.where(mask_r, q2_ref[r0:stop, :], 0.0)], axis=1).astype(BF16)
                p = _att_probs(qcat, kcat, r0, stop)
                outs.append(jnp.dot(p.astype(BF16), vb, preferred_element_type=F32))
            mask_n0, _ = _att_masks(pair, 0)
            o_ref[r0:stop, :] = jnp.where(mask_n0, outs[0], outs[1])

    return pl.pallas_call(
        body, name=name + '_f', grid=(NPAIR,), in_specs=_att_in_specs(T),
        out_specs=pl.BlockSpec((T, LANES), lambda p: (0, p)),
        out_shape=jax.ShapeDtypeStruct((T, MLA_HEADS * V_HEAD), F32), compiler_params=_cparams(('parallel',)),
    )(qn, q1, q2, kn, k1, k2, v)


def _att_bwd_call(qn, q1, q2, kn, k1, k2, v, o, do, name):
    T = qn.shape[0]
    bq = _att_bq(T)

    def body(qn_ref, q1_ref, q2_ref, kn_ref, k1_ref, k2_ref, v_ref, o_ref, do_ref,
             dqn_ref, dq1_ref, dq2_ref, dkn_ref, dk1_ref, dk2_ref, dv_ref, dk_acc, dv_acc):
        pair = pl.program_id(0)

        @pl.when(pair == 0)
        def _():
            dq1_ref[...] = jnp.zeros(dq1_ref.shape, F32)
            dq2_ref[...] = jnp.zeros(dq2_ref.shape, F32)
            dk1_ref[...] = jnp.zeros(dk1_ref.shape, F32)
            dk2_ref[...] = jnp.zeros(dk2_ref.shape, F32)

        dk_acc[...] = jnp.zeros(dk_acc.shape, F32)
        dv_acc[...] = jnp.zeros(dv_acc.shape, F32)
        for i in range(T // bq):
            r0, stop = i * bq, (i + 1) * bq
            kcat = jnp.concatenate([kn_ref[0:stop, :], k1_ref[0:stop, :], k2_ref[0:stop, :]], axis=1).astype(BF16)
            vb = v_ref[0:stop, :].astype(BF16)
            dqn = jnp.zeros((bq, LANES), F32)
            for j in range(2):
                mask_n, mask_r = _att_masks(pair, j)
                qcat = jnp.concatenate([jnp.where(mask_n, qn_ref[r0:stop, :], 0.0),
                                        jnp.where(mask_r, q1_ref[r0:stop, :], 0.0),
                                        jnp.where(mask_r, q2_ref[r0:stop, :], 0.0)], axis=1).astype(BF16)
                p = _att_probs(qcat, kcat, r0, stop)
                dom = jnp.where(mask_n, do_ref[r0:stop, :], 0.0)
                delta = jnp.sum(dom * o_ref[r0:stop, :], axis=-1, keepdims=True)
                domb = dom.astype(BF16)
                dp = lax.dot_general(domb, vb, (((1,), (1,)), ((), ())), preferred_element_type=F32)
                ds = (p * (dp - delta) * ATT_SCALE).astype(BF16)
                dqc = jnp.dot(ds, kcat, preferred_element_type=F32)
                dqn = dqn + jnp.where(mask_n, dqc[:, 0:LANES], 0.0)
                dq1_ref[r0:stop, :] += jnp.where(mask_r, dqc[:, LANES:2 * LANES], 0.0)
                dq2_ref[r0:stop, :] += jnp.where(mask_r, dqc[:, 2 * LANES:3 * LANES], 0.0)
                dk_acc[0:stop, :] += lax.dot_general(ds, qcat, (((0,), (0,)), ((), ())),
                                                     preferred_element_type=F32)
                dv_acc[0:stop, :] += lax.dot_general(p.astype(BF16), domb, (((0,), (0,)), ((), ())),
                                                     preferred_element_type=F32)
            dqn_ref[r0:stop, :] = dqn
        dkn_ref[...] = dk_acc[:, 0:LANES]
        dk1_ref[...] += dk_acc[:, LANES:2 * LANES]
        dk2_ref[...] += dk_acc[:, 2 * LANES:3 * LANES]
        dv_ref[...] = dv_acc[...]

    per_pair = pl.BlockSpec((T, LANES), lambda p: (0, p))
    shared = pl.BlockSpec((T, LANES), lambda p: (0, 0))
    wide = jax.ShapeDtypeStruct((T, MLA_HEADS * QK_NOPE), F32)
    narrow = jax.ShapeDtypeStruct((T, LANES), F32)
    return pl.pallas_call(
        body, name=name + '_b', grid=(NPAIR,), in_specs=_att_in_specs(T) + [per_pair, per_pair],
        out_specs=[per_pair, shared, shared, per_pair, shared, shared, per_pair],
        out_shape=[wide, narrow, narrow, wide, narrow, narrow, wide],
        scratch_shapes=[pltpu.VMEM((T, 3 * LANES), F32), pltpu.VMEM((T, LANES), F32)],
        compiler_params=_cparams(('arbitrary',)),
    )(qn, q1, q2, kn, k1, k2, v, o, do)


@functools.partial(jax.custom_vjp, nondiff_argnums=(7,))
def attention(qn, q1, q2, kn, k1, k2, v, name):
    return _att_fwd_call(qn, q1, q2, kn, k1, k2, v, name)


def _attn_fwd(qn, q1, q2, kn, k1, k2, v, name):
    o = _att_fwd_call(qn, q1, q2, kn, k1, k2, v, name)
    return o, (qn, q1, q2, kn, k1, k2, v, o)


def _attn_bwd(name, res, do):
    return tuple(_att_bwd_call(*res, do, name))


attention.defvjp(_attn_fwd, _attn_bwd)


SCAN_CHUNK = 64
SCAN_UNROLL = 4


def _block_ones(n, seg):
    i = np.arange(n)
    return (i[:, None] // seg == i[None, :] // seg).astype(np.float32)


def _scan_diag():
    i = np.arange(RW_WIDTH)
    return jnp.asarray((np.arange(RW_N)[:, None] == (i[None, :] % RW_N)).astype(np.float32))


def _head_rowsum(x):
    low = lax.broadcasted_iota(jnp.int32, (1, LANES), 1) < RW_N
    tiles = []
    for j in range(RW_WIDTH // LANES):
        xt = x[:, j * LANES:(j + 1) * LANES]
        x0 = jnp.where(low, xt, 0.0)
        s0 = jnp.sum(x0, axis=-1, keepdims=True)
        s1 = jnp.sum(xt - x0, axis=-1, keepdims=True)
        tiles.append(jnp.where(low, s0, s1))
    return jnp.concatenate(tiles, axis=1)


def _unrolled_loop(n, step, init):
    def body(i, carry):
        for j in range(SCAN_UNROLL):
            carry = step(i * SCAN_UNROLL + j, carry)
        return carry
    return lax.fori_loop(0, n // SCAN_UNROLL, body, init)


def _scan_fwd_call(r, w, k, v, a, b, name):
    T = r.shape[0]
    tc = min(SCAN_CHUNK, T)
    dg = _scan_diag()

    def body(r_ref, w_ref, k_ref, v_ref, a_ref, b_ref, dg_ref, y_ref, st_ref, s_ref):
        @pl.when(pl.program_id(0) == 0)
        def _():
            s_ref[...] = jnp.zeros(s_ref.shape, F32)

        dgv = dg_ref[...]
        readout = lambda s, t: jnp.sum(_head_rowsum(s * r_ref[t]) * dgv, axis=0, keepdims=True)

        def step(t, carry):
            s, vcol = carry
            st_ref[t] = s
            sa = _head_rowsum(s * a_ref[t])
            prev = jnp.maximum(t - 1, 0)
            y_ref[prev] = readout(s, prev)
            vcol_next = _head_rowsum(dgv * v_ref[jnp.minimum(t + 1, tc - 1)])
            sn = s * w_ref[t] + sa * b_ref[t] + vcol * k_ref[t]
            return sn, vcol_next

        s_end, _ = _unrolled_loop(tc, step, (s_ref[...], _head_rowsum(dgv * v_ref[0])))
        y_ref[tc - 1] = readout(s_end, tc - 1)
        s_ref[...] = s_end

    vec = pl.BlockSpec((tc, 1, RW_WIDTH), lambda i: (i, 0, 0))
    return pl.pallas_call(
        body, name=name + '_f', grid=(T // tc,),
        in_specs=[vec] * 6 + [pl.BlockSpec((RW_N, RW_WIDTH), lambda i: (0, 0))],
        out_specs=[vec, pl.BlockSpec((tc, RW_N, RW_WIDTH), lambda i: (i, 0, 0))],
        out_shape=[jax.ShapeDtypeStruct((T, 1, RW_WIDTH), F32), jax.ShapeDtypeStruct((T, RW_N, RW_WIDTH), F32)],
        scratch_shapes=[pltpu.VMEM((RW_N, RW_WIDTH), F32)],
        compiler_params=_cparams(('arbitrary',)),
    )(r, w, k, v, a, b, dg)


def _scan_bwd_call(r, w, k, v, a, b, st, dy, name):
    T = r.shape[0]
    tc = min(SCAN_CHUNK, T)
    nt = T // tc
    dg = _scan_diag()

    def body(r_ref, w_ref, k_ref, v_ref, a_ref, b_ref, st_ref, dy_ref, dg_ref,
             dr_ref, dw_ref, dk_ref, dv_ref, da_ref, db_ref, ds_ref):
        @pl.when(pl.program_id(0) == 0)
        def _():
            ds_ref[...] = jnp.zeros(ds_ref.shape, F32)

        dgv = dg_ref[...]
        colsum = lambda x: jnp.sum(x, axis=0, keepdims=True)

        def step(i, carry):
            ds, dycol = carry
            t = tc - 1 - i
            sp = st_ref[t]
            rt, wt, kt, at, bt = r_ref[t], w_ref[t], k_ref[t], a_ref[t], b_ref[t]
            ds = ds + dycol * rt
            dsa = _head_rowsum(ds * bt)
            sa = _head_rowsum(sp * at)
            vcol = _head_rowsum(dgv * v_ref[t])
            dycol_next = _head_rowsum(dgv * dy_ref[jnp.maximum(t - 1, 0)])
            sn = sp * wt + sa * bt + vcol * kt
            dr_ref[t] = colsum(sn * dycol)
            dk_ref[t] = colsum(ds * vcol)
            db_ref[t] = colsum(ds * sa)
            dw_ref[t] = colsum(ds * sp)
            dv_ref[t] = colsum(_head_rowsum(ds * kt) * dgv)
            da_ref[t] = colsum(sp * dsa)
            return ds * wt + dsa * at, dycol_next

        ds_end, _ = _unrolled_loop(tc, step, (ds_ref[...], _head_rowsum(dgv * dy_ref[tc - 1])))
        ds_ref[...] = ds_end

    vec = pl.BlockSpec((tc, 1, RW_WIDTH), lambda i: (nt - 1 - i, 0, 0))
    vsh = jax.ShapeDtypeStruct((T, 1, RW_WIDTH), F32)
    return pl.pallas_call(
        body, name=name + '_b', grid=(nt,),
        in_specs=[vec] * 6 + [pl.BlockSpec((tc, RW_N, RW_WIDTH), lambda i: (nt - 1 - i, 0, 0)), vec,
                              pl.BlockSpec((RW_N, RW_WIDTH), lambda i: (0, 0))],
        out_specs=[vec] * 6, out_shape=[vsh] * 6,
        scratch_shapes=[pltpu.VMEM((RW_N, RW_WIDTH), F32)],
        compiler_params=_cparams(('arbitrary',)),
    )(r, w, k, v, a, b, st, dy, dg)


@functools.partial(jax.custom_vjp, nondiff_argnums=(6,))
def wkv7(r, w, k, v, a, b, name):
    return _scan_fwd_call(r, w, k, v, a, b, name)[0]


def _wkv7_fwd(r, w, k, v, a, b, name):
    y, st = _scan_fwd_call(r, w, k, v, a, b, name)
    return y, (r, w, k, v, a, b, st)


def _wkv7_bwd(name, res, dy):
    return tuple(_scan_bwd_call(*res, dy, name))


wkv7.defvjp(_wkv7_fwd, _wkv7_bwd)


def _np_bf16(a):
    return jnp.asarray(a, BF16)


def _mla_consts():
    seg_n = (np.arange(512)[:, None] // QK_NOPE == np.arange(LANES)[None, :]).astype(np.float32)
    seg_r = (np.arange(LANES)[:, None] // 16 == np.arange(LANES)[None, :]).astype(np.float32)
    e1 = np.zeros((LANES, LANES), np.float32)
    e2 = np.zeros((LANES, LANES), np.float32)
    for h in range(MLA_HEADS):
        for i in range(16):
            e1[i, h * 16 + i] = 1.0
            e2[16 + i, h * 16 + i] = 1.0
    mats = [seg_n, seg_n.T, seg_r, seg_r.T, e1, e1.T, e2, e2.T]
    return [_np_bf16(m) for m in mats]


def _qk_prep_fn(qn, q1, q2, kn, kx, cos, sin, gqn, gq1, gq2, gkn, gk1, gk2,
                seg_n, seg_nt, seg_r, seg_rt, e1, e1t, e2, e2t):
    def normrope(xn, x1, x2, gn, g1, g2):
        ss = sdot(xn * xn, seg_n, seg_nt) + sdot(x1 * x1, seg_r, seg_rt) + sdot(x2 * x2, seg_r, seg_rt)
        inv = lax.rsqrt(ss * (1.0 / QK_HEAD) + NORM_EPS)
        inv_n = sdot(inv, seg_nt, seg_n)
        inv_r = sdot(inv, seg_rt, seg_r)
        y1 = x1 * inv_r * g1
        y2 = x2 * inv_r * g2
        return xn * inv_n * gn, y1 * cos - y2 * sin, y1 * sin + y2 * cos

    k1 = sdot(kx, e1, e1t)
    k2 = sdot(kx, e2, e2t)
    return normrope(qn, q1, q2, gqn, gq1, gq2) + normrope(kn, k1, k2, gkn, gk1, gk2)


def _rwkv_prep_fn(vres):
    def fn(r, k, v, xg, xwa, kx, *rest):
        if vres:
            vfirst, w0, a0, k_k, k_a, w2p, a2p, g2, v0, v2p, bm = rest
        else:
            w0, a0, k_k, k_a, w2p, a2p, g2, bm = rest
        z = w0 + bdot(jnp.tanh(xwa), w2p)
        nz = -z
        softplus = jnp.maximum(nz, 0.0) + jnp.log(1.0 + jnp.exp(-jnp.abs(nz)))
        decay = jnp.exp(-jnp.exp(-softplus - 0.5))
        a = _sigmoid(a0 + bdot(xwa, a2p))
        g = bdot(_sigmoid(xg), g2)
        if vres:
            vv = v + (vfirst - v) * _sigmoid(v0 + bdot(kx, v2p))
        else:
            vv = v
        kkr = k * k_k
        kk = kkr / jnp.maximum(jnp.sqrt(sdot(kkr * kkr, bm, bm)), 1e-12)
        k2 = k * (1.0 + (a - 1.0) * k_a)
        return r * 1.0, decay, k2, vv, -kk, kk * a, g
    return fn


def _rwkv_post_fn(y, r, k2, vv, g, ln_w, ln_b, rk, bm):
    inv_n = 1.0 / RW_N
    mean = sdot(y, bm, bm) * inv_n
    yc = y - mean
    var = sdot(yc * yc, bm, bm) * inv_n
    yn = yc * lax.rsqrt(var + GN_EPS) * ln_w + ln_b
    bonus = sdot(r * k2 * rk, bm, bm) * vv
    return ((yn + bonus) * g,)


def _merge_fn(g0, g1, g2, oa, ob, oc):
    return (_sigmoid(g0) * oa + _sigmoid(g1) * ob + _sigmoid(g2) * oc,)


def _relu2_fn(u):
    r = jnp.maximum(u, 0.0)
    return (r * r,)


def _loss_call(y, target):
    T, C = y.shape
    tb = min(256, T)

    def body(y_ref, t_ref, dy_ref, part_ref):
        err = y_ref[...] - t_ref[...]
        dy_ref[...] = err * (1.0 / C)
        sq = jnp.sum(err * err, axis=0, keepdims=True)
        acc = sq[:, 0:LANES]
        for j in range(1, C // LANES):
            acc = acc + sq[:, j * LANES:(j + 1) * LANES]
        part_ref[...] = jnp.zeros(part_ref.shape, F32)
        part_ref[0:1, :] = acc * (0.5 / C)

    return pl.pallas_call(
        body, name='loss', grid=(T // tb,),
        in_specs=[pl.BlockSpec((tb, C), lambda i: (i, 0))] * 2,
        out_specs=[pl.BlockSpec((tb, C), lambda i: (i, 0)), pl.BlockSpec((8, LANES), lambda i: (i, 0))],
        out_shape=[jax.ShapeDtypeStruct((T, C), F32), jax.ShapeDtypeStruct((8 * (T // tb), LANES), F32)],
        compiler_params=_cparams(('parallel',)),
    )(y, target)


def _cols(w, lo, hi):
    return w.map(lambda t: t[:, lo:hi])


def _pad_rows(t, before, total):
    return jnp.pad(t, ((before, total - before - t.shape[0]), (0, 0)))


def _head_tile(g, lo, hi):
    return jnp.tile(g[lo:hi], MLA_HEADS).reshape(1, -1)


def _layer(l, x, v_first, wd, sp, cos, sin):
    T = x.shape[0]
    nm = f'l{l}'
    vres = l > 0
    w_in = wd['w_in']
    if vres:
        v1t = wd['rwkv_v1'].map(lambda t: t.T)
    else:
        v1t = W(jnp.zeros((MV_LORA, D_MODEL), BF16), jnp.zeros((MV_LORA, D_MODEL), F32))
    zpad = W(jnp.zeros((64, D_MODEL), BF16), jnp.zeros((64, D_MODEL), F32))
    w_rw = _wcat([w_in['rkv'], w_in['xg'], w_in['xwa'], w_in['kpe'], v1t, zpad], 0)
    h = rmsnorm(x, sp['attn_norm'], nm + '_anorm')
    gates, cq, ckv, rw, cv = multi_linear_t(
        h, [w_in['gates'], w_in['cq'], w_in['ckv'], w_rw, w_in['conv']], nm + '_win')

    v_mu = sp['rwkv_v_mu'] if vres else jnp.zeros((MV_LORA,), F32)
    mu_all = jnp.concatenate([sp['rwkv_mu'][0:768], sp['rwkv_mu'][896:1024], sp['rwkv_mu'][768:896],
                              jnp.zeros((QK_ROPE,), F32), v_mu, jnp.zeros((64,), F32)]).reshape(1, -1)
    rws = token_shift_mix(rw, mu_all, nm + '_shift')

    cqn = rmsnorm(cq, sp['mla_q_a_norm'], nm + '_qan')
    ckvn = rmsnorm(ckv, sp['mla_kv_a_norm'], nm + '_kvan')
    wq = wd['mla_wq_b'].map(lambda t: jnp.concatenate(
        [t.reshape(Q_LORA, MLA_HEADS, QK_HEAD)[:, :, 0:64].reshape(Q_LORA, 512),
         t.reshape(Q_LORA, MLA_HEADS, QK_HEAD)[:, :, 64:80].reshape(Q_LORA, 128),
         t.reshape(Q_LORA, MLA_HEADS, QK_HEAD)[:, :, 80:96].reshape(Q_LORA, 128)], axis=1))
    wkn = wd['mla_wkv_b'].map(lambda t: t.reshape(KV_LORA, MLA_HEADS, 128)[:, :, 0:64].reshape(KV_LORA, 512))
    wv = wd['mla_wkv_b'].map(lambda t: t.reshape(KV_LORA, MLA_HEADS, 128)[:, :, 64:128].reshape(KV_LORA, 512))
    q = linear(cqn, wq, name=nm + '_wq')
    kn, vv_att = multi_linear(ckvn, [wkn, wv], nm + '_wkv')
    gq, gk = sp['mla_q_norm'], sp['mla_k_norm']
    consts = _mla_consts()
    qk_specs = ([ROW(pieces=((0, 512), (512, 128), (640, 128))), ROW(), ROW(pieces=((1024, 128),)),
                 ROW(False), ROW(False)] + [FULL()] * 6 + [FULL(False)] * 8)
    qk_op = stage_op(_qk_prep_fn, qk_specs, [512, 128, 128, 512, 128, 128], nm + '_qkprep')
    Qn, Q1, Q2, Kn, K1, K2 = qk_op(q, kn, rws, cos, sin,
                                   _head_tile(gq, 0, 64), _head_tile(gq, 64, 80), _head_tile(gq, 80, 96),
                                   _head_tile(gk, 0, 64), _head_tile(gk, 64, 80), _head_tile(gk, 80, 96), *consts)
    o_att = attention(Qn, Q1, Q2, Kn, K1, K2, vv_att, nm + '_att')
    o_a = linear(o_att, wd['mla_w_o'], name=nm + '_wo')

    bm = _np_bf16(_block_ones(RW_WIDTH, RW_N))
    vec = lambda n: sp[n].reshape(1, -1)
    f32w = lambda n: wd[n].c + wd[n].b.astype(F32)
    w2p = _pad_rows(f32w('rwkv_w2'), 0, 128)
    a2p = _pad_rows(f32w('rwkv_a2'), 64, 128)
    g2 = f32w('rwkv_g2')
    rw_pieces = ((0, 256), (256, 256), (512, 256), (768, 128), (896, 128), (1024, 128))
    if vres:
        v2p = _pad_rows(f32w('rwkv_v2'), 32, 128)
        prep_specs = [ROW(pieces=rw_pieces), ROW()] + [FULL()] * 9 + [FULL(False)]
        prep_in = [rws, v_first, vec('rwkv_w0'), vec('rwkv_a0'), vec('rwkv_k_k'), vec('rwkv_k_a'), w2p, a2p, g2,
                   vec('rwkv_v0'), v2p, bm]
    else:
        prep_specs = [ROW(pieces=rw_pieces)] + [FULL()] * 7 + [FULL(False)]
        prep_in = [rws, vec('rwkv_w0'), vec('rwkv_a0'), vec('rwkv_k_k'), vec('rwkv_k_a'), w2p, a2p, g2, bm]
    prep_op = stage_op(_rwkv_prep_fn(vres), prep_specs, [256] * 7, nm + '_rwprep')
    r_, dec, k2, vv, an, bn, g = prep_op(*prep_in)
    if not vres:
        v_first = vv
    t3 = lambda t: t.reshape(T, 1, RW_WIDTH)
    y = wkv7(t3(r_), t3(dec), t3(k2), t3(vv), t3(an), t3(bn), nm + '_scan').reshape(T, RW_WIDTH)
    post_op = stage_op(_rwkv_post_fn, [ROW()] * 5 + [FULL()] * 3 + [FULL(False)], [256], nm + '_rwpost')
    yb = post_op(y, r_, k2, vv, g, vec('rwkv_ln_w'), vec('rwkv_ln_b'), sp['rwkv_r_k'].reshape(1, -1), bm)[0]
    o_b = linear(yb, wd['rwkv_w_o'], name=nm + '_rwo')

    oc_in = short_conv(cv, f32w('conv_w'), nm + '_conv')
    o_c = linear(oc_in, wd['conv_w_o'], name=nm + '_cwo')

    merge_op = stage_op(_merge_fn, [ROW(pieces=((0, 1024), (1024, 1024), (2048, 1024))), ROW(), ROW(), ROW()],
                        [D_MODEL], nm + '_merge')
    merged = merge_op(gates, o_a, o_b, o_c)[0]
    x2 = linear(merged, wd['w_out'], add=x, name=nm + '_wout')
    h2 = rmsnorm(x2, sp['mlp_norm'], nm + '_mnorm')
    u = linear(h2, wd['w_up'], name=nm + '_wup')
    act = stage_op(_relu2_fn, [ROW()], [D_FF], nm + '_relu2')(u)[0]
    x3 = linear(act, wd['w_down'], add=x2, name=nm + '_wdown')
    return x3, v_first


def _entries(ml):
    out = []
    for name, (layers, shape, axis) in SHARDED.items():
        l = ml if layers == DEPTH else ml - 1
        if not 0 <= l < layers:
            continue
        n = shape[0] * shape[1] // N_DEV
        if name == 'conv_w':
            out.append(('conv_w_hi', name, l, n))
            out.append(('conv_w_lo', name, l, n))
        else:
            out.append((name, name, l, n))
    return out


def _slot_size(n):
    return -(-n // LANES) * LANES


def _pack_rows(ml):
    total = sum(_slot_size(n) for _, _, _, n in _entries(ml))
    rows = -(-total // LANES)
    return -(-rows // PACK_ROW_MULT) * PACK_ROW_MULT


def _pack_flat(pieces, ml):
    rows = _pack_rows(ml)
    padded = []
    for p, (_, _, _, n) in zip(pieces, _entries(ml)):
        pad = _slot_size(n) - n
        if pad:
            p = jnp.pad(p, [(0, 0)] * (p.ndim - 1) + [(0, pad)])
        padded.append(p)
    flat = jnp.concatenate(padded, axis=-1)
    tail = rows * LANES - flat.shape[-1]
    if tail:
        flat = jnp.pad(flat, [(0, 0)] * (flat.ndim - 1) + [(0, tail)])
    return flat.reshape(flat.shape[:-1] + (rows, LANES))


def _unpack_flat(buf, ml):
    out, row = [], 0
    for _, _, _, n in _entries(ml):
        nrows = _slot_size(n) // LANES
        piece = buf[..., row:row + nrows, :].reshape(buf.shape[:-2] + (-1,))
        out.append(piece[..., :n])
        row += nrows
    return out


def _pack_shards(shards, ml, dtype, split_conv):
    pieces = []
    for slot, name, l, n in _entries(ml):
        a = shards[name][l]
        a = (a.T if name in TRANSPOSED else a).reshape(-1)
        if slot == 'conv_w_hi':
            a = a.astype(BF16).astype(F32) if split_conv else a
        elif slot == 'conv_w_lo':
            a = (a - a.astype(BF16).astype(F32)) if split_conv else jnp.zeros_like(a)
        pieces.append(a.astype(dtype))
    return _pack_flat(pieces, ml)


def _unpack_shards(buf, ml):
    out = {}
    for (slot, name, l, n), v in zip(_entries(ml), _unpack_flat(buf, ml)):
        if slot == 'conv_w_lo':
            continue
        layers, shape, axis = SHARDED[name]
        sshape = (shape[0] // N_DEV, shape[1]) if axis == 0 else (shape[0], shape[1] // N_DEV)
        out[(name, l)] = v.reshape(sshape[::-1]).T if name in TRANSPOSED else v.reshape(sshape)
    return out


def _to_full(blocks, shape, axis):
    if axis == 0:
        return blocks.reshape(shape)
    return blocks.reshape(N_DEV, shape[0], shape[1] // N_DEV).transpose(1, 0, 2).reshape(shape)


def _to_blocks(full, axis):
    r, c = full.shape
    if axis == 0:
        return full.reshape(N_DEV, -1)
    return full.reshape(r, N_DEV, c // N_DEV).transpose(1, 0, 2).reshape(N_DEV, -1)


def _unpack_gathered(gathered, ml):
    out, conv_hi = {}, None
    for (slot, name, l, n), v in zip(_entries(ml), _unpack_flat(gathered, ml)):
        layers, shape, axis = SHARDED[name]
        if name in TRANSPOSED:
            out[name] = v.reshape(-1, 16, LANES)
            continue
        full = _to_full(v, shape, axis)
        if slot == 'conv_w_hi':
            conv_hi = full
        elif slot == 'conv_w_lo':
            out[name] = conv_hi.astype(F32) + full.astype(F32)
        else:
            out[name] = full
    return out


def _pack_grads(grads, ml):
    pieces = []
    for slot, name, l, n in _entries(ml):
        if name in TRANSPOSED:
            blocks = jnp.concatenate(grads[name], axis=0).reshape(N_DEV, -1)
        else:
            blocks = _to_blocks(grads[name], SHARDED[name][2])
        if slot == 'conv_w_lo':
            blocks = jnp.zeros_like(blocks)
        pieces.append(blocks)
    return _pack_flat(pieces, ml)


def _my_pos():
    return lax.axis_index('x'), lax.axis_index('y'), lax.axis_index('c')


def _flip(v, bit):
    return 1 - v if bit else v


def all_gather_blocks(x):
    rows = x.shape[0]

    def body(x_ref, out_ref, send_sems, recv_sems, local_sem):
        mx, my, mc = _my_pos()
        me, sibling = (mx, my, mc), (mx, my, 1 - mc)
        chips = [(1 - mx, my), (mx, 1 - my), (1 - mx, 1 - my)]

        def block(px, py, pc):
            return out_ref.at[4 * px + 2 * py + pc]

        def copy(k, blk, to, src=None):
            return pltpu.make_async_remote_copy(
                src_ref=block(*blk) if src is None else src, dst_ref=block(*blk),
                send_sem=send_sems.at[k], recv_sem=recv_sems.at[k], device_id=to, device_id_type=MESH)

        mine = pltpu.make_async_copy(x_ref, block(*me), local_sem)
        mine.start()
        first = [copy(0, me, sibling, src=x_ref)]
        first += [copy(1 + j, me, (*chip, mc), src=x_ref) for j, chip in enumerate(chips)]
        for cp in first:
            cp.start()
        passed = [copy(4 + j, (*chip, mc), sibling) for j, chip in enumerate(chips)]
        for j, chip in enumerate(chips):
            copy(1 + j, (*chip, mc), me).wait_recv()
            passed[j].start()
        copy(0, sibling, me).wait_recv()
        for j, chip in enumerate(chips):
            copy(4 + j, (*chip, 1 - mc), me).wait_recv()
        for cp in first + passed:
            cp.wait_send()
        mine.wait()

    return pl.pallas_call(
        body, name='all_gather_weights',
        out_shape=jax.ShapeDtypeStruct((N_DEV, rows, LANES), x.dtype),
        in_specs=[pl.BlockSpec(memory_space=pl.ANY)], out_specs=pl.BlockSpec(memory_space=pl.ANY),
        scratch_shapes=[pltpu.SemaphoreType.DMA((7,)), pltpu.SemaphoreType.DMA((7,)), pltpu.SemaphoreType.DMA],
    )(x)


def exchange_blocks(g):
    def body(g_ref, out_ref, send_sems, recv_sems, local_sem):
        mx, my, mc = _my_pos()
        me = 4 * mx + 2 * my + mc
        mine = pltpu.make_async_copy(g_ref.at[me], out_ref.at[me], local_sem)
        mine.start()
        copies, arrivals = [], []
        for k in range(1, N_DEV):
            peer = (_flip(mx, k & 4), _flip(my, k & 2), _flip(mc, k & 1))
            pidx = 4 * peer[0] + 2 * peer[1] + peer[2]
            copies.append(pltpu.make_async_remote_copy(
                src_ref=g_ref.at[pidx], dst_ref=out_ref.at[me], send_sem=send_sems.at[k - 1],
                recv_sem=recv_sems.at[k - 1], device_id=peer, device_id_type=MESH))
            arrivals.append(pltpu.make_async_remote_copy(
                src_ref=g_ref.at[me], dst_ref=out_ref.at[pidx], send_sem=send_sems.at[k - 1],
                recv_sem=recv_sems.at[k - 1], device_id=peer, device_id_type=MESH))
        for cp in copies:
            cp.start()
        for cp in arrivals:
            cp.wait_recv()
        for cp in copies:
            cp.wait_send()
        mine.wait()

    return pl.pallas_call(
        body, name='exchange_grads',
        out_shape=jax.ShapeDtypeStruct(g.shape, g.dtype),
        in_specs=[pl.BlockSpec(memory_space=pl.ANY)], out_specs=pl.BlockSpec(memory_space=pl.ANY),
        scratch_shapes=[pltpu.SemaphoreType.DMA((7,)), pltpu.SemaphoreType.DMA((7,)), pltpu.SemaphoreType.DMA],
    )(g)


HBM_SPEC = pl.BlockSpec(memory_space=pltpu.HBM)
SEM_SPEC = pl.BlockSpec(memory_space=pltpu.SEMAPHORE)
DATAFLOW_EFFECT = pltpu.SideEffectType.DATAFLOW_SIDE_EFFECTING


def _direct_copies(src_ref, land_ref, send_sems, recv_sems, per_peer):
    mx, my, mc = _my_pos()
    me = 4 * mx + 2 * my + mc
    copies = []
    for k in range(1, N_DEV):
        peer = (_flip(mx, k & 4), _flip(my, k & 2), _flip(mc, k & 1))
        pidx = 4 * peer[0] + 2 * peer[1] + peer[2]
        copies.append(pltpu.make_async_remote_copy(
            src_ref=src_ref.at[pidx] if per_peer else src_ref, dst_ref=land_ref.at[me],
            send_sem=send_sems.at[k - 1], recv_sem=recv_sems.at[k - 1], device_id=peer, device_id_type=MESH))
    return copies


def send_start(src, per_peer, name):
    block = src.shape[1:] if per_peer else src.shape
    land_shape = (N_DEV,) + tuple(block)

    def body(src_ref, land_ref, send_sems, recv_sems, src_thru, land_thru, token):
        for cp in _direct_copies(src_ref, land_ref, send_sems, recv_sems, per_peer):
            cp.start()
        token[...] = jnp.zeros(token.shape, F32)

    send_sems, recv_sems, src_thru, land_thru, token = pl.pallas_call(
        body, name=name,
        out_shape=(pltpu.SemaphoreType.DMA((N_DEV - 1,)), pltpu.SemaphoreType.DMA((N_DEV - 1,)),
                   pltpu.HBM(src.shape, src.dtype), pltpu.HBM(land_shape, src.dtype),
                   jax.ShapeDtypeStruct((8, LANES), F32)),
        in_specs=(HBM_SPEC, HBM_SPEC),
        out_specs=(SEM_SPEC, SEM_SPEC, HBM_SPEC, HBM_SPEC, pl.BlockSpec(memory_space=pltpu.VMEM)),
        input_output_aliases={0: 2, 1: 3},
        compiler_params=pltpu.CompilerParams(has_side_effects=DATAFLOW_EFFECT),
    )(pltpu.with_memory_space_constraint(src, pltpu.HBM),
      pltpu.with_memory_space_constraint(lax.empty(land_shape, src.dtype), pltpu.HBM))
    return (send_sems, recv_sems, src_thru, land_thru), token[0, 0]


def send_wait(handles, after, per_peer, name):
    send_sems, recv_sems, src_thru, land_thru = handles

    def body(src_ref, land_ref, send_sems, recv_sems, after_ref, src_dead, got_ref):
        for cp in _direct_copies(src_ref, land_ref, send_sems, recv_sems, per_peer):
            cp.wait_send()
            cp.wait_recv()

    return pl.pallas_call(
        body, name=name,
        out_shape=(pltpu.HBM(src_thru.shape, src_thru.dtype), pltpu.HBM(land_thru.shape, land_thru.dtype)),
        in_specs=(HBM_SPEC, HBM_SPEC, SEM_SPEC, SEM_SPEC, pl.BlockSpec(memory_space=pl.ANY)),
        out_specs=(HBM_SPEC, HBM_SPEC), input_output_aliases={0: 0, 1: 1},
        compiler_params=pltpu.CompilerParams(has_side_effects=DATAFLOW_EFFECT),
    )(src_thru, land_thru, send_sems, recv_sems, after)[1]


def _adamw_math(w, g, m, v):
    m2 = ADAM_B1 * m + (1.0 - ADAM_B1) * g
    v2 = ADAM_B2 * v + (1.0 - ADAM_B2) * (g * g)
    m_hat = m2 / (1.0 - ADAM_B1 ** ADAM_STEP)
    v_hat = v2 / (1.0 - ADAM_B2 ** ADAM_STEP)
    delta = -ADAM_LR * (m_hat / (jnp.sqrt(v_hat) + ADAM_EPS) + ADAM_WD * w)
    return delta, m2, v2


def sum_parts(parts, name):
    rows = parts.shape[1]
    rb = PACK_ROW_MULT

    def body(p_ref, g_ref):
        g = p_ref[0].astype(F32)
        for j in range(1, N_DEV):
            g = g + p_ref[j].astype(F32)
        g_ref[...] = g

    return pl.pallas_call(
        body, name=name, grid=(rows // rb,),
        in_specs=[pl.BlockSpec((N_DEV, rb, LANES), lambda i: (0, i, 0))],
        out_specs=pl.BlockSpec((rb, LANES), lambda i: (i, 0)),
        out_shape=jax.ShapeDtypeStruct((rows, LANES), F32), compiler_params=_cparams(('parallel',)),
    )(parts)


ADAMW_BLOCK_BYTES = 1 << 20


def adamw_weight(name, w, m, v, grads):
    layers, a, b = w.shape
    ra = a
    while ra * b * 4 > ADAMW_BLOCK_BYTES and ra % 16 == 0:
        ra //= 2

    def body(*refs):
        w_ref, m_ref, v_ref = refs[:3]
        g_refs = refs[3:3 + layers]
        g_ref, d_ref, m2_ref, v2_ref = refs[3 + layers:]
        g = g_refs[0][...]
        for l in range(1, layers):
            g = jnp.where(pl.program_id(0) == l, g_refs[l][...], g)
        delta, m2, v2 = _adamw_math(w_ref[0], g, m_ref[0], v_ref[0])
        g_ref[0] = g
        d_ref[0] = delta
        m2_ref[0] = m2
        v2_ref[0] = v2

    blk = pl.BlockSpec((1, ra, b), lambda l, i: (l, i, 0))
    gblk = pl.BlockSpec((ra, b), lambda l, i: (i, 0))
    sh = jax.ShapeDtypeStruct(w.shape, F32)
    return pl.pallas_call(
        body, name='adamw_' + name, grid=(layers, a // ra), in_specs=[blk] * 3 + [gblk] * layers,
        out_specs=[blk] * 4, out_shape=[sh] * 4, compiler_params=_cparams(('parallel', 'parallel')),
    )(w, m, v, *grads)


def allreduce_adamw_small(g, w, m, v):
    rows = g.shape[0]

    def body(g_ref, w_ref, m_ref, v_ref, gs_ref, d_ref, m2_ref, v2_ref, all_ref, send_sems, recv_sems):
        mx, my, mc = _my_pos()
        me, sibling = (mx, my, mc), (mx, my, 1 - mc)
        chips = [(1 - mx, my), (mx, 1 - my), (1 - mx, 1 - my)]

        def block(px, py, pc):
            return all_ref.at[4 * px + 2 * py + pc]

        def copy(k, blk, to, src=None):
            return pltpu.make_async_remote_copy(
                src_ref=block(*blk) if src is None else src, dst_ref=block(*blk),
                send_sem=send_sems.at[k], recv_sem=recv_sems.at[k], device_id=to, device_id_type=MESH)

        first = [copy(0, me, sibling, src=g_ref)]
        first += [copy(1 + j, me, (*chip, mc), src=g_ref) for j, chip in enumerate(chips)]
        for cp in first:
            cp.start()
        passed = [copy(4 + j, (*chip, mc), sibling) for j, chip in enumerate(chips)]
        for j, chip in enumerate(chips):
            copy(1 + j, (*chip, mc), me).wait_recv()
            passed[j].start()
        copy(0, sibling, me).wait_recv()
        for j, chip in enumerate(chips):
            copy(4 + j, (*chip, 1 - mc), me).wait_recv()
        for cp in first + passed:
            cp.wait_send()
        my_idx = 4 * mx + 2 * my + mc
        total = jnp.zeros((rows, LANES), F32)
        for j in range(N_DEV):
            total = total + jnp.where(my_idx == j, g_ref[...], all_ref[j])
        delta, m2, v2 = _adamw_math(w_ref[...], total, m_ref[...], v_ref[...])
        gs_ref[...] = total
        d_ref[...] = delta
        m2_ref[...] = m2
        v2_ref[...] = v2

    vm = pl.BlockSpec(memory_space=pltpu.VMEM)
    sh = jax.ShapeDtypeStruct((rows, LANES), F32)
    return pl.pallas_call(
        body, name='allreduce_adamw_small', in_specs=[vm] * 4, out_specs=[vm] * 4, out_shape=[sh] * 4,
        scratch_shapes=[pltpu.VMEM((N_DEV, rows, LANES), F32), pltpu.SemaphoreType.DMA((7,)),
                        pltpu.SemaphoreType.DMA((7,))],
    )(g, w, m, v)


def _small_pack(d):
    flat = jnp.concatenate([d[n].reshape(-1) for n in SMALL_NAMES])
    rows = -(-flat.shape[0] // (8 * LANES)) * 8
    return jnp.pad(flat, (0, rows * LANES - flat.shape[0])).reshape(rows, LANES)


def _small_unpack(buf, like):
    flat = buf.reshape(-1)
    out, off = {}, 0
    for n in SMALL_NAMES:
        sz = int(np.prod(like[n].shape))
        out[n] = flat[off:off + sz].reshape(like[n].shape)
        off += sz
    return out


def _rope_tables(positions):
    freqs = ROPE_THETA ** (-(jnp.arange(QK_ROPE // 2, dtype=F32) * 2.0 / QK_ROPE))
    ang = positions.astype(F32)[:, None] * freqs
    return jnp.tile(jnp.cos(ang), (1, MLA_HEADS)), jnp.tile(jnp.sin(ang), (1, MLA_HEADS))


def _layer_weights(gathered, carriers):
    wd = {}
    for name, full in gathered.items():
        if name == 'w_in':
            wd[name] = {seg: W(full[lo // 2:hi // 2].reshape(hi - lo, D_MODEL), c)
                        for (seg, lo, hi), c in zip(WIN_SEGS, carriers[name])}
        else:
            wd[name] = W(full, carriers[name])
    return wd


def _make_carriers(gathered):
    gathered, carriers = dict(gathered), {}
    for name, full in gathered.items():
        if name == 'w_in':
            carriers[name] = tuple(jnp.zeros((hi - lo, D_MODEL), F32) for _, lo, hi in WIN_SEGS)
        elif name == 'conv_w':
            gathered[name] = full.astype(BF16)
            carriers[name] = full - full.astype(BF16).astype(F32)
        else:
            carriers[name] = jnp.zeros(SHARDED[name][1], F32)
    return gathered, carriers


def _layer_small(small, ml):
    out = {}
    for n in SMALL_NAMES:
        l = ml if small[n].shape[0] == DEPTH else ml - 1
        if 0 <= l < small[n].shape[0]:
            out[n] = small[n][l]
    return out


def _train_step(x, positions, loss_target, weights, moms_m, moms_v):
    shards = {n: weights[n] for n in SHARDED}
    small = {n: weights[n] for n in SMALL_NAMES}
    me = 4 * lax.axis_index('x') + 2 * lax.axis_index('y') + lax.axis_index('c')
    cos, sin = _rope_tables(positions[0])

    own1 = _pack_shards(shards, 1, BF16, True)
    blocks0 = all_gather_blocks(_pack_shards(shards, 0, BF16, True))
    own1, blocks0 = lax.optimization_barrier((own1, blocks0))
    gather1, token = send_start(own1, False, 'gather_l1_start')
    gathered0, carriers0 = _make_carriers(_unpack_gathered(blocks0, 0))
    (x1, v_first), vjp0 = jax.vjp(
        lambda c, s, xx: _layer(0, xx, None, _layer_weights(gathered0, c), s, cos, sin),
        carriers0, _layer_small(small, 0), x[0] + token)

    blocks1 = send_wait(gather1, x1, False, 'gather_l1_wait')
    blocks1 = lax.dynamic_update_slice(blocks1, own1[None], (me, 0, 0))
    gathered1, carriers1 = _make_carriers(_unpack_gathered(blocks1, 1))
    y, vjp1 = jax.vjp(
        lambda c, s, xx, vf: _layer(1, xx, vf, _layer_weights(gathered1, c), s, cos, sin)[0],
        carriers1, _layer_small(small, 1), x1, v_first)

    dy, parts = _loss_call(y, loss_target[0])
    loss = lax.psum(jnp.sum(parts), ('x', 'y', 'c'))

    gw1, gs1, dx1, dvf = vjp1(dy)
    packed1 = _pack_grads(gw1, 1).astype(BF16)
    exchange1, token = send_start(packed1, True, 'grads_l1_start')
    gw0, gs0, g_x = vjp0((dx1 + token, dvf))
    parts1 = send_wait(exchange1, g_x, True, 'grads_l1_wait')
    parts1 = lax.dynamic_update_slice(parts1, lax.dynamic_slice_in_dim(packed1, me, 1, axis=0), (me, 0, 0))
    parts0 = exchange_blocks(_pack_grads(gw0, 0).astype(BF16))

    grads = {**_unpack_shards(sum_parts(parts0, 'sum_grads_l0'), 0), **_unpack_shards(sum_parts(parts1, 'sum_grads_l1'), 1)}
    sharded_out = [{}, {}, {}, {}]
    for n in SHARDED:
        outs = adamw_weight(n, weights[n], moms_m[n], moms_v[n], [grads[(n, l)] for l in range(SHARDED[n][0])])
        for i in range(4):
            sharded_out[i][n] = outs[i]

    g_small = {}
    for n in SMALL_NAMES:
        per = [g[n] for g in (gs0, gs1) if n in g]
        g_small[n] = jnp.stack(per)
    small_out = [_small_unpack(b, small) for b in allreduce_adamw_small(
        _small_pack(g_small), _small_pack(small), _small_pack({n: moms_m[n] for n in SMALL_NAMES}),
        _small_pack({n: moms_v[n] for n in SMALL_NAMES}))]

    pick = lambda i: [sharded_out[i][n] if n in SHARDED else small_out[i][n] for n in WEIGHT_NAMES]
    return (loss, g_x[None], *pick(0), *pick(1), *pick(2), *pick(3))


def kernel(x, positions, attn_norm, w_in, mla_q_a_norm, mla_wq_b, mla_kv_a_norm, mla_wkv_b, mla_q_norm, mla_k_norm, mla_w_o, rwkv_mu, rwkv_w0, rwkv_w2, rwkv_a0, rwkv_a2, rwkv_g2, rwkv_k_k, rwkv_k_a, rwkv_r_k, rwkv_ln_w, rwkv_ln_b, rwkv_w_o, rwkv_v1, rwkv_v_mu, rwkv_v0, rwkv_v2, conv_w, conv_w_o, w_out, mlp_norm, w_up, w_down, loss_target, m_attn_norm, m_w_in, m_mla_q_a_norm, m_mla_wq_b, m_mla_kv_a_norm, m_mla_wkv_b, m_mla_q_norm, m_mla_k_norm, m_mla_w_o, m_rwkv_mu, m_rwkv_w0, m_rwkv_w2, m_rwkv_a0, m_rwkv_a2, m_rwkv_g2, m_rwkv_k_k, m_rwkv_k_a, m_rwkv_r_k, m_rwkv_ln_w, m_rwkv_ln_b, m_rwkv_w_o, m_rwkv_v1, m_rwkv_v_mu, m_rwkv_v0, m_rwkv_v2, m_conv_w, m_conv_w_o, m_w_out, m_mlp_norm, m_w_up, m_w_down, v_attn_norm, v_w_in, v_mla_q_a_norm, v_mla_wq_b, v_mla_kv_a_norm, v_mla_wkv_b, v_mla_q_norm, v_mla_k_norm, v_mla_w_o, v_rwkv_mu, v_rwkv_w0, v_rwkv_w2, v_rwkv_a0, v_rwkv_a2, v_rwkv_g2, v_rwkv_k_k, v_rwkv_k_a, v_rwkv_r_k, v_rwkv_ln_w, v_rwkv_ln_b, v_rwkv_w_o, v_rwkv_v1, v_rwkv_v_mu, v_rwkv_v0, v_rwkv_v2, v_conv_w, v_conv_w_o, v_w_out, v_mlp_norm, v_w_up, v_w_down):
    args = locals()
    weights = {n: args[n] for n in WEIGHT_NAMES}
    moms_m = {n: args['m_' + n] for n in WEIGHT_NAMES}
    moms_v = {n: args['v_' + n] for n in WEIGHT_NAMES}
    return _train_step(x, positions, loss_target, weights, moms_m, moms_v)
```

```python
import functools

import numpy as np
import jax
import jax.numpy as jnp
from jax import lax
from jax.experimental import pallas as pl
from jax.experimental.pallas import tpu as pltpu

F32 = jnp.float32
BF16 = jnp.bfloat16

N_DEV = 8
LANES = 128
D_MODEL = 1024
DEPTH = 2
MLA_HEADS = 8
QK_NOPE = 64
QK_ROPE = 32
QK_HEAD = QK_NOPE + QK_ROPE
V_HEAD = 64
Q_LORA = 384
KV_LORA = 256
ROPE_THETA = 10000.0
RW_HEADS = 4
RW_N = 64
RW_WIDTH = RW_HEADS * RW_N
MV_LORA = 32
GN_EPS = 64e-5
CONV_WIDTH = 256
D_FF = 4 * D_MODEL
NORM_EPS = 1e-6
ADAM_LR = 0.001
ADAM_B1 = 0.9
ADAM_B2 = 0.999
ADAM_EPS = 1e-08
ADAM_WD = 0.01
ADAM_STEP = 10

VMEM_LIMIT = 56 * 1024 * 1024
MESH = pl.DeviceIdType.MESH

WEIGHT_NAMES = ['attn_norm', 'w_in', 'mla_q_a_norm', 'mla_wq_b', 'mla_kv_a_norm', 'mla_wkv_b', 'mla_q_norm',
                'mla_k_norm', 'mla_w_o', 'rwkv_mu', 'rwkv_w0', 'rwkv_w2', 'rwkv_a0', 'rwkv_a2', 'rwkv_g2',
                'rwkv_k_k', 'rwkv_k_a', 'rwkv_r_k', 'rwkv_ln_w', 'rwkv_ln_b', 'rwkv_w_o', 'rwkv_v1',
                'rwkv_v_mu', 'rwkv_v0', 'rwkv_v2', 'conv_w', 'conv_w_o', 'w_out', 'mlp_norm', 'w_up', 'w_down']

SHARDED = {
    'w_in': (2, (1024, 5536), 1), 'mla_wq_b': (2, (384, 768), 1), 'mla_wkv_b': (2, (256, 1024), 1),
    'mla_w_o': (2, (512, 1024), 1), 'rwkv_w2': (2, (64, 256), 1), 'rwkv_a2': (2, (64, 256), 1),
    'rwkv_g2': (2, (128, 256), 1), 'rwkv_w_o': (2, (256, 1024), 1), 'conv_w': (2, (3, 256), 1),
    'conv_w_o': (2, (256, 1024), 1), 'w_out': (2, (1024, 1024), 0), 'w_up': (2, (1024, 4096), 1),
    'w_down': (2, (4096, 1024), 0), 'rwkv_v1': (1, (1024, 32), 0), 'rwkv_v2': (1, (32, 256), 1),
}
SMALL_NAMES = [n for n in WEIGHT_NAMES if n not in SHARDED]
TRANSPOSED = ('w_in',)
WIN_SEGS = (('gates', 0, 3072), ('cq', 3072, 3456), ('ckv', 3456, 3712), ('kpe', 3712, 3744), ('rkv', 3744, 4512),
            ('xwa', 4512, 4640), ('xg', 4640, 4768), ('conv', 4768, 5536))
PACK_ROW_MULT = 512


def _cparams(sem=None, **kw):
    if sem is not None:
        kw['dimension_semantics'] = sem
    return pltpu.CompilerParams(vmem_limit_bytes=VMEM_LIMIT, **kw)


def _pick(n, cands):
    for c in cands:
        if n % c == 0:
            return c
    raise ValueError(f'no tile for {n}')


def _mm_nn(a, b, add=None, name='mm_nn'):
    M, K = a.shape
    N = b.shape[1]
    tm = _pick(M, (1024, 512, 256, 128))
    tn = _pick(N, (512, 384, 256, 128))
    tk = _pick(K, (1024, 512, 384, 256, 128))
    nk = K // tk
    has_add = add is not None

    def body(*refs):
        if has_add:
            a_ref, b_ref, add_ref, o_ref, acc_ref = refs
        else:
            a_ref, b_ref, o_ref, acc_ref = refs
        kk = pl.program_id(2)
        part = jnp.dot(a_ref[...].astype(BF16), b_ref[...].astype(BF16), preferred_element_type=F32)

        @pl.when(kk == 0)
        def _():
            acc_ref[...] = part

        @pl.when(kk > 0)
        def _():
            acc_ref[...] += part

        @pl.when(kk == nk - 1)
        def _():
            if has_add:
                o_ref[...] = acc_ref[...] + add_ref[...]
            else:
                o_ref[...] = acc_ref[...]

    in_specs = [pl.BlockSpec((tm, tk), lambda i, j, k: (i, k)), pl.BlockSpec((tk, tn), lambda i, j, k: (k, j))]
    args = [a, b]
    if has_add:
        in_specs.append(pl.BlockSpec((tm, tn), lambda i, j, k: (i, j)))
        args.append(add)
    return pl.pallas_call(
        body, name=name, grid=(M // tm, N // tn, nk), in_specs=in_specs,
        out_specs=pl.BlockSpec((tm, tn), lambda i, j, k: (i, j)),
        out_shape=jax.ShapeDtypeStruct((M, N), F32),
        scratch_shapes=[pltpu.VMEM((tm, tn), F32)],
        compiler_params=_cparams(('parallel', 'parallel', 'arbitrary')),
    )(*args)


def _mm_nt(a, b, add=None, name='mm_nt'):
    M, N = a.shape
    K = b.shape[0]
    tm = _pick(M, (1024, 512, 256, 128))
    tk = _pick(K, (512, 384, 256, 128))
    tn = _pick(N, (1024, 512, 384, 256, 128))
    nn = N // tn
    has_add = add is not None

    def body(*refs):
        if has_add:
            a_ref, b_ref, add_ref, o_ref, acc_ref = refs
        else:
            a_ref, b_ref, o_ref, acc_ref = refs
        kk = pl.program_id(2)
        part = lax.dot_general(a_ref[...].astype(BF16), b_ref[...].astype(BF16), (((1,), (1,)), ((), ())),
                               preferred_element_type=F32)

        @pl.when(kk == 0)
        def _():
            acc_ref[...] = part

        @pl.when(kk > 0)
        def _():
            acc_ref[...] += part

        @pl.when(kk == nn - 1)
        def _():
            if has_add:
                o_ref[...] = acc_ref[...] + add_ref[...]
            else:
                o_ref[...] = acc_ref[...]

    in_specs = [pl.BlockSpec((tm, tn), lambda i, j, k: (i, k)), pl.BlockSpec((tk, tn), lambda i, j, k: (j, k))]
    args = [a, b]
    if has_add:
        in_specs.append(pl.BlockSpec((tm, tk), lambda i, j, k: (i, j)))
        args.append(add)
    return pl.pallas_call(
        body, name=name, grid=(M // tm, K // tk, nn), in_specs=in_specs,
        out_specs=pl.BlockSpec((tm, tk), lambda i, j, k: (i, j)),
        out_shape=jax.ShapeDtypeStruct((M, K), F32),
        scratch_shapes=[pltpu.VMEM((tm, tk), F32)],
        compiler_params=_cparams(('parallel', 'parallel', 'arbitrary')),
    )(*args)


def _mm_tn(a, b, name='mm_tn'):
    M, K = a.shape
    N = b.shape[1]
    tm = _pick(M, (1024, 512, 256, 128))
    tk = _pick(K, (512, 384, 256, 128))
    tn = _pick(N, (512, 384, 256, 128))
    nm = M // tm

    def body(a_ref, b_ref, o_ref, acc_ref):
        mm = pl.program_id(2)
        part = lax.dot_general(a_ref[...].astype(BF16), b_ref[...].astype(BF16), (((0,), (0,)), ((), ())),
                               preferred_element_type=F32)

        @pl.when(mm == 0)
        def _():
            acc_ref[...] = part

        @pl.when(mm > 0)
        def _():
            acc_ref[...] += part

        @pl.when(mm == nm - 1)
        def _():
            o_ref[...] = acc_ref[...]

    return pl.pallas_call(
        body, name=name, grid=(K // tk, N // tn, nm),
        in_specs=[pl.BlockSpec((tm, tk), lambda i, j, m: (m, i)), pl.BlockSpec((tm, tn), lambda i, j, m: (m, j))],
        out_specs=pl.BlockSpec((tk, tn), lambda i, j, m: (i, j)),
        out_shape=jax.ShapeDtypeStruct((K, N), F32),
        scratch_shapes=[pltpu.VMEM((tk, tn), F32)],
        compiler_params=_cparams(('parallel', 'parallel', 'arbitrary')),
    )(a, b)


@functools.partial(jax.custom_vjp, nondiff_argnums=(4,))
def _linear_add(a, wb, wc, add, name):
    return _mm_nn(a, wb, add, name=name + '_f')


def _linear_add_fwd(a, wb, wc, add, name):
    return _mm_nn(a, wb, add, name=name + '_f'), (a, wb)


def _linear_add_bwd(name, res, dy):
    a, wb = res
    return _mm_nt(dy, wb, name=name + '_da'), None, _mm_tn(a, dy, name=name + '_dw'), dy


_linear_add.defvjp(_linear_add_fwd, _linear_add_bwd)


@functools.partial(jax.custom_vjp, nondiff_argnums=(3,))
def _multi_linear(a, wbs, wcs, name):
    return tuple(_mm_nn(a, wb, name=f'{name}_f{i}') for i, wb in enumerate(wbs))


def _multi_linear_fwd(a, wbs, wcs, name):
    return _multi_linear(a, wbs, wcs, name), (a, wbs)


def _multi_linear_bwd(name, res, dys):
    a, wbs = res
    da = None
    for i, (dy, wb) in enumerate(zip(dys, wbs)):
        da = _mm_nt(dy, wb, add=da, name=f'{name}_da{i}')
    dws = tuple(_mm_tn(a, dy, name=f'{name}_dw{i}') for i, dy in enumerate(dys))
    return da, None, dws


_multi_linear.defvjp(_multi_linear_fwd, _multi_linear_bwd)


class W:
    def __init__(self, b, c):
        self.b, self.c = b, c

    def map(self, fn):
        return W(fn(self.b), fn(self.c))


def _wcat(ws, axis):
    return W(jnp.concatenate([w.b for w in ws], axis), jnp.concatenate([w.c for w in ws], axis))


def linear(a, w, add=None, name='lin'):
    if add is None:
        return _multi_linear(a, (w.b,), (w.c,), name)[0]
    return _linear_add(a, w.b, w.c, add, name)


def multi_linear(a, ws, name):
    return _multi_linear(a, tuple(w.b for w in ws), tuple(w.c for w in ws), name)


@functools.partial(jax.custom_vjp, nondiff_argnums=(3,))
def _multi_linear_t(a, wbs, wcs, name):
    return tuple(_mm_nt(a, wb, name=f'{name}_f{i}') for i, wb in enumerate(wbs))


def _multi_linear_t_fwd(a, wbs, wcs, name):
    return _multi_linear_t(a, wbs, wcs, name), (a, wbs)


def _multi_linear_t_bwd(name, res, dys):
    a, wbs = res
    da = None
    for i, (dy, wb) in enumerate(zip(dys, wbs)):
        da = _mm_nn(dy, wb, add=da, name=f'{name}_da{i}')
    dws = tuple(_mm_tn(dy, a, name=f'{name}_dw{i}') for i, dy in enumerate(dys))
    return da, None, dws


_multi_linear_t.defvjp(_multi_linear_t_fwd, _multi_linear_t_bwd)


def multi_linear_t(a, ws, name):
    return _multi_linear_t(a, tuple(w.b for w in ws), tuple(w.c for w in ws), name)


def ROW(diff=True, pieces=None):
    return ('row', diff, pieces)


def FULL(diff=True):
    return ('full', diff, None)


def _load_args(refs, specs):
    args, amap = [], []
    for i, (ref, (kind, diff, pieces)) in enumerate(zip(refs, specs)):
        if pieces is None:
            args.append(ref[...])
            amap.append((i, None))
        else:
            for (s, w) in pieces:
                args.append(ref[:, s:s + w])
                amap.append((i, (s, w)))
    return args, amap


def _stage_in_specs(ins, specs, tb):
    out = []
    for a, (kind, _, _) in zip(ins, specs):
        if kind == 'row':
            out.append(pl.BlockSpec((tb, a.shape[1]), lambda i: (i, 0)))
        else:
            out.append(pl.BlockSpec(a.shape, lambda i: (0, 0)))
    return out


def _stage_fwd(fn, ins, specs, out_widths, name, tb, out_dtype=F32):
    T = [a for a, s in zip(ins, specs) if s[0] == 'row'][0].shape[0]
    tb = min(tb, T)
    n_in = len(ins)

    def body(*refs):
        args, _ = _load_args(refs[:n_in], specs)
        outs = fn(*args)
        for o_ref, o in zip(refs[n_in:], outs):
            o_ref[...] = o.astype(out_dtype)

    return pl.pallas_call(
        body, name=name + '_f', grid=(T // tb,), in_specs=_stage_in_specs(ins, specs, tb),
        out_specs=[pl.BlockSpec((tb, w), lambda i: (i, 0)) for w in out_widths],
        out_shape=[jax.ShapeDtypeStruct((T, w), out_dtype) for w in out_widths],
        compiler_params=_cparams(('parallel',)),
    )(*ins)


def _stage_bwd(fn, ins, specs, out_widths, douts, name, tb):
    T = [a for a, s in zip(ins, specs) if s[0] == 'row'][0].shape[0]
    tb = min(tb, T)
    n_in, n_out = len(ins), len(out_widths)
    diff_inputs = [i for i, s in enumerate(specs) if s[1]]

    def body(*refs):
        in_refs, dout_refs, g_refs = refs[:n_in], refs[n_in:n_in + n_out], refs[n_in + n_out:]
        args, amap = _load_args(in_refs, specs)
        didx = [j for j, (i, _) in enumerate(amap) if specs[i][1]]

        def f(*dv):
            full = list(args)
            for j, v in zip(didx, dv):
                full[j] = v
            return tuple(fn(*full))

        _, vjp = jax.vjp(f, *[args[j] for j in didx])
        gs = vjp(tuple(d[...] for d in dout_refs))
        gmap = {j: g for j, g in zip(didx, gs)}
        first = pl.program_id(0) == 0
        for g_ref, i in zip(g_refs, diff_inputs):
            kind, _, pieces = specs[i]
            js = [j for j, (ii, _) in enumerate(amap) if ii == i]
            if kind == 'row':
                if pieces is None:
                    g_ref[...] = gmap[js[0]]
                else:
                    if sum(w for _, w in pieces) != ins[i].shape[1]:
                        g_ref[...] = jnp.zeros(g_ref.shape, F32)
                    for j in js:
                        s, w = amap[j][1]
                        g_ref[:, s:s + w] = gmap[j]
            else:
                @pl.when(first)
                def _(g_ref=g_ref):
                    g_ref[...] = jnp.zeros(g_ref.shape, F32)

                g_ref[...] += gmap[js[0]]

    in_specs = _stage_in_specs(ins, specs, tb) + [pl.BlockSpec((tb, w), lambda i: (i, 0)) for w in out_widths]
    out_specs, out_shape = [], []
    for i in diff_inputs:
        a = ins[i]
        if specs[i][0] == 'row':
            out_specs.append(pl.BlockSpec((tb, a.shape[1]), lambda i: (i, 0)))
        else:
            out_specs.append(pl.BlockSpec(a.shape, lambda i: (0, 0)))
        out_shape.append(jax.ShapeDtypeStruct(a.shape, F32))
    return pl.pallas_call(
        body, name=name + '_b', grid=(T // tb,), in_specs=in_specs, out_specs=out_specs, out_shape=out_shape,
        compiler_params=_cparams(('arbitrary',)),
    )(*ins, *douts)


def stage_op(fn, specs, out_widths, name, tb=256):
    n = len(specs)
    diff_inputs = [i for i, s in enumerate(specs) if s[1]]

    @jax.custom_vjp
    def op(*ins):
        return tuple(_stage_fwd(fn, ins, specs, out_widths, name, tb))

    def op_fwd(*ins):
        return op(*ins), ins

    def op_bwd(ins, douts):
        gs = _stage_bwd(fn, ins, specs, out_widths, douts, name, tb)
        res = [None] * n
        for i, g in zip(diff_inputs, gs):
            res[i] = g
        return tuple(res)

    op.defvjp(op_fwd, op_bwd)
    return op


@jax.custom_vjp
def bdot(x, w):
    return jnp.dot(x.astype(BF16), w.astype(BF16), preferred_element_type=F32)


def _bdot_fwd(x, w):
    return bdot(x, w), (x, w)


def _bdot_bwd(res, dy):
    x, w = res
    dyb = dy.astype(BF16)
    dx = lax.dot_general(dyb, w.astype(BF16), (((1,), (1,)), ((), ())), preferred_element_type=F32)
    dw = lax.dot_general(x.astype(BF16), dyb, (((0,), (0,)), ((), ())), preferred_element_type=F32)
    return dx, dw


bdot.defvjp(_bdot_fwd, _bdot_bwd)


def _sdot_raw(x, c):
    hi = x.astype(BF16)
    r1 = x - hi.astype(F32)
    mid = r1.astype(BF16)
    lo = (r1 - mid.astype(F32)).astype(BF16)
    d = lambda u: jnp.dot(u, c, preferred_element_type=F32)
    return d(hi) + d(mid) + d(lo)


@jax.custom_vjp
def sdot(x, c, ct):
    return _sdot_raw(x, c)


def _sdot_fwd(x, c, ct):
    return _sdot_raw(x, c), (c, ct)


def _sdot_bwd(res, dy):
    c, ct = res
    return _sdot_raw(dy, ct), None, None


sdot.defvjp(_sdot_fwd, _sdot_bwd)


def _sigmoid(x):
    return 1.0 / (1.0 + jnp.exp(-x))


def _rms(x, g):
    return x * lax.rsqrt(jnp.mean(x * x, axis=-1, keepdims=True) + NORM_EPS) * g


def rmsnorm(x, g, name):
    op = stage_op(lambda xv, gv: (_rms(xv, gv),), [ROW(), FULL()], [x.shape[1]], name)
    return op(x, g.reshape(1, -1))[0]


def _mm_up_relu2(a, b, name):
    M, K = a.shape
    N = b.shape[1]
    tm = _pick(M, (1024, 512, 256, 128))
    tn = _pick(N, (512, 256, 128))

    def body(a_ref, b_ref, u_ref, act_ref):
        u = jnp.dot(a_ref[...], b_ref[...], preferred_element_type=F32)
        r = jnp.maximum(u, 0.0)
        u_ref[...] = u.astype(BF16)
        act_ref[...] = (r * r).astype(BF16)

    out = pl.BlockSpec((tm, tn), lambda i, j: (i, j))
    sh = jax.ShapeDtypeStruct((M, N), BF16)
    return pl.pallas_call(
        body, name=name, grid=(M // tm, N // tn),
        in_specs=[pl.BlockSpec((tm, K), lambda i, j: (i, 0)), pl.BlockSpec((K, tn), lambda i, j: (0, j))],
        out_specs=[out, out], out_shape=[sh, sh], compiler_params=_cparams(('parallel', 'parallel')),
    )(a, b)


def _mm_down_bwd(dy, b, u, name):
    M, N = dy.shape
    K = b.shape[0]
    tm = _pick(M, (1024, 512, 256, 128))
    tk = _pick(K, (512, 256, 128))

    def body(dy_ref, b_ref, u_ref, du_ref):
        d = lax.dot_general(dy_ref[...].astype(BF16), b_ref[...], (((1,), (1,)), ((), ())),
                            preferred_element_type=F32)
        du_ref[...] = (d * (2.0 * jnp.maximum(u_ref[...].astype(F32), 0.0))).astype(BF16)

    blk = pl.BlockSpec((tm, tk), lambda i, j: (i, j))
    return pl.pallas_call(
        body, name=name, grid=(M // tm, K // tk),
        in_specs=[pl.BlockSpec((tm, N), lambda i, j: (i, 0)), pl.BlockSpec((tk, N), lambda i, j: (j, 0)), blk],
        out_specs=blk, out_shape=jax.ShapeDtypeStruct((M, K), BF16),
        compiler_params=_cparams(('parallel', 'parallel')),
    )(dy, b, u)


_RMS_SPECS = [ROW(), FULL()]
_rms_fn = lambda xv, gv: (_rms(xv, gv),)


@functools.partial(jax.custom_vjp, nondiff_argnums=(6,))
def mlp_block(x, g, wup_b, wup_c, wdown_b, wdown_c, name):
    return _mlp_fwd(x, g, wup_b, wup_c, wdown_b, wdown_c, name)[0]


def _mlp_fwd(x, g, wup_b, wup_c, wdown_b, wdown_c, name):
    h = _stage_fwd(_rms_fn, (x, g), _RMS_SPECS, [x.shape[1]], name + '_norm', 256, out_dtype=BF16)[0]
    u, act = _mm_up_relu2(h, wup_b, name + '_up')
    y = _mm_nn(act, wdown_b, add=x, name=name + '_down')
    return y, (x, g, h, u, act, wup_b, wdown_b)


def _mlp_bwd(name, res, dy):
    x, g, h, u, act, wup_b, wdown_b = res
    du = _mm_down_bwd(dy, wdown_b, u, name + '_down_da')
    dwdown = _mm_tn(act, dy, name=name + '_down_dw')
    dh = _mm_nt(du, wup_b, name=name + '_up_da')
    dwup = _mm_tn(h, du, name=name + '_up_dw')
    dx, dg = _stage_bwd(_rms_fn, (x, g), _RMS_SPECS, [x.shape[1]], (dh,), name + '_norm', 256)
    return dx + dy, dg, None, dwup, None, dwdown


mlp_block.defvjp(_mlp_fwd, _mlp_bwd)


def _shift_down(x, rows):
    return jnp.where(rows == 0, 0.0, pltpu.roll(x, 1, 0))


def _shift_up(x, rows, T):
    return jnp.where(rows == T - 1, 0.0, pltpu.roll(x, T - 1, 0))


def _tshift_fwd_call(x, mu, name):
    T, C = x.shape

    def body(x_ref, mu_ref, o_ref):
        xv = x_ref[...]
        rows = lax.broadcasted_iota(jnp.int32, xv.shape, 0)
        o_ref[...] = xv + (_shift_down(xv, rows) - xv) * mu_ref[...]

    return pl.pallas_call(
        body, name=name + '_f', grid=(C // LANES,),
        in_specs=[pl.BlockSpec((T, LANES), lambda j: (0, j)), pl.BlockSpec((1, LANES), lambda j: (0, j))],
        out_specs=pl.BlockSpec((T, LANES), lambda j: (0, j)), out_shape=jax.ShapeDtypeStruct((T, C), F32),
        compiler_params=_cparams(('parallel',)),
    )(x, mu)


def _tshift_bwd_call(x, mu, dy, name):
    T, C = x.shape

    def body(x_ref, mu_ref, dy_ref, dx_ref, dmu_ref):
        xv, d = x_ref[...], dy_ref[...]
        rows = lax.broadcasted_iota(jnp.int32, xv.shape, 0)
        z = d * mu_ref[...]
        dx_ref[...] = d - z + _shift_up(z, rows, T)
        dmu_ref[...] = jnp.sum(d * (_shift_down(xv, rows) - xv), axis=0, keepdims=True)

    return pl.pallas_call(
        body, name=name + '_b', grid=(C // LANES,),
        in_specs=[pl.BlockSpec((T, LANES), lambda j: (0, j)), pl.BlockSpec((1, LANES), lambda j: (0, j)),
                  pl.BlockSpec((T, LANES), lambda j: (0, j))],
        out_specs=[pl.BlockSpec((T, LANES), lambda j: (0, j)), pl.BlockSpec((1, LANES), lambda j: (0, j))],
        out_shape=[jax.ShapeDtypeStruct((T, C), F32), jax.ShapeDtypeStruct((1, C), F32)],
        compiler_params=_cparams(('parallel',)),
    )(x, mu, dy)


@functools.partial(jax.custom_vjp, nondiff_argnums=(2,))
def token_shift_mix(x, mu, name):
    return _tshift_fwd_call(x, mu, name)


def _tsm_fwd(x, mu, name):
    return _tshift_fwd_call(x, mu, name), (x, mu)


def _tsm_bwd(name, res, dy):
    x, mu = res
    dx, dmu = _tshift_bwd_call(x, mu, dy, name)
    return dx, dmu


token_shift_mix.defvjp(_tsm_fwd, _tsm_bwd)


def _conv_specs(T):
    nb = CONV_WIDTH // LANES
    return [pl.BlockSpec((T, LANES), lambda j: (0, j)), pl.BlockSpec((T, LANES), lambda j: (0, nb + j)),
            pl.BlockSpec((T, LANES), lambda j: (0, 2 * nb + j)), pl.BlockSpec((3, LANES), lambda j: (0, j))]


def _conv_fwd_call(cv, w, name):
    T = cv.shape[0]

    def body(b_ref, c_ref, x_ref, w_ref, o_ref):
        u = c_ref[...] * x_ref[...]
        rows = lax.broadcasted_iota(jnp.int32, u.shape, 0)
        u1 = _shift_down(u, rows)
        u2 = _shift_down(u1, rows)
        o_ref[...] = b_ref[...] * (w_ref[0:1, :] * u2 + w_ref[1:2, :] * u1 + w_ref[2:3, :] * u)

    return pl.pallas_call(
        body, name=name + '_f', grid=(CONV_WIDTH // LANES,), in_specs=_conv_specs(T),
        out_specs=pl.BlockSpec((T, LANES), lambda j: (0, j)),
        out_shape=jax.ShapeDtypeStruct((T, CONV_WIDTH), F32), compiler_params=_cparams(('parallel',)),
    )(cv, cv, cv, w)


def _conv_bwd_call(cv, w, do, name):
    T = cv.shape[0]

    def body(b_ref, c_ref, x_ref, w_ref, do_ref, db_ref, dc_ref, dx_ref, dw_ref):
        c, x, d = c_ref[...], x_ref[...], do_ref[...]
        u = c * x
        rows = lax.broadcasted_iota(jnp.int32, u.shape, 0)
        u1 = _shift_down(u, rows)
        u2 = _shift_down(u1, rows)
        w0, w1, w2 = w_ref[0:1, :], w_ref[1:2, :], w_ref[2:3, :]
        db_ref[...] = d * (w0 * u2 + w1 * u1 + w2 * u)
        dy = d * b_ref[...]
        dy1 = _shift_up(dy, rows, T)
        dy2 = _shift_up(dy1, rows, T)
        du = w2 * dy + w1 * dy1 + w0 * dy2
        dc_ref[...] = du * x
        dx_ref[...] = du * c
        dw_ref[0:1, :] = jnp.sum(dy * u2, axis=0, keepdims=True)
        dw_ref[1:2, :] = jnp.sum(dy * u1, axis=0, keepdims=True)
        dw_ref[2:3, :] = jnp.sum(dy * u, axis=0, keepdims=True)

    blk = pl.BlockSpec((T, LANES), lambda j: (0, j))
    sh = jax.ShapeDtypeStruct((T, CONV_WIDTH), F32)
    return pl.pallas_call(
        body, name=name + '_b', grid=(CONV_WIDTH // LANES,), in_specs=_conv_specs(T) + [blk],
        out_specs=[blk, blk, blk, pl.BlockSpec((3, LANES), lambda j: (0, j))],
        out_shape=[sh, sh, sh, jax.ShapeDtypeStruct((3, CONV_WIDTH), F32)],
        compiler_params=_cparams(('parallel',)),
    )(cv, cv, cv, w, do)


@functools.partial(jax.custom_vjp, nondiff_argnums=(2,))
def short_conv(cv, w, name):
    return _conv_fwd_call(cv, w, name)


def _sc_fwd(cv, w, name):
    return _conv_fwd_call(cv, w, name), (cv, w)


def _sc_bwd(name, res, do):
    cv, w = res
    db, dc, dx, dw = _conv_bwd_call(cv, w, do, name)
    return jnp.concatenate([db, dc, dx], axis=1), dw


short_conv.defvjp(_sc_fwd, _sc_bwd)


ATT_SCALE = QK_HEAD ** -0.5
NPAIR = MLA_HEADS // 2


def _att_bq(T):
    return min(256, T)


def _att_masks(pair, j):
    lane = lax.broadcasted_iota(jnp.int32, (1, LANES), 1)
    mask_n = (lane // QK_NOPE) == j
    mask_r = (lane // (QK_ROPE // 2)) == (2 * pair + j)
    return mask_n, mask_r


def _att_probs(qcat, kcat, row0, stop):
    s = lax.dot_general(qcat, kcat, (((1,), (1,)), ((), ())), preferred_element_type=F32) * ATT_SCALE
    r = row0 + lax.broadcasted_iota(jnp.int32, s.shape, 0)
    c = lax.broadcasted_iota(jnp.int32, s.shape, 1)
    s = jnp.where(c <= r, s, -jnp.inf)
    e = jnp.exp(s - jnp.max(s, axis=-1, keepdims=True))
    return e / jnp.sum(e, axis=-1, keepdims=True)


def _att_in_specs(T):
    blk = lambda f: pl.BlockSpec((T, LANES), f)
    return [blk(lambda p: (0, p)), blk(lambda p: (0, 0)), blk(lambda p: (0, 0)),
            blk(lambda p: (0, p)), blk(lambda p: (0, 0)), blk(lambda p: (0, 0)), blk(lambda p: (0, p))]


def _att_fwd_call(qn, q1, q2, kn, k1, k2, v, name):
    T = qn.shape[0]
    bq = _att_bq(T)

    def body(qn_ref, q1_ref, q2_ref, kn_ref, k1_ref, k2_ref, v_ref, o_ref):
        pair = pl.program_id(0)
        for i in range(T // bq):
            r0, stop = i * bq, (i + 1) * bq
            kcat = jnp.concatenate([kn_ref[0:stop, :], k1_ref[0:stop, :], k2_ref[0:stop, :]], axis=1).astype(BF16)
            vb = v_ref[0:stop, :].astype(BF16)
            outs = []
            for j in range(2):
                mask_n, mask_r = _att_masks(pair, j)
                qcat = jnp.concatenate([jnp.where(mask_n, qn_ref[r0:stop, :], 0.0),
                                        jnp.where(mask_r, q1_ref[r0:stop, :], 0.0),
                                        jnp.where(mask_r, q2_ref[r0:stop, :], 0.0)], axis=1).astype(BF16)
                p = _att_probs(qcat, kcat, r0, stop)
                outs.append(jnp.dot(p.astype(BF16), vb, preferred_element_type=F32))
            mask_n0, _ = _att_masks(pair, 0)
            o_ref[r0:stop, :] = jnp.where(mask_n0, outs[0], outs[1])

    return pl.pallas_call(
        body, name=name + '_f', grid=(NPAIR,), in_specs=_att_in_specs(T),
        out_specs=pl.BlockSpec((T, LANES), lambda p: (0, p)),
        out_shape=jax.ShapeDtypeStruct((T, MLA_HEADS * V_HEAD), F32), compiler_params=_cparams(('parallel',)),
    )(qn, q1, q2, kn, k1, k2, v)


def _att_bwd_call(qn, q1, q2, kn, k1, k2, v, o, do, name):
    T = qn.shape[0]
    bq = _att_bq(T)

    def body(qn_ref, q1_ref, q2_ref, kn_ref, k1_ref, k2_ref, v_ref, o_ref, do_ref,
             dqn_ref, dq1_ref, dq2_ref, dkn_ref, dk1_ref, dk2_ref, dv_ref, dk_acc, dv_acc):
        pair = pl.program_id(0)

        @pl.when(pair == 0)
        def _():
            dq1_ref[...] = jnp.zeros(dq1_ref.shape, F32)
            dq2_ref[...] = jnp.zeros(dq2_ref.shape, F32)
            dk1_ref[...] = jnp.zeros(dk1_ref.shape, F32)
            dk2_ref[...] = jnp.zeros(dk2_ref.shape, F32)

        dk_acc[...] = jnp.zeros(dk_acc.shape, F32)
        dv_acc[...] = jnp.zeros(dv_acc.shape, F32)
        for i in range(T // bq):
            r0, stop = i * bq, (i + 1) * bq
            kcat = jnp.concatenate([kn_ref[0:stop, :], k1_ref[0:stop, :], k2_ref[0:stop, :]], axis=1).astype(BF16)
            vb = v_ref[0:stop, :].astype(BF16)
            dqn = jnp.zeros((bq, LANES), F32)
            for j in range(2):
                mask_n, mask_r = _att_masks(pair, j)
                qcat = jnp.concatenate([jnp.where(mask_n, qn_ref[r0:stop, :], 0.0),
                                        jnp.where(mask_r, q1_ref[r0:stop, :], 0.0),
                                        jnp.where(mask_r, q2_ref[r0:stop, :], 0.0)], axis=1).astype(BF16)
                p = _att_probs(qcat, kcat, r0, stop)
                dom = jnp.where(mask_n, do_ref[r0:stop, :], 0.0)
                delta = jnp.sum(dom * o_ref[r0:stop, :], axis=-1, keepdims=True)
                domb = dom.astype(BF16)
                dp = lax.dot_general(domb, vb, (((1,), (1,)), ((), ())), preferred_element_type=F32)
                ds = (p * (dp - delta) * ATT_SCALE).astype(BF16)
                dqc = jnp.dot(ds, kcat, preferred_element_type=F32)
                dqn = dqn + jnp.where(mask_n, dqc[:, 0:LANES], 0.0)
                dq1_ref[r0:stop, :] += jnp.where(mask_r, dqc[:, LANES:2 * LANES], 0.0)
                dq2_ref[r0:stop, :] += jnp.where(mask_r, dqc[:, 2 * LANES:3 * LANES], 0.0)
                dk_acc[0:stop, :] += lax.dot_general(ds, qcat, (((0,), (0,)), ((), ())),
                                                     preferred_element_type=F32)
                dv_acc[0:stop, :] += lax.dot_general(p.astype(BF16), domb, (((0,), (0,)), ((), ())),
                                                     preferred_element_type=F32)
            dqn_ref[r0:stop, :] = dqn
        dkn_ref[...] = dk_acc[:, 0:LANES]
        dk1_ref[...] += dk_acc[:, LANES:2 * LANES]
        dk2_ref[...] += dk_acc[:, 2 * LANES:3 * LANES]
        dv_ref[...] = dv_acc[...]

    per_pair = pl.BlockSpec((T, LANES), lambda p: (0, p))
    shared = pl.BlockSpec((T, LANES), lambda p: (0, 0))
    wide = jax.ShapeDtypeStruct((T, MLA_HEADS * QK_NOPE), F32)
    narrow = jax.ShapeDtypeStruct((T, LANES), F32)
    return pl.pallas_call(
        body, name=name + '_b', grid=(NPAIR,), in_specs=_att_in_specs(T) + [per_pair, per_pair],
        out_specs=[per_pair, shared, shared, per_pair, shared, shared, per_pair],
        out_shape=[wide, narrow, narrow, wide, narrow, narrow, wide],
        scratch_shapes=[pltpu.VMEM((T, 3 * LANES), F32), pltpu.VMEM((T, LANES), F32)],
        compiler_params=_cparams(('arbitrary',)),
    )(qn, q1, q2, kn, k1, k2, v, o, do)


@functools.partial(jax.custom_vjp, nondiff_argnums=(7,))
def attention(qn, q1, q2, kn, k1, k2, v, name):
    return _att_fwd_call(qn, q1, q2, kn, k1, k2, v, name)


def _attn_fwd(qn, q1, q2, kn, k1, k2, v, name):
    o = _att_fwd_call(qn, q1, q2, kn, k1, k2, v, name)
    return o, (qn, q1, q2, kn, k1, k2, v, o)


def _attn_bwd(name, res, do):
    return tuple(_att_bwd_call(*res, do, name))


attention.defvjp(_attn_fwd, _attn_bwd)


SCAN_CHUNK = 64
SCAN_UNROLL = 4


def _block_ones(n, seg):
    i = np.arange(n)
    return (i[:, None] // seg == i[None, :] // seg).astype(np.float32)


def _scan_diag():
    i = np.arange(RW_WIDTH)
    return jnp.asarray((np.arange(RW_N)[:, None] == (i[None, :] % RW_N)).astype(np.float32))


def _head_rowsum(x):
    low = lax.broadcasted_iota(jnp.int32, (1, LANES), 1) < RW_N
    tiles = []
    for j in range(RW_WIDTH // LANES):
        xt = x[:, j * LANES:(j + 1) * LANES]
        x0 = jnp.where(low, xt, 0.0)
        s0 = jnp.sum(x0, axis=-1, keepdims=True)
        s1 = jnp.sum(xt - x0, axis=-1, keepdims=True)
        tiles.append(jnp.where(low, s0, s1))
    return jnp.concatenate(tiles, axis=1)


def _unrolled_loop(n, step, init):
    def body(i, carry):
        for j in range(SCAN_UNROLL):
            carry = step(i * SCAN_UNROLL + j, carry)
        return carry
    return lax.fori_loop(0, n // SCAN_UNROLL, body, init)


def _scan_fwd_call(r, w, k, v, a, b, name):
    T = r.shape[0]
    tc = min(SCAN_CHUNK, T)
    dg = _scan_diag()

    def body(r_ref, w_ref, k_ref, v_ref, a_ref, b_ref, dg_ref, y_ref, st_ref, sa_ref, vc_ref, s_ref):
        @pl.when(pl.program_id(0) == 0)
        def _():
            s_ref[...] = jnp.zeros(s_ref.shape, F32)

        dgv = dg_ref[...]
        readout = lambda s, t: jnp.sum(_head_rowsum(s * r_ref[t]) * dgv, axis=0, keepdims=True)

        def step(t, carry):
            s, vcol = carry
            st_ref[t] = s
            vc_ref[t] = vcol
            sa = _head_rowsum(s * a_ref[t])
            sa_ref[t] = sa
            prev = jnp.maximum(t - 1, 0)
            y_ref[prev] = readout(s, prev)
            vcol_next = _head_rowsum(dgv * v_ref[jnp.minimum(t + 1, tc - 1)])
            sn = s * w_ref[t] + sa * b_ref[t] + vcol * k_ref[t]
            return sn, vcol_next

        s_end, _ = _unrolled_loop(tc, step, (s_ref[...], _head_rowsum(dgv * v_ref[0])))
        y_ref[tc - 1] = readout(s_end, tc - 1)
        s_ref[...] = s_end

    vec = pl.BlockSpec((tc, 1, RW_WIDTH), lambda i: (i, 0, 0))
    mat = pl.BlockSpec((tc, RW_N, RW_WIDTH), lambda i: (i, 0, 0))
    msh = jax.ShapeDtypeStruct((T, RW_N, RW_WIDTH), F32)
    return pl.pallas_call(
        body, name=name + '_f', grid=(T // tc,),
        in_specs=[vec] * 6 + [pl.BlockSpec((RW_N, RW_WIDTH), lambda i: (0, 0))],
        out_specs=[vec, mat, mat, mat],
        out_shape=[jax.ShapeDtypeStruct((T, 1, RW_WIDTH), F32), msh, msh, msh],
        scratch_shapes=[pltpu.VMEM((RW_N, RW_WIDTH), F32)],
        compiler_params=_cparams(('arbitrary',)),
    )(r, w, k, v, a, b, dg)


def _scan_bwd_call(r, w, k, a, b, st, sa_all, vc_all, dy, name):
    T = r.shape[0]
    tc = min(SCAN_CHUNK, T)
    nt = T // tc
    dg = _scan_diag()

    def body(r_ref, w_ref, k_ref, a_ref, b_ref, st_ref, sa_ref, vc_ref, dy_ref, dg_ref,
             dr_ref, dw_ref, dk_ref, dv_ref, da_ref, db_ref, ds_ref):
        @pl.when(pl.program_id(0) == 0)
        def _():
            ds_ref[...] = jnp.zeros(ds_ref.shape, F32)

        dgv = dg_ref[...]
        colsum = lambda x: jnp.sum(x, axis=0, keepdims=True)

        def step(i, carry):
            ds, dycol = carry
            t = tc - 1 - i
            sp = st_ref[t]
            rt, wt, kt, at, bt = r_ref[t], w_ref[t], k_ref[t], a_ref[t], b_ref[t]
            ds = ds + dycol * rt
            dsa = _head_rowsum(ds * bt)
            sa, vcol = sa_ref[t], vc_ref[t]
            dycol_next = _head_rowsum(dgv * dy_ref[jnp.maximum(t - 1, 0)])
            sn = sp * wt + sa * bt + vcol * kt
            dr_ref[t] = colsum(sn * dycol)
            dk_ref[t] = colsum(ds * vcol)
            db_ref[t] = colsum(ds * sa)
            dw_ref[t] = colsum(ds * sp)
            dv_ref[t] = colsum(_head_rowsum(ds * kt) * dgv)
            da_ref[t] = colsum(sp * dsa)
            return ds * wt + dsa * at, dycol_next

        ds_end, _ = _unrolled_loop(tc, step, (ds_ref[...], _head_rowsum(dgv * dy_ref[tc - 1])))
        ds_ref[...] = ds_end

    vec = pl.BlockSpec((tc, 1, RW_WIDTH), lambda i: (nt - 1 - i, 0, 0))
    mat = pl.BlockSpec((tc, RW_N, RW_WIDTH), lambda i: (nt - 1 - i, 0, 0))
    vsh = jax.ShapeDtypeStruct((T, 1, RW_WIDTH), F32)
    return pl.pallas_call(
        body, name=name + '_b', grid=(nt,),
        in_specs=[vec] * 5 + [mat] * 3 + [vec, pl.BlockSpec((RW_N, RW_WIDTH), lambda i: (0, 0))],
        out_specs=[vec] * 6, out_shape=[vsh] * 6,
        scratch_shapes=[pltpu.VMEM((RW_N, RW_WIDTH), F32)],
        compiler_params=_cparams(('arbitrary',)),
    )(r, w, k, a, b, st, sa_all, vc_all, dy, dg)


@functools.partial(jax.custom_vjp, nondiff_argnums=(6,))
def wkv7(r, w, k, v, a, b, name):
    return _scan_fwd_call(r, w, k, v, a, b, name)[0]


def _wkv7_fwd(r, w, k, v, a, b, name):
    y, st, sa_all, vc_all = _scan_fwd_call(r, w, k, v, a, b, name)
    return y, (r, w, k, a, b, st, sa_all, vc_all)


def _wkv7_bwd(name, res, dy):
    return tuple(_scan_bwd_call(*res, dy, name))


wkv7.defvjp(_wkv7_fwd, _wkv7_bwd)


def _np_bf16(a):
    return jnp.asarray(a, BF16)


def _mla_consts():
    seg_n = (np.arange(512)[:, None] // QK_NOPE == np.arange(LANES)[None, :]).astype(np.float32)
    seg_r = (np.arange(LANES)[:, None] // 16 == np.arange(LANES)[None, :]).astype(np.float32)
    e1 = np.zeros((LANES, LANES), np.float32)
    e2 = np.zeros((LANES, LANES), np.float32)
    for h in range(MLA_HEADS):
        for i in range(16):
            e1[i, h * 16 + i] = 1.0
            e2[16 + i, h * 16 + i] = 1.0
    mats = [seg_n, seg_n.T, seg_r, seg_r.T, e1, e1.T, e2, e2.T]
    return [_np_bf16(m) for m in mats]


def _qk_prep_fn(qn, q1, q2, kn, kx, cos, sin, gqn, gq1, gq2, gkn, gk1, gk2,
                seg_n, seg_nt, seg_r, seg_rt, e1, e1t, e2, e2t):
    def normrope(xn, x1, x2, gn, g1, g2):
        ss = sdot(xn * xn, seg_n, seg_nt) + sdot(x1 * x1, seg_r, seg_rt) + sdot(x2 * x2, seg_r, seg_rt)
        inv = lax.rsqrt(ss * (1.0 / QK_HEAD) + NORM_EPS)
        inv_n = sdot(inv, seg_nt, seg_n)
        inv_r = sdot(inv, seg_rt, seg_r)
        y1 = x1 * inv_r * g1
        y2 = x2 * inv_r * g2
        return xn * inv_n * gn, y1 * cos - y2 * sin, y1 * sin + y2 * cos

    k1 = sdot(kx, e1, e1t)
    k2 = sdot(kx, e2, e2t)
    return normrope(qn, q1, q2, gqn, gq1, gq2) + normrope(kn, k1, k2, gkn, gk1, gk2)


def _rwkv_prep_fn(vres):
    def fn(r, k, v, xg, xwa, kx, *rest):
        if vres:
            vfirst, w0, a0, k_k, k_a, w2p, a2p, g2, v0, v2p, bm = rest
        else:
            w0, a0, k_k, k_a, w2p, a2p, g2, bm = rest
        z = w0 + bdot(jnp.tanh(xwa), w2p)
        nz = -z
        softplus = jnp.maximum(nz, 0.0) + jnp.log(1.0 + jnp.exp(-jnp.abs(nz)))
        decay = jnp.exp(-jnp.exp(-softplus - 0.5))
        a = _sigmoid(a0 + bdot(xwa, a2p))
        g = bdot(_sigmoid(xg), g2)
        if vres:
            vv = v + (vfirst - v) * _sigmoid(v0 + bdot(kx, v2p))
        else:
            vv = v
        kkr = k * k_k
        kk = kkr / jnp.maximum(jnp.sqrt(sdot(kkr * kkr, bm, bm)), 1e-12)
        k2 = k * (1.0 + (a - 1.0) * k_a)
        return r * 1.0, decay, k2, vv, -kk, kk * a, g
    return fn


def _rwkv_post_fn(y, r, k2, vv, g, ln_w, ln_b, rk, bm):
    inv_n = 1.0 / RW_N
    mean = sdot(y, bm, bm) * inv_n
    yc = y - mean
    var = sdot(yc * yc, bm, bm) * inv_n
    yn = yc * lax.rsqrt(var + GN_EPS) * ln_w + ln_b
    bonus = sdot(r * k2 * rk, bm, bm) * vv
    return ((yn + bonus) * g,)


def _merge_fn(g0, g1, g2, oa, ob, oc):
    return (_sigmoid(g0) * oa + _sigmoid(g1) * ob + _sigmoid(g2) * oc,)


def _relu2_fn(u):
    r = jnp.maximum(u, 0.0)
    return (r * r,)


def _loss_call(y, target):
    T, C = y.shape
    tb = min(256, T)

    def body(y_ref, t_ref, dy_ref, part_ref):
        err = y_ref[...] - t_ref[...]
        dy_ref[...] = err * (1.0 / C)
        sq = jnp.sum(err * err, axis=0, keepdims=True)
        acc = sq[:, 0:LANES]
        for j in range(1, C // LANES):
            acc = acc + sq[:, j * LANES:(j + 1) * LANES]
        part_ref[...] = jnp.zeros(part_ref.shape, F32)
        part_ref[0:1, :] = acc * (0.5 / C)

    return pl.pallas_call(
        body, name='loss', grid=(T // tb,),
        in_specs=[pl.BlockSpec((tb, C), lambda i: (i, 0))] * 2,
        out_specs=[pl.BlockSpec((tb, C), lambda i: (i, 0)), pl.BlockSpec((8, LANES), lambda i: (i, 0))],
        out_shape=[jax.ShapeDtypeStruct((T, C), F32), jax.ShapeDtypeStruct((8 * (T // tb), LANES), F32)],
        compiler_params=_cparams(('parallel',)),
    )(y, target)


def _cols(w, lo, hi):
    return w.map(lambda t: t[:, lo:hi])


def _pad_rows(t, before, total):
    return jnp.pad(t, ((before, total - before - t.shape[0]), (0, 0)))


def _head_tile(g, lo, hi):
    return jnp.tile(g[lo:hi], MLA_HEADS).reshape(1, -1)


def _layer(l, x, v_first, wd, sp, cos, sin):
    T = x.shape[0]
    nm = f'l{l}'
    vres = l > 0
    w_in = wd['w_in']
    if vres:
        v1t = wd['rwkv_v1'].map(lambda t: t.T)
    else:
        v1t = W(jnp.zeros((MV_LORA, D_MODEL), BF16), jnp.zeros((MV_LORA, D_MODEL), F32))
    zpad = W(jnp.zeros((64, D_MODEL), BF16), jnp.zeros((64, D_MODEL), F32))
    w_rw = _wcat([w_in['rkv'], w_in['xg'], w_in['xwa'], w_in['kpe'], v1t, zpad], 0)
    h = rmsnorm(x, sp['attn_norm'], nm + '_anorm')
    gates, cq, ckv, rw, cv = multi_linear_t(
        h, [w_in['gates'], w_in['cq'], w_in['ckv'], w_rw, w_in['conv']], nm + '_win')

    v_mu = sp['rwkv_v_mu'] if vres else jnp.zeros((MV_LORA,), F32)
    mu_all = jnp.concatenate([sp['rwkv_mu'][0:768], sp['rwkv_mu'][896:1024], sp['rwkv_mu'][768:896],
                              jnp.zeros((QK_ROPE,), F32), v_mu, jnp.zeros((64,), F32)]).reshape(1, -1)
    rws = token_shift_mix(rw, mu_all, nm + '_shift')

    cqn = rmsnorm(cq, sp['mla_q_a_norm'], nm + '_qan')
    ckvn = rmsnorm(ckv, sp['mla_kv_a_norm'], nm + '_kvan')
    wq = wd['mla_wq_b'].map(lambda t: jnp.concatenate(
        [t.reshape(Q_LORA, MLA_HEADS, QK_HEAD)[:, :, 0:64].reshape(Q_LORA, 512),
         t.reshape(Q_LORA, MLA_HEADS, QK_HEAD)[:, :, 64:80].reshape(Q_LORA, 128),
         t.reshape(Q_LORA, MLA_HEADS, QK_HEAD)[:, :, 80:96].reshape(Q_LORA, 128)], axis=1))
    wkn = wd['mla_wkv_b'].map(lambda t: t.reshape(KV_LORA, MLA_HEADS, 128)[:, :, 0:64].reshape(KV_LORA, 512))
    wv = wd['mla_wkv_b'].map(lambda t: t.reshape(KV_LORA, MLA_HEADS, 128)[:, :, 64:128].reshape(KV_LORA, 512))
    q = linear(cqn, wq, name=nm + '_wq')
    kn, vv_att = multi_linear(ckvn, [wkn, wv], nm + '_wkv')
    gq, gk = sp['mla_q_norm'], sp['mla_k_norm']
    consts = _mla_consts()
    qk_specs = ([ROW(pieces=((0, 512), (512, 128), (640, 128))), ROW(), ROW(pieces=((1024, 128),)),
                 ROW(False), ROW(False)] + [FULL()] * 6 + [FULL(False)] * 8)
    qk_op = stage_op(_qk_prep_fn, qk_specs, [512, 128, 128, 512, 128, 128], nm + '_qkprep')
    Qn, Q1, Q2, Kn, K1, K2 = qk_op(q, kn, rws, cos, sin,
                                   _head_tile(gq, 0, 64), _head_tile(gq, 64, 80), _head_tile(gq, 80, 96),
                                   _head_tile(gk, 0, 64), _head_tile(gk, 64, 80), _head_tile(gk, 80, 96), *consts)
    o_att = attention(Qn, Q1, Q2, Kn, K1, K2, vv_att, nm + '_att')
    o_a = linear(o_att, wd['mla_w_o'], name=nm + '_wo')

    bm = _np_bf16(_block_ones(RW_WIDTH, RW_N))
    vec = lambda n: sp[n].reshape(1, -1)
    f32w = lambda n: wd[n].c + wd[n].b.astype(F32)
    w2p = _pad_rows(f32w('rwkv_w2'), 0, 128)
    a2p = _pad_rows(f32w('rwkv_a2'), 64, 128)
    g2 = f32w('rwkv_g2')
    rw_pieces = ((0, 256), (256, 256), (512, 256), (768, 128), (896, 128), (1024, 128))
    if vres:
        v2p = _pad_rows(f32w('rwkv_v2'), 32, 128)
        prep_specs = [ROW(pieces=rw_pieces), ROW()] + [FULL()] * 9 + [FULL(False)]
        prep_in = [rws, v_first, vec('rwkv_w0'), vec('rwkv_a0'), vec('rwkv_k_k'), vec('rwkv_k_a'), w2p, a2p, g2,
                   vec('rwkv_v0'), v2p, bm]
    else:
        prep_specs = [ROW(pieces=rw_pieces)] + [FULL()] * 7 + [FULL(False)]
        prep_in = [rws, vec('rwkv_w0'), vec('rwkv_a0'), vec('rwkv_k_k'), vec('rwkv_k_a'), w2p, a2p, g2, bm]
    prep_op = stage_op(_rwkv_prep_fn(vres), prep_specs, [256] * 7, nm + '_rwprep')
    r_, dec, k2, vv, an, bn, g = prep_op(*prep_in)
    if not vres:
        v_first = vv
    t3 = lambda t: t.reshape(T, 1, RW_WIDTH)
    y = wkv7(t3(r_), t3(dec), t3(k2), t3(vv), t3(an), t3(bn), nm + '_scan').reshape(T, RW_WIDTH)
    post_op = stage_op(_rwkv_post_fn, [ROW()] * 5 + [FULL()] * 3 + [FULL(False)], [256], nm + '_rwpost')
    yb = post_op(y, r_, k2, vv, g, vec('rwkv_ln_w'), vec('rwkv_ln_b'), sp['rwkv_r_k'].reshape(1, -1), bm)[0]
    o_b = linear(yb, wd['rwkv_w_o'], name=nm + '_rwo')

    oc_in = short_conv(cv, f32w('conv_w'), nm + '_conv')
    o_c = linear(oc_in, wd['conv_w_o'], name=nm + '_cwo')

    merge_op = stage_op(_merge_fn, [ROW(pieces=((0, 1024), (1024, 1024), (2048, 1024))), ROW(), ROW(), ROW()],
                        [D_MODEL], nm + '_merge')
    merged = merge_op(gates, o_a, o_b, o_c)[0]
    x2 = linear(merged, wd['w_out'], add=x, name=nm + '_wout')
    x3 = mlp_block(x2, sp['mlp_norm'].reshape(1, -1), wd['w_up'].b, wd['w_up'].c, wd['w_down'].b, wd['w_down'].c,
                   nm + '_mlp')
    return x3, v_first


def _entries(ml):
    out = []
    for name, (layers, shape, axis) in SHARDED.items():
        l = ml if layers == DEPTH else ml - 1
        if not 0 <= l < layers:
            continue
        n = shape[0] * shape[1] // N_DEV
        if name == 'conv_w':
            out.append(('conv_w_hi', name, l, n))
            out.append(('conv_w_lo', name, l, n))
        else:
            out.append((name, name, l, n))
    return out


def _slot_size(n):
    return -(-n // LANES) * LANES


def _pack_rows(ml):
    total = sum(_slot_size(n) for _, _, _, n in _entries(ml))
    rows = -(-total // LANES)
    return -(-rows // PACK_ROW_MULT) * PACK_ROW_MULT


def _pack_flat(pieces, ml):
    rows = _pack_rows(ml)
    padded = []
    for p, (_, _, _, n) in zip(pieces, _entries(ml)):
        pad = _slot_size(n) - n
        if pad:
            p = jnp.pad(p, [(0, 0)] * (p.ndim - 1) + [(0, pad)])
        padded.append(p)
    flat = jnp.concatenate(padded, axis=-1)
    tail = rows * LANES - flat.shape[-1]
    if tail:
        flat = jnp.pad(flat, [(0, 0)] * (flat.ndim - 1) + [(0, tail)])
    return flat.reshape(flat.shape[:-1] + (rows, LANES))


def _unpack_flat(buf, ml):
    out, row = [], 0
    for _, _, _, n in _entries(ml):
        nrows = _slot_size(n) // LANES
        piece = buf[..., row:row + nrows, :].reshape(buf.shape[:-2] + (-1,))
        out.append(piece[..., :n])
        row += nrows
    return out


def _pack_shards(shards, ml, dtype, split_conv):
    pieces = []
    for slot, name, l, n in _entries(ml):
        a = shards[name][l]
        a = (a.T if name in TRANSPOSED else a).reshape(-1)
        if slot == 'conv_w_hi':
            a = a.astype(BF16).astype(F32) if split_conv else a
        elif slot == 'conv_w_lo':
            a = (a - a.astype(BF16).astype(F32)) if split_conv else jnp.zeros_like(a)
        pieces.append(a.astype(dtype))
    return _pack_flat(pieces, ml)


def _unpack_shards(buf, ml):
    out = {}
    for (slot, name, l, n), v in zip(_entries(ml), _unpack_flat(buf, ml)):
        if slot == 'conv_w_lo':
            continue
        layers, shape, axis = SHARDED[name]
        sshape = (shape[0] // N_DEV, shape[1]) if axis == 0 else (shape[0], shape[1] // N_DEV)
        out[(name, l)] = v.reshape(sshape[::-1]).T if name in TRANSPOSED else v.reshape(sshape)
    return out


def _to_full(blocks, shape, axis):
    if axis == 0:
        return blocks.reshape(shape)
    return blocks.reshape(N_DEV, shape[0], shape[1] // N_DEV).transpose(1, 0, 2).reshape(shape)


def _to_blocks(full, axis):
    r, c = full.shape
    if axis == 0:
        return full.reshape(N_DEV, -1)
    return full.reshape(r, N_DEV, c // N_DEV).transpose(1, 0, 2).reshape(N_DEV, -1)


def _unpack_gathered(gathered, ml):
    out, conv_hi = {}, None
    for (slot, name, l, n), v in zip(_entries(ml), _unpack_flat(gathered, ml)):
        layers, shape, axis = SHARDED[name]
        if name in TRANSPOSED:
            out[name] = v.reshape(-1, 16, LANES)
            continue
        full = _to_full(v, shape, axis)
        if slot == 'conv_w_hi':
            conv_hi = full
        elif slot == 'conv_w_lo':
            out[name] = conv_hi.astype(F32) + full.astype(F32)
        else:
            out[name] = full
    return out


def _pack_grads(grads, ml):
    pieces = []
    for slot, name, l, n in _entries(ml):
        if name in TRANSPOSED:
            blocks = jnp.concatenate(grads[name], axis=0).reshape(N_DEV, -1)
        else:
            blocks = _to_blocks(grads[name], SHARDED[name][2])
        if slot == 'conv_w_lo':
            blocks = jnp.zeros_like(blocks)
        pieces.append(blocks)
    return _pack_flat(pieces, ml)


def _my_pos():
    return lax.axis_index('x'), lax.axis_index('y'), lax.axis_index('c')


def _flip(v, bit):
    return 1 - v if bit else v


def all_gather_blocks(x):
    rows = x.shape[0]

    def body(x_ref, out_ref, send_sems, recv_sems, local_sem):
        mx, my, mc = _my_pos()
        me, sibling = (mx, my, mc), (mx, my, 1 - mc)
        chips = [(1 - mx, my), (mx, 1 - my), (1 - mx, 1 - my)]

        def block(px, py, pc):
            return out_ref.at[4 * px + 2 * py + pc]

        def copy(k, blk, to, src=None):
            return pltpu.make_async_remote_copy(
                src_ref=block(*blk) if src is None else src, dst_ref=block(*blk),
                send_sem=send_sems.at[k], recv_sem=recv_sems.at[k], device_id=to, device_id_type=MESH)

        mine = pltpu.make_async_copy(x_ref, block(*me), local_sem)
        mine.start()
        first = [copy(0, me, sibling, src=x_ref)]
        first += [copy(1 + j, me, (*chip, mc), src=x_ref) for j, chip in enumerate(chips)]
        for cp in first:
            cp.start()
        passed = [copy(4 + j, (*chip, mc), sibling) for j, chip in enumerate(chips)]
        for j, chip in enumerate(chips):
            copy(1 + j, (*chip, mc), me).wait_recv()
            passed[j].start()
        copy(0, sibling, me).wait_recv()
        for j, chip in enumerate(chips):
            copy(4 + j, (*chip, 1 - mc), me).wait_recv()
        for cp in first + passed:
            cp.wait_send()
        mine.wait()

    return pl.pallas_call(
        body, name='all_gather_weights',
        out_shape=jax.ShapeDtypeStruct((N_DEV, rows, LANES), x.dtype),
        in_specs=[pl.BlockSpec(memory_space=pl.ANY)], out_specs=pl.BlockSpec(memory_space=pl.ANY),
        scratch_shapes=[pltpu.SemaphoreType.DMA((7,)), pltpu.SemaphoreType.DMA((7,)), pltpu.SemaphoreType.DMA],
    )(x)


def exchange_blocks(g):
    def body(g_ref, out_ref, send_sems, recv_sems, local_sem):
        mx, my, mc = _my_pos()
        me = 4 * mx + 2 * my + mc
        mine = pltpu.make_async_copy(g_ref.at[me], out_ref.at[me], local_sem)
        mine.start()
        copies, arrivals = [], []
        for k in range(1, N_DEV):
            peer = (_flip(mx, k & 4), _flip(my, k & 2), _flip(mc, k & 1))
            pidx = 4 * peer[0] + 2 * peer[1] + peer[2]
            copies.append(pltpu.make_async_remote_copy(
                src_ref=g_ref.at[pidx], dst_ref=out_ref.at[me], send_sem=send_sems.at[k - 1],
                recv_sem=recv_sems.at[k - 1], device_id=peer, device_id_type=MESH))
            arrivals.append(pltpu.make_async_remote_copy(
                src_ref=g_ref.at[me], dst_ref=out_ref.at[pidx], send_sem=send_sems.at[k - 1],
                recv_sem=recv_sems.at[k - 1], device_id=peer, device_id_type=MESH))
        for cp in copies:
            cp.start()
        for cp in arrivals:
            cp.wait_recv()
        for cp in copies:
            cp.wait_send()
        mine.wait()

    return pl.pallas_call(
        body, name='exchange_grads',
        out_shape=jax.ShapeDtypeStruct(g.shape, g.dtype),
        in_specs=[pl.BlockSpec(memory_space=pl.ANY)], out_specs=pl.BlockSpec(memory_space=pl.ANY),
        scratch_shapes=[pltpu.SemaphoreType.DMA((7,)), pltpu.SemaphoreType.DMA((7,)), pltpu.SemaphoreType.DMA],
    )(g)


HBM_SPEC = pl.BlockSpec(memory_space=pltpu.HBM)
SEM_SPEC = pl.BlockSpec(memory_space=pltpu.SEMAPHORE)
DATAFLOW_EFFECT = pltpu.SideEffectType.DATAFLOW_SIDE_EFFECTING


def _direct_copies(src_ref, land_ref, send_sems, recv_sems, per_peer):
    mx, my, mc = _my_pos()
    me = 4 * mx + 2 * my + mc
    copies = []
    for k in range(1, N_DEV):
        peer = (_flip(mx, k & 4), _flip(my, k & 2), _flip(mc, k & 1))
        pidx = 4 * peer[0] + 2 * peer[1] + peer[2]
        copies.append(pltpu.make_async_remote_copy(
            src_ref=src_ref.at[pidx] if per_peer else src_ref, dst_ref=land_ref.at[me],
            send_sem=send_sems.at[k - 1], recv_sem=recv_sems.at[k - 1], device_id=peer, device_id_type=MESH))
    return copies


def send_start(src, per_peer, name):
    block = src.shape[1:] if per_peer else src.shape
    land_shape = (N_DEV,) + tuple(block)

    def body(src_ref, land_ref, send_sems, recv_sems, src_thru, land_thru, token):
        for cp in _direct_copies(src_ref, land_ref, send_sems, recv_sems, per_peer):
            cp.start()
        token[...] = jnp.zeros(token.shape, F32)

    send_sems, recv_sems, src_thru, land_thru, token = pl.pallas_call(
        body, name=name,
        out_shape=(pltpu.SemaphoreType.DMA((N_DEV - 1,)), pltpu.SemaphoreType.DMA((N_DEV - 1,)),
                   pltpu.HBM(src.shape, src.dtype), pltpu.HBM(land_shape, src.dtype),
                   jax.ShapeDtypeStruct((8, LANES), F32)),
        in_specs=(HBM_SPEC, HBM_SPEC),
        out_specs=(SEM_SPEC, SEM_SPEC, HBM_SPEC, HBM_SPEC, pl.BlockSpec(memory_space=pltpu.VMEM)),
        input_output_aliases={0: 2, 1: 3},
        compiler_params=pltpu.CompilerParams(has_side_effects=DATAFLOW_EFFECT),
    )(pltpu.with_memory_space_constraint(src, pltpu.HBM),
      pltpu.with_memory_space_constraint(lax.empty(land_shape, src.dtype), pltpu.HBM))
    return (send_sems, recv_sems, src_thru, land_thru), token[0, 0]


def send_wait(handles, after, per_peer, name):
    send_sems, recv_sems, src_thru, land_thru = handles

    def body(src_ref, land_ref, send_sems, recv_sems, after_ref, src_dead, got_ref):
        for cp in _direct_copies(src_ref, land_ref, send_sems, recv_sems, per_peer):
            cp.wait_send()
            cp.wait_recv()

    return pl.pallas_call(
        body, name=name,
        out_shape=(pltpu.HBM(src_thru.shape, src_thru.dtype), pltpu.HBM(land_thru.shape, land_thru.dtype)),
        in_specs=(HBM_SPEC, HBM_SPEC, SEM_SPEC, SEM_SPEC, pl.BlockSpec(memory_space=pl.ANY)),
        out_specs=(HBM_SPEC, HBM_SPEC), input_output_aliases={0: 0, 1: 1},
        compiler_params=pltpu.CompilerParams(has_side_effects=DATAFLOW_EFFECT),
    )(src_thru, land_thru, send_sems, recv_sems, after)[1]


def _adamw_math(w, g, m, v):
    m2 = ADAM_B1 * m + (1.0 - ADAM_B1) * g
    v2 = ADAM_B2 * v + (1.0 - ADAM_B2) * (g * g)
    m_hat = m2 / (1.0 - ADAM_B1 ** ADAM_STEP)
    v_hat = v2 / (1.0 - ADAM_B2 ** ADAM_STEP)
    delta = -ADAM_LR * (m_hat / (jnp.sqrt(v_hat) + ADAM_EPS) + ADAM_WD * w)
    return delta, m2, v2


def sum_parts(parts, name):
    rows = parts.shape[1]
    rb = PACK_ROW_MULT

    def body(p_ref, g_ref):
        g = p_ref[0].astype(F32)
        for j in range(1, N_DEV):
            g = g + p_ref[j].astype(F32)
        g_ref[...] = g

    return pl.pallas_call(
        body, name=name, grid=(rows // rb,),
        in_specs=[pl.BlockSpec((N_DEV, rb, LANES), lambda i: (0, i, 0))],
        out_specs=pl.BlockSpec((rb, LANES), lambda i: (i, 0)),
        out_shape=jax.ShapeDtypeStruct((rows, LANES), F32), compiler_params=_cparams(('parallel',)),
    )(parts)


ADAMW_BLOCK_BYTES = 1 << 20


def adamw_weight(name, w, m, v, grads):
    layers, a, b = w.shape
    ra = a
    while ra * b * 4 > ADAMW_BLOCK_BYTES and ra % 16 == 0:
        ra //= 2

    def body(*refs):
        w_ref, m_ref, v_ref = refs[:3]
        g_refs = refs[3:3 + layers]
        g_ref, d_ref, m2_ref, v2_ref = refs[3 + layers:]
        g = g_refs[0][...]
        for l in range(1, layers):
            g = jnp.where(pl.program_id(0) == l, g_refs[l][...], g)
        delta, m2, v2 = _adamw_math(w_ref[0], g, m_ref[0], v_ref[0])
        g_ref[0] = g
        d_ref[0] = delta
        m2_ref[0] = m2
        v2_ref[0] = v2

    blk = pl.BlockSpec((1, ra, b), lambda l, i: (l, i, 0))
    gblk = pl.BlockSpec((ra, b), lambda l, i: (i, 0))
    sh = jax.ShapeDtypeStruct(w.shape, F32)
    return pl.pallas_call(
        body, name='adamw_' + name, grid=(layers, a // ra), in_specs=[blk] * 3 + [gblk] * layers,
        out_specs=[blk] * 4, out_shape=[sh] * 4, compiler_params=_cparams(('parallel', 'parallel')),
    )(w, m, v, *grads)


def allreduce_adamw_small(g, w, m, v):
    rows = g.shape[0]

    def body(g_ref, w_ref, m_ref, v_ref, gs_ref, d_ref, m2_ref, v2_ref, all_ref, send_sems, recv_sems):
        mx, my, mc = _my_pos()
        me, sibling = (mx, my, mc), (mx, my, 1 - mc)
        chips = [(1 - mx, my), (mx, 1 - my), (1 - mx, 1 - my)]

        def block(px, py, pc):
            return all_ref.at[4 * px + 2 * py + pc]

        def copy(k, blk, to, src=None):
            return pltpu.make_async_remote_copy(
                src_ref=block(*blk) if src is None else src, dst_ref=block(*blk),
                send_sem=send_sems.at[k], recv_sem=recv_sems.at[k], device_id=to, device_id_type=MESH)

        first = [copy(0, me, sibling, src=g_ref)]
        first += [copy(1 + j, me, (*chip, mc), src=g_ref) for j, chip in enumerate(chips)]
        for cp in first:
            cp.start()
        passed = [copy(4 + j, (*chip, mc), sibling) for j, chip in enumerate(chips)]
        for j, chip in enumerate(chips):
            copy(1 + j, (*chip, mc), me).wait_recv()
            passed[j].start()
        copy(0, sibling, me).wait_recv()
        for j, chip in enumerate(chips):
            copy(4 + j, (*chip, 1 - mc), me).wait_recv()
        for cp in first + passed:
            cp.wait_send()
        my_idx = 4 * mx + 2 * my + mc
        total = jnp.zeros((rows, LANES), F32)
        for j in range(N_DEV):
            total = total + jnp.where(my_idx == j, g_ref[...], all_ref[j])
        delta, m2, v2 = _adamw_math(w_ref[...], total, m_ref[...], v_ref[...])
        gs_ref[...] = total
        d_ref[...] = delta
        m2_ref[...] = m2
        v2_ref[...] = v2

    vm = pl.BlockSpec(memory_space=pltpu.VMEM)
    sh = jax.ShapeDtypeStruct((rows, LANES), F32)
    return pl.pallas_call(
        body, name='allreduce_adamw_small', in_specs=[vm] * 4, out_specs=[vm] * 4, out_shape=[sh] * 4,
        scratch_shapes=[pltpu.VMEM((N_DEV, rows, LANES), F32), pltpu.SemaphoreType.DMA((7,)),
                        pltpu.SemaphoreType.DMA((7,))],
    )(g, w, m, v)


def _small_pack(d):
    flat = jnp.concatenate([d[n].reshape(-1) for n in SMALL_NAMES])
    rows = -(-flat.shape[0] // (8 * LANES)) * 8
    return jnp.pad(flat, (0, rows * LANES - flat.shape[0])).reshape(rows, LANES)


def _small_unpack(buf, like):
    flat = buf.reshape(-1)
    out, off = {}, 0
    for n in SMALL_NAMES:
        sz = int(np.prod(like[n].shape))
        out[n] = flat[off:off + sz].reshape(like[n].shape)
        off += sz
    return out


def _rope_tables(positions):
    freqs = ROPE_THETA ** (-(jnp.arange(QK_ROPE // 2, dtype=F32) * 2.0 / QK_ROPE))
    ang = positions.astype(F32)[:, None] * freqs
    return jnp.tile(jnp.cos(ang), (1, MLA_HEADS)), jnp.tile(jnp.sin(ang), (1, MLA_HEADS))


def _layer_weights(gathered, carriers):
    wd = {}
    for name, full in gathered.items():
        if name == 'w_in':
            wd[name] = {seg: W(full[lo // 2:hi // 2].reshape(hi - lo, D_MODEL), c)
                        for (seg, lo, hi), c in zip(WIN_SEGS, carriers[name])}
        else:
            wd[name] = W(full, carriers[name])
    return wd


def _make_carriers(gathered):
    gathered, carriers = dict(gathered), {}
    for name, full in gathered.items():
        if name == 'w_in':
            carriers[name] = tuple(jnp.zeros((hi - lo, D_MODEL), F32) for _, lo, hi in WIN_SEGS)
        elif name == 'conv_w':
            gathered[name] = full.astype(BF16)
            carriers[name] = full - full.astype(BF16).astype(F32)
        else:
            carriers[name] = jnp.zeros(SHARDED[name][1], F32)
    return gathered, carriers


def _layer_small(small, ml):
    out = {}
    for n in SMALL_NAMES:
        l = ml if small[n].shape[0] == DEPTH else ml - 1
        if 0 <= l < small[n].shape[0]:
            out[n] = small[n][l]
    return out


def _train_step(x, positions, loss_target, weights, moms_m, moms_v):
    shards = {n: weights[n] for n in SHARDED}
    small = {n: weights[n] for n in SMALL_NAMES}
    me = 4 * lax.axis_index('x') + 2 * lax.axis_index('y') + lax.axis_index('c')
    cos, sin = _rope_tables(positions[0])

    own1 = _pack_shards(shards, 1, BF16, True)
    blocks0 = all_gather_blocks(_pack_shards(shards, 0, BF16, True))
    own1, blocks0 = lax.optimization_barrier((own1, blocks0))
    gather1, token = send_start(own1, False, 'gather_l1_start')
    gathered0, carriers0 = _make_carriers(_unpack_gathered(blocks0, 0))
    (x1, v_first), vjp0 = jax.vjp(
        lambda c, s, xx: _layer(0, xx, None, _layer_weights(gathered0, c), s, cos, sin),
        carriers0, _layer_small(small, 0), x[0] + token)

    blocks1 = send_wait(gather1, x1, False, 'gather_l1_wait')
    blocks1 = lax.dynamic_update_slice(blocks1, own1[None], (me, 0, 0))
    gathered1, carriers1 = _make_carriers(_unpack_gathered(blocks1, 1))
    y, vjp1 = jax.vjp(
        lambda c, s, xx, vf: _layer(1, xx, vf, _layer_weights(gathered1, c), s, cos, sin)[0],
        carriers1, _layer_small(small, 1), x1, v_first)

    dy, parts = _loss_call(y, loss_target[0])
    loss = lax.psum(jnp.sum(parts), ('x', 'y', 'c'))

    gw1, gs1, dx1, dvf = vjp1(dy)
    packed1 = _pack_grads(gw1, 1).astype(BF16)
    exchange1, token = send_start(packed1, True, 'grads_l1_start')
    gw0, gs0, g_x = vjp0((dx1 + token, dvf))
    parts1 = send_wait(exchange1, g_x, True, 'grads_l1_wait')
    parts1 = lax.dynamic_update_slice(parts1, lax.dynamic_slice_in_dim(packed1, me, 1, axis=0), (me, 0, 0))
    parts0 = exchange_blocks(_pack_grads(gw0, 0).astype(BF16))

    grads = {**_unpack_shards(sum_parts(parts0, 'sum_grads_l0'), 0), **_unpack_shards(sum_parts(parts1, 'sum_grads_l1'), 1)}
    sharded_out = [{}, {}, {}, {}]
    for n in SHARDED:
        outs = adamw_weight(n, weights[n], moms_m[n], moms_v[n], [grads[(n, l)] for l in range(SHARDED[n][0])])
        for i in range(4):
            sharded_out[i][n] = outs[i]

    g_small = {}
    for n in SMALL_NAMES:
        per = [g[n] for g in (gs0, gs1) if n in g]
        g_small[n] = jnp.stack(per)
    small_out = [_small_unpack(b, small) for b in allreduce_adamw_small(
        _small_pack(g_small), _small_pack(small), _small_pack({n: moms_m[n] for n in SMALL_NAMES}),
        _small_pack({n: moms_v[n] for n in SMALL_NAMES}))]

    pick = lambda i: [sharded_out[i][n] if n in SHARDED else small_out[i][n] for n in WEIGHT_NAMES]
    return (loss, g_x[None], *pick(0), *pick(1), *pick(2), *pick(3))


def kernel(x, positions, attn_norm, w_in, mla_q_a_norm, mla_wq_b, mla_kv_a_norm, mla_wkv_b, mla_q_norm, mla_k_norm, mla_w_o, rwkv_mu, rwkv_w0, rwkv_w2, rwkv_a0, rwkv_a2, rwkv_g2, rwkv_k_k, rwkv_k_a, rwkv_r_k, rwkv_ln_w, rwkv_ln_b, rwkv_w_o, rwkv_v1, rwkv_v_mu, rwkv_v0, rwkv_v2, conv_w, conv_w_o, w_out, mlp_norm, w_up, w_down, loss_target, m_attn_norm, m_w_in, m_mla_q_a_norm, m_mla_wq_b, m_mla_kv_a_norm, m_mla_wkv_b, m_mla_q_norm, m_mla_k_norm, m_mla_w_o, m_rwkv_mu, m_rwkv_w0, m_rwkv_w2, m_rwkv_a0, m_rwkv_a2, m_rwkv_g2, m_rwkv_k_k, m_rwkv_k_a, m_rwkv_r_k, m_rwkv_ln_w, m_rwkv_ln_b, m_rwkv_w_o, m_rwkv_v1, m_rwkv_v_mu, m_rwkv_v0, m_rwkv_v2, m_conv_w, m_conv_w_o, m_w_out, m_mlp_norm, m_w_up, m_w_down, v_attn_norm, v_w_in, v_mla_q_a_norm, v_mla_wq_b, v_mla_kv_a_norm, v_mla_wkv_b, v_mla_q_norm, v_mla_k_norm, v_mla_w_o, v_rwkv_mu, v_rwkv_w0, v_rwkv_w2, v_rwkv_a0, v_rwkv_a2, v_rwkv_g2, v_rwkv_k_k, v_rwkv_k_a, v_rwkv_r_k, v_rwkv_ln_w, v_rwkv_ln_b, v_rwkv_w_o, v_rwkv_v1, v_rwkv_v_mu, v_rwkv_v0, v_rwkv_v2, v_conv_w, v_conv_w_o, v_w_out, v_mlp_norm, v_w_up, v_w_down):
    args = locals()
    weights = {n: args[n] for n in WEIGHT_NAMES}
    moms_m = {n: args['m_' + n] for n in WEIGHT_NAMES}
    moms_v = {n: args['v_' + n] for n in WEIGHT_NAMES}
    return _train_step(x, positions, loss_target, weights, moms_m, moms_v)
```

```python
import functools

import numpy as np
import jax
import jax.numpy as jnp
from jax import lax
from jax.experimental import pallas as pl
from jax.experimental.pallas import tpu as pltpu

F32 = jnp.float32
BF16 = jnp.bfloat16

N_DEV = 8
LANES = 128
D_MODEL = 1024
DEPTH = 2
MLA_HEADS = 8
QK_NOPE = 64
QK_ROPE = 32
QK_HEAD = QK_NOPE + QK_ROPE
V_HEAD = 64
Q_LORA = 384
KV_LORA = 256
ROPE_THETA = 10000.0
RW_HEADS = 4
RW_N = 64
RW_WIDTH = RW_HEADS * RW_N
MV_LORA = 32
GN_EPS = 64e-5
CONV_WIDTH = 256
D_FF = 4 * D_MODEL
NORM_EPS = 1e-6
ADAM_LR = 0.001
ADAM_B1 = 0.9
ADAM_B2 = 0.999
ADAM_EPS = 1e-08
ADAM_WD = 0.01
ADAM_STEP = 10

VMEM_LIMIT = 56 * 1024 * 1024
MESH = pl.DeviceIdType.MESH

WEIGHT_NAMES = ['attn_norm', 'w_in', 'mla_q_a_norm', 'mla_wq_b', 'mla_kv_a_norm', 'mla_wkv_b', 'mla_q_norm',
                'mla_k_norm', 'mla_w_o', 'rwkv_mu', 'rwkv_w0', 'rwkv_w2', 'rwkv_a0', 'rwkv_a2', 'rwkv_g2',
                'rwkv_k_k', 'rwkv_k_a', 'rwkv_r_k', 'rwkv_ln_w', 'rwkv_ln_b', 'rwkv_w_o', 'rwkv_v1',
                'rwkv_v_mu', 'rwkv_v0', 'rwkv_v2', 'conv_w', 'conv_w_o', 'w_out', 'mlp_norm', 'w_up', 'w_down']

SHARDED = {
    'w_in': (2, (1024, 5536), 1), 'mla_wq_b': (2, (384, 768), 1), 'mla_wkv_b': (2, (256, 1024), 1),
    'mla_w_o': (2, (512, 1024), 1), 'rwkv_w2': (2, (64, 256), 1), 'rwkv_a2': (2, (64, 256), 1),
    'rwkv_g2': (2, (128, 256), 1), 'rwkv_w_o': (2, (256, 1024), 1), 'conv_w': (2, (3, 256), 1),
    'conv_w_o': (2, (256, 1024), 1), 'w_out': (2, (1024, 1024), 0), 'w_up': (2, (1024, 4096), 1),
    'w_down': (2, (4096, 1024), 0), 'rwkv_v1': (1, (1024, 32), 0), 'rwkv_v2': (1, (32, 256), 1),
}
SMALL_NAMES = [n for n in WEIGHT_NAMES if n not in SHARDED]
TRANSPOSED = ('w_in',)
WIN_SEGS = (('gates', 0, 3072), ('cq', 3072, 3456), ('ckv', 3456, 3712), ('kpe', 3712, 3744), ('rkv', 3744, 4512),
            ('xwa', 4512, 4640), ('xg', 4640, 4768), ('conv', 4768, 5536))
PACK_ROW_MULT = 512


def _cparams(sem=None, **kw):
    if sem is not None:
        kw['dimension_semantics'] = sem
    return pltpu.CompilerParams(vmem_limit_bytes=VMEM_LIMIT, **kw)


def _pick(n, cands):
    for c in cands:
        if n % c == 0:
            return c
    raise ValueError(f'no tile for {n}')


def _mm_nn(a, b, add=None, name='mm_nn'):
    M, K = a.shape
    N = b.shape[1]
    tm = _pick(M, (1024, 512, 256, 128))
    tn = _pick(N, (512, 384, 256, 128))
    tk = _pick(K, (1024, 512, 384, 256, 128))
    nk = K // tk
    has_add = add is not None

    def body(*refs):
        if has_add:
            a_ref, b_ref, add_ref, o_ref, acc_ref = refs
        else:
            a_ref, b_ref, o_ref, acc_ref = refs
        kk = pl.program_id(2)
        part = jnp.dot(a_ref[...].astype(BF16), b_ref[...].astype(BF16), preferred_element_type=F32)

        @pl.when(kk == 0)
        def _():
            acc_ref[...] = part

        @pl.when(kk > 0)
        def _():
            acc_ref[...] += part

        @pl.when(kk == nk - 1)
        def _():
            if has_add:
                o_ref[...] = acc_ref[...] + add_ref[...]
            else:
                o_ref[...] = acc_ref[...]

    in_specs = [pl.BlockSpec((tm, tk), lambda i, j, k: (i, k)), pl.BlockSpec((tk, tn), lambda i, j, k: (k, j))]
    args = [a, b]
    if has_add:
        in_specs.append(pl.BlockSpec((tm, tn), lambda i, j, k: (i, j)))
        args.append(add)
    return pl.pallas_call(
        body, name=name, grid=(M // tm, N // tn, nk), in_specs=in_specs,
        out_specs=pl.BlockSpec((tm, tn), lambda i, j, k: (i, j)),
        out_shape=jax.ShapeDtypeStruct((M, N), F32),
        scratch_shapes=[pltpu.VMEM((tm, tn), F32)],
        compiler_params=_cparams(('parallel', 'parallel', 'arbitrary')),
    )(*args)


def _mm_nt(a, b, add=None, name='mm_nt'):
    M, N = a.shape
    K = b.shape[0]
    tm = _pick(M, (1024, 512, 256, 128))
    tk = _pick(K, (512, 384, 256, 128))
    tn = _pick(N, (1024, 512, 384, 256, 128))
    nn = N // tn
    has_add = add is not None

    def body(*refs):
        if has_add:
            a_ref, b_ref, add_ref, o_ref, acc_ref = refs
        else:
            a_ref, b_ref, o_ref, acc_ref = refs
        kk = pl.program_id(2)
        part = lax.dot_general(a_ref[...].astype(BF16), b_ref[...].astype(BF16), (((1,), (1,)), ((), ())),
                               preferred_element_type=F32)

        @pl.when(kk == 0)
        def _():
            acc_ref[...] = part

        @pl.when(kk > 0)
        def _():
            acc_ref[...] += part

        @pl.when(kk == nn - 1)
        def _():
            if has_add:
                o_ref[...] = acc_ref[...] + add_ref[...]
            else:
                o_ref[...] = acc_ref[...]

    in_specs = [pl.BlockSpec((tm, tn), lambda i, j, k: (i, k)), pl.BlockSpec((tk, tn), lambda i, j, k: (j, k))]
    args = [a, b]
    if has_add:
        in_specs.append(pl.BlockSpec((tm, tk), lambda i, j, k: (i, j)))
        args.append(add)
    return pl.pallas_call(
        body, name=name, grid=(M // tm, K // tk, nn), in_specs=in_specs,
        out_specs=pl.BlockSpec((tm, tk), lambda i, j, k: (i, j)),
        out_shape=jax.ShapeDtypeStruct((M, K), F32),
        scratch_shapes=[pltpu.VMEM((tm, tk), F32)],
        compiler_params=_cparams(('parallel', 'parallel', 'arbitrary')),
    )(*args)


def _mm_tn(a, b, name='mm_tn'):
    M, K = a.shape
    N = b.shape[1]
    tm = _pick(M, (1024, 512, 256, 128))
    tk = _pick(K, (512, 384, 256, 128))
    tn = _pick(N, (512, 384, 256, 128))
    nm = M // tm

    def body(a_ref, b_ref, o_ref, acc_ref):
        mm = pl.program_id(2)
        part = lax.dot_general(a_ref[...].astype(BF16), b_ref[...].astype(BF16), (((0,), (0,)), ((), ())),
                               preferred_element_type=F32)

        @pl.when(mm == 0)
        def _():
            acc_ref[...] = part

        @pl.when(mm > 0)
        def _():
            acc_ref[...] += part

        @pl.when(mm == nm - 1)
        def _():
            o_ref[...] = acc_ref[...].astype(BF16)

    return pl.pallas_call(
        body, name=name, grid=(K // tk, N // tn, nm),
        in_specs=[pl.BlockSpec((tm, tk), lambda i, j, m: (m, i)), pl.BlockSpec((tm, tn), lambda i, j, m: (m, j))],
        out_specs=pl.BlockSpec((tk, tn), lambda i, j, m: (i, j)),
        out_shape=jax.ShapeDtypeStruct((K, N), BF16),
        scratch_shapes=[pltpu.VMEM((tk, tn), F32)],
        compiler_params=_cparams(('parallel', 'parallel', 'arbitrary')),
    )(a, b)


@functools.partial(jax.custom_vjp, nondiff_argnums=(4,))
def _linear_add(a, wb, wc, add, name):
    return _mm_nn(a, wb, add, name=name + '_f')


def _linear_add_fwd(a, wb, wc, add, name):
    return _mm_nn(a, wb, add, name=name + '_f'), (a, wb)


def _linear_add_bwd(name, res, dy):
    a, wb = res
    return _mm_nt(dy, wb, name=name + '_da'), None, _mm_tn(a, dy, name=name + '_dw'), dy


_linear_add.defvjp(_linear_add_fwd, _linear_add_bwd)


@functools.partial(jax.custom_vjp, nondiff_argnums=(3,))
def _multi_linear(a, wbs, wcs, name):
    return tuple(_mm_nn(a, wb, name=f'{name}_f{i}') for i, wb in enumerate(wbs))


def _multi_linear_fwd(a, wbs, wcs, name):
    return _multi_linear(a, wbs, wcs, name), (a, wbs)


def _multi_linear_bwd(name, res, dys):
    a, wbs = res
    da = None
    for i, (dy, wb) in enumerate(zip(dys, wbs)):
        da = _mm_nt(dy, wb, add=da, name=f'{name}_da{i}')
    dws = tuple(_mm_tn(a, dy, name=f'{name}_dw{i}') for i, dy in enumerate(dys))
    return da, None, dws


_multi_linear.defvjp(_multi_linear_fwd, _multi_linear_bwd)


class W:
    def __init__(self, b, c):
        self.b, self.c = b, c

    def map(self, fn):
        return W(fn(self.b), fn(self.c))


def _wcat(ws, axis):
    return W(jnp.concatenate([w.b for w in ws], axis), jnp.concatenate([w.c for w in ws], axis))


def linear(a, w, add=None, name='lin'):
    if add is None:
        return _multi_linear(a, (w.b,), (w.c,), name)[0]
    return _linear_add(a, w.b, w.c, add, name)


def multi_linear(a, ws, name):
    return _multi_linear(a, tuple(w.b for w in ws), tuple(w.c for w in ws), name)


@functools.partial(jax.custom_vjp, nondiff_argnums=(3,))
def _multi_linear_t(a, wbs, wcs, name):
    return tuple(_mm_nt(a, wb, name=f'{name}_f{i}') for i, wb in enumerate(wbs))


def _multi_linear_t_fwd(a, wbs, wcs, name):
    return _multi_linear_t(a, wbs, wcs, name), (a, wbs)


def _multi_linear_t_bwd(name, res, dys):
    a, wbs = res
    da = None
    for i, (dy, wb) in enumerate(zip(dys, wbs)):
        da = _mm_nn(dy, wb, add=da, name=f'{name}_da{i}')
    dws = tuple(_mm_tn(dy, a, name=f'{name}_dw{i}') for i, dy in enumerate(dys))
    return da, None, dws


_multi_linear_t.defvjp(_multi_linear_t_fwd, _multi_linear_t_bwd)


def multi_linear_t(a, ws, name):
    return _multi_linear_t(a, tuple(w.b for w in ws), tuple(w.c for w in ws), name)


def ROW(diff=True, pieces=None):
    return ('row', diff, pieces)


def FULL(diff=True):
    return ('full', diff, None)


def _load_args(refs, specs):
    args, amap = [], []
    for i, (ref, (kind, diff, pieces)) in enumerate(zip(refs, specs)):
        if pieces is None:
            args.append(ref[...])
            amap.append((i, None))
        else:
            for (s, w) in pieces:
                args.append(ref[:, s:s + w])
                amap.append((i, (s, w)))
    return args, amap


def _stage_in_specs(ins, specs, tb):
    out = []
    for a, (kind, _, _) in zip(ins, specs):
        if kind == 'row':
            out.append(pl.BlockSpec((tb, a.shape[1]), lambda i: (i, 0)))
        else:
            out.append(pl.BlockSpec(a.shape, lambda i: (0, 0)))
    return out


def _stage_fwd(fn, ins, specs, out_widths, name, tb, out_dtype=F32):
    T = [a for a, s in zip(ins, specs) if s[0] == 'row'][0].shape[0]
    tb = min(tb, T)
    n_in = len(ins)

    def body(*refs):
        args, _ = _load_args(refs[:n_in], specs)
        outs = fn(*args)
        for o_ref, o in zip(refs[n_in:], outs):
            o_ref[...] = o.astype(out_dtype)

    return pl.pallas_call(
        body, name=name + '_f', grid=(T // tb,), in_specs=_stage_in_specs(ins, specs, tb),
        out_specs=[pl.BlockSpec((tb, w), lambda i: (i, 0)) for w in out_widths],
        out_shape=[jax.ShapeDtypeStruct((T, w), out_dtype) for w in out_widths],
        compiler_params=_cparams(('parallel',)),
    )(*ins)


def _stage_bwd(fn, ins, specs, out_widths, douts, name, tb):
    T = [a for a, s in zip(ins, specs) if s[0] == 'row'][0].shape[0]
    tb = min(tb, T)
    n_in, n_out = len(ins), len(out_widths)
    diff_inputs = [i for i, s in enumerate(specs) if s[1]]

    def body(*refs):
        in_refs, dout_refs, g_refs = refs[:n_in], refs[n_in:n_in + n_out], refs[n_in + n_out:]
        args, amap = _load_args(in_refs, specs)
        didx = [j for j, (i, _) in enumerate(amap) if specs[i][1]]

        def f(*dv):
            full = list(args)
            for j, v in zip(didx, dv):
                full[j] = v
            return tuple(fn(*full))

        _, vjp = jax.vjp(f, *[args[j] for j in didx])
        gs = vjp(tuple(d[...] for d in dout_refs))
        gmap = {j: g for j, g in zip(didx, gs)}
        first = pl.program_id(0) == 0
        for g_ref, i in zip(g_refs, diff_inputs):
            kind, _, pieces = specs[i]
            js = [j for j, (ii, _) in enumerate(amap) if ii == i]
            if kind == 'row':
                if pieces is None:
                    g_ref[...] = gmap[js[0]]
                else:
                    if sum(w for _, w in pieces) != ins[i].shape[1]:
                        g_ref[...] = jnp.zeros(g_ref.shape, F32)
                    for j in js:
                        s, w = amap[j][1]
                        g_ref[:, s:s + w] = gmap[j]
            else:
                @pl.when(first)
                def _(g_ref=g_ref):
                    g_ref[...] = jnp.zeros(g_ref.shape, F32)

                g_ref[...] += gmap[js[0]]

    in_specs = _stage_in_specs(ins, specs, tb) + [pl.BlockSpec((tb, w), lambda i: (i, 0)) for w in out_widths]
    out_specs, out_shape = [], []
    for i in diff_inputs:
        a = ins[i]
        if specs[i][0] == 'row':
            out_specs.append(pl.BlockSpec((tb, a.shape[1]), lambda i: (i, 0)))
        else:
            out_specs.append(pl.BlockSpec(a.shape, lambda i: (0, 0)))
        out_shape.append(jax.ShapeDtypeStruct(a.shape, F32))
    return pl.pallas_call(
        body, name=name + '_b', grid=(T // tb,), in_specs=in_specs, out_specs=out_specs, out_shape=out_shape,
        compiler_params=_cparams(('arbitrary',)),
    )(*ins, *douts)


def stage_op(fn, specs, out_widths, name, tb=256):
    n = len(specs)
    diff_inputs = [i for i, s in enumerate(specs) if s[1]]

    @jax.custom_vjp
    def op(*ins):
        return tuple(_stage_fwd(fn, ins, specs, out_widths, name, tb))

    def op_fwd(*ins):
        return op(*ins), ins

    def op_bwd(ins, douts):
        gs = _stage_bwd(fn, ins, specs, out_widths, douts, name, tb)
        res = [None] * n
        for i, g in zip(diff_inputs, gs):
            res[i] = g
        return tuple(res)

    op.defvjp(op_fwd, op_bwd)
    return op


@jax.custom_vjp
def bdot(x, w):
    return jnp.dot(x.astype(BF16), w.astype(BF16), preferred_element_type=F32)


def _bdot_fwd(x, w):
    return bdot(x, w), (x, w)


def _bdot_bwd(res, dy):
    x, w = res
    dyb = dy.astype(BF16)
    dx = lax.dot_general(dyb, w.astype(BF16), (((1,), (1,)), ((), ())), preferred_element_type=F32)
    dw = lax.dot_general(x.astype(BF16), dyb, (((0,), (0,)), ((), ())), preferred_element_type=F32)
    return dx, dw


bdot.defvjp(_bdot_fwd, _bdot_bwd)


def _sdot_raw(x, c):
    hi = x.astype(BF16)
    r1 = x - hi.astype(F32)
    mid = r1.astype(BF16)
    lo = (r1 - mid.astype(F32)).astype(BF16)
    d = lambda u: jnp.dot(u, c, preferred_element_type=F32)
    return d(hi) + d(mid) + d(lo)


@jax.custom_vjp
def sdot(x, c, ct):
    return _sdot_raw(x, c)


def _sdot_fwd(x, c, ct):
    return _sdot_raw(x, c), (c, ct)


def _sdot_bwd(res, dy):
    c, ct = res
    return _sdot_raw(dy, ct), None, None


sdot.defvjp(_sdot_fwd, _sdot_bwd)


def _sigmoid(x):
    return 1.0 / (1.0 + jnp.exp(-x))


def _rms(x, g):
    return x * lax.rsqrt(jnp.mean(x * x, axis=-1, keepdims=True) + NORM_EPS) * g


def rmsnorm(x, g, name):
    op = stage_op(lambda xv, gv: (_rms(xv, gv),), [ROW(), FULL()], [x.shape[1]], name)
    return op(x, g.reshape(1, -1))[0]


def _mm_up_relu2(a, b, name):
    M, K = a.shape
    N = b.shape[1]
    tm = _pick(M, (1024, 512, 256, 128))
    tn = _pick(N, (512, 256, 128))

    def body(a_ref, b_ref, u_ref, act_ref):
        u = jnp.dot(a_ref[...], b_ref[...], preferred_element_type=F32)
        r = jnp.maximum(u, 0.0)
        u_ref[...] = u.astype(BF16)
        act_ref[...] = (r * r).astype(BF16)

    out = pl.BlockSpec((tm, tn), lambda i, j: (i, j))
    sh = jax.ShapeDtypeStruct((M, N), BF16)
    return pl.pallas_call(
        body, name=name, grid=(M // tm, N // tn),
        in_specs=[pl.BlockSpec((tm, K), lambda i, j: (i, 0)), pl.BlockSpec((K, tn), lambda i, j: (0, j))],
        out_specs=[out, out], out_shape=[sh, sh], compiler_params=_cparams(('parallel', 'parallel')),
    )(a, b)


def _mm_down_bwd(dy, b, u, name):
    M, N = dy.shape
    K = b.shape[0]
    tm = _pick(M, (1024, 512, 256, 128))
    tk = _pick(K, (512, 256, 128))

    def body(dy_ref, b_ref, u_ref, du_ref):
        d = lax.dot_general(dy_ref[...].astype(BF16), b_ref[...], (((1,), (1,)), ((), ())),
                            preferred_element_type=F32)
        du_ref[...] = (d * (2.0 * jnp.maximum(u_ref[...].astype(F32), 0.0))).astype(BF16)

    blk = pl.BlockSpec((tm, tk), lambda i, j: (i, j))
    return pl.pallas_call(
        body, name=name, grid=(M // tm, K // tk),
        in_specs=[pl.BlockSpec((tm, N), lambda i, j: (i, 0)), pl.BlockSpec((tk, N), lambda i, j: (j, 0)), blk],
        out_specs=blk, out_shape=jax.ShapeDtypeStruct((M, K), BF16),
        compiler_params=_cparams(('parallel', 'parallel')),
    )(dy, b, u)


_RMS_SPECS = [ROW(), FULL()]
_rms_fn = lambda xv, gv: (_rms(xv, gv),)


@functools.partial(jax.custom_vjp, nondiff_argnums=(6,))
def mlp_block(x, g, wup_b, wup_c, wdown_b, wdown_c, name):
    return _mlp_fwd(x, g, wup_b, wup_c, wdown_b, wdown_c, name)[0]


def _mlp_fwd(x, g, wup_b, wup_c, wdown_b, wdown_c, name):
    h = _stage_fwd(_rms_fn, (x, g), _RMS_SPECS, [x.shape[1]], name + '_norm', 256, out_dtype=BF16)[0]
    u, act = _mm_up_relu2(h, wup_b, name + '_up')
    y = _mm_nn(act, wdown_b, add=x, name=name + '_down')
    return y, (x, g, h, u, act, wup_b, wdown_b)


def _mlp_bwd(name, res, dy):
    x, g, h, u, act, wup_b, wdown_b = res
    du = _mm_down_bwd(dy, wdown_b, u, name + '_down_da')
    dwdown = _mm_tn(act, dy, name=name + '_down_dw')
    dh = _mm_nt(du, wup_b, name=name + '_up_da')
    dwup = _mm_tn(h, du, name=name + '_up_dw')
    dx, dg = _stage_bwd(_rms_fn, (x, g), _RMS_SPECS, [x.shape[1]], (dh,), name + '_norm', 256)
    return dx + dy, dg, None, dwup, None, dwdown


mlp_block.defvjp(_mlp_fwd, _mlp_bwd)


def _shift_down(x, rows):
    return jnp.where(rows == 0, 0.0, pltpu.roll(x, 1, 0))


def _shift_up(x, rows, T):
    return jnp.where(rows == T - 1, 0.0, pltpu.roll(x, T - 1, 0))


def _tshift_fwd_call(x, mu, name):
    T, C = x.shape

    def body(x_ref, mu_ref, o_ref):
        xv = x_ref[...]
        rows = lax.broadcasted_iota(jnp.int32, xv.shape, 0)
        o_ref[...] = xv + (_shift_down(xv, rows) - xv) * mu_ref[...]

    return pl.pallas_call(
        body, name=name + '_f', grid=(C // LANES,),
        in_specs=[pl.BlockSpec((T, LANES), lambda j: (0, j)), pl.BlockSpec((1, LANES), lambda j: (0, j))],
        out_specs=pl.BlockSpec((T, LANES), lambda j: (0, j)), out_shape=jax.ShapeDtypeStruct((T, C), F32),
        compiler_params=_cparams(('parallel',)),
    )(x, mu)


def _tshift_bwd_call(x, mu, dy, name):
    T, C = x.shape

    def body(x_ref, mu_ref, dy_ref, dx_ref, dmu_ref):
        xv, d = x_ref[...], dy_ref[...]
        rows = lax.broadcasted_iota(jnp.int32, xv.shape, 0)
        z = d * mu_ref[...]
        dx_ref[...] = d - z + _shift_up(z, rows, T)
        dmu_ref[...] = jnp.sum(d * (_shift_down(xv, rows) - xv), axis=0, keepdims=True)

    return pl.pallas_call(
        body, name=name + '_b', grid=(C // LANES,),
        in_specs=[pl.BlockSpec((T, LANES), lambda j: (0, j)), pl.BlockSpec((1, LANES), lambda j: (0, j)),
                  pl.BlockSpec((T, LANES), lambda j: (0, j))],
        out_specs=[pl.BlockSpec((T, LANES), lambda j: (0, j)), pl.BlockSpec((1, LANES), lambda j: (0, j))],
        out_shape=[jax.ShapeDtypeStruct((T, C), F32), jax.ShapeDtypeStruct((1, C), F32)],
        compiler_params=_cparams(('parallel',)),
    )(x, mu, dy)


@functools.partial(jax.custom_vjp, nondiff_argnums=(2,))
def token_shift_mix(x, mu, name):
    return _tshift_fwd_call(x, mu, name)


def _tsm_fwd(x, mu, name):
    return _tshift_fwd_call(x, mu, name), (x, mu)


def _tsm_bwd(name, res, dy):
    x, mu = res
    dx, dmu = _tshift_bwd_call(x, mu, dy, name)
    return dx, dmu


token_shift_mix.defvjp(_tsm_fwd, _tsm_bwd)


def _conv_specs(T):
    nb = CONV_WIDTH // LANES
    return [pl.BlockSpec((T, LANES), lambda j: (0, j)), pl.BlockSpec((T, LANES), lambda j: (0, nb + j)),
            pl.BlockSpec((T, LANES), lambda j: (0, 2 * nb + j)), pl.BlockSpec((3, LANES), lambda j: (0, j))]


def _conv_fwd_call(cv, w, name):
    T = cv.shape[0]

    def body(b_ref, c_ref, x_ref, w_ref, o_ref):
        u = c_ref[...] * x_ref[...]
        rows = lax.broadcasted_iota(jnp.int32, u.shape, 0)
        u1 = _shift_down(u, rows)
        u2 = _shift_down(u1, rows)
        o_ref[...] = b_ref[...] * (w_ref[0:1, :] * u2 + w_ref[1:2, :] * u1 + w_ref[2:3, :] * u)

    return pl.pallas_call(
        body, name=name + '_f', grid=(CONV_WIDTH // LANES,), in_specs=_conv_specs(T),
        out_specs=pl.BlockSpec((T, LANES), lambda j: (0, j)),
        out_shape=jax.ShapeDtypeStruct((T, CONV_WIDTH), F32), compiler_params=_cparams(('parallel',)),
    )(cv, cv, cv, w)


def _conv_bwd_call(cv, w, do, name):
    T = cv.shape[0]

    def body(b_ref, c_ref, x_ref, w_ref, do_ref, db_ref, dc_ref, dx_ref, dw_ref):
        c, x, d = c_ref[...], x_ref[...], do_ref[...]
        u = c * x
        rows = lax.broadcasted_iota(jnp.int32, u.shape, 0)
        u1 = _shift_down(u, rows)
        u2 = _shift_down(u1, rows)
        w0, w1, w2 = w_ref[0:1, :], w_ref[1:2, :], w_ref[2:3, :]
        db_ref[...] = d * (w0 * u2 + w1 * u1 + w2 * u)
        dy = d * b_ref[...]
        dy1 = _shift_up(dy, rows, T)
        dy2 = _shift_up(dy1, rows, T)
        du = w2 * dy + w1 * dy1 + w0 * dy2
        dc_ref[...] = du * x
        dx_ref[...] = du * c
        dw_ref[0:1, :] = jnp.sum(dy * u2, axis=0, keepdims=True)
        dw_ref[1:2, :] = jnp.sum(dy * u1, axis=0, keepdims=True)
        dw_ref[2:3, :] = jnp.sum(dy * u, axis=0, keepdims=True)

    blk = pl.BlockSpec((T, LANES), lambda j: (0, j))
    sh = jax.ShapeDtypeStruct((T, CONV_WIDTH), F32)
    return pl.pallas_call(
        body, name=name + '_b', grid=(CONV_WIDTH // LANES,), in_specs=_conv_specs(T) + [blk],
        out_specs=[blk, blk, blk, pl.BlockSpec((3, LANES), lambda j: (0, j))],
        out_shape=[sh, sh, sh, jax.ShapeDtypeStruct((3, CONV_WIDTH), F32)],
        compiler_params=_cparams(('parallel',)),
    )(cv, cv, cv, w, do)


@functools.partial(jax.custom_vjp, nondiff_argnums=(2,))
def short_conv(cv, w, name):
    return _conv_fwd_call(cv, w, name)


def _sc_fwd(cv, w, name):
    return _conv_fwd_call(cv, w, name), (cv, w)


def _sc_bwd(name, res, do):
    cv, w = res
    db, dc, dx, dw = _conv_bwd_call(cv, w, do, name)
    return jnp.concatenate([db, dc, dx], axis=1), dw


short_conv.defvjp(_sc_fwd, _sc_bwd)


ATT_SCALE = QK_HEAD ** -0.5
NPAIR = MLA_HEADS // 2


def _att_bq(T):
    return min(256, T)


def _att_masks(pair, j):
    lane = lax.broadcasted_iota(jnp.int32, (1, LANES), 1)
    mask_n = (lane // QK_NOPE) == j
    mask_r = (lane // (QK_ROPE // 2)) == (2 * pair + j)
    return mask_n, mask_r


def _att_probs(qcat, kcat, row0, stop):
    s = lax.dot_general(qcat, kcat, (((1,), (1,)), ((), ())), preferred_element_type=F32) * ATT_SCALE
    r = row0 + lax.broadcasted_iota(jnp.int32, s.shape, 0)
    c = lax.broadcasted_iota(jnp.int32, s.shape, 1)
    s = jnp.where(c <= r, s, -jnp.inf)
    e = jnp.exp(s - jnp.max(s, axis=-1, keepdims=True))
    return e / jnp.sum(e, axis=-1, keepdims=True)


def _att_in_specs(T):
    blk = lambda f: pl.BlockSpec((T, LANES), f)
    return [blk(lambda p: (0, p)), blk(lambda p: (0, 0)), blk(lambda p: (0, 0)),
            blk(lambda p: (0, p)), blk(lambda p: (0, 0)), blk(lambda p: (0, 0)), blk(lambda p: (0, p))]


def _att_fwd_call(qn, q1, q2, kn, k1, k2, v, name):
    T = qn.shape[0]
    bq = _att_bq(T)

    def body(qn_ref, q1_ref, q2_ref, kn_ref, k1_ref, k2_ref, v_ref, o_ref):
        pair = pl.program_id(0)
        for i in range(T // bq):
            r0, stop = i * bq, (i + 1) * bq
            kcat = jnp.concatenate([kn_ref[0:stop, :], k1_ref[0:stop, :], k2_ref[0:stop, :]], axis=1).astype(BF16)
            vb = v_ref[0:stop, :].astype(BF16)
            outs = []
            for j in range(2):
                mask_n, mask_r = _att_masks(pair, j)
                qcat = jnp.concatenate([jnp.where(mask_n, qn_ref[r0:stop, :], 0.0),
                                        jnp.where(mask_r, q1_ref[r0:stop, :], 0.0),
                                        jnp.where(mask_r, q2_ref[r0:stop, :], 0.0)], axis=1).astype(BF16)
                p = _att_probs(qcat, kcat, r0, stop)
                outs.append(jnp.dot(p.astype(BF16), vb, preferred_element_type=F32))
            mask_n0, _ = _att_masks(pair, 0)
            o_ref[r0:stop, :] = jnp.where(mask_n0, outs[0], outs[1])

    return pl.pallas_call(
        body, name=name + '_f', grid=(NPAIR,), in_specs=_att_in_specs(T),
        out_specs=pl.BlockSpec((T, LANES), lambda p: (0, p)),
        out_shape=jax.ShapeDtypeStruct((T, MLA_HEADS * V_HEAD), F32), compiler_params=_cparams(('parallel',)),
    )(qn, q1, q2, kn, k1, k2, v)


def _att_bwd_call(qn, q1, q2, kn, k1, k2, v, o, do, name):
    T = qn.shape[0]
    bq = _att_bq(T)

    def body(qn_ref, q1_ref, q2_ref, kn_ref, k1_ref, k2_ref, v_ref, o_ref, do_ref,
             dqn_ref, dq1_ref, dq2_ref, dkn_ref, dk1_ref, dk2_ref, dv_ref, dk_acc, dv_acc):
        pair = pl.program_id(0)

        @pl.when(pair == 0)
        def _():
            dq1_ref[...] = jnp.zeros(dq1_ref.shape, F32)
            dq2_ref[...] = jnp.zeros(dq2_ref.shape, F32)
            dk1_ref[...] = jnp.zeros(dk1_ref.shape, F32)
            dk2_ref[...] = jnp.zeros(dk2_ref.shape, F32)

        dk_acc[...] = jnp.zeros(dk_acc.shape, F32)
        dv_acc[...] = jnp.zeros(dv_acc.shape, F32)
        for i in range(T // bq):
            r0, stop = i * bq, (i + 1) * bq
            kcat = jnp.concatenate([kn_ref[0:stop, :], k1_ref[0:stop, :], k2_ref[0:stop, :]], axis=1).astype(BF16)
            vb = v_ref[0:stop, :].astype(BF16)
            dqn = jnp.zeros((bq, LANES), F32)
            for j in range(2):
                mask_n, mask_r = _att_masks(pair, j)
                qcat = jnp.concatenate([jnp.where(mask_n, qn_ref[r0:stop, :], 0.0),
                                        jnp.where(mask_r, q1_ref[r0:stop, :], 0.0),
                                        jnp.where(mask_r, q2_ref[r0:stop, :], 0.0)], axis=1).astype(BF16)
                p = _att_probs(qcat, kcat, r0, stop)
                dom = jnp.where(mask_n, do_ref[r0:stop, :], 0.0)
                delta = jnp.sum(dom * o_ref[r0:stop, :], axis=-1, keepdims=True)
                domb = dom.astype(BF16)
                dp = lax.dot_general(domb, vb, (((1,), (1,)), ((), ())), preferred_element_type=F32)
                ds = (p * (dp - delta) * ATT_SCALE).astype(BF16)
                dqc = jnp.dot(ds, kcat, preferred_element_type=F32)
                dqn = dqn + jnp.where(mask_n, dqc[:, 0:LANES], 0.0)
                dq1_ref[r0:stop, :] += jnp.where(mask_r, dqc[:, LANES:2 * LANES], 0.0)
                dq2_ref[r0:stop, :] += jnp.where(mask_r, dqc[:, 2 * LANES:3 * LANES], 0.0)
                dk_acc[0:stop, :] += lax.dot_general(ds, qcat, (((0,), (0,)), ((), ())),
                                                     preferred_element_type=F32)
                dv_acc[0:stop, :] += lax.dot_general(p.astype(BF16), domb, (((0,), (0,)), ((), ())),
                                                     preferred_element_type=F32)
            dqn_ref[r0:stop, :] = dqn
        dkn_ref[...] = dk_acc[:, 0:LANES]
        dk1_ref[...] += dk_acc[:, LANES:2 * LANES]
        dk2_ref[...] += dk_acc[:, 2 * LANES:3 * LANES]
        dv_ref[...] = dv_acc[...]

    per_pair = pl.BlockSpec((T, LANES), lambda p: (0, p))
    shared = pl.BlockSpec((T, LANES), lambda p: (0, 0))
    wide = jax.ShapeDtypeStruct((T, MLA_HEADS * QK_NOPE), F32)
    narrow = jax.ShapeDtypeStruct((T, LANES), F32)
    return pl.pallas_call(
        body, name=name + '_b', grid=(NPAIR,), in_specs=_att_in_specs(T) + [per_pair, per_pair],
        out_specs=[per_pair, shared, shared, per_pair, shared, shared, per_pair],
        out_shape=[wide, narrow, narrow, wide, narrow, narrow, wide],
        scratch_shapes=[pltpu.VMEM((T, 3 * LANES), F32), pltpu.VMEM((T, LANES), F32)],
        compiler_params=_cparams(('arbitrary',)),
    )(qn, q1, q2, kn, k1, k2, v, o, do)


@functools.partial(jax.custom_vjp, nondiff_argnums=(7,))
def attention(qn, q1, q2, kn, k1, k2, v, name):
    return _att_fwd_call(qn, q1, q2, kn, k1, k2, v, name)


def _attn_fwd(qn, q1, q2, kn, k1, k2, v, name):
    o = _att_fwd_call(qn, q1, q2, kn, k1, k2, v, name)
    return o, (qn, q1, q2, kn, k1, k2, v, o)


def _attn_bwd(name, res, do):
    return tuple(_att_bwd_call(*res, do, name))


attention.defvjp(_attn_fwd, _attn_bwd)


SCAN_CHUNK = 64
SCAN_UNROLL = 4


def _block_ones(n, seg):
    i = np.arange(n)
    return (i[:, None] // seg == i[None, :] // seg).astype(np.float32)


def _scan_diag():
    i = np.arange(RW_WIDTH)
    return jnp.asarray((np.arange(RW_N)[:, None] == (i[None, :] % RW_N)).astype(np.float32))


def _head_rowsum(x):
    low = lax.broadcasted_iota(jnp.int32, (1, LANES), 1) < RW_N
    tiles = []
    for j in range(RW_WIDTH // LANES):
        xt = x[:, j * LANES:(j + 1) * LANES]
        x0 = jnp.where(low, xt, 0.0)
        s0 = jnp.sum(x0, axis=-1, keepdims=True)
        s1 = jnp.sum(xt - x0, axis=-1, keepdims=True)
        tiles.append(jnp.where(low, s0, s1))
    return jnp.concatenate(tiles, axis=1)


def _unrolled_loop(n, step, init):
    def body(i, carry):
        for j in range(SCAN_UNROLL):
            carry = step(i * SCAN_UNROLL + j, carry)
        return carry
    return lax.fori_loop(0, n // SCAN_UNROLL, body, init)


def _scan_fwd_call(r, w, k, v, a, b, name):
    T = r.shape[0]
    tc = min(SCAN_CHUNK, T)
    dg = _scan_diag()

    def body(r_ref, w_ref, k_ref, v_ref, a_ref, b_ref, dg_ref, y_ref, st_ref, sa_ref, vc_ref, s_ref):
        @pl.when(pl.program_id(0) == 0)
        def _():
            s_ref[...] = jnp.zeros(s_ref.shape, F32)

        dgv = dg_ref[...]
        readout = lambda s, t: jnp.sum(_head_rowsum(s * r_ref[t]) * dgv, axis=0, keepdims=True)

        def step(t, carry):
            s, vcol = carry
            st_ref[t] = s
            vc_ref[t] = vcol
            sa = _head_rowsum(s * a_ref[t])
            sa_ref[t] = sa
            prev = jnp.maximum(t - 1, 0)
            y_ref[prev] = readout(s, prev)
            vcol_next = _head_rowsum(dgv * v_ref[jnp.minimum(t + 1, tc - 1)])
            sn = s * w_ref[t] + sa * b_ref[t] + vcol * k_ref[t]
            return sn, vcol_next

        s_end, _ = _unrolled_loop(tc, step, (s_ref[...], _head_rowsum(dgv * v_ref[0])))
        y_ref[tc - 1] = readout(s_end, tc - 1)
        s_ref[...] = s_end

    vec = pl.BlockSpec((tc, 1, RW_WIDTH), lambda i: (i, 0, 0))
    mat = pl.BlockSpec((tc, RW_N, RW_WIDTH), lambda i: (i, 0, 0))
    msh = jax.ShapeDtypeStruct((T, RW_N, RW_WIDTH), F32)
    return pl.pallas_call(
        body, name=name + '_f', grid=(T // tc,),
        in_specs=[vec] * 6 + [pl.BlockSpec((RW_N, RW_WIDTH), lambda i: (0, 0))],
        out_specs=[vec, mat, mat, mat],
        out_shape=[jax.ShapeDtypeStruct((T, 1, RW_WIDTH), F32), msh, msh, msh],
        scratch_shapes=[pltpu.VMEM((RW_N, RW_WIDTH), F32)],
        compiler_params=_cparams(('arbitrary',)),
    )(r, w, k, v, a, b, dg)


def _scan_bwd_call(r, w, k, a, b, st, sa_all, vc_all, dy, name):
    T = r.shape[0]
    tc = min(SCAN_CHUNK, T)
    nt = T // tc
    dg = _scan_diag()

    def body(r_ref, w_ref, k_ref, a_ref, b_ref, st_ref, sa_ref, vc_ref, dy_ref, dg_ref,
             dr_ref, dw_ref, dk_ref, dv_ref, da_ref, db_ref, ds_ref):
        @pl.when(pl.program_id(0) == 0)
        def _():
            ds_ref[...] = jnp.zeros(ds_ref.shape, F32)

        dgv = dg_ref[...]
        colsum = lambda x: jnp.sum(x, axis=0, keepdims=True)

        def step(i, carry):
            ds, dycol = carry
            t = tc - 1 - i
            sp = st_ref[t]
            rt, wt, kt, at, bt = r_ref[t], w_ref[t], k_ref[t], a_ref[t], b_ref[t]
            ds = ds + dycol * rt
            dsa = _head_rowsum(ds * bt)
            sa, vcol = sa_ref[t], vc_ref[t]
            dycol_next = _head_rowsum(dgv * dy_ref[jnp.maximum(t - 1, 0)])
            sn = sp * wt + sa * bt + vcol * kt
            dr_ref[t] = colsum(sn * dycol)
            dk_ref[t] = colsum(ds * vcol)
            db_ref[t] = colsum(ds * sa)
            dw_ref[t] = colsum(ds * sp)
            dv_ref[t] = colsum(_head_rowsum(ds * kt) * dgv)
            da_ref[t] = colsum(sp * dsa)
            return ds * wt + dsa * at, dycol_next

        ds_end, _ = _unrolled_loop(tc, step, (ds_ref[...], _head_rowsum(dgv * dy_ref[tc - 1])))
        ds_ref[...] = ds_end

    vec = pl.BlockSpec((tc, 1, RW_WIDTH), lambda i: (nt - 1 - i, 0, 0))
    mat = pl.BlockSpec((tc, RW_N, RW_WIDTH), lambda i: (nt - 1 - i, 0, 0))
    vsh = jax.ShapeDtypeStruct((T, 1, RW_WIDTH), F32)
    return pl.pallas_call(
        body, name=name + '_b', grid=(nt,),
        in_specs=[vec] * 5 + [mat] * 3 + [vec, pl.BlockSpec((RW_N, RW_WIDTH), lambda i: (0, 0))],
        out_specs=[vec] * 6, out_shape=[vsh] * 6,
        scratch_shapes=[pltpu.VMEM((RW_N, RW_WIDTH), F32)],
        compiler_params=_cparams(('arbitrary',)),
    )(r, w, k, a, b, st, sa_all, vc_all, dy, dg)


@functools.partial(jax.custom_vjp, nondiff_argnums=(6,))
def wkv7(r, w, k, v, a, b, name):
    return _scan_fwd_call(r, w, k, v, a, b, name)[0]


def _wkv7_fwd(r, w, k, v, a, b, name):
    y, st, sa_all, vc_all = _scan_fwd_call(r, w, k, v, a, b, name)
    return y, (r, w, k, a, b, st, sa_all, vc_all)


def _wkv7_bwd(name, res, dy):
    return tuple(_scan_bwd_call(*res, dy, name))


wkv7.defvjp(_wkv7_fwd, _wkv7_bwd)


def _np_bf16(a):
    return jnp.asarray(a, BF16)


def _mla_consts():
    seg_n = (np.arange(512)[:, None] // QK_NOPE == np.arange(LANES)[None, :]).astype(np.float32)
    seg_r = (np.arange(LANES)[:, None] // 16 == np.arange(LANES)[None, :]).astype(np.float32)
    e1 = np.zeros((LANES, LANES), np.float32)
    e2 = np.zeros((LANES, LANES), np.float32)
    for h in range(MLA_HEADS):
        for i in range(16):
            e1[i, h * 16 + i] = 1.0
            e2[16 + i, h * 16 + i] = 1.0
    mats = [seg_n, seg_n.T, seg_r, seg_r.T, e1, e1.T, e2, e2.T]
    return [_np_bf16(m) for m in mats]


def _qk_prep_fn(qn, q1, q2, kn, kx, cos, sin, gqn, gq1, gq2, gkn, gk1, gk2,
                seg_n, seg_nt, seg_r, seg_rt, e1, e1t, e2, e2t):
    def normrope(xn, x1, x2, gn, g1, g2):
        ss = sdot(xn * xn, seg_n, seg_nt) + sdot(x1 * x1, seg_r, seg_rt) + sdot(x2 * x2, seg_r, seg_rt)
        inv = lax.rsqrt(ss * (1.0 / QK_HEAD) + NORM_EPS)
        inv_n = sdot(inv, seg_nt, seg_n)
        inv_r = sdot(inv, seg_rt, seg_r)
        y1 = x1 * inv_r * g1
        y2 = x2 * inv_r * g2
        return xn * inv_n * gn, y1 * cos - y2 * sin, y1 * sin + y2 * cos

    k1 = sdot(kx, e1, e1t)
    k2 = sdot(kx, e2, e2t)
    return normrope(qn, q1, q2, gqn, gq1, gq2) + normrope(kn, k1, k2, gkn, gk1, gk2)


def _rwkv_prep_fn(vres):
    def fn(r, k, v, xg, xwa, kx, *rest):
        if vres:
            vfirst, w0, a0, k_k, k_a, w2p, a2p, g2, v0, v2p, bm = rest
        else:
            w0, a0, k_k, k_a, w2p, a2p, g2, bm = rest
        z = w0 + bdot(jnp.tanh(xwa), w2p)
        nz = -z
        softplus = jnp.maximum(nz, 0.0) + jnp.log(1.0 + jnp.exp(-jnp.abs(nz)))
        decay = jnp.exp(-jnp.exp(-softplus - 0.5))
        a = _sigmoid(a0 + bdot(xwa, a2p))
        g = bdot(_sigmoid(xg), g2)
        if vres:
            vv = v + (vfirst - v) * _sigmoid(v0 + bdot(kx, v2p))
        else:
            vv = v
        kkr = k * k_k
        kk = kkr / jnp.maximum(jnp.sqrt(sdot(kkr * kkr, bm, bm)), 1e-12)
        k2 = k * (1.0 + (a - 1.0) * k_a)
        return r * 1.0, decay, k2, vv, -kk, kk * a, g
    return fn


def _rwkv_post_fn(y, r, k2, vv, g, ln_w, ln_b, rk, bm):
    inv_n = 1.0 / RW_N
    mean = sdot(y, bm, bm) * inv_n
    yc = y - mean
    var = sdot(yc * yc, bm, bm) * inv_n
    yn = yc * lax.rsqrt(var + GN_EPS) * ln_w + ln_b
    bonus = sdot(r * k2 * rk, bm, bm) * vv
    return ((yn + bonus) * g,)


def _merge_fn(g0, g1, g2, oa, ob, oc):
    return (_sigmoid(g0) * oa + _sigmoid(g1) * ob + _sigmoid(g2) * oc,)


def _relu2_fn(u):
    r = jnp.maximum(u, 0.0)
    return (r * r,)


def _loss_call(y, target):
    T, C = y.shape
    tb = min(256, T)

    def body(y_ref, t_ref, dy_ref, part_ref):
        err = y_ref[...] - t_ref[...]
        dy_ref[...] = err * (1.0 / C)
        sq = jnp.sum(err * err, axis=0, keepdims=True)
        acc = sq[:, 0:LANES]
        for j in range(1, C // LANES):
            acc = acc + sq[:, j * LANES:(j + 1) * LANES]
        part_ref[...] = jnp.zeros(part_ref.shape, F32)
        part_ref[0:1, :] = acc * (0.5 / C)

    return pl.pallas_call(
        body, name='loss', grid=(T // tb,),
        in_specs=[pl.BlockSpec((tb, C), lambda i: (i, 0))] * 2,
        out_specs=[pl.BlockSpec((tb, C), lambda i: (i, 0)), pl.BlockSpec((8, LANES), lambda i: (i, 0))],
        out_shape=[jax.ShapeDtypeStruct((T, C), F32), jax.ShapeDtypeStruct((8 * (T // tb), LANES), F32)],
        compiler_params=_cparams(('parallel',)),
    )(y, target)


def _cols(w, lo, hi):
    return w.map(lambda t: t[:, lo:hi])


def _pad_rows(t, before, total):
    return jnp.pad(t, ((before, total - before - t.shape[0]), (0, 0)))


def _head_tile(g, lo, hi):
    return jnp.tile(g[lo:hi], MLA_HEADS).reshape(1, -1)


def _layer(l, x, v_first, wd, sp, cos, sin):
    T = x.shape[0]
    nm = f'l{l}'
    vres = l > 0
    w_in = wd['w_in']
    if vres:
        v1t = wd['rwkv_v1'].map(lambda t: t.T)
    else:
        v1t = W(jnp.zeros((MV_LORA, D_MODEL), BF16), jnp.zeros((MV_LORA, D_MODEL), BF16))
    zpad = W(jnp.zeros((64, D_MODEL), BF16), jnp.zeros((64, D_MODEL), BF16))
    w_rw = _wcat([w_in['rkv'], w_in['xg'], w_in['xwa'], w_in['kpe'], v1t, zpad], 0)
    h = rmsnorm(x, sp['attn_norm'], nm + '_anorm')
    gates, cq, ckv, rw, cv = multi_linear_t(
        h, [w_in['gates'], w_in['cq'], w_in['ckv'], w_rw, w_in['conv']], nm + '_win')

    v_mu = sp['rwkv_v_mu'] if vres else jnp.zeros((MV_LORA,), F32)
    mu_all = jnp.concatenate([sp['rwkv_mu'][0:768], sp['rwkv_mu'][896:1024], sp['rwkv_mu'][768:896],
                              jnp.zeros((QK_ROPE,), F32), v_mu, jnp.zeros((64,), F32)]).reshape(1, -1)
    rws = token_shift_mix(rw, mu_all, nm + '_shift')

    cqn = rmsnorm(cq, sp['mla_q_a_norm'], nm + '_qan')
    ckvn = rmsnorm(ckv, sp['mla_kv_a_norm'], nm + '_kvan')
    wq = wd['mla_wq_b'].map(lambda t: jnp.concatenate(
        [t.reshape(Q_LORA, MLA_HEADS, QK_HEAD)[:, :, 0:64].reshape(Q_LORA, 512),
         t.reshape(Q_LORA, MLA_HEADS, QK_HEAD)[:, :, 64:80].reshape(Q_LORA, 128),
         t.reshape(Q_LORA, MLA_HEADS, QK_HEAD)[:, :, 80:96].reshape(Q_LORA, 128)], axis=1))
    wkn = wd['mla_wkv_b'].map(lambda t: t.reshape(KV_LORA, MLA_HEADS, 128)[:, :, 0:64].reshape(KV_LORA, 512))
    wv = wd['mla_wkv_b'].map(lambda t: t.reshape(KV_LORA, MLA_HEADS, 128)[:, :, 64:128].reshape(KV_LORA, 512))
    q = linear(cqn, wq, name=nm + '_wq')
    kn, vv_att = multi_linear(ckvn, [wkn, wv], nm + '_wkv')
    gq, gk = sp['mla_q_norm'], sp['mla_k_norm']
    consts = _mla_consts()
    qk_specs = ([ROW(pieces=((0, 512), (512, 128), (640, 128))), ROW(), ROW(pieces=((1024, 128),)),
                 ROW(False), ROW(False)] + [FULL()] * 6 + [FULL(False)] * 8)
    qk_op = stage_op(_qk_prep_fn, qk_specs, [512, 128, 128, 512, 128, 128], nm + '_qkprep')
    Qn, Q1, Q2, Kn, K1, K2 = qk_op(q, kn, rws, cos, sin,
                                   _head_tile(gq, 0, 64), _head_tile(gq, 64, 80), _head_tile(gq, 80, 96),
                                   _head_tile(gk, 0, 64), _head_tile(gk, 64, 80), _head_tile(gk, 80, 96), *consts)
    o_att = attention(Qn, Q1, Q2, Kn, K1, K2, vv_att, nm + '_att')
    o_a = linear(o_att, wd['mla_w_o'], name=nm + '_wo')

    bm = _np_bf16(_block_ones(RW_WIDTH, RW_N))
    vec = lambda n: sp[n].reshape(1, -1)
    f32w = lambda n: wd[n].c + wd[n].b.astype(F32)
    w2p = _pad_rows(f32w('rwkv_w2'), 0, 128)
    a2p = _pad_rows(f32w('rwkv_a2'), 64, 128)
    g2 = f32w('rwkv_g2')
    rw_pieces = ((0, 256), (256, 256), (512, 256), (768, 128), (896, 128), (1024, 128))
    if vres:
        v2p = _pad_rows(f32w('rwkv_v2'), 32, 128)
        prep_specs = [ROW(pieces=rw_pieces), ROW()] + [FULL()] * 9 + [FULL(False)]
        prep_in = [rws, v_first, vec('rwkv_w0'), vec('rwkv_a0'), vec('rwkv_k_k'), vec('rwkv_k_a'), w2p, a2p, g2,
                   vec('rwkv_v0'), v2p, bm]
    else:
        prep_specs = [ROW(pieces=rw_pieces)] + [FULL()] * 7 + [FULL(False)]
        prep_in = [rws, vec('rwkv_w0'), vec('rwkv_a0'), vec('rwkv_k_k'), vec('rwkv_k_a'), w2p, a2p, g2, bm]
    prep_op = stage_op(_rwkv_prep_fn(vres), prep_specs, [256] * 7, nm + '_rwprep')
    r_, dec, k2, vv, an, bn, g = prep_op(*prep_in)
    if not vres:
        v_first = vv
    t3 = lambda t: t.reshape(T, 1, RW_WIDTH)
    y = wkv7(t3(r_), t3(dec), t3(k2), t3(vv), t3(an), t3(bn), nm + '_scan').reshape(T, RW_WIDTH)
    post_op = stage_op(_rwkv_post_fn, [ROW()] * 5 + [FULL()] * 3 + [FULL(False)], [256], nm + '_rwpost')
    yb = post_op(y, r_, k2, vv, g, vec('rwkv_ln_w'), vec('rwkv_ln_b'), sp['rwkv_r_k'].reshape(1, -1), bm)[0]
    o_b = linear(yb, wd['rwkv_w_o'], name=nm + '_rwo')

    oc_in = short_conv(cv, f32w('conv_w'), nm + '_conv')
    o_c = linear(oc_in, wd['conv_w_o'], name=nm + '_cwo')

    merge_op = stage_op(_merge_fn, [ROW(pieces=((0, 1024), (1024, 1024), (2048, 1024))), ROW(), ROW(), ROW()],
                        [D_MODEL], nm + '_merge')
    merged = merge_op(gates, o_a, o_b, o_c)[0]
    x2 = linear(merged, wd['w_out'], add=x, name=nm + '_wout')
    return x2, v_first


def _mlp(l, x, wd, sp):
    return mlp_block(x, sp['mlp_norm'].reshape(1, -1), wd['w_up'].b, wd['w_up'].c, wd['w_down'].b, wd['w_down'].c,
                     f'l{l}_mlp')


MLP_WEIGHTS = ('w_up', 'w_down')
GROUPS = ((0, 'mix'), (0, 'mlp'), (1, 'mix'), (1, 'mlp'))


def _entries(ml):
    layer, part = ml
    out = []
    for name, (layers, shape, axis) in SHARDED.items():
        l = layer if layers == DEPTH else layer - 1
        if not 0 <= l < layers or (name in MLP_WEIGHTS) != (part == 'mlp'):
            continue
        n = shape[0] * shape[1] // N_DEV
        if name == 'conv_w':
            out.append(('conv_w_hi', name, l, n))
            out.append(('conv_w_lo', name, l, n))
        else:
            out.append((name, name, l, n))
    return out


def _slot_size(n):
    return -(-n // LANES) * LANES


def _pack_rows(ml):
    total = sum(_slot_size(n) for _, _, _, n in _entries(ml))
    rows = -(-total // LANES)
    return -(-rows // PACK_ROW_MULT) * PACK_ROW_MULT


def _pack_flat(pieces, ml):
    rows = _pack_rows(ml)
    padded = []
    for p, (_, _, _, n) in zip(pieces, _entries(ml)):
        pad = _slot_size(n) - n
        if pad:
            p = jnp.pad(p, [(0, 0)] * (p.ndim - 1) + [(0, pad)])
        padded.append(p)
    flat = jnp.concatenate(padded, axis=-1)
    tail = rows * LANES - flat.shape[-1]
    if tail:
        flat = jnp.pad(flat, [(0, 0)] * (flat.ndim - 1) + [(0, tail)])
    return flat.reshape(flat.shape[:-1] + (rows, LANES))


def _unpack_flat(buf, ml):
    out, row = [], 0
    for _, _, _, n in _entries(ml):
        nrows = _slot_size(n) // LANES
        piece = buf[..., row:row + nrows, :].reshape(buf.shape[:-2] + (-1,))
        out.append(piece[..., :n])
        row += nrows
    return out


def _pack_shards(shards, ml, dtype, split_conv):
    pieces = []
    for slot, name, l, n in _entries(ml):
        a = shards[name][l]
        if name in TRANSPOSED:
            a = a.astype(dtype).T
        a = a.reshape(-1)
        if slot == 'conv_w_hi':
            a = a.astype(BF16).astype(F32) if split_conv else a
        elif slot == 'conv_w_lo':
            a = (a - a.astype(BF16).astype(F32)) if split_conv else jnp.zeros_like(a)
        pieces.append(a.astype(dtype))
    return _pack_flat(pieces, ml)


def _unpack_shards(buf, ml):
    out = {}
    for (slot, name, l, n), v in zip(_entries(ml), _unpack_flat(buf, ml)):
        if slot == 'conv_w_lo':
            continue
        layers, shape, axis = SHARDED[name]
        sshape = (shape[0] // N_DEV, shape[1]) if axis == 0 else (shape[0], shape[1] // N_DEV)
        out[(name, l)] = v.reshape(sshape[::-1]).T if name in TRANSPOSED else v.reshape(sshape)
    return out


def _to_full(blocks, shape, axis):
    if axis == 0:
        return blocks.reshape(shape)
    return blocks.reshape(N_DEV, shape[0], shape[1] // N_DEV).transpose(1, 0, 2).reshape(shape)


def _to_blocks(full, axis):
    r, c = full.shape
    if axis == 0:
        return full.reshape(N_DEV, -1)
    return full.reshape(r, N_DEV, c // N_DEV).transpose(1, 0, 2).reshape(N_DEV, -1)


def _unpack_gathered(gathered, ml):
    out, conv_hi = {}, None
    for (slot, name, l, n), v in zip(_entries(ml), _unpack_flat(gathered, ml)):
        layers, shape, axis = SHARDED[name]
        if name in TRANSPOSED:
            out[name] = v.reshape(-1, 16, LANES)
            continue
        full = _to_full(v, shape, axis)
        if slot == 'conv_w_hi':
            conv_hi = full
        elif slot == 'conv_w_lo':
            out[name] = conv_hi.astype(F32) + full.astype(F32)
        else:
            out[name] = full
    return out


def _pack_grads(grads, ml):
    pieces = []
    for slot, name, l, n in _entries(ml):
        if name in TRANSPOSED:
            blocks = jnp.concatenate(grads[name], axis=0).reshape(N_DEV, -1)
        else:
            blocks = _to_blocks(grads[name], SHARDED[name][2])
        if slot == 'conv_w_lo':
            blocks = jnp.zeros_like(blocks)
        pieces.append(blocks.astype(BF16))
    return _pack_flat(pieces, ml)


def _my_pos():
    return lax.axis_index('x'), lax.axis_index('y'), lax.axis_index('c')


def _flip(v, bit):
    return 1 - v if bit else v


def all_gather_blocks(x):
    rows = x.shape[0]

    def body(x_ref, out_ref, send_sems, recv_sems, local_sem):
        mx, my, mc = _my_pos()
        me, sibling = (mx, my, mc), (mx, my, 1 - mc)
        chips = [(1 - mx, my), (mx, 1 - my), (1 - mx, 1 - my)]

        def block(px, py, pc):
            return out_ref.at[4 * px + 2 * py + pc]

        def copy(k, blk, to, src=None):
            return pltpu.make_async_remote_copy(
                src_ref=block(*blk) if src is None else src, dst_ref=block(*blk),
                send_sem=send_sems.at[k], recv_sem=recv_sems.at[k], device_id=to, device_id_type=MESH)

        mine = pltpu.make_async_copy(x_ref, block(*me), local_sem)
        mine.start()
        first = [copy(0, me, sibling, src=x_ref)]
        first += [copy(1 + j, me, (*chip, mc), src=x_ref) for j, chip in enumerate(chips)]
        for cp in first:
            cp.start()
        passed = [copy(4 + j, (*chip, mc), sibling) for j, chip in enumerate(chips)]
        for j, chip in enumerate(chips):
            copy(1 + j, (*chip, mc), me).wait_recv()
            passed[j].start()
        copy(0, sibling, me).wait_recv()
        for j, chip in enumerate(chips):
            copy(4 + j, (*chip, 1 - mc), me).wait_recv()
        for cp in first + passed:
            cp.wait_send()
        mine.wait()

    return pl.pallas_call(
        body, name='all_gather_weights',
        out_shape=jax.ShapeDtypeStruct((N_DEV, rows, LANES), x.dtype),
        in_specs=[pl.BlockSpec(memory_space=pl.ANY)], out_specs=pl.BlockSpec(memory_space=pl.ANY),
        scratch_shapes=[pltpu.SemaphoreType.DMA((7,)), pltpu.SemaphoreType.DMA((7,)), pltpu.SemaphoreType.DMA],
    )(x)


def exchange_blocks(g):
    def body(g_ref, out_ref, send_sems, recv_sems, local_sem):
        mx, my, mc = _my_pos()
        me = 4 * mx + 2 * my + mc
        mine = pltpu.make_async_copy(g_ref.at[me], out_ref.at[me], local_sem)
        mine.start()
        copies, arrivals = [], []
        for k in range(1, N_DEV):
            peer = (_flip(mx, k & 4), _flip(my, k & 2), _flip(mc, k & 1))
            pidx = 4 * peer[0] + 2 * peer[1] + peer[2]
            copies.append(pltpu.make_async_remote_copy(
                src_ref=g_ref.at[pidx], dst_ref=out_ref.at[me], send_sem=send_sems.at[k - 1],
                recv_sem=recv_sems.at[k - 1], device_id=peer, device_id_type=MESH))
            arrivals.append(pltpu.make_async_remote_copy(
                src_ref=g_ref.at[me], dst_ref=out_ref.at[pidx], send_sem=send_sems.at[k - 1],
                recv_sem=recv_sems.at[k - 1], device_id=peer, device_id_type=MESH))
        for cp in copies:
            cp.start()
        for cp in arrivals:
            cp.wait_recv()
        for cp in copies:
            cp.wait_send()
        mine.wait()

    return pl.pallas_call(
        body, name='exchange_grads',
        out_shape=jax.ShapeDtypeStruct(g.shape, g.dtype),
        in_specs=[pl.BlockSpec(memory_space=pl.ANY)], out_specs=pl.BlockSpec(memory_space=pl.ANY),
        scratch_shapes=[pltpu.SemaphoreType.DMA((7,)), pltpu.SemaphoreType.DMA((7,)), pltpu.SemaphoreType.DMA],
    )(g)


HBM_SPEC = pl.BlockSpec(memory_space=pltpu.HBM)
SEM_SPEC = pl.BlockSpec(memory_space=pltpu.SEMAPHORE)
DATAFLOW_EFFECT = pltpu.SideEffectType.DATAFLOW_SIDE_EFFECTING


def _direct_copies(src_ref, land_ref, send_sems, recv_sems, per_peer):
    mx, my, mc = _my_pos()
    me = 4 * mx + 2 * my + mc
    copies = []
    for k in range(1, N_DEV):
        peer = (_flip(mx, k & 4), _flip(my, k & 2), _flip(mc, k & 1))
        pidx = 4 * peer[0] + 2 * peer[1] + peer[2]
        copies.append(pltpu.make_async_remote_copy(
            src_ref=src_ref.at[pidx] if per_peer else src_ref, dst_ref=land_ref.at[me],
            send_sem=send_sems.at[k - 1], recv_sem=recv_sems.at[k - 1], device_id=peer, device_id_type=MESH))
    return copies


def send_start(src, per_peer, name):
    block = src.shape[1:] if per_peer else src.shape
    land_shape = (N_DEV,) + tuple(block)

    def body(src_ref, land_ref, send_sems, recv_sems, src_thru, land_thru, token):
        for cp in _direct_copies(src_ref, land_ref, send_sems, recv_sems, per_peer):
            cp.start()
        token[...] = jnp.zeros(token.shape, F32)

    send_sems, recv_sems, src_thru, land_thru, token = pl.pallas_call(
        body, name=name,
        out_shape=(pltpu.SemaphoreType.DMA((N_DEV - 1,)), pltpu.SemaphoreType.DMA((N_DEV - 1,)),
                   pltpu.HBM(src.shape, src.dtype), pltpu.HBM(land_shape, src.dtype),
                   jax.ShapeDtypeStruct((8, LANES), F32)),
        in_specs=(HBM_SPEC, HBM_SPEC),
        out_specs=(SEM_SPEC, SEM_SPEC, HBM_SPEC, HBM_SPEC, pl.BlockSpec(memory_space=pltpu.VMEM)),
        input_output_aliases={0: 2, 1: 3},
        compiler_params=pltpu.CompilerParams(has_side_effects=DATAFLOW_EFFECT),
    )(pltpu.with_memory_space_constraint(src, pltpu.HBM),
      pltpu.with_memory_space_constraint(lax.empty(land_shape, src.dtype), pltpu.HBM))
    return (send_sems, recv_sems, src_thru, land_thru), token[0, 0]


def send_wait(handles, after, per_peer, name):
    send_sems, recv_sems, src_thru, land_thru = handles

    def body(src_ref, land_ref, send_sems, recv_sems, after_ref, src_dead, got_ref):
        for cp in _direct_copies(src_ref, land_ref, send_sems, recv_sems, per_peer):
            cp.wait_send()
            cp.wait_recv()

    return pl.pallas_call(
        body, name=name,
        out_shape=(pltpu.HBM(src_thru.shape, src_thru.dtype), pltpu.HBM(land_thru.shape, land_thru.dtype)),
        in_specs=(HBM_SPEC, HBM_SPEC, SEM_SPEC, SEM_SPEC, pl.BlockSpec(memory_space=pl.ANY)),
        out_specs=(HBM_SPEC, HBM_SPEC), input_output_aliases={0: 0, 1: 1},
        compiler_params=pltpu.CompilerParams(has_side_effects=DATAFLOW_EFFECT),
    )(src_thru, land_thru, send_sems, recv_sems, after)[1]


def _adamw_math(w, g, m, v):
    m2 = ADAM_B1 * m + (1.0 - ADAM_B1) * g
    v2 = ADAM_B2 * v + (1.0 - ADAM_B2) * (g * g)
    m_hat = m2 / (1.0 - ADAM_B1 ** ADAM_STEP)
    v_hat = v2 / (1.0 - ADAM_B2 ** ADAM_STEP)
    delta = -ADAM_LR * (m_hat / (jnp.sqrt(v_hat) + ADAM_EPS) + ADAM_WD * w)
    return delta, m2, v2


def sum_parts(parts, name):
    rows = parts.shape[1]
    rb = PACK_ROW_MULT

    def body(p_ref, g_ref):
        g = p_ref[0].astype(F32)
        for j in range(1, N_DEV):
            g = g + p_ref[j].astype(F32)
        g_ref[...] = g

    return pl.pallas_call(
        body, name=name, grid=(rows // rb,),
        in_specs=[pl.BlockSpec((N_DEV, rb, LANES), lambda i: (0, i, 0))],
        out_specs=pl.BlockSpec((rb, LANES), lambda i: (i, 0)),
        out_shape=jax.ShapeDtypeStruct((rows, LANES), F32), compiler_params=_cparams(('parallel',)),
    )(parts)


ADAMW_BLOCK_BYTES = 1 << 20


def adamw_weight(name, w, m, v, grads):
    layers, a, b = w.shape
    ra = a
    while ra * b * 4 > ADAMW_BLOCK_BYTES and ra % 16 == 0:
        ra //= 2

    def body(*refs):
        w_ref, m_ref, v_ref = refs[:3]
        g_refs = refs[3:3 + layers]
        g_ref, d_ref, m2_ref, v2_ref = refs[3 + layers:]
        g = g_refs[0][...]
        for l in range(1, layers):
            g = jnp.where(pl.program_id(0) == l, g_refs[l][...], g)
        delta, m2, v2 = _adamw_math(w_ref[0], g, m_ref[0], v_ref[0])
        g_ref[0] = g
        d_ref[0] = delta
        m2_ref[0] = m2
        v2_ref[0] = v2

    blk = pl.BlockSpec((1, ra, b), lambda l, i: (l, i, 0))
    gblk = pl.BlockSpec((ra, b), lambda l, i: (i, 0))
    sh = jax.ShapeDtypeStruct(w.shape, F32)
    return pl.pallas_call(
        body, name='adamw_' + name, grid=(layers, a // ra), in_specs=[blk] * 3 + [gblk] * layers,
        out_specs=[blk] * 4, out_shape=[sh] * 4, compiler_params=_cparams(('parallel', 'parallel')),
    )(w, m, v, *grads)


def allreduce_adamw_small(g, w, m, v):
    rows = g.shape[0]

    def body(g_ref, w_ref, m_ref, v_ref, gs_ref, d_ref, m2_ref, v2_ref, all_ref, send_sems, recv_sems):
        mx, my, mc = _my_pos()
        me, sibling = (mx, my, mc), (mx, my, 1 - mc)
        chips = [(1 - mx, my), (mx, 1 - my), (1 - mx, 1 - my)]

        def block(px, py, pc):
            return all_ref.at[4 * px + 2 * py + pc]

        def copy(k, blk, to, src=None):
            return pltpu.make_async_remote_copy(
                src_ref=block(*blk) if src is None else src, dst_ref=block(*blk),
                send_sem=send_sems.at[k], recv_sem=recv_sems.at[k], device_id=to, device_id_type=MESH)

        first = [copy(0, me, sibling, src=g_ref)]
        first += [copy(1 + j, me, (*chip, mc), src=g_ref) for j, chip in enumerate(chips)]
        for cp in first:
            cp.start()
        passed = [copy(4 + j, (*chip, mc), sibling) for j, chip in enumerate(chips)]
        for j, chip in enumerate(chips):
            copy(1 + j, (*chip, mc), me).wait_recv()
            passed[j].start()
        copy(0, sibling, me).wait_recv()
        for j, chip in enumerate(chips):
            copy(4 + j, (*chip, 1 - mc), me).wait_recv()
        for cp in first + passed:
            cp.wait_send()
        my_idx = 4 * mx + 2 * my + mc
        total = jnp.zeros((rows, LANES), F32)
        for j in range(N_DEV):
            total = total + jnp.where(my_idx == j, g_ref[...], all_ref[j])
        delta, m2, v2 = _adamw_math(w_ref[...], total, m_ref[...], v_ref[...])
        gs_ref[...] = total
        d_ref[...] = delta
        m2_ref[...] = m2
        v2_ref[...] = v2

    vm = pl.BlockSpec(memory_space=pltpu.VMEM)
    sh = jax.ShapeDtypeStruct((rows, LANES), F32)
    return pl.pallas_call(
        body, name='allreduce_adamw_small', in_specs=[vm] * 4, out_specs=[vm] * 4, out_shape=[sh] * 4,
        scratch_shapes=[pltpu.VMEM((N_DEV, rows, LANES), F32), pltpu.SemaphoreType.DMA((7,)),
                        pltpu.SemaphoreType.DMA((7,))],
    )(g, w, m, v)


SMALL_COUNT = 11680


def _small_pack(d, extra=None):
    assert sum(d[n].size for n in SMALL_NAMES) == SMALL_COUNT
    flat = jnp.concatenate([d[n].reshape(-1) for n in SMALL_NAMES] + ([] if extra is None else [extra.reshape(1)]))
    rows = -(-(SMALL_COUNT + 1) // (8 * LANES)) * 8
    return jnp.pad(flat, (0, rows * LANES - flat.shape[0])).reshape(rows, LANES)


def _small_unpack(buf, like):
    flat = buf.reshape(-1)
    out, off = {}, 0
    for n in SMALL_NAMES:
        sz = int(np.prod(like[n].shape))
        out[n] = flat[off:off + sz].reshape(like[n].shape)
        off += sz
    return out


def _rope_tables(positions):
    freqs = ROPE_THETA ** (-(jnp.arange(QK_ROPE // 2, dtype=F32) * 2.0 / QK_ROPE))
    ang = positions.astype(F32)[:, None] * freqs
    return jnp.tile(jnp.cos(ang), (1, MLA_HEADS)), jnp.tile(jnp.sin(ang), (1, MLA_HEADS))


def _layer_weights(gathered, carriers):
    wd = {}
    for name, full in gathered.items():
        if name == 'w_in':
            wd[name] = {seg: W(full[lo // 2:hi // 2].reshape(hi - lo, D_MODEL), c)
                        for (seg, lo, hi), c in zip(WIN_SEGS, carriers[name])}
        else:
            wd[name] = W(full, carriers[name])
    return wd


F32_GRAD_WEIGHTS = ('rwkv_w2', 'rwkv_a2', 'rwkv_g2', 'rwkv_v2', 'conv_w')


def _make_carriers(gathered):
    gathered, carriers = dict(gathered), {}
    for name, full in gathered.items():
        if name == 'w_in':
            carriers[name] = tuple(jnp.zeros((hi - lo, D_MODEL), BF16) for _, lo, hi in WIN_SEGS)
        elif name == 'conv_w':
            gathered[name] = full.astype(BF16)
            carriers[name] = full - full.astype(BF16).astype(F32)
        else:
            carriers[name] = jnp.zeros(SHARDED[name][1], F32 if name in F32_GRAD_WEIGHTS else BF16)
    return gathered, carriers


def _layer_small(small, ml):
    out = {}
    for n in SMALL_NAMES:
        l = ml if small[n].shape[0] == DEPTH else ml - 1
        if 0 <= l < small[n].shape[0]:
            out[n] = small[n][l]
    return out


def _train_step(x, positions, loss_target, weights, moms_m, moms_v):
    shards = {n: weights[n] for n in SHARDED}
    small = {n: weights[n] for n in SMALL_NAMES}
    me = 4 * lax.axis_index('x') + 2 * lax.axis_index('y') + lax.axis_index('c')
    cos, sin = _rope_tables(positions[0])

    tag = lambda key: f'l{key[0]}_{key[1]}'
    first, later = GROUPS[0], GROUPS[1:]

    own = {key: _pack_shards(shards, key, BF16, True) for key in GROUPS}
    held = lax.optimization_barrier((all_gather_blocks(own[first]), *[own[key] for key in later]))
    blocks, gathers, x0 = {first: held[0]}, {}, x[0]
    for key, mine in zip(later, held[1:]):
        own[key] = mine
        gathers[key], token = send_start(mine, False, f'gather_{tag(key)}_start')
        x0 = x0 + token

    def weights_of(key, after):
        if key not in blocks:
            landed = send_wait(gathers[key], after, False, f'gather_{tag(key)}_wait')
            blocks[key] = lax.dynamic_update_slice(landed, own[key][None], (me, 0, 0))
        return _make_carriers(_unpack_gathered(blocks[key], key))

    def small_of(layer, part):
        sp = _layer_small(small, layer)
        return {n: v for n, v in sp.items() if (n == 'mlp_norm') == (part == 'mlp')}

    def mixer(layer, gathered):
        if layer == 0:
            return lambda c, s, xx: _layer(0, xx, None, _layer_weights(gathered, c), s, cos, sin)
        return lambda c, s, xx, vf: _layer(layer, xx, vf, _layer_weights(gathered, c), s, cos, sin)[0]

    def mlp(layer, gathered):
        return lambda c, s, xx: _mlp(layer, xx, _layer_weights(gathered, c), s)

    gathered, carriers = weights_of((0, 'mix'), None)
    (h, v_first), vjp_mix0 = jax.vjp(mixer(0, gathered), carriers, small_of(0, 'mix'), x0)
    gathered, carriers = weights_of((0, 'mlp'), h)
    h, vjp_mlp0 = jax.vjp(mlp(0, gathered), carriers, small_of(0, 'mlp'), h)
    gathered, carriers = weights_of((1, 'mix'), h)
    h, vjp_mix1 = jax.vjp(mixer(1, gathered), carriers, small_of(1, 'mix'), h, v_first)
    gathered, carriers = weights_of((1, 'mlp'), h)
    y, vjp_mlp1 = jax.vjp(mlp(1, gathered), carriers, small_of(1, 'mlp'), h)

    dy, loss_parts = _loss_call(y, loss_target[0])

    shipped = {}

    def ship(key, gw):
        packed = _pack_grads(gw, key)
        handles, token = send_start(packed, True, f'grads_{tag(key)}_start')
        shipped[key] = (handles, packed)
        return token

    gw, gs_mlp1, d = vjp_mlp1(dy)
    d = d + ship((1, 'mlp'), gw)
    gw, gs_mix1, d, dvf = vjp_mix1(d)
    d = d + ship((1, 'mix'), gw)
    gw, gs_mlp0, d = vjp_mlp0(d)
    d = d + ship((0, 'mlp'), gw)
    gw, gs_mix0, g_x = vjp_mix0((d, dvf))
    parts = {first: exchange_blocks(_pack_grads(gw, first))}
    for key, (handles, packed) in shipped.items():
        landed = send_wait(handles, g_x, True, f'grads_{tag(key)}_wait')
        parts[key] = lax.dynamic_update_slice(landed, lax.dynamic_slice_in_dim(packed, me, 1, axis=0), (me, 0, 0))
    gs0, gs1 = {**gs_mix0, **gs_mlp0}, {**gs_mix1, **gs_mlp1}

    grads = {}
    for key in GROUPS:
        grads.update(_unpack_shards(sum_parts(parts[key], f'sum_grads_{tag(key)}'), key))
    sharded_out = [{}, {}, {}, {}]
    for n in SHARDED:
        outs = adamw_weight(n, weights[n], moms_m[n], moms_v[n], [grads[(n, l)] for l in range(SHARDED[n][0])])
        for i in range(4):
            sharded_out[i][n] = outs[i]

    g_small = {}
    for n in SMALL_NAMES:
        per = [g[n] for g in (gs0, gs1) if n in g]
        g_small[n] = jnp.stack(per)
    small_bufs = allreduce_adamw_small(
        _small_pack(g_small, jnp.sum(loss_parts)), _small_pack(small), _small_pack({n: moms_m[n] for n in SMALL_NAMES}),
        _small_pack({n: moms_v[n] for n in SMALL_NAMES}))
    loss = small_bufs[0].reshape(-1)[SMALL_COUNT]
    small_out = [_small_unpack(b, small) for b in small_bufs]

    pick = lambda i: [sharded_out[i][n] if n in SHARDED else small_out[i][n] for n in WEIGHT_NAMES]
    return (loss, g_x[None], *pick(0), *pick(1), *pick(2), *pick(3))


def kernel(x, positions, attn_norm, w_in, mla_q_a_norm, mla_wq_b, mla_kv_a_norm, mla_wkv_b, mla_q_norm, mla_k_norm, mla_w_o, rwkv_mu, rwkv_w0, rwkv_w2, rwkv_a0, rwkv_a2, rwkv_g2, rwkv_k_k, rwkv_k_a, rwkv_r_k, rwkv_ln_w, rwkv_ln_b, rwkv_w_o, rwkv_v1, rwkv_v_mu, rwkv_v0, rwkv_v2, conv_w, conv_w_o, w_out, mlp_norm, w_up, w_down, loss_target, m_attn_norm, m_w_in, m_mla_q_a_norm, m_mla_wq_b, m_mla_kv_a_norm, m_mla_wkv_b, m_mla_q_norm, m_mla_k_norm, m_mla_w_o, m_rwkv_mu, m_rwkv_w0, m_rwkv_w2, m_rwkv_a0, m_rwkv_a2, m_rwkv_g2, m_rwkv_k_k, m_rwkv_k_a, m_rwkv_r_k, m_rwkv_ln_w, m_rwkv_ln_b, m_rwkv_w_o, m_rwkv_v1, m_rwkv_v_mu, m_rwkv_v0, m_rwkv_v2, m_conv_w, m_conv_w_o, m_w_out, m_mlp_norm, m_w_up, m_w_down, v_attn_norm, v_w_in, v_mla_q_a_norm, v_mla_wq_b, v_mla_kv_a_norm, v_mla_wkv_b, v_mla_q_norm, v_mla_k_norm, v_mla_w_o, v_rwkv_mu, v_rwkv_w0, v_rwkv_w2, v_rwkv_a0, v_rwkv_a2, v_rwkv_g2, v_rwkv_k_k, v_rwkv_k_a, v_rwkv_r_k, v_rwkv_ln_w, v_rwkv_ln_b, v_rwkv_w_o, v_rwkv_v1, v_rwkv_v_mu, v_rwkv_v0, v_rwkv_v2, v_conv_w, v_conv_w_o, v_w_out, v_mlp_norm, v_w_up, v_w_down):
    args = locals()
    weights = {n: args[n] for n in WEIGHT_NAMES}
    moms_m = {n: args['m_' + n] for n in WEIGHT_NAMES}
    moms_v = {n: args['v_' + n] for n in WEIGHT_NAMES}
    return _train_step(x, positions, loss_target, weights, moms_m, moms_v)
```

```python
import functools

import numpy as np
import jax
import jax.numpy as jnp
from jax import lax
from jax.experimental import pallas as pl
from jax.experimental.pallas import tpu as pltpu

F32 = jnp.float32
BF16 = jnp.bfloat16

N_DEV = 8
LANES = 128
D_MODEL = 1024
DEPTH = 2
MLA_HEADS = 8
QK_NOPE = 64
QK_ROPE = 32
QK_HEAD = QK_NOPE + QK_ROPE
V_HEAD = 64
Q_LORA = 384
KV_LORA = 256
ROPE_THETA = 10000.0
RW_HEADS = 4
RW_N = 64
RW_WIDTH = RW_HEADS * RW_N
MV_LORA = 32
GN_EPS = 64e-5
CONV_WIDTH = 256
D_FF = 4 * D_MODEL
NORM_EPS = 1e-6
ADAM_LR = 0.001
ADAM_B1 = 0.9
ADAM_B2 = 0.999
ADAM_EPS = 1e-08
ADAM_WD = 0.01
ADAM_STEP = 10

VMEM_LIMIT = 56 * 1024 * 1024
MESH = pl.DeviceIdType.MESH

WEIGHT_NAMES = ['attn_norm', 'w_in', 'mla_q_a_norm', 'mla_wq_b', 'mla_kv_a_norm', 'mla_wkv_b', 'mla_q_norm',
                'mla_k_norm', 'mla_w_o', 'rwkv_mu', 'rwkv_w0', 'rwkv_w2', 'rwkv_a0', 'rwkv_a2', 'rwkv_g2',
                'rwkv_k_k', 'rwkv_k_a', 'rwkv_r_k', 'rwkv_ln_w', 'rwkv_ln_b', 'rwkv_w_o', 'rwkv_v1',
                'rwkv_v_mu', 'rwkv_v0', 'rwkv_v2', 'conv_w', 'conv_w_o', 'w_out', 'mlp_norm', 'w_up', 'w_down']

SHARDED = {
    'w_in': (2, (1024, 5536), 1), 'mla_wq_b': (2, (384, 768), 1), 'mla_wkv_b': (2, (256, 1024), 1),
    'mla_w_o': (2, (512, 1024), 1), 'rwkv_w2': (2, (64, 256), 1), 'rwkv_a2': (2, (64, 256), 1),
    'rwkv_g2': (2, (128, 256), 1), 'rwkv_w_o': (2, (256, 1024), 1), 'conv_w': (2, (3, 256), 1),
    'conv_w_o': (2, (256, 1024), 1), 'w_out': (2, (1024, 1024), 0), 'w_up': (2, (1024, 4096), 1),
    'w_down': (2, (4096, 1024), 0), 'rwkv_v1': (1, (1024, 32), 0), 'rwkv_v2': (1, (32, 256), 1),
}
SMALL_NAMES = [n for n in WEIGHT_NAMES if n not in SHARDED]
TRANSPOSED = ('w_in',)
WIN_SEGS = (('gates', 0, 3072), ('cq', 3072, 3456), ('ckv', 3456, 3712), ('kpe', 3712, 3744), ('rkv', 3744, 4512),
            ('xwa', 4512, 4640), ('xg', 4640, 4768), ('conv', 4768, 5536))
PACK_ROW_MULT = 512


def _cparams(sem=None, **kw):
    if sem is not None:
        kw['dimension_semantics'] = sem
    return pltpu.CompilerParams(vmem_limit_bytes=VMEM_LIMIT, **kw)


def _pick(n, cands):
    for c in cands:
        if n % c == 0:
            return c
    raise ValueError(f'no tile for {n}')


def _mm_nn(a, b, add=None, name='mm_nn'):
    M, K = a.shape
    N = b.shape[1]
    tm = _pick(M, (1024, 512, 256, 128))
    tn = _pick(N, (512, 384, 256, 128))
    tk = _pick(K, (1024, 512, 384, 256, 128))
    nk = K // tk
    has_add = add is not None

    def body(*refs):
        if has_add:
            a_ref, b_ref, add_ref, o_ref, acc_ref = refs
        else:
            a_ref, b_ref, o_ref, acc_ref = refs
        kk = pl.program_id(2)
        part = jnp.dot(a_ref[...].astype(BF16), b_ref[...].astype(BF16), preferred_element_type=F32)

        @pl.when(kk == 0)
        def _():
            acc_ref[...] = part

        @pl.when(kk > 0)
        def _():
            acc_ref[...] += part

        @pl.when(kk == nk - 1)
        def _():
            if has_add:
                o_ref[...] = acc_ref[...] + add_ref[...]
            else:
                o_ref[...] = acc_ref[...]

    in_specs = [pl.BlockSpec((tm, tk), lambda i, j, k: (i, k)), pl.BlockSpec((tk, tn), lambda i, j, k: (k, j))]
    args = [a, b]
    if has_add:
        in_specs.append(pl.BlockSpec((tm, tn), lambda i, j, k: (i, j)))
        args.append(add)
    return pl.pallas_call(
        body, name=name, grid=(M // tm, N // tn, nk), in_specs=in_specs,
        out_specs=pl.BlockSpec((tm, tn), lambda i, j, k: (i, j)),
        out_shape=jax.ShapeDtypeStruct((M, N), F32),
        scratch_shapes=[pltpu.VMEM((tm, tn), F32)],
        compiler_params=_cparams(('parallel', 'parallel', 'arbitrary')),
    )(*args)


def _mm_nt(a, b, add=None, name='mm_nt'):
    M, N = a.shape
    blocked = b.ndim == 3
    K = b.shape[-2]
    tm = _pick(M, (1024, 512, 256, 128))
    tk = _pick(K, (512, 384, 256, 128))
    tn = N // N_DEV if blocked else _pick(N, (1024, 512, 384, 256, 128))
    nn = N // tn
    has_add = add is not None

    def body(*refs):
        if has_add:
            a_ref, b_ref, add_ref, o_ref, acc_ref = refs
        else:
            a_ref, b_ref, o_ref, acc_ref = refs
        kk = pl.program_id(2)
        part = lax.dot_general(a_ref[...].astype(BF16), b_ref[...].astype(BF16), (((1,), (1,)), ((), ())),
                               preferred_element_type=F32)

        @pl.when(kk == 0)
        def _():
            acc_ref[...] = part

        @pl.when(kk > 0)
        def _():
            acc_ref[...] += part

        @pl.when(kk == nn - 1)
        def _():
            if has_add:
                o_ref[...] = acc_ref[...] + add_ref[...]
            else:
                o_ref[...] = acc_ref[...]

    if blocked:
        b_spec = pl.BlockSpec((None, tk, tn), lambda i, j, k: (k, j, 0))
    else:
        b_spec = pl.BlockSpec((tk, tn), lambda i, j, k: (j, k))
    in_specs = [pl.BlockSpec((tm, tn), lambda i, j, k: (i, k)), b_spec]
    args = [a, b]
    if has_add:
        in_specs.append(pl.BlockSpec((tm, tk), lambda i, j, k: (i, j)))
        args.append(add)
    return pl.pallas_call(
        body, name=name, grid=(M // tm, K // tk, nn), in_specs=in_specs,
        out_specs=pl.BlockSpec((tm, tk), lambda i, j, k: (i, j)),
        out_shape=jax.ShapeDtypeStruct((M, K), F32),
        scratch_shapes=[pltpu.VMEM((tm, tk), F32)],
        compiler_params=_cparams(('parallel', 'parallel', 'arbitrary')),
    )(*args)


def _mm_tn(a, b, name='mm_tn', blocked=False):
    M, K = a.shape
    N = b.shape[1]
    tm = _pick(M, (1024, 512, 256, 128))
    tk = _pick(K, (512, 384, 256, 128))
    tn = N // N_DEV if blocked else _pick(N, (512, 384, 256, 128))
    nm = M // tm

    def body(a_ref, b_ref, o_ref, acc_ref):
        mm = pl.program_id(2)
        part = lax.dot_general(a_ref[...].astype(BF16), b_ref[...].astype(BF16), (((0,), (0,)), ((), ())),
                               preferred_element_type=F32)

        @pl.when(mm == 0)
        def _():
            acc_ref[...] = part

        @pl.when(mm > 0)
        def _():
            acc_ref[...] += part

        @pl.when(mm == nm - 1)
        def _():
            o_ref[...] = acc_ref[...].astype(BF16)

    if blocked:
        out_spec = pl.BlockSpec((None, tk, tn), lambda i, j, m: (j, i, 0))
        out_shape = jax.ShapeDtypeStruct((N_DEV, K, tn), BF16)
    else:
        out_spec = pl.BlockSpec((tk, tn), lambda i, j, m: (i, j))
        out_shape = jax.ShapeDtypeStruct((K, N), BF16)
    return pl.pallas_call(
        body, name=name, grid=(K // tk, N // tn, nm),
        in_specs=[pl.BlockSpec((tm, tk), lambda i, j, m: (m, i)), pl.BlockSpec((tm, tn), lambda i, j, m: (m, j))],
        out_specs=out_spec, out_shape=out_shape,
        scratch_shapes=[pltpu.VMEM((tk, tn), F32)],
        compiler_params=_cparams(('parallel', 'parallel', 'arbitrary')),
    )(a, b)


@functools.partial(jax.custom_vjp, nondiff_argnums=(4,))
def _linear_add(a, wb, wc, add, name):
    return _mm_nn(a, wb, add, name=name + '_f')


def _linear_add_fwd(a, wb, wc, add, name):
    return _mm_nn(a, wb, add, name=name + '_f'), (a, wb)


def _linear_add_bwd(name, res, dy):
    a, wb = res
    return _mm_nt(dy, wb, name=name + '_da'), None, _mm_tn(a, dy, name=name + '_dw'), dy


_linear_add.defvjp(_linear_add_fwd, _linear_add_bwd)


@functools.partial(jax.custom_vjp, nondiff_argnums=(3,))
def _multi_linear(a, wbs, wcs, name):
    return tuple(_mm_nn(a, wb, name=f'{name}_f{i}') for i, wb in enumerate(wbs))


def _multi_linear_fwd(a, wbs, wcs, name):
    return _multi_linear(a, wbs, wcs, name), (a, wbs)


def _multi_linear_bwd(name, res, dys):
    a, wbs = res
    da = None
    for i, (dy, wb) in enumerate(zip(dys, wbs)):
        da = _mm_nt(dy, wb, add=da, name=f'{name}_da{i}')
    dws = tuple(_mm_tn(a, dy, name=f'{name}_dw{i}') for i, dy in enumerate(dys))
    return da, None, dws


_multi_linear.defvjp(_multi_linear_fwd, _multi_linear_bwd)


class W:
    def __init__(self, b, c):
        self.b, self.c = b, c

    def map(self, fn):
        return W(fn(self.b), fn(self.c))


def _wcat(ws, axis):
    return W(jnp.concatenate([w.b for w in ws], axis), jnp.concatenate([w.c for w in ws], axis))


def linear(a, w, add=None, name='lin'):
    if add is None:
        return _multi_linear(a, (w.b,), (w.c,), name)[0]
    return _linear_add(a, w.b, w.c, add, name)


def multi_linear(a, ws, name):
    return _multi_linear(a, tuple(w.b for w in ws), tuple(w.c for w in ws), name)


@functools.partial(jax.custom_vjp, nondiff_argnums=(3,))
def _multi_linear_t(a, wbs, wcs, name):
    return tuple(_mm_nt(a, wb, name=f'{name}_f{i}') for i, wb in enumerate(wbs))


def _multi_linear_t_fwd(a, wbs, wcs, name):
    return _multi_linear_t(a, wbs, wcs, name), (a, wbs)


def _multi_linear_t_bwd(name, res, dys):
    a, wbs = res
    da = None
    for i, (dy, wb) in enumerate(zip(dys, wbs)):
        da = _mm_nn(dy, wb, add=da, name=f'{name}_da{i}')
    dws = tuple(_mm_tn(dy, a, name=f'{name}_dw{i}') for i, dy in enumerate(dys))
    return da, None, dws


_multi_linear_t.defvjp(_multi_linear_t_fwd, _multi_linear_t_bwd)


def multi_linear_t(a, ws, name):
    return _multi_linear_t(a, tuple(w.b for w in ws), tuple(w.c for w in ws), name)


def ROW(diff=True, pieces=None):
    return ('row', diff, pieces)


def FULL(diff=True):
    return ('full', diff, None)


def _load_args(refs, specs):
    args, amap = [], []
    for i, (ref, (kind, diff, pieces)) in enumerate(zip(refs, specs)):
        if pieces is None:
            args.append(ref[...])
            amap.append((i, None))
        else:
            for (s, w) in pieces:
                args.append(ref[:, s:s + w])
                amap.append((i, (s, w)))
    return args, amap


def _stage_in_specs(ins, specs, tb):
    out = []
    for a, (kind, _, _) in zip(ins, specs):
        if kind == 'row':
            out.append(pl.BlockSpec((tb, a.shape[1]), lambda i: (i, 0)))
        else:
            out.append(pl.BlockSpec(a.shape, lambda i: (0, 0)))
    return out


def _stage_fwd(fn, ins, specs, out_widths, name, tb, out_dtype=F32):
    T = [a for a, s in zip(ins, specs) if s[0] == 'row'][0].shape[0]
    tb = min(tb, T)
    n_in = len(ins)

    def body(*refs):
        args, _ = _load_args(refs[:n_in], specs)
        outs = fn(*args)
        for o_ref, o in zip(refs[n_in:], outs):
            o_ref[...] = o.astype(out_dtype)

    return pl.pallas_call(
        body, name=name + '_f', grid=(T // tb,), in_specs=_stage_in_specs(ins, specs, tb),
        out_specs=[pl.BlockSpec((tb, w), lambda i: (i, 0)) for w in out_widths],
        out_shape=[jax.ShapeDtypeStruct((T, w), out_dtype) for w in out_widths],
        compiler_params=_cparams(('parallel',)),
    )(*ins)


def _stage_bwd(fn, ins, specs, out_widths, douts, name, tb):
    T = [a for a, s in zip(ins, specs) if s[0] == 'row'][0].shape[0]
    tb = min(tb, T)
    n_in, n_out = len(ins), len(out_widths)
    diff_inputs = [i for i, s in enumerate(specs) if s[1]]

    def body(*refs):
        in_refs, dout_refs, g_refs = refs[:n_in], refs[n_in:n_in + n_out], refs[n_in + n_out:]
        args, amap = _load_args(in_refs, specs)
        didx = [j for j, (i, _) in enumerate(amap) if specs[i][1]]

        def f(*dv):
            full = list(args)
            for j, v in zip(didx, dv):
                full[j] = v
            return tuple(fn(*full))

        _, vjp = jax.vjp(f, *[args[j] for j in didx])
        gs = vjp(tuple(d[...] for d in dout_refs))
        gmap = {j: g for j, g in zip(didx, gs)}
        first = pl.program_id(0) == 0
        for g_ref, i in zip(g_refs, diff_inputs):
            kind, _, pieces = specs[i]
            js = [j for j, (ii, _) in enumerate(amap) if ii == i]
            if kind == 'row':
                if pieces is None:
                    g_ref[...] = gmap[js[0]]
                else:
                    if sum(w for _, w in pieces) != ins[i].shape[1]:
                        g_ref[...] = jnp.zeros(g_ref.shape, F32)
                    for j in js:
                        s, w = amap[j][1]
                        g_ref[:, s:s + w] = gmap[j]
            else:
                @pl.when(first)
                def _(g_ref=g_ref):
                    g_ref[...] = jnp.zeros(g_ref.shape, F32)

                g_ref[...] += gmap[js[0]]

    in_specs = _stage_in_specs(ins, specs, tb) + [pl.BlockSpec((tb, w), lambda i: (i, 0)) for w in out_widths]
    out_specs, out_shape = [], []
    for i in diff_inputs:
        a = ins[i]
        if specs[i][0] == 'row':
            out_specs.append(pl.BlockSpec((tb, a.shape[1]), lambda i: (i, 0)))
        else:
            out_specs.append(pl.BlockSpec(a.shape, lambda i: (0, 0)))
        out_shape.append(jax.ShapeDtypeStruct(a.shape, F32))
    return pl.pallas_call(
        body, name=name + '_b', grid=(T // tb,), in_specs=in_specs, out_specs=out_specs, out_shape=out_shape,
        compiler_params=_cparams(('arbitrary',)),
    )(*ins, *douts)


def stage_op(fn, specs, out_widths, name, tb=256):
    n = len(specs)
    diff_inputs = [i for i, s in enumerate(specs) if s[1]]

    @jax.custom_vjp
    def op(*ins):
        return tuple(_stage_fwd(fn, ins, specs, out_widths, name, tb))

    def op_fwd(*ins):
        return op(*ins), ins

    def op_bwd(ins, douts):
        gs = _stage_bwd(fn, ins, specs, out_widths, douts, name, tb)
        res = [None] * n
        for i, g in zip(diff_inputs, gs):
            res[i] = g
        return tuple(res)

    op.defvjp(op_fwd, op_bwd)
    return op


@jax.custom_vjp
def bdot(x, w):
    return jnp.dot(x.astype(BF16), w.astype(BF16), preferred_element_type=F32)


def _bdot_fwd(x, w):
    return bdot(x, w), (x, w)


def _bdot_bwd(res, dy):
    x, w = res
    dyb = dy.astype(BF16)
    dx = lax.dot_general(dyb, w.astype(BF16), (((1,), (1,)), ((), ())), preferred_element_type=F32)
    dw = lax.dot_general(x.astype(BF16), dyb, (((0,), (0,)), ((), ())), preferred_element_type=F32)
    return dx, dw


bdot.defvjp(_bdot_fwd, _bdot_bwd)


def _sdot_raw(x, c):
    hi = x.astype(BF16)
    r1 = x - hi.astype(F32)
    mid = r1.astype(BF16)
    lo = (r1 - mid.astype(F32)).astype(BF16)
    d = lambda u: jnp.dot(u, c, preferred_element_type=F32)
    return d(hi) + d(mid) + d(lo)


@jax.custom_vjp
def sdot(x, c, ct):
    return _sdot_raw(x, c)


def _sdot_fwd(x, c, ct):
    return _sdot_raw(x, c), (c, ct)


def _sdot_bwd(res, dy):
    c, ct = res
    return _sdot_raw(dy, ct), None, None


sdot.defvjp(_sdot_fwd, _sdot_bwd)


def _sigmoid(x):
    return 1.0 / (1.0 + jnp.exp(-x))


def _rms(x, g):
    return x * lax.rsqrt(jnp.mean(x * x, axis=-1, keepdims=True) + NORM_EPS) * g


def rmsnorm(x, g, name):
    op = stage_op(lambda xv, gv: (_rms(xv, gv),), [ROW(), FULL()], [x.shape[1]], name)
    return op(x, g.reshape(1, -1))[0]


def _mm_up_relu2(a, b, name):
    M, K = a.shape
    tn = b.shape[2]
    N = N_DEV * tn
    tm = _pick(M, (1024, 512, 256, 128))

    def body(a_ref, b_ref, u_ref, act_ref):
        u = jnp.dot(a_ref[...], b_ref[...], preferred_element_type=F32)
        r = jnp.maximum(u, 0.0)
        u_ref[...] = u.astype(BF16)
        act_ref[...] = (r * r).astype(BF16)

    out = pl.BlockSpec((tm, tn), lambda i, j: (i, j))
    sh = jax.ShapeDtypeStruct((M, N), BF16)
    return pl.pallas_call(
        body, name=name, grid=(M // tm, N // tn),
        in_specs=[pl.BlockSpec((tm, K), lambda i, j: (i, 0)), pl.BlockSpec((None, K, tn), lambda i, j: (j, 0, 0))],
        out_specs=[out, out], out_shape=[sh, sh], compiler_params=_cparams(('parallel', 'parallel')),
    )(a, b)


def _mm_down_bwd(dy, b, u, name):
    M, N = dy.shape
    K = b.shape[0]
    tm = _pick(M, (1024, 512, 256, 128))
    tk = _pick(K, (512, 256, 128))

    def body(dy_ref, b_ref, u_ref, du_ref):
        d = lax.dot_general(dy_ref[...].astype(BF16), b_ref[...], (((1,), (1,)), ((), ())),
                            preferred_element_type=F32)
        du_ref[...] = (d * (2.0 * jnp.maximum(u_ref[...].astype(F32), 0.0))).astype(BF16)

    blk = pl.BlockSpec((tm, tk), lambda i, j: (i, j))
    return pl.pallas_call(
        body, name=name, grid=(M // tm, K // tk),
        in_specs=[pl.BlockSpec((tm, N), lambda i, j: (i, 0)), pl.BlockSpec((tk, N), lambda i, j: (j, 0)), blk],
        out_specs=blk, out_shape=jax.ShapeDtypeStruct((M, K), BF16),
        compiler_params=_cparams(('parallel', 'parallel')),
    )(dy, b, u)


_RMS_SPECS = [ROW(), FULL()]
_rms_fn = lambda xv, gv: (_rms(xv, gv),)


@functools.partial(jax.custom_vjp, nondiff_argnums=(6,))
def mlp_block(x, g, wup_b, wup_c, wdown_b, wdown_c, name):
    return _mlp_fwd(x, g, wup_b, wup_c, wdown_b, wdown_c, name)[0]


def _mlp_fwd(x, g, wup_b, wup_c, wdown_b, wdown_c, name):
    h = _stage_fwd(_rms_fn, (x, g), _RMS_SPECS, [x.shape[1]], name + '_norm', 256, out_dtype=BF16)[0]
    u, act = _mm_up_relu2(h, wup_b, name + '_up')
    y = _mm_nn(act, wdown_b, add=x, name=name + '_down')
    return y, (x, g, h, u, act, wup_b, wdown_b)


def _mlp_bwd(name, res, dy):
    x, g, h, u, act, wup_b, wdown_b = res
    du = _mm_down_bwd(dy, wdown_b, u, name + '_down_da')
    dwdown = _mm_tn(act, dy, name=name + '_down_dw')
    dh = _mm_nt(du, wup_b, name=name + '_up_da')
    dwup = _mm_tn(h, du, name=name + '_up_dw', blocked=True)
    dx, dg = _stage_bwd(_rms_fn, (x, g), _RMS_SPECS, [x.shape[1]], (dh,), name + '_norm', 256)
    return dx + dy, dg, None, dwup, None, dwdown


mlp_block.defvjp(_mlp_fwd, _mlp_bwd)


def _shift_down(x, rows):
    return jnp.where(rows == 0, 0.0, pltpu.roll(x, 1, 0))


def _shift_up(x, rows, T):
    return jnp.where(rows == T - 1, 0.0, pltpu.roll(x, T - 1, 0))


def _tshift_fwd_call(x, mu, name):
    T, C = x.shape

    def body(x_ref, mu_ref, o_ref):
        xv = x_ref[...]
        rows = lax.broadcasted_iota(jnp.int32, xv.shape, 0)
        o_ref[...] = xv + (_shift_down(xv, rows) - xv) * mu_ref[...]

    return pl.pallas_call(
        body, name=name + '_f', grid=(C // LANES,),
        in_specs=[pl.BlockSpec((T, LANES), lambda j: (0, j)), pl.BlockSpec((1, LANES), lambda j: (0, j))],
        out_specs=pl.BlockSpec((T, LANES), lambda j: (0, j)), out_shape=jax.ShapeDtypeStruct((T, C), F32),
        compiler_params=_cparams(('parallel',)),
    )(x, mu)


def _tshift_bwd_call(x, mu, dy, name):
    T, C = x.shape

    def body(x_ref, mu_ref, dy_ref, dx_ref, dmu_ref):
        xv, d = x_ref[...], dy_ref[...]
        rows = lax.broadcasted_iota(jnp.int32, xv.shape, 0)
        z = d * mu_ref[...]
        dx_ref[...] = d - z + _shift_up(z, rows, T)
        dmu_ref[...] = jnp.sum(d * (_shift_down(xv, rows) - xv), axis=0, keepdims=True)

    return pl.pallas_call(
        body, name=name + '_b', grid=(C // LANES,),
        in_specs=[pl.BlockSpec((T, LANES), lambda j: (0, j)), pl.BlockSpec((1, LANES), lambda j: (0, j)),
                  pl.BlockSpec((T, LANES), lambda j: (0, j))],
        out_specs=[pl.BlockSpec((T, LANES), lambda j: (0, j)), pl.BlockSpec((1, LANES), lambda j: (0, j))],
        out_shape=[jax.ShapeDtypeStruct((T, C), F32), jax.ShapeDtypeStruct((1, C), F32)],
        compiler_params=_cparams(('parallel',)),
    )(x, mu, dy)


@functools.partial(jax.custom_vjp, nondiff_argnums=(2,))
def token_shift_mix(x, mu, name):
    return _tshift_fwd_call(x, mu, name)


def _tsm_fwd(x, mu, name):
    return _tshift_fwd_call(x, mu, name), (x, mu)


def _tsm_bwd(name, res, dy):
    x, mu = res
    dx, dmu = _tshift_bwd_call(x, mu, dy, name)
    return dx, dmu


token_shift_mix.defvjp(_tsm_fwd, _tsm_bwd)


def _conv_specs(T):
    nb = CONV_WIDTH // LANES
    return [pl.BlockSpec((T, LANES), lambda j: (0, j)), pl.BlockSpec((T, LANES), lambda j: (0, nb + j)),
            pl.BlockSpec((T, LANES), lambda j: (0, 2 * nb + j)), pl.BlockSpec((3, LANES), lambda j: (0, j))]


def _conv_fwd_call(cv, w, name):
    T = cv.shape[0]

    def body(b_ref, c_ref, x_ref, w_ref, o_ref):
        u = c_ref[...] * x_ref[...]
        rows = lax.broadcasted_iota(jnp.int32, u.shape, 0)
        u1 = _shift_down(u, rows)
        u2 = _shift_down(u1, rows)
        o_ref[...] = b_ref[...] * (w_ref[0:1, :] * u2 + w_ref[1:2, :] * u1 + w_ref[2:3, :] * u)

    return pl.pallas_call(
        body, name=name + '_f', grid=(CONV_WIDTH // LANES,), in_specs=_conv_specs(T),
        out_specs=pl.BlockSpec((T, LANES), lambda j: (0, j)),
        out_shape=jax.ShapeDtypeStruct((T, CONV_WIDTH), F32), compiler_params=_cparams(('parallel',)),
    )(cv, cv, cv, w)


def _conv_bwd_call(cv, w, do, name):
    T = cv.shape[0]

    def body(b_ref, c_ref, x_ref, w_ref, do_ref, db_ref, dc_ref, dx_ref, dw_ref):
        c, x, d = c_ref[...], x_ref[...], do_ref[...]
        u = c * x
        rows = lax.broadcasted_iota(jnp.int32, u.shape, 0)
        u1 = _shift_down(u, rows)
        u2 = _shift_down(u1, rows)
        w0, w1, w2 = w_ref[0:1, :], w_ref[1:2, :], w_ref[2:3, :]
        db_ref[...] = d * (w0 * u2 + w1 * u1 + w2 * u)
        dy = d * b_ref[...]
        dy1 = _shift_up(dy, rows, T)
        dy2 = _shift_up(dy1, rows, T)
        du = w2 * dy + w1 * dy1 + w0 * dy2
        dc_ref[...] = du * x
        dx_ref[...] = du * c
        dw_ref[0:1, :] = jnp.sum(dy * u2, axis=0, keepdims=True)
        dw_ref[1:2, :] = jnp.sum(dy * u1, axis=0, keepdims=True)
        dw_ref[2:3, :] = jnp.sum(dy * u, axis=0, keepdims=True)

    blk = pl.BlockSpec((T, LANES), lambda j: (0, j))
    sh = jax.ShapeDtypeStruct((T, CONV_WIDTH), F32)
    return pl.pallas_call(
        body, name=name + '_b', grid=(CONV_WIDTH // LANES,), in_specs=_conv_specs(T) + [blk],
        out_specs=[blk, blk, blk, pl.BlockSpec((3, LANES), lambda j: (0, j))],
        out_shape=[sh, sh, sh, jax.ShapeDtypeStruct((3, CONV_WIDTH), F32)],
        compiler_params=_cparams(('parallel',)),
    )(cv, cv, cv, w, do)


@functools.partial(jax.custom_vjp, nondiff_argnums=(2,))
def short_conv(cv, w, name):
    return _conv_fwd_call(cv, w, name)


def _sc_fwd(cv, w, name):
    return _conv_fwd_call(cv, w, name), (cv, w)


def _sc_bwd(name, res, do):
    cv, w = res
    db, dc, dx, dw = _conv_bwd_call(cv, w, do, name)
    return jnp.concatenate([db, dc, dx], axis=1), dw


short_conv.defvjp(_sc_fwd, _sc_bwd)


ATT_SCALE = QK_HEAD ** -0.5
NPAIR = MLA_HEADS // 2


def _att_bq(T):
    return min(256, T)


def _att_masks(pair, j):
    lane = lax.broadcasted_iota(jnp.int32, (1, LANES), 1)
    mask_n = (lane // QK_NOPE) == j
    mask_r = (lane // (QK_ROPE // 2)) == (2 * pair + j)
    return mask_n, mask_r


def _att_probs(qcat, kcat, row0, stop):
    s = lax.dot_general(qcat, kcat, (((1,), (1,)), ((), ())), preferred_element_type=F32) * ATT_SCALE
    r = row0 + lax.broadcasted_iota(jnp.int32, s.shape, 0)
    c = lax.broadcasted_iota(jnp.int32, s.shape, 1)
    s = jnp.where(c <= r, s, -jnp.inf)
    e = jnp.exp(s - jnp.max(s, axis=-1, keepdims=True))
    return e / jnp.sum(e, axis=-1, keepdims=True)


def _att_in_specs(T):
    blk = lambda f: pl.BlockSpec((T, LANES), f)
    return [blk(lambda p: (0, p)), blk(lambda p: (0, 0)), blk(lambda p: (0, 0)),
            blk(lambda p: (0, p)), blk(lambda p: (0, 0)), blk(lambda p: (0, 0)), blk(lambda p: (0, p))]


def _att_fwd_call(qn, q1, q2, kn, k1, k2, v, name):
    T = qn.shape[0]
    bq = _att_bq(T)

    def body(qn_ref, q1_ref, q2_ref, kn_ref, k1_ref, k2_ref, v_ref, o_ref):
        pair = pl.program_id(0)
        for i in range(T // bq):
            r0, stop = i * bq, (i + 1) * bq
            kcat = jnp.concatenate([kn_ref[0:stop, :], k1_ref[0:stop, :], k2_ref[0:stop, :]], axis=1).astype(BF16)
            vb = v_ref[0:stop, :].astype(BF16)
            outs = []
            for j in range(2):
                mask_n, mask_r = _att_masks(pair, j)
                qcat = jnp.concatenate([jnp.where(mask_n, qn_ref[r0:stop, :], 0.0),
                                        jnp.where(mask_r, q1_ref[r0:stop, :], 0.0),
                                        jnp.where(mask_r, q2_ref[r0:stop, :], 0.0)], axis=1).astype(BF16)
                p = _att_probs(qcat, kcat, r0, stop)
                outs.append(jnp.dot(p.astype(BF16), vb, preferred_element_type=F32))
            mask_n0, _ = _att_masks(pair, 0)
            o_ref[r0:stop, :] = jnp.where(mask_n0, outs[0], outs[1])

    return pl.pallas_call(
        body, name=name + '_f', grid=(NPAIR,), in_specs=_att_in_specs(T),
        out_specs=pl.BlockSpec((T, LANES), lambda p: (0, p)),
        out_shape=jax.ShapeDtypeStruct((T, MLA_HEADS * V_HEAD), F32), compiler_params=_cparams(('parallel',)),
    )(qn, q1, q2, kn, k1, k2, v)


def _att_bwd_call(qn, q1, q2, kn, k1, k2, v, o, do, name):
    T = qn.shape[0]
    bq = _att_bq(T)

    def body(qn_ref, q1_ref, q2_ref, kn_ref, k1_ref, k2_ref, v_ref, o_ref, do_ref,
             dqn_ref, dq1_ref, dq2_ref, dkn_ref, dk1_ref, dk2_ref, dv_ref, dk_acc, dv_acc):
        pair = pl.program_id(0)

        @pl.when(pair == 0)
        def _():
            dq1_ref[...] = jnp.zeros(dq1_ref.shape, F32)
            dq2_ref[...] = jnp.zeros(dq2_ref.shape, F32)
            dk1_ref[...] = jnp.zeros(dk1_ref.shape, F32)
            dk2_ref[...] = jnp.zeros(dk2_ref.shape, F32)

        dk_acc[...] = jnp.zeros(dk_acc.shape, F32)
        dv_acc[...] = jnp.zeros(dv_acc.shape, F32)
        for i in range(T // bq):
            r0, stop = i * bq, (i + 1) * bq
            kcat = jnp.concatenate([kn_ref[0:stop, :], k1_ref[0:stop, :], k2_ref[0:stop, :]], axis=1).astype(BF16)
            vb = v_ref[0:stop, :].astype(BF16)
            dqn = jnp.zeros((bq, LANES), F32)
            for j in range(2):
                mask_n, mask_r = _att_masks(pair, j)
                qcat = jnp.concatenate([jnp.where(mask_n, qn_ref[r0:stop, :], 0.0),
                                        jnp.where(mask_r, q1_ref[r0:stop, :], 0.0),
                                        jnp.where(mask_r, q2_ref[r0:stop, :], 0.0)], axis=1).astype(BF16)
                p = _att_probs(qcat, kcat, r0, stop)
                dom = jnp.where(mask_n, do_ref[r0:stop, :], 0.0)
                delta = jnp.sum(dom * o_ref[r0:stop, :], axis=-1, keepdims=True)
                domb = dom.astype(BF16)
                dp = lax.dot_general(domb, vb, (((1,), (1,)), ((), ())), preferred_element_type=F32)
                ds = (p * (dp - delta) * ATT_SCALE).astype(BF16)
                dqc = jnp.dot(ds, kcat, preferred_element_type=F32)
                dqn = dqn + jnp.where(mask_n, dqc[:, 0:LANES], 0.0)
                dq1_ref[r0:stop, :] += jnp.where(mask_r, dqc[:, LANES:2 * LANES], 0.0)
                dq2_ref[r0:stop, :] += jnp.where(mask_r, dqc[:, 2 * LANES:3 * LANES], 0.0)
                dk_acc[0:stop, :] += lax.dot_general(ds, qcat, (((0,), (0,)), ((), ())),
                                                     preferred_element_type=F32)
                dv_acc[0:stop, :] += lax.dot_general(p.astype(BF16), domb, (((0,), (0,)), ((), ())),
                                                     preferred_element_type=F32)
            dqn_ref[r0:stop, :] = dqn
        dkn_ref[...] = dk_acc[:, 0:LANES]
        dk1_ref[...] += dk_acc[:, LANES:2 * LANES]
        dk2_ref[...] += dk_acc[:, 2 * LANES:3 * LANES]
        dv_ref[...] = dv_acc[...]

    per_pair = pl.BlockSpec((T, LANES), lambda p: (0, p))
    shared = pl.BlockSpec((T, LANES), lambda p: (0, 0))
    wide = jax.ShapeDtypeStruct((T, MLA_HEADS * QK_NOPE), F32)
    narrow = jax.ShapeDtypeStruct((T, LANES), F32)
    return pl.pallas_call(
        body, name=name + '_b', grid=(NPAIR,), in_specs=_att_in_specs(T) + [per_pair, per_pair],
        out_specs=[per_pair, shared, shared, per_pair, shared, shared, per_pair],
        out_shape=[wide, narrow, narrow, wide, narrow, narrow, wide],
        scratch_shapes=[pltpu.VMEM((T, 3 * LANES), F32), pltpu.VMEM((T, LANES), F32)],
        compiler_params=_cparams(('arbitrary',)),
    )(qn, q1, q2, kn, k1, k2, v, o, do)


@functools.partial(jax.custom_vjp, nondiff_argnums=(7,))
def attention(qn, q1, q2, kn, k1, k2, v, name):
    return _att_fwd_call(qn, q1, q2, kn, k1, k2, v, name)


def _attn_fwd(qn, q1, q2, kn, k1, k2, v, name):
    o = _att_fwd_call(qn, q1, q2, kn, k1, k2, v, name)
    return o, (qn, q1, q2, kn, k1, k2, v, o)


def _attn_bwd(name, res, do):
    return tuple(_att_bwd_call(*res, do, name))


attention.defvjp(_attn_fwd, _attn_bwd)


SCAN_CHUNK = 64
SCAN_UNROLL = 4


def _block_ones(n, seg):
    i = np.arange(n)
    return (i[:, None] // seg == i[None, :] // seg).astype(np.float32)


def _scan_diag():
    i = np.arange(RW_WIDTH)
    return jnp.asarray((np.arange(RW_N)[:, None] == (i[None, :] % RW_N)).astype(np.float32))


def _head_rowsum(x):
    low = lax.broadcasted_iota(jnp.int32, (1, LANES), 1) < RW_N
    tiles = []
    for j in range(RW_WIDTH // LANES):
        xt = x[:, j * LANES:(j + 1) * LANES]
        x0 = jnp.where(low, xt, 0.0)
        s0 = jnp.sum(x0, axis=-1, keepdims=True)
        s1 = jnp.sum(xt - x0, axis=-1, keepdims=True)
        tiles.append(jnp.where(low, s0, s1))
    return jnp.concatenate(tiles, axis=1)


def _unrolled_loop(n, step, init):
    def body(i, carry):
        for j in range(SCAN_UNROLL):
            carry = step(i * SCAN_UNROLL + j, carry)
        return carry
    return lax.fori_loop(0, n // SCAN_UNROLL, body, init)


def _scan_fwd_call(r, w, k, v, a, b, name):
    T = r.shape[0]
    tc = min(SCAN_CHUNK, T)
    dg = _scan_diag()

    def body(r_ref, w_ref, k_ref, v_ref, a_ref, b_ref, dg_ref, y_ref, st_ref, sa_ref, vc_ref, s_ref):
        @pl.when(pl.program_id(0) == 0)
        def _():
            s_ref[...] = jnp.zeros(s_ref.shape, F32)

        dgv = dg_ref[...]
        readout = lambda s, t: jnp.sum(_head_rowsum(s * r_ref[t]) * dgv, axis=0, keepdims=True)

        def step(t, carry):
            s, vcol = carry
            st_ref[t] = s
            vc_ref[t] = vcol
            sa = _head_rowsum(s * a_ref[t])
            sa_ref[t] = sa
            prev = jnp.maximum(t - 1, 0)
            y_ref[prev] = readout(s, prev)
            vcol_next = _head_rowsum(dgv * v_ref[jnp.minimum(t + 1, tc - 1)])
            sn = s * w_ref[t] + sa * b_ref[t] + vcol * k_ref[t]
            return sn, vcol_next

        s_end, _ = _unrolled_loop(tc, step, (s_ref[...], _head_rowsum(dgv * v_ref[0])))
        y_ref[tc - 1] = readout(s_end, tc - 1)
        s_ref[...] = s_end

    vec = pl.BlockSpec((tc, 1, RW_WIDTH), lambda i: (i, 0, 0))
    mat = pl.BlockSpec((tc, RW_N, RW_WIDTH), lambda i: (i, 0, 0))
    msh = jax.ShapeDtypeStruct((T, RW_N, RW_WIDTH), F32)
    return pl.pallas_call(
        body, name=name + '_f', grid=(T // tc,),
        in_specs=[vec] * 6 + [pl.BlockSpec((RW_N, RW_WIDTH), lambda i: (0, 0))],
        out_specs=[vec, mat, mat, mat],
        out_shape=[jax.ShapeDtypeStruct((T, 1, RW_WIDTH), F32), msh, msh, msh],
        scratch_shapes=[pltpu.VMEM((RW_N, RW_WIDTH), F32)],
        compiler_params=_cparams(('arbitrary',)),
    )(r, w, k, v, a, b, dg)


def _scan_bwd_call(r, w, k, a, b, st, sa_all, vc_all, dy, name):
    T = r.shape[0]
    tc = min(SCAN_CHUNK, T)
    nt = T // tc
    dg = _scan_diag()

    def body(r_ref, w_ref, k_ref, a_ref, b_ref, st_ref, sa_ref, vc_ref, dy_ref, dg_ref,
             dr_ref, dw_ref, dk_ref, dv_ref, da_ref, db_ref, ds_ref):
        @pl.when(pl.program_id(0) == 0)
        def _():
            ds_ref[...] = jnp.zeros(ds_ref.shape, F32)

        dgv = dg_ref[...]
        colsum = lambda x: jnp.sum(x, axis=0, keepdims=True)

        def step(i, carry):
            ds, dycol = carry
            t = tc - 1 - i
            sp = st_ref[t]
            rt, wt, kt, at, bt = r_ref[t], w_ref[t], k_ref[t], a_ref[t], b_ref[t]
            ds = ds + dycol * rt
            dsa = _head_rowsum(ds * bt)
            sa, vcol = sa_ref[t], vc_ref[t]
            dycol_next = _head_rowsum(dgv * dy_ref[jnp.maximum(t - 1, 0)])
            sn = sp * wt + sa * bt + vcol * kt
            dr_ref[t] = colsum(sn * dycol)
            dk_ref[t] = colsum(ds * vcol)
            db_ref[t] = colsum(ds * sa)
            dw_ref[t] = colsum(ds * sp)
            dv_ref[t] = colsum(_head_rowsum(ds * kt) * dgv)
            da_ref[t] = colsum(sp * dsa)
            return ds * wt + dsa * at, dycol_next

        ds_end, _ = _unrolled_loop(tc, step, (ds_ref[...], _head_rowsum(dgv * dy_ref[tc - 1])))
        ds_ref[...] = ds_end

    vec = pl.BlockSpec((tc, 1, RW_WIDTH), lambda i: (nt - 1 - i, 0, 0))
    mat = pl.BlockSpec((tc, RW_N, RW_WIDTH), lambda i: (nt - 1 - i, 0, 0))
    vsh = jax.ShapeDtypeStruct((T, 1, RW_WIDTH), F32)
    return pl.pallas_call(
        body, name=name + '_b', grid=(nt,),
        in_specs=[vec] * 5 + [mat] * 3 + [vec, pl.BlockSpec((RW_N, RW_WIDTH), lambda i: (0, 0))],
        out_specs=[vec] * 6, out_shape=[vsh] * 6,
        scratch_shapes=[pltpu.VMEM((RW_N, RW_WIDTH), F32)],
        compiler_params=_cparams(('arbitrary',)),
    )(r, w, k, a, b, st, sa_all, vc_all, dy, dg)


@functools.partial(jax.custom_vjp, nondiff_argnums=(6,))
def wkv7(r, w, k, v, a, b, name):
    return _scan_fwd_call(r, w, k, v, a, b, name)[0]


def _wkv7_fwd(r, w, k, v, a, b, name):
    y, st, sa_all, vc_all = _scan_fwd_call(r, w, k, v, a, b, name)
    return y, (r, w, k, a, b, st, sa_all, vc_all)


def _wkv7_bwd(name, res, dy):
    return tuple(_scan_bwd_call(*res, dy, name))


wkv7.defvjp(_wkv7_fwd, _wkv7_bwd)


def _np_bf16(a):
    return jnp.asarray(a, BF16)


def _mla_consts():
    seg_n = (np.arange(512)[:, None] // QK_NOPE == np.arange(LANES)[None, :]).astype(np.float32)
    seg_r = (np.arange(LANES)[:, None] // 16 == np.arange(LANES)[None, :]).astype(np.float32)
    e1 = np.zeros((LANES, LANES), np.float32)
    e2 = np.zeros((LANES, LANES), np.float32)
    for h in range(MLA_HEADS):
        for i in range(16):
            e1[i, h * 16 + i] = 1.0
            e2[16 + i, h * 16 + i] = 1.0
    mats = [seg_n, seg_n.T, seg_r, seg_r.T, e1, e1.T, e2, e2.T]
    return [_np_bf16(m) for m in mats]


def _qk_prep_fn(qn, q1, q2, kn, kx, cos, sin, gqn, gq1, gq2, gkn, gk1, gk2,
                seg_n, seg_nt, seg_r, seg_rt, e1, e1t, e2, e2t):
    def normrope(xn, x1, x2, gn, g1, g2):
        ss = sdot(xn * xn, seg_n, seg_nt) + sdot(x1 * x1, seg_r, seg_rt) + sdot(x2 * x2, seg_r, seg_rt)
        inv = lax.rsqrt(ss * (1.0 / QK_HEAD) + NORM_EPS)
        inv_n = sdot(inv, seg_nt, seg_n)
        inv_r = sdot(inv, seg_rt, seg_r)
        y1 = x1 * inv_r * g1
        y2 = x2 * inv_r * g2
        return xn * inv_n * gn, y1 * cos - y2 * sin, y1 * sin + y2 * cos

    k1 = sdot(kx, e1, e1t)
    k2 = sdot(kx, e2, e2t)
    return normrope(qn, q1, q2, gqn, gq1, gq2) + normrope(kn, k1, k2, gkn, gk1, gk2)


def _rwkv_prep_fn(vres):
    def fn(r, k, v, xg, xwa, kx, *rest):
        if vres:
            vfirst, w0, a0, k_k, k_a, w2p, a2p, g2, v0, v2p, bm = rest
        else:
            w0, a0, k_k, k_a, w2p, a2p, g2, bm = rest
        z = w0 + bdot(jnp.tanh(xwa), w2p)
        nz = -z
        softplus = jnp.maximum(nz, 0.0) + jnp.log(1.0 + jnp.exp(-jnp.abs(nz)))
        decay = jnp.exp(-jnp.exp(-softplus - 0.5))
        a = _sigmoid(a0 + bdot(xwa, a2p))
        g = bdot(_sigmoid(xg), g2)
        if vres:
            vv = v + (vfirst - v) * _sigmoid(v0 + bdot(kx, v2p))
        else:
            vv = v
        kkr = k * k_k
        kk = kkr / jnp.maximum(jnp.sqrt(sdot(kkr * kkr, bm, bm)), 1e-12)
        k2 = k * (1.0 + (a - 1.0) * k_a)
        return r * 1.0, decay, k2, vv, -kk, kk * a, g
    return fn


def _rwkv_post_fn(y, r, k2, vv, g, ln_w, ln_b, rk, bm):
    inv_n = 1.0 / RW_N
    mean = sdot(y, bm, bm) * inv_n
    yc = y - mean
    var = sdot(yc * yc, bm, bm) * inv_n
    yn = yc * lax.rsqrt(var + GN_EPS) * ln_w + ln_b
    bonus = sdot(r * k2 * rk, bm, bm) * vv
    return ((yn + bonus) * g,)


def _merge_fn(g0, g1, g2, oa, ob, oc):
    return (_sigmoid(g0) * oa + _sigmoid(g1) * ob + _sigmoid(g2) * oc,)


def _loss_call(y, target):
    T, C = y.shape
    tb = min(256, T)

    def body(y_ref, t_ref, dy_ref, part_ref):
        err = y_ref[...] - t_ref[...]
        dy_ref[...] = err * (1.0 / C)
        sq = jnp.sum(err * err, axis=0, keepdims=True)
        acc = sq[:, 0:LANES]
        for j in range(1, C // LANES):
            acc = acc + sq[:, j * LANES:(j + 1) * LANES]
        part_ref[...] = jnp.zeros(part_ref.shape, F32)
        part_ref[0:1, :] = acc * (0.5 / C)

    return pl.pallas_call(
        body, name='loss', grid=(T // tb,),
        in_specs=[pl.BlockSpec((tb, C), lambda i: (i, 0))] * 2,
        out_specs=[pl.BlockSpec((tb, C), lambda i: (i, 0)), pl.BlockSpec((8, LANES), lambda i: (i, 0))],
        out_shape=[jax.ShapeDtypeStruct((T, C), F32), jax.ShapeDtypeStruct((8 * (T // tb), LANES), F32)],
        compiler_params=_cparams(('parallel',)),
    )(y, target)


def _pad_rows(t, before, total):
    return jnp.pad(t, ((before, total - before - t.shape[0]), (0, 0)))


def _head_tile(g, lo, hi):
    return jnp.tile(g[lo:hi], MLA_HEADS).reshape(1, -1)


def _layer(l, x, v_first, wd, sp, cos, sin):
    T = x.shape[0]
    nm = f'l{l}'
    vres = l > 0
    w_in = wd['w_in']
    if vres:
        v1t = wd['rwkv_v1'].map(lambda t: t.T)
    else:
        v1t = W(jnp.zeros((MV_LORA, D_MODEL), BF16), jnp.zeros((MV_LORA, D_MODEL), BF16))
    zpad = W(jnp.zeros((64, D_MODEL), BF16), jnp.zeros((64, D_MODEL), BF16))
    w_rw = _wcat([w_in['rkv'], w_in['xg'], w_in['xwa'], w_in['kpe'], v1t, zpad], 0)
    h = rmsnorm(x, sp['attn_norm'], nm + '_anorm')
    gates, cq, ckv, rw, cv = multi_linear_t(
        h, [w_in['gates'], w_in['cq'], w_in['ckv'], w_rw, w_in['conv']], nm + '_win')

    v_mu = sp['rwkv_v_mu'] if vres else jnp.zeros((MV_LORA,), F32)
    mu_all = jnp.concatenate([sp['rwkv_mu'][0:768], sp['rwkv_mu'][896:1024], sp['rwkv_mu'][768:896],
                              jnp.zeros((QK_ROPE,), F32), v_mu, jnp.zeros((64,), F32)]).reshape(1, -1)
    rws = token_shift_mix(rw, mu_all, nm + '_shift')

    cqn = rmsnorm(cq, sp['mla_q_a_norm'], nm + '_qan')
    ckvn = rmsnorm(ckv, sp['mla_kv_a_norm'], nm + '_kvan')
    wq = wd['mla_wq_b'].map(lambda t: jnp.concatenate(
        [t.reshape(Q_LORA, MLA_HEADS, QK_HEAD)[:, :, 0:64].reshape(Q_LORA, 512),
         t.reshape(Q_LORA, MLA_HEADS, QK_HEAD)[:, :, 64:80].reshape(Q_LORA, 128),
         t.reshape(Q_LORA, MLA_HEADS, QK_HEAD)[:, :, 80:96].reshape(Q_LORA, 128)], axis=1))
    wkn = wd['mla_wkv_b'].map(lambda t: t.reshape(KV_LORA, MLA_HEADS, 128)[:, :, 0:64].reshape(KV_LORA, 512))
    wv = wd['mla_wkv_b'].map(lambda t: t.reshape(KV_LORA, MLA_HEADS, 128)[:, :, 64:128].reshape(KV_LORA, 512))
    q = linear(cqn, wq, name=nm + '_wq')
    kn, vv_att = multi_linear(ckvn, [wkn, wv], nm + '_wkv')
    gq, gk = sp['mla_q_norm'], sp['mla_k_norm']
    consts = _mla_consts()
    qk_specs = ([ROW(pieces=((0, 512), (512, 128), (640, 128))), ROW(), ROW(pieces=((1024, 128),)),
                 ROW(False), ROW(False)] + [FULL()] * 6 + [FULL(False)] * 8)
    qk_op = stage_op(_qk_prep_fn, qk_specs, [512, 128, 128, 512, 128, 128], nm + '_qkprep')
    Qn, Q1, Q2, Kn, K1, K2 = qk_op(q, kn, rws, cos, sin,
                                   _head_tile(gq, 0, 64), _head_tile(gq, 64, 80), _head_tile(gq, 80, 96),
                                   _head_tile(gk, 0, 64), _head_tile(gk, 64, 80), _head_tile(gk, 80, 96), *consts)
    o_att = attention(Qn, Q1, Q2, Kn, K1, K2, vv_att, nm + '_att')
    o_a = linear(o_att, wd['mla_w_o'], name=nm + '_wo')

    bm = _np_bf16(_block_ones(RW_WIDTH, RW_N))
    vec = lambda n: sp[n].reshape(1, -1)
    f32w = lambda n: wd[n].c + wd[n].b.astype(F32)
    w2p = _pad_rows(f32w('rwkv_w2'), 0, 128)
    a2p = _pad_rows(f32w('rwkv_a2'), 64, 128)
    g2 = f32w('rwkv_g2')
    rw_pieces = ((0, 256), (256, 256), (512, 256), (768, 128), (896, 128), (1024, 128))
    if vres:
        v2p = _pad_rows(f32w('rwkv_v2'), 32, 128)
        prep_specs = [ROW(pieces=rw_pieces), ROW()] + [FULL()] * 9 + [FULL(False)]
        prep_in = [rws, v_first, vec('rwkv_w0'), vec('rwkv_a0'), vec('rwkv_k_k'), vec('rwkv_k_a'), w2p, a2p, g2,
                   vec('rwkv_v0'), v2p, bm]
    else:
        prep_specs = [ROW(pieces=rw_pieces)] + [FULL()] * 7 + [FULL(False)]
        prep_in = [rws, vec('rwkv_w0'), vec('rwkv_a0'), vec('rwkv_k_k'), vec('rwkv_k_a'), w2p, a2p, g2, bm]
    prep_op = stage_op(_rwkv_prep_fn(vres), prep_specs, [256] * 7, nm + '_rwprep')
    r_, dec, k2, vv, an, bn, g = prep_op(*prep_in)
    if not vres:
        v_first = vv
    t3 = lambda t: t.reshape(T, 1, RW_WIDTH)
    y = wkv7(t3(r_), t3(dec), t3(k2), t3(vv), t3(an), t3(bn), nm + '_scan').reshape(T, RW_WIDTH)
    post_op = stage_op(_rwkv_post_fn, [ROW()] * 5 + [FULL()] * 3 + [FULL(False)], [256], nm + '_rwpost')
    yb = post_op(y, r_, k2, vv, g, vec('rwkv_ln_w'), vec('rwkv_ln_b'), sp['rwkv_r_k'].reshape(1, -1), bm)[0]
    o_b = linear(yb, wd['rwkv_w_o'], name=nm + '_rwo')

    oc_in = short_conv(cv, f32w('conv_w'), nm + '_conv')
    o_c = linear(oc_in, wd['conv_w_o'], name=nm + '_cwo')

    merge_op = stage_op(_merge_fn, [ROW(pieces=((0, 1024), (1024, 1024), (2048, 1024))), ROW(), ROW(), ROW()],
                        [D_MODEL], nm + '_merge')
    merged = merge_op(gates, o_a, o_b, o_c)[0]
    x2 = linear(merged, wd['w_out'], add=x, name=nm + '_wout')
    return x2, v_first


def _mlp(l, x, wd, sp):
    return mlp_block(x, sp['mlp_norm'].reshape(1, -1), wd['w_up'].b, wd['w_up'].c, wd['w_down'].b, wd['w_down'].c,
                     f'l{l}_mlp')


MLP_WEIGHTS = ('w_up', 'w_down')


def _entries(ml):
    out = []
    for name, (layers, shape, axis) in SHARDED.items():
        l = ml if layers == DEPTH else ml - 1
        if not 0 <= l < layers or name in MLP_WEIGHTS:
            continue
        n = shape[0] * shape[1] // N_DEV
        if name == 'conv_w':
            out.append(('conv_w_hi', name, l, n))
            out.append(('conv_w_lo', name, l, n))
        else:
            out.append((name, name, l, n))
    return out


def _slot_size(n):
    return -(-n // LANES) * LANES


def _pack_rows(ml):
    total = sum(_slot_size(n) for _, _, _, n in _entries(ml))
    rows = -(-total // LANES)
    return -(-rows // PACK_ROW_MULT) * PACK_ROW_MULT


def _pack_flat(pieces, ml):
    rows = _pack_rows(ml)
    padded = []
    for p, (_, _, _, n) in zip(pieces, _entries(ml)):
        pad = _slot_size(n) - n
        if pad:
            p = jnp.pad(p, [(0, 0)] * (p.ndim - 1) + [(0, pad)])
        padded.append(p)
    flat = jnp.concatenate(padded, axis=-1)
    tail = rows * LANES - flat.shape[-1]
    if tail:
        flat = jnp.pad(flat, [(0, 0)] * (flat.ndim - 1) + [(0, tail)])
    return flat.reshape(flat.shape[:-1] + (rows, LANES))


def _unpack_flat(buf, ml):
    out, row = [], 0
    for _, _, _, n in _entries(ml):
        nrows = _slot_size(n) // LANES
        piece = buf[..., row:row + nrows, :].reshape(buf.shape[:-2] + (-1,))
        out.append(piece[..., :n])
        row += nrows
    return out


def _pack_shards(shards, ml, dtype, split_conv):
    pieces = []
    for slot, name, l, n in _entries(ml):
        a = shards[name][l]
        if name in TRANSPOSED:
            a = a.astype(dtype).T
        a = a.reshape(-1)
        if slot == 'conv_w_hi':
            a = a.astype(BF16).astype(F32) if split_conv else a
        elif slot == 'conv_w_lo':
            a = (a - a.astype(BF16).astype(F32)) if split_conv else jnp.zeros_like(a)
        pieces.append(a.astype(dtype))
    return _pack_flat(pieces, ml)


def _unpack_shards(buf, ml):
    out = {}
    for (slot, name, l, n), v in zip(_entries(ml), _unpack_flat(buf, ml)):
        if slot == 'conv_w_lo':
            continue
        layers, shape, axis = SHARDED[name]
        sshape = (shape[0] // N_DEV, shape[1]) if axis == 0 else (shape[0], shape[1] // N_DEV)
        out[(name, l)] = v.reshape(sshape[::-1]).T if name in TRANSPOSED else v.reshape(sshape)
    return out


def _to_full(blocks, shape, axis):
    if axis == 0:
        return blocks.reshape(shape)
    return blocks.reshape(N_DEV, shape[0], shape[1] // N_DEV).transpose(1, 0, 2).reshape(shape)


def _to_blocks(full, axis):
    r, c = full.shape
    if axis == 0:
        return full.reshape(N_DEV, -1)
    return full.reshape(r, N_DEV, c // N_DEV).transpose(1, 0, 2).reshape(N_DEV, -1)


def _unpack_gathered(gathered, ml):
    out, conv_hi = {}, None
    for (slot, name, l, n), v in zip(_entries(ml), _unpack_flat(gathered, ml)):
        layers, shape, axis = SHARDED[name]
        if name in TRANSPOSED:
            out[name] = v.reshape(-1, 16, LANES)
            continue
        full = _to_full(v, shape, axis)
        if slot == 'conv_w_hi':
            conv_hi = full
        elif slot == 'conv_w_lo':
            out[name] = conv_hi.astype(F32) + full.astype(F32)
        else:
            out[name] = full
    return out


def _pack_grads(grads, ml):
    pieces = []
    for slot, name, l, n in _entries(ml):
        if name in TRANSPOSED:
            blocks = jnp.concatenate(grads[name], axis=0).reshape(N_DEV, -1)
        else:
            blocks = _to_blocks(grads[name], SHARDED[name][2])
        if slot == 'conv_w_lo':
            blocks = jnp.zeros_like(blocks)
        pieces.append(blocks.astype(BF16))
    return _pack_flat(pieces, ml)


def _my_pos():
    return lax.axis_index('x'), lax.axis_index('y'), lax.axis_index('c')


def _flip(v, bit):
    return 1 - v if bit else v


def all_gather_blocks(x):
    rows = x.shape[0]

    def body(x_ref, out_ref, send_sems, recv_sems, local_sem):
        mx, my, mc = _my_pos()
        me, sibling = (mx, my, mc), (mx, my, 1 - mc)
        chips = [(1 - mx, my), (mx, 1 - my), (1 - mx, 1 - my)]

        def block(px, py, pc):
            return out_ref.at[4 * px + 2 * py + pc]

        def copy(k, blk, to, src=None):
            return pltpu.make_async_remote_copy(
                src_ref=block(*blk) if src is None else src, dst_ref=block(*blk),
                send_sem=send_sems.at[k], recv_sem=recv_sems.at[k], device_id=to, device_id_type=MESH)

        mine = pltpu.make_async_copy(x_ref, block(*me), local_sem)
        mine.start()
        first = [copy(0, me, sibling, src=x_ref)]
        first += [copy(1 + j, me, (*chip, mc), src=x_ref) for j, chip in enumerate(chips)]
        for cp in first:
            cp.start()
        passed = [copy(4 + j, (*chip, mc), sibling) for j, chip in enumerate(chips)]
        for j, chip in enumerate(chips):
            copy(1 + j, (*chip, mc), me).wait_recv()
            passed[j].start()
        copy(0, sibling, me).wait_recv()
        for j, chip in enumerate(chips):
            copy(4 + j, (*chip, 1 - mc), me).wait_recv()
        for cp in first + passed:
            cp.wait_send()
        mine.wait()

    return pl.pallas_call(
        body, name='all_gather_weights',
        out_shape=jax.ShapeDtypeStruct((N_DEV, rows, LANES), x.dtype),
        in_specs=[pl.BlockSpec(memory_space=pl.ANY)], out_specs=pl.BlockSpec(memory_space=pl.ANY),
        scratch_shapes=[pltpu.SemaphoreType.DMA((7,)), pltpu.SemaphoreType.DMA((7,)), pltpu.SemaphoreType.DMA],
    )(x)


HBM_SPEC = pl.BlockSpec(memory_space=pltpu.HBM)
SEM_SPEC = pl.BlockSpec(memory_space=pltpu.SEMAPHORE)
DATAFLOW_EFFECT = pltpu.SideEffectType.DATAFLOW_SIDE_EFFECTING


def _direct_copies(src_ref, land_ref, send_sems, recv_sems, per_peer):
    mx, my, mc = _my_pos()
    me = 4 * mx + 2 * my + mc
    copies = []
    for k in range(1, N_DEV):
        peer = (_flip(mx, k & 4), _flip(my, k & 2), _flip(mc, k & 1))
        pidx = 4 * peer[0] + 2 * peer[1] + peer[2]
        copies.append(pltpu.make_async_remote_copy(
            src_ref=src_ref.at[pidx] if per_peer else src_ref, dst_ref=land_ref.at[me],
            send_sem=send_sems.at[k - 1], recv_sem=recv_sems.at[k - 1], device_id=peer, device_id_type=MESH))
    return copies


def send_start(src, per_peer, name):
    block = src.shape[1:] if per_peer else src.shape
    land_shape = (N_DEV,) + tuple(block)

    def body(src_ref, land_ref, send_sems, recv_sems, src_thru, land_thru, token):
        for cp in _direct_copies(src_ref, land_ref, send_sems, recv_sems, per_peer):
            cp.start()
        token[...] = jnp.zeros(token.shape, F32)

    send_sems, recv_sems, src_thru, land_thru, token = pl.pallas_call(
        body, name=name,
        out_shape=(pltpu.SemaphoreType.DMA((N_DEV - 1,)), pltpu.SemaphoreType.DMA((N_DEV - 1,)),
                   pltpu.HBM(src.shape, src.dtype), pltpu.HBM(land_shape, src.dtype),
                   jax.ShapeDtypeStruct((8, LANES), F32)),
        in_specs=(HBM_SPEC, HBM_SPEC),
        out_specs=(SEM_SPEC, SEM_SPEC, HBM_SPEC, HBM_SPEC, pl.BlockSpec(memory_space=pltpu.VMEM)),
        input_output_aliases={0: 2, 1: 3},
        compiler_params=pltpu.CompilerParams(has_side_effects=DATAFLOW_EFFECT),
    )(pltpu.with_memory_space_constraint(src, pltpu.HBM),
      pltpu.with_memory_space_constraint(lax.empty(land_shape, src.dtype), pltpu.HBM))
    return (send_sems, recv_sems, src_thru, land_thru), token[0, 0]


def send_wait(handles, after, per_peer, name):
    send_sems, recv_sems, src_thru, land_thru = handles

    def body(src_ref, land_ref, send_sems, recv_sems, after_ref, src_dead, got_ref):
        for cp in _direct_copies(src_ref, land_ref, send_sems, recv_sems, per_peer):
            cp.wait_send()
            cp.wait_recv()

    return pl.pallas_call(
        body, name=name,
        out_shape=(pltpu.HBM(src_thru.shape, src_thru.dtype), pltpu.HBM(land_thru.shape, land_thru.dtype)),
        in_specs=(HBM_SPEC, HBM_SPEC, SEM_SPEC, SEM_SPEC, pl.BlockSpec(memory_space=pl.ANY)),
        out_specs=(HBM_SPEC, HBM_SPEC), input_output_aliases={0: 0, 1: 1},
        compiler_params=pltpu.CompilerParams(has_side_effects=DATAFLOW_EFFECT),
    )(src_thru, land_thru, send_sems, recv_sems, after)[1]


def _adamw_math(w, g, m, v):
    m2 = ADAM_B1 * m + (1.0 - ADAM_B1) * g
    v2 = ADAM_B2 * v + (1.0 - ADAM_B2) * (g * g)
    m_hat = m2 / (1.0 - ADAM_B1 ** ADAM_STEP)
    v_hat = v2 / (1.0 - ADAM_B2 ** ADAM_STEP)
    delta = -ADAM_LR * (m_hat / (jnp.sqrt(v_hat) + ADAM_EPS) + ADAM_WD * w)
    return delta, m2, v2


def sum_parts(parts, name):
    _, rows, cols = parts.shape
    rb = rows
    while N_DEV * rb * cols * 2 > (2 << 20) and rb % 32 == 0:
        rb //= 2

    def body(p_ref, g_ref):
        g = p_ref[0].astype(F32)
        for j in range(1, N_DEV):
            g = g + p_ref[j].astype(F32)
        g_ref[...] = g

    return pl.pallas_call(
        body, name=name, grid=(rows // rb,),
        in_specs=[pl.BlockSpec((N_DEV, rb, cols), lambda i: (0, i, 0))],
        out_specs=pl.BlockSpec((rb, cols), lambda i: (i, 0)),
        out_shape=jax.ShapeDtypeStruct((rows, cols), F32), compiler_params=_cparams(('parallel',)),
    )(parts)


ADAMW_BLOCK_BYTES = 1 << 20


def adamw_weight(name, w, m, v, grads):
    layers, a, b = w.shape
    ra = a
    while ra * b * 4 > ADAMW_BLOCK_BYTES and ra % 16 == 0:
        ra //= 2

    def body(*refs):
        w_ref, m_ref, v_ref = refs[:3]
        g_refs = refs[3:3 + layers]
        g_ref, d_ref, m2_ref, v2_ref = refs[3 + layers:]
        g = g_refs[0][...]
        for l in range(1, layers):
            g = jnp.where(pl.program_id(0) == l, g_refs[l][...], g)
        delta, m2, v2 = _adamw_math(w_ref[0], g, m_ref[0], v_ref[0])
        g_ref[0] = g
        d_ref[0] = delta
        m2_ref[0] = m2
        v2_ref[0] = v2

    blk = pl.BlockSpec((1, ra, b), lambda l, i: (l, i, 0))
    gblk = pl.BlockSpec((ra, b), lambda l, i: (i, 0))
    sh = jax.ShapeDtypeStruct(w.shape, F32)
    return pl.pallas_call(
        body, name='adamw_' + name, grid=(layers, a // ra), in_specs=[blk] * 3 + [gblk] * layers,
        out_specs=[blk] * 4, out_shape=[sh] * 4, compiler_params=_cparams(('parallel', 'parallel')),
    )(w, m, v, *grads)


def allreduce_adamw_small(g, w, m, v):
    rows = g.shape[0]

    def body(g_ref, w_ref, m_ref, v_ref, gs_ref, d_ref, m2_ref, v2_ref, all_ref, send_sems, recv_sems):
        mx, my, mc = _my_pos()
        me, sibling = (mx, my, mc), (mx, my, 1 - mc)
        chips = [(1 - mx, my), (mx, 1 - my), (1 - mx, 1 - my)]

        def block(px, py, pc):
            return all_ref.at[4 * px + 2 * py + pc]

        def copy(k, blk, to, src=None):
            return pltpu.make_async_remote_copy(
                src_ref=block(*blk) if src is None else src, dst_ref=block(*blk),
                send_sem=send_sems.at[k], recv_sem=recv_sems.at[k], device_id=to, device_id_type=MESH)

        first = [copy(0, me, sibling, src=g_ref)]
        first += [copy(1 + j, me, (*chip, mc), src=g_ref) for j, chip in enumerate(chips)]
        for cp in first:
            cp.start()
        passed = [copy(4 + j, (*chip, mc), sibling) for j, chip in enumerate(chips)]
        for j, chip in enumerate(chips):
            copy(1 + j, (*chip, mc), me).wait_recv()
            passed[j].start()
        copy(0, sibling, me).wait_recv()
        for j, chip in enumerate(chips):
            copy(4 + j, (*chip, 1 - mc), me).wait_recv()
        for cp in first + passed:
            cp.wait_send()
        my_idx = 4 * mx + 2 * my + mc
        total = jnp.zeros((rows, LANES), F32)
        for j in range(N_DEV):
            total = total + jnp.where(my_idx == j, g_ref[...], all_ref[j])
        delta, m2, v2 = _adamw_math(w_ref[...], total, m_ref[...], v_ref[...])
        gs_ref[...] = total
        d_ref[...] = delta
        m2_ref[...] = m2
        v2_ref[...] = v2

    vm = pl.BlockSpec(memory_space=pltpu.VMEM)
    sh = jax.ShapeDtypeStruct((rows, LANES), F32)
    return pl.pallas_call(
        body, name='allreduce_adamw_small', in_specs=[vm] * 4, out_specs=[vm] * 4, out_shape=[sh] * 4,
        scratch_shapes=[pltpu.VMEM((N_DEV, rows, LANES), F32), pltpu.SemaphoreType.DMA((7,)),
                        pltpu.SemaphoreType.DMA((7,))],
    )(g, w, m, v)


SMALL_COUNT = 11680


def _small_pack(d, extra=None):
    assert sum(d[n].size for n in SMALL_NAMES) == SMALL_COUNT
    flat = jnp.concatenate([d[n].reshape(-1) for n in SMALL_NAMES] + ([] if extra is None else [extra.reshape(1)]))
    rows = -(-(SMALL_COUNT + 1) // (8 * LANES)) * 8
    return jnp.pad(flat, (0, rows * LANES - flat.shape[0])).reshape(rows, LANES)


def _small_unpack(buf, like):
    flat = buf.reshape(-1)
    out, off = {}, 0
    for n in SMALL_NAMES:
        sz = int(np.prod(like[n].shape))
        out[n] = flat[off:off + sz].reshape(like[n].shape)
        off += sz
    return out


def _rope_tables(positions):
    freqs = ROPE_THETA ** (-(jnp.arange(QK_ROPE // 2, dtype=F32) * 2.0 / QK_ROPE))
    ang = positions.astype(F32)[:, None] * freqs
    return jnp.tile(jnp.cos(ang), (1, MLA_HEADS)), jnp.tile(jnp.sin(ang), (1, MLA_HEADS))


def _layer_weights(gathered, carriers):
    wd = {}
    for name, full in gathered.items():
        if name == 'w_in':
            wd[name] = {seg: W(full[lo // 2:hi // 2].reshape(hi - lo, D_MODEL), c)
                        for (seg, lo, hi), c in zip(WIN_SEGS, carriers[name])}
        else:
            wd[name] = W(full, carriers[name])
    return wd


F32_GRAD_WEIGHTS = ('rwkv_w2', 'rwkv_a2', 'rwkv_g2', 'rwkv_v2', 'conv_w')


def _make_carriers(gathered):
    gathered, carriers = dict(gathered), {}
    for name, full in gathered.items():
        if name == 'w_in':
            carriers[name] = tuple(jnp.zeros((hi - lo, D_MODEL), BF16) for _, lo, hi in WIN_SEGS)
        elif name == 'conv_w':
            gathered[name] = full.astype(BF16)
            carriers[name] = full - full.astype(BF16).astype(F32)
        else:
            carriers[name] = jnp.zeros(full.shape, F32 if name in F32_GRAD_WEIGHTS else BF16)
    return gathered, carriers


def _layer_small(small, ml):
    out = {}
    for n in SMALL_NAMES:
        l = ml if small[n].shape[0] == DEPTH else ml - 1
        if 0 <= l < small[n].shape[0]:
            out[n] = small[n][l]
    return out


def _train_step(x, positions, loss_target, weights, moms_m, moms_v):
    shards = {n: weights[n] for n in SHARDED}
    small = {n: weights[n] for n in SMALL_NAMES}
    me = 4 * lax.axis_index('x') + 2 * lax.axis_index('y') + lax.axis_index('c')
    cos, sin = _rope_tables(positions[0])

    units = [(kind, l) for l in range(DEPTH) for kind in ('pack',) + MLP_WEIGHTS]
    first, later = units[0], units[1:]
    tag = lambda u: f'{u[0]}_l{u[1]}'

    def own_block(u):
        kind, l = u
        return _pack_shards(shards, l, BF16, True) if kind == 'pack' else shards[kind][l].astype(BF16)

    own = {u: own_block(u) for u in units}
    held = lax.optimization_barrier((all_gather_blocks(own[first]), *[own[u] for u in later]))
    blocks, gathers, x0 = {first: held[0]}, {}, x[0]
    for u, mine in zip(later, held[1:]):
        own[u] = mine
        gathers[u], token = send_start(mine, False, f'gather_{tag(u)}_start')
        x0 = x0 + token

    def gathered_block(u, after):
        if u not in blocks:
            landed = send_wait(gathers[u], after, False, f'gather_{tag(u)}_wait')
            blocks[u] = lax.dynamic_update_slice(landed, own[u][None], (me,) + (0,) * own[u].ndim)
        return blocks[u]

    def mixer_weights(l, after):
        return _make_carriers(_unpack_gathered(gathered_block(('pack', l), after), l))

    def mlp_weights(l, after):
        up = gathered_block(('w_up', l), after)
        down = gathered_block(('w_down', l), after).reshape(D_FF, D_MODEL)
        return _make_carriers({'w_up': up, 'w_down': down})

    def small_of(layer, part):
        sp = _layer_small(small, layer)
        return {n: v for n, v in sp.items() if (n == 'mlp_norm') == (part == 'mlp')}

    def mixer(layer, gathered):
        if layer == 0:
            return lambda c, s, xx: _layer(0, xx, None, _layer_weights(gathered, c), s, cos, sin)
        return lambda c, s, xx, vf: _layer(layer, xx, vf, _layer_weights(gathered, c), s, cos, sin)[0]

    def mlp(layer, gathered):
        return lambda c, s, xx: _mlp(layer, xx, _layer_weights(gathered, c), s)

    gathered, carriers = mixer_weights(0, None)
    (h, v_first), vjp_mix0 = jax.vjp(mixer(0, gathered), carriers, small_of(0, 'mix'), x0)
    gathered, carriers = mlp_weights(0, h)
    h, vjp_mlp0 = jax.vjp(mlp(0, gathered), carriers, small_of(0, 'mlp'), h)
    gathered, carriers = mixer_weights(1, h)
    h, vjp_mix1 = jax.vjp(mixer(1, gathered), carriers, small_of(1, 'mix'), h, v_first)
    gathered, carriers = mlp_weights(1, h)
    y, vjp_mlp1 = jax.vjp(mlp(1, gathered), carriers, small_of(1, 'mlp'), h)

    dy, loss_parts = _loss_call(y, loss_target[0])

    shipped = {}

    def ship(u, to_send):
        handles, token = send_start(to_send, True, f'grads_{tag(u)}_start')
        shipped[u] = (handles, to_send)
        return token

    def ship_mlp(l, gw):
        down = gw['w_down'].reshape(N_DEV, D_FF // N_DEV, D_MODEL)
        return ship(('w_up', l), gw['w_up']) + ship(('w_down', l), down)

    gw, gs_mlp1, d = vjp_mlp1(dy)
    d = d + ship_mlp(1, gw)
    gw, gs_mix1, d, dvf = vjp_mix1(d)
    d = d + ship(('pack', 1), _pack_grads(gw, 1))
    gw, gs_mlp0, d = vjp_mlp0(d)
    d = d + ship_mlp(0, gw)
    gw, gs_mix0, g_x = vjp_mix0((d, dvf))
    last_token = ship(first, _pack_grads(gw, 0))
    gs0, gs1 = {**gs_mix0, **gs_mlp0}, {**gs_mix1, **gs_mlp1}

    def arrived(u, after):
        handles, sent = shipped[u]
        landed = send_wait(handles, after, True, f'grads_{tag(u)}_wait')
        mine = lax.dynamic_slice_in_dim(sent, me, 1, axis=0)
        landed = lax.dynamic_update_slice(landed, mine, (me,) + (0,) * (sent.ndim - 1))
        return sum_parts(landed, f'sum_grads_{tag(u)}')

    after = g_x + last_token
    grads = {u: arrived(u, after) for u in later if u[0] != 'pack'}
    grads.update(_unpack_shards(arrived(('pack', 1), after), 1))
    sharded_out = [{}, {}, {}, {}]

    def update(n):
        outs = adamw_weight(n, weights[n], moms_m[n], moms_v[n], [grads[(n, l)] for l in range(SHARDED[n][0])])
        for i in range(4):
            sharded_out[i][n] = outs[i]

    for n in MLP_WEIGHTS:
        update(n)
    grads.update(_unpack_shards(arrived(first, sharded_out[0][MLP_WEIGHTS[-1]]), 0))
    for n in SHARDED:
        if n not in MLP_WEIGHTS:
            update(n)

    g_small = {}
    for n in SMALL_NAMES:
        per = [g[n] for g in (gs0, gs1) if n in g]
        g_small[n] = jnp.stack(per)
    small_bufs = allreduce_adamw_small(
        _small_pack(g_small, jnp.sum(loss_parts)), _small_pack(small), _small_pack({n: moms_m[n] for n in SMALL_NAMES}),
        _small_pack({n: moms_v[n] for n in SMALL_NAMES}))
    loss = small_bufs[0].reshape(-1)[SMALL_COUNT]
    small_out = [_small_unpack(b, small) for b in small_bufs]

    pick = lambda i: [sharded_out[i][n] if n in SHARDED else small_out[i][n] for n in WEIGHT_NAMES]
    return (loss, g_x[None], *pick(0), *pick(1), *pick(2), *pick(3))


def kernel(x, positions, attn_norm, w_in, mla_q_a_norm, mla_wq_b, mla_kv_a_norm, mla_wkv_b, mla_q_norm, mla_k_norm, mla_w_o, rwkv_mu, rwkv_w0, rwkv_w2, rwkv_a0, rwkv_a2, rwkv_g2, rwkv_k_k, rwkv_k_a, rwkv_r_k, rwkv_ln_w, rwkv_ln_b, rwkv_w_o, rwkv_v1, rwkv_v_mu, rwkv_v0, rwkv_v2, conv_w, conv_w_o, w_out, mlp_norm, w_up, w_down, loss_target, m_attn_norm, m_w_in, m_mla_q_a_norm, m_mla_wq_b, m_mla_kv_a_norm, m_mla_wkv_b, m_mla_q_norm, m_mla_k_norm, m_mla_w_o, m_rwkv_mu, m_rwkv_w0, m_rwkv_w2, m_rwkv_a0, m_rwkv_a2, m_rwkv_g2, m_rwkv_k_k, m_rwkv_k_a, m_rwkv_r_k, m_rwkv_ln_w, m_rwkv_ln_b, m_rwkv_w_o, m_rwkv_v1, m_rwkv_v_mu, m_rwkv_v0, m_rwkv_v2, m_conv_w, m_conv_w_o, m_w_out, m_mlp_norm, m_w_up, m_w_down, v_attn_norm, v_w_in, v_mla_q_a_norm, v_mla_wq_b, v_mla_kv_a_norm, v_mla_wkv_b, v_mla_q_norm, v_mla_k_norm, v_mla_w_o, v_rwkv_mu, v_rwkv_w0, v_rwkv_w2, v_rwkv_a0, v_rwkv_a2, v_rwkv_g2, v_rwkv_k_k, v_rwkv_k_a, v_rwkv_r_k, v_rwkv_ln_w, v_rwkv_ln_b, v_rwkv_w_o, v_rwkv_v1, v_rwkv_v_mu, v_rwkv_v0, v_rwkv_v2, v_conv_w, v_conv_w_o, v_w_out, v_mlp_norm, v_w_up, v_w_down):
    args = locals()
    weights = {n: args[n] for n in WEIGHT_NAMES}
    moms_m = {n: args['m_' + n] for n in WEIGHT_NAMES}
    moms_v = {n: args['v_' + n] for n in WEIGHT_NAMES}
    return _train_step(x, positions, loss_target, weights, moms_m, moms_v)
```

```python
import functools

import numpy as np
import jax
import jax.numpy as jnp
from jax import lax
from jax.experimental import pallas as pl
from jax.experimental.pallas import tpu as pltpu

F32 = jnp.float32
BF16 = jnp.bfloat16

N_DEV = 8
LANES = 128
D_MODEL = 1024
DEPTH = 2
MLA_HEADS = 8
QK_NOPE = 64
QK_ROPE = 32
QK_HEAD = QK_NOPE + QK_ROPE
V_HEAD = 64
Q_LORA = 384
KV_LORA = 256
ROPE_THETA = 10000.0
RW_HEADS = 4
RW_N = 64
RW_WIDTH = RW_HEADS * RW_N
MV_LORA = 32
GN_EPS = 64e-5
CONV_WIDTH = 256
D_FF = 4 * D_MODEL
NORM_EPS = 1e-6
ADAM_LR = 0.001
ADAM_B1 = 0.9
ADAM_B2 = 0.999
ADAM_EPS = 1e-08
ADAM_WD = 0.01
ADAM_STEP = 10

VMEM_LIMIT = 56 * 1024 * 1024
MESH = pl.DeviceIdType.MESH

WEIGHT_NAMES = ['attn_norm', 'w_in', 'mla_q_a_norm', 'mla_wq_b', 'mla_kv_a_norm', 'mla_wkv_b', 'mla_q_norm',
                'mla_k_norm', 'mla_w_o', 'rwkv_mu', 'rwkv_w0', 'rwkv_w2', 'rwkv_a0', 'rwkv_a2', 'rwkv_g2',
                'rwkv_k_k', 'rwkv_k_a', 'rwkv_r_k', 'rwkv_ln_w', 'rwkv_ln_b', 'rwkv_w_o', 'rwkv_v1',
                'rwkv_v_mu', 'rwkv_v0', 'rwkv_v2', 'conv_w', 'conv_w_o', 'w_out', 'mlp_norm', 'w_up', 'w_down']

SHARDED = {
    'w_in': (2, (1024, 5536), 1), 'mla_wq_b': (2, (384, 768), 1), 'mla_wkv_b': (2, (256, 1024), 1),
    'mla_w_o': (2, (512, 1024), 1), 'rwkv_w2': (2, (64, 256), 1), 'rwkv_a2': (2, (64, 256), 1),
    'rwkv_g2': (2, (128, 256), 1), 'rwkv_w_o': (2, (256, 1024), 1), 'conv_w': (2, (3, 256), 1),
    'conv_w_o': (2, (256, 1024), 1), 'w_out': (2, (1024, 1024), 0), 'w_up': (2, (1024, 4096), 1),
    'w_down': (2, (4096, 1024), 0), 'rwkv_v1': (1, (1024, 32), 0), 'rwkv_v2': (1, (32, 256), 1),
}
SMALL_NAMES = [n for n in WEIGHT_NAMES if n not in SHARDED]
TRANSPOSED = ('w_in',)
WIN_SEGS = (('gates', 0, 3072), ('cq', 3072, 3456), ('ckv', 3456, 3712), ('kpe', 3712, 3744), ('rkv', 3744, 4512),
            ('xwa', 4512, 4640), ('xg', 4640, 4768), ('conv', 4768, 5536))
PACK_ROW_MULT = 512


def _cparams(sem=None, **kw):
    if sem is not None:
        kw['dimension_semantics'] = sem
    return pltpu.CompilerParams(vmem_limit_bytes=VMEM_LIMIT, **kw)


def _pick(n, cands):
    for c in cands:
        if n % c == 0:
            return c
    raise ValueError(f'no tile for {n}')


def _mm_nn(a, b, add=None, name='mm_nn'):
    M, K = a.shape
    N = b.shape[1]
    tm = _pick(M, (1024, 512, 256, 128))
    tn = _pick(N, (512, 384, 256, 128))
    tk = _pick(K, (1024, 512, 384, 256, 128))
    nk = K // tk
    has_add = add is not None

    def body(*refs):
        if has_add:
            a_ref, b_ref, add_ref, o_ref, acc_ref = refs
        else:
            a_ref, b_ref, o_ref, acc_ref = refs
        kk = pl.program_id(2)
        part = jnp.dot(a_ref[...].astype(BF16), b_ref[...].astype(BF16), preferred_element_type=F32)

        @pl.when(kk == 0)
        def _():
            acc_ref[...] = part

        @pl.when(kk > 0)
        def _():
            acc_ref[...] += part

        @pl.when(kk == nk - 1)
        def _():
            if has_add:
                o_ref[...] = acc_ref[...] + add_ref[...]
            else:
                o_ref[...] = acc_ref[...]

    in_specs = [pl.BlockSpec((tm, tk), lambda i, j, k: (i, k)), pl.BlockSpec((tk, tn), lambda i, j, k: (k, j))]
    args = [a, b]
    if has_add:
        in_specs.append(pl.BlockSpec((tm, tn), lambda i, j, k: (i, j)))
        args.append(add)
    return pl.pallas_call(
        body, name=name, grid=(M // tm, N // tn, nk), in_specs=in_specs,
        out_specs=pl.BlockSpec((tm, tn), lambda i, j, k: (i, j)),
        out_shape=jax.ShapeDtypeStruct((M, N), F32),
        scratch_shapes=[pltpu.VMEM((tm, tn), F32)],
        compiler_params=_cparams(('parallel', 'parallel', 'arbitrary')),
    )(*args)


def _mm_nt(a, b, add=None, name='mm_nt'):
    M, N = a.shape
    blocked = b.ndim == 3
    K = b.shape[-2]
    tm = _pick(M, (1024, 512, 256, 128))
    tk = _pick(K, (512, 384, 256, 128))
    tn = N // N_DEV if blocked else _pick(N, (1024, 512, 384, 256, 128))
    nn = N // tn
    has_add = add is not None

    def body(*refs):
        if has_add:
            a_ref, b_ref, add_ref, o_ref, acc_ref = refs
        else:
            a_ref, b_ref, o_ref, acc_ref = refs
        kk = pl.program_id(2)
        part = lax.dot_general(a_ref[...].astype(BF16), b_ref[...].astype(BF16), (((1,), (1,)), ((), ())),
                               preferred_element_type=F32)

        @pl.when(kk == 0)
        def _():
            acc_ref[...] = part

        @pl.when(kk > 0)
        def _():
            acc_ref[...] += part

        @pl.when(kk == nn - 1)
        def _():
            if has_add:
                o_ref[...] = acc_ref[...] + add_ref[...]
            else:
                o_ref[...] = acc_ref[...]

    if blocked:
        b_spec = pl.BlockSpec((None, tk, tn), lambda i, j, k: (k, j, 0))
    else:
        b_spec = pl.BlockSpec((tk, tn), lambda i, j, k: (j, k))
    in_specs = [pl.BlockSpec((tm, tn), lambda i, j, k: (i, k)), b_spec]
    args = [a, b]
    if has_add:
        in_specs.append(pl.BlockSpec((tm, tk), lambda i, j, k: (i, j)))
        args.append(add)
    return pl.pallas_call(
        body, name=name, grid=(M // tm, K // tk, nn), in_specs=in_specs,
        out_specs=pl.BlockSpec((tm, tk), lambda i, j, k: (i, j)),
        out_shape=jax.ShapeDtypeStruct((M, K), F32),
        scratch_shapes=[pltpu.VMEM((tm, tk), F32)],
        compiler_params=_cparams(('parallel', 'parallel', 'arbitrary')),
    )(*args)


def _mm_tn(a, b, name='mm_tn', blocked=False):
    M, K = a.shape
    N = b.shape[1]
    tm = _pick(M, (1024, 512, 256, 128))
    tk = _pick(K, (512, 384, 256, 128))
    tn = N // N_DEV if blocked else _pick(N, (512, 384, 256, 128))
    nm = M // tm

    def body(a_ref, b_ref, o_ref, acc_ref):
        mm = pl.program_id(2)
        part = lax.dot_general(a_ref[...].astype(BF16), b_ref[...].astype(BF16), (((0,), (0,)), ((), ())),
                               preferred_element_type=F32)

        @pl.when(mm == 0)
        def _():
            acc_ref[...] = part

        @pl.when(mm > 0)
        def _():
            acc_ref[...] += part

        @pl.when(mm == nm - 1)
        def _():
            o_ref[...] = acc_ref[...].astype(BF16)

    if blocked:
        out_spec = pl.BlockSpec((None, tk, tn), lambda i, j, m: (j, i, 0))
        out_shape = jax.ShapeDtypeStruct((N_DEV, K, tn), BF16)
    else:
        out_spec = pl.BlockSpec((tk, tn), lambda i, j, m: (i, j))
        out_shape = jax.ShapeDtypeStruct((K, N), BF16)
    return pl.pallas_call(
        body, name=name, grid=(K // tk, N // tn, nm),
        in_specs=[pl.BlockSpec((tm, tk), lambda i, j, m: (m, i)), pl.BlockSpec((tm, tn), lambda i, j, m: (m, j))],
        out_specs=out_spec, out_shape=out_shape,
        scratch_shapes=[pltpu.VMEM((tk, tn), F32)],
        compiler_params=_cparams(('parallel', 'parallel', 'arbitrary')),
    )(a, b)


@functools.partial(jax.custom_vjp, nondiff_argnums=(4,))
def _linear_add(a, wb, wc, add, name):
    return _mm_nn(a, wb, add, name=name + '_f')


def _linear_add_fwd(a, wb, wc, add, name):
    return _mm_nn(a, wb, add, name=name + '_f'), (a, wb)


def _linear_add_bwd(name, res, dy):
    a, wb = res
    return _mm_nt(dy, wb, name=name + '_da'), None, _mm_tn(a, dy, name=name + '_dw'), dy


_linear_add.defvjp(_linear_add_fwd, _linear_add_bwd)


@functools.partial(jax.custom_vjp, nondiff_argnums=(3,))
def _multi_linear(a, wbs, wcs, name):
    return tuple(_mm_nn(a, wb, name=f'{name}_f{i}') for i, wb in enumerate(wbs))


def _multi_linear_fwd(a, wbs, wcs, name):
    return _multi_linear(a, wbs, wcs, name), (a, wbs)


def _multi_linear_bwd(name, res, dys):
    a, wbs = res
    da = None
    for i, (dy, wb) in enumerate(zip(dys, wbs)):
        da = _mm_nt(dy, wb, add=da, name=f'{name}_da{i}')
    dws = tuple(_mm_tn(a, dy, name=f'{name}_dw{i}') for i, dy in enumerate(dys))
    return da, None, dws


_multi_linear.defvjp(_multi_linear_fwd, _multi_linear_bwd)


class W:
    def __init__(self, b, c):
        self.b, self.c = b, c

    def map(self, fn):
        return W(fn(self.b), fn(self.c))


def _wcat(ws, axis):
    return W(jnp.concatenate([w.b for w in ws], axis), jnp.concatenate([w.c for w in ws], axis))


def linear(a, w, add=None, name='lin'):
    if add is None:
        return _multi_linear(a, (w.b,), (w.c,), name)[0]
    return _linear_add(a, w.b, w.c, add, name)


def ROW(diff=True, pieces=None):
    return ('row', diff, pieces)


def FULL(diff=True):
    return ('full', diff, None)


def _load_args(refs, specs):
    args, amap = [], []
    for i, (ref, (kind, diff, pieces)) in enumerate(zip(refs, specs)):
        if pieces is None:
            args.append(ref[...])
            amap.append((i, None))
        else:
            for (s, w) in pieces:
                args.append(ref[:, s:s + w])
                amap.append((i, (s, w)))
    return args, amap


def _stage_in_specs(ins, specs, tb):
    out = []
    for a, (kind, _, _) in zip(ins, specs):
        if kind == 'row':
            out.append(pl.BlockSpec((tb, a.shape[1]), lambda i: (i, 0)))
        else:
            out.append(pl.BlockSpec(a.shape, lambda i: (0, 0)))
    return out


def _stage_fwd(fn, ins, specs, out_widths, name, tb, out_dtype=F32):
    T = [a for a, s in zip(ins, specs) if s[0] == 'row'][0].shape[0]
    tb = min(tb, T)
    n_in = len(ins)

    def body(*refs):
        args, _ = _load_args(refs[:n_in], specs)
        outs = fn(*args)
        for o_ref, o in zip(refs[n_in:], outs):
            o_ref[...] = o.astype(out_dtype)

    return pl.pallas_call(
        body, name=name + '_f', grid=(T // tb,), in_specs=_stage_in_specs(ins, specs, tb),
        out_specs=[pl.BlockSpec((tb, w), lambda i: (i, 0)) for w in out_widths],
        out_shape=[jax.ShapeDtypeStruct((T, w), out_dtype) for w in out_widths],
        compiler_params=_cparams(('parallel',)),
    )(*ins)


def _stage_bwd(fn, ins, specs, out_widths, douts, name, tb):
    T = [a for a, s in zip(ins, specs) if s[0] == 'row'][0].shape[0]
    tb = min(tb, T)
    n_in, n_out = len(ins), len(out_widths)
    diff_inputs = [i for i, s in enumerate(specs) if s[1]]

    def body(*refs):
        in_refs, dout_refs, g_refs = refs[:n_in], refs[n_in:n_in + n_out], refs[n_in + n_out:]
        args, amap = _load_args(in_refs, specs)
        didx = [j for j, (i, _) in enumerate(amap) if specs[i][1]]

        def f(*dv):
            full = list(args)
            for j, v in zip(didx, dv):
                full[j] = v
            return tuple(fn(*full))

        _, vjp = jax.vjp(f, *[args[j] for j in didx])
        gs = vjp(tuple(d[...] for d in dout_refs))
        gmap = {j: g for j, g in zip(didx, gs)}
        first = pl.program_id(0) == 0
        for g_ref, i in zip(g_refs, diff_inputs):
            kind, _, pieces = specs[i]
            js = [j for j, (ii, _) in enumerate(amap) if ii == i]
            if kind == 'row':
                if pieces is None:
                    g_ref[...] = gmap[js[0]]
                else:
                    if sum(w for _, w in pieces) != ins[i].shape[1]:
                        g_ref[...] = jnp.zeros(g_ref.shape, F32)
                    for j in js:
                        s, w = amap[j][1]
                        g_ref[:, s:s + w] = gmap[j]
            else:
                @pl.when(first)
                def _(g_ref=g_ref):
                    g_ref[...] = jnp.zeros(g_ref.shape, F32)

                g_ref[...] += gmap[js[0]]

    in_specs = _stage_in_specs(ins, specs, tb) + [pl.BlockSpec((tb, w), lambda i: (i, 0)) for w in out_widths]
    out_specs, out_shape = [], []
    for i in diff_inputs:
        a = ins[i]
        if specs[i][0] == 'row':
            out_specs.append(pl.BlockSpec((tb, a.shape[1]), lambda i: (i, 0)))
        else:
            out_specs.append(pl.BlockSpec(a.shape, lambda i: (0, 0)))
        out_shape.append(jax.ShapeDtypeStruct(a.shape, F32))
    return pl.pallas_call(
        body, name=name + '_b', grid=(T // tb,), in_specs=in_specs, out_specs=out_specs, out_shape=out_shape,
        compiler_params=_cparams(('arbitrary',)),
    )(*ins, *douts)


def stage_op(fn, specs, out_widths, name, tb=256):
    n = len(specs)
    diff_inputs = [i for i, s in enumerate(specs) if s[1]]

    @jax.custom_vjp
    def op(*ins):
        return tuple(_stage_fwd(fn, ins, specs, out_widths, name, tb))

    def op_fwd(*ins):
        return op(*ins), ins

    def op_bwd(ins, douts):
        gs = _stage_bwd(fn, ins, specs, out_widths, douts, name, tb)
        res = [None] * n
        for i, g in zip(diff_inputs, gs):
            res[i] = g
        return tuple(res)

    op.defvjp(op_fwd, op_bwd)
    return op


@jax.custom_vjp
def bdot(x, w):
    return jnp.dot(x.astype(BF16), w.astype(BF16), preferred_element_type=F32)


def _bdot_fwd(x, w):
    return bdot(x, w), (x, w)


def _bdot_bwd(res, dy):
    x, w = res
    dyb = dy.astype(BF16)
    dx = lax.dot_general(dyb, w.astype(BF16), (((1,), (1,)), ((), ())), preferred_element_type=F32)
    dw = lax.dot_general(x.astype(BF16), dyb, (((0,), (0,)), ((), ())), preferred_element_type=F32)
    return dx, dw


bdot.defvjp(_bdot_fwd, _bdot_bwd)


def _sdot_raw(x, c):
    hi = x.astype(BF16)
    r1 = x - hi.astype(F32)
    mid = r1.astype(BF16)
    lo = (r1 - mid.astype(F32)).astype(BF16)
    d = lambda u: jnp.dot(u, c, preferred_element_type=F32)
    return d(hi) + d(mid) + d(lo)


@jax.custom_vjp
def sdot(x, c, ct):
    return _sdot_raw(x, c)


def _sdot_fwd(x, c, ct):
    return _sdot_raw(x, c), (c, ct)


def _sdot_bwd(res, dy):
    c, ct = res
    return _sdot_raw(dy, ct), None, None


sdot.defvjp(_sdot_fwd, _sdot_bwd)


def _sigmoid(x):
    return 1.0 / (1.0 + jnp.exp(-x))


def _rms(x, g):
    return x * lax.rsqrt(jnp.mean(x * x, axis=-1, keepdims=True) + NORM_EPS) * g


def _mm_up_relu2(a, b, name):
    M, K = a.shape
    tn = b.shape[2]
    N = N_DEV * tn
    tm = _pick(M, (1024, 512, 256, 128))

    def body(a_ref, b_ref, u_ref, act_ref):
        u = jnp.dot(a_ref[...], b_ref[...], preferred_element_type=F32)
        r = jnp.maximum(u, 0.0)
        u_ref[...] = u.astype(BF16)
        act_ref[...] = (r * r).astype(BF16)

    out = pl.BlockSpec((tm, tn), lambda i, j: (i, j))
    sh = jax.ShapeDtypeStruct((M, N), BF16)
    return pl.pallas_call(
        body, name=name, grid=(M // tm, N // tn),
        in_specs=[pl.BlockSpec((tm, K), lambda i, j: (i, 0)), pl.BlockSpec((None, K, tn), lambda i, j: (j, 0, 0))],
        out_specs=[out, out], out_shape=[sh, sh], compiler_params=_cparams(('parallel', 'parallel')),
    )(a, b)


def _mm_down_bwd(dy, b, u, name):
    M, N = dy.shape
    K = b.shape[0]
    tm = _pick(M, (1024, 512, 256, 128))
    tk = _pick(K, (512, 256, 128))

    def body(dy_ref, b_ref, u_ref, du_ref):
        d = lax.dot_general(dy_ref[...].astype(BF16), b_ref[...], (((1,), (1,)), ((), ())),
                            preferred_element_type=F32)
        du_ref[...] = (d * (2.0 * jnp.maximum(u_ref[...].astype(F32), 0.0))).astype(BF16)

    blk = pl.BlockSpec((tm, tk), lambda i, j: (i, j))
    return pl.pallas_call(
        body, name=name, grid=(M // tm, K // tk),
        in_specs=[pl.BlockSpec((tm, N), lambda i, j: (i, 0)), pl.BlockSpec((tk, N), lambda i, j: (j, 0)), blk],
        out_specs=blk, out_shape=jax.ShapeDtypeStruct((M, K), BF16),
        compiler_params=_cparams(('parallel', 'parallel')),
    )(dy, b, u)


_RMS_SPECS = [ROW(), FULL()]
_rms_fn = lambda xv, gv: (_rms(xv, gv),)


@functools.partial(jax.custom_vjp, nondiff_argnums=(6,))
def mlp_block(x, g, wup_b, wup_c, wdown_b, wdown_c, name):
    return _mlp_fwd(x, g, wup_b, wup_c, wdown_b, wdown_c, name)[0]


def _mlp_fwd(x, g, wup_b, wup_c, wdown_b, wdown_c, name):
    h = _stage_fwd(_rms_fn, (x, g), _RMS_SPECS, [x.shape[1]], name + '_norm', 256, out_dtype=BF16)[0]
    u, act = _mm_up_relu2(h, wup_b, name + '_up')
    y = _mm_nn(act, wdown_b, add=x, name=name + '_down')
    return y, (x, g, h, u, act, wup_b, wdown_b)


def _mlp_bwd(name, res, dy):
    x, g, h, u, act, wup_b, wdown_b = res
    du = _mm_down_bwd(dy, wdown_b, u, name + '_down_da')
    dwdown = _mm_tn(act, dy, name=name + '_down_dw')
    dh = _mm_nt(du, wup_b, name=name + '_up_da')
    dwup = _mm_tn(h, du, name=name + '_up_dw', blocked=True)
    dx, dg = _stage_bwd(_rms_fn, (x, g), _RMS_SPECS, [x.shape[1]], (dh,), name + '_norm', 256)
    return dx + dy, dg, None, dwup, None, dwdown


mlp_block.defvjp(_mlp_fwd, _mlp_bwd)


@functools.partial(jax.custom_vjp, nondiff_argnums=(4, 5))
def _norm_projections(x, g, wbs, wcs, transposed, name):
    return _norm_projections_fwd(x, g, wbs, wcs, transposed, name)[0]


def _norm_projections_fwd(x, g, wbs, wcs, transposed, name):
    h = _stage_fwd(_rms_fn, (x, g), _RMS_SPECS, [x.shape[1]], name + '_norm', 256, out_dtype=BF16)[0]
    mm = _mm_nt if transposed else _mm_nn
    return tuple(mm(h, wb, name=f'{name}_f{i}') for i, wb in enumerate(wbs)), (x, g, h, wbs)


def _norm_projections_bwd(transposed, name, res, dys):
    x, g, h, wbs = res
    dh = None
    for i, (dy, wb) in enumerate(zip(dys, wbs)):
        dh = (_mm_nn if transposed else _mm_nt)(dy, wb, add=dh, name=f'{name}_da{i}')
    if transposed:
        dws = tuple(_mm_tn(dy, h, name=f'{name}_dw{i}') for i, dy in enumerate(dys))
    else:
        dws = tuple(_mm_tn(h, dy, name=f'{name}_dw{i}') for i, dy in enumerate(dys))
    dx, dg = _stage_bwd(_rms_fn, (x, g), _RMS_SPECS, [x.shape[1]], (dh,), name + '_norm', 256)
    return dx, dg, None, dws


_norm_projections.defvjp(_norm_projections_fwd, _norm_projections_bwd)


def norm_projections(x, g, ws, name, transposed=False):
    return _norm_projections(x, g.reshape(1, -1), tuple(w.b for w in ws), tuple(w.c for w in ws), transposed, name)


def _shift_down(x, rows):
    return jnp.where(rows == 0, 0.0, pltpu.roll(x, 1, 0))


def _shift_up(x, rows, T):
    return jnp.where(rows == T - 1, 0.0, pltpu.roll(x, T - 1, 0))


def _tshift_fwd_call(x, mu, name):
    T, C = x.shape

    def body(x_ref, mu_ref, o_ref):
        xv = x_ref[...]
        rows = lax.broadcasted_iota(jnp.int32, xv.shape, 0)
        o_ref[...] = xv + (_shift_down(xv, rows) - xv) * mu_ref[...]

    return pl.pallas_call(
        body, name=name + '_f', grid=(C // LANES,),
        in_specs=[pl.BlockSpec((T, LANES), lambda j: (0, j)), pl.BlockSpec((1, LANES), lambda j: (0, j))],
        out_specs=pl.BlockSpec((T, LANES), lambda j: (0, j)), out_shape=jax.ShapeDtypeStruct((T, C), F32),
        compiler_params=_cparams(('parallel',)),
    )(x, mu)


def _tshift_bwd_call(x, mu, dy, name):
    T, C = x.shape

    def body(x_ref, mu_ref, dy_ref, dx_ref, dmu_ref):
        xv, d = x_ref[...], dy_ref[...]
        rows = lax.broadcasted_iota(jnp.int32, xv.shape, 0)
        z = d * mu_ref[...]
        dx_ref[...] = d - z + _shift_up(z, rows, T)
        dmu_ref[...] = jnp.sum(d * (_shift_down(xv, rows) - xv), axis=0, keepdims=True)

    return pl.pallas_call(
        body, name=name + '_b', grid=(C // LANES,),
        in_specs=[pl.BlockSpec((T, LANES), lambda j: (0, j)), pl.BlockSpec((1, LANES), lambda j: (0, j)),
                  pl.BlockSpec((T, LANES), lambda j: (0, j))],
        out_specs=[pl.BlockSpec((T, LANES), lambda j: (0, j)), pl.BlockSpec((1, LANES), lambda j: (0, j))],
        out_shape=[jax.ShapeDtypeStruct((T, C), F32), jax.ShapeDtypeStruct((1, C), F32)],
        compiler_params=_cparams(('parallel',)),
    )(x, mu, dy)


@functools.partial(jax.custom_vjp, nondiff_argnums=(2,))
def token_shift_mix(x, mu, name):
    return _tshift_fwd_call(x, mu, name)


def _tsm_fwd(x, mu, name):
    return _tshift_fwd_call(x, mu, name), (x, mu)


def _tsm_bwd(name, res, dy):
    x, mu = res
    dx, dmu = _tshift_bwd_call(x, mu, dy, name)
    return dx, dmu


token_shift_mix.defvjp(_tsm_fwd, _tsm_bwd)


def _conv_specs(T):
    nb = CONV_WIDTH // LANES
    return [pl.BlockSpec((T, LANES), lambda j: (0, j)), pl.BlockSpec((T, LANES), lambda j: (0, nb + j)),
            pl.BlockSpec((T, LANES), lambda j: (0, 2 * nb + j)), pl.BlockSpec((3, LANES), lambda j: (0, j))]


def _conv_fwd_call(cv, w, name):
    T = cv.shape[0]

    def body(b_ref, c_ref, x_ref, w_ref, o_ref):
        u = c_ref[...] * x_ref[...]
        rows = lax.broadcasted_iota(jnp.int32, u.shape, 0)
        u1 = _shift_down(u, rows)
        u2 = _shift_down(u1, rows)
        o_ref[...] = b_ref[...] * (w_ref[0:1, :] * u2 + w_ref[1:2, :] * u1 + w_ref[2:3, :] * u)

    return pl.pallas_call(
        body, name=name + '_f', grid=(CONV_WIDTH // LANES,), in_specs=_conv_specs(T),
        out_specs=pl.BlockSpec((T, LANES), lambda j: (0, j)),
        out_shape=jax.ShapeDtypeStruct((T, CONV_WIDTH), F32), compiler_params=_cparams(('parallel',)),
    )(cv, cv, cv, w)


def _conv_bwd_call(cv, w, do, name):
    T = cv.shape[0]

    def body(b_ref, c_ref, x_ref, w_ref, do_ref, db_ref, dc_ref, dx_ref, dw_ref):
        c, x, d = c_ref[...], x_ref[...], do_ref[...]
        u = c * x
        rows = lax.broadcasted_iota(jnp.int32, u.shape, 0)
        u1 = _shift_down(u, rows)
        u2 = _shift_down(u1, rows)
        w0, w1, w2 = w_ref[0:1, :], w_ref[1:2, :], w_ref[2:3, :]
        db_ref[...] = d * (w0 * u2 + w1 * u1 + w2 * u)
        dy = d * b_ref[...]
        dy1 = _shift_up(dy, rows, T)
        dy2 = _shift_up(dy1, rows, T)
        du = w2 * dy + w1 * dy1 + w0 * dy2
        dc_ref[...] = du * x
        dx_ref[...] = du * c
        dw_ref[0:1, :] = jnp.sum(dy * u2, axis=0, keepdims=True)
        dw_ref[1:2, :] = jnp.sum(dy * u1, axis=0, keepdims=True)
        dw_ref[2:3, :] = jnp.sum(dy * u, axis=0, keepdims=True)

    blk = pl.BlockSpec((T, LANES), lambda j: (0, j))
    sh = jax.ShapeDtypeStruct((T, CONV_WIDTH), F32)
    return pl.pallas_call(
        body, name=name + '_b', grid=(CONV_WIDTH // LANES,), in_specs=_conv_specs(T) + [blk],
        out_specs=[blk, blk, blk, pl.BlockSpec((3, LANES), lambda j: (0, j))],
        out_shape=[sh, sh, sh, jax.ShapeDtypeStruct((3, CONV_WIDTH), F32)],
        compiler_params=_cparams(('parallel',)),
    )(cv, cv, cv, w, do)


@functools.partial(jax.custom_vjp, nondiff_argnums=(2,))
def short_conv(cv, w, name):
    return _conv_fwd_call(cv, w, name)


def _sc_fwd(cv, w, name):
    return _conv_fwd_call(cv, w, name), (cv, w)


def _sc_bwd(name, res, do):
    cv, w = res
    db, dc, dx, dw = _conv_bwd_call(cv, w, do, name)
    return jnp.concatenate([db, dc, dx], axis=1), dw


short_conv.defvjp(_sc_fwd, _sc_bwd)


ATT_SCALE = QK_HEAD ** -0.5
NPAIR = MLA_HEADS // 2


def _att_bq(T):
    return min(256, T)


def _att_masks(pair, j):
    lane = lax.broadcasted_iota(jnp.int32, (1, LANES), 1)
    mask_n = (lane // QK_NOPE) == j
    mask_r = (lane // (QK_ROPE // 2)) == (2 * pair + j)
    return mask_n, mask_r


def _att_probs(qcat, kcat, row0, stop):
    s = lax.dot_general(qcat, kcat, (((1,), (1,)), ((), ())), preferred_element_type=F32) * ATT_SCALE
    r = row0 + lax.broadcasted_iota(jnp.int32, s.shape, 0)
    c = lax.broadcasted_iota(jnp.int32, s.shape, 1)
    s = jnp.where(c <= r, s, -jnp.inf)
    e = jnp.exp(s - jnp.max(s, axis=-1, keepdims=True))
    return e / jnp.sum(e, axis=-1, keepdims=True)


def _att_in_specs(T):
    blk = lambda f: pl.BlockSpec((T, LANES), f)
    return [blk(lambda p: (0, p)), blk(lambda p: (0, 0)), blk(lambda p: (0, 0)),
            blk(lambda p: (0, p)), blk(lambda p: (0, 0)), blk(lambda p: (0, 0)), blk(lambda p: (0, p))]


def _att_fwd_call(qn, q1, q2, kn, k1, k2, v, name):
    T = qn.shape[0]
    bq = _att_bq(T)

    def body(qn_ref, q1_ref, q2_ref, kn_ref, k1_ref, k2_ref, v_ref, o_ref):
        pair = pl.program_id(0)
        for i in range(T // bq):
            r0, stop = i * bq, (i + 1) * bq
            kcat = jnp.concatenate([kn_ref[0:stop, :], k1_ref[0:stop, :], k2_ref[0:stop, :]], axis=1).astype(BF16)
            vb = v_ref[0:stop, :].astype(BF16)
            outs = []
            for j in range(2):
                mask_n, mask_r = _att_masks(pair, j)
                qcat = jnp.concatenate([jnp.where(mask_n, qn_ref[r0:stop, :], 0.0),
                                        jnp.where(mask_r, q1_ref[r0:stop, :], 0.0),
                                        jnp.where(mask_r, q2_ref[r0:stop, :], 0.0)], axis=1).astype(BF16)
                p = _att_probs(qcat, kcat, r0, stop)
                outs.append(jnp.dot(p.astype(BF16), vb, preferred_element_type=F32))
            mask_n0, _ = _att_masks(pair, 0)
            o_ref[r0:stop, :] = jnp.where(mask_n0, outs[0], outs[1])

    return pl.pallas_call(
        body, name=name + '_f', grid=(NPAIR,), in_specs=_att_in_specs(T),
        out_specs=pl.BlockSpec((T, LANES), lambda p: (0, p)),
        out_shape=jax.ShapeDtypeStruct((T, MLA_HEADS * V_HEAD), F32), compiler_params=_cparams(('parallel',)),
    )(qn, q1, q2, kn, k1, k2, v)


def _att_bwd_call(qn, q1, q2, kn, k1, k2, v, o, do, name):
    T = qn.shape[0]
    bq = _att_bq(T)

    def body(qn_ref, q1_ref, q2_ref, kn_ref, k1_ref, k2_ref, v_ref, o_ref, do_ref,
             dqn_ref, dq1_ref, dq2_ref, dkn_ref, dk1_ref, dk2_ref, dv_ref, dk_acc, dv_acc):
        pair = pl.program_id(0)

        @pl.when(pair == 0)
        def _():
            dq1_ref[...] = jnp.zeros(dq1_ref.shape, F32)
            dq2_ref[...] = jnp.zeros(dq2_ref.shape, F32)
            dk1_ref[...] = jnp.zeros(dk1_ref.shape, F32)
            dk2_ref[...] = jnp.zeros(dk2_ref.shape, F32)

        dk_acc[...] = jnp.zeros(dk_acc.shape, F32)
        dv_acc[...] = jnp.zeros(dv_acc.shape, F32)
        for i in range(T // bq):
            r0, stop = i * bq, (i + 1) * bq
            kcat = jnp.concatenate([kn_ref[0:stop, :], k1_ref[0:stop, :], k2_ref[0:stop, :]], axis=1).astype(BF16)
            vb = v_ref[0:stop, :].astype(BF16)
            dqn = jnp.zeros((bq, LANES), F32)
            for j in range(2):
                mask_n, mask_r = _att_masks(pair, j)
                qcat = jnp.concatenate([jnp.where(mask_n, qn_ref[r0:stop, :], 0.0),
                                        jnp.where(mask_r, q1_ref[r0:stop, :], 0.0),
                                        jnp.where(mask_r, q2_ref[r0:stop, :], 0.0)], axis=1).astype(BF16)
                p = _att_probs(qcat, kcat, r0, stop)
                dom = jnp.where(mask_n, do_ref[r0:stop, :], 0.0)
                delta = jnp.sum(dom * o_ref[r0:stop, :], axis=-1, keepdims=True)
                domb = dom.astype(BF16)
                dp = lax.dot_general(domb, vb, (((1,), (1,)), ((), ())), preferred_element_type=F32)
                ds = (p * (dp - delta) * ATT_SCALE).astype(BF16)
                dqc = jnp.dot(ds, kcat, preferred_element_type=F32)
                dqn = dqn + jnp.where(mask_n, dqc[:, 0:LANES], 0.0)
                dq1_ref[r0:stop, :] += jnp.where(mask_r, dqc[:, LANES:2 * LANES], 0.0)
                dq2_ref[r0:stop, :] += jnp.where(mask_r, dqc[:, 2 * LANES:3 * LANES], 0.0)
                dk_acc[0:stop, :] += lax.dot_general(ds, qcat, (((0,), (0,)), ((), ())),
                                                     preferred_element_type=F32)
                dv_acc[0:stop, :] += lax.dot_general(p.astype(BF16), domb, (((0,), (0,)), ((), ())),
                                                     preferred_element_type=F32)
            dqn_ref[r0:stop, :] = dqn
        dkn_ref[...] = dk_acc[:, 0:LANES]
        dk1_ref[...] += dk_acc[:, LANES:2 * LANES]
        dk2_ref[...] += dk_acc[:, 2 * LANES:3 * LANES]
        dv_ref[...] = dv_acc[...]

    per_pair = pl.BlockSpec((T, LANES), lambda p: (0, p))
    shared = pl.BlockSpec((T, LANES), lambda p: (0, 0))
    wide = jax.ShapeDtypeStruct((T, MLA_HEADS * QK_NOPE), F32)
    narrow = jax.ShapeDtypeStruct((T, LANES), F32)
    return pl.pallas_call(
        body, name=name + '_b', grid=(NPAIR,), in_specs=_att_in_specs(T) + [per_pair, per_pair],
        out_specs=[per_pair, shared, shared, per_pair, shared, shared, per_pair],
        out_shape=[wide, narrow, narrow, wide, narrow, narrow, wide],
        scratch_shapes=[pltpu.VMEM((T, 3 * LANES), F32), pltpu.VMEM((T, LANES), F32)],
        compiler_params=_cparams(('arbitrary',)),
    )(qn, q1, q2, kn, k1, k2, v, o, do)


@functools.partial(jax.custom_vjp, nondiff_argnums=(7,))
def attention(qn, q1, q2, kn, k1, k2, v, name):
    return _att_fwd_call(qn, q1, q2, kn, k1, k2, v, name)


def _attn_fwd(qn, q1, q2, kn, k1, k2, v, name):
    o = _att_fwd_call(qn, q1, q2, kn, k1, k2, v, name)
    return o, (qn, q1, q2, kn, k1, k2, v, o)


def _attn_bwd(name, res, do):
    return tuple(_att_bwd_call(*res, do, name))


attention.defvjp(_attn_fwd, _attn_bwd)


SCAN_CHUNK = 64
SCAN_UNROLL = 4


def _block_ones(n, seg):
    i = np.arange(n)
    return (i[:, None] // seg == i[None, :] // seg).astype(np.float32)


def _scan_diag():
    i = np.arange(RW_WIDTH)
    return jnp.asarray((np.arange(RW_N)[:, None] == (i[None, :] % RW_N)).astype(np.float32))


def _head_rowsum(x):
    low = lax.broadcasted_iota(jnp.int32, (1, LANES), 1) < RW_N
    tiles = []
    for j in range(RW_WIDTH // LANES):
        xt = x[:, j * LANES:(j + 1) * LANES]
        x0 = jnp.where(low, xt, 0.0)
        s0 = jnp.sum(x0, axis=-1, keepdims=True)
        s1 = jnp.sum(xt - x0, axis=-1, keepdims=True)
        tiles.append(jnp.where(low, s0, s1))
    return jnp.concatenate(tiles, axis=1)


def _unrolled_loop(n, step, init):
    def body(i, carry):
        for j in range(SCAN_UNROLL):
            carry = step(i * SCAN_UNROLL + j, carry)
        return carry
    return lax.fori_loop(0, n // SCAN_UNROLL, body, init)


def _scan_fwd_call(r, w, k, v, a, b, name):
    T = r.shape[0]
    tc = min(SCAN_CHUNK, T)
    dg = _scan_diag()

    def body(r_ref, w_ref, k_ref, v_ref, a_ref, b_ref, dg_ref, y_ref, st_ref, sa_ref, vc_ref, s_ref):
        @pl.when(pl.program_id(0) == 0)
        def _():
            s_ref[...] = jnp.zeros(s_ref.shape, F32)

        dgv = dg_ref[...]
        readout = lambda s, t: jnp.sum(_head_rowsum(s * r_ref[t]) * dgv, axis=0, keepdims=True)

        def step(t, carry):
            s, vcol = carry
            st_ref[t] = s
            vc_ref[t] = vcol
            sa = _head_rowsum(s * a_ref[t])
            sa_ref[t] = sa
            prev = jnp.maximum(t - 1, 0)
            y_ref[prev] = readout(s, prev)
            vcol_next = _head_rowsum(dgv * v_ref[jnp.minimum(t + 1, tc - 1)])
            sn = s * w_ref[t] + sa * b_ref[t] + vcol * k_ref[t]
            return sn, vcol_next

        s_end, _ = _unrolled_loop(tc, step, (s_ref[...], _head_rowsum(dgv * v_ref[0])))
        y_ref[tc - 1] = readout(s_end, tc - 1)
        s_ref[...] = s_end

    vec = pl.BlockSpec((tc, 1, RW_WIDTH), lambda i: (i, 0, 0))
    mat = pl.BlockSpec((tc, RW_N, RW_WIDTH), lambda i: (i, 0, 0))
    msh = jax.ShapeDtypeStruct((T, RW_N, RW_WIDTH), F32)
    return pl.pallas_call(
        body, name=name + '_f', grid=(T // tc,),
        in_specs=[vec] * 6 + [pl.BlockSpec((RW_N, RW_WIDTH), lambda i: (0, 0))],
        out_specs=[vec, mat, mat, mat],
        out_shape=[jax.ShapeDtypeStruct((T, 1, RW_WIDTH), F32), msh, msh, msh],
        scratch_shapes=[pltpu.VMEM((RW_N, RW_WIDTH), F32)],
        compiler_params=_cparams(('arbitrary',)),
    )(r, w, k, v, a, b, dg)


def _scan_bwd_call(r, w, k, a, b, st, sa_all, vc_all, dy, name):
    T = r.shape[0]
    tc = min(SCAN_CHUNK, T)
    nt = T // tc
    dg = _scan_diag()

    def body(r_ref, w_ref, k_ref, a_ref, b_ref, st_ref, sa_ref, vc_ref, dy_ref, dg_ref,
             dr_ref, dw_ref, dk_ref, dv_ref, da_ref, db_ref, ds_ref):
        @pl.when(pl.program_id(0) == 0)
        def _():
            ds_ref[...] = jnp.zeros(ds_ref.shape, F32)

        dgv = dg_ref[...]
        colsum = lambda x: jnp.sum(x, axis=0, keepdims=True)

        def step(i, carry):
            ds, dycol = carry
            t = tc - 1 - i
            sp = st_ref[t]
            rt, wt, kt, at, bt = r_ref[t], w_ref[t], k_ref[t], a_ref[t], b_ref[t]
            ds = ds + dycol * rt
            dsa = _head_rowsum(ds * bt)
            sa, vcol = sa_ref[t], vc_ref[t]
            dycol_next = _head_rowsum(dgv * dy_ref[jnp.maximum(t - 1, 0)])
            sn = sp * wt + sa * bt + vcol * kt
            dr_ref[t] = colsum(sn * dycol)
            dk_ref[t] = colsum(ds * vcol)
            db_ref[t] = colsum(ds * sa)
            dw_ref[t] = colsum(ds * sp)
            dv_ref[t] = colsum(_head_rowsum(ds * kt) * dgv)
            da_ref[t] = colsum(sp * dsa)
            return ds * wt + dsa * at, dycol_next

        ds_end, _ = _unrolled_loop(tc, step, (ds_ref[...], _head_rowsum(dgv * dy_ref[tc - 1])))
        ds_ref[...] = ds_end

    vec = pl.BlockSpec((tc, 1, RW_WIDTH), lambda i: (nt - 1 - i, 0, 0))
    mat = pl.BlockSpec((tc, RW_N, RW_WIDTH), lambda i: (nt - 1 - i, 0, 0))
    vsh = jax.ShapeDtypeStruct((T, 1, RW_WIDTH), F32)
    return pl.pallas_call(
        body, name=name + '_b', grid=(nt,),
        in_specs=[vec] * 5 + [mat] * 3 + [vec, pl.BlockSpec((RW_N, RW_WIDTH), lambda i: (0, 0))],
        out_specs=[vec] * 6, out_shape=[vsh] * 6,
        scratch_shapes=[pltpu.VMEM((RW_N, RW_WIDTH), F32)],
        compiler_params=_cparams(('arbitrary',)),
    )(r, w, k, a, b, st, sa_all, vc_all, dy, dg)


@functools.partial(jax.custom_vjp, nondiff_argnums=(6,))
def wkv7(r, w, k, v, a, b, name):
    return _scan_fwd_call(r, w, k, v, a, b, name)[0]


def _wkv7_fwd(r, w, k, v, a, b, name):
    y, st, sa_all, vc_all = _scan_fwd_call(r, w, k, v, a, b, name)
    return y, (r, w, k, a, b, st, sa_all, vc_all)


def _wkv7_bwd(name, res, dy):
    return tuple(_scan_bwd_call(*res, dy, name))


wkv7.defvjp(_wkv7_fwd, _wkv7_bwd)


def _np_bf16(a):
    return jnp.asarray(a, BF16)


def _mla_consts():
    seg_n = (np.arange(512)[:, None] // QK_NOPE == np.arange(LANES)[None, :]).astype(np.float32)
    seg_r = (np.arange(LANES)[:, None] // 16 == np.arange(LANES)[None, :]).astype(np.float32)
    e1 = np.zeros((LANES, LANES), np.float32)
    e2 = np.zeros((LANES, LANES), np.float32)
    for h in range(MLA_HEADS):
        for i in range(16):
            e1[i, h * 16 + i] = 1.0
            e2[16 + i, h * 16 + i] = 1.0
    mats = [seg_n, seg_n.T, seg_r, seg_r.T, e1, e1.T, e2, e2.T]
    return [_np_bf16(m) for m in mats]


def _qk_prep_fn(qn, q1, q2, kn, kx, cos, sin, gqn, gq1, gq2, gkn, gk1, gk2,
                seg_n, seg_nt, seg_r, seg_rt, e1, e1t, e2, e2t):
    def normrope(xn, x1, x2, gn, g1, g2):
        ss = sdot(xn * xn, seg_n, seg_nt) + sdot(x1 * x1, seg_r, seg_rt) + sdot(x2 * x2, seg_r, seg_rt)
        inv = lax.rsqrt(ss * (1.0 / QK_HEAD) + NORM_EPS)
        inv_n = sdot(inv, seg_nt, seg_n)
        inv_r = sdot(inv, seg_rt, seg_r)
        y1 = x1 * inv_r * g1
        y2 = x2 * inv_r * g2
        return xn * inv_n * gn, y1 * cos - y2 * sin, y1 * sin + y2 * cos

    k1 = sdot(kx, e1, e1t)
    k2 = sdot(kx, e2, e2t)
    return normrope(qn, q1, q2, gqn, gq1, gq2) + normrope(kn, k1, k2, gkn, gk1, gk2)


def _rwkv_prep_fn(vres):
    def fn(r, k, v, xg, xwa, kx, *rest):
        if vres:
            vfirst, w0, a0, k_k, k_a, w2p, a2p, g2, v0, v2p, bm = rest
        else:
            w0, a0, k_k, k_a, w2p, a2p, g2, bm = rest
        z = w0 + bdot(jnp.tanh(xwa), w2p)
        nz = -z
        softplus = jnp.maximum(nz, 0.0) + jnp.log(1.0 + jnp.exp(-jnp.abs(nz)))
        decay = jnp.exp(-jnp.exp(-softplus - 0.5))
        a = _sigmoid(a0 + bdot(xwa, a2p))
        g = bdot(_sigmoid(xg), g2)
        if vres:
            vv = v + (vfirst - v) * _sigmoid(v0 + bdot(kx, v2p))
        else:
            vv = v
        kkr = k * k_k
        kk = kkr / jnp.maximum(jnp.sqrt(sdot(kkr * kkr, bm, bm)), 1e-12)
        k2 = k * (1.0 + (a - 1.0) * k_a)
        return r * 1.0, decay, k2, vv, -kk, kk * a, g
    return fn


def _rwkv_post_fn(y, r, k2, vv, g, ln_w, ln_b, rk, bm):
    inv_n = 1.0 / RW_N
    mean = sdot(y, bm, bm) * inv_n
    yc = y - mean
    var = sdot(yc * yc, bm, bm) * inv_n
    yn = yc * lax.rsqrt(var + GN_EPS) * ln_w + ln_b
    bonus = sdot(r * k2 * rk, bm, bm) * vv
    return ((yn + bonus) * g,)


def _merge_fn(g0, g1, g2, oa, ob, oc):
    return (_sigmoid(g0) * oa + _sigmoid(g1) * ob + _sigmoid(g2) * oc,)


def _loss_call(y, target):
    T, C = y.shape
    tb = min(256, T)

    def body(y_ref, t_ref, dy_ref, part_ref):
        err = y_ref[...] - t_ref[...]
        dy_ref[...] = err * (1.0 / C)
        sq = jnp.sum(err * err, axis=0, keepdims=True)
        acc = sq[:, 0:LANES]
        for j in range(1, C // LANES):
            acc = acc + sq[:, j * LANES:(j + 1) * LANES]
        part_ref[...] = jnp.zeros(part_ref.shape, F32)
        part_ref[0:1, :] = acc * (0.5 / C)

    return pl.pallas_call(
        body, name='loss', grid=(T // tb,),
        in_specs=[pl.BlockSpec((tb, C), lambda i: (i, 0))] * 2,
        out_specs=[pl.BlockSpec((tb, C), lambda i: (i, 0)), pl.BlockSpec((8, LANES), lambda i: (i, 0))],
        out_shape=[jax.ShapeDtypeStruct((T, C), F32), jax.ShapeDtypeStruct((8 * (T // tb), LANES), F32)],
        compiler_params=_cparams(('parallel',)),
    )(y, target)


def _pad_rows(t, before, total):
    return jnp.pad(t, ((before, total - before - t.shape[0]), (0, 0)))


def _head_tile(g, lo, hi):
    return jnp.tile(g[lo:hi], MLA_HEADS).reshape(1, -1)


def _layer(l, x, v_first, wd, sp, cos, sin):
    T = x.shape[0]
    nm = f'l{l}'
    vres = l > 0
    w_in = wd['w_in']
    if vres:
        v1t = wd['rwkv_v1'].map(lambda t: t.T)
    else:
        v1t = W(jnp.zeros((MV_LORA, D_MODEL), BF16), jnp.zeros((MV_LORA, D_MODEL), BF16))
    zpad = W(jnp.zeros((64, D_MODEL), BF16), jnp.zeros((64, D_MODEL), BF16))
    w_rw = _wcat([w_in['rkv'], w_in['xg'], w_in['xwa'], w_in['kpe'], v1t, zpad], 0)
    gates, cq, ckv, rw, cv = norm_projections(
        x, sp['attn_norm'], [w_in['gates'], w_in['cq'], w_in['ckv'], w_rw, w_in['conv']], nm + '_win', transposed=True)

    v_mu = sp['rwkv_v_mu'] if vres else jnp.zeros((MV_LORA,), F32)
    mu_all = jnp.concatenate([sp['rwkv_mu'][0:768], sp['rwkv_mu'][896:1024], sp['rwkv_mu'][768:896],
                              jnp.zeros((QK_ROPE,), F32), v_mu, jnp.zeros((64,), F32)]).reshape(1, -1)
    rws = token_shift_mix(rw, mu_all, nm + '_shift')

    wq = wd['mla_wq_b'].map(lambda t: jnp.concatenate(
        [t.reshape(Q_LORA, MLA_HEADS, QK_HEAD)[:, :, 0:64].reshape(Q_LORA, 512),
         t.reshape(Q_LORA, MLA_HEADS, QK_HEAD)[:, :, 64:80].reshape(Q_LORA, 128),
         t.reshape(Q_LORA, MLA_HEADS, QK_HEAD)[:, :, 80:96].reshape(Q_LORA, 128)], axis=1))
    wkn = wd['mla_wkv_b'].map(lambda t: t.reshape(KV_LORA, MLA_HEADS, 128)[:, :, 0:64].reshape(KV_LORA, 512))
    wv = wd['mla_wkv_b'].map(lambda t: t.reshape(KV_LORA, MLA_HEADS, 128)[:, :, 64:128].reshape(KV_LORA, 512))
    q, = norm_projections(cq, sp['mla_q_a_norm'], [wq], nm + '_wq')
    kn, vv_att = norm_projections(ckv, sp['mla_kv_a_norm'], [wkn, wv], nm + '_wkv')
    gq, gk = sp['mla_q_norm'], sp['mla_k_norm']
    consts = _mla_consts()
    qk_specs = ([ROW(pieces=((0, 512), (512, 128), (640, 128))), ROW(), ROW(pieces=((1024, 128),)),
                 ROW(False), ROW(False)] + [FULL()] * 6 + [FULL(False)] * 8)
    qk_op = stage_op(_qk_prep_fn, qk_specs, [512, 128, 128, 512, 128, 128], nm + '_qkprep')
    Qn, Q1, Q2, Kn, K1, K2 = qk_op(q, kn, rws, cos, sin,
                                   _head_tile(gq, 0, 64), _head_tile(gq, 64, 80), _head_tile(gq, 80, 96),
                                   _head_tile(gk, 0, 64), _head_tile(gk, 64, 80), _head_tile(gk, 80, 96), *consts)
    o_att = attention(Qn, Q1, Q2, Kn, K1, K2, vv_att, nm + '_att')
    o_a = linear(o_att, wd['mla_w_o'], name=nm + '_wo')

    bm = _np_bf16(_block_ones(RW_WIDTH, RW_N))
    vec = lambda n: sp[n].reshape(1, -1)
    f32w = lambda n: wd[n].c + wd[n].b.astype(F32)
    w2p = _pad_rows(f32w('rwkv_w2'), 0, 128)
    a2p = _pad_rows(f32w('rwkv_a2'), 64, 128)
    g2 = f32w('rwkv_g2')
    rw_pieces = ((0, 256), (256, 256), (512, 256), (768, 128), (896, 128), (1024, 128))
    if vres:
        v2p = _pad_rows(f32w('rwkv_v2'), 32, 128)
        prep_specs = [ROW(pieces=rw_pieces), ROW()] + [FULL()] * 9 + [FULL(False)]
        prep_in = [rws, v_first, vec('rwkv_w0'), vec('rwkv_a0'), vec('rwkv_k_k'), vec('rwkv_k_a'), w2p, a2p, g2,
                   vec('rwkv_v0'), v2p, bm]
    else:
        prep_specs = [ROW(pieces=rw_pieces)] + [FULL()] * 7 + [FULL(False)]
        prep_in = [rws, vec('rwkv_w0'), vec('rwkv_a0'), vec('rwkv_k_k'), vec('rwkv_k_a'), w2p, a2p, g2, bm]
    prep_op = stage_op(_rwkv_prep_fn(vres), prep_specs, [256] * 7, nm + '_rwprep')
    r_, dec, k2, vv, an, bn, g = prep_op(*prep_in)
    if not vres:
        v_first = vv
    t3 = lambda t: t.reshape(T, 1, RW_WIDTH)
    y = wkv7(t3(r_), t3(dec), t3(k2), t3(vv), t3(an), t3(bn), nm + '_scan').reshape(T, RW_WIDTH)
    post_op = stage_op(_rwkv_post_fn, [ROW()] * 5 + [FULL()] * 3 + [FULL(False)], [256], nm + '_rwpost')
    yb = post_op(y, r_, k2, vv, g, vec('rwkv_ln_w'), vec('rwkv_ln_b'), sp['rwkv_r_k'].reshape(1, -1), bm)[0]
    o_b = linear(yb, wd['rwkv_w_o'], name=nm + '_rwo')

    oc_in = short_conv(cv, f32w('conv_w'), nm + '_conv')
    o_c = linear(oc_in, wd['conv_w_o'], name=nm + '_cwo')

    merge_op = stage_op(_merge_fn, [ROW(pieces=((0, 1024), (1024, 1024), (2048, 1024))), ROW(), ROW(), ROW()],
                        [D_MODEL], nm + '_merge')
    merged = merge_op(gates, o_a, o_b, o_c)[0]
    x2 = linear(merged, wd['w_out'], add=x, name=nm + '_wout')
    return x2, v_first


def _mlp(l, x, wd, sp):
    return mlp_block(x, sp['mlp_norm'].reshape(1, -1), wd['w_up'].b, wd['w_up'].c, wd['w_down'].b, wd['w_down'].c,
                     f'l{l}_mlp')


MLP_WEIGHTS = ('w_up', 'w_down')


def _entries(ml):
    out = []
    for name, (layers, shape, axis) in SHARDED.items():
        l = ml if layers == DEPTH else ml - 1
        if not 0 <= l < layers or name in MLP_WEIGHTS:
            continue
        n = shape[0] * shape[1] // N_DEV
        if name == 'conv_w':
            out.append(('conv_w_hi', name, l, n))
            out.append(('conv_w_lo', name, l, n))
        else:
            out.append((name, name, l, n))
    return out


def _slot_size(n):
    return -(-n // LANES) * LANES


def _pack_rows(ml):
    total = sum(_slot_size(n) for _, _, _, n in _entries(ml))
    rows = -(-total // LANES)
    return -(-rows // PACK_ROW_MULT) * PACK_ROW_MULT


def _pack_flat(pieces, ml):
    rows = _pack_rows(ml)
    padded = []
    for p, (_, _, _, n) in zip(pieces, _entries(ml)):
        pad = _slot_size(n) - n
        if pad:
            p = jnp.pad(p, [(0, 0)] * (p.ndim - 1) + [(0, pad)])
        padded.append(p)
    flat = jnp.concatenate(padded, axis=-1)
    tail = rows * LANES - flat.shape[-1]
    if tail:
        flat = jnp.pad(flat, [(0, 0)] * (flat.ndim - 1) + [(0, tail)])
    return flat.reshape(flat.shape[:-1] + (rows, LANES))


def _unpack_flat(buf, ml):
    out, row = [], 0
    for _, _, _, n in _entries(ml):
        nrows = _slot_size(n) // LANES
        piece = buf[..., row:row + nrows, :].reshape(buf.shape[:-2] + (-1,))
        out.append(piece[..., :n])
        row += nrows
    return out


def _pack_shards(shards, ml, dtype, split_conv):
    pieces = []
    for slot, name, l, n in _entries(ml):
        a = shards[name][l]
        if name in TRANSPOSED:
            a = a.astype(dtype).T
        a = a.reshape(-1)
        if slot == 'conv_w_hi':
            a = a.astype(BF16).astype(F32) if split_conv else a
        elif slot == 'conv_w_lo':
            a = (a - a.astype(BF16).astype(F32)) if split_conv else jnp.zeros_like(a)
        pieces.append(a.astype(dtype))
    return _pack_flat(pieces, ml)


def _unpack_shards(buf, ml):
    out = {}
    for (slot, name, l, n), v in zip(_entries(ml), _unpack_flat(buf, ml)):
        if slot == 'conv_w_lo':
            continue
        layers, shape, axis = SHARDED[name]
        sshape = (shape[0] // N_DEV, shape[1]) if axis == 0 else (shape[0], shape[1] // N_DEV)
        out[(name, l)] = v.reshape(sshape[::-1]).T if name in TRANSPOSED else v.reshape(sshape)
    return out


def _to_full(blocks, shape, axis):
    if axis == 0:
        return blocks.reshape(shape)
    return blocks.reshape(N_DEV, shape[0], shape[1] // N_DEV).transpose(1, 0, 2).reshape(shape)


def _to_blocks(full, axis):
    r, c = full.shape
    if axis == 0:
        return full.reshape(N_DEV, -1)
    return full.reshape(r, N_DEV, c // N_DEV).transpose(1, 0, 2).reshape(N_DEV, -1)


def _unpack_gathered(gathered, ml):
    out, conv_hi = {}, None
    for (slot, name, l, n), v in zip(_entries(ml), _unpack_flat(gathered, ml)):
        layers, shape, axis = SHARDED[name]
        if name in TRANSPOSED:
            out[name] = v.reshape(-1, shape[0])
            continue
        full = _to_full(v, shape, axis)
        if slot == 'conv_w_hi':
            conv_hi = full
        elif slot == 'conv_w_lo':
            out[name] = conv_hi.astype(F32) + full.astype(F32)
        else:
            out[name] = full
    return out


def _pack_grads(grads, ml):
    pieces = []
    for slot, name, l, n in _entries(ml):
        if name in TRANSPOSED:
            blocks = jnp.concatenate(grads[name], axis=0).reshape(N_DEV, -1)
        else:
            blocks = _to_blocks(grads[name], SHARDED[name][2])
        if slot == 'conv_w_lo':
            blocks = jnp.zeros_like(blocks)
        pieces.append(blocks.astype(BF16))
    return _pack_flat(pieces, ml)


def _my_pos():
    return lax.axis_index('x'), lax.axis_index('y'), lax.axis_index('c')


def _flip(v, bit):
    return 1 - v if bit else v


def all_gather_blocks(x):
    rows = x.shape[0]

    def body(x_ref, out_ref, send_sems, recv_sems, local_sem):
        mx, my, mc = _my_pos()
        me, sibling = (mx, my, mc), (mx, my, 1 - mc)
        chips = [(1 - mx, my), (mx, 1 - my), (1 - mx, 1 - my)]

        def block(px, py, pc):
            return out_ref.at[4 * px + 2 * py + pc]

        def copy(k, blk, to, src=None):
            return pltpu.make_async_remote_copy(
                src_ref=block(*blk) if src is None else src, dst_ref=block(*blk),
                send_sem=send_sems.at[k], recv_sem=recv_sems.at[k], device_id=to, device_id_type=MESH)

        mine = pltpu.make_async_copy(x_ref, block(*me), local_sem)
        mine.start()
        first = [copy(0, me, sibling, src=x_ref)]
        first += [copy(1 + j, me, (*chip, mc), src=x_ref) for j, chip in enumerate(chips)]
        for cp in first:
            cp.start()
        passed = [copy(4 + j, (*chip, mc), sibling) for j, chip in enumerate(chips)]
        for j, chip in enumerate(chips):
            copy(1 + j, (*chip, mc), me).wait_recv()
            passed[j].start()
        copy(0, sibling, me).wait_recv()
        for j, chip in enumerate(chips):
            copy(4 + j, (*chip, 1 - mc), me).wait_recv()
        for cp in first + passed:
            cp.wait_send()
        mine.wait()

    return pl.pallas_call(
        body, name='all_gather_weights',
        out_shape=jax.ShapeDtypeStruct((N_DEV, rows, LANES), x.dtype),
        in_specs=[pl.BlockSpec(memory_space=pl.ANY)], out_specs=pl.BlockSpec(memory_space=pl.ANY),
        scratch_shapes=[pltpu.SemaphoreType.DMA((7,)), pltpu.SemaphoreType.DMA((7,)), pltpu.SemaphoreType.DMA],
    )(x)


HBM_SPEC = pl.BlockSpec(memory_space=pltpu.HBM)
SEM_SPEC = pl.BlockSpec(memory_space=pltpu.SEMAPHORE)
DATAFLOW_EFFECT = pltpu.SideEffectType.DATAFLOW_SIDE_EFFECTING


def _direct_copies(src_ref, land_ref, send_sems, recv_sems, per_peer):
    mx, my, mc = _my_pos()
    me = 4 * mx + 2 * my + mc
    copies = []
    for k in range(1, N_DEV):
        peer = (_flip(mx, k & 4), _flip(my, k & 2), _flip(mc, k & 1))
        pidx = 4 * peer[0] + 2 * peer[1] + peer[2]
        copies.append(pltpu.make_async_remote_copy(
            src_ref=src_ref.at[pidx] if per_peer else src_ref, dst_ref=land_ref.at[me],
            send_sem=send_sems.at[k - 1], recv_sem=recv_sems.at[k - 1], device_id=peer, device_id_type=MESH))
    return copies


def send_start(src, per_peer, name):
    block = src.shape[1:] if per_peer else src.shape
    land_shape = (N_DEV,) + tuple(block)

    def body(src_ref, land_ref, send_sems, recv_sems, src_thru, land_thru, token):
        for cp in _direct_copies(src_ref, land_ref, send_sems, recv_sems, per_peer):
            cp.start()
        token[...] = jnp.zeros(token.shape, F32)

    send_sems, recv_sems, src_thru, land_thru, token = pl.pallas_call(
        body, name=name,
        out_shape=(pltpu.SemaphoreType.DMA((N_DEV - 1,)), pltpu.SemaphoreType.DMA((N_DEV - 1,)),
                   pltpu.HBM(src.shape, src.dtype), pltpu.HBM(land_shape, src.dtype),
                   jax.ShapeDtypeStruct((8, LANES), F32)),
        in_specs=(HBM_SPEC, HBM_SPEC),
        out_specs=(SEM_SPEC, SEM_SPEC, HBM_SPEC, HBM_SPEC, pl.BlockSpec(memory_space=pltpu.VMEM)),
        input_output_aliases={0: 2, 1: 3},
        compiler_params=pltpu.CompilerParams(has_side_effects=DATAFLOW_EFFECT),
    )(pltpu.with_memory_space_constraint(src, pltpu.HBM),
      pltpu.with_memory_space_constraint(lax.empty(land_shape, src.dtype), pltpu.HBM))
    return (send_sems, recv_sems, src_thru, land_thru), token[0, 0]


def send_wait(handles, after, per_peer, name):
    send_sems, recv_sems, src_thru, land_thru = handles

    def body(src_ref, land_ref, send_sems, recv_sems, after_ref, src_dead, got_ref):
        for cp in _direct_copies(src_ref, land_ref, send_sems, recv_sems, per_peer):
            cp.wait_send()
            cp.wait_recv()

    return pl.pallas_call(
        body, name=name,
        out_shape=(pltpu.HBM(src_thru.shape, src_thru.dtype), pltpu.HBM(land_thru.shape, land_thru.dtype)),
        in_specs=(HBM_SPEC, HBM_SPEC, SEM_SPEC, SEM_SPEC, pl.BlockSpec(memory_space=pl.ANY)),
        out_specs=(HBM_SPEC, HBM_SPEC), input_output_aliases={0: 0, 1: 1},
        compiler_params=pltpu.CompilerParams(has_side_effects=DATAFLOW_EFFECT),
    )(src_thru, land_thru, send_sems, recv_sems, after)[1]


def _adamw_math(w, g, m, v):
    m2 = ADAM_B1 * m + (1.0 - ADAM_B1) * g
    v2 = ADAM_B2 * v + (1.0 - ADAM_B2) * (g * g)
    m_hat = m2 / (1.0 - ADAM_B1 ** ADAM_STEP)
    v_hat = v2 / (1.0 - ADAM_B2 ** ADAM_STEP)
    delta = -ADAM_LR * (m_hat / (jnp.sqrt(v_hat) + ADAM_EPS) + ADAM_WD * w)
    return delta, m2, v2


def sum_parts(parts, name):
    _, rows, cols = parts.shape
    rb = rows
    while N_DEV * rb * cols * 2 > (2 << 20) and rb % 32 == 0:
        rb //= 2

    def body(p_ref, g_ref):
        g = p_ref[0].astype(F32)
        for j in range(1, N_DEV):
            g = g + p_ref[j].astype(F32)
        g_ref[...] = g

    return pl.pallas_call(
        body, name=name, grid=(rows // rb,),
        in_specs=[pl.BlockSpec((N_DEV, rb, cols), lambda i: (0, i, 0))],
        out_specs=pl.BlockSpec((rb, cols), lambda i: (i, 0)),
        out_shape=jax.ShapeDtypeStruct((rows, cols), F32), compiler_params=_cparams(('parallel',)),
    )(parts)


ADAMW_BLOCK_BYTES = 1 << 20


def adamw_weight(name, w, m, v, grads):
    layers, a, b = w.shape
    ra = a
    while ra * b * 4 > ADAMW_BLOCK_BYTES and ra % 16 == 0:
        ra //= 2

    def body(*refs):
        w_ref, m_ref, v_ref = refs[:3]
        g_refs = refs[3:3 + layers]
        g_ref, d_ref, m2_ref, v2_ref = refs[3 + layers:]
        g = g_refs[0][...]
        for l in range(1, layers):
            g = jnp.where(pl.program_id(0) == l, g_refs[l][...], g)
        delta, m2, v2 = _adamw_math(w_ref[0], g, m_ref[0], v_ref[0])
        g_ref[0] = g
        d_ref[0] = delta
        m2_ref[0] = m2
        v2_ref[0] = v2

    blk = pl.BlockSpec((1, ra, b), lambda l, i: (l, i, 0))
    gblk = pl.BlockSpec((ra, b), lambda l, i: (i, 0))
    sh = jax.ShapeDtypeStruct(w.shape, F32)
    return pl.pallas_call(
        body, name='adamw_' + name, grid=(layers, a // ra), in_specs=[blk] * 3 + [gblk] * layers,
        out_specs=[blk] * 4, out_shape=[sh] * 4, compiler_params=_cparams(('parallel', 'parallel')),
    )(w, m, v, *grads)


def allreduce_adamw_small(g, w, m, v):
    rows = g.shape[0]

    def body(g_ref, w_ref, m_ref, v_ref, gs_ref, d_ref, m2_ref, v2_ref, all_ref, send_sems, recv_sems):
        mx, my, mc = _my_pos()
        me, sibling = (mx, my, mc), (mx, my, 1 - mc)
        chips = [(1 - mx, my), (mx, 1 - my), (1 - mx, 1 - my)]

        def block(px, py, pc):
            return all_ref.at[4 * px + 2 * py + pc]

        def copy(k, blk, to, src=None):
            return pltpu.make_async_remote_copy(
                src_ref=block(*blk) if src is None else src, dst_ref=block(*blk),
                send_sem=send_sems.at[k], recv_sem=recv_sems.at[k], device_id=to, device_id_type=MESH)

        first = [copy(0, me, sibling, src=g_ref)]
        first += [copy(1 + j, me, (*chip, mc), src=g_ref) for j, chip in enumerate(chips)]
        for cp in first:
            cp.start()
        passed = [copy(4 + j, (*chip, mc), sibling) for j, chip in enumerate(chips)]
        for j, chip in enumerate(chips):
            copy(1 + j, (*chip, mc), me).wait_recv()
            passed[j].start()
        copy(0, sibling, me).wait_recv()
        for j, chip in enumerate(chips):
            copy(4 + j, (*chip, 1 - mc), me).wait_recv()
        for cp in first + passed:
            cp.wait_send()
        my_idx = 4 * mx + 2 * my + mc
        total = jnp.zeros((rows, LANES), F32)
        for j in range(N_DEV):
            total = total + jnp.where(my_idx == j, g_ref[...], all_ref[j])
        delta, m2, v2 = _adamw_math(w_ref[...], total, m_ref[...], v_ref[...])
        gs_ref[...] = total
        d_ref[...] = delta
        m2_ref[...] = m2
        v2_ref[...] = v2

    vm = pl.BlockSpec(memory_space=pltpu.VMEM)
    sh = jax.ShapeDtypeStruct((rows, LANES), F32)
    return pl.pallas_call(
        body, name='allreduce_adamw_small', in_specs=[vm] * 4, out_specs=[vm] * 4, out_shape=[sh] * 4,
        scratch_shapes=[pltpu.VMEM((N_DEV, rows, LANES), F32), pltpu.SemaphoreType.DMA((7,)),
                        pltpu.SemaphoreType.DMA((7,))],
    )(g, w, m, v)


SMALL_COUNT = 11680


def _small_pack(d, extra=None):
    assert sum(d[n].size for n in SMALL_NAMES) == SMALL_COUNT
    flat = jnp.concatenate([d[n].reshape(-1) for n in SMALL_NAMES] + ([] if extra is None else [extra.reshape(1)]))
    rows = -(-(SMALL_COUNT + 1) // (8 * LANES)) * 8
    return jnp.pad(flat, (0, rows * LANES - flat.shape[0])).reshape(rows, LANES)


def _small_unpack(buf, like):
    flat = buf.reshape(-1)
    out, off = {}, 0
    for n in SMALL_NAMES:
        sz = int(np.prod(like[n].shape))
        out[n] = flat[off:off + sz].reshape(like[n].shape)
        off += sz
    return out


def _rope_tables(positions):
    freqs = ROPE_THETA ** (-(jnp.arange(QK_ROPE // 2, dtype=F32) * 2.0 / QK_ROPE))
    ang = positions.astype(F32)[:, None] * freqs
    return jnp.tile(jnp.cos(ang), (1, MLA_HEADS)), jnp.tile(jnp.sin(ang), (1, MLA_HEADS))


def _layer_weights(gathered, carriers):
    wd = {}
    for name, full in gathered.items():
        if name == 'w_in':
            wd[name] = {seg: W(full[lo:hi], c) for (seg, lo, hi), c in zip(WIN_SEGS, carriers[name])}
        else:
            wd[name] = W(full, carriers[name])
    return wd


F32_GRAD_WEIGHTS = ('rwkv_w2', 'rwkv_a2', 'rwkv_g2', 'rwkv_v2', 'conv_w')


def _make_carriers(gathered):
    gathered, carriers = dict(gathered), {}
    for name, full in gathered.items():
        if name == 'w_in':
            carriers[name] = tuple(jnp.zeros((hi - lo, D_MODEL), BF16) for _, lo, hi in WIN_SEGS)
        elif name == 'conv_w':
            gathered[name] = full.astype(BF16)
            carriers[name] = full - full.astype(BF16).astype(F32)
        else:
            carriers[name] = jnp.zeros(full.shape, F32 if name in F32_GRAD_WEIGHTS else BF16)
    return gathered, carriers


def _layer_small(small, ml):
    out = {}
    for n in SMALL_NAMES:
        l = ml if small[n].shape[0] == DEPTH else ml - 1
        if 0 <= l < small[n].shape[0]:
            out[n] = small[n][l]
    return out


def _after(value, tokens):
    return lax.optimization_barrier((value, *tokens))[0]


def _train_step(x, positions, loss_target, weights, moms_m, moms_v):
    shards = {n: weights[n] for n in SHARDED}
    small = {n: weights[n] for n in SMALL_NAMES}
    me = 4 * lax.axis_index('x') + 2 * lax.axis_index('y') + lax.axis_index('c')
    cos, sin = _rope_tables(positions[0])

    units = [(kind, l) for l in range(DEPTH) for kind in ('pack',) + MLP_WEIGHTS]
    first, later = units[0], units[1:]
    tag = lambda u: f'{u[0]}_l{u[1]}'

    def own_block(u):
        kind, l = u
        return _pack_shards(shards, l, BF16, True) if kind == 'pack' else shards[kind][l].astype(BF16)

    own = {u: own_block(u) for u in units}
    held = lax.optimization_barrier((all_gather_blocks(own[first]), *[own[u] for u in later]))
    blocks, gathers, tokens = {first: held[0]}, {}, []
    for u, mine in zip(later, held[1:]):
        own[u] = mine
        gathers[u], token = send_start(mine, False, f'gather_{tag(u)}_start')
        tokens.append(token)
    x0 = _after(x[0], tokens)

    def gathered_block(u, after):
        if u not in blocks:
            landed = send_wait(gathers[u], after, False, f'gather_{tag(u)}_wait')
            blocks[u] = lax.dynamic_update_slice(landed, own[u][None], (me,) + (0,) * own[u].ndim)
        return blocks[u]

    def mixer_weights(l, after):
        return _make_carriers(_unpack_gathered(gathered_block(('pack', l), after), l))

    def mlp_weights(l, after):
        up = gathered_block(('w_up', l), after)
        down = gathered_block(('w_down', l), after).reshape(D_FF, D_MODEL)
        return _make_carriers({'w_up': up, 'w_down': down})

    def small_of(layer, part):
        sp = _layer_small(small, layer)
        return {n: v for n, v in sp.items() if (n == 'mlp_norm') == (part == 'mlp')}

    def mixer(layer, gathered):
        if layer == 0:
            return lambda c, s, xx: _layer(0, xx, None, _layer_weights(gathered, c), s, cos, sin)
        return lambda c, s, xx, vf: _layer(layer, xx, vf, _layer_weights(gathered, c), s, cos, sin)[0]

    def mlp(layer, gathered):
        return lambda c, s, xx: _mlp(layer, xx, _layer_weights(gathered, c), s)

    gathered, carriers = mixer_weights(0, None)
    (h, v_first), vjp_mix0 = jax.vjp(mixer(0, gathered), carriers, small_of(0, 'mix'), x0)
    gathered, carriers = mlp_weights(0, h)
    h, vjp_mlp0 = jax.vjp(mlp(0, gathered), carriers, small_of(0, 'mlp'), h)
    gathered, carriers = mixer_weights(1, h)
    h, vjp_mix1 = jax.vjp(mixer(1, gathered), carriers, small_of(1, 'mix'), h, v_first)
    gathered, carriers = mlp_weights(1, h)
    y, vjp_mlp1 = jax.vjp(mlp(1, gathered), carriers, small_of(1, 'mlp'), h)

    dy, loss_parts = _loss_call(y, loss_target[0])

    shipped = {}

    def ship(u, to_send):
        handles, token = send_start(to_send, True, f'grads_{tag(u)}_start')
        shipped[u] = (handles, to_send)
        return token

    def ship_mlp(l, gw):
        down = gw['w_down'].reshape(N_DEV, D_FF // N_DEV, D_MODEL)
        return [ship(('w_up', l), gw['w_up']), ship(('w_down', l), down)]

    gw, gs_mlp1, d = vjp_mlp1(dy)
    d = _after(d, ship_mlp(1, gw))
    gw, gs_mix1, d, dvf = vjp_mix1(d)
    d = _after(d, [ship(('pack', 1), _pack_grads(gw, 1))])
    gw, gs_mlp0, d = vjp_mlp0(d)
    d = _after(d, ship_mlp(0, gw))
    gw, gs_mix0, g_x = vjp_mix0((d, dvf))
    last_token = ship(first, _pack_grads(gw, 0))
    gs0, gs1 = {**gs_mix0, **gs_mlp0}, {**gs_mix1, **gs_mlp1}

    def arrived(u, after):
        handles, sent = shipped[u]
        landed = send_wait(handles, after, True, f'grads_{tag(u)}_wait')
        mine = lax.dynamic_slice_in_dim(sent, me, 1, axis=0)
        landed = lax.dynamic_update_slice(landed, mine, (me,) + (0,) * (sent.ndim - 1))
        return sum_parts(landed, f'sum_grads_{tag(u)}')

    after = _after(g_x, [last_token])
    grads = {u: arrived(u, after) for u in later if u[0] != 'pack'}
    grads.update(_unpack_shards(arrived(('pack', 1), after), 1))
    sharded_out = [{}, {}, {}, {}]

    def update(n):
        outs = adamw_weight(n, weights[n], moms_m[n], moms_v[n], [grads[(n, l)] for l in range(SHARDED[n][0])])
        for i in range(4):
            sharded_out[i][n] = outs[i]

    for n in MLP_WEIGHTS:
        update(n)
    grads.update(_unpack_shards(arrived(first, sharded_out[0][MLP_WEIGHTS[-1]]), 0))
    for n in SHARDED:
        if n not in MLP_WEIGHTS:
            update(n)

    g_small = {}
    for n in SMALL_NAMES:
        per = [g[n] for g in (gs0, gs1) if n in g]
        g_small[n] = jnp.stack(per)
    small_bufs = allreduce_adamw_small(
        _small_pack(g_small, jnp.sum(loss_parts)), _small_pack(small), _small_pack({n: moms_m[n] for n in SMALL_NAMES}),
        _small_pack({n: moms_v[n] for n in SMALL_NAMES}))
    loss = small_bufs[0].reshape(-1)[SMALL_COUNT]
    small_out = [_small_unpack(b, small) for b in small_bufs]

    pick = lambda i: [sharded_out[i][n] if n in SHARDED else small_out[i][n] for n in WEIGHT_NAMES]
    return (loss, g_x[None], *pick(0), *pick(1), *pick(2), *pick(3))


def kernel(x, positions, attn_norm, w_in, mla_q_a_norm, mla_wq_b, mla_kv_a_norm, mla_wkv_b, mla_q_norm, mla_k_norm, mla_w_o, rwkv_mu, rwkv_w0, rwkv_w2, rwkv_a0, rwkv_a2, rwkv_g2, rwkv_k_k, rwkv_k_a, rwkv_r_k, rwkv_ln_w, rwkv_ln_b, rwkv_w_o, rwkv_v1, rwkv_v_mu, rwkv_v0, rwkv_v2, conv_w, conv_w_o, w_out, mlp_norm, w_up, w_down, loss_target, m_attn_norm, m_w_in, m_mla_q_a_norm, m_mla_wq_b, m_mla_kv_a_norm, m_mla_wkv_b, m_mla_q_norm, m_mla_k_norm, m_mla_w_o, m_rwkv_mu, m_rwkv_w0, m_rwkv_w2, m_rwkv_a0, m_rwkv_a2, m_rwkv_g2, m_rwkv_k_k, m_rwkv_k_a, m_rwkv_r_k, m_rwkv_ln_w, m_rwkv_ln_b, m_rwkv_w_o, m_rwkv_v1, m_rwkv_v_mu, m_rwkv_v0, m_rwkv_v2, m_conv_w, m_conv_w_o, m_w_out, m_mlp_norm, m_w_up, m_w_down, v_attn_norm, v_w_in, v_mla_q_a_norm, v_mla_wq_b, v_mla_kv_a_norm, v_mla_wkv_b, v_mla_q_norm, v_mla_k_norm, v_mla_w_o, v_rwkv_mu, v_rwkv_w0, v_rwkv_w2, v_rwkv_a0, v_rwkv_a2, v_rwkv_g2, v_rwkv_k_k, v_rwkv_k_a, v_rwkv_r_k, v_rwkv_ln_w, v_rwkv_ln_b, v_rwkv_w_o, v_rwkv_v1, v_rwkv_v_mu, v_rwkv_v0, v_rwkv_v2, v_conv_w, v_conv_w_o, v_w_out, v_mlp_norm, v_w_up, v_w_down):
    args = locals()
    weights = {n: args[n] for n in WEIGHT_NAMES}
    moms_m = {n: args['m_' + n] for n in WEIGHT_NAMES}
    moms_v = {n: args['v_' + n] for n in WEIGHT_NAMES}
    return _train_step(x, positions, loss_target, weights, moms_m, moms_v)
```

```python
import functools

import numpy as np
import jax
import jax.numpy as jnp
from jax import lax
from jax.experimental import pallas as pl
from jax.experimental.pallas import tpu as pltpu

F32 = jnp.float32
BF16 = jnp.bfloat16

N_DEV = 8
LANES = 128
D_MODEL = 1024
DEPTH = 2
MLA_HEADS = 8
QK_NOPE = 64
QK_ROPE = 32
QK_HEAD = QK_NOPE + QK_ROPE
V_HEAD = 64
Q_LORA = 384
KV_LORA = 256
ROPE_THETA = 10000.0
RW_HEADS = 4
RW_N = 64
RW_WIDTH = RW_HEADS * RW_N
MV_LORA = 32
GN_EPS = 64e-5
CONV_WIDTH = 256
D_FF = 4 * D_MODEL
NORM_EPS = 1e-6
ADAM_LR = 0.001
ADAM_B1 = 0.9
ADAM_B2 = 0.999
ADAM_EPS = 1e-08
ADAM_WD = 0.01
ADAM_STEP = 10

VMEM_LIMIT = 56 * 1024 * 1024
MESH = pl.DeviceIdType.MESH

WEIGHT_NAMES = ['attn_norm', 'w_in', 'mla_q_a_norm', 'mla_wq_b', 'mla_kv_a_norm', 'mla_wkv_b', 'mla_q_norm',
                'mla_k_norm', 'mla_w_o', 'rwkv_mu', 'rwkv_w0', 'rwkv_w2', 'rwkv_a0', 'rwkv_a2', 'rwkv_g2',
                'rwkv_k_k', 'rwkv_k_a', 'rwkv_r_k', 'rwkv_ln_w', 'rwkv_ln_b', 'rwkv_w_o', 'rwkv_v1',
                'rwkv_v_mu', 'rwkv_v0', 'rwkv_v2', 'conv_w', 'conv_w_o', 'w_out', 'mlp_norm', 'w_up', 'w_down']

SHARDED = {
    'w_in': (2, (1024, 5536), 1), 'mla_wq_b': (2, (384, 768), 1), 'mla_wkv_b': (2, (256, 1024), 1),
    'mla_w_o': (2, (512, 1024), 1), 'rwkv_w2': (2, (64, 256), 1), 'rwkv_a2': (2, (64, 256), 1),
    'rwkv_g2': (2, (128, 256), 1), 'rwkv_w_o': (2, (256, 1024), 1), 'conv_w': (2, (3, 256), 1),
    'conv_w_o': (2, (256, 1024), 1), 'w_out': (2, (1024, 1024), 0), 'w_up': (2, (1024, 4096), 1),
    'w_down': (2, (4096, 1024), 0), 'rwkv_v1': (1, (1024, 32), 0), 'rwkv_v2': (1, (32, 256), 1),
}
SMALL_NAMES = [n for n in WEIGHT_NAMES if n not in SHARDED]
TRANSPOSED = ('w_in',)
WIN_SEGS = (('gates', 0, 3072), ('cq', 3072, 3456), ('ckv', 3456, 3712), ('kpe', 3712, 3744), ('rkv', 3744, 4512),
            ('xwa', 4512, 4640), ('xg', 4640, 4768), ('conv', 4768, 5536))
PACK_ROW_MULT = 512


def _cparams(sem=None, **kw):
    if sem is not None:
        kw['dimension_semantics'] = sem
    return pltpu.CompilerParams(vmem_limit_bytes=VMEM_LIMIT, **kw)


def _pick(n, cands):
    for c in cands:
        if n % c == 0:
            return c
    raise ValueError(f'no tile for {n}')


def _mm_nn(a, b, add=None, name='mm_nn'):
    M, K = a.shape
    N = b.shape[1]
    tm = _pick(M, (1024, 512, 256, 128))
    tn = _pick(N, (512, 384, 256, 128))
    tk = _pick(K, (1024, 512, 384, 256, 128))
    nk = K // tk
    has_add = add is not None

    def body(*refs):
        if has_add:
            a_ref, b_ref, add_ref, o_ref, acc_ref = refs
        else:
            a_ref, b_ref, o_ref, acc_ref = refs
        kk = pl.program_id(2)
        part = jnp.dot(a_ref[...].astype(BF16), b_ref[...].astype(BF16), preferred_element_type=F32)

        @pl.when(kk == 0)
        def _():
            acc_ref[...] = part

        @pl.when(kk > 0)
        def _():
            acc_ref[...] += part

        @pl.when(kk == nk - 1)
        def _():
            if has_add:
                o_ref[...] = acc_ref[...] + add_ref[...]
            else:
                o_ref[...] = acc_ref[...]

    in_specs = [pl.BlockSpec((tm, tk), lambda i, j, k: (i, k)), pl.BlockSpec((tk, tn), lambda i, j, k: (k, j))]
    args = [a, b]
    if has_add:
        in_specs.append(pl.BlockSpec((tm, tn), lambda i, j, k: (i, j)))
        args.append(add)
    return pl.pallas_call(
        body, name=name, grid=(M // tm, N // tn, nk), in_specs=in_specs,
        out_specs=pl.BlockSpec((tm, tn), lambda i, j, k: (i, j)),
        out_shape=jax.ShapeDtypeStruct((M, N), F32),
        scratch_shapes=[pltpu.VMEM((tm, tn), F32)],
        compiler_params=_cparams(('parallel', 'parallel', 'arbitrary')),
    )(*args)


def _mm_nt(a, b, add=None, name='mm_nt'):
    M, N = a.shape
    blocked = b.ndim == 3
    K = b.shape[-2]
    tm = _pick(M, (1024, 512, 256, 128))
    tk = _pick(K, (512, 384, 256, 128))
    tn = N // N_DEV if blocked else _pick(N, (1024, 512, 384, 256, 128))
    nn = N // tn
    has_add = add is not None

    def body(*refs):
        if has_add:
            a_ref, b_ref, add_ref, o_ref, acc_ref = refs
        else:
            a_ref, b_ref, o_ref, acc_ref = refs
        kk = pl.program_id(2)
        part = lax.dot_general(a_ref[...].astype(BF16), b_ref[...].astype(BF16), (((1,), (1,)), ((), ())),
                               preferred_element_type=F32)

        @pl.when(kk == 0)
        def _():
            acc_ref[...] = part

        @pl.when(kk > 0)
        def _():
            acc_ref[...] += part

        @pl.when(kk == nn - 1)
        def _():
            if has_add:
                o_ref[...] = acc_ref[...] + add_ref[...]
            else:
                o_ref[...] = acc_ref[...]

    if blocked:
        b_spec = pl.BlockSpec((None, tk, tn), lambda i, j, k: (k, j, 0))
    else:
        b_spec = pl.BlockSpec((tk, tn), lambda i, j, k: (j, k))
    in_specs = [pl.BlockSpec((tm, tn), lambda i, j, k: (i, k)), b_spec]
    args = [a, b]
    if has_add:
        in_specs.append(pl.BlockSpec((tm, tk), lambda i, j, k: (i, j)))
        args.append(add)
    return pl.pallas_call(
        body, name=name, grid=(M // tm, K // tk, nn), in_specs=in_specs,
        out_specs=pl.BlockSpec((tm, tk), lambda i, j, k: (i, j)),
        out_shape=jax.ShapeDtypeStruct((M, K), F32),
        scratch_shapes=[pltpu.VMEM((tm, tk), F32)],
        compiler_params=_cparams(('parallel', 'parallel', 'arbitrary')),
    )(*args)


def _mm_tn(a, b, name='mm_tn', blocked=False):
    M, K = a.shape
    N = b.shape[1]
    tm = _pick(M, (1024, 512, 256, 128))
    tk = _pick(K, (512, 384, 256, 128))
    tn = N // N_DEV if blocked else _pick(N, (512, 384, 256, 128))
    nm = M // tm

    def body(a_ref, b_ref, o_ref, acc_ref):
        mm = pl.program_id(2)
        part = lax.dot_general(a_ref[...].astype(BF16), b_ref[...].astype(BF16), (((0,), (0,)), ((), ())),
                               preferred_element_type=F32)

        @pl.when(mm == 0)
        def _():
            acc_ref[...] = part

        @pl.when(mm > 0)
        def _():
            acc_ref[...] += part

        @pl.when(mm == nm - 1)
        def _():
            o_ref[...] = acc_ref[...].astype(BF16)

    if blocked:
        out_spec = pl.BlockSpec((None, tk, tn), lambda i, j, m: (j, i, 0))
        out_shape = jax.ShapeDtypeStruct((N_DEV, K, tn), BF16)
    else:
        out_spec = pl.BlockSpec((tk, tn), lambda i, j, m: (i, j))
        out_shape = jax.ShapeDtypeStruct((K, N), BF16)
    return pl.pallas_call(
        body, name=name, grid=(K // tk, N // tn, nm),
        in_specs=[pl.BlockSpec((tm, tk), lambda i, j, m: (m, i)), pl.BlockSpec((tm, tn), lambda i, j, m: (m, j))],
        out_specs=out_spec, out_shape=out_shape,
        scratch_shapes=[pltpu.VMEM((tk, tn), F32)],
        compiler_params=_cparams(('parallel', 'parallel', 'arbitrary')),
    )(a, b)


@functools.partial(jax.custom_vjp, nondiff_argnums=(4,))
def _linear_add(a, wb, wc, add, name):
    return _mm_nn(a, wb, add, name=name + '_f')


def _linear_add_fwd(a, wb, wc, add, name):
    return _mm_nn(a, wb, add, name=name + '_f'), (a, wb)


def _linear_add_bwd(name, res, dy):
    a, wb = res
    return _mm_nt(dy, wb, name=name + '_da'), None, _mm_tn(a, dy, name=name + '_dw'), dy


_linear_add.defvjp(_linear_add_fwd, _linear_add_bwd)


@functools.partial(jax.custom_vjp, nondiff_argnums=(3,))
def _multi_linear(a, wbs, wcs, name):
    return tuple(_mm_nn(a, wb, name=f'{name}_f{i}') for i, wb in enumerate(wbs))


def _multi_linear_fwd(a, wbs, wcs, name):
    return _multi_linear(a, wbs, wcs, name), (a, wbs)


def _multi_linear_bwd(name, res, dys):
    a, wbs = res
    da = None
    for i, (dy, wb) in enumerate(zip(dys, wbs)):
        da = _mm_nt(dy, wb, add=da, name=f'{name}_da{i}')
    dws = tuple(_mm_tn(a, dy, name=f'{name}_dw{i}') for i, dy in enumerate(dys))
    return da, None, dws


_multi_linear.defvjp(_multi_linear_fwd, _multi_linear_bwd)


class W:
    def __init__(self, b, c):
        self.b, self.c = b, c

    def map(self, fn):
        return W(fn(self.b), fn(self.c))


def _wcat(ws, axis):
    return W(jnp.concatenate([w.b for w in ws], axis), jnp.concatenate([w.c for w in ws], axis))


def linear(a, w, add=None, name='lin'):
    if add is None:
        return _multi_linear(a, (w.b,), (w.c,), name)[0]
    return _linear_add(a, w.b, w.c, add, name)


def ROW(diff=True, pieces=None):
    return ('row', diff, pieces)


def FULL(diff=True):
    return ('full', diff, None)


def _load_args(refs, specs):
    args, amap = [], []
    for i, (ref, (kind, diff, pieces)) in enumerate(zip(refs, specs)):
        if pieces is None:
            args.append(ref[...])
            amap.append((i, None))
        else:
            for (s, w) in pieces:
                args.append(ref[:, s:s + w])
                amap.append((i, (s, w)))
    return args, amap


def _stage_in_specs(ins, specs, tb):
    out = []
    for a, (kind, _, _) in zip(ins, specs):
        if kind == 'row':
            out.append(pl.BlockSpec((tb, a.shape[1]), lambda i: (i, 0)))
        else:
            out.append(pl.BlockSpec(a.shape, lambda i: (0, 0)))
    return out


def _stage_fwd(fn, ins, specs, out_widths, name, tb, out_dtype=F32):
    T = [a for a, s in zip(ins, specs) if s[0] == 'row'][0].shape[0]
    tb = min(tb, T)
    n_in = len(ins)

    def body(*refs):
        args, _ = _load_args(refs[:n_in], specs)
        outs = fn(*args)
        for o_ref, o in zip(refs[n_in:], outs):
            o_ref[...] = o.astype(out_dtype)

    return pl.pallas_call(
        body, name=name + '_f', grid=(T // tb,), in_specs=_stage_in_specs(ins, specs, tb),
        out_specs=[pl.BlockSpec((tb, w), lambda i: (i, 0)) for w in out_widths],
        out_shape=[jax.ShapeDtypeStruct((T, w), out_dtype) for w in out_widths],
        compiler_params=_cparams(('parallel',)),
    )(*ins)


def _stage_bwd(fn, ins, specs, out_widths, douts, name, tb):
    T = [a for a, s in zip(ins, specs) if s[0] == 'row'][0].shape[0]
    tb = min(tb, T)
    n_in, n_out = len(ins), len(out_widths)
    diff_inputs = [i for i, s in enumerate(specs) if s[1]]

    def body(*refs):
        in_refs, dout_refs, g_refs = refs[:n_in], refs[n_in:n_in + n_out], refs[n_in + n_out:]
        args, amap = _load_args(in_refs, specs)
        didx = [j for j, (i, _) in enumerate(amap) if specs[i][1]]

        def f(*dv):
            full = list(args)
            for j, v in zip(didx, dv):
                full[j] = v
            return tuple(fn(*full))

        _, vjp = jax.vjp(f, *[args[j] for j in didx])
        gs = vjp(tuple(d[...] for d in dout_refs))
        gmap = {j: g for j, g in zip(didx, gs)}
        first = pl.program_id(0) == 0
        for g_ref, i in zip(g_refs, diff_inputs):
            kind, _, pieces = specs[i]
            js = [j for j, (ii, _) in enumerate(amap) if ii == i]
            if kind == 'row':
                if pieces is None:
                    g_ref[...] = gmap[js[0]]
                else:
                    if sum(w for _, w in pieces) != ins[i].shape[1]:
                        g_ref[...] = jnp.zeros(g_ref.shape, F32)
                    for j in js:
                        s, w = amap[j][1]
                        g_ref[:, s:s + w] = gmap[j]
            else:
                @pl.when(first)
                def _(g_ref=g_ref):
                    g_ref[...] = jnp.zeros(g_ref.shape, F32)

                g_ref[...] += gmap[js[0]]

    in_specs = _stage_in_specs(ins, specs, tb) + [pl.BlockSpec((tb, w), lambda i: (i, 0)) for w in out_widths]
    out_specs, out_shape = [], []
    for i in diff_inputs:
        a = ins[i]
        if specs[i][0] == 'row':
            out_specs.append(pl.BlockSpec((tb, a.shape[1]), lambda i: (i, 0)))
        else:
            out_specs.append(pl.BlockSpec(a.shape, lambda i: (0, 0)))
        out_shape.append(jax.ShapeDtypeStruct(a.shape, F32))
    return pl.pallas_call(
        body, name=name + '_b', grid=(T // tb,), in_specs=in_specs, out_specs=out_specs, out_shape=out_shape,
        compiler_params=_cparams(('arbitrary',)),
    )(*ins, *douts)


def stage_op(fn, specs, out_widths, name, tb=256):
    n = len(specs)
    diff_inputs = [i for i, s in enumerate(specs) if s[1]]

    @jax.custom_vjp
    def op(*ins):
        return tuple(_stage_fwd(fn, ins, specs, out_widths, name, tb))

    def op_fwd(*ins):
        return op(*ins), ins

    def op_bwd(ins, douts):
        gs = _stage_bwd(fn, ins, specs, out_widths, douts, name, tb)
        res = [None] * n
        for i, g in zip(diff_inputs, gs):
            res[i] = g
        return tuple(res)

    op.defvjp(op_fwd, op_bwd)
    return op


@jax.custom_vjp
def bdot(x, w):
    return jnp.dot(x.astype(BF16), w.astype(BF16), preferred_element_type=F32)


def _bdot_fwd(x, w):
    return bdot(x, w), (x, w)


def _bdot_bwd(res, dy):
    x, w = res
    dyb = dy.astype(BF16)
    dx = lax.dot_general(dyb, w.astype(BF16), (((1,), (1,)), ((), ())), preferred_element_type=F32)
    dw = lax.dot_general(x.astype(BF16), dyb, (((0,), (0,)), ((), ())), preferred_element_type=F32)
    return dx, dw


bdot.defvjp(_bdot_fwd, _bdot_bwd)


def _sdot_raw(x, c):
    hi = x.astype(BF16)
    r1 = x - hi.astype(F32)
    mid = r1.astype(BF16)
    lo = (r1 - mid.astype(F32)).astype(BF16)
    d = lambda u: jnp.dot(u, c, preferred_element_type=F32)
    return d(hi) + d(mid) + d(lo)


@jax.custom_vjp
def sdot(x, c, ct):
    return _sdot_raw(x, c)


def _sdot_fwd(x, c, ct):
    return _sdot_raw(x, c), (c, ct)


def _sdot_bwd(res, dy):
    c, ct = res
    return _sdot_raw(dy, ct), None, None


sdot.defvjp(_sdot_fwd, _sdot_bwd)


def _sigmoid(x):
    return 1.0 / (1.0 + jnp.exp(-x))


def _rms(x, g):
    return x * lax.rsqrt(jnp.mean(x * x, axis=-1, keepdims=True) + NORM_EPS) * g


def _mm_up_relu2(a, b, name):
    M, K = a.shape
    tn = b.shape[2]
    N = N_DEV * tn
    tm = _pick(M, (1024, 512, 256, 128))

    def body(a_ref, b_ref, u_ref, act_ref):
        u = jnp.dot(a_ref[...], b_ref[...], preferred_element_type=F32)
        r = jnp.maximum(u, 0.0)
        u_ref[...] = u.astype(BF16)
        act_ref[...] = (r * r).astype(BF16)

    out = pl.BlockSpec((tm, tn), lambda i, j: (i, j))
    sh = jax.ShapeDtypeStruct((M, N), BF16)
    return pl.pallas_call(
        body, name=name, grid=(M // tm, N // tn),
        in_specs=[pl.BlockSpec((tm, K), lambda i, j: (i, 0)), pl.BlockSpec((None, K, tn), lambda i, j: (j, 0, 0))],
        out_specs=[out, out], out_shape=[sh, sh], compiler_params=_cparams(('parallel', 'parallel')),
    )(a, b)


def _mm_down_bwd(dy, b, u, name):
    M, N = dy.shape
    K = b.shape[0]
    tm = _pick(M, (1024, 512, 256, 128))
    tk = _pick(K, (512, 256, 128))

    def body(dy_ref, b_ref, u_ref, du_ref):
        d = lax.dot_general(dy_ref[...].astype(BF16), b_ref[...], (((1,), (1,)), ((), ())),
                            preferred_element_type=F32)
        du_ref[...] = (d * (2.0 * jnp.maximum(u_ref[...].astype(F32), 0.0))).astype(BF16)

    blk = pl.BlockSpec((tm, tk), lambda i, j: (i, j))
    return pl.pallas_call(
        body, name=name, grid=(M // tm, K // tk),
        in_specs=[pl.BlockSpec((tm, N), lambda i, j: (i, 0)), pl.BlockSpec((tk, N), lambda i, j: (j, 0)), blk],
        out_specs=blk, out_shape=jax.ShapeDtypeStruct((M, K), BF16),
        compiler_params=_cparams(('parallel', 'parallel')),
    )(dy, b, u)


_RMS_SPECS = [ROW(), FULL()]
_rms_fn = lambda xv, gv: (_rms(xv, gv),)


@functools.partial(jax.custom_vjp, nondiff_argnums=(6,))
def mlp_block(x, g, wup_b, wup_c, wdown_b, wdown_c, name):
    return _mlp_fwd(x, g, wup_b, wup_c, wdown_b, wdown_c, name)[0]


def _mlp_fwd(x, g, wup_b, wup_c, wdown_b, wdown_c, name):
    h = _stage_fwd(_rms_fn, (x, g), _RMS_SPECS, [x.shape[1]], name + '_norm', 256, out_dtype=BF16)[0]
    u, act = _mm_up_relu2(h, wup_b, name + '_up')
    y = _mm_nn(act, wdown_b, add=x, name=name + '_down')
    return y, (x, g, h, u, act, wup_b, wdown_b)


def _mlp_bwd(name, res, dy):
    x, g, h, u, act, wup_b, wdown_b = res
    du = _mm_down_bwd(dy, wdown_b, u, name + '_down_da')
    dwdown = _mm_tn(act, dy, name=name + '_down_dw')
    dh = _mm_nt(du, wup_b, name=name + '_up_da')
    dwup = _mm_tn(h, du, name=name + '_up_dw', blocked=True)
    dx, dg = _stage_bwd(_rms_fn, (x, g), _RMS_SPECS, [x.shape[1]], (dh,), name + '_norm', 256)
    return dx + dy, dg, None, dwup, None, dwdown


mlp_block.defvjp(_mlp_fwd, _mlp_bwd)


@functools.partial(jax.custom_vjp, nondiff_argnums=(4, 5))
def _norm_projections(x, g, wbs, wcs, transposed, name):
    return _norm_projections_fwd(x, g, wbs, wcs, transposed, name)[0]


def _norm_projections_fwd(x, g, wbs, wcs, transposed, name):
    h = _stage_fwd(_rms_fn, (x, g), _RMS_SPECS, [x.shape[1]], name + '_norm', 256, out_dtype=BF16)[0]
    mm = _mm_nt if transposed else _mm_nn
    return tuple(mm(h, wb, name=f'{name}_f{i}') for i, wb in enumerate(wbs)), (x, g, h, wbs)


def _norm_projections_bwd(transposed, name, res, dys):
    x, g, h, wbs = res
    dh = None
    for i, (dy, wb) in enumerate(zip(dys, wbs)):
        dh = (_mm_nn if transposed else _mm_nt)(dy, wb, add=dh, name=f'{name}_da{i}')
    if transposed:
        dws = tuple(_mm_tn(dy, h, name=f'{name}_dw{i}') for i, dy in enumerate(dys))
    else:
        dws = tuple(_mm_tn(h, dy, name=f'{name}_dw{i}') for i, dy in enumerate(dys))
    dx, dg = _stage_bwd(_rms_fn, (x, g), _RMS_SPECS, [x.shape[1]], (dh,), name + '_norm', 256)
    return dx, dg, None, dws


_norm_projections.defvjp(_norm_projections_fwd, _norm_projections_bwd)


def norm_projections(x, g, ws, name, transposed=False):
    return _norm_projections(x, g.reshape(1, -1), tuple(w.b for w in ws), tuple(w.c for w in ws), transposed, name)


def _shift_down(x, rows):
    return jnp.where(rows == 0, 0.0, pltpu.roll(x, 1, 0))


def _shift_up(x, rows, T):
    return jnp.where(rows == T - 1, 0.0, pltpu.roll(x, T - 1, 0))


def _tshift_fwd_call(x, mu, name):
    T, C = x.shape

    def body(x_ref, mu_ref, o_ref):
        xv = x_ref[...]
        rows = lax.broadcasted_iota(jnp.int32, xv.shape, 0)
        o_ref[...] = xv + (_shift_down(xv, rows) - xv) * mu_ref[...]

    return pl.pallas_call(
        body, name=name + '_f', grid=(C // LANES,),
        in_specs=[pl.BlockSpec((T, LANES), lambda j: (0, j)), pl.BlockSpec((1, LANES), lambda j: (0, j))],
        out_specs=pl.BlockSpec((T, LANES), lambda j: (0, j)), out_shape=jax.ShapeDtypeStruct((T, C), F32),
        compiler_params=_cparams(('parallel',)),
    )(x, mu)


def _tshift_bwd_call(x, mu, dy, name):
    T, C = x.shape

    def body(x_ref, mu_ref, dy_ref, dx_ref, dmu_ref):
        xv, d = x_ref[...], dy_ref[...]
        rows = lax.broadcasted_iota(jnp.int32, xv.shape, 0)
        z = d * mu_ref[...]
        dx_ref[...] = d - z + _shift_up(z, rows, T)
        dmu_ref[...] = jnp.sum(d * (_shift_down(xv, rows) - xv), axis=0, keepdims=True)

    return pl.pallas_call(
        body, name=name + '_b', grid=(C // LANES,),
        in_specs=[pl.BlockSpec((T, LANES), lambda j: (0, j)), pl.BlockSpec((1, LANES), lambda j: (0, j)),
                  pl.BlockSpec((T, LANES), lambda j: (0, j))],
        out_specs=[pl.BlockSpec((T, LANES), lambda j: (0, j)), pl.BlockSpec((1, LANES), lambda j: (0, j))],
        out_shape=[jax.ShapeDtypeStruct((T, C), F32), jax.ShapeDtypeStruct((1, C), F32)],
        compiler_params=_cparams(('parallel',)),
    )(x, mu, dy)


@functools.partial(jax.custom_vjp, nondiff_argnums=(2,))
def token_shift_mix(x, mu, name):
    return _tshift_fwd_call(x, mu, name)


def _tsm_fwd(x, mu, name):
    return _tshift_fwd_call(x, mu, name), (x, mu)


def _tsm_bwd(name, res, dy):
    x, mu = res
    dx, dmu = _tshift_bwd_call(x, mu, dy, name)
    return dx, dmu


token_shift_mix.defvjp(_tsm_fwd, _tsm_bwd)


def _conv_specs(T):
    nb = CONV_WIDTH // LANES
    return [pl.BlockSpec((T, LANES), lambda j: (0, j)), pl.BlockSpec((T, LANES), lambda j: (0, nb + j)),
            pl.BlockSpec((T, LANES), lambda j: (0, 2 * nb + j)), pl.BlockSpec((3, LANES), lambda j: (0, j))]


def _conv_fwd_call(cv, w, name):
    T = cv.shape[0]

    def body(b_ref, c_ref, x_ref, w_ref, o_ref):
        u = c_ref[...] * x_ref[...]
        rows = lax.broadcasted_iota(jnp.int32, u.shape, 0)
        u1 = _shift_down(u, rows)
        u2 = _shift_down(u1, rows)
        o_ref[...] = b_ref[...] * (w_ref[0:1, :] * u2 + w_ref[1:2, :] * u1 + w_ref[2:3, :] * u)

    return pl.pallas_call(
        body, name=name + '_f', grid=(CONV_WIDTH // LANES,), in_specs=_conv_specs(T),
        out_specs=pl.BlockSpec((T, LANES), lambda j: (0, j)),
        out_shape=jax.ShapeDtypeStruct((T, CONV_WIDTH), F32), compiler_params=_cparams(('parallel',)),
    )(cv, cv, cv, w)


def _conv_bwd_call(cv, w, do, name):
    T = cv.shape[0]

    def body(b_ref, c_ref, x_ref, w_ref, do_ref, db_ref, dc_ref, dx_ref, dw_ref):
        c, x, d = c_ref[...], x_ref[...], do_ref[...]
        u = c * x
        rows = lax.broadcasted_iota(jnp.int32, u.shape, 0)
        u1 = _shift_down(u, rows)
        u2 = _shift_down(u1, rows)
        w0, w1, w2 = w_ref[0:1, :], w_ref[1:2, :], w_ref[2:3, :]
        db_ref[...] = d * (w0 * u2 + w1 * u1 + w2 * u)
        dy = d * b_ref[...]
        dy1 = _shift_up(dy, rows, T)
        dy2 = _shift_up(dy1, rows, T)
        du = w2 * dy + w1 * dy1 + w0 * dy2
        dc_ref[...] = du * x
        dx_ref[...] = du * c
        dw_ref[0:1, :] = jnp.sum(dy * u2, axis=0, keepdims=True)
        dw_ref[1:2, :] = jnp.sum(dy * u1, axis=0, keepdims=True)
        dw_ref[2:3, :] = jnp.sum(dy * u, axis=0, keepdims=True)

    blk = pl.BlockSpec((T, LANES), lambda j: (0, j))
    sh = jax.ShapeDtypeStruct((T, CONV_WIDTH), F32)
    return pl.pallas_call(
        body, name=name + '_b', grid=(CONV_WIDTH // LANES,), in_specs=_conv_specs(T) + [blk],
        out_specs=[blk, blk, blk, pl.BlockSpec((3, LANES), lambda j: (0, j))],
        out_shape=[sh, sh, sh, jax.ShapeDtypeStruct((3, CONV_WIDTH), F32)],
        compiler_params=_cparams(('parallel',)),
    )(cv, cv, cv, w, do)


@functools.partial(jax.custom_vjp, nondiff_argnums=(2,))
def short_conv(cv, w, name):
    return _conv_fwd_call(cv, w, name)


def _sc_fwd(cv, w, name):
    return _conv_fwd_call(cv, w, name), (cv, w)


def _sc_bwd(name, res, do):
    cv, w = res
    db, dc, dx, dw = _conv_bwd_call(cv, w, do, name)
    return jnp.concatenate([db, dc, dx], axis=1), dw


short_conv.defvjp(_sc_fwd, _sc_bwd)


ATT_SCALE = QK_HEAD ** -0.5
NPAIR = MLA_HEADS // 2


def _att_bq(T):
    return min(256, T)


def _att_masks(pair, j):
    lane = lax.broadcasted_iota(jnp.int32, (1, LANES), 1)
    mask_n = (lane // QK_NOPE) == j
    mask_r = (lane // (QK_ROPE // 2)) == (2 * pair + j)
    return mask_n, mask_r


def _att_probs(qcat, kcat, row0, stop):
    s = lax.dot_general(qcat, kcat, (((1,), (1,)), ((), ())), preferred_element_type=F32) * ATT_SCALE
    r = row0 + lax.broadcasted_iota(jnp.int32, s.shape, 0)
    c = lax.broadcasted_iota(jnp.int32, s.shape, 1)
    s = jnp.where(c <= r, s, -jnp.inf)
    e = jnp.exp(s - jnp.max(s, axis=-1, keepdims=True))
    return e / jnp.sum(e, axis=-1, keepdims=True)


def _att_in_specs(T):
    blk = lambda f: pl.BlockSpec((T, LANES), f)
    return [blk(lambda p: (0, p)), blk(lambda p: (0, 0)), blk(lambda p: (0, 0)),
            blk(lambda p: (0, p)), blk(lambda p: (0, 0)), blk(lambda p: (0, 0)), blk(lambda p: (0, p))]


def _att_fwd_call(qn, q1, q2, kn, k1, k2, v, name):
    T = qn.shape[0]
    bq = _att_bq(T)

    def body(qn_ref, q1_ref, q2_ref, kn_ref, k1_ref, k2_ref, v_ref, o_ref):
        pair = pl.program_id(0)
        for i in range(T // bq):
            r0, stop = i * bq, (i + 1) * bq
            kcat = jnp.concatenate([kn_ref[0:stop, :], k1_ref[0:stop, :], k2_ref[0:stop, :]], axis=1).astype(BF16)
            vb = v_ref[0:stop, :].astype(BF16)
            outs = []
            for j in range(2):
                mask_n, mask_r = _att_masks(pair, j)
                qcat = jnp.concatenate([jnp.where(mask_n, qn_ref[r0:stop, :], 0.0),
                                        jnp.where(mask_r, q1_ref[r0:stop, :], 0.0),
                                        jnp.where(mask_r, q2_ref[r0:stop, :], 0.0)], axis=1).astype(BF16)
                p = _att_probs(qcat, kcat, r0, stop)
                outs.append(jnp.dot(p.astype(BF16), vb, preferred_element_type=F32))
            mask_n0, _ = _att_masks(pair, 0)
            o_ref[r0:stop, :] = jnp.where(mask_n0, outs[0], outs[1])

    return pl.pallas_call(
        body, name=name + '_f', grid=(NPAIR,), in_specs=_att_in_specs(T),
        out_specs=pl.BlockSpec((T, LANES), lambda p: (0, p)),
        out_shape=jax.ShapeDtypeStruct((T, MLA_HEADS * V_HEAD), F32), compiler_params=_cparams(('parallel',)),
    )(qn, q1, q2, kn, k1, k2, v)


def _att_bwd_call(qn, q1, q2, kn, k1, k2, v, o, do, name):
    T = qn.shape[0]
    bq = _att_bq(T)

    def body(qn_ref, q1_ref, q2_ref, kn_ref, k1_ref, k2_ref, v_ref, o_ref, do_ref,
             dqn_ref, dq1_ref, dq2_ref, dkn_ref, dk1_ref, dk2_ref, dv_ref, dk_acc, dv_acc):
        pair = pl.program_id(0)

        @pl.when(pair == 0)
        def _():
            dq1_ref[...] = jnp.zeros(dq1_ref.shape, F32)
            dq2_ref[...] = jnp.zeros(dq2_ref.shape, F32)
            dk1_ref[...] = jnp.zeros(dk1_ref.shape, F32)
            dk2_ref[...] = jnp.zeros(dk2_ref.shape, F32)

        dk_acc[...] = jnp.zeros(dk_acc.shape, F32)
        dv_acc[...] = jnp.zeros(dv_acc.shape, F32)
        for i in range(T // bq):
            r0, stop = i * bq, (i + 1) * bq
            kcat = jnp.concatenate([kn_ref[0:stop, :], k1_ref[0:stop, :], k2_ref[0:stop, :]], axis=1).astype(BF16)
            vb = v_ref[0:stop, :].astype(BF16)
            dqn = jnp.zeros((bq, LANES), F32)
            for j in range(2):
                mask_n, mask_r = _att_masks(pair, j)
                qcat = jnp.concatenate([jnp.where(mask_n, qn_ref[r0:stop, :], 0.0),
                                        jnp.where(mask_r, q1_ref[r0:stop, :], 0.0),
                                        jnp.where(mask_r, q2_ref[r0:stop, :], 0.0)], axis=1).astype(BF16)
                p = _att_probs(qcat, kcat, r0, stop)
                dom = jnp.where(mask_n, do_ref[r0:stop, :], 0.0)
                delta = jnp.sum(dom * o_ref[r0:stop, :], axis=-1, keepdims=True)
                domb = dom.astype(BF16)
                dp = lax.dot_general(domb, vb, (((1,), (1,)), ((), ())), preferred_element_type=F32)
                ds = (p * (dp - delta) * ATT_SCALE).astype(BF16)
                dqc = jnp.dot(ds, kcat, preferred_element_type=F32)
                dqn = dqn + jnp.where(mask_n, dqc[:, 0:LANES], 0.0)
                dq1_ref[r0:stop, :] += jnp.where(mask_r, dqc[:, LANES:2 * LANES], 0.0)
                dq2_ref[r0:stop, :] += jnp.where(mask_r, dqc[:, 2 * LANES:3 * LANES], 0.0)
                dk_acc[0:stop, :] += lax.dot_general(ds, qcat, (((0,), (0,)), ((), ())),
                                                     preferred_element_type=F32)
                dv_acc[0:stop, :] += lax.dot_general(p.astype(BF16), domb, (((0,), (0,)), ((), ())),
                                                     preferred_element_type=F32)
            dqn_ref[r0:stop, :] = dqn
        dkn_ref[...] = dk_acc[:, 0:LANES]
        dk1_ref[...] += dk_acc[:, LANES:2 * LANES]
        dk2_ref[...] += dk_acc[:, 2 * LANES:3 * LANES]
        dv_ref[...] = dv_acc[...]

    per_pair = pl.BlockSpec((T, LANES), lambda p: (0, p))
    shared = pl.BlockSpec((T, LANES), lambda p: (0, 0))
    wide = jax.ShapeDtypeStruct((T, MLA_HEADS * QK_NOPE), F32)
    narrow = jax.ShapeDtypeStruct((T, LANES), F32)
    return pl.pallas_call(
        body, name=name + '_b', grid=(NPAIR,), in_specs=_att_in_specs(T) + [per_pair, per_pair],
        out_specs=[per_pair, shared, shared, per_pair, shared, shared, per_pair],
        out_shape=[wide, narrow, narrow, wide, narrow, narrow, wide],
        scratch_shapes=[pltpu.VMEM((T, 3 * LANES), F32), pltpu.VMEM((T, LANES), F32)],
        compiler_params=_cparams(('arbitrary',)),
    )(qn, q1, q2, kn, k1, k2, v, o, do)


@functools.partial(jax.custom_vjp, nondiff_argnums=(7,))
def attention(qn, q1, q2, kn, k1, k2, v, name):
    return _att_fwd_call(qn, q1, q2, kn, k1, k2, v, name)


def _attn_fwd(qn, q1, q2, kn, k1, k2, v, name):
    o = _att_fwd_call(qn, q1, q2, kn, k1, k2, v, name)
    return o, (qn, q1, q2, kn, k1, k2, v, o)


def _attn_bwd(name, res, do):
    return tuple(_att_bwd_call(*res, do, name))


attention.defvjp(_attn_fwd, _attn_bwd)


SCAN_CHUNK = 64
SCAN_UNROLL = 8


def _block_ones(n, seg):
    i = np.arange(n)
    return (i[:, None] // seg == i[None, :] // seg).astype(np.float32)


def _scan_diag():
    i = np.arange(RW_WIDTH)
    return jnp.asarray((np.arange(RW_N)[:, None] == (i[None, :] % RW_N)).astype(np.float32))


def _head_rowsum(x):
    low = lax.broadcasted_iota(jnp.int32, (1, LANES), 1) < RW_N
    tiles = []
    for j in range(RW_WIDTH // LANES):
        xt = x[:, j * LANES:(j + 1) * LANES]
        x0 = jnp.where(low, xt, 0.0)
        s0 = jnp.sum(x0, axis=-1, keepdims=True)
        s1 = jnp.sum(xt - x0, axis=-1, keepdims=True)
        tiles.append(jnp.where(low, s0, s1))
    return jnp.concatenate(tiles, axis=1)


def _unrolled_loop(n, step, init):
    def body(i, carry):
        for j in range(SCAN_UNROLL):
            carry = step(i * SCAN_UNROLL + j, carry)
        return carry
    return lax.fori_loop(0, n // SCAN_UNROLL, body, init)


def _scan_fwd_call(r, w, k, v, a, b, name):
    T = r.shape[0]
    tc = min(SCAN_CHUNK, T)
    dg = _scan_diag()

    def body(r_ref, w_ref, k_ref, v_ref, a_ref, b_ref, dg_ref, y_ref, st_ref, sa_ref, vc_ref, s_ref):
        @pl.when(pl.program_id(0) == 0)
        def _():
            s_ref[...] = jnp.zeros(s_ref.shape, F32)

        dgv = dg_ref[...]
        readout = lambda s, t: jnp.sum(_head_rowsum(s * r_ref[t]) * dgv, axis=0, keepdims=True)

        def step(t, carry):
            s, vcol = carry
            st_ref[t] = s
            vc_ref[t] = vcol
            sa = _head_rowsum(s * a_ref[t])
            sa_ref[t] = sa
            prev = jnp.maximum(t - 1, 0)
            y_ref[prev] = readout(s, prev)
            vcol_next = _head_rowsum(dgv * v_ref[jnp.minimum(t + 1, tc - 1)])
            sn = s * w_ref[t] + sa * b_ref[t] + vcol * k_ref[t]
            return sn, vcol_next

        s_end, _ = _unrolled_loop(tc, step, (s_ref[...], _head_rowsum(dgv * v_ref[0])))
        y_ref[tc - 1] = readout(s_end, tc - 1)
        s_ref[...] = s_end

    vec = pl.BlockSpec((tc, 1, RW_WIDTH), lambda i: (i, 0, 0))
    mat = pl.BlockSpec((tc, RW_N, RW_WIDTH), lambda i: (i, 0, 0))
    msh = jax.ShapeDtypeStruct((T, RW_N, RW_WIDTH), F32)
    return pl.pallas_call(
        body, name=name + '_f', grid=(T // tc,),
        in_specs=[vec] * 6 + [pl.BlockSpec((RW_N, RW_WIDTH), lambda i: (0, 0))],
        out_specs=[vec, mat, mat, mat],
        out_shape=[jax.ShapeDtypeStruct((T, 1, RW_WIDTH), F32), msh, msh, msh],
        scratch_shapes=[pltpu.VMEM((RW_N, RW_WIDTH), F32)],
        compiler_params=_cparams(('arbitrary',)),
    )(r, w, k, v, a, b, dg)


def _scan_bwd_call(r, w, k, a, b, st, sa_all, vc_all, dy, name):
    T = r.shape[0]
    tc = min(SCAN_CHUNK, T)
    nt = T // tc
    dg = _scan_diag()

    def body(r_ref, w_ref, k_ref, a_ref, b_ref, st_ref, sa_ref, vc_ref, dy_ref, dg_ref,
             dr_ref, dw_ref, dk_ref, dv_ref, da_ref, db_ref, ds_ref):
        @pl.when(pl.program_id(0) == 0)
        def _():
            ds_ref[...] = jnp.zeros(ds_ref.shape, F32)

        dgv = dg_ref[...]
        colsum = lambda x: jnp.sum(x, axis=0, keepdims=True)

        def step(i, carry):
            ds, dycol = carry
            t = tc - 1 - i
            sp = st_ref[t]
            rt, wt, kt, at, bt = r_ref[t], w_ref[t], k_ref[t], a_ref[t], b_ref[t]
            ds = ds + dycol * rt
            dsa = _head_rowsum(ds * bt)
            sa, vcol = sa_ref[t], vc_ref[t]
            dycol_next = _head_rowsum(dgv * dy_ref[jnp.maximum(t - 1, 0)])
            sn = sp * wt + sa * bt + vcol * kt
            dr_ref[t] = colsum(sn * dycol)
            dk_ref[t] = colsum(ds * vcol)
            db_ref[t] = colsum(ds * sa)
            dw_ref[t] = colsum(ds * sp)
            dv_ref[t] = colsum(_head_rowsum(ds * kt) * dgv)
            da_ref[t] = colsum(sp * dsa)
            return ds * wt + dsa * at, dycol_next

        ds_end, _ = _unrolled_loop(tc, step, (ds_ref[...], _head_rowsum(dgv * dy_ref[tc - 1])))
        ds_ref[...] = ds_end

    vec = pl.BlockSpec((tc, 1, RW_WIDTH), lambda i: (nt - 1 - i, 0, 0))
    mat = pl.BlockSpec((tc, RW_N, RW_WIDTH), lambda i: (nt - 1 - i, 0, 0))
    vsh = jax.ShapeDtypeStruct((T, 1, RW_WIDTH), F32)
    return pl.pallas_call(
        body, name=name + '_b', grid=(nt,),
        in_specs=[vec] * 5 + [mat] * 3 + [vec, pl.BlockSpec((RW_N, RW_WIDTH), lambda i: (0, 0))],
        out_specs=[vec] * 6, out_shape=[vsh] * 6,
        scratch_shapes=[pltpu.VMEM((RW_N, RW_WIDTH), F32)],
        compiler_params=_cparams(('arbitrary',)),
    )(r, w, k, a, b, st, sa_all, vc_all, dy, dg)


@functools.partial(jax.custom_vjp, nondiff_argnums=(6,))
def wkv7(r, w, k, v, a, b, name):
    return _scan_fwd_call(r, w, k, v, a, b, name)[0]


def _wkv7_fwd(r, w, k, v, a, b, name):
    y, st, sa_all, vc_all = _scan_fwd_call(r, w, k, v, a, b, name)
    return y, (r, w, k, a, b, st, sa_all, vc_all)


def _wkv7_bwd(name, res, dy):
    return tuple(_scan_bwd_call(*res, dy, name))


wkv7.defvjp(_wkv7_fwd, _wkv7_bwd)


def _np_bf16(a):
    return jnp.asarray(a, BF16)


def _mla_consts():
    seg_n = (np.arange(512)[:, None] // QK_NOPE == np.arange(LANES)[None, :]).astype(np.float32)
    seg_r = (np.arange(LANES)[:, None] // 16 == np.arange(LANES)[None, :]).astype(np.float32)
    e1 = np.zeros((LANES, LANES), np.float32)
    e2 = np.zeros((LANES, LANES), np.float32)
    for h in range(MLA_HEADS):
        for i in range(16):
            e1[i, h * 16 + i] = 1.0
            e2[16 + i, h * 16 + i] = 1.0
    mats = [seg_n, seg_n.T, seg_r, seg_r.T, e1, e1.T, e2, e2.T]
    return [_np_bf16(m) for m in mats]


def _qk_prep_fn(qn, q1, q2, kn, kx, cos, sin, gqn, gq1, gq2, gkn, gk1, gk2,
                seg_n, seg_nt, seg_r, seg_rt, e1, e1t, e2, e2t):
    def normrope(xn, x1, x2, gn, g1, g2):
        ss = sdot(xn * xn, seg_n, seg_nt) + sdot(x1 * x1, seg_r, seg_rt) + sdot(x2 * x2, seg_r, seg_rt)
        inv = lax.rsqrt(ss * (1.0 / QK_HEAD) + NORM_EPS)
        inv_n = sdot(inv, seg_nt, seg_n)
        inv_r = sdot(inv, seg_rt, seg_r)
        y1 = x1 * inv_r * g1
        y2 = x2 * inv_r * g2
        return xn * inv_n * gn, y1 * cos - y2 * sin, y1 * sin + y2 * cos

    k1 = sdot(kx, e1, e1t)
    k2 = sdot(kx, e2, e2t)
    return normrope(qn, q1, q2, gqn, gq1, gq2) + normrope(kn, k1, k2, gkn, gk1, gk2)


def _rwkv_prep_fn(vres):
    def fn(r, k, v, xg, xwa, kx, *rest):
        if vres:
            vfirst, w0, a0, k_k, k_a, w2p, a2p, g2, v0, v2p, bm = rest
        else:
            w0, a0, k_k, k_a, w2p, a2p, g2, bm = rest
        z = w0 + bdot(jnp.tanh(xwa), w2p)
        nz = -z
        softplus = jnp.maximum(nz, 0.0) + jnp.log(1.0 + jnp.exp(-jnp.abs(nz)))
        decay = jnp.exp(-jnp.exp(-softplus - 0.5))
        a = _sigmoid(a0 + bdot(xwa, a2p))
        g = bdot(_sigmoid(xg), g2)
        if vres:
            vv = v + (vfirst - v) * _sigmoid(v0 + bdot(kx, v2p))
        else:
            vv = v
        kkr = k * k_k
        kk = kkr / jnp.maximum(jnp.sqrt(sdot(kkr * kkr, bm, bm)), 1e-12)
        k2 = k * (1.0 + (a - 1.0) * k_a)
        return r * 1.0, decay, k2, vv, -kk, kk * a, g
    return fn


def _rwkv_post_fn(y, r, k2, vv, g, ln_w, ln_b, rk, bm):
    inv_n = 1.0 / RW_N
    mean = sdot(y, bm, bm) * inv_n
    yc = y - mean
    var = sdot(yc * yc, bm, bm) * inv_n
    yn = yc * lax.rsqrt(var + GN_EPS) * ln_w + ln_b
    bonus = sdot(r * k2 * rk, bm, bm) * vv
    return ((yn + bonus) * g,)


def _merge_fn(g0, g1, g2, oa, ob, oc):
    return (_sigmoid(g0) * oa + _sigmoid(g1) * ob + _sigmoid(g2) * oc,)


def _loss_call(y, target):
    T, C = y.shape
    tb = min(256, T)

    def body(y_ref, t_ref, dy_ref, part_ref):
        err = y_ref[...] - t_ref[...]
        dy_ref[...] = err * (1.0 / C)
        sq = jnp.sum(err * err, axis=0, keepdims=True)
        acc = sq[:, 0:LANES]
        for j in range(1, C // LANES):
            acc = acc + sq[:, j * LANES:(j + 1) * LANES]
        part_ref[...] = jnp.zeros(part_ref.shape, F32)
        part_ref[0:1, :] = acc * (0.5 / C)

    return pl.pallas_call(
        body, name='loss', grid=(T // tb,),
        in_specs=[pl.BlockSpec((tb, C), lambda i: (i, 0))] * 2,
        out_specs=[pl.BlockSpec((tb, C), lambda i: (i, 0)), pl.BlockSpec((8, LANES), lambda i: (i, 0))],
        out_shape=[jax.ShapeDtypeStruct((T, C), F32), jax.ShapeDtypeStruct((8 * (T // tb), LANES), F32)],
        compiler_params=_cparams(('parallel',)),
    )(y, target)


def _pad_rows(t, before, total):
    return jnp.pad(t, ((before, total - before - t.shape[0]), (0, 0)))


def _head_tile(g, lo, hi):
    return jnp.tile(g[lo:hi], MLA_HEADS).reshape(1, -1)


def _layer(l, x, v_first, wd, sp, cos, sin):
    T = x.shape[0]
    nm = f'l{l}'
    vres = l > 0
    w_in = wd['w_in']
    if vres:
        v1t = wd['rwkv_v1'].map(lambda t: t.T)
    else:
        v1t = W(jnp.zeros((MV_LORA, D_MODEL), BF16), jnp.zeros((MV_LORA, D_MODEL), BF16))
    zpad = W(jnp.zeros((64, D_MODEL), BF16), jnp.zeros((64, D_MODEL), BF16))
    w_rw = _wcat([w_in['rkv'], w_in['xg'], w_in['xwa'], w_in['kpe'], v1t, zpad], 0)
    gates, cq, ckv, rw, cv = norm_projections(
        x, sp['attn_norm'], [w_in['gates'], w_in['cq'], w_in['ckv'], w_rw, w_in['conv']], nm + '_win', transposed=True)

    v_mu = sp['rwkv_v_mu'] if vres else jnp.zeros((MV_LORA,), F32)
    mu_all = jnp.concatenate([sp['rwkv_mu'][0:768], sp['rwkv_mu'][896:1024], sp['rwkv_mu'][768:896],
                              jnp.zeros((QK_ROPE,), F32), v_mu, jnp.zeros((64,), F32)]).reshape(1, -1)
    rws = token_shift_mix(rw, mu_all, nm + '_shift')

    wq = wd['mla_wq_b'].map(lambda t: jnp.concatenate(
        [t.reshape(Q_LORA, MLA_HEADS, QK_HEAD)[:, :, 0:64].reshape(Q_LORA, 512),
         t.reshape(Q_LORA, MLA_HEADS, QK_HEAD)[:, :, 64:80].reshape(Q_LORA, 128),
         t.reshape(Q_LORA, MLA_HEADS, QK_HEAD)[:, :, 80:96].reshape(Q_LORA, 128)], axis=1))
    wkn = wd['mla_wkv_b'].map(lambda t: t.reshape(KV_LORA, MLA_HEADS, 128)[:, :, 0:64].reshape(KV_LORA, 512))
    wv = wd['mla_wkv_b'].map(lambda t: t.reshape(KV_LORA, MLA_HEADS, 128)[:, :, 64:128].reshape(KV_LORA, 512))
    q, = norm_projections(cq, sp['mla_q_a_norm'], [wq], nm + '_wq')
    kn, vv_att = norm_projections(ckv, sp['mla_kv_a_norm'], [wkn, wv], nm + '_wkv')
    gq, gk = sp['mla_q_norm'], sp['mla_k_norm']
    consts = _mla_consts()
    qk_specs = ([ROW(pieces=((0, 512), (512, 128), (640, 128))), ROW(), ROW(pieces=((1024, 128),)),
                 ROW(False), ROW(False)] + [FULL()] * 6 + [FULL(False)] * 8)
    qk_op = stage_op(_qk_prep_fn, qk_specs, [512, 128, 128, 512, 128, 128], nm + '_qkprep')
    Qn, Q1, Q2, Kn, K1, K2 = qk_op(q, kn, rws, cos, sin,
                                   _head_tile(gq, 0, 64), _head_tile(gq, 64, 80), _head_tile(gq, 80, 96),
                                   _head_tile(gk, 0, 64), _head_tile(gk, 64, 80), _head_tile(gk, 80, 96), *consts)
    o_att = attention(Qn, Q1, Q2, Kn, K1, K2, vv_att, nm + '_att')
    o_a = linear(o_att, wd['mla_w_o'], name=nm + '_wo')

    bm = _np_bf16(_block_ones(RW_WIDTH, RW_N))
    vec = lambda n: sp[n].reshape(1, -1)
    f32w = lambda n: wd[n].c + wd[n].b.astype(F32)
    w2p = _pad_rows(f32w('rwkv_w2'), 0, 128)
    a2p = _pad_rows(f32w('rwkv_a2'), 64, 128)
    g2 = f32w('rwkv_g2')
    rw_pieces = ((0, 256), (256, 256), (512, 256), (768, 128), (896, 128), (1024, 128))
    if vres:
        v2p = _pad_rows(f32w('rwkv_v2'), 32, 128)
        prep_specs = [ROW(pieces=rw_pieces), ROW()] + [FULL()] * 9 + [FULL(False)]
        prep_in = [rws, v_first, vec('rwkv_w0'), vec('rwkv_a0'), vec('rwkv_k_k'), vec('rwkv_k_a'), w2p, a2p, g2,
                   vec('rwkv_v0'), v2p, bm]
    else:
        prep_specs = [ROW(pieces=rw_pieces)] + [FULL()] * 7 + [FULL(False)]
        prep_in = [rws, vec('rwkv_w0'), vec('rwkv_a0'), vec('rwkv_k_k'), vec('rwkv_k_a'), w2p, a2p, g2, bm]
    prep_op = stage_op(_rwkv_prep_fn(vres), prep_specs, [256] * 7, nm + '_rwprep')
    r_, dec, k2, vv, an, bn, g = prep_op(*prep_in)
    if not vres:
        v_first = vv
    t3 = lambda t: t.reshape(T, 1, RW_WIDTH)
    y = wkv7(t3(r_), t3(dec), t3(k2), t3(vv), t3(an), t3(bn), nm + '_scan').reshape(T, RW_WIDTH)
    post_op = stage_op(_rwkv_post_fn, [ROW()] * 5 + [FULL()] * 3 + [FULL(False)], [256], nm + '_rwpost')
    yb = post_op(y, r_, k2, vv, g, vec('rwkv_ln_w'), vec('rwkv_ln_b'), sp['rwkv_r_k'].reshape(1, -1), bm)[0]
    o_b = linear(yb, wd['rwkv_w_o'], name=nm + '_rwo')

    oc_in = short_conv(cv, f32w('conv_w'), nm + '_conv')
    o_c = linear(oc_in, wd['conv_w_o'], name=nm + '_cwo')

    merge_op = stage_op(_merge_fn, [ROW(pieces=((0, 1024), (1024, 1024), (2048, 1024))), ROW(), ROW(), ROW()],
                        [D_MODEL], nm + '_merge')
    merged = merge_op(gates, o_a, o_b, o_c)[0]
    x2 = linear(merged, wd['w_out'], add=x, name=nm + '_wout')
    return x2, v_first


def _mlp(l, x, wd, sp):
    return mlp_block(x, sp['mlp_norm'].reshape(1, -1), wd['w_up'].b, wd['w_up'].c, wd['w_down'].b, wd['w_down'].c,
                     f'l{l}_mlp')


MLP_WEIGHTS = ('w_up', 'w_down')


def _entries(ml):
    out = []
    for name, (layers, shape, axis) in SHARDED.items():
        l = ml if layers == DEPTH else ml - 1
        if not 0 <= l < layers or name in MLP_WEIGHTS:
            continue
        n = shape[0] * shape[1] // N_DEV
        if name == 'conv_w':
            out.append(('conv_w_hi', name, l, n))
            out.append(('conv_w_lo', name, l, n))
        else:
            out.append((name, name, l, n))
    return out


def _slot_size(n):
    return -(-n // LANES) * LANES


def _pack_rows(ml):
    total = sum(_slot_size(n) for _, _, _, n in _entries(ml))
    rows = -(-total // LANES)
    return -(-rows // PACK_ROW_MULT) * PACK_ROW_MULT


def _pack_flat(pieces, ml):
    rows = _pack_rows(ml)
    padded = []
    for p, (_, _, _, n) in zip(pieces, _entries(ml)):
        pad = _slot_size(n) - n
        if pad:
            p = jnp.pad(p, [(0, 0)] * (p.ndim - 1) + [(0, pad)])
        padded.append(p)
    flat = jnp.concatenate(padded, axis=-1)
    tail = rows * LANES - flat.shape[-1]
    if tail:
        flat = jnp.pad(flat, [(0, 0)] * (flat.ndim - 1) + [(0, tail)])
    return flat.reshape(flat.shape[:-1] + (rows, LANES))


def _unpack_flat(buf, ml):
    out, row = [], 0
    for _, _, _, n in _entries(ml):
        nrows = _slot_size(n) // LANES
        piece = buf[..., row:row + nrows, :].reshape(buf.shape[:-2] + (-1,))
        out.append(piece[..., :n])
        row += nrows
    return out


def _pack_shards(shards, ml, dtype, split_conv):
    pieces = []
    for slot, name, l, n in _entries(ml):
        a = shards[name][l]
        if name in TRANSPOSED:
            a = a.astype(dtype).T
        a = a.reshape(-1)
        if slot == 'conv_w_hi':
            a = a.astype(BF16).astype(F32) if split_conv else a
        elif slot == 'conv_w_lo':
            a = (a - a.astype(BF16).astype(F32)) if split_conv else jnp.zeros_like(a)
        pieces.append(a.astype(dtype))
    return _pack_flat(pieces, ml)


def _unpack_shards(buf, ml):
    out = {}
    for (slot, name, l, n), v in zip(_entries(ml), _unpack_flat(buf, ml)):
        if slot == 'conv_w_lo':
            continue
        layers, shape, axis = SHARDED[name]
        sshape = (shape[0] // N_DEV, shape[1]) if axis == 0 else (shape[0], shape[1] // N_DEV)
        out[(name, l)] = v.reshape(sshape[::-1]).T if name in TRANSPOSED else v.reshape(sshape)
    return out


def _to_full(blocks, shape, axis):
    if axis == 0:
        return blocks.reshape(shape)
    return blocks.reshape(N_DEV, shape[0], shape[1] // N_DEV).transpose(1, 0, 2).reshape(shape)


def _to_blocks(full, axis):
    r, c = full.shape
    if axis == 0:
        return full.reshape(N_DEV, -1)
    return full.reshape(r, N_DEV, c // N_DEV).transpose(1, 0, 2).reshape(N_DEV, -1)


def _unpack_gathered(gathered, ml):
    out, conv_hi = {}, None
    for (slot, name, l, n), v in zip(_entries(ml), _unpack_flat(gathered, ml)):
        layers, shape, axis = SHARDED[name]
        if name in TRANSPOSED:
            out[name] = v.reshape(-1, shape[0])
            continue
        full = _to_full(v, shape, axis)
        if slot == 'conv_w_hi':
            conv_hi = full
        elif slot == 'conv_w_lo':
            out[name] = conv_hi.astype(F32) + full.astype(F32)
        else:
            out[name] = full
    return out


def _pack_grads(grads, ml):
    pieces = []
    for slot, name, l, n in _entries(ml):
        if name in TRANSPOSED:
            blocks = jnp.concatenate(grads[name], axis=0).reshape(N_DEV, -1)
        else:
            blocks = _to_blocks(grads[name], SHARDED[name][2])
        if slot == 'conv_w_lo':
            blocks = jnp.zeros_like(blocks)
        pieces.append(blocks.astype(BF16))
    return _pack_flat(pieces, ml)


def _my_pos():
    return lax.axis_index('x'), lax.axis_index('y'), lax.axis_index('c')


def _flip(v, bit):
    return 1 - v if bit else v


def all_gather_blocks(x):
    rows = x.shape[0]

    def body(x_ref, out_ref, send_sems, recv_sems, local_sem):
        mx, my, mc = _my_pos()
        me, sibling = (mx, my, mc), (mx, my, 1 - mc)
        chips = [(1 - mx, my), (mx, 1 - my), (1 - mx, 1 - my)]

        def block(px, py, pc):
            return out_ref.at[4 * px + 2 * py + pc]

        def copy(k, blk, to, src=None):
            return pltpu.make_async_remote_copy(
                src_ref=block(*blk) if src is None else src, dst_ref=block(*blk),
                send_sem=send_sems.at[k], recv_sem=recv_sems.at[k], device_id=to, device_id_type=MESH)

        mine = pltpu.make_async_copy(x_ref, block(*me), local_sem)
        mine.start()
        first = [copy(0, me, sibling, src=x_ref)]
        first += [copy(1 + j, me, (*chip, mc), src=x_ref) for j, chip in enumerate(chips)]
        for cp in first:
            cp.start()
        passed = [copy(4 + j, (*chip, mc), sibling) for j, chip in enumerate(chips)]
        for j, chip in enumerate(chips):
            copy(1 + j, (*chip, mc), me).wait_recv()
            passed[j].start()
        copy(0, sibling, me).wait_recv()
        for j, chip in enumerate(chips):
            copy(4 + j, (*chip, 1 - mc), me).wait_recv()
        for cp in first + passed:
            cp.wait_send()
        mine.wait()

    return pl.pallas_call(
        body, name='all_gather_weights',
        out_shape=jax.ShapeDtypeStruct((N_DEV, rows, LANES), x.dtype),
        in_specs=[pl.BlockSpec(memory_space=pl.ANY)], out_specs=pl.BlockSpec(memory_space=pl.ANY),
        scratch_shapes=[pltpu.SemaphoreType.DMA((7,)), pltpu.SemaphoreType.DMA((7,)), pltpu.SemaphoreType.DMA],
    )(x)


HBM_SPEC = pl.BlockSpec(memory_space=pltpu.HBM)
SEM_SPEC = pl.BlockSpec(memory_space=pltpu.SEMAPHORE)
DATAFLOW_EFFECT = pltpu.SideEffectType.DATAFLOW_SIDE_EFFECTING


def _direct_copies(src_ref, land_ref, send_sems, recv_sems, per_peer):
    mx, my, mc = _my_pos()
    me = 4 * mx + 2 * my + mc
    copies = []
    for k in range(1, N_DEV):
        peer = (_flip(mx, k & 4), _flip(my, k & 2), _flip(mc, k & 1))
        pidx = 4 * peer[0] + 2 * peer[1] + peer[2]
        copies.append(pltpu.make_async_remote_copy(
            src_ref=src_ref.at[pidx] if per_peer else src_ref, dst_ref=land_ref.at[me],
            send_sem=send_sems.at[k - 1], recv_sem=recv_sems.at[k - 1], device_id=peer, device_id_type=MESH))
    return copies


def send_start(src, per_peer, name):
    block = src.shape[1:] if per_peer else src.shape
    land_shape = (N_DEV,) + tuple(block)

    def body(src_ref, land_ref, send_sems, recv_sems, src_thru, land_thru, token):
        for cp in _direct_copies(src_ref, land_ref, send_sems, recv_sems, per_peer):
            cp.start()
        token[...] = jnp.zeros(token.shape, F32)

    send_sems, recv_sems, src_thru, land_thru, token = pl.pallas_call(
        body, name=name,
        out_shape=(pltpu.SemaphoreType.DMA((N_DEV - 1,)), pltpu.SemaphoreType.DMA((N_DEV - 1,)),
                   pltpu.HBM(src.shape, src.dtype), pltpu.HBM(land_shape, src.dtype),
                   jax.ShapeDtypeStruct((8, LANES), F32)),
        in_specs=(HBM_SPEC, HBM_SPEC),
        out_specs=(SEM_SPEC, SEM_SPEC, HBM_SPEC, HBM_SPEC, pl.BlockSpec(memory_space=pltpu.VMEM)),
        input_output_aliases={0: 2, 1: 3},
        compiler_params=pltpu.CompilerParams(has_side_effects=DATAFLOW_EFFECT),
    )(pltpu.with_memory_space_constraint(src, pltpu.HBM),
      pltpu.with_memory_space_constraint(lax.empty(land_shape, src.dtype), pltpu.HBM))
    return (send_sems, recv_sems, src_thru, land_thru), token[0, 0]


def send_wait(handles, after, per_peer, name):
    send_sems, recv_sems, src_thru, land_thru = handles

    def body(src_ref, land_ref, send_sems, recv_sems, after_ref, src_dead, got_ref):
        for cp in _direct_copies(src_ref, land_ref, send_sems, recv_sems, per_peer):
            cp.wait_send()
            cp.wait_recv()

    return pl.pallas_call(
        body, name=name,
        out_shape=(pltpu.HBM(src_thru.shape, src_thru.dtype), pltpu.HBM(land_thru.shape, land_thru.dtype)),
        in_specs=(HBM_SPEC, HBM_SPEC, SEM_SPEC, SEM_SPEC, pl.BlockSpec(memory_space=pl.ANY)),
        out_specs=(HBM_SPEC, HBM_SPEC), input_output_aliases={0: 0, 1: 1},
        compiler_params=pltpu.CompilerParams(has_side_effects=DATAFLOW_EFFECT),
    )(src_thru, land_thru, send_sems, recv_sems, after)[1]


def _adamw_math(w, g, m, v):
    m2 = ADAM_B1 * m + (1.0 - ADAM_B1) * g
    v2 = ADAM_B2 * v + (1.0 - ADAM_B2) * (g * g)
    m_hat = m2 / (1.0 - ADAM_B1 ** ADAM_STEP)
    v_hat = v2 / (1.0 - ADAM_B2 ** ADAM_STEP)
    delta = -ADAM_LR * (m_hat / (jnp.sqrt(v_hat) + ADAM_EPS) + ADAM_WD * w)
    return delta, m2, v2


def sum_parts(parts, name):
    _, rows, cols = parts.shape
    rb = rows
    while N_DEV * rb * cols * 2 > (2 << 20) and rb % 32 == 0:
        rb //= 2

    def body(p_ref, g_ref):
        g = p_ref[0].astype(F32)
        for j in range(1, N_DEV):
            g = g + p_ref[j].astype(F32)
        g_ref[...] = g

    return pl.pallas_call(
        body, name=name, grid=(rows // rb,),
        in_specs=[pl.BlockSpec((N_DEV, rb, cols), lambda i: (0, i, 0))],
        out_specs=pl.BlockSpec((rb, cols), lambda i: (i, 0)),
        out_shape=jax.ShapeDtypeStruct((rows, cols), F32), compiler_params=_cparams(('parallel',)),
    )(parts)


ADAMW_BLOCK_BYTES = 1 << 20


def adamw_weight(name, w, m, v, grads):
    layers, a, b = w.shape
    ra = a
    while ra * b * 4 > ADAMW_BLOCK_BYTES and ra % 16 == 0:
        ra //= 2

    def body(*refs):
        w_ref, m_ref, v_ref = refs[:3]
        g_refs = refs[3:3 + layers]
        g_ref, d_ref, m2_ref, v2_ref = refs[3 + layers:]
        g = g_refs[0][...]
        for l in range(1, layers):
            g = jnp.where(pl.program_id(0) == l, g_refs[l][...], g)
        delta, m2, v2 = _adamw_math(w_ref[0], g, m_ref[0], v_ref[0])
        g_ref[0] = g
        d_ref[0] = delta
        m2_ref[0] = m2
        v2_ref[0] = v2

    blk = pl.BlockSpec((1, ra, b), lambda l, i: (l, i, 0))
    gblk = pl.BlockSpec((ra, b), lambda l, i: (i, 0))
    sh = jax.ShapeDtypeStruct(w.shape, F32)
    return pl.pallas_call(
        body, name='adamw_' + name, grid=(layers, a // ra), in_specs=[blk] * 3 + [gblk] * layers,
        out_specs=[blk] * 4, out_shape=[sh] * 4, compiler_params=_cparams(('parallel', 'parallel')),
    )(w, m, v, *grads)


def allreduce_adamw_small(g, w, m, v):
    rows = g.shape[0]

    def body(g_ref, w_ref, m_ref, v_ref, gs_ref, d_ref, m2_ref, v2_ref, all_ref, send_sems, recv_sems):
        mx, my, mc = _my_pos()
        me, sibling = (mx, my, mc), (mx, my, 1 - mc)
        chips = [(1 - mx, my), (mx, 1 - my), (1 - mx, 1 - my)]

        def block(px, py, pc):
            return all_ref.at[4 * px + 2 * py + pc]

        def copy(k, blk, to, src=None):
            return pltpu.make_async_remote_copy(
                src_ref=block(*blk) if src is None else src, dst_ref=block(*blk),
                send_sem=send_sems.at[k], recv_sem=recv_sems.at[k], device_id=to, device_id_type=MESH)

        first = [copy(0, me, sibling, src=g_ref)]
        first += [copy(1 + j, me, (*chip, mc), src=g_ref) for j, chip in enumerate(chips)]
        for cp in first:
            cp.start()
        passed = [copy(4 + j, (*chip, mc), sibling) for j, chip in enumerate(chips)]
        for j, chip in enumerate(chips):
            copy(1 + j, (*chip, mc), me).wait_recv()
            passed[j].start()
        copy(0, sibling, me).wait_recv()
        for j, chip in enumerate(chips):
            copy(4 + j, (*chip, 1 - mc), me).wait_recv()
        for cp in first + passed:
            cp.wait_send()
        my_idx = 4 * mx + 2 * my + mc
        total = jnp.zeros((rows, LANES), F32)
        for j in range(N_DEV):
            total = total + jnp.where(my_idx == j, g_ref[...], all_ref[j])
        delta, m2, v2 = _adamw_math(w_ref[...], total, m_ref[...], v_ref[...])
        gs_ref[...] = total
        d_ref[...] = delta
        m2_ref[...] = m2
        v2_ref[...] = v2

    vm = pl.BlockSpec(memory_space=pltpu.VMEM)
    sh = jax.ShapeDtypeStruct((rows, LANES), F32)
    return pl.pallas_call(
        body, name='allreduce_adamw_small', in_specs=[vm] * 4, out_specs=[vm] * 4, out_shape=[sh] * 4,
        scratch_shapes=[pltpu.VMEM((N_DEV, rows, LANES), F32), pltpu.SemaphoreType.DMA((7,)),
                        pltpu.SemaphoreType.DMA((7,))],
    )(g, w, m, v)


SMALL_COUNT = 11680


def _small_pack(d, extra=None):
    assert sum(d[n].size for n in SMALL_NAMES) == SMALL_COUNT
    flat = jnp.concatenate([d[n].reshape(-1) for n in SMALL_NAMES] + ([] if extra is None else [extra.reshape(1)]))
    rows = -(-(SMALL_COUNT + 1) // (8 * LANES)) * 8
    return jnp.pad(flat, (0, rows * LANES - flat.shape[0])).reshape(rows, LANES)


def _small_unpack(buf, like):
    flat = buf.reshape(-1)
    out, off = {}, 0
    for n in SMALL_NAMES:
        sz = int(np.prod(like[n].shape))
        out[n] = flat[off:off + sz].reshape(like[n].shape)
        off += sz
    return out


def _rope_tables(positions):
    freqs = ROPE_THETA ** (-(jnp.arange(QK_ROPE // 2, dtype=F32) * 2.0 / QK_ROPE))
    ang = positions.astype(F32)[:, None] * freqs
    return jnp.tile(jnp.cos(ang), (1, MLA_HEADS)), jnp.tile(jnp.sin(ang), (1, MLA_HEADS))


def _layer_weights(gathered, carriers):
    wd = {}
    for name, full in gathered.items():
        if name == 'w_in':
            wd[name] = {seg: W(full[lo:hi], c) for (seg, lo, hi), c in zip(WIN_SEGS, carriers[name])}
        else:
            wd[name] = W(full, carriers[name])
    return wd


F32_GRAD_WEIGHTS = ('rwkv_w2', 'rwkv_a2', 'rwkv_g2', 'rwkv_v2', 'conv_w')


def _make_carriers(gathered):
    gathered, carriers = dict(gathered), {}
    for name, full in gathered.items():
        if name == 'w_in':
            carriers[name] = tuple(jnp.zeros((hi - lo, D_MODEL), BF16) for _, lo, hi in WIN_SEGS)
        elif name == 'conv_w':
            gathered[name] = full.astype(BF16)
            carriers[name] = full - full.astype(BF16).astype(F32)
        else:
            carriers[name] = jnp.zeros(full.shape, F32 if name in F32_GRAD_WEIGHTS else BF16)
    return gathered, carriers


def _layer_small(small, ml):
    out = {}
    for n in SMALL_NAMES:
        l = ml if small[n].shape[0] == DEPTH else ml - 1
        if 0 <= l < small[n].shape[0]:
            out[n] = small[n][l]
    return out


def _after(value, tokens):
    return value + sum(tokens[1:], tokens[0])


def _train_step(x, positions, loss_target, weights, moms_m, moms_v):
    shards = {n: weights[n] for n in SHARDED}
    small = {n: weights[n] for n in SMALL_NAMES}
    me = 4 * lax.axis_index('x') + 2 * lax.axis_index('y') + lax.axis_index('c')
    cos, sin = _rope_tables(positions[0])

    units = [(kind, l) for l in range(DEPTH) for kind in ('pack',) + MLP_WEIGHTS]
    first, later = units[0], units[1:]
    tag = lambda u: f'{u[0]}_l{u[1]}'

    def own_block(u):
        kind, l = u
        return _pack_shards(shards, l, BF16, True) if kind == 'pack' else shards[kind][l].astype(BF16)

    own = {u: own_block(u) for u in units}
    held = lax.optimization_barrier((all_gather_blocks(own[first]), *[own[u] for u in later]))
    blocks, gathers, tokens = {first: held[0]}, {}, []
    for u, mine in zip(later, held[1:]):
        own[u] = mine
        gathers[u], token = send_start(mine, False, f'gather_{tag(u)}_start')
        tokens.append(token)
    x0 = _after(x[0], tokens)

    def gathered_block(u, after):
        if u not in blocks:
            landed = send_wait(gathers[u], after, False, f'gather_{tag(u)}_wait')
            blocks[u] = lax.dynamic_update_slice(landed, own[u][None], (me,) + (0,) * own[u].ndim)
        return blocks[u]

    def mixer_weights(l, after):
        return _make_carriers(_unpack_gathered(gathered_block(('pack', l), after), l))

    def mlp_weights(l, after):
        up = gathered_block(('w_up', l), after)
        down = gathered_block(('w_down', l), after).reshape(D_FF, D_MODEL)
        return _make_carriers({'w_up': up, 'w_down': down})

    def small_of(layer, part):
        sp = _layer_small(small, layer)
        return {n: v for n, v in sp.items() if (n == 'mlp_norm') == (part == 'mlp')}

    def mixer(layer, gathered):
        if layer == 0:
            return lambda c, s, xx: _layer(0, xx, None, _layer_weights(gathered, c), s, cos, sin)
        return lambda c, s, xx, vf: _layer(layer, xx, vf, _layer_weights(gathered, c), s, cos, sin)[0]

    def mlp(layer, gathered):
        return lambda c, s, xx: _mlp(layer, xx, _layer_weights(gathered, c), s)

    gathered, carriers = mixer_weights(0, None)
    (h, v_first), vjp_mix0 = jax.vjp(mixer(0, gathered), carriers, small_of(0, 'mix'), x0)
    gathered, carriers = mlp_weights(0, h)
    h, vjp_mlp0 = jax.vjp(mlp(0, gathered), carriers, small_of(0, 'mlp'), h)
    gathered, carriers = mixer_weights(1, h)
    h, vjp_mix1 = jax.vjp(mixer(1, gathered), carriers, small_of(1, 'mix'), h, v_first)
    gathered, carriers = mlp_weights(1, h)
    y, vjp_mlp1 = jax.vjp(mlp(1, gathered), carriers, small_of(1, 'mlp'), h)

    dy, loss_parts = _loss_call(y, loss_target[0])

    shipped = {}

    def ship(u, to_send):
        handles, token = send_start(to_send, True, f'grads_{tag(u)}_start')
        shipped[u] = (handles, to_send)
        return token

    def ship_mlp(l, gw):
        down = gw['w_down'].reshape(N_DEV, D_FF // N_DEV, D_MODEL)
        return [ship(('w_up', l), gw['w_up']), ship(('w_down', l), down)]

    gw, gs_mlp1, d = vjp_mlp1(dy)
    d = _after(d, ship_mlp(1, gw))
    gw, gs_mix1, d, dvf = vjp_mix1(d)
    d = _after(d, [ship(('pack', 1), _pack_grads(gw, 1))])
    gw, gs_mlp0, d = vjp_mlp0(d)
    d = _after(d, ship_mlp(0, gw))
    gw, gs_mix0, g_x = vjp_mix0((d, dvf))
    last_token = ship(first, _pack_grads(gw, 0))
    gs0, gs1 = {**gs_mix0, **gs_mlp0}, {**gs_mix1, **gs_mlp1}

    def arrived(u, after):
        handles, sent = shipped[u]
        landed = send_wait(handles, after, True, f'grads_{tag(u)}_wait')
        mine = lax.dynamic_slice_in_dim(sent, me, 1, axis=0)
        landed = lax.dynamic_update_slice(landed, mine, (me,) + (0,) * (sent.ndim - 1))
        return sum_parts(landed, f'sum_grads_{tag(u)}')

    after = _after(g_x, [last_token])
    grads = {u: arrived(u, after) for u in later if u[0] != 'pack'}
    grads.update(_unpack_shards(arrived(('pack', 1), after), 1))
    sharded_out = [{}, {}, {}, {}]

    def update(n):
        outs = adamw_weight(n, weights[n], moms_m[n], moms_v[n], [grads[(n, l)] for l in range(SHARDED[n][0])])
        for i in range(4):
            sharded_out[i][n] = outs[i]

    for n in MLP_WEIGHTS:
        update(n)
    grads.update(_unpack_shards(arrived(first, sharded_out[0][MLP_WEIGHTS[-1]]), 0))
    for n in SHARDED:
        if n not in MLP_WEIGHTS:
            update(n)

    g_small = {}
    for n in SMALL_NAMES:
        per = [g[n] for g in (gs0, gs1) if n in g]
        g_small[n] = jnp.stack(per)
    small_bufs = allreduce_adamw_small(
        _small_pack(g_small, jnp.sum(loss_parts)), _small_pack(small), _small_pack({n: moms_m[n] for n in SMALL_NAMES}),
        _small_pack({n: moms_v[n] for n in SMALL_NAMES}))
    loss = small_bufs[0].reshape(-1)[SMALL_COUNT]
    small_out = [_small_unpack(b, small) for b in small_bufs]

    pick = lambda i: [sharded_out[i][n] if n in SHARDED else small_out[i][n] for n in WEIGHT_NAMES]
    return (loss, g_x[None], *pick(0), *pick(1), *pick(2), *pick(3))


def kernel(x, positions, attn_norm, w_in, mla_q_a_norm, mla_wq_b, mla_kv_a_norm, mla_wkv_b, mla_q_norm, mla_k_norm, mla_w_o, rwkv_mu, rwkv_w0, rwkv_w2, rwkv_a0, rwkv_a2, rwkv_g2, rwkv_k_k, rwkv_k_a, rwkv_r_k, rwkv_ln_w, rwkv_ln_b, rwkv_w_o, rwkv_v1, rwkv_v_mu, rwkv_v0, rwkv_v2, conv_w, conv_w_o, w_out, mlp_norm, w_up, w_down, loss_target, m_attn_norm, m_w_in, m_mla_q_a_norm, m_mla_wq_b, m_mla_kv_a_norm, m_mla_wkv_b, m_mla_q_norm, m_mla_k_norm, m_mla_w_o, m_rwkv_mu, m_rwkv_w0, m_rwkv_w2, m_rwkv_a0, m_rwkv_a2, m_rwkv_g2, m_rwkv_k_k, m_rwkv_k_a, m_rwkv_r_k, m_rwkv_ln_w, m_rwkv_ln_b, m_rwkv_w_o, m_rwkv_v1, m_rwkv_v_mu, m_rwkv_v0, m_rwkv_v2, m_conv_w, m_conv_w_o, m_w_out, m_mlp_norm, m_w_up, m_w_down, v_attn_norm, v_w_in, v_mla_q_a_norm, v_mla_wq_b, v_mla_kv_a_norm, v_mla_wkv_b, v_mla_q_norm, v_mla_k_norm, v_mla_w_o, v_rwkv_mu, v_rwkv_w0, v_rwkv_w2, v_rwkv_a0, v_rwkv_a2, v_rwkv_g2, v_rwkv_k_k, v_rwkv_k_a, v_rwkv_r_k, v_rwkv_ln_w, v_rwkv_ln_b, v_rwkv_w_o, v_rwkv_v1, v_rwkv_v_mu, v_rwkv_v0, v_rwkv_v2, v_conv_w, v_conv_w_o, v_w_out, v_mlp_norm, v_w_up, v_w_down):
    args = locals()
    weights = {n: args[n] for n in WEIGHT_NAMES}
    moms_m = {n: args['m_' + n] for n in WEIGHT_NAMES}
    moms_v = {n: args['v_' + n] for n in WEIGHT_NAMES}
    return _train_step(x, positions, loss_target, weights, moms_m, moms_v)
```

```python
import functools

import numpy as np
import jax
import jax.numpy as jnp
from jax import lax
from jax.experimental import pallas as pl
from jax.experimental.pallas import tpu as pltpu

F32 = jnp.float32
BF16 = jnp.bfloat16

N_DEV = 8
LANES = 128
D_MODEL = 1024
DEPTH = 2
MLA_HEADS = 8
QK_NOPE = 64
QK_ROPE = 32
QK_HEAD = QK_NOPE + QK_ROPE
V_HEAD = 64
Q_LORA = 384
KV_LORA = 256
ROPE_THETA = 10000.0
RW_HEADS = 4
RW_N = 64
RW_WIDTH = RW_HEADS * RW_N
MV_LORA = 32
GN_EPS = 64e-5
CONV_WIDTH = 256
D_FF = 4 * D_MODEL
NORM_EPS = 1e-6
ADAM_LR = 0.001
ADAM_B1 = 0.9
ADAM_B2 = 0.999
ADAM_EPS = 1e-08
ADAM_WD = 0.01
ADAM_STEP = 10

VMEM_LIMIT = 56 * 1024 * 1024
MESH = pl.DeviceIdType.MESH

WEIGHT_NAMES = ['attn_norm', 'w_in', 'mla_q_a_norm', 'mla_wq_b', 'mla_kv_a_norm', 'mla_wkv_b', 'mla_q_norm',
                'mla_k_norm', 'mla_w_o', 'rwkv_mu', 'rwkv_w0', 'rwkv_w2', 'rwkv_a0', 'rwkv_a2', 'rwkv_g2',
                'rwkv_k_k', 'rwkv_k_a', 'rwkv_r_k', 'rwkv_ln_w', 'rwkv_ln_b', 'rwkv_w_o', 'rwkv_v1',
                'rwkv_v_mu', 'rwkv_v0', 'rwkv_v2', 'conv_w', 'conv_w_o', 'w_out', 'mlp_norm', 'w_up', 'w_down']

SHARDED = {
    'w_in': (2, (1024, 5536), 1), 'mla_wq_b': (2, (384, 768), 1), 'mla_wkv_b': (2, (256, 1024), 1),
    'mla_w_o': (2, (512, 1024), 1), 'rwkv_w2': (2, (64, 256), 1), 'rwkv_a2': (2, (64, 256), 1),
    'rwkv_g2': (2, (128, 256), 1), 'rwkv_w_o': (2, (256, 1024), 1), 'conv_w': (2, (3, 256), 1),
    'conv_w_o': (2, (256, 1024), 1), 'w_out': (2, (1024, 1024), 0), 'w_up': (2, (1024, 4096), 1),
    'w_down': (2, (4096, 1024), 0), 'rwkv_v1': (1, (1024, 32), 0), 'rwkv_v2': (1, (32, 256), 1),
}
SMALL_NAMES = [n for n in WEIGHT_NAMES if n not in SHARDED]
TRANSPOSED = ('w_in',)
WIN_SEGS = (('gates', 0, 3072), ('cq', 3072, 3456), ('ckv', 3456, 3712), ('kpe', 3712, 3744), ('rkv', 3744, 4512),
            ('xwa', 4512, 4640), ('xg', 4640, 4768), ('conv', 4768, 5536))
PACK_ROW_MULT = 512


def _cparams(sem=None, **kw):
    if sem is not None:
        kw['dimension_semantics'] = sem
    return pltpu.CompilerParams(vmem_limit_bytes=VMEM_LIMIT, **kw)


def _pick(n, cands):
    for c in cands:
        if n % c == 0:
            return c
    raise ValueError(f'no tile for {n}')


def _mm_nn(a, b, add=None, name='mm_nn'):
    M, K = a.shape
    N = b.shape[1]
    tm = _pick(M, (1024, 512, 256, 128))
    tn = _pick(N, (512, 384, 256, 128))
    tk = _pick(K, (1024, 512, 384, 256, 128))
    nk = K // tk
    has_add = add is not None

    def body(*refs):
        if has_add:
            a_ref, b_ref, add_ref, o_ref, acc_ref = refs
        else:
            a_ref, b_ref, o_ref, acc_ref = refs
        kk = pl.program_id(2)
        part = jnp.dot(a_ref[...].astype(BF16), b_ref[...].astype(BF16), preferred_element_type=F32)

        @pl.when(kk == 0)
        def _():
            acc_ref[...] = part

        @pl.when(kk > 0)
        def _():
            acc_ref[...] += part

        @pl.when(kk == nk - 1)
        def _():
            if has_add:
                o_ref[...] = acc_ref[...] + add_ref[...]
            else:
                o_ref[...] = acc_ref[...]

    in_specs = [pl.BlockSpec((tm, tk), lambda i, j, k: (i, k)), pl.BlockSpec((tk, tn), lambda i, j, k: (k, j))]
    args = [a, b]
    if has_add:
        in_specs.append(pl.BlockSpec((tm, tn), lambda i, j, k: (i, j)))
        args.append(add)
    return pl.pallas_call(
        body, name=name, grid=(M // tm, N // tn, nk), in_specs=in_specs,
        out_specs=pl.BlockSpec((tm, tn), lambda i, j, k: (i, j)),
        out_shape=jax.ShapeDtypeStruct((M, N), F32),
        scratch_shapes=[pltpu.VMEM((tm, tn), F32)],
        compiler_params=_cparams(('parallel', 'parallel', 'arbitrary')),
    )(*args)


def _mm_nt(a, b, add=None, name='mm_nt'):
    M, N = a.shape
    blocked = b.ndim == 3
    K = b.shape[-2]
    tm = _pick(M, (1024, 512, 256, 128))
    tk = _pick(K, (512, 384, 256, 128))
    tn = N // N_DEV if blocked else _pick(N, (1024, 512, 384, 256, 128))
    nn = N // tn
    has_add = add is not None

    def body(*refs):
        if has_add:
            a_ref, b_ref, add_ref, o_ref, acc_ref = refs
        else:
            a_ref, b_ref, o_ref, acc_ref = refs
        kk = pl.program_id(2)
        part = lax.dot_general(a_ref[...].astype(BF16), b_ref[...].astype(BF16), (((1,), (1,)), ((), ())),
                               preferred_element_type=F32)

        @pl.when(kk == 0)
        def _():
            acc_ref[...] = part

        @pl.when(kk > 0)
        def _():
            acc_ref[...] += part

        @pl.when(kk == nn - 1)
        def _():
            if has_add:
                o_ref[...] = acc_ref[...] + add_ref[...]
            else:
                o_ref[...] = acc_ref[...]

    if blocked:
        b_spec = pl.BlockSpec((None, tk, tn), lambda i, j, k: (k, j, 0))
    else:
        b_spec = pl.BlockSpec((tk, tn), lambda i, j, k: (j, k))
    in_specs = [pl.BlockSpec((tm, tn), lambda i, j, k: (i, k)), b_spec]
    args = [a, b]
    if has_add:
        in_specs.append(pl.BlockSpec((tm, tk), lambda i, j, k: (i, j)))
        args.append(add)
    return pl.pallas_call(
        body, name=name, grid=(M // tm, K // tk, nn), in_specs=in_specs,
        out_specs=pl.BlockSpec((tm, tk), lambda i, j, k: (i, j)),
        out_shape=jax.ShapeDtypeStruct((M, K), F32),
        scratch_shapes=[pltpu.VMEM((tm, tk), F32)],
        compiler_params=_cparams(('parallel', 'parallel', 'arbitrary')),
    )(*args)


def _mm_tn(a, b, name='mm_tn', blocked=False):
    M, K = a.shape
    N = b.shape[1]
    tm = _pick(M, (1024, 512, 256, 128))
    tk = _pick(K, (512, 384, 256, 128))
    tn = N // N_DEV if blocked else _pick(N, (512, 384, 256, 128))
    nm = M // tm

    def body(a_ref, b_ref, o_ref, acc_ref):
        mm = pl.program_id(2)
        part = lax.dot_general(a_ref[...].astype(BF16), b_ref[...].astype(BF16), (((0,), (0,)), ((), ())),
                               preferred_element_type=F32)

        @pl.when(mm == 0)
        def _():
            acc_ref[...] = part

        @pl.when(mm > 0)
        def _():
            acc_ref[...] += part

        @pl.when(mm == nm - 1)
        def _():
            o_ref[...] = acc_ref[...].astype(BF16)

    if blocked:
        out_spec = pl.BlockSpec((None, tk, tn), lambda i, j, m: (j, i, 0))
        out_shape = jax.ShapeDtypeStruct((N_DEV, K, tn), BF16)
    else:
        out_spec = pl.BlockSpec((tk, tn), lambda i, j, m: (i, j))
        out_shape = jax.ShapeDtypeStruct((K, N), BF16)
    return pl.pallas_call(
        body, name=name, grid=(K // tk, N // tn, nm),
        in_specs=[pl.BlockSpec((tm, tk), lambda i, j, m: (m, i)), pl.BlockSpec((tm, tn), lambda i, j, m: (m, j))],
        out_specs=out_spec, out_shape=out_shape,
        scratch_shapes=[pltpu.VMEM((tk, tn), F32)],
        compiler_params=_cparams(('parallel', 'parallel', 'arbitrary')),
    )(a, b)


@functools.partial(jax.custom_vjp, nondiff_argnums=(4,))
def _linear_add(a, wb, wc, add, name):
    return _mm_nn(a, wb, add, name=name + '_f')


def _linear_add_fwd(a, wb, wc, add, name):
    return _mm_nn(a, wb, add, name=name + '_f'), (a, wb)


def _linear_add_bwd(name, res, dy):
    a, wb = res
    return _mm_nt(dy, wb, name=name + '_da'), None, _mm_tn(a, dy, name=name + '_dw'), dy


_linear_add.defvjp(_linear_add_fwd, _linear_add_bwd)


@functools.partial(jax.custom_vjp, nondiff_argnums=(3,))
def _multi_linear(a, wbs, wcs, name):
    return tuple(_mm_nn(a, wb, name=f'{name}_f{i}') for i, wb in enumerate(wbs))


def _multi_linear_fwd(a, wbs, wcs, name):
    return _multi_linear(a, wbs, wcs, name), (a, wbs)


def _multi_linear_bwd(name, res, dys):
    a, wbs = res
    da = None
    for i, (dy, wb) in enumerate(zip(dys, wbs)):
        da = _mm_nt(dy, wb, add=da, name=f'{name}_da{i}')
    dws = tuple(_mm_tn(a, dy, name=f'{name}_dw{i}') for i, dy in enumerate(dys))
    return da, None, dws


_multi_linear.defvjp(_multi_linear_fwd, _multi_linear_bwd)


class W:
    def __init__(self, b, c):
        self.b, self.c = b, c

    def map(self, fn):
        return W(fn(self.b), fn(self.c))


def _wcat(ws, axis):
    return W(jnp.concatenate([w.b for w in ws], axis), jnp.concatenate([w.c for w in ws], axis))


def linear(a, w, add=None, name='lin'):
    if add is None:
        return _multi_linear(a, (w.b,), (w.c,), name)[0]
    return _linear_add(a, w.b, w.c, add, name)


def ROW(diff=True, pieces=None):
    return ('row', diff, pieces)


def FULL(diff=True):
    return ('full', diff, None)


def _load_args(refs, specs):
    args, amap = [], []
    for i, (ref, (kind, diff, pieces)) in enumerate(zip(refs, specs)):
        if pieces is None:
            args.append(ref[...])
            amap.append((i, None))
        else:
            for (s, w) in pieces:
                args.append(ref[:, s:s + w])
                amap.append((i, (s, w)))
    return args, amap


def _stage_in_specs(ins, specs, tb):
    out = []
    for a, (kind, _, _) in zip(ins, specs):
        if kind == 'row':
            out.append(pl.BlockSpec((tb, a.shape[1]), lambda i: (i, 0)))
        else:
            out.append(pl.BlockSpec(a.shape, lambda i: (0, 0)))
    return out


def _stage_fwd(fn, ins, specs, out_widths, name, tb, out_dtype=F32):
    T = [a for a, s in zip(ins, specs) if s[0] == 'row'][0].shape[0]
    tb = min(tb, T)
    n_in = len(ins)

    def body(*refs):
        args, _ = _load_args(refs[:n_in], specs)
        outs = fn(*args)
        for o_ref, o in zip(refs[n_in:], outs):
            o_ref[...] = o.astype(out_dtype)

    return pl.pallas_call(
        body, name=name + '_f', grid=(T // tb,), in_specs=_stage_in_specs(ins, specs, tb),
        out_specs=[pl.BlockSpec((tb, w), lambda i: (i, 0)) for w in out_widths],
        out_shape=[jax.ShapeDtypeStruct((T, w), out_dtype) for w in out_widths],
        compiler_params=_cparams(('parallel',)),
    )(*ins)


def _stage_bwd(fn, ins, specs, out_widths, douts, name, tb, add_to_first=None):
    T = [a for a, s in zip(ins, specs) if s[0] == 'row'][0].shape[0]
    tb = min(tb, T)
    n_in, n_out = len(ins), len(out_widths)
    diff_inputs = [i for i, s in enumerate(specs) if s[1]]
    n_add = 0 if add_to_first is None else 1

    def body(*refs):
        in_refs, dout_refs = refs[:n_in], refs[n_in:n_in + n_out]
        g_refs = refs[n_in + n_out + n_add:]
        args, amap = _load_args(in_refs, specs)
        didx = [j for j, (i, _) in enumerate(amap) if specs[i][1]]

        def f(*dv):
            full = list(args)
            for j, v in zip(didx, dv):
                full[j] = v
            return tuple(fn(*full))

        _, vjp = jax.vjp(f, *[args[j] for j in didx])
        gs = vjp(tuple(d[...] for d in dout_refs))
        gmap = {j: g for j, g in zip(didx, gs)}
        first = pl.program_id(0) == 0
        for g_ref, i in zip(g_refs, diff_inputs):
            kind, _, pieces = specs[i]
            js = [j for j, (ii, _) in enumerate(amap) if ii == i]
            if kind == 'row':
                if pieces is None:
                    if n_add and i == diff_inputs[0]:
                        g_ref[...] = gmap[js[0]] + refs[n_in + n_out][...]
                    else:
                        g_ref[...] = gmap[js[0]]
                else:
                    if sum(w for _, w in pieces) != ins[i].shape[1]:
                        g_ref[...] = jnp.zeros(g_ref.shape, F32)
                    for j in js:
                        s, w = amap[j][1]
                        g_ref[:, s:s + w] = gmap[j]
            else:
                @pl.when(first)
                def _(g_ref=g_ref):
                    g_ref[...] = jnp.zeros(g_ref.shape, F32)

                g_ref[...] += gmap[js[0]]

    in_specs = _stage_in_specs(ins, specs, tb) + [pl.BlockSpec((tb, w), lambda i: (i, 0)) for w in out_widths]
    extra = []
    if n_add:
        in_specs.append(pl.BlockSpec((tb, add_to_first.shape[1]), lambda i: (i, 0)))
        extra.append(add_to_first)
    out_specs, out_shape = [], []
    for i in diff_inputs:
        a = ins[i]
        if specs[i][0] == 'row':
            out_specs.append(pl.BlockSpec((tb, a.shape[1]), lambda i: (i, 0)))
        else:
            out_specs.append(pl.BlockSpec(a.shape, lambda i: (0, 0)))
        out_shape.append(jax.ShapeDtypeStruct(a.shape, F32))
    return pl.pallas_call(
        body, name=name + '_b', grid=(T // tb,), in_specs=in_specs, out_specs=out_specs, out_shape=out_shape,
        compiler_params=_cparams(('arbitrary',)),
    )(*ins, *douts, *extra)


def stage_op(fn, specs, out_widths, name, tb=256):
    n = len(specs)
    diff_inputs = [i for i, s in enumerate(specs) if s[1]]

    @jax.custom_vjp
    def op(*ins):
        return tuple(_stage_fwd(fn, ins, specs, out_widths, name, tb))

    def op_fwd(*ins):
        return op(*ins), ins

    def op_bwd(ins, douts):
        gs = _stage_bwd(fn, ins, specs, out_widths, douts, name, tb)
        res = [None] * n
        for i, g in zip(diff_inputs, gs):
            res[i] = g
        return tuple(res)

    op.defvjp(op_fwd, op_bwd)
    return op


@jax.custom_vjp
def bdot(x, w):
    return jnp.dot(x.astype(BF16), w.astype(BF16), preferred_element_type=F32)


def _bdot_fwd(x, w):
    return bdot(x, w), (x, w)


def _bdot_bwd(res, dy):
    x, w = res
    dyb = dy.astype(BF16)
    dx = lax.dot_general(dyb, w.astype(BF16), (((1,), (1,)), ((), ())), preferred_element_type=F32)
    dw = lax.dot_general(x.astype(BF16), dyb, (((0,), (0,)), ((), ())), preferred_element_type=F32)
    return dx, dw


bdot.defvjp(_bdot_fwd, _bdot_bwd)


def _sdot_raw(x, c):
    hi = x.astype(BF16)
    r1 = x - hi.astype(F32)
    mid = r1.astype(BF16)
    lo = (r1 - mid.astype(F32)).astype(BF16)
    d = lambda u: jnp.dot(u, c, preferred_element_type=F32)
    return d(hi) + d(mid) + d(lo)


@jax.custom_vjp
def sdot(x, c, ct):
    return _sdot_raw(x, c)


def _sdot_fwd(x, c, ct):
    return _sdot_raw(x, c), (c, ct)


def _sdot_bwd(res, dy):
    c, ct = res
    return _sdot_raw(dy, ct), None, None


sdot.defvjp(_sdot_fwd, _sdot_bwd)


def _sigmoid(x):
    return 1.0 / (1.0 + jnp.exp(-x))


def _rms(x, g):
    return x * lax.rsqrt(jnp.mean(x * x, axis=-1, keepdims=True) + NORM_EPS) * g


def _mm_up_relu2(a, b, name):
    M, K = a.shape
    tn = b.shape[2]
    N = N_DEV * tn
    tm = _pick(M, (1024, 512, 256, 128))

    def body(a_ref, b_ref, u_ref, act_ref):
        u = jnp.dot(a_ref[...], b_ref[...], preferred_element_type=F32)
        r = jnp.maximum(u, 0.0)
        u_ref[...] = u.astype(BF16)
        act_ref[...] = (r * r).astype(BF16)

    out = pl.BlockSpec((tm, tn), lambda i, j: (i, j))
    sh = jax.ShapeDtypeStruct((M, N), BF16)
    return pl.pallas_call(
        body, name=name, grid=(M // tm, N // tn),
        in_specs=[pl.BlockSpec((tm, K), lambda i, j: (i, 0)), pl.BlockSpec((None, K, tn), lambda i, j: (j, 0, 0))],
        out_specs=[out, out], out_shape=[sh, sh], compiler_params=_cparams(('parallel', 'parallel')),
    )(a, b)


def _mm_down_bwd(dy, b, u, name):
    M, N = dy.shape
    K = b.shape[0]
    tm = _pick(M, (1024, 512, 256, 128))
    tk = _pick(K, (512, 256, 128))

    def body(dy_ref, b_ref, u_ref, du_ref):
        d = lax.dot_general(dy_ref[...].astype(BF16), b_ref[...], (((1,), (1,)), ((), ())),
                            preferred_element_type=F32)
        du_ref[...] = (d * (2.0 * jnp.maximum(u_ref[...].astype(F32), 0.0))).astype(BF16)

    blk = pl.BlockSpec((tm, tk), lambda i, j: (i, j))
    return pl.pallas_call(
        body, name=name, grid=(M // tm, K // tk),
        in_specs=[pl.BlockSpec((tm, N), lambda i, j: (i, 0)), pl.BlockSpec((tk, N), lambda i, j: (j, 0)), blk],
        out_specs=blk, out_shape=jax.ShapeDtypeStruct((M, K), BF16),
        compiler_params=_cparams(('parallel', 'parallel')),
    )(dy, b, u)


_RMS_SPECS = [ROW(), FULL()]
_rms_fn = lambda xv, gv: (_rms(xv, gv),)


@functools.partial(jax.custom_vjp, nondiff_argnums=(6,))
def mlp_block(x, g, wup_b, wup_c, wdown_b, wdown_c, name):
    return _mlp_fwd(x, g, wup_b, wup_c, wdown_b, wdown_c, name)[0]


def _mlp_fwd(x, g, wup_b, wup_c, wdown_b, wdown_c, name):
    h = _stage_fwd(_rms_fn, (x, g), _RMS_SPECS, [x.shape[1]], name + '_norm', 256, out_dtype=BF16)[0]
    u, act = _mm_up_relu2(h, wup_b, name + '_up')
    y = _mm_nn(act, wdown_b, add=x, name=name + '_down')
    return y, (x, g, h, u, act, wup_b, wdown_b)


def _mlp_bwd(name, res, dy):
    x, g, h, u, act, wup_b, wdown_b = res
    du = _mm_down_bwd(dy, wdown_b, u, name + '_down_da')
    dwdown = _mm_tn(act, dy, name=name + '_down_dw')
    dh = _mm_nt(du, wup_b, name=name + '_up_da')
    dwup = _mm_tn(h, du, name=name + '_up_dw', blocked=True)
    dx, dg = _stage_bwd(_rms_fn, (x, g), _RMS_SPECS, [x.shape[1]], (dh,), name + '_norm', 256, add_to_first=dy)
    return dx, dg, None, dwup, None, dwdown


mlp_block.defvjp(_mlp_fwd, _mlp_bwd)


@functools.partial(jax.custom_vjp, nondiff_argnums=(4, 5))
def _norm_projections(x, g, wbs, wcs, transposed, name):
    return _norm_projections_fwd(x, g, wbs, wcs, transposed, name)[0]


def _norm_projections_fwd(x, g, wbs, wcs, transposed, name):
    h = _stage_fwd(_rms_fn, (x, g), _RMS_SPECS, [x.shape[1]], name + '_norm', 256, out_dtype=BF16)[0]
    mm = _mm_nt if transposed else _mm_nn
    return tuple(mm(h, wb, name=f'{name}_f{i}') for i, wb in enumerate(wbs)), (x, g, h, wbs)


def _norm_projections_bwd(transposed, name, res, dys):
    x, g, h, wbs = res
    dh = None
    for i, (dy, wb) in enumerate(zip(dys, wbs)):
        dh = (_mm_nn if transposed else _mm_nt)(dy, wb, add=dh, name=f'{name}_da{i}')
    if transposed:
        dws = tuple(_mm_tn(dy, h, name=f'{name}_dw{i}') for i, dy in enumerate(dys))
    else:
        dws = tuple(_mm_tn(h, dy, name=f'{name}_dw{i}') for i, dy in enumerate(dys))
    dx, dg = _stage_bwd(_rms_fn, (x, g), _RMS_SPECS, [x.shape[1]], (dh,), name + '_norm', 256)
    return dx, dg, None, dws


_norm_projections.defvjp(_norm_projections_fwd, _norm_projections_bwd)


def norm_projections(x, g, ws, name, transposed=False):
    return _norm_projections(x, g.reshape(1, -1), tuple(w.b for w in ws), tuple(w.c for w in ws), transposed, name)


def _shift_down(x, rows):
    return jnp.where(rows == 0, 0.0, pltpu.roll(x, 1, 0))


def _shift_up(x, rows, T):
    return jnp.where(rows == T - 1, 0.0, pltpu.roll(x, T - 1, 0))


def _tshift_fwd_call(x, mu, name):
    T, C = x.shape

    def body(x_ref, mu_ref, o_ref):
        xv = x_ref[...]
        rows = lax.broadcasted_iota(jnp.int32, xv.shape, 0)
        o_ref[...] = xv + (_shift_down(xv, rows) - xv) * mu_ref[...]

    return pl.pallas_call(
        body, name=name + '_f', grid=(C // LANES,),
        in_specs=[pl.BlockSpec((T, LANES), lambda j: (0, j)), pl.BlockSpec((1, LANES), lambda j: (0, j))],
        out_specs=pl.BlockSpec((T, LANES), lambda j: (0, j)), out_shape=jax.ShapeDtypeStruct((T, C), F32),
        compiler_params=_cparams(('parallel',)),
    )(x, mu)


def _tshift_bwd_call(x, mu, dy, name):
    T, C = x.shape

    def body(x_ref, mu_ref, dy_ref, dx_ref, dmu_ref):
        xv, d = x_ref[...], dy_ref[...]
        rows = lax.broadcasted_iota(jnp.int32, xv.shape, 0)
        z = d * mu_ref[...]
        dx_ref[...] = d - z + _shift_up(z, rows, T)
        dmu_ref[...] = jnp.sum(d * (_shift_down(xv, rows) - xv), axis=0, keepdims=True)

    return pl.pallas_call(
        body, name=name + '_b', grid=(C // LANES,),
        in_specs=[pl.BlockSpec((T, LANES), lambda j: (0, j)), pl.BlockSpec((1, LANES), lambda j: (0, j)),
                  pl.BlockSpec((T, LANES), lambda j: (0, j))],
        out_specs=[pl.BlockSpec((T, LANES), lambda j: (0, j)), pl.BlockSpec((1, LANES), lambda j: (0, j))],
        out_shape=[jax.ShapeDtypeStruct((T, C), F32), jax.ShapeDtypeStruct((1, C), F32)],
        compiler_params=_cparams(('parallel',)),
    )(x, mu, dy)


@functools.partial(jax.custom_vjp, nondiff_argnums=(2,))
def token_shift_mix(x, mu, name):
    return _tshift_fwd_call(x, mu, name)


def _tsm_fwd(x, mu, name):
    return _tshift_fwd_call(x, mu, name), (x, mu)


def _tsm_bwd(name, res, dy):
    x, mu = res
    dx, dmu = _tshift_bwd_call(x, mu, dy, name)
    return dx, dmu


token_shift_mix.defvjp(_tsm_fwd, _tsm_bwd)


def _conv_specs(T):
    nb = CONV_WIDTH // LANES
    return [pl.BlockSpec((T, LANES), lambda j: (0, j)), pl.BlockSpec((T, LANES), lambda j: (0, nb + j)),
            pl.BlockSpec((T, LANES), lambda j: (0, 2 * nb + j)), pl.BlockSpec((3, LANES), lambda j: (0, j))]


def _conv_fwd_call(cv, w, name):
    T = cv.shape[0]

    def body(b_ref, c_ref, x_ref, w_ref, o_ref):
        u = c_ref[...] * x_ref[...]
        rows = lax.broadcasted_iota(jnp.int32, u.shape, 0)
        u1 = _shift_down(u, rows)
        u2 = _shift_down(u1, rows)
        o_ref[...] = b_ref[...] * (w_ref[0:1, :] * u2 + w_ref[1:2, :] * u1 + w_ref[2:3, :] * u)

    return pl.pallas_call(
        body, name=name + '_f', grid=(CONV_WIDTH // LANES,), in_specs=_conv_specs(T),
        out_specs=pl.BlockSpec((T, LANES), lambda j: (0, j)),
        out_shape=jax.ShapeDtypeStruct((T, CONV_WIDTH), F32), compiler_params=_cparams(('parallel',)),
    )(cv, cv, cv, w)


def _conv_bwd_call(cv, w, do, name):
    T = cv.shape[0]

    def body(b_ref, c_ref, x_ref, w_ref, do_ref, db_ref, dc_ref, dx_ref, dw_ref):
        c, x, d = c_ref[...], x_ref[...], do_ref[...]
        u = c * x
        rows = lax.broadcasted_iota(jnp.int32, u.shape, 0)
        u1 = _shift_down(u, rows)
        u2 = _shift_down(u1, rows)
        w0, w1, w2 = w_ref[0:1, :], w_ref[1:2, :], w_ref[2:3, :]
        db_ref[...] = d * (w0 * u2 + w1 * u1 + w2 * u)
        dy = d * b_ref[...]
        dy1 = _shift_up(dy, rows, T)
        dy2 = _shift_up(dy1, rows, T)
        du = w2 * dy + w1 * dy1 + w0 * dy2
        dc_ref[...] = du * x
        dx_ref[...] = du * c
        dw_ref[0:1, :] = jnp.sum(dy * u2, axis=0, keepdims=True)
        dw_ref[1:2, :] = jnp.sum(dy * u1, axis=0, keepdims=True)
        dw_ref[2:3, :] = jnp.sum(dy * u, axis=0, keepdims=True)

    blk = pl.BlockSpec((T, LANES), lambda j: (0, j))
    sh = jax.ShapeDtypeStruct((T, CONV_WIDTH), F32)
    return pl.pallas_call(
        body, name=name + '_b', grid=(CONV_WIDTH // LANES,), in_specs=_conv_specs(T) + [blk],
        out_specs=[blk, blk, blk, pl.BlockSpec((3, LANES), lambda j: (0, j))],
        out_shape=[sh, sh, sh, jax.ShapeDtypeStruct((3, CONV_WIDTH), F32)],
        compiler_params=_cparams(('parallel',)),
    )(cv, cv, cv, w, do)


@functools.partial(jax.custom_vjp, nondiff_argnums=(2,))
def short_conv(cv, w, name):
    return _conv_fwd_call(cv, w, name)


def _sc_fwd(cv, w, name):
    return _conv_fwd_call(cv, w, name), (cv, w)


def _sc_bwd(name, res, do):
    cv, w = res
    db, dc, dx, dw = _conv_bwd_call(cv, w, do, name)
    return jnp.concatenate([db, dc, dx], axis=1), dw


short_conv.defvjp(_sc_fwd, _sc_bwd)


ATT_SCALE = QK_HEAD ** -0.5
NPAIR = MLA_HEADS // 2


def _att_bq(T):
    return min(256, T)


def _att_masks(pair, j):
    lane = lax.broadcasted_iota(jnp.int32, (1, LANES), 1)
    mask_n = (lane // QK_NOPE) == j
    mask_r = (lane // (QK_ROPE // 2)) == (2 * pair + j)
    return mask_n, mask_r


def _att_probs(qcat, kcat, row0, stop):
    s = lax.dot_general(qcat, kcat, (((1,), (1,)), ((), ())), preferred_element_type=F32) * ATT_SCALE
    r = row0 + lax.broadcasted_iota(jnp.int32, s.shape, 0)
    c = lax.broadcasted_iota(jnp.int32, s.shape, 1)
    s = jnp.where(c <= r, s, -jnp.inf)
    e = jnp.exp(s - jnp.max(s, axis=-1, keepdims=True))
    return e / jnp.sum(e, axis=-1, keepdims=True)


def _att_in_specs(T):
    blk = lambda f: pl.BlockSpec((T, LANES), f)
    return [blk(lambda p: (0, p)), blk(lambda p: (0, 0)), blk(lambda p: (0, 0)),
            blk(lambda p: (0, p)), blk(lambda p: (0, 0)), blk(lambda p: (0, 0)), blk(lambda p: (0, p))]


def _att_fwd_call(qn, q1, q2, kn, k1, k2, v, name):
    T = qn.shape[0]
    bq = _att_bq(T)

    def body(qn_ref, q1_ref, q2_ref, kn_ref, k1_ref, k2_ref, v_ref, o_ref):
        pair = pl.program_id(0)
        for i in range(T // bq):
            r0, stop = i * bq, (i + 1) * bq
            kcat = jnp.concatenate([kn_ref[0:stop, :], k1_ref[0:stop, :], k2_ref[0:stop, :]], axis=1).astype(BF16)
            vb = v_ref[0:stop, :].astype(BF16)
            outs = []
            for j in range(2):
                mask_n, mask_r = _att_masks(pair, j)
                qcat = jnp.concatenate([jnp.where(mask_n, qn_ref[r0:stop, :], 0.0),
                                        jnp.where(mask_r, q1_ref[r0:stop, :], 0.0),
                                        jnp.where(mask_r, q2_ref[r0:stop, :], 0.0)], axis=1).astype(BF16)
                p = _att_probs(qcat, kcat, r0, stop)
                outs.append(jnp.dot(p.astype(BF16), vb, preferred_element_type=F32))
            mask_n0, _ = _att_masks(pair, 0)
            o_ref[r0:stop, :] = jnp.where(mask_n0, outs[0], outs[1])

    return pl.pallas_call(
        body, name=name + '_f', grid=(NPAIR,), in_specs=_att_in_specs(T),
        out_specs=pl.BlockSpec((T, LANES), lambda p: (0, p)),
        out_shape=jax.ShapeDtypeStruct((T, MLA_HEADS * V_HEAD), F32), compiler_params=_cparams(('parallel',)),
    )(qn, q1, q2, kn, k1, k2, v)


def _att_bwd_call(qn, q1, q2, kn, k1, k2, v, o, do, name):
    T = qn.shape[0]
    bq = _att_bq(T)

    def body(qn_ref, q1_ref, q2_ref, kn_ref, k1_ref, k2_ref, v_ref, o_ref, do_ref,
             dqn_ref, dq1_ref, dq2_ref, dkn_ref, dk1_ref, dk2_ref, dv_ref, dk_acc, dv_acc):
        pair = pl.program_id(0)

        @pl.when(pair == 0)
        def _():
            dq1_ref[...] = jnp.zeros(dq1_ref.shape, F32)
            dq2_ref[...] = jnp.zeros(dq2_ref.shape, F32)
            dk1_ref[...] = jnp.zeros(dk1_ref.shape, F32)
            dk2_ref[...] = jnp.zeros(dk2_ref.shape, F32)

        dk_acc[...] = jnp.zeros(dk_acc.shape, F32)
        dv_acc[...] = jnp.zeros(dv_acc.shape, F32)
        for i in range(T // bq):
            r0, stop = i * bq, (i + 1) * bq
            kcat = jnp.concatenate([kn_ref[0:stop, :], k1_ref[0:stop, :], k2_ref[0:stop, :]], axis=1).astype(BF16)
            vb = v_ref[0:stop, :].astype(BF16)
            dqn = jnp.zeros((bq, LANES), F32)
            for j in range(2):
                mask_n, mask_r = _att_masks(pair, j)
                qcat = jnp.concatenate([jnp.where(mask_n, qn_ref[r0:stop, :], 0.0),
                                        jnp.where(mask_r, q1_ref[r0:stop, :], 0.0),
                                        jnp.where(mask_r, q2_ref[r0:stop, :], 0.0)], axis=1).astype(BF16)
                p = _att_probs(qcat, kcat, r0, stop)
                dom = jnp.where(mask_n, do_ref[r0:stop, :], 0.0)
                delta = jnp.sum(dom * o_ref[r0:stop, :], axis=-1, keepdims=True)
                domb = dom.astype(BF16)
                dp = lax.dot_general(domb, vb, (((1,), (1,)), ((), ())), preferred_element_type=F32)
                ds = (p * (dp - delta) * ATT_SCALE).astype(BF16)
                dqc = jnp.dot(ds, kcat, preferred_element_type=F32)
                dqn = dqn + jnp.where(mask_n, dqc[:, 0:LANES], 0.0)
                dq1_ref[r0:stop, :] += jnp.where(mask_r, dqc[:, LANES:2 * LANES], 0.0)
                dq2_ref[r0:stop, :] += jnp.where(mask_r, dqc[:, 2 * LANES:3 * LANES], 0.0)
                dk_acc[0:stop, :] += lax.dot_general(ds, qcat, (((0,), (0,)), ((), ())),
                                                     preferred_element_type=F32)
                dv_acc[0:stop, :] += lax.dot_general(p.astype(BF16), domb, (((0,), (0,)), ((), ())),
                                                     preferred_element_type=F32)
            dqn_ref[r0:stop, :] = dqn
        dkn_ref[...] = dk_acc[:, 0:LANES]
        dk1_ref[...] += dk_acc[:, LANES:2 * LANES]
        dk2_ref[...] += dk_acc[:, 2 * LANES:3 * LANES]
        dv_ref[...] = dv_acc[...]

    per_pair = pl.BlockSpec((T, LANES), lambda p: (0, p))
    shared = pl.BlockSpec((T, LANES), lambda p: (0, 0))
    wide = jax.ShapeDtypeStruct((T, MLA_HEADS * QK_NOPE), F32)
    narrow = jax.ShapeDtypeStruct((T, LANES), F32)
    return pl.pallas_call(
        body, name=name + '_b', grid=(NPAIR,), in_specs=_att_in_specs(T) + [per_pair, per_pair],
        out_specs=[per_pair, shared, shared, per_pair, shared, shared, per_pair],
        out_shape=[wide, narrow, narrow, wide, narrow, narrow, wide],
        scratch_shapes=[pltpu.VMEM((T, 3 * LANES), F32), pltpu.VMEM((T, LANES), F32)],
        compiler_params=_cparams(('arbitrary',)),
    )(qn, q1, q2, kn, k1, k2, v, o, do)


@functools.partial(jax.custom_vjp, nondiff_argnums=(7,))
def attention(qn, q1, q2, kn, k1, k2, v, name):
    return _att_fwd_call(qn, q1, q2, kn, k1, k2, v, name)


def _attn_fwd(qn, q1, q2, kn, k1, k2, v, name):
    o = _att_fwd_call(qn, q1, q2, kn, k1, k2, v, name)
    return o, (qn, q1, q2, kn, k1, k2, v, o)


def _attn_bwd(name, res, do):
    return tuple(_att_bwd_call(*res, do, name))


attention.defvjp(_attn_fwd, _attn_bwd)


SCAN_CHUNK = 64
SCAN_UNROLL = 8


def _block_ones(n, seg):
    i = np.arange(n)
    return (i[:, None] // seg == i[None, :] // seg).astype(np.float32)


def _scan_diag():
    i = np.arange(RW_WIDTH)
    return jnp.asarray((np.arange(RW_N)[:, None] == (i[None, :] % RW_N)).astype(np.float32))


def _head_rowsum(x):
    low = lax.broadcasted_iota(jnp.int32, (1, LANES), 1) < RW_N
    tiles = []
    for j in range(RW_WIDTH // LANES):
        xt = x[:, j * LANES:(j + 1) * LANES]
        x0 = jnp.where(low, xt, 0.0)
        s0 = jnp.sum(x0, axis=-1, keepdims=True)
        s1 = jnp.sum(xt - x0, axis=-1, keepdims=True)
        tiles.append(jnp.where(low, s0, s1))
    return jnp.concatenate(tiles, axis=1)


def _unrolled_loop(n, step, init):
    def body(i, carry):
        for j in range(SCAN_UNROLL):
            carry = step(i * SCAN_UNROLL + j, carry)
        return carry
    return lax.fori_loop(0, n // SCAN_UNROLL, body, init)


def _scan_fwd_call(r, w, k, v, a, b, name):
    T = r.shape[0]
    tc = min(SCAN_CHUNK, T)
    dg = _scan_diag()

    def body(r_ref, w_ref, k_ref, v_ref, a_ref, b_ref, dg_ref, y_ref, st_ref, sa_ref, vc_ref, s_ref):
        @pl.when(pl.program_id(0) == 0)
        def _():
            s_ref[...] = jnp.zeros(s_ref.shape, F32)

        dgv = dg_ref[...]
        readout = lambda s, t: jnp.sum(_head_rowsum(s * r_ref[t]) * dgv, axis=0, keepdims=True)

        def step(t, carry):
            s, vcol = carry
            st_ref[t] = s
            vc_ref[t] = vcol
            sa = _head_rowsum(s * a_ref[t])
            sa_ref[t] = sa
            prev = jnp.maximum(t - 1, 0)
            y_ref[prev] = readout(s, prev)
            vcol_next = _head_rowsum(dgv * v_ref[jnp.minimum(t + 1, tc - 1)])
            sn = s * w_ref[t] + sa * b_ref[t] + vcol * k_ref[t]
            return sn, vcol_next

        s_end, _ = _unrolled_loop(tc, step, (s_ref[...], _head_rowsum(dgv * v_ref[0])))
        y_ref[tc - 1] = readout(s_end, tc - 1)
        s_ref[...] = s_end

    vec = pl.BlockSpec((tc, 1, RW_WIDTH), lambda i: (i, 0, 0))
    mat = pl.BlockSpec((tc, RW_N, RW_WIDTH), lambda i: (i, 0, 0))
    msh = jax.ShapeDtypeStruct((T, RW_N, RW_WIDTH), F32)
    return pl.pallas_call(
        body, name=name + '_f', grid=(T // tc,),
        in_specs=[vec] * 6 + [pl.BlockSpec((RW_N, RW_WIDTH), lambda i: (0, 0))],
        out_specs=[vec, mat, mat, mat],
        out_shape=[jax.ShapeDtypeStruct((T, 1, RW_WIDTH), F32), msh, msh, msh],
        scratch_shapes=[pltpu.VMEM((RW_N, RW_WIDTH), F32)],
        compiler_params=_cparams(('arbitrary',)),
    )(r, w, k, v, a, b, dg)


def _scan_bwd_call(r, w, k, a, b, st, sa_all, vc_all, dy, name):
    T = r.shape[0]
    tc = min(SCAN_CHUNK, T)
    nt = T // tc
    dg = _scan_diag()

    def body(r_ref, w_ref, k_ref, a_ref, b_ref, st_ref, sa_ref, vc_ref, dy_ref, dg_ref,
             dr_ref, dw_ref, dk_ref, dv_ref, da_ref, db_ref, ds_ref):
        @pl.when(pl.program_id(0) == 0)
        def _():
            ds_ref[...] = jnp.zeros(ds_ref.shape, F32)

        dgv = dg_ref[...]
        colsum = lambda x: jnp.sum(x, axis=0, keepdims=True)

        def step(i, carry):
            ds, dycol = carry
            t = tc - 1 - i
            sp = st_ref[t]
            rt, wt, kt, at, bt = r_ref[t], w_ref[t], k_ref[t], a_ref[t], b_ref[t]
            ds = ds + dycol * rt
            dsa = _head_rowsum(ds * bt)
            sa, vcol = sa_ref[t], vc_ref[t]
            dycol_next = _head_rowsum(dgv * dy_ref[jnp.maximum(t - 1, 0)])
            sn = sp * wt + sa * bt + vcol * kt
            dr_ref[t] = colsum(sn * dycol)
            dk_ref[t] = colsum(ds * vcol)
            db_ref[t] = colsum(ds * sa)
            dw_ref[t] = colsum(ds * sp)
            dv_ref[t] = colsum(_head_rowsum(ds * kt) * dgv)
            da_ref[t] = colsum(sp * dsa)
            return ds * wt + dsa * at, dycol_next

        ds_end, _ = _unrolled_loop(tc, step, (ds_ref[...], _head_rowsum(dgv * dy_ref[tc - 1])))
        ds_ref[...] = ds_end

    vec = pl.BlockSpec((tc, 1, RW_WIDTH), lambda i: (nt - 1 - i, 0, 0))
    mat = pl.BlockSpec((tc, RW_N, RW_WIDTH), lambda i: (nt - 1 - i, 0, 0))
    vsh = jax.ShapeDtypeStruct((T, 1, RW_WIDTH), F32)
    return pl.pallas_call(
        body, name=name + '_b', grid=(nt,),
        in_specs=[vec] * 5 + [mat] * 3 + [vec, pl.BlockSpec((RW_N, RW_WIDTH), lambda i: (0, 0))],
        out_specs=[vec] * 6, out_shape=[vsh] * 6,
        scratch_shapes=[pltpu.VMEM((RW_N, RW_WIDTH), F32)],
        compiler_params=_cparams(('arbitrary',)),
    )(r, w, k, a, b, st, sa_all, vc_all, dy, dg)


@functools.partial(jax.custom_vjp, nondiff_argnums=(6,))
def wkv7(r, w, k, v, a, b, name):
    return _scan_fwd_call(r, w, k, v, a, b, name)[0]


def _wkv7_fwd(r, w, k, v, a, b, name):
    y, st, sa_all, vc_all = _scan_fwd_call(r, w, k, v, a, b, name)
    return y, (r, w, k, a, b, st, sa_all, vc_all)


def _wkv7_bwd(name, res, dy):
    return tuple(_scan_bwd_call(*res, dy, name))


wkv7.defvjp(_wkv7_fwd, _wkv7_bwd)


def _np_bf16(a):
    return jnp.asarray(a, BF16)


def _mla_consts():
    seg_n = (np.arange(512)[:, None] // QK_NOPE == np.arange(LANES)[None, :]).astype(np.float32)
    seg_r = (np.arange(LANES)[:, None] // 16 == np.arange(LANES)[None, :]).astype(np.float32)
    e1 = np.zeros((LANES, LANES), np.float32)
    e2 = np.zeros((LANES, LANES), np.float32)
    for h in range(MLA_HEADS):
        for i in range(16):
            e1[i, h * 16 + i] = 1.0
            e2[16 + i, h * 16 + i] = 1.0
    mats = [seg_n, seg_n.T, seg_r, seg_r.T, e1, e1.T, e2, e2.T]
    return [_np_bf16(m) for m in mats]


def _qk_prep_fn(qn, q1, q2, kn, kx, cos, sin, gqn, gq1, gq2, gkn, gk1, gk2,
                seg_n, seg_nt, seg_r, seg_rt, e1, e1t, e2, e2t):
    def normrope(xn, x1, x2, gn, g1, g2):
        ss = sdot(xn * xn, seg_n, seg_nt) + sdot(x1 * x1, seg_r, seg_rt) + sdot(x2 * x2, seg_r, seg_rt)
        inv = lax.rsqrt(ss * (1.0 / QK_HEAD) + NORM_EPS)
        inv_n = sdot(inv, seg_nt, seg_n)
        inv_r = sdot(inv, seg_rt, seg_r)
        y1 = x1 * inv_r * g1
        y2 = x2 * inv_r * g2
        return xn * inv_n * gn, y1 * cos - y2 * sin, y1 * sin + y2 * cos

    k1 = sdot(kx, e1, e1t)
    k2 = sdot(kx, e2, e2t)
    return normrope(qn, q1, q2, gqn, gq1, gq2) + normrope(kn, k1, k2, gkn, gk1, gk2)


def _rwkv_prep_fn(vres):
    def fn(r, k, v, xg, xwa, kx, *rest):
        if vres:
            vfirst, w0, a0, k_k, k_a, w2p, a2p, g2, v0, v2p, bm = rest
        else:
            w0, a0, k_k, k_a, w2p, a2p, g2, bm = rest
        z = w0 + bdot(jnp.tanh(xwa), w2p)
        nz = -z
        softplus = jnp.maximum(nz, 0.0) + jnp.log(1.0 + jnp.exp(-jnp.abs(nz)))
        decay = jnp.exp(-jnp.exp(-softplus - 0.5))
        a = _sigmoid(a0 + bdot(xwa, a2p))
        g = bdot(_sigmoid(xg), g2)
        if vres:
            vv = v + (vfirst - v) * _sigmoid(v0 + bdot(kx, v2p))
        else:
            vv = v
        kkr = k * k_k
        kk = kkr / jnp.maximum(jnp.sqrt(sdot(kkr * kkr, bm, bm)), 1e-12)
        k2 = k * (1.0 + (a - 1.0) * k_a)
        return r * 1.0, decay, k2, vv, -kk, kk * a, g
    return fn


def _rwkv_post_fn(y, r, k2, vv, g, ln_w, ln_b, rk, bm):
    inv_n = 1.0 / RW_N
    mean = sdot(y, bm, bm) * inv_n
    yc = y - mean
    var = sdot(yc * yc, bm, bm) * inv_n
    yn = yc * lax.rsqrt(var + GN_EPS) * ln_w + ln_b
    bonus = sdot(r * k2 * rk, bm, bm) * vv
    return ((yn + bonus) * g,)


def _merge_fn(g0, g1, g2, oa, ob, oc):
    return (_sigmoid(g0) * oa + _sigmoid(g1) * ob + _sigmoid(g2) * oc,)


def _loss_call(y, target):
    T, C = y.shape
    tb = min(256, T)

    def body(y_ref, t_ref, dy_ref, part_ref):
        err = y_ref[...] - t_ref[...]
        dy_ref[...] = err * (1.0 / C)
        sq = jnp.sum(err * err, axis=0, keepdims=True)
        acc = sq[:, 0:LANES]
        for j in range(1, C // LANES):
            acc = acc + sq[:, j * LANES:(j + 1) * LANES]
        part_ref[...] = jnp.zeros(part_ref.shape, F32)
        part_ref[0:1, :] = acc * (0.5 / C)

    return pl.pallas_call(
        body, name='loss', grid=(T // tb,),
        in_specs=[pl.BlockSpec((tb, C), lambda i: (i, 0))] * 2,
        out_specs=[pl.BlockSpec((tb, C), lambda i: (i, 0)), pl.BlockSpec((8, LANES), lambda i: (i, 0))],
        out_shape=[jax.ShapeDtypeStruct((T, C), F32), jax.ShapeDtypeStruct((8 * (T // tb), LANES), F32)],
        compiler_params=_cparams(('parallel',)),
    )(y, target)


def _pad_rows(t, before, total):
    return jnp.pad(t, ((before, total - before - t.shape[0]), (0, 0)))


def _head_tile(g, lo, hi):
    return jnp.tile(g[lo:hi], MLA_HEADS).reshape(1, -1)


def _layer(l, x, v_first, wd, sp, cos, sin):
    T = x.shape[0]
    nm = f'l{l}'
    vres = l > 0
    w_in = wd['w_in']
    if vres:
        v1t = wd['rwkv_v1'].map(lambda t: t.T)
    else:
        v1t = W(jnp.zeros((MV_LORA, D_MODEL), BF16), jnp.zeros((MV_LORA, D_MODEL), BF16))
    zpad = W(jnp.zeros((64, D_MODEL), BF16), jnp.zeros((64, D_MODEL), BF16))
    w_rw = _wcat([w_in['rkv'], w_in['xg'], w_in['xwa'], w_in['kpe'], v1t, zpad], 0)
    gates, cq, ckv, rw, cv = norm_projections(
        x, sp['attn_norm'], [w_in['gates'], w_in['cq'], w_in['ckv'], w_rw, w_in['conv']], nm + '_win', transposed=True)

    v_mu = sp['rwkv_v_mu'] if vres else jnp.zeros((MV_LORA,), F32)
    mu_all = jnp.concatenate([sp['rwkv_mu'][0:768], sp['rwkv_mu'][896:1024], sp['rwkv_mu'][768:896],
                              jnp.zeros((QK_ROPE,), F32), v_mu, jnp.zeros((64,), F32)]).reshape(1, -1)
    rws = token_shift_mix(rw, mu_all, nm + '_shift')

    wq = wd['mla_wq_b'].map(lambda t: jnp.concatenate(
        [t.reshape(Q_LORA, MLA_HEADS, QK_HEAD)[:, :, 0:64].reshape(Q_LORA, 512),
         t.reshape(Q_LORA, MLA_HEADS, QK_HEAD)[:, :, 64:80].reshape(Q_LORA, 128),
         t.reshape(Q_LORA, MLA_HEADS, QK_HEAD)[:, :, 80:96].reshape(Q_LORA, 128)], axis=1))
    wkn = wd['mla_wkv_b'].map(lambda t: t.reshape(KV_LORA, MLA_HEADS, 128)[:, :, 0:64].reshape(KV_LORA, 512))
    wv = wd['mla_wkv_b'].map(lambda t: t.reshape(KV_LORA, MLA_HEADS, 128)[:, :, 64:128].reshape(KV_LORA, 512))
    q, = norm_projections(cq, sp['mla_q_a_norm'], [wq], nm + '_wq')
    kn, vv_att = norm_projections(ckv, sp['mla_kv_a_norm'], [wkn, wv], nm + '_wkv')
    gq, gk = sp['mla_q_norm'], sp['mla_k_norm']
    consts = _mla_consts()
    qk_specs = ([ROW(pieces=((0, 512), (512, 128), (640, 128))), ROW(), ROW(pieces=((1024, 128),)),
                 ROW(False), ROW(False)] + [FULL()] * 6 + [FULL(False)] * 8)
    qk_op = stage_op(_qk_prep_fn, qk_specs, [512, 128, 128, 512, 128, 128], nm + '_qkprep')
    Qn, Q1, Q2, Kn, K1, K2 = qk_op(q, kn, rws, cos, sin,
                                   _head_tile(gq, 0, 64), _head_tile(gq, 64, 80), _head_tile(gq, 80, 96),
                                   _head_tile(gk, 0, 64), _head_tile(gk, 64, 80), _head_tile(gk, 80, 96), *consts)
    o_att = attention(Qn, Q1, Q2, Kn, K1, K2, vv_att, nm + '_att')
    o_a = linear(o_att, wd['mla_w_o'], name=nm + '_wo')

    bm = _np_bf16(_block_ones(RW_WIDTH, RW_N))
    vec = lambda n: sp[n].reshape(1, -1)
    f32w = lambda n: wd[n].c + wd[n].b.astype(F32)
    w2p = _pad_rows(f32w('rwkv_w2'), 0, 128)
    a2p = _pad_rows(f32w('rwkv_a2'), 64, 128)
    g2 = f32w('rwkv_g2')
    rw_pieces = ((0, 256), (256, 256), (512, 256), (768, 128), (896, 128), (1024, 128))
    if vres:
        v2p = _pad_rows(f32w('rwkv_v2'), 32, 128)
        prep_specs = [ROW(pieces=rw_pieces), ROW()] + [FULL()] * 9 + [FULL(False)]
        prep_in = [rws, v_first, vec('rwkv_w0'), vec('rwkv_a0'), vec('rwkv_k_k'), vec('rwkv_k_a'), w2p, a2p, g2,
                   vec('rwkv_v0'), v2p, bm]
    else:
        prep_specs = [ROW(pieces=rw_pieces)] + [FULL()] * 7 + [FULL(False)]
        prep_in = [rws, vec('rwkv_w0'), vec('rwkv_a0'), vec('rwkv_k_k'), vec('rwkv_k_a'), w2p, a2p, g2, bm]
    prep_op = stage_op(_rwkv_prep_fn(vres), prep_specs, [256] * 7, nm + '_rwprep')
    r_, dec, k2, vv, an, bn, g = prep_op(*prep_in)
    if not vres:
        v_first = vv
    t3 = lambda t: t.reshape(T, 1, RW_WIDTH)
    y = wkv7(t3(r_), t3(dec), t3(k2), t3(vv), t3(an), t3(bn), nm + '_scan').reshape(T, RW_WIDTH)
    post_op = stage_op(_rwkv_post_fn, [ROW()] * 5 + [FULL()] * 3 + [FULL(False)], [256], nm + '_rwpost')
    yb = post_op(y, r_, k2, vv, g, vec('rwkv_ln_w'), vec('rwkv_ln_b'), sp['rwkv_r_k'].reshape(1, -1), bm)[0]
    o_b = linear(yb, wd['rwkv_w_o'], name=nm + '_rwo')

    oc_in = short_conv(cv, f32w('conv_w'), nm + '_conv')
    o_c = linear(oc_in, wd['conv_w_o'], name=nm + '_cwo')

    merge_op = stage_op(_merge_fn, [ROW(pieces=((0, 1024), (1024, 1024), (2048, 1024))), ROW(), ROW(), ROW()],
                        [D_MODEL], nm + '_merge')
    merged = merge_op(gates, o_a, o_b, o_c)[0]
    x2 = linear(merged, wd['w_out'], add=x, name=nm + '_wout')
    return x2, v_first


def _mlp(l, x, wd, sp):
    return mlp_block(x, sp['mlp_norm'].reshape(1, -1), wd['w_up'].b, wd['w_up'].c, wd['w_down'].b, wd['w_down'].c,
                     f'l{l}_mlp')


MLP_WEIGHTS = ('w_up', 'w_down')


def _entries(ml):
    out = []
    for name, (layers, shape, axis) in SHARDED.items():
        l = ml if layers == DEPTH else ml - 1
        if not 0 <= l < layers or name in MLP_WEIGHTS:
            continue
        n = shape[0] * shape[1] // N_DEV
        if name == 'conv_w':
            out.append(('conv_w_hi', name, l, n))
            out.append(('conv_w_lo', name, l, n))
        else:
            out.append((name, name, l, n))
    return out


def _slot_size(n):
    return -(-n // LANES) * LANES


def _pack_rows(ml):
    total = sum(_slot_size(n) for _, _, _, n in _entries(ml))
    rows = -(-total // LANES)
    return -(-rows // PACK_ROW_MULT) * PACK_ROW_MULT


def _pack_flat(pieces, ml):
    rows = _pack_rows(ml)
    padded = []
    for p, (_, _, _, n) in zip(pieces, _entries(ml)):
        pad = _slot_size(n) - n
        if pad:
            p = jnp.pad(p, [(0, 0)] * (p.ndim - 1) + [(0, pad)])
        padded.append(p)
    flat = jnp.concatenate(padded, axis=-1)
    tail = rows * LANES - flat.shape[-1]
    if tail:
        flat = jnp.pad(flat, [(0, 0)] * (flat.ndim - 1) + [(0, tail)])
    return flat.reshape(flat.shape[:-1] + (rows, LANES))


def _unpack_flat(buf, ml):
    out, row = [], 0
    for _, _, _, n in _entries(ml):
        nrows = _slot_size(n) // LANES
        piece = buf[..., row:row + nrows, :].reshape(buf.shape[:-2] + (-1,))
        out.append(piece[..., :n])
        row += nrows
    return out


def _pack_shards(shards, ml, dtype, split_conv):
    pieces = []
    for slot, name, l, n in _entries(ml):
        a = shards[name][l]
        if name in TRANSPOSED:
            a = a.astype(dtype).T
        a = a.reshape(-1)
        if slot == 'conv_w_hi':
            a = a.astype(BF16).astype(F32) if split_conv else a
        elif slot == 'conv_w_lo':
            a = (a - a.astype(BF16).astype(F32)) if split_conv else jnp.zeros_like(a)
        pieces.append(a.astype(dtype))
    return _pack_flat(pieces, ml)


def _unpack_shards(buf, ml):
    out = {}
    for (slot, name, l, n), v in zip(_entries(ml), _unpack_flat(buf, ml)):
        if slot == 'conv_w_lo':
            continue
        layers, shape, axis = SHARDED[name]
        sshape = (shape[0] // N_DEV, shape[1]) if axis == 0 else (shape[0], shape[1] // N_DEV)
        out[(name, l)] = v.reshape(sshape[::-1]).T if name in TRANSPOSED else v.reshape(sshape)
    return out


def _to_full(blocks, shape, axis):
    if axis == 0:
        return blocks.reshape(shape)
    return blocks.reshape(N_DEV, shape[0], shape[1] // N_DEV).transpose(1, 0, 2).reshape(shape)


def _to_blocks(full, axis):
    r, c = full.shape
    if axis == 0:
        return full.reshape(N_DEV, -1)
    return full.reshape(r, N_DEV, c // N_DEV).transpose(1, 0, 2).reshape(N_DEV, -1)


def _unpack_gathered(gathered, ml):
    out, conv_hi = {}, None
    for (slot, name, l, n), v in zip(_entries(ml), _unpack_flat(gathered, ml)):
        layers, shape, axis = SHARDED[name]
        if name in TRANSPOSED:
            out[name] = v.reshape(-1, shape[0])
            continue
        full = _to_full(v, shape, axis)
        if slot == 'conv_w_hi':
            conv_hi = full
        elif slot == 'conv_w_lo':
            out[name] = conv_hi.astype(F32) + full.astype(F32)
        else:
            out[name] = full
    return out


def _pack_grads(grads, ml):
    pieces = []
    for slot, name, l, n in _entries(ml):
        if name in TRANSPOSED:
            blocks = jnp.concatenate(grads[name], axis=0).reshape(N_DEV, -1)
        else:
            blocks = _to_blocks(grads[name], SHARDED[name][2])
        if slot == 'conv_w_lo':
            blocks = jnp.zeros_like(blocks)
        pieces.append(blocks.astype(BF16))
    return _pack_flat(pieces, ml)


def _my_pos():
    return lax.axis_index('x'), lax.axis_index('y'), lax.axis_index('c')


def _flip(v, bit):
    return 1 - v if bit else v


def all_gather_blocks(x):
    rows = x.shape[0]

    def body(x_ref, out_ref, send_sems, recv_sems, local_sem):
        mx, my, mc = _my_pos()
        me, sibling = (mx, my, mc), (mx, my, 1 - mc)
        chips = [(1 - mx, my), (mx, 1 - my), (1 - mx, 1 - my)]

        def block(px, py, pc):
            return out_ref.at[4 * px + 2 * py + pc]

        def copy(k, blk, to, src=None):
            return pltpu.make_async_remote_copy(
                src_ref=block(*blk) if src is None else src, dst_ref=block(*blk),
                send_sem=send_sems.at[k], recv_sem=recv_sems.at[k], device_id=to, device_id_type=MESH)

        mine = pltpu.make_async_copy(x_ref, block(*me), local_sem)
        mine.start()
        first = [copy(0, me, sibling, src=x_ref)]
        first += [copy(1 + j, me, (*chip, mc), src=x_ref) for j, chip in enumerate(chips)]
        for cp in first:
            cp.start()
        passed = [copy(4 + j, (*chip, mc), sibling) for j, chip in enumerate(chips)]
        for j, chip in enumerate(chips):
            copy(1 + j, (*chip, mc), me).wait_recv()
            passed[j].start()
        copy(0, sibling, me).wait_recv()
        for j, chip in enumerate(chips):
            copy(4 + j, (*chip, 1 - mc), me).wait_recv()
        for cp in first + passed:
            cp.wait_send()
        mine.wait()

    return pl.pallas_call(
        body, name='all_gather_weights',
        out_shape=jax.ShapeDtypeStruct((N_DEV, rows, LANES), x.dtype),
        in_specs=[pl.BlockSpec(memory_space=pl.ANY)], out_specs=pl.BlockSpec(memory_space=pl.ANY),
        scratch_shapes=[pltpu.SemaphoreType.DMA((7,)), pltpu.SemaphoreType.DMA((7,)), pltpu.SemaphoreType.DMA],
    )(x)


HBM_SPEC = pl.BlockSpec(memory_space=pltpu.HBM)
SEM_SPEC = pl.BlockSpec(memory_space=pltpu.SEMAPHORE)
DATAFLOW_EFFECT = pltpu.SideEffectType.DATAFLOW_SIDE_EFFECTING


def _direct_copies(src_ref, land_ref, send_sems, recv_sems, per_peer):
    mx, my, mc = _my_pos()
    me = 4 * mx + 2 * my + mc
    copies = []
    for k in range(1, N_DEV):
        peer = (_flip(mx, k & 4), _flip(my, k & 2), _flip(mc, k & 1))
        pidx = 4 * peer[0] + 2 * peer[1] + peer[2]
        copies.append(pltpu.make_async_remote_copy(
            src_ref=src_ref.at[pidx] if per_peer else src_ref, dst_ref=land_ref.at[me],
            send_sem=send_sems.at[k - 1], recv_sem=recv_sems.at[k - 1], device_id=peer, device_id_type=MESH))
    return copies


def send_start(src, per_peer, name):
    block = src.shape[1:] if per_peer else src.shape
    land_shape = (N_DEV,) + tuple(block)

    def body(src_ref, land_ref, send_sems, recv_sems, src_thru, land_thru, token):
        for cp in _direct_copies(src_ref, land_ref, send_sems, recv_sems, per_peer):
            cp.start()
        token[...] = jnp.zeros(token.shape, F32)

    send_sems, recv_sems, src_thru, land_thru, token = pl.pallas_call(
        body, name=name,
        out_shape=(pltpu.SemaphoreType.DMA((N_DEV - 1,)), pltpu.SemaphoreType.DMA((N_DEV - 1,)),
                   pltpu.HBM(src.shape, src.dtype), pltpu.HBM(land_shape, src.dtype),
                   jax.ShapeDtypeStruct((8, LANES), F32)),
        in_specs=(HBM_SPEC, HBM_SPEC),
        out_specs=(SEM_SPEC, SEM_SPEC, HBM_SPEC, HBM_SPEC, pl.BlockSpec(memory_space=pltpu.VMEM)),
        input_output_aliases={0: 2, 1: 3},
        compiler_params=pltpu.CompilerParams(has_side_effects=DATAFLOW_EFFECT),
    )(pltpu.with_memory_space_constraint(src, pltpu.HBM),
      pltpu.with_memory_space_constraint(lax.empty(land_shape, src.dtype), pltpu.HBM))
    return (send_sems, recv_sems, src_thru, land_thru), token[0, 0]


def send_wait(handles, after, per_peer, name):
    send_sems, recv_sems, src_thru, land_thru = handles

    def body(src_ref, land_ref, send_sems, recv_sems, after_ref, src_dead, got_ref):
        for cp in _direct_copies(src_ref, land_ref, send_sems, recv_sems, per_peer):
            cp.wait_send()
            cp.wait_recv()

    return pl.pallas_call(
        body, name=name,
        out_shape=(pltpu.HBM(src_thru.shape, src_thru.dtype), pltpu.HBM(land_thru.shape, land_thru.dtype)),
        in_specs=(HBM_SPEC, HBM_SPEC, SEM_SPEC, SEM_SPEC, pl.BlockSpec(memory_space=pl.ANY)),
        out_specs=(HBM_SPEC, HBM_SPEC), input_output_aliases={0: 0, 1: 1},
        compiler_params=pltpu.CompilerParams(has_side_effects=DATAFLOW_EFFECT),
    )(src_thru, land_thru, send_sems, recv_sems, after)[1]


def _adamw_math(w, g, m, v):
    m2 = ADAM_B1 * m + (1.0 - ADAM_B1) * g
    v2 = ADAM_B2 * v + (1.0 - ADAM_B2) * (g * g)
    m_hat = m2 / (1.0 - ADAM_B1 ** ADAM_STEP)
    v_hat = v2 / (1.0 - ADAM_B2 ** ADAM_STEP)
    delta = -ADAM_LR * (m_hat / (jnp.sqrt(v_hat) + ADAM_EPS) + ADAM_WD * w)
    return delta, m2, v2


def sum_parts(parts, name):
    _, rows, cols = parts.shape
    rb = rows
    while N_DEV * rb * cols * 2 > (2 << 20) and rb % 32 == 0:
        rb //= 2

    def body(p_ref, g_ref):
        g = p_ref[0].astype(F32)
        for j in range(1, N_DEV):
            g = g + p_ref[j].astype(F32)
        g_ref[...] = g

    return pl.pallas_call(
        body, name=name, grid=(rows // rb,),
        in_specs=[pl.BlockSpec((N_DEV, rb, cols), lambda i: (0, i, 0))],
        out_specs=pl.BlockSpec((rb, cols), lambda i: (i, 0)),
        out_shape=jax.ShapeDtypeStruct((rows, cols), F32), compiler_params=_cparams(('parallel',)),
    )(parts)


ADAMW_BLOCK_BYTES = 1 << 20


def adamw_weight(name, w, m, v, grads):
    layers, a, b = w.shape
    ra = a
    while ra * b * 4 > ADAMW_BLOCK_BYTES and ra % 16 == 0:
        ra //= 2

    def body(*refs):
        w_ref, m_ref, v_ref = refs[:3]
        g_refs = refs[3:3 + layers]
        g_ref, d_ref, m2_ref, v2_ref = refs[3 + layers:]
        g = g_refs[0][...]
        for l in range(1, layers):
            g = jnp.where(pl.program_id(0) == l, g_refs[l][...], g)
        delta, m2, v2 = _adamw_math(w_ref[0], g, m_ref[0], v_ref[0])
        g_ref[0] = g
        d_ref[0] = delta
        m2_ref[0] = m2
        v2_ref[0] = v2

    blk = pl.BlockSpec((1, ra, b), lambda l, i: (l, i, 0))
    gblk = pl.BlockSpec((ra, b), lambda l, i: (i, 0))
    sh = jax.ShapeDtypeStruct(w.shape, F32)
    return pl.pallas_call(
        body, name='adamw_' + name, grid=(layers, a // ra), in_specs=[blk] * 3 + [gblk] * layers,
        out_specs=[blk] * 4, out_shape=[sh] * 4, compiler_params=_cparams(('parallel', 'parallel')),
    )(w, m, v, *grads)


def allreduce_adamw_small(g, w, m, v):
    rows = g.shape[0]

    def body(g_ref, w_ref, m_ref, v_ref, gs_ref, d_ref, m2_ref, v2_ref, all_ref, send_sems, recv_sems):
        mx, my, mc = _my_pos()
        me, sibling = (mx, my, mc), (mx, my, 1 - mc)
        chips = [(1 - mx, my), (mx, 1 - my), (1 - mx, 1 - my)]

        def block(px, py, pc):
            return all_ref.at[4 * px + 2 * py + pc]

        def copy(k, blk, to, src=None):
            return pltpu.make_async_remote_copy(
                src_ref=block(*blk) if src is None else src, dst_ref=block(*blk),
                send_sem=send_sems.at[k], recv_sem=recv_sems.at[k], device_id=to, device_id_type=MESH)

        first = [copy(0, me, sibling, src=g_ref)]
        first += [copy(1 + j, me, (*chip, mc), src=g_ref) for j, chip in enumerate(chips)]
        for cp in first:
            cp.start()
        passed = [copy(4 + j, (*chip, mc), sibling) for j, chip in enumerate(chips)]
        for j, chip in enumerate(chips):
            copy(1 + j, (*chip, mc), me).wait_recv()
            passed[j].start()
        copy(0, sibling, me).wait_recv()
        for j, chip in enumerate(chips):
            copy(4 + j, (*chip, 1 - mc), me).wait_recv()
        for cp in first + passed:
            cp.wait_send()
        my_idx = 4 * mx + 2 * my + mc
        total = jnp.zeros((rows, LANES), F32)
        for j in range(N_DEV):
            total = total + jnp.where(my_idx == j, g_ref[...], all_ref[j])
        delta, m2, v2 = _adamw_math(w_ref[...], total, m_ref[...], v_ref[...])
        gs_ref[...] = total
        d_ref[...] = delta
        m2_ref[...] = m2
        v2_ref[...] = v2

    vm = pl.BlockSpec(memory_space=pltpu.VMEM)
    sh = jax.ShapeDtypeStruct((rows, LANES), F32)
    return pl.pallas_call(
        body, name='allreduce_adamw_small', in_specs=[vm] * 4, out_specs=[vm] * 4, out_shape=[sh] * 4,
        scratch_shapes=[pltpu.VMEM((N_DEV, rows, LANES), F32), pltpu.SemaphoreType.DMA((7,)),
                        pltpu.SemaphoreType.DMA((7,))],
    )(g, w, m, v)


SMALL_COUNT = 11680


def _small_pack(d, extra=None):
    assert sum(d[n].size for n in SMALL_NAMES) == SMALL_COUNT
    flat = jnp.concatenate([d[n].reshape(-1) for n in SMALL_NAMES] + ([] if extra is None else [extra.reshape(1)]))
    rows = -(-(SMALL_COUNT + 1) // (8 * LANES)) * 8
    return jnp.pad(flat, (0, rows * LANES - flat.shape[0])).reshape(rows, LANES)


def _small_unpack(buf, like):
    flat = buf.reshape(-1)
    out, off = {}, 0
    for n in SMALL_NAMES:
        sz = int(np.prod(like[n].shape))
        out[n] = flat[off:off + sz].reshape(like[n].shape)
        off += sz
    return out


def _rope_tables(positions):
    freqs = ROPE_THETA ** (-(jnp.arange(QK_ROPE // 2, dtype=F32) * 2.0 / QK_ROPE))
    ang = positions.astype(F32)[:, None] * freqs
    return jnp.tile(jnp.cos(ang), (1, MLA_HEADS)), jnp.tile(jnp.sin(ang), (1, MLA_HEADS))


def _layer_weights(gathered, carriers):
    wd = {}
    for name, full in gathered.items():
        if name == 'w_in':
            wd[name] = {seg: W(full[lo:hi], c) for (seg, lo, hi), c in zip(WIN_SEGS, carriers[name])}
        else:
            wd[name] = W(full, carriers[name])
    return wd


F32_GRAD_WEIGHTS = ('rwkv_w2', 'rwkv_a2', 'rwkv_g2', 'rwkv_v2', 'conv_w')


def _make_carriers(gathered):
    gathered, carriers = dict(gathered), {}
    for name, full in gathered.items():
        if name == 'w_in':
            carriers[name] = tuple(jnp.zeros((hi - lo, D_MODEL), BF16) for _, lo, hi in WIN_SEGS)
        elif name == 'conv_w':
            gathered[name] = full.astype(BF16)
            carriers[name] = full - full.astype(BF16).astype(F32)
        else:
            carriers[name] = jnp.zeros(full.shape, F32 if name in F32_GRAD_WEIGHTS else BF16)
    return gathered, carriers


def _layer_small(small, ml):
    out = {}
    for n in SMALL_NAMES:
        l = ml if small[n].shape[0] == DEPTH else ml - 1
        if 0 <= l < small[n].shape[0]:
            out[n] = small[n][l]
    return out


def _after(value, tokens):
    return value + sum(tokens[1:], tokens[0])


def _train_step(x, positions, loss_target, weights, moms_m, moms_v):
    shards = {n: weights[n] for n in SHARDED}
    small = {n: weights[n] for n in SMALL_NAMES}
    me = 4 * lax.axis_index('x') + 2 * lax.axis_index('y') + lax.axis_index('c')
    cos, sin = _rope_tables(positions[0])

    units = [(kind, l) for l in range(DEPTH) for kind in ('pack',) + MLP_WEIGHTS]
    first, later = units[0], units[1:]
    tag = lambda u: f'{u[0]}_l{u[1]}'

    def own_block(u):
        kind, l = u
        return _pack_shards(shards, l, BF16, True) if kind == 'pack' else shards[kind][l].astype(BF16)

    own = {u: own_block(u) for u in units}
    held = lax.optimization_barrier((all_gather_blocks(own[first]), *[own[u] for u in later]))
    blocks, gathers, tokens = {first: held[0]}, {}, []
    for u, mine in zip(later, held[1:]):
        own[u] = mine
        gathers[u], token = send_start(mine, False, f'gather_{tag(u)}_start')
        tokens.append(token)
    x0 = _after(x[0], tokens)

    def gathered_block(u, after):
        if u not in blocks:
            landed = send_wait(gathers[u], after, False, f'gather_{tag(u)}_wait')
            blocks[u] = lax.dynamic_update_slice(landed, own[u][None], (me,) + (0,) * own[u].ndim)
        return blocks[u]

    def mixer_weights(l, after):
        return _make_carriers(_unpack_gathered(gathered_block(('pack', l), after), l))

    def mlp_weights(l, after):
        up = gathered_block(('w_up', l), after)
        down = gathered_block(('w_down', l), after).reshape(D_FF, D_MODEL)
        return _make_carriers({'w_up': up, 'w_down': down})

    def small_of(layer, part):
        sp = _layer_small(small, layer)
        return {n: v for n, v in sp.items() if (n == 'mlp_norm') == (part == 'mlp')}

    def mixer(layer, gathered):
        if layer == 0:
            return lambda c, s, xx: _layer(0, xx, None, _layer_weights(gathered, c), s, cos, sin)
        return lambda c, s, xx, vf: _layer(layer, xx, vf, _layer_weights(gathered, c), s, cos, sin)[0]

    def mlp(layer, gathered):
        return lambda c, s, xx: _mlp(layer, xx, _layer_weights(gathered, c), s)

    gathered, carriers = mixer_weights(0, None)
    (h, v_first), vjp_mix0 = jax.vjp(mixer(0, gathered), carriers, small_of(0, 'mix'), x0)
    gathered, carriers = mlp_weights(0, h)
    h, vjp_mlp0 = jax.vjp(mlp(0, gathered), carriers, small_of(0, 'mlp'), h)
    gathered, carriers = mixer_weights(1, h)
    h, vjp_mix1 = jax.vjp(mixer(1, gathered), carriers, small_of(1, 'mix'), h, v_first)
    gathered, carriers = mlp_weights(1, h)
    y, vjp_mlp1 = jax.vjp(mlp(1, gathered), carriers, small_of(1, 'mlp'), h)

    dy, loss_parts = _loss_call(y, loss_target[0])

    shipped = {}

    def ship(u, to_send):
        handles, token = send_start(to_send, True, f'grads_{tag(u)}_start')
        shipped[u] = (handles, to_send)
        return token

    def ship_mlp(l, gw):
        down = gw['w_down'].reshape(N_DEV, D_FF // N_DEV, D_MODEL)
        return [ship(('w_up', l), gw['w_up']), ship(('w_down', l), down)]

    gw, gs_mlp1, d = vjp_mlp1(dy)
    d = _after(d, ship_mlp(1, gw))
    gw, gs_mix1, d, dvf = vjp_mix1(d)
    d = _after(d, [ship(('pack', 1), _pack_grads(gw, 1))])
    gw, gs_mlp0, d = vjp_mlp0(d)
    d = _after(d, ship_mlp(0, gw))
    gw, gs_mix0, g_x = vjp_mix0((d, dvf))
    last_token = ship(first, _pack_grads(gw, 0))
    gs0, gs1 = {**gs_mix0, **gs_mlp0}, {**gs_mix1, **gs_mlp1}

    def arrived(u, after):
        handles, sent = shipped[u]
        landed = send_wait(handles, after, True, f'grads_{tag(u)}_wait')
        mine = lax.dynamic_slice_in_dim(sent, me, 1, axis=0)
        landed = lax.dynamic_update_slice(landed, mine, (me,) + (0,) * (sent.ndim - 1))
        return sum_parts(landed, f'sum_grads_{tag(u)}')

    after = _after(g_x, [last_token])
    grads = {u: arrived(u, after) for u in later if u[0] != 'pack'}
    grads.update(_unpack_shards(arrived(('pack', 1), after), 1))
    sharded_out = [{}, {}, {}, {}]

    def update(n):
        outs = adamw_weight(n, weights[n], moms_m[n], moms_v[n], [grads[(n, l)] for l in range(SHARDED[n][0])])
        for i in range(4):
            sharded_out[i][n] = outs[i]

    for n in MLP_WEIGHTS:
        update(n)

    g_small = {}
    for n in SMALL_NAMES:
        per = [g[n] for g in (gs0, gs1) if n in g]
        g_small[n] = jnp.stack(per)
    small_grads = _after(_small_pack(g_small, jnp.sum(loss_parts)), [sharded_out[1][MLP_WEIGHTS[-1]][0, 0, 0] * 0.0])
    small_bufs = allreduce_adamw_small(
        small_grads, _small_pack(small), _small_pack({n: moms_m[n] for n in SMALL_NAMES}),
        _small_pack({n: moms_v[n] for n in SMALL_NAMES}))
    loss = small_bufs[0].reshape(-1)[SMALL_COUNT]
    small_out = [_small_unpack(b, small) for b in small_bufs]

    grads.update(_unpack_shards(arrived(first, small_bufs[0]), 0))
    for n in SHARDED:
        if n not in MLP_WEIGHTS:
            update(n)

    pick = lambda i: [sharded_out[i][n] if n in SHARDED else small_out[i][n] for n in WEIGHT_NAMES]
    return (loss, g_x[None], *pick(0), *pick(1), *pick(2), *pick(3))


def kernel(x, positions, attn_norm, w_in, mla_q_a_norm, mla_wq_b, mla_kv_a_norm, mla_wkv_b, mla_q_norm, mla_k_norm, mla_w_o, rwkv_mu, rwkv_w0, rwkv_w2, rwkv_a0, rwkv_a2, rwkv_g2, rwkv_k_k, rwkv_k_a, rwkv_r_k, rwkv_ln_w, rwkv_ln_b, rwkv_w_o, rwkv_v1, rwkv_v_mu, rwkv_v0, rwkv_v2, conv_w, conv_w_o, w_out, mlp_norm, w_up, w_down, loss_target, m_attn_norm, m_w_in, m_mla_q_a_norm, m_mla_wq_b, m_mla_kv_a_norm, m_mla_wkv_b, m_mla_q_norm, m_mla_k_norm, m_mla_w_o, m_rwkv_mu, m_rwkv_w0, m_rwkv_w2, m_rwkv_a0, m_rwkv_a2, m_rwkv_g2, m_rwkv_k_k, m_rwkv_k_a, m_rwkv_r_k, m_rwkv_ln_w, m_rwkv_ln_b, m_rwkv_w_o, m_rwkv_v1, m_rwkv_v_mu, m_rwkv_v0, m_rwkv_v2, m_conv_w, m_conv_w_o, m_w_out, m_mlp_norm, m_w_up, m_w_down, v_attn_norm, v_w_in, v_mla_q_a_norm, v_mla_wq_b, v_mla_kv_a_norm, v_mla_wkv_b, v_mla_q_norm, v_mla_k_norm, v_mla_w_o, v_rwkv_mu, v_rwkv_w0, v_rwkv_w2, v_rwkv_a0, v_rwkv_a2, v_rwkv_g2, v_rwkv_k_k, v_rwkv_k_a, v_rwkv_r_k, v_rwkv_ln_w, v_rwkv_ln_b, v_rwkv_w_o, v_rwkv_v1, v_rwkv_v_mu, v_rwkv_v0, v_rwkv_v2, v_conv_w, v_conv_w_o, v_w_out, v_mlp_norm, v_w_up, v_w_down):
    args = locals()
    weights = {n: args[n] for n in WEIGHT_NAMES}
    moms_m = {n: args['m_' + n] for n in WEIGHT_NAMES}
    moms_v = {n: args['v_' + n] for n in WEIGHT_NAMES}
    return _train_step(x, positions, loss_target, weights, moms_m, moms_v)
```

```python
import functools

import numpy as np
import jax
import jax.numpy as jnp
from jax import lax
from jax.experimental import pallas as pl
from jax.experimental.pallas import tpu as pltpu

F32 = jnp.float32
BF16 = jnp.bfloat16

N_DEV = 8
LANES = 128
D_MODEL = 1024
DEPTH = 2
MLA_HEADS = 8
QK_NOPE = 64
QK_ROPE = 32
QK_HEAD = QK_NOPE + QK_ROPE
V_HEAD = 64
Q_LORA = 384
KV_LORA = 256
ROPE_THETA = 10000.0
RW_HEADS = 4
RW_N = 64
RW_WIDTH = RW_HEADS * RW_N
MV_LORA = 32
GN_EPS = 64e-5
CONV_WIDTH = 256
D_FF = 4 * D_MODEL
NORM_EPS = 1e-6
ADAM_LR = 0.001
ADAM_B1 = 0.9
ADAM_B2 = 0.999
ADAM_EPS = 1e-08
ADAM_WD = 0.01
ADAM_STEP = 10

VMEM_LIMIT = 56 * 1024 * 1024
MESH = pl.DeviceIdType.MESH

WEIGHT_NAMES = ['attn_norm', 'w_in', 'mla_q_a_norm', 'mla_wq_b', 'mla_kv_a_norm', 'mla_wkv_b', 'mla_q_norm',
                'mla_k_norm', 'mla_w_o', 'rwkv_mu', 'rwkv_w0', 'rwkv_w2', 'rwkv_a0', 'rwkv_a2', 'rwkv_g2',
                'rwkv_k_k', 'rwkv_k_a', 'rwkv_r_k', 'rwkv_ln_w', 'rwkv_ln_b', 'rwkv_w_o', 'rwkv_v1',
                'rwkv_v_mu', 'rwkv_v0', 'rwkv_v2', 'conv_w', 'conv_w_o', 'w_out', 'mlp_norm', 'w_up', 'w_down']

SHARDED = {
    'w_in': (2, (1024, 5536), 1), 'mla_wq_b': (2, (384, 768), 1), 'mla_wkv_b': (2, (256, 1024), 1),
    'mla_w_o': (2, (512, 1024), 1), 'rwkv_w2': (2, (64, 256), 1), 'rwkv_a2': (2, (64, 256), 1),
    'rwkv_g2': (2, (128, 256), 1), 'rwkv_w_o': (2, (256, 1024), 1), 'conv_w': (2, (3, 256), 1),
    'conv_w_o': (2, (256, 1024), 1), 'w_out': (2, (1024, 1024), 0), 'w_up': (2, (1024, 4096), 1),
    'w_down': (2, (4096, 1024), 0), 'rwkv_v1': (1, (1024, 32), 0), 'rwkv_v2': (1, (32, 256), 1),
}
SMALL_NAMES = [n for n in WEIGHT_NAMES if n not in SHARDED]
TRANSPOSED = ('w_in',)
WIN_SEGS = (('gates', 0, 3072), ('cq', 3072, 3456), ('ckv', 3456, 3712), ('kpe', 3712, 3744), ('rkv', 3744, 4512),
            ('xwa', 4512, 4640), ('xg', 4640, 4768), ('conv', 4768, 5536))
PACK_ROW_MULT = 512


def _cparams(sem=None, **kw):
    if sem is not None:
        kw['dimension_semantics'] = sem
    return pltpu.CompilerParams(vmem_limit_bytes=VMEM_LIMIT, **kw)


def _pick(n, cands):
    for c in cands:
        if n % c == 0:
            return c
    raise ValueError(f'no tile for {n}')


def _mm_nn(a, b, add=None, name='mm_nn'):
    M, K = a.shape
    N = b.shape[1]
    tm = _pick(M, (1024, 512, 256, 128))
    tn = _pick(N, (512, 384, 256, 128))
    tk = _pick(K, (1024, 512, 384, 256, 128))
    nk = K // tk
    has_add = add is not None

    def body(*refs):
        if has_add:
            a_ref, b_ref, add_ref, o_ref, acc_ref = refs
        else:
            a_ref, b_ref, o_ref, acc_ref = refs
        kk = pl.program_id(2)
        part = jnp.dot(a_ref[...].astype(BF16), b_ref[...].astype(BF16), preferred_element_type=F32)

        @pl.when(kk == 0)
        def _():
            acc_ref[...] = part

        @pl.when(kk > 0)
        def _():
            acc_ref[...] += part

        @pl.when(kk == nk - 1)
        def _():
            if has_add:
                o_ref[...] = acc_ref[...] + add_ref[...]
            else:
                o_ref[...] = acc_ref[...]

    in_specs = [pl.BlockSpec((tm, tk), lambda i, j, k: (i, k)), pl.BlockSpec((tk, tn), lambda i, j, k: (k, j))]
    args = [a, b]
    if has_add:
        in_specs.append(pl.BlockSpec((tm, tn), lambda i, j, k: (i, j)))
        args.append(add)
    return pl.pallas_call(
        body, name=name, grid=(M // tm, N // tn, nk), in_specs=in_specs,
        out_specs=pl.BlockSpec((tm, tn), lambda i, j, k: (i, j)),
        out_shape=jax.ShapeDtypeStruct((M, N), F32),
        scratch_shapes=[pltpu.VMEM((tm, tn), F32)],
        compiler_params=_cparams(('parallel', 'parallel', 'arbitrary')),
    )(*args)


def _mm_nt(a, b, add=None, name='mm_nt'):
    M, N = a.shape
    blocked = b.ndim == 3
    K = b.shape[-2]
    tm = _pick(M, (1024, 512, 256, 128))
    tk = _pick(K, (512, 384, 256, 128))
    tn = N // N_DEV if blocked else _pick(N, (1024, 512, 384, 256, 128))
    nn = N // tn
    has_add = add is not None

    def body(*refs):
        if has_add:
            a_ref, b_ref, add_ref, o_ref, acc_ref = refs
        else:
            a_ref, b_ref, o_ref, acc_ref = refs
        kk = pl.program_id(2)
        part = lax.dot_general(a_ref[...].astype(BF16), b_ref[...].astype(BF16), (((1,), (1,)), ((), ())),
                               preferred_element_type=F32)

        @pl.when(kk == 0)
        def _():
            acc_ref[...] = part

        @pl.when(kk > 0)
        def _():
            acc_ref[...] += part

        @pl.when(kk == nn - 1)
        def _():
            if has_add:
                o_ref[...] = acc_ref[...] + add_ref[...]
            else:
                o_ref[...] = acc_ref[...]

    if blocked:
        b_spec = pl.BlockSpec((None, tk, tn), lambda i, j, k: (k, j, 0))
    else:
        b_spec = pl.BlockSpec((tk, tn), lambda i, j, k: (j, k))
    in_specs = [pl.BlockSpec((tm, tn), lambda i, j, k: (i, k)), b_spec]
    args = [a, b]
    if has_add:
        in_specs.append(pl.BlockSpec((tm, tk), lambda i, j, k: (i, j)))
        args.append(add)
    return pl.pallas_call(
        body, name=name, grid=(M // tm, K // tk, nn), in_specs=in_specs,
        out_specs=pl.BlockSpec((tm, tk), lambda i, j, k: (i, j)),
        out_shape=jax.ShapeDtypeStruct((M, K), F32),
        scratch_shapes=[pltpu.VMEM((tm, tk), F32)],
        compiler_params=_cparams(('parallel', 'parallel', 'arbitrary')),
    )(*args)


def _mm_tn(a, b, name='mm_tn', blocked=False):
    M, K = a.shape
    N = b.shape[1]
    tm = _pick(M, (1024, 512, 256, 128))
    tk = _pick(K, (512, 384, 256, 128))
    tn = N // N_DEV if blocked else _pick(N, (512, 384, 256, 128))
    nm = M // tm

    def body(a_ref, b_ref, o_ref, acc_ref):
        mm = pl.program_id(2)
        part = lax.dot_general(a_ref[...].astype(BF16), b_ref[...].astype(BF16), (((0,), (0,)), ((), ())),
                               preferred_element_type=F32)

        @pl.when(mm == 0)
        def _():
            acc_ref[...] = part

        @pl.when(mm > 0)
        def _():
            acc_ref[...] += part

        @pl.when(mm == nm - 1)
        def _():
            o_ref[...] = acc_ref[...].astype(BF16)

    if blocked:
        out_spec = pl.BlockSpec((None, tk, tn), lambda i, j, m: (j, i, 0))
        out_shape = jax.ShapeDtypeStruct((N_DEV, K, tn), BF16)
    else:
        out_spec = pl.BlockSpec((tk, tn), lambda i, j, m: (i, j))
        out_shape = jax.ShapeDtypeStruct((K, N), BF16)
    return pl.pallas_call(
        body, name=name, grid=(K // tk, N // tn, nm),
        in_specs=[pl.BlockSpec((tm, tk), lambda i, j, m: (m, i)), pl.BlockSpec((tm, tn), lambda i, j, m: (m, j))],
        out_specs=out_spec, out_shape=out_shape,
        scratch_shapes=[pltpu.VMEM((tk, tn), F32)],
        compiler_params=_cparams(('parallel', 'parallel', 'arbitrary')),
    )(a, b)


@functools.partial(jax.custom_vjp, nondiff_argnums=(4,))
def _linear_add(a, wb, wc, add, name):
    return _mm_nn(a, wb, add, name=name + '_f')


def _linear_add_fwd(a, wb, wc, add, name):
    return _mm_nn(a, wb, add, name=name + '_f'), (a, wb)


def _linear_add_bwd(name, res, dy):
    a, wb = res
    return _mm_nt(dy, wb, name=name + '_da'), None, _mm_tn(a, dy, name=name + '_dw'), dy


_linear_add.defvjp(_linear_add_fwd, _linear_add_bwd)


@functools.partial(jax.custom_vjp, nondiff_argnums=(3,))
def _multi_linear(a, wbs, wcs, name):
    return tuple(_mm_nn(a, wb, name=f'{name}_f{i}') for i, wb in enumerate(wbs))


def _multi_linear_fwd(a, wbs, wcs, name):
    return _multi_linear(a, wbs, wcs, name), (a, wbs)


def _multi_linear_bwd(name, res, dys):
    a, wbs = res
    da = None
    for i, (dy, wb) in enumerate(zip(dys, wbs)):
        da = _mm_nt(dy, wb, add=da, name=f'{name}_da{i}')
    dws = tuple(_mm_tn(a, dy, name=f'{name}_dw{i}') for i, dy in enumerate(dys))
    return da, None, dws


_multi_linear.defvjp(_multi_linear_fwd, _multi_linear_bwd)


class W:
    def __init__(self, b, c):
        self.b, self.c = b, c

    def map(self, fn):
        return W(fn(self.b), fn(self.c))


def _wcat(ws, axis):
    return W(jnp.concatenate([w.b for w in ws], axis), jnp.concatenate([w.c for w in ws], axis))


def linear(a, w, add=None, name='lin'):
    if add is None:
        return _multi_linear(a, (w.b,), (w.c,), name)[0]
    return _linear_add(a, w.b, w.c, add, name)


def ROW(diff=True, pieces=None):
    return ('row', diff, pieces)


def FULL(diff=True):
    return ('full', diff, None)


def _load_args(refs, specs):
    args, amap = [], []
    for i, (ref, (kind, diff, pieces)) in enumerate(zip(refs, specs)):
        if pieces is None:
            args.append(ref[...])
            amap.append((i, None))
        else:
            for (s, w) in pieces:
                args.append(ref[:, s:s + w])
                amap.append((i, (s, w)))
    return args, amap


def _stage_in_specs(ins, specs, tb):
    out = []
    for a, (kind, _, _) in zip(ins, specs):
        if kind == 'row':
            out.append(pl.BlockSpec((tb, a.shape[1]), lambda i: (i, 0)))
        else:
            out.append(pl.BlockSpec(a.shape, lambda i: (0, 0)))
    return out


def _stage_fwd(fn, ins, specs, out_widths, name, tb, out_dtype=F32):
    T = [a for a, s in zip(ins, specs) if s[0] == 'row'][0].shape[0]
    tb = min(tb, T)
    n_in = len(ins)

    def body(*refs):
        args, _ = _load_args(refs[:n_in], specs)
        outs = fn(*args)
        for o_ref, o in zip(refs[n_in:], outs):
            o_ref[...] = o.astype(out_dtype)

    return pl.pallas_call(
        body, name=name + '_f', grid=(T // tb,), in_specs=_stage_in_specs(ins, specs, tb),
        out_specs=[pl.BlockSpec((tb, w), lambda i: (i, 0)) for w in out_widths],
        out_shape=[jax.ShapeDtypeStruct((T, w), out_dtype) for w in out_widths],
        compiler_params=_cparams(('parallel',)),
    )(*ins)


def _stage_bwd(fn, ins, specs, out_widths, douts, name, tb, add_to_first=None):
    T = [a for a, s in zip(ins, specs) if s[0] == 'row'][0].shape[0]
    tb = min(tb, T)
    n_in, n_out = len(ins), len(out_widths)
    diff_inputs = [i for i, s in enumerate(specs) if s[1]]
    n_add = 0 if add_to_first is None else 1

    def body(*refs):
        in_refs, dout_refs = refs[:n_in], refs[n_in:n_in + n_out]
        g_refs = refs[n_in + n_out + n_add:]
        args, amap = _load_args(in_refs, specs)
        didx = [j for j, (i, _) in enumerate(amap) if specs[i][1]]

        def f(*dv):
            full = list(args)
            for j, v in zip(didx, dv):
                full[j] = v
            return tuple(fn(*full))

        _, vjp = jax.vjp(f, *[args[j] for j in didx])
        gs = vjp(tuple(d[...] for d in dout_refs))
        gmap = {j: g for j, g in zip(didx, gs)}
        first = pl.program_id(0) == 0
        for g_ref, i in zip(g_refs, diff_inputs):
            kind, _, pieces = specs[i]
            js = [j for j, (ii, _) in enumerate(amap) if ii == i]
            if kind == 'row':
                if pieces is None:
                    if n_add and i == diff_inputs[0]:
                        g_ref[...] = gmap[js[0]] + refs[n_in + n_out][...]
                    else:
                        g_ref[...] = gmap[js[0]]
                else:
                    if sum(w for _, w in pieces) != ins[i].shape[1]:
                        g_ref[...] = jnp.zeros(g_ref.shape, F32)
                    for j in js:
                        s, w = amap[j][1]
                        g_ref[:, s:s + w] = gmap[j]
            else:
                @pl.when(first)
                def _(g_ref=g_ref):
                    g_ref[...] = jnp.zeros(g_ref.shape, F32)

                g_ref[...] += gmap[js[0]]

    in_specs = _stage_in_specs(ins, specs, tb) + [pl.BlockSpec((tb, w), lambda i: (i, 0)) for w in out_widths]
    extra = []
    if n_add:
        in_specs.append(pl.BlockSpec((tb, add_to_first.shape[1]), lambda i: (i, 0)))
        extra.append(add_to_first)
    out_specs, out_shape = [], []
    for i in diff_inputs:
        a = ins[i]
        if specs[i][0] == 'row':
            out_specs.append(pl.BlockSpec((tb, a.shape[1]), lambda i: (i, 0)))
        else:
            out_specs.append(pl.BlockSpec(a.shape, lambda i: (0, 0)))
        out_shape.append(jax.ShapeDtypeStruct(a.shape, F32))
    return pl.pallas_call(
        body, name=name + '_b', grid=(T // tb,), in_specs=in_specs, out_specs=out_specs, out_shape=out_shape,
        compiler_params=_cparams(('arbitrary',)),
    )(*ins, *douts, *extra)


def stage_op(fn, specs, out_widths, name, tb=256):
    n = len(specs)
    diff_inputs = [i for i, s in enumerate(specs) if s[1]]

    @jax.custom_vjp
    def op(*ins):
        return tuple(_stage_fwd(fn, ins, specs, out_widths, name, tb))

    def op_fwd(*ins):
        return op(*ins), ins

    def op_bwd(ins, douts):
        gs = _stage_bwd(fn, ins, specs, out_widths, douts, name, tb)
        res = [None] * n
        for i, g in zip(diff_inputs, gs):
            res[i] = g
        return tuple(res)

    op.defvjp(op_fwd, op_bwd)
    return op


@jax.custom_vjp
def bdot(x, w):
    return jnp.dot(x.astype(BF16), w.astype(BF16), preferred_element_type=F32)


def _bdot_fwd(x, w):
    return bdot(x, w), (x, w)


def _bdot_bwd(res, dy):
    x, w = res
    dyb = dy.astype(BF16)
    dx = lax.dot_general(dyb, w.astype(BF16), (((1,), (1,)), ((), ())), preferred_element_type=F32)
    dw = lax.dot_general(x.astype(BF16), dyb, (((0,), (0,)), ((), ())), preferred_element_type=F32)
    return dx, dw


bdot.defvjp(_bdot_fwd, _bdot_bwd)


def _sdot_raw(x, c):
    hi = x.astype(BF16)
    r1 = x - hi.astype(F32)
    mid = r1.astype(BF16)
    lo = (r1 - mid.astype(F32)).astype(BF16)
    d = lambda u: jnp.dot(u, c, preferred_element_type=F32)
    return d(hi) + d(mid) + d(lo)


@jax.custom_vjp
def sdot(x, c, ct):
    return _sdot_raw(x, c)


def _sdot_fwd(x, c, ct):
    return _sdot_raw(x, c), (c, ct)


def _sdot_bwd(res, dy):
    c, ct = res
    return _sdot_raw(dy, ct), None, None


sdot.defvjp(_sdot_fwd, _sdot_bwd)


def _sigmoid(x):
    return 1.0 / (1.0 + jnp.exp(-x))


def _rms(x, g):
    return x * lax.rsqrt(jnp.mean(x * x, axis=-1, keepdims=True) + NORM_EPS) * g


def _mm_up_relu2(a, b, name):
    M, K = a.shape
    tn = b.shape[2]
    N = N_DEV * tn
    tm = _pick(M, (1024, 512, 256, 128))

    def body(a_ref, b_ref, u_ref, act_ref):
        u = jnp.dot(a_ref[...], b_ref[...], preferred_element_type=F32)
        r = jnp.maximum(u, 0.0)
        u_ref[...] = u.astype(BF16)
        act_ref[...] = (r * r).astype(BF16)

    out = pl.BlockSpec((tm, tn), lambda i, j: (i, j))
    sh = jax.ShapeDtypeStruct((M, N), BF16)
    return pl.pallas_call(
        body, name=name, grid=(M // tm, N // tn),
        in_specs=[pl.BlockSpec((tm, K), lambda i, j: (i, 0)), pl.BlockSpec((None, K, tn), lambda i, j: (j, 0, 0))],
        out_specs=[out, out], out_shape=[sh, sh], compiler_params=_cparams(('parallel', 'parallel')),
    )(a, b)


def _mm_down_bwd(dy, b, u, name):
    M, N = dy.shape
    K = b.shape[0]
    tm = _pick(M, (1024, 512, 256, 128))
    tk = _pick(K, (512, 256, 128))

    def body(dy_ref, b_ref, u_ref, du_ref):
        d = lax.dot_general(dy_ref[...].astype(BF16), b_ref[...], (((1,), (1,)), ((), ())),
                            preferred_element_type=F32)
        du_ref[...] = (d * (2.0 * jnp.maximum(u_ref[...].astype(F32), 0.0))).astype(BF16)

    blk = pl.BlockSpec((tm, tk), lambda i, j: (i, j))
    return pl.pallas_call(
        body, name=name, grid=(M // tm, K // tk),
        in_specs=[pl.BlockSpec((tm, N), lambda i, j: (i, 0)), pl.BlockSpec((tk, N), lambda i, j: (j, 0)), blk],
        out_specs=blk, out_shape=jax.ShapeDtypeStruct((M, K), BF16),
        compiler_params=_cparams(('parallel', 'parallel')),
    )(dy, b, u)


_RMS_SPECS = [ROW(), FULL()]
_rms_fn = lambda xv, gv: (_rms(xv, gv),)


@functools.partial(jax.custom_vjp, nondiff_argnums=(6,))
def mlp_block(x, g, wup_b, wup_c, wdown_b, wdown_c, name):
    return _mlp_fwd(x, g, wup_b, wup_c, wdown_b, wdown_c, name)[0]


def _mlp_fwd(x, g, wup_b, wup_c, wdown_b, wdown_c, name):
    h = _stage_fwd(_rms_fn, (x, g), _RMS_SPECS, [x.shape[1]], name + '_norm', 256, out_dtype=BF16)[0]
    u, act = _mm_up_relu2(h, wup_b, name + '_up')
    y = _mm_nn(act, wdown_b, add=x, name=name + '_down')
    return y, (x, g, h, u, act, wup_b, wdown_b)


def _mlp_bwd(name, res, dy):
    x, g, h, u, act, wup_b, wdown_b = res
    du = _mm_down_bwd(dy, wdown_b, u, name + '_down_da')
    dwdown = _mm_tn(act, dy, name=name + '_down_dw')
    dh = _mm_nt(du, wup_b, name=name + '_up_da')
    dwup = _mm_tn(h, du, name=name + '_up_dw', blocked=True)
    dx, dg = _stage_bwd(_rms_fn, (x, g), _RMS_SPECS, [x.shape[1]], (dh,), name + '_norm', 256, add_to_first=dy)
    return dx, dg, None, dwup, None, dwdown


mlp_block.defvjp(_mlp_fwd, _mlp_bwd)


@functools.partial(jax.custom_vjp, nondiff_argnums=(4, 5, 6))
def _norm_projections(x, g, wbs, wcs, transposed, skip, name):
    return _norm_projections_fwd(x, g, wbs, wcs, transposed, skip, name)[0]


def _norm_projections_fwd(x, g, wbs, wcs, transposed, skip, name):
    h = _stage_fwd(_rms_fn, (x, g), _RMS_SPECS, [x.shape[1]], name + '_norm', 256, out_dtype=BF16)[0]
    mm = _mm_nt if transposed else _mm_nn
    outs = tuple(mm(h, wb, name=f'{name}_f{i}') for i, wb in enumerate(wbs))
    return ((outs, x) if skip else outs), (x, g, h, wbs)


def _norm_projections_bwd(transposed, skip, name, res, cts):
    x, g, h, wbs = res
    dys, dx_skip = cts if skip else (cts, None)
    dh = None
    for i, (dy, wb) in enumerate(zip(dys, wbs)):
        dh = (_mm_nn if transposed else _mm_nt)(dy, wb, add=dh, name=f'{name}_da{i}')
    if transposed:
        dws = tuple(_mm_tn(dy, h, name=f'{name}_dw{i}') for i, dy in enumerate(dys))
    else:
        dws = tuple(_mm_tn(h, dy, name=f'{name}_dw{i}') for i, dy in enumerate(dys))
    dx, dg = _stage_bwd(_rms_fn, (x, g), _RMS_SPECS, [x.shape[1]], (dh,), name + '_norm', 256, add_to_first=dx_skip)
    return dx, dg, None, dws


_norm_projections.defvjp(_norm_projections_fwd, _norm_projections_bwd)


def norm_projections(x, g, ws, name, transposed=False, skip=False):
    return _norm_projections(x, g.reshape(1, -1), tuple(w.b for w in ws), tuple(w.c for w in ws), transposed, skip,
                             name)


def _shift_down(x, rows):
    return jnp.where(rows == 0, 0.0, pltpu.roll(x, 1, 0))


def _shift_up(x, rows, T):
    return jnp.where(rows == T - 1, 0.0, pltpu.roll(x, T - 1, 0))


def _tshift_fwd_call(x, mu, name):
    T, C = x.shape

    def body(x_ref, mu_ref, o_ref):
        xv = x_ref[...]
        rows = lax.broadcasted_iota(jnp.int32, xv.shape, 0)
        o_ref[...] = xv + (_shift_down(xv, rows) - xv) * mu_ref[...]

    return pl.pallas_call(
        body, name=name + '_f', grid=(C // LANES,),
        in_specs=[pl.BlockSpec((T, LANES), lambda j: (0, j)), pl.BlockSpec((1, LANES), lambda j: (0, j))],
        out_specs=pl.BlockSpec((T, LANES), lambda j: (0, j)), out_shape=jax.ShapeDtypeStruct((T, C), F32),
        compiler_params=_cparams(('parallel',)),
    )(x, mu)


def _tshift_bwd_call(x, mu, dy, name):
    T, C = x.shape

    def body(x_ref, mu_ref, dy_ref, dx_ref, dmu_ref):
        xv, d = x_ref[...], dy_ref[...]
        rows = lax.broadcasted_iota(jnp.int32, xv.shape, 0)
        z = d * mu_ref[...]
        dx_ref[...] = d - z + _shift_up(z, rows, T)
        dmu_ref[...] = jnp.sum(d * (_shift_down(xv, rows) - xv), axis=0, keepdims=True)

    return pl.pallas_call(
        body, name=name + '_b', grid=(C // LANES,),
        in_specs=[pl.BlockSpec((T, LANES), lambda j: (0, j)), pl.BlockSpec((1, LANES), lambda j: (0, j)),
                  pl.BlockSpec((T, LANES), lambda j: (0, j))],
        out_specs=[pl.BlockSpec((T, LANES), lambda j: (0, j)), pl.BlockSpec((1, LANES), lambda j: (0, j))],
        out_shape=[jax.ShapeDtypeStruct((T, C), F32), jax.ShapeDtypeStruct((1, C), F32)],
        compiler_params=_cparams(('parallel',)),
    )(x, mu, dy)


@functools.partial(jax.custom_vjp, nondiff_argnums=(2,))
def token_shift_mix(x, mu, name):
    return _tshift_fwd_call(x, mu, name)


def _tsm_fwd(x, mu, name):
    return _tshift_fwd_call(x, mu, name), (x, mu)


def _tsm_bwd(name, res, dy):
    x, mu = res
    dx, dmu = _tshift_bwd_call(x, mu, dy, name)
    return dx, dmu


token_shift_mix.defvjp(_tsm_fwd, _tsm_bwd)


def _conv_specs(T):
    nb = CONV_WIDTH // LANES
    return [pl.BlockSpec((T, LANES), lambda j: (0, j)), pl.BlockSpec((T, LANES), lambda j: (0, nb + j)),
            pl.BlockSpec((T, LANES), lambda j: (0, 2 * nb + j)), pl.BlockSpec((3, LANES), lambda j: (0, j))]


def _conv_fwd_call(cv, w, name):
    T = cv.shape[0]

    def body(b_ref, c_ref, x_ref, w_ref, o_ref):
        u = c_ref[...] * x_ref[...]
        rows = lax.broadcasted_iota(jnp.int32, u.shape, 0)
        u1 = _shift_down(u, rows)
        u2 = _shift_down(u1, rows)
        o_ref[...] = b_ref[...] * (w_ref[0:1, :] * u2 + w_ref[1:2, :] * u1 + w_ref[2:3, :] * u)

    return pl.pallas_call(
        body, name=name + '_f', grid=(CONV_WIDTH // LANES,), in_specs=_conv_specs(T),
        out_specs=pl.BlockSpec((T, LANES), lambda j: (0, j)),
        out_shape=jax.ShapeDtypeStruct((T, CONV_WIDTH), F32), compiler_params=_cparams(('parallel',)),
    )(cv, cv, cv, w)


def _conv_bwd_call(cv, w, do, name):
    T = cv.shape[0]

    def body(b_ref, c_ref, x_ref, w_ref, do_ref, db_ref, dc_ref, dx_ref, dw_ref):
        c, x, d = c_ref[...], x_ref[...], do_ref[...]
        u = c * x
        rows = lax.broadcasted_iota(jnp.int32, u.shape, 0)
        u1 = _shift_down(u, rows)
        u2 = _shift_down(u1, rows)
        w0, w1, w2 = w_ref[0:1, :], w_ref[1:2, :], w_ref[2:3, :]
        db_ref[...] = d * (w0 * u2 + w1 * u1 + w2 * u)
        dy = d * b_ref[...]
        dy1 = _shift_up(dy, rows, T)
        dy2 = _shift_up(dy1, rows, T)
        du = w2 * dy + w1 * dy1 + w0 * dy2
        dc_ref[...] = du * x
        dx_ref[...] = du * c
        dw_ref[0:1, :] = jnp.sum(dy * u2, axis=0, keepdims=True)
        dw_ref[1:2, :] = jnp.sum(dy * u1, axis=0, keepdims=True)
        dw_ref[2:3, :] = jnp.sum(dy * u, axis=0, keepdims=True)

    blk = pl.BlockSpec((T, LANES), lambda j: (0, j))
    sh = jax.ShapeDtypeStruct((T, CONV_WIDTH), F32)
    return pl.pallas_call(
        body, name=name + '_b', grid=(CONV_WIDTH // LANES,), in_specs=_conv_specs(T) + [blk],
        out_specs=[blk, blk, blk, pl.BlockSpec((3, LANES), lambda j: (0, j))],
        out_shape=[sh, sh, sh, jax.ShapeDtypeStruct((3, CONV_WIDTH), F32)],
        compiler_params=_cparams(('parallel',)),
    )(cv, cv, cv, w, do)


@functools.partial(jax.custom_vjp, nondiff_argnums=(2,))
def short_conv(cv, w, name):
    return _conv_fwd_call(cv, w, name)


def _sc_fwd(cv, w, name):
    return _conv_fwd_call(cv, w, name), (cv, w)


def _sc_bwd(name, res, do):
    cv, w = res
    db, dc, dx, dw = _conv_bwd_call(cv, w, do, name)
    return jnp.concatenate([db, dc, dx], axis=1), dw


short_conv.defvjp(_sc_fwd, _sc_bwd)


ATT_SCALE = QK_HEAD ** -0.5
NPAIR = MLA_HEADS // 2


def _att_bq(T):
    return min(256, T)


def _att_masks(pair, j):
    lane = lax.broadcasted_iota(jnp.int32, (1, LANES), 1)
    mask_n = (lane // QK_NOPE) == j
    mask_r = (lane // (QK_ROPE // 2)) == (2 * pair + j)
    return mask_n, mask_r


def _att_probs(qcat, kcat, row0, stop):
    s = lax.dot_general(qcat, kcat, (((1,), (1,)), ((), ())), preferred_element_type=F32) * ATT_SCALE
    r = row0 + lax.broadcasted_iota(jnp.int32, s.shape, 0)
    c = lax.broadcasted_iota(jnp.int32, s.shape, 1)
    s = jnp.where(c <= r, s, -jnp.inf)
    e = jnp.exp(s - jnp.max(s, axis=-1, keepdims=True))
    return e / jnp.sum(e, axis=-1, keepdims=True)


def _att_in_specs(T):
    blk = lambda f: pl.BlockSpec((T, LANES), f)
    return [blk(lambda p: (0, p)), blk(lambda p: (0, 0)), blk(lambda p: (0, 0)),
            blk(lambda p: (0, p)), blk(lambda p: (0, 0)), blk(lambda p: (0, 0)), blk(lambda p: (0, p))]


def _att_fwd_call(qn, q1, q2, kn, k1, k2, v, name):
    T = qn.shape[0]
    bq = _att_bq(T)

    def body(qn_ref, q1_ref, q2_ref, kn_ref, k1_ref, k2_ref, v_ref, o_ref):
        pair = pl.program_id(0)
        for i in range(T // bq):
            r0, stop = i * bq, (i + 1) * bq
            kcat = jnp.concatenate([kn_ref[0:stop, :], k1_ref[0:stop, :], k2_ref[0:stop, :]], axis=1).astype(BF16)
            vb = v_ref[0:stop, :].astype(BF16)
            outs = []
            for j in range(2):
                mask_n, mask_r = _att_masks(pair, j)
                qcat = jnp.concatenate([jnp.where(mask_n, qn_ref[r0:stop, :], 0.0),
                                        jnp.where(mask_r, q1_ref[r0:stop, :], 0.0),
                                        jnp.where(mask_r, q2_ref[r0:stop, :], 0.0)], axis=1).astype(BF16)
                p = _att_probs(qcat, kcat, r0, stop)
                outs.append(jnp.dot(p.astype(BF16), vb, preferred_element_type=F32))
            mask_n0, _ = _att_masks(pair, 0)
            o_ref[r0:stop, :] = jnp.where(mask_n0, outs[0], outs[1])

    return pl.pallas_call(
        body, name=name + '_f', grid=(NPAIR,), in_specs=_att_in_specs(T),
        out_specs=pl.BlockSpec((T, LANES), lambda p: (0, p)),
        out_shape=jax.ShapeDtypeStruct((T, MLA_HEADS * V_HEAD), F32), compiler_params=_cparams(('parallel',)),
    )(qn, q1, q2, kn, k1, k2, v)


def _att_bwd_call(qn, q1, q2, kn, k1, k2, v, o, do, name):
    T = qn.shape[0]
    bq = _att_bq(T)

    def body(qn_ref, q1_ref, q2_ref, kn_ref, k1_ref, k2_ref, v_ref, o_ref, do_ref,
             dqn_ref, dq1_ref, dq2_ref, dkn_ref, dk1_ref, dk2_ref, dv_ref, dk_acc, dv_acc):
        pair = pl.program_id(0)

        @pl.when(pair == 0)
        def _():
            dq1_ref[...] = jnp.zeros(dq1_ref.shape, F32)
            dq2_ref[...] = jnp.zeros(dq2_ref.shape, F32)
            dk1_ref[...] = jnp.zeros(dk1_ref.shape, F32)
            dk2_ref[...] = jnp.zeros(dk2_ref.shape, F32)

        dk_acc[...] = jnp.zeros(dk_acc.shape, F32)
        dv_acc[...] = jnp.zeros(dv_acc.shape, F32)
        for i in range(T // bq):
            r0, stop = i * bq, (i + 1) * bq
            kcat = jnp.concatenate([kn_ref[0:stop, :], k1_ref[0:stop, :], k2_ref[0:stop, :]], axis=1).astype(BF16)
            vb = v_ref[0:stop, :].astype(BF16)
            dqn = jnp.zeros((bq, LANES), F32)
            for j in range(2):
                mask_n, mask_r = _att_masks(pair, j)
                qcat = jnp.concatenate([jnp.where(mask_n, qn_ref[r0:stop, :], 0.0),
                                        jnp.where(mask_r, q1_ref[r0:stop, :], 0.0),
                                        jnp.where(mask_r, q2_ref[r0:stop, :], 0.0)], axis=1).astype(BF16)
                p = _att_probs(qcat, kcat, r0, stop)
                dom = jnp.where(mask_n, do_ref[r0:stop, :], 0.0)
                delta = jnp.sum(dom * o_ref[r0:stop, :], axis=-1, keepdims=True)
                domb = dom.astype(BF16)
                dp = lax.dot_general(domb, vb, (((1,), (1,)), ((), ())), preferred_element_type=F32)
                ds = (p * (dp - delta) * ATT_SCALE).astype(BF16)
                dqc = jnp.dot(ds, kcat, preferred_element_type=F32)
                dqn = dqn + jnp.where(mask_n, dqc[:, 0:LANES], 0.0)
                dq1_ref[r0:stop, :] += jnp.where(mask_r, dqc[:, LANES:2 * LANES], 0.0)
                dq2_ref[r0:stop, :] += jnp.where(mask_r, dqc[:, 2 * LANES:3 * LANES], 0.0)
                dk_acc[0:stop, :] += lax.dot_general(ds, qcat, (((0,), (0,)), ((), ())),
                                                     preferred_element_type=F32)
                dv_acc[0:stop, :] += lax.dot_general(p.astype(BF16), domb, (((0,), (0,)), ((), ())),
                                                     preferred_element_type=F32)
            dqn_ref[r0:stop, :] = dqn
        dkn_ref[...] = dk_acc[:, 0:LANES]
        dk1_ref[...] += dk_acc[:, LANES:2 * LANES]
        dk2_ref[...] += dk_acc[:, 2 * LANES:3 * LANES]
        dv_ref[...] = dv_acc[...]

    per_pair = pl.BlockSpec((T, LANES), lambda p: (0, p))
    shared = pl.BlockSpec((T, LANES), lambda p: (0, 0))
    wide = jax.ShapeDtypeStruct((T, MLA_HEADS * QK_NOPE), F32)
    narrow = jax.ShapeDtypeStruct((T, LANES), F32)
    return pl.pallas_call(
        body, name=name + '_b', grid=(NPAIR,), in_specs=_att_in_specs(T) + [per_pair, per_pair],
        out_specs=[per_pair, shared, shared, per_pair, shared, shared, per_pair],
        out_shape=[wide, narrow, narrow, wide, narrow, narrow, wide],
        scratch_shapes=[pltpu.VMEM((T, 3 * LANES), F32), pltpu.VMEM((T, LANES), F32)],
        compiler_params=_cparams(('arbitrary',)),
    )(qn, q1, q2, kn, k1, k2, v, o, do)


@functools.partial(jax.custom_vjp, nondiff_argnums=(7,))
def attention(qn, q1, q2, kn, k1, k2, v, name):
    return _att_fwd_call(qn, q1, q2, kn, k1, k2, v, name)


def _attn_fwd(qn, q1, q2, kn, k1, k2, v, name):
    o = _att_fwd_call(qn, q1, q2, kn, k1, k2, v, name)
    return o, (qn, q1, q2, kn, k1, k2, v, o)


def _attn_bwd(name, res, do):
    return tuple(_att_bwd_call(*res, do, name))


attention.defvjp(_attn_fwd, _attn_bwd)


SCAN_CHUNK = 64
SCAN_UNROLL = 8


def _block_ones(n, seg):
    i = np.arange(n)
    return (i[:, None] // seg == i[None, :] // seg).astype(np.float32)


def _scan_diag():
    i = np.arange(RW_WIDTH)
    return jnp.asarray((np.arange(RW_N)[:, None] == (i[None, :] % RW_N)).astype(np.float32))


def _head_rowsum(x):
    low = lax.broadcasted_iota(jnp.int32, (1, LANES), 1) < RW_N
    tiles = []
    for j in range(RW_WIDTH // LANES):
        xt = x[:, j * LANES:(j + 1) * LANES]
        x0 = jnp.where(low, xt, 0.0)
        s0 = jnp.sum(x0, axis=-1, keepdims=True)
        s1 = jnp.sum(xt - x0, axis=-1, keepdims=True)
        tiles.append(jnp.where(low, s0, s1))
    return jnp.concatenate(tiles, axis=1)


def _unrolled_loop(n, step, init):
    def body(i, carry):
        for j in range(SCAN_UNROLL):
            carry = step(i * SCAN_UNROLL + j, carry)
        return carry
    return lax.fori_loop(0, n // SCAN_UNROLL, body, init)


def _scan_fwd_call(r, w, k, v, a, b, name):
    T = r.shape[0]
    tc = min(SCAN_CHUNK, T)
    dg = _scan_diag()

    def body(r_ref, w_ref, k_ref, v_ref, a_ref, b_ref, dg_ref, y_ref, st_ref, sa_ref, vc_ref, s_ref):
        @pl.when(pl.program_id(0) == 0)
        def _():
            s_ref[...] = jnp.zeros(s_ref.shape, F32)

        dgv = dg_ref[...]
        readout = lambda s, t: jnp.sum(_head_rowsum(s * r_ref[t]) * dgv, axis=0, keepdims=True)

        def step(t, carry):
            s, vcol = carry
            st_ref[t] = s
            vc_ref[t] = vcol
            sa = _head_rowsum(s * a_ref[t])
            sa_ref[t] = sa
            prev = jnp.maximum(t - 1, 0)
            y_ref[prev] = readout(s, prev)
            vcol_next = _head_rowsum(dgv * v_ref[jnp.minimum(t + 1, tc - 1)])
            sn = s * w_ref[t] + sa * b_ref[t] + vcol * k_ref[t]
            return sn, vcol_next

        s_end, _ = _unrolled_loop(tc, step, (s_ref[...], _head_rowsum(dgv * v_ref[0])))
        y_ref[tc - 1] = readout(s_end, tc - 1)
        s_ref[...] = s_end

    vec = pl.BlockSpec((tc, 1, RW_WIDTH), lambda i: (i, 0, 0))
    mat = pl.BlockSpec((tc, RW_N, RW_WIDTH), lambda i: (i, 0, 0))
    msh = jax.ShapeDtypeStruct((T, RW_N, RW_WIDTH), F32)
    return pl.pallas_call(
        body, name=name + '_f', grid=(T // tc,),
        in_specs=[vec] * 6 + [pl.BlockSpec((RW_N, RW_WIDTH), lambda i: (0, 0))],
        out_specs=[vec, mat, mat, mat],
        out_shape=[jax.ShapeDtypeStruct((T, 1, RW_WIDTH), F32), msh, msh, msh],
        scratch_shapes=[pltpu.VMEM((RW_N, RW_WIDTH), F32)],
        compiler_params=_cparams(('arbitrary',)),
    )(r, w, k, v, a, b, dg)


def _scan_bwd_call(r, w, k, a, b, st, sa_all, vc_all, dy, name):
    T = r.shape[0]
    tc = min(SCAN_CHUNK, T)
    nt = T // tc
    dg = _scan_diag()

    def body(r_ref, w_ref, k_ref, a_ref, b_ref, st_ref, sa_ref, vc_ref, dy_ref, dg_ref,
             dr_ref, dw_ref, dk_ref, dv_ref, da_ref, db_ref, ds_ref):
        @pl.when(pl.program_id(0) == 0)
        def _():
            ds_ref[...] = jnp.zeros(ds_ref.shape, F32)

        dgv = dg_ref[...]
        colsum = lambda x: jnp.sum(x, axis=0, keepdims=True)

        def step(i, carry):
            ds, dycol = carry
            t = tc - 1 - i
            sp = st_ref[t]
            rt, wt, kt, at, bt = r_ref[t], w_ref[t], k_ref[t], a_ref[t], b_ref[t]
            ds = ds + dycol * rt
            dsa = _head_rowsum(ds * bt)
            sa, vcol = sa_ref[t], vc_ref[t]
            dycol_next = _head_rowsum(dgv * dy_ref[jnp.maximum(t - 1, 0)])
            sn = sp * wt + sa * bt + vcol * kt
            dr_ref[t] = colsum(sn * dycol)
            dk_ref[t] = colsum(ds * vcol)
            db_ref[t] = colsum(ds * sa)
            dw_ref[t] = colsum(ds * sp)
            dv_ref[t] = colsum(_head_rowsum(ds * kt) * dgv)
            da_ref[t] = colsum(sp * dsa)
            return ds * wt + dsa * at, dycol_next

        ds_end, _ = _unrolled_loop(tc, step, (ds_ref[...], _head_rowsum(dgv * dy_ref[tc - 1])))
        ds_ref[...] = ds_end

    vec = pl.BlockSpec((tc, 1, RW_WIDTH), lambda i: (nt - 1 - i, 0, 0))
    mat = pl.BlockSpec((tc, RW_N, RW_WIDTH), lambda i: (nt - 1 - i, 0, 0))
    vsh = jax.ShapeDtypeStruct((T, 1, RW_WIDTH), F32)
    return pl.pallas_call(
        body, name=name + '_b', grid=(nt,),
        in_specs=[vec] * 5 + [mat] * 3 + [vec, pl.BlockSpec((RW_N, RW_WIDTH), lambda i: (0, 0))],
        out_specs=[vec] * 6, out_shape=[vsh] * 6,
        scratch_shapes=[pltpu.VMEM((RW_N, RW_WIDTH), F32)],
        compiler_params=_cparams(('arbitrary',)),
    )(r, w, k, a, b, st, sa_all, vc_all, dy, dg)


@functools.partial(jax.custom_vjp, nondiff_argnums=(6,))
def wkv7(r, w, k, v, a, b, name):
    return _scan_fwd_call(r, w, k, v, a, b, name)[0]


def _wkv7_fwd(r, w, k, v, a, b, name):
    y, st, sa_all, vc_all = _scan_fwd_call(r, w, k, v, a, b, name)
    return y, (r, w, k, a, b, st, sa_all, vc_all)


def _wkv7_bwd(name, res, dy):
    return tuple(_scan_bwd_call(*res, dy, name))


wkv7.defvjp(_wkv7_fwd, _wkv7_bwd)


def _np_bf16(a):
    return jnp.asarray(a, BF16)


def _mla_consts():
    seg_n = (np.arange(512)[:, None] // QK_NOPE == np.arange(LANES)[None, :]).astype(np.float32)
    seg_r = (np.arange(LANES)[:, None] // 16 == np.arange(LANES)[None, :]).astype(np.float32)
    e1 = np.zeros((LANES, LANES), np.float32)
    e2 = np.zeros((LANES, LANES), np.float32)
    for h in range(MLA_HEADS):
        for i in range(16):
            e1[i, h * 16 + i] = 1.0
            e2[16 + i, h * 16 + i] = 1.0
    mats = [seg_n, seg_n.T, seg_r, seg_r.T, e1, e1.T, e2, e2.T]
    return [_np_bf16(m) for m in mats]


def _qk_prep_fn(qn, q1, q2, kn, kx, cos, sin, gqn, gq1, gq2, gkn, gk1, gk2,
                seg_n, seg_nt, seg_r, seg_rt, e1, e1t, e2, e2t):
    def normrope(xn, x1, x2, gn, g1, g2):
        ss = sdot(xn * xn, seg_n, seg_nt) + sdot(x1 * x1, seg_r, seg_rt) + sdot(x2 * x2, seg_r, seg_rt)
        inv = lax.rsqrt(ss * (1.0 / QK_HEAD) + NORM_EPS)
        inv_n = sdot(inv, seg_nt, seg_n)
        inv_r = sdot(inv, seg_rt, seg_r)
        y1 = x1 * inv_r * g1
        y2 = x2 * inv_r * g2
        return xn * inv_n * gn, y1 * cos - y2 * sin, y1 * sin + y2 * cos

    k1 = sdot(kx, e1, e1t)
    k2 = sdot(kx, e2, e2t)
    return normrope(qn, q1, q2, gqn, gq1, gq2) + normrope(kn, k1, k2, gkn, gk1, gk2)


def _rwkv_prep_fn(vres):
    def fn(r, k, v, xg, xwa, kx, *rest):
        if vres:
            vfirst, w0, a0, k_k, k_a, w2p, a2p, g2, v0, v2p, bm = rest
        else:
            w0, a0, k_k, k_a, w2p, a2p, g2, bm = rest
        z = w0 + bdot(jnp.tanh(xwa), w2p)
        nz = -z
        softplus = jnp.maximum(nz, 0.0) + jnp.log(1.0 + jnp.exp(-jnp.abs(nz)))
        decay = jnp.exp(-jnp.exp(-softplus - 0.5))
        a = _sigmoid(a0 + bdot(xwa, a2p))
        g = bdot(_sigmoid(xg), g2)
        if vres:
            vv = v + (vfirst - v) * _sigmoid(v0 + bdot(kx, v2p))
        else:
            vv = v
        kkr = k * k_k
        kk = kkr / jnp.maximum(jnp.sqrt(sdot(kkr * kkr, bm, bm)), 1e-12)
        k2 = k * (1.0 + (a - 1.0) * k_a)
        return r * 1.0, decay, k2, vv, -kk, kk * a, g
    return fn


def _rwkv_post_fn(y, r, k2, vv, g, ln_w, ln_b, rk, bm):
    inv_n = 1.0 / RW_N
    mean = sdot(y, bm, bm) * inv_n
    yc = y - mean
    var = sdot(yc * yc, bm, bm) * inv_n
    yn = yc * lax.rsqrt(var + GN_EPS) * ln_w + ln_b
    bonus = sdot(r * k2 * rk, bm, bm) * vv
    return ((yn + bonus) * g,)


def _merge_fn(g0, g1, g2, oa, ob, oc):
    return (_sigmoid(g0) * oa + _sigmoid(g1) * ob + _sigmoid(g2) * oc,)


def _loss_call(y, target):
    T, C = y.shape
    tb = min(256, T)

    def body(y_ref, t_ref, dy_ref, part_ref):
        err = y_ref[...] - t_ref[...]
        dy_ref[...] = err * (1.0 / C)
        sq = jnp.sum(err * err, axis=0, keepdims=True)
        acc = sq[:, 0:LANES]
        for j in range(1, C // LANES):
            acc = acc + sq[:, j * LANES:(j + 1) * LANES]
        part_ref[...] = jnp.zeros(part_ref.shape, F32)
        part_ref[0:1, :] = acc * (0.5 / C)

    return pl.pallas_call(
        body, name='loss', grid=(T // tb,),
        in_specs=[pl.BlockSpec((tb, C), lambda i: (i, 0))] * 2,
        out_specs=[pl.BlockSpec((tb, C), lambda i: (i, 0)), pl.BlockSpec((8, LANES), lambda i: (i, 0))],
        out_shape=[jax.ShapeDtypeStruct((T, C), F32), jax.ShapeDtypeStruct((8 * (T // tb), LANES), F32)],
        compiler_params=_cparams(('parallel',)),
    )(y, target)


def _pad_rows(t, before, total):
    return jnp.pad(t, ((before, total - before - t.shape[0]), (0, 0)))


def _head_tile(g, lo, hi):
    return jnp.tile(g[lo:hi], MLA_HEADS).reshape(1, -1)


def _layer(l, x, v_first, wd, sp, cos, sin):
    T = x.shape[0]
    nm = f'l{l}'
    vres = l > 0
    w_in = wd['w_in']
    if vres:
        v1t = wd['rwkv_v1'].map(lambda t: t.T)
    else:
        v1t = W(jnp.zeros((MV_LORA, D_MODEL), BF16), jnp.zeros((MV_LORA, D_MODEL), BF16))
    zpad = W(jnp.zeros((64, D_MODEL), BF16), jnp.zeros((64, D_MODEL), BF16))
    w_rw = _wcat([w_in['rkv'], w_in['xg'], w_in['xwa'], w_in['kpe'], v1t, zpad], 0)
    (gates, cq, ckv, rw, cv), x_skip = norm_projections(
        x, sp['attn_norm'], [w_in['gates'], w_in['cq'], w_in['ckv'], w_rw, w_in['conv']], nm + '_win',
        transposed=True, skip=True)

    v_mu = sp['rwkv_v_mu'] if vres else jnp.zeros((MV_LORA,), F32)
    mu_all = jnp.concatenate([sp['rwkv_mu'][0:768], sp['rwkv_mu'][896:1024], sp['rwkv_mu'][768:896],
                              jnp.zeros((QK_ROPE,), F32), v_mu, jnp.zeros((64,), F32)]).reshape(1, -1)
    rws = token_shift_mix(rw, mu_all, nm + '_shift')

    wq = wd['mla_wq_b'].map(lambda t: jnp.concatenate(
        [t.reshape(Q_LORA, MLA_HEADS, QK_HEAD)[:, :, 0:64].reshape(Q_LORA, 512),
         t.reshape(Q_LORA, MLA_HEADS, QK_HEAD)[:, :, 64:80].reshape(Q_LORA, 128),
         t.reshape(Q_LORA, MLA_HEADS, QK_HEAD)[:, :, 80:96].reshape(Q_LORA, 128)], axis=1))
    wkn = wd['mla_wkv_b'].map(lambda t: t.reshape(KV_LORA, MLA_HEADS, 128)[:, :, 0:64].reshape(KV_LORA, 512))
    wv = wd['mla_wkv_b'].map(lambda t: t.reshape(KV_LORA, MLA_HEADS, 128)[:, :, 64:128].reshape(KV_LORA, 512))
    q, = norm_projections(cq, sp['mla_q_a_norm'], [wq], nm + '_wq')
    kn, vv_att = norm_projections(ckv, sp['mla_kv_a_norm'], [wkn, wv], nm + '_wkv')
    gq, gk = sp['mla_q_norm'], sp['mla_k_norm']
    consts = _mla_consts()
    qk_specs = ([ROW(pieces=((0, 512), (512, 128), (640, 128))), ROW(), ROW(pieces=((1024, 128),)),
                 ROW(False), ROW(False)] + [FULL()] * 6 + [FULL(False)] * 8)
    qk_op = stage_op(_qk_prep_fn, qk_specs, [512, 128, 128, 512, 128, 128], nm + '_qkprep')
    Qn, Q1, Q2, Kn, K1, K2 = qk_op(q, kn, rws, cos, sin,
                                   _head_tile(gq, 0, 64), _head_tile(gq, 64, 80), _head_tile(gq, 80, 96),
                                   _head_tile(gk, 0, 64), _head_tile(gk, 64, 80), _head_tile(gk, 80, 96), *consts)
    o_att = attention(Qn, Q1, Q2, Kn, K1, K2, vv_att, nm + '_att')
    o_a = linear(o_att, wd['mla_w_o'], name=nm + '_wo')

    bm = _np_bf16(_block_ones(RW_WIDTH, RW_N))
    vec = lambda n: sp[n].reshape(1, -1)
    f32w = lambda n: wd[n].c + wd[n].b.astype(F32)
    w2p = _pad_rows(f32w('rwkv_w2'), 0, 128)
    a2p = _pad_rows(f32w('rwkv_a2'), 64, 128)
    g2 = f32w('rwkv_g2')
    rw_pieces = ((0, 256), (256, 256), (512, 256), (768, 128), (896, 128), (1024, 128))
    if vres:
        v2p = _pad_rows(f32w('rwkv_v2'), 32, 128)
        prep_specs = [ROW(pieces=rw_pieces), ROW()] + [FULL()] * 9 + [FULL(False)]
        prep_in = [rws, v_first, vec('rwkv_w0'), vec('rwkv_a0'), vec('rwkv_k_k'), vec('rwkv_k_a'), w2p, a2p, g2,
                   vec('rwkv_v0'), v2p, bm]
    else:
        prep_specs = [ROW(pieces=rw_pieces)] + [FULL()] * 7 + [FULL(False)]
        prep_in = [rws, vec('rwkv_w0'), vec('rwkv_a0'), vec('rwkv_k_k'), vec('rwkv_k_a'), w2p, a2p, g2, bm]
    prep_op = stage_op(_rwkv_prep_fn(vres), prep_specs, [256] * 7, nm + '_rwprep')
    r_, dec, k2, vv, an, bn, g = prep_op(*prep_in)
    if not vres:
        v_first = vv
    t3 = lambda t: t.reshape(T, 1, RW_WIDTH)
    y = wkv7(t3(r_), t3(dec), t3(k2), t3(vv), t3(an), t3(bn), nm + '_scan').reshape(T, RW_WIDTH)
    post_op = stage_op(_rwkv_post_fn, [ROW()] * 5 + [FULL()] * 3 + [FULL(False)], [256], nm + '_rwpost')
    yb = post_op(y, r_, k2, vv, g, vec('rwkv_ln_w'), vec('rwkv_ln_b'), sp['rwkv_r_k'].reshape(1, -1), bm)[0]
    o_b = linear(yb, wd['rwkv_w_o'], name=nm + '_rwo')

    oc_in = short_conv(cv, f32w('conv_w'), nm + '_conv')
    o_c = linear(oc_in, wd['conv_w_o'], name=nm + '_cwo')

    merge_op = stage_op(_merge_fn, [ROW(pieces=((0, 1024), (1024, 1024), (2048, 1024))), ROW(), ROW(), ROW()],
                        [D_MODEL], nm + '_merge')
    merged = merge_op(gates, o_a, o_b, o_c)[0]
    x2 = linear(merged, wd['w_out'], add=x_skip, name=nm + '_wout')
    return x2, v_first


def _mlp(l, x, wd, sp):
    return mlp_block(x, sp['mlp_norm'].reshape(1, -1), wd['w_up'].b, wd['w_up'].c, wd['w_down'].b, wd['w_down'].c,
                     f'l{l}_mlp')


MLP_WEIGHTS = ('w_up', 'w_down')


def _entries(ml):
    out = []
    for name, (layers, shape, axis) in SHARDED.items():
        l = ml if layers == DEPTH else ml - 1
        if not 0 <= l < layers or name in MLP_WEIGHTS:
            continue
        n = shape[0] * shape[1] // N_DEV
        if name == 'conv_w':
            out.append(('conv_w_hi', name, l, n))
            out.append(('conv_w_lo', name, l, n))
        else:
            out.append((name, name, l, n))
    return out


def _slot_size(n):
    return -(-n // LANES) * LANES


def _pack_rows(ml):
    total = sum(_slot_size(n) for _, _, _, n in _entries(ml))
    rows = -(-total // LANES)
    return -(-rows // PACK_ROW_MULT) * PACK_ROW_MULT


def _pack_flat(pieces, ml):
    rows = _pack_rows(ml)
    padded = []
    for p, (_, _, _, n) in zip(pieces, _entries(ml)):
        pad = _slot_size(n) - n
        if pad:
            p = jnp.pad(p, [(0, 0)] * (p.ndim - 1) + [(0, pad)])
        padded.append(p)
    flat = jnp.concatenate(padded, axis=-1)
    tail = rows * LANES - flat.shape[-1]
    if tail:
        flat = jnp.pad(flat, [(0, 0)] * (flat.ndim - 1) + [(0, tail)])
    return flat.reshape(flat.shape[:-1] + (rows, LANES))


def _unpack_flat(buf, ml):
    out, row = [], 0
    for _, _, _, n in _entries(ml):
        nrows = _slot_size(n) // LANES
        piece = buf[..., row:row + nrows, :].reshape(buf.shape[:-2] + (-1,))
        out.append(piece[..., :n])
        row += nrows
    return out


def _pack_shards(shards, ml, dtype, split_conv):
    pieces = []
    for slot, name, l, n in _entries(ml):
        a = shards[name][l]
        if name in TRANSPOSED:
            a = a.astype(dtype).T
        a = a.reshape(-1)
        if slot == 'conv_w_hi':
            a = a.astype(BF16).astype(F32) if split_conv else a
        elif slot == 'conv_w_lo':
            a = (a - a.astype(BF16).astype(F32)) if split_conv else jnp.zeros_like(a)
        pieces.append(a.astype(dtype))
    return _pack_flat(pieces, ml)


def _unpack_shards(buf, ml):
    out = {}
    for (slot, name, l, n), v in zip(_entries(ml), _unpack_flat(buf, ml)):
        if slot == 'conv_w_lo':
            continue
        layers, shape, axis = SHARDED[name]
        sshape = (shape[0] // N_DEV, shape[1]) if axis == 0 else (shape[0], shape[1] // N_DEV)
        out[(name, l)] = v.reshape(sshape[::-1]).T if name in TRANSPOSED else v.reshape(sshape)
    return out


def _to_full(blocks, shape, axis):
    if axis == 0:
        return blocks.reshape(shape)
    return blocks.reshape(N_DEV, shape[0], shape[1] // N_DEV).transpose(1, 0, 2).reshape(shape)


def _to_blocks(full, axis):
    r, c = full.shape
    if axis == 0:
        return full.reshape(N_DEV, -1)
    return full.reshape(r, N_DEV, c // N_DEV).transpose(1, 0, 2).reshape(N_DEV, -1)


def _unpack_gathered(gathered, ml):
    out, conv_hi = {}, None
    for (slot, name, l, n), v in zip(_entries(ml), _unpack_flat(gathered, ml)):
        layers, shape, axis = SHARDED[name]
        if name in TRANSPOSED:
            out[name] = v.reshape(-1, shape[0])
            continue
        full = _to_full(v, shape, axis)
        if slot == 'conv_w_hi':
            conv_hi = full
        elif slot == 'conv_w_lo':
            out[name] = conv_hi.astype(F32) + full.astype(F32)
        else:
            out[name] = full
    return out


def _pack_grads(grads, ml):
    pieces = []
    for slot, name, l, n in _entries(ml):
        if name in TRANSPOSED:
            blocks = jnp.concatenate(grads[name], axis=0).reshape(N_DEV, -1)
        else:
            blocks = _to_blocks(grads[name], SHARDED[name][2])
        if slot == 'conv_w_lo':
            blocks = jnp.zeros_like(blocks)
        pieces.append(blocks.astype(BF16))
    return _pack_flat(pieces, ml)


def _my_pos():
    return lax.axis_index('x'), lax.axis_index('y'), lax.axis_index('c')


def _flip(v, bit):
    return 1 - v if bit else v


def all_gather_blocks(x):
    rows = x.shape[0]

    def body(x_ref, out_ref, send_sems, recv_sems, local_sem):
        mx, my, mc = _my_pos()
        me, sibling = (mx, my, mc), (mx, my, 1 - mc)
        chips = [(1 - mx, my), (mx, 1 - my), (1 - mx, 1 - my)]

        def block(px, py, pc):
            return out_ref.at[4 * px + 2 * py + pc]

        def copy(k, blk, to, src=None):
            return pltpu.make_async_remote_copy(
                src_ref=block(*blk) if src is None else src, dst_ref=block(*blk),
                send_sem=send_sems.at[k], recv_sem=recv_sems.at[k], device_id=to, device_id_type=MESH)

        mine = pltpu.make_async_copy(x_ref, block(*me), local_sem)
        mine.start()
        first = [copy(0, me, sibling, src=x_ref)]
        first += [copy(1 + j, me, (*chip, mc), src=x_ref) for j, chip in enumerate(chips)]
        for cp in first:
            cp.start()
        passed = [copy(4 + j, (*chip, mc), sibling) for j, chip in enumerate(chips)]
        for j, chip in enumerate(chips):
            copy(1 + j, (*chip, mc), me).wait_recv()
            passed[j].start()
        copy(0, sibling, me).wait_recv()
        for j, chip in enumerate(chips):
            copy(4 + j, (*chip, 1 - mc), me).wait_recv()
        for cp in first + passed:
            cp.wait_send()
        mine.wait()

    return pl.pallas_call(
        body, name='all_gather_weights',
        out_shape=jax.ShapeDtypeStruct((N_DEV, rows, LANES), x.dtype),
        in_specs=[pl.BlockSpec(memory_space=pl.ANY)], out_specs=pl.BlockSpec(memory_space=pl.ANY),
        scratch_shapes=[pltpu.SemaphoreType.DMA((7,)), pltpu.SemaphoreType.DMA((7,)), pltpu.SemaphoreType.DMA],
    )(x)


HBM_SPEC = pl.BlockSpec(memory_space=pltpu.HBM)
SEM_SPEC = pl.BlockSpec(memory_space=pltpu.SEMAPHORE)
DATAFLOW_EFFECT = pltpu.SideEffectType.DATAFLOW_SIDE_EFFECTING


def _direct_copies(src_ref, land_ref, send_sems, recv_sems, per_peer):
    mx, my, mc = _my_pos()
    me = 4 * mx + 2 * my + mc
    copies = []
    for k in range(1, N_DEV):
        peer = (_flip(mx, k & 4), _flip(my, k & 2), _flip(mc, k & 1))
        pidx = 4 * peer[0] + 2 * peer[1] + peer[2]
        copies.append(pltpu.make_async_remote_copy(
            src_ref=src_ref.at[pidx] if per_peer else src_ref, dst_ref=land_ref.at[me],
            send_sem=send_sems.at[k - 1], recv_sem=recv_sems.at[k - 1], device_id=peer, device_id_type=MESH))
    return copies


def send_start(src, per_peer, name):
    block = src.shape[1:] if per_peer else src.shape
    land_shape = (N_DEV,) + tuple(block)

    def body(src_ref, land_ref, send_sems, recv_sems, src_thru, land_thru, token):
        for cp in _direct_copies(src_ref, land_ref, send_sems, recv_sems, per_peer):
            cp.start()
        token[...] = jnp.zeros(token.shape, F32)

    send_sems, recv_sems, src_thru, land_thru, token = pl.pallas_call(
        body, name=name,
        out_shape=(pltpu.SemaphoreType.DMA((N_DEV - 1,)), pltpu.SemaphoreType.DMA((N_DEV - 1,)),
                   pltpu.HBM(src.shape, src.dtype), pltpu.HBM(land_shape, src.dtype),
                   jax.ShapeDtypeStruct((8, LANES), F32)),
        in_specs=(HBM_SPEC, HBM_SPEC),
        out_specs=(SEM_SPEC, SEM_SPEC, HBM_SPEC, HBM_SPEC, pl.BlockSpec(memory_space=pltpu.VMEM)),
        input_output_aliases={0: 2, 1: 3},
        compiler_params=pltpu.CompilerParams(has_side_effects=DATAFLOW_EFFECT),
    )(pltpu.with_memory_space_constraint(src, pltpu.HBM),
      pltpu.with_memory_space_constraint(lax.empty(land_shape, src.dtype), pltpu.HBM))
    return (send_sems, recv_sems, src_thru, land_thru), token[0, 0]


def send_wait(handles, after, per_peer, name):
    send_sems, recv_sems, src_thru, land_thru = handles

    def body(src_ref, land_ref, send_sems, recv_sems, after_ref, src_dead, got_ref):
        for cp in _direct_copies(src_ref, land_ref, send_sems, recv_sems, per_peer):
            cp.wait_send()
            cp.wait_recv()

    return pl.pallas_call(
        body, name=name,
        out_shape=(pltpu.HBM(src_thru.shape, src_thru.dtype), pltpu.HBM(land_thru.shape, land_thru.dtype)),
        in_specs=(HBM_SPEC, HBM_SPEC, SEM_SPEC, SEM_SPEC, pl.BlockSpec(memory_space=pl.ANY)),
        out_specs=(HBM_SPEC, HBM_SPEC), input_output_aliases={0: 0, 1: 1},
        compiler_params=pltpu.CompilerParams(has_side_effects=DATAFLOW_EFFECT),
    )(src_thru, land_thru, send_sems, recv_sems, after)[1]


def _adamw_math(w, g, m, v):
    m2 = ADAM_B1 * m + (1.0 - ADAM_B1) * g
    v2 = ADAM_B2 * v + (1.0 - ADAM_B2) * (g * g)
    m_hat = m2 / (1.0 - ADAM_B1 ** ADAM_STEP)
    v_hat = v2 / (1.0 - ADAM_B2 ** ADAM_STEP)
    delta = -ADAM_LR * (m_hat / (jnp.sqrt(v_hat) + ADAM_EPS) + ADAM_WD * w)
    return delta, m2, v2


def sum_parts(parts, mine, name):
    _, rows, cols = parts.shape
    rb = rows
    while N_DEV * rb * cols * 2 > (2 << 20) and rb % 32 == 0:
        rb //= 2

    def body(p_ref, mine_ref, g_ref):
        mx, my, mc = _my_pos()
        me = 4 * mx + 2 * my + mc
        own = mine_ref[0].astype(F32)
        g = jnp.where(me == 0, own, p_ref[0].astype(F32))
        for j in range(1, N_DEV):
            g = g + jnp.where(me == j, own, p_ref[j].astype(F32))
        g_ref[...] = g

    return pl.pallas_call(
        body, name=name, grid=(rows // rb,),
        in_specs=[pl.BlockSpec((N_DEV, rb, cols), lambda i: (0, i, 0)), pl.BlockSpec((1, rb, cols), lambda i: (0, i, 0))],
        out_specs=pl.BlockSpec((rb, cols), lambda i: (i, 0)),
        out_shape=jax.ShapeDtypeStruct((rows, cols), F32), compiler_params=_cparams(('parallel',)),
    )(parts, mine)


ADAMW_BLOCK_BYTES = 1 << 20


def adamw_weight(name, w, m, v, grads):
    layers, a, b = w.shape
    ra = a
    while ra * b * 4 > ADAMW_BLOCK_BYTES and ra % 16 == 0:
        ra //= 2

    def body(*refs):
        w_ref, m_ref, v_ref = refs[:3]
        g_refs = refs[3:3 + layers]
        g_ref, d_ref, m2_ref, v2_ref = refs[3 + layers:]
        g = g_refs[0][...]
        for l in range(1, layers):
            g = jnp.where(pl.program_id(0) == l, g_refs[l][...], g)
        delta, m2, v2 = _adamw_math(w_ref[0], g, m_ref[0], v_ref[0])
        g_ref[0] = g
        d_ref[0] = delta
        m2_ref[0] = m2
        v2_ref[0] = v2

    blk = pl.BlockSpec((1, ra, b), lambda l, i: (l, i, 0))
    gblk = pl.BlockSpec((ra, b), lambda l, i: (i, 0))
    sh = jax.ShapeDtypeStruct(w.shape, F32)
    return pl.pallas_call(
        body, name='adamw_' + name, grid=(layers, a // ra), in_specs=[blk] * 3 + [gblk] * layers,
        out_specs=[blk] * 4, out_shape=[sh] * 4, compiler_params=_cparams(('parallel', 'parallel')),
    )(w, m, v, *grads)


def allreduce_adamw_small(g, w, m, v):
    rows = g.shape[0]

    def body(g_ref, w_ref, m_ref, v_ref, gs_ref, d_ref, m2_ref, v2_ref, all_ref, send_sems, recv_sems):
        mx, my, mc = _my_pos()
        me, sibling = (mx, my, mc), (mx, my, 1 - mc)
        chips = [(1 - mx, my), (mx, 1 - my), (1 - mx, 1 - my)]

        def block(px, py, pc):
            return all_ref.at[4 * px + 2 * py + pc]

        def copy(k, blk, to, src=None):
            return pltpu.make_async_remote_copy(
                src_ref=block(*blk) if src is None else src, dst_ref=block(*blk),
                send_sem=send_sems.at[k], recv_sem=recv_sems.at[k], device_id=to, device_id_type=MESH)

        first = [copy(0, me, sibling, src=g_ref)]
        first += [copy(1 + j, me, (*chip, mc), src=g_ref) for j, chip in enumerate(chips)]
        for cp in first:
            cp.start()
        passed = [copy(4 + j, (*chip, mc), sibling) for j, chip in enumerate(chips)]
        for j, chip in enumerate(chips):
            copy(1 + j, (*chip, mc), me).wait_recv()
            passed[j].start()
        copy(0, sibling, me).wait_recv()
        for j, chip in enumerate(chips):
            copy(4 + j, (*chip, 1 - mc), me).wait_recv()
        for cp in first + passed:
            cp.wait_send()
        my_idx = 4 * mx + 2 * my + mc
        total = jnp.zeros((rows, LANES), F32)
        for j in range(N_DEV):
            total = total + jnp.where(my_idx == j, g_ref[...], all_ref[j])
        delta, m2, v2 = _adamw_math(w_ref[...], total, m_ref[...], v_ref[...])
        gs_ref[...] = total
        d_ref[...] = delta
        m2_ref[...] = m2
        v2_ref[...] = v2

    vm = pl.BlockSpec(memory_space=pltpu.VMEM)
    sh = jax.ShapeDtypeStruct((rows, LANES), F32)
    return pl.pallas_call(
        body, name='allreduce_adamw_small', in_specs=[vm] * 4, out_specs=[vm] * 4, out_shape=[sh] * 4,
        scratch_shapes=[pltpu.VMEM((N_DEV, rows, LANES), F32), pltpu.SemaphoreType.DMA((7,)),
                        pltpu.SemaphoreType.DMA((7,))],
    )(g, w, m, v)


SMALL_COUNT = 11680


def _small_pack(d, extra=None):
    assert sum(d[n].size for n in SMALL_NAMES) == SMALL_COUNT
    flat = jnp.concatenate([d[n].reshape(-1) for n in SMALL_NAMES] + ([] if extra is None else [extra.reshape(1)]))
    rows = -(-(SMALL_COUNT + 1) // (8 * LANES)) * 8
    return jnp.pad(flat, (0, rows * LANES - flat.shape[0])).reshape(rows, LANES)


def _small_unpack(buf, like):
    flat = buf.reshape(-1)
    out, off = {}, 0
    for n in SMALL_NAMES:
        sz = int(np.prod(like[n].shape))
        out[n] = flat[off:off + sz].reshape(like[n].shape)
        off += sz
    return out


def _rope_tables(positions):
    freqs = ROPE_THETA ** (-(jnp.arange(QK_ROPE // 2, dtype=F32) * 2.0 / QK_ROPE))
    ang = positions.astype(F32)[:, None] * freqs
    return jnp.tile(jnp.cos(ang), (1, MLA_HEADS)), jnp.tile(jnp.sin(ang), (1, MLA_HEADS))


def _layer_weights(gathered, carriers):
    wd = {}
    for name, full in gathered.items():
        if name == 'w_in':
            wd[name] = {seg: W(full[lo:hi], c) for (seg, lo, hi), c in zip(WIN_SEGS, carriers[name])}
        else:
            wd[name] = W(full, carriers[name])
    return wd


F32_GRAD_WEIGHTS = ('rwkv_w2', 'rwkv_a2', 'rwkv_g2', 'rwkv_v2', 'conv_w')


def _make_carriers(gathered):
    gathered, carriers = dict(gathered), {}
    for name, full in gathered.items():
        if name == 'w_in':
            carriers[name] = tuple(jnp.zeros((hi - lo, D_MODEL), BF16) for _, lo, hi in WIN_SEGS)
        elif name == 'conv_w':
            gathered[name] = full.astype(BF16)
            carriers[name] = full - full.astype(BF16).astype(F32)
        else:
            carriers[name] = jnp.zeros(full.shape, F32 if name in F32_GRAD_WEIGHTS else BF16)
    return gathered, carriers


def _layer_small(small, ml):
    out = {}
    for n in SMALL_NAMES:
        l = ml if small[n].shape[0] == DEPTH else ml - 1
        if 0 <= l < small[n].shape[0]:
            out[n] = small[n][l]
    return out


def _after(value, tokens):
    return value + sum(tokens[1:], tokens[0])


def _train_step(x, positions, loss_target, weights, moms_m, moms_v):
    shards = {n: weights[n] for n in SHARDED}
    small = {n: weights[n] for n in SMALL_NAMES}
    me = 4 * lax.axis_index('x') + 2 * lax.axis_index('y') + lax.axis_index('c')
    cos, sin = _rope_tables(positions[0])

    units = [(kind, l) for l in range(DEPTH) for kind in ('pack',) + MLP_WEIGHTS]
    first, later = units[0], units[1:]
    tag = lambda u: f'{u[0]}_l{u[1]}'

    def own_block(u):
        kind, l = u
        return _pack_shards(shards, l, BF16, True) if kind == 'pack' else shards[kind][l].astype(BF16)

    own = {u: own_block(u) for u in units}
    held = lax.optimization_barrier((all_gather_blocks(own[first]), *[own[u] for u in later]))
    blocks, gathers, tokens = {first: held[0]}, {}, []
    for u, mine in zip(later, held[1:]):
        own[u] = mine
        gathers[u], token = send_start(mine, False, f'gather_{tag(u)}_start')
        tokens.append(token)
    x0 = _after(x[0], tokens)

    def gathered_block(u, after):
        if u not in blocks:
            landed = send_wait(gathers[u], after, False, f'gather_{tag(u)}_wait')
            blocks[u] = lax.dynamic_update_slice(landed, own[u][None], (me,) + (0,) * own[u].ndim)
        return blocks[u]

    def mixer_weights(l, after):
        return _make_carriers(_unpack_gathered(gathered_block(('pack', l), after), l))

    def mlp_weights(l, after):
        up = gathered_block(('w_up', l), after)
        down = gathered_block(('w_down', l), after).reshape(D_FF, D_MODEL)
        return _make_carriers({'w_up': up, 'w_down': down})

    def small_of(layer, part):
        sp = _layer_small(small, layer)
        return {n: v for n, v in sp.items() if (n == 'mlp_norm') == (part == 'mlp')}

    def mixer(layer, gathered):
        if layer == 0:
            return lambda c, s, xx: _layer(0, xx, None, _layer_weights(gathered, c), s, cos, sin)
        return lambda c, s, xx, vf: _layer(layer, xx, vf, _layer_weights(gathered, c), s, cos, sin)[0]

    def mlp(layer, gathered):
        return lambda c, s, xx: _mlp(layer, xx, _layer_weights(gathered, c), s)

    gathered, carriers = mixer_weights(0, None)
    (h, v_first), vjp_mix0 = jax.vjp(mixer(0, gathered), carriers, small_of(0, 'mix'), x0)
    gathered, carriers = mlp_weights(0, h)
    h, vjp_mlp0 = jax.vjp(mlp(0, gathered), carriers, small_of(0, 'mlp'), h)
    gathered, carriers = mixer_weights(1, h)
    h, vjp_mix1 = jax.vjp(mixer(1, gathered), carriers, small_of(1, 'mix'), h, v_first)
    gathered, carriers = mlp_weights(1, h)
    y, vjp_mlp1 = jax.vjp(mlp(1, gathered), carriers, small_of(1, 'mlp'), h)

    dy, loss_parts = _loss_call(y, loss_target[0])

    shipped = {}

    def ship(u, to_send):
        handles, token = send_start(to_send, True, f'grads_{tag(u)}_start')
        shipped[u] = (handles, to_send)
        return token

    def ship_mlp(l, gw):
        down = gw['w_down'].reshape(N_DEV, D_FF // N_DEV, D_MODEL)
        return [ship(('w_up', l), gw['w_up']), ship(('w_down', l), down)]

    gw, gs_mlp1, d = vjp_mlp1(dy)
    d = _after(d, ship_mlp(1, gw))
    gw, gs_mix1, d, dvf = vjp_mix1(d)
    d = _after(d, [ship(('pack', 1), _pack_grads(gw, 1))])
    gw, gs_mlp0, d = vjp_mlp0(d)
    d = _after(d, ship_mlp(0, gw))
    gw, gs_mix0, g_x = vjp_mix0((d, dvf))
    last_token = ship(first, _pack_grads(gw, 0))
    gs0, gs1 = {**gs_mix0, **gs_mlp0}, {**gs_mix1, **gs_mlp1}

    def arrived(u, after):
        handles, sent = shipped[u]
        landed = send_wait(handles, after, True, f'grads_{tag(u)}_wait')
        return sum_parts(landed, lax.dynamic_slice_in_dim(sent, me, 1, axis=0), f'sum_grads_{tag(u)}')

    after = _after(g_x, [last_token])
    grads = {u: arrived(u, after) for u in later if u[0] != 'pack'}
    grads.update(_unpack_shards(arrived(('pack', 1), after), 1))
    sharded_out = [{}, {}, {}, {}]

    def update(n):
        outs = adamw_weight(n, weights[n], moms_m[n], moms_v[n], [grads[(n, l)] for l in range(SHARDED[n][0])])
        for i in range(4):
            sharded_out[i][n] = outs[i]

    for n in MLP_WEIGHTS:
        update(n)

    g_small = {}
    for n in SMALL_NAMES:
        per = [g[n] for g in (gs0, gs1) if n in g]
        g_small[n] = jnp.stack(per)
    small_grads = _after(_small_pack(g_small, jnp.sum(loss_parts)), [sharded_out[1][MLP_WEIGHTS[-1]][0, 0, 0] * 0.0])
    small_bufs = allreduce_adamw_small(
        small_grads, _small_pack(small), _small_pack({n: moms_m[n] for n in SMALL_NAMES}),
        _small_pack({n: moms_v[n] for n in SMALL_NAMES}))
    loss = small_bufs[0].reshape(-1)[SMALL_COUNT]
    small_out = [_small_unpack(b, small) for b in small_bufs]

    grads.update(_unpack_shards(arrived(first, small_bufs[0]), 0))
    for n in SHARDED:
        if n not in MLP_WEIGHTS:
            update(n)

    pick = lambda i: [sharded_out[i][n] if n in SHARDED else small_out[i][n] for n in WEIGHT_NAMES]
    return (loss, g_x[None], *pick(0), *pick(1), *pick(2), *pick(3))


def kernel(x, positions, attn_norm, w_in, mla_q_a_norm, mla_wq_b, mla_kv_a_norm, mla_wkv_b, mla_q_norm, mla_k_norm, mla_w_o, rwkv_mu, rwkv_w0, rwkv_w2, rwkv_a0, rwkv_a2, rwkv_g2, rwkv_k_k, rwkv_k_a, rwkv_r_k, rwkv_ln_w, rwkv_ln_b, rwkv_w_o, rwkv_v1, rwkv_v_mu, rwkv_v0, rwkv_v2, conv_w, conv_w_o, w_out, mlp_norm, w_up, w_down, loss_target, m_attn_norm, m_w_in, m_mla_q_a_norm, m_mla_wq_b, m_mla_kv_a_norm, m_mla_wkv_b, m_mla_q_norm, m_mla_k_norm, m_mla_w_o, m_rwkv_mu, m_rwkv_w0, m_rwkv_w2, m_rwkv_a0, m_rwkv_a2, m_rwkv_g2, m_rwkv_k_k, m_rwkv_k_a, m_rwkv_r_k, m_rwkv_ln_w, m_rwkv_ln_b, m_rwkv_w_o, m_rwkv_v1, m_rwkv_v_mu, m_rwkv_v0, m_rwkv_v2, m_conv_w, m_conv_w_o, m_w_out, m_mlp_norm, m_w_up, m_w_down, v_attn_norm, v_w_in, v_mla_q_a_norm, v_mla_wq_b, v_mla_kv_a_norm, v_mla_wkv_b, v_mla_q_norm, v_mla_k_norm, v_mla_w_o, v_rwkv_mu, v_rwkv_w0, v_rwkv_w2, v_rwkv_a0, v_rwkv_a2, v_rwkv_g2, v_rwkv_k_k, v_rwkv_k_a, v_rwkv_r_k, v_rwkv_ln_w, v_rwkv_ln_b, v_rwkv_w_o, v_rwkv_v1, v_rwkv_v_mu, v_rwkv_v0, v_rwkv_v2, v_conv_w, v_conv_w_o, v_w_out, v_mlp_norm, v_w_up, v_w_down):
    args = locals()
    weights = {n: args[n] for n in WEIGHT_NAMES}
    moms_m = {n: args['m_' + n] for n in WEIGHT_NAMES}
    moms_v = {n: args['v_' + n] for n in WEIGHT_NAMES}
    return _train_step(x, positions, loss_target, weights, moms_m, moms_v)
```

```python
import functools

import numpy as np
import jax
import jax.numpy as jnp
from jax import lax
from jax.experimental import pallas as pl
from jax.experimental.pallas import tpu as pltpu

F32 = jnp.float32
BF16 = jnp.bfloat16

N_DEV = 8
LANES = 128
D_MODEL = 1024
DEPTH = 2
MLA_HEADS = 8
QK_NOPE = 64
QK_ROPE = 32
QK_HEAD = QK_NOPE + QK_ROPE
V_HEAD = 64
Q_LORA = 384
KV_LORA = 256
ROPE_THETA = 10000.0
RW_HEADS = 4
RW_N = 64
RW_WIDTH = RW_HEADS * RW_N
MV_LORA = 32
GN_EPS = 64e-5
CONV_WIDTH = 256
D_FF = 4 * D_MODEL
NORM_EPS = 1e-6
ADAM_LR = 0.001
ADAM_B1 = 0.9
ADAM_B2 = 0.999
ADAM_EPS = 1e-08
ADAM_WD = 0.01
ADAM_STEP = 10

VMEM_LIMIT = 56 * 1024 * 1024
MESH = pl.DeviceIdType.MESH

WEIGHT_NAMES = ['attn_norm', 'w_in', 'mla_q_a_norm', 'mla_wq_b', 'mla_kv_a_norm', 'mla_wkv_b', 'mla_q_norm',
                'mla_k_norm', 'mla_w_o', 'rwkv_mu', 'rwkv_w0', 'rwkv_w2', 'rwkv_a0', 'rwkv_a2', 'rwkv_g2',
                'rwkv_k_k', 'rwkv_k_a', 'rwkv_r_k', 'rwkv_ln_w', 'rwkv_ln_b', 'rwkv_w_o', 'rwkv_v1',
                'rwkv_v_mu', 'rwkv_v0', 'rwkv_v2', 'conv_w', 'conv_w_o', 'w_out', 'mlp_norm', 'w_up', 'w_down']

SHARDED = {
    'w_in': (2, (1024, 5536), 1), 'mla_wq_b': (2, (384, 768), 1), 'mla_wkv_b': (2, (256, 1024), 1),
    'mla_w_o': (2, (512, 1024), 1), 'rwkv_w2': (2, (64, 256), 1), 'rwkv_a2': (2, (64, 256), 1),
    'rwkv_g2': (2, (128, 256), 1), 'rwkv_w_o': (2, (256, 1024), 1), 'conv_w': (2, (3, 256), 1),
    'conv_w_o': (2, (256, 1024), 1), 'w_out': (2, (1024, 1024), 0), 'w_up': (2, (1024, 4096), 1),
    'w_down': (2, (4096, 1024), 0), 'rwkv_v1': (1, (1024, 32), 0), 'rwkv_v2': (1, (32, 256), 1),
}
SMALL_NAMES = [n for n in WEIGHT_NAMES if n not in SHARDED]
TRANSPOSED = ('w_in',)
WIN_SEGS = (('gates', 0, 3072), ('cq', 3072, 3456), ('ckv', 3456, 3712), ('kpe', 3712, 3744), ('rkv', 3744, 4512),
            ('xwa', 4512, 4640), ('xg', 4640, 4768), ('conv', 4768, 5536))
PACK_ROW_MULT = 512


def _cparams(sem=None, **kw):
    if sem is not None:
        kw['dimension_semantics'] = sem
    return pltpu.CompilerParams(vmem_limit_bytes=VMEM_LIMIT, **kw)


def _pick(n, cands):
    for c in cands:
        if n % c == 0:
            return c
    raise ValueError(f'no tile for {n}')


def _mm_nn(a, b, add=None, name='mm_nn'):
    M, K = a.shape
    N = b.shape[1]
    tm = _pick(M, (1024, 512, 256, 128))
    tn = _pick(N, (512, 384, 256, 128))
    tk = _pick(K, (1024, 512, 384, 256, 128))
    nk = K // tk
    has_add = add is not None

    def body(*refs):
        if has_add:
            a_ref, b_ref, add_ref, o_ref, acc_ref = refs
        else:
            a_ref, b_ref, o_ref, acc_ref = refs
        kk = pl.program_id(2)
        part = jnp.dot(a_ref[...].astype(BF16), b_ref[...].astype(BF16), preferred_element_type=F32)

        @pl.when(kk == 0)
        def _():
            acc_ref[...] = part

        @pl.when(kk > 0)
        def _():
            acc_ref[...] += part

        @pl.when(kk == nk - 1)
        def _():
            if has_add:
                o_ref[...] = acc_ref[...] + add_ref[...]
            else:
                o_ref[...] = acc_ref[...]

    in_specs = [pl.BlockSpec((tm, tk), lambda i, j, k: (i, k)), pl.BlockSpec((tk, tn), lambda i, j, k: (k, j))]
    args = [a, b]
    if has_add:
        in_specs.append(pl.BlockSpec((tm, tn), lambda i, j, k: (i, j)))
        args.append(add)
    return pl.pallas_call(
        body, name=name, grid=(M // tm, N // tn, nk), in_specs=in_specs,
        out_specs=pl.BlockSpec((tm, tn), lambda i, j, k: (i, j)),
        out_shape=jax.ShapeDtypeStruct((M, N), F32),
        scratch_shapes=[pltpu.VMEM((tm, tn), F32)],
        compiler_params=_cparams(('parallel', 'parallel', 'arbitrary')),
    )(*args)


def _mm_nt(a, b, add=None, name='mm_nt'):
    M, N = a.shape
    blocked = b.ndim == 3
    K = b.shape[-2]
    tm = _pick(M, (1024, 512, 256, 128))
    tk = _pick(K, (512, 384, 256, 128))
    tn = N // N_DEV if blocked else _pick(N, (1024, 512, 384, 256, 128))
    nn = N // tn
    has_add = add is not None

    def body(*refs):
        if has_add:
            a_ref, b_ref, add_ref, o_ref, acc_ref = refs
        else:
            a_ref, b_ref, o_ref, acc_ref = refs
        kk = pl.program_id(2)
        part = lax.dot_general(a_ref[...].astype(BF16), b_ref[...].astype(BF16), (((1,), (1,)), ((), ())),
                               preferred_element_type=F32)

        @pl.when(kk == 0)
        def _():
            acc_ref[...] = part

        @pl.when(kk > 0)
        def _():
            acc_ref[...] += part

        @pl.when(kk == nn - 1)
        def _():
            if has_add:
                o_ref[...] = acc_ref[...] + add_ref[...]
            else:
                o_ref[...] = acc_ref[...]

    if blocked:
        b_spec = pl.BlockSpec((None, tk, tn), lambda i, j, k: (k, j, 0))
    else:
        b_spec = pl.BlockSpec((tk, tn), lambda i, j, k: (j, k))
    in_specs = [pl.BlockSpec((tm, tn), lambda i, j, k: (i, k)), b_spec]
    args = [a, b]
    if has_add:
        in_specs.append(pl.BlockSpec((tm, tk), lambda i, j, k: (i, j)))
        args.append(add)
    return pl.pallas_call(
        body, name=name, grid=(M // tm, K // tk, nn), in_specs=in_specs,
        out_specs=pl.BlockSpec((tm, tk), lambda i, j, k: (i, j)),
        out_shape=jax.ShapeDtypeStruct((M, K), F32),
        scratch_shapes=[pltpu.VMEM((tm, tk), F32)],
        compiler_params=_cparams(('parallel', 'parallel', 'arbitrary')),
    )(*args)


def _mm_tn(a, b, name='mm_tn', blocked=False):
    M, K = a.shape
    N = b.shape[1]
    tm = _pick(M, (1024, 512, 256, 128))
    tk = _pick(K, (512, 384, 256, 128))
    tn = N // N_DEV if blocked else _pick(N, (512, 384, 256, 128))
    nm = M // tm

    def body(a_ref, b_ref, o_ref, acc_ref):
        mm = pl.program_id(2)
        part = lax.dot_general(a_ref[...].astype(BF16), b_ref[...].astype(BF16), (((0,), (0,)), ((), ())),
                               preferred_element_type=F32)

        @pl.when(mm == 0)
        def _():
            acc_ref[...] = part

        @pl.when(mm > 0)
        def _():
            acc_ref[...] += part

        @pl.when(mm == nm - 1)
        def _():
            o_ref[...] = acc_ref[...].astype(BF16)

    if blocked:
        out_spec = pl.BlockSpec((None, tk, tn), lambda i, j, m: (j, i, 0))
        out_shape = jax.ShapeDtypeStruct((N_DEV, K, tn), BF16)
    else:
        out_spec = pl.BlockSpec((tk, tn), lambda i, j, m: (i, j))
        out_shape = jax.ShapeDtypeStruct((K, N), BF16)
    return pl.pallas_call(
        body, name=name, grid=(K // tk, N // tn, nm),
        in_specs=[pl.BlockSpec((tm, tk), lambda i, j, m: (m, i)), pl.BlockSpec((tm, tn), lambda i, j, m: (m, j))],
        out_specs=out_spec, out_shape=out_shape,
        scratch_shapes=[pltpu.VMEM((tk, tn), F32)],
        compiler_params=_cparams(('parallel', 'parallel', 'arbitrary')),
    )(a, b)


@functools.partial(jax.custom_vjp, nondiff_argnums=(4,))
def _linear_add(a, wb, wc, add, name):
    return _mm_nn(a, wb, add, name=name + '_f')


def _linear_add_fwd(a, wb, wc, add, name):
    return _mm_nn(a, wb, add, name=name + '_f'), (a, wb)


def _linear_add_bwd(name, res, dy):
    a, wb = res
    return _mm_nt(dy, wb, name=name + '_da'), None, _mm_tn(a, dy, name=name + '_dw'), dy


_linear_add.defvjp(_linear_add_fwd, _linear_add_bwd)


@functools.partial(jax.custom_vjp, nondiff_argnums=(3,))
def _multi_linear(a, wbs, wcs, name):
    return tuple(_mm_nn(a, wb, name=f'{name}_f{i}') for i, wb in enumerate(wbs))


def _multi_linear_fwd(a, wbs, wcs, name):
    return _multi_linear(a, wbs, wcs, name), (a, wbs)


def _multi_linear_bwd(name, res, dys):
    a, wbs = res
    da = None
    for i, (dy, wb) in enumerate(zip(dys, wbs)):
        da = _mm_nt(dy, wb, add=da, name=f'{name}_da{i}')
    dws = tuple(_mm_tn(a, dy, name=f'{name}_dw{i}') for i, dy in enumerate(dys))
    return da, None, dws


_multi_linear.defvjp(_multi_linear_fwd, _multi_linear_bwd)


class W:
    def __init__(self, b, c):
        self.b, self.c = b, c

    def map(self, fn):
        return W(fn(self.b), fn(self.c))


def _wcat(ws, axis):
    return W(jnp.concatenate([w.b for w in ws], axis), jnp.concatenate([w.c for w in ws], axis))


def linear(a, w, add=None, name='lin'):
    if add is None:
        return _multi_linear(a, (w.b,), (w.c,), name)[0]
    return _linear_add(a, w.b, w.c, add, name)


def ROW(diff=True, pieces=None):
    return ('row', diff, pieces)


def FULL(diff=True):
    return ('full', diff, None)


def _load_args(refs, specs):
    args, amap = [], []
    for i, (ref, (kind, diff, pieces)) in enumerate(zip(refs, specs)):
        if pieces is None:
            args.append(ref[...])
            amap.append((i, None))
        else:
            for (s, w) in pieces:
                args.append(ref[:, s:s + w])
                amap.append((i, (s, w)))
    return args, amap


def _stage_in_specs(ins, specs, tb):
    out = []
    for a, (kind, _, _) in zip(ins, specs):
        if kind == 'row':
            out.append(pl.BlockSpec((tb, a.shape[1]), lambda i: (i, 0)))
        else:
            out.append(pl.BlockSpec(a.shape, lambda i: (0, 0)))
    return out


def _stage_fwd(fn, ins, specs, out_widths, name, tb, out_dtype=F32):
    T = [a for a, s in zip(ins, specs) if s[0] == 'row'][0].shape[0]
    tb = min(tb, T)
    n_in = len(ins)

    def body(*refs):
        args, _ = _load_args(refs[:n_in], specs)
        outs = fn(*args)
        for o_ref, o in zip(refs[n_in:], outs):
            o_ref[...] = o.astype(out_dtype)

    return pl.pallas_call(
        body, name=name + '_f', grid=(T // tb,), in_specs=_stage_in_specs(ins, specs, tb),
        out_specs=[pl.BlockSpec((tb, w), lambda i: (i, 0)) for w in out_widths],
        out_shape=[jax.ShapeDtypeStruct((T, w), out_dtype) for w in out_widths],
        compiler_params=_cparams(('parallel',)),
    )(*ins)


def _stage_bwd(fn, ins, specs, out_widths, douts, name, tb, add_to_first=None):
    T = [a for a, s in zip(ins, specs) if s[0] == 'row'][0].shape[0]
    tb = min(tb, T)
    n_in, n_out = len(ins), len(out_widths)
    diff_inputs = [i for i, s in enumerate(specs) if s[1]]
    n_add = 0 if add_to_first is None else 1

    def body(*refs):
        in_refs, dout_refs = refs[:n_in], refs[n_in:n_in + n_out]
        g_refs = refs[n_in + n_out + n_add:]
        args, amap = _load_args(in_refs, specs)
        didx = [j for j, (i, _) in enumerate(amap) if specs[i][1]]

        def f(*dv):
            full = list(args)
            for j, v in zip(didx, dv):
                full[j] = v
            return tuple(fn(*full))

        _, vjp = jax.vjp(f, *[args[j] for j in didx])
        gs = vjp(tuple(d[...] for d in dout_refs))
        gmap = {j: g for j, g in zip(didx, gs)}
        first = pl.program_id(0) == 0
        for g_ref, i in zip(g_refs, diff_inputs):
            kind, _, pieces = specs[i]
            js = [j for j, (ii, _) in enumerate(amap) if ii == i]
            if kind == 'row':
                if pieces is None:
                    if n_add and i == diff_inputs[0]:
                        g_ref[...] = gmap[js[0]] + refs[n_in + n_out][...]
                    else:
                        g_ref[...] = gmap[js[0]]
                else:
                    if sum(w for _, w in pieces) != ins[i].shape[1]:
                        g_ref[...] = jnp.zeros(g_ref.shape, F32)
                    for j in js:
                        s, w = amap[j][1]
                        g_ref[:, s:s + w] = gmap[j]
            else:
                @pl.when(first)
                def _(g_ref=g_ref):
                    g_ref[...] = jnp.zeros(g_ref.shape, F32)

                g_ref[...] += gmap[js[0]]

    in_specs = _stage_in_specs(ins, specs, tb) + [pl.BlockSpec((tb, w), lambda i: (i, 0)) for w in out_widths]
    extra = []
    if n_add:
        in_specs.append(pl.BlockSpec((tb, add_to_first.shape[1]), lambda i: (i, 0)))
        extra.append(add_to_first)
    out_specs, out_shape = [], []
    for i in diff_inputs:
        a = ins[i]
        if specs[i][0] == 'row':
            out_specs.append(pl.BlockSpec((tb, a.shape[1]), lambda i: (i, 0)))
        else:
            out_specs.append(pl.BlockSpec(a.shape, lambda i: (0, 0)))
        out_shape.append(jax.ShapeDtypeStruct(a.shape, F32))
    return pl.pallas_call(
        body, name=name + '_b', grid=(T // tb,), in_specs=in_specs, out_specs=out_specs, out_shape=out_shape,
        compiler_params=_cparams(('arbitrary',)),
    )(*ins, *douts, *extra)


def stage_op(fn, specs, out_widths, name, tb=256):
    n = len(specs)
    diff_inputs = [i for i, s in enumerate(specs) if s[1]]

    @jax.custom_vjp
    def op(*ins):
        return tuple(_stage_fwd(fn, ins, specs, out_widths, name, tb))

    def op_fwd(*ins):
        return op(*ins), ins

    def op_bwd(ins, douts):
        gs = _stage_bwd(fn, ins, specs, out_widths, douts, name, tb)
        res = [None] * n
        for i, g in zip(diff_inputs, gs):
            res[i] = g
        return tuple(res)

    op.defvjp(op_fwd, op_bwd)
    return op


@jax.custom_vjp
def bdot(x, w):
    return jnp.dot(x.astype(BF16), w.astype(BF16), preferred_element_type=F32)


def _bdot_fwd(x, w):
    return bdot(x, w), (x, w)


def _bdot_bwd(res, dy):
    x, w = res
    dyb = dy.astype(BF16)
    dx = lax.dot_general(dyb, w.astype(BF16), (((1,), (1,)), ((), ())), preferred_element_type=F32)
    dw = lax.dot_general(x.astype(BF16), dyb, (((0,), (0,)), ((), ())), preferred_element_type=F32)
    return dx, dw


bdot.defvjp(_bdot_fwd, _bdot_bwd)


def _sdot_raw(x, c):
    hi = x.astype(BF16)
    r1 = x - hi.astype(F32)
    mid = r1.astype(BF16)
    lo = (r1 - mid.astype(F32)).astype(BF16)
    d = lambda u: jnp.dot(u, c, preferred_element_type=F32)
    return d(hi) + d(mid) + d(lo)


@jax.custom_vjp
def sdot(x, c, ct):
    return _sdot_raw(x, c)


def _sdot_fwd(x, c, ct):
    return _sdot_raw(x, c), (c, ct)


def _sdot_bwd(res, dy):
    c, ct = res
    return _sdot_raw(dy, ct), None, None


sdot.defvjp(_sdot_fwd, _sdot_bwd)


def _sigmoid(x):
    return 1.0 / (1.0 + jnp.exp(-x))


def _rms(x, g):
    return x * lax.rsqrt(jnp.mean(x * x, axis=-1, keepdims=True) + NORM_EPS) * g


def _mm_up_relu2(a, b, name):
    M, K = a.shape
    tn = b.shape[2]
    N = N_DEV * tn
    tm = _pick(M, (1024, 512, 256, 128))

    def body(a_ref, b_ref, u_ref, act_ref):
        u = jnp.dot(a_ref[...], b_ref[...], preferred_element_type=F32)
        r = jnp.maximum(u, 0.0)
        u_ref[...] = u.astype(BF16)
        act_ref[...] = (r * r).astype(BF16)

    out = pl.BlockSpec((tm, tn), lambda i, j: (i, j))
    sh = jax.ShapeDtypeStruct((M, N), BF16)
    return pl.pallas_call(
        body, name=name, grid=(M // tm, N // tn),
        in_specs=[pl.BlockSpec((tm, K), lambda i, j: (i, 0)), pl.BlockSpec((None, K, tn), lambda i, j: (j, 0, 0))],
        out_specs=[out, out], out_shape=[sh, sh], compiler_params=_cparams(('parallel', 'parallel')),
    )(a, b)


def _mm_down_bwd(dy, b, u, name):
    M, N = dy.shape
    K = b.shape[0]
    tm = _pick(M, (1024, 512, 256, 128))
    tk = _pick(K, (512, 256, 128))

    def body(dy_ref, b_ref, u_ref, du_ref):
        d = lax.dot_general(dy_ref[...].astype(BF16), b_ref[...], (((1,), (1,)), ((), ())),
                            preferred_element_type=F32)
        du_ref[...] = (d * (2.0 * jnp.maximum(u_ref[...].astype(F32), 0.0))).astype(BF16)

    blk = pl.BlockSpec((tm, tk), lambda i, j: (i, j))
    return pl.pallas_call(
        body, name=name, grid=(M // tm, K // tk),
        in_specs=[pl.BlockSpec((tm, N), lambda i, j: (i, 0)), pl.BlockSpec((tk, N), lambda i, j: (j, 0)), blk],
        out_specs=blk, out_shape=jax.ShapeDtypeStruct((M, K), BF16),
        compiler_params=_cparams(('parallel', 'parallel')),
    )(dy, b, u)


_RMS_SPECS = [ROW(), FULL()]
_rms_fn = lambda xv, gv: (_rms(xv, gv),)


@functools.partial(jax.custom_vjp, nondiff_argnums=(6,))
def mlp_block(x, g, wup_b, wup_c, wdown_b, wdown_c, name):
    return _mlp_fwd(x, g, wup_b, wup_c, wdown_b, wdown_c, name)[0]


def _mlp_fwd(x, g, wup_b, wup_c, wdown_b, wdown_c, name):
    h = _stage_fwd(_rms_fn, (x, g), _RMS_SPECS, [x.shape[1]], name + '_norm', 256, out_dtype=BF16)[0]
    u, act = _mm_up_relu2(h, wup_b, name + '_up')
    y = _mm_nn(act, wdown_b, add=x, name=name + '_down')
    return y, (x, g, h, u, act, wup_b, wdown_b)


def _mlp_bwd(name, res, dy):
    x, g, h, u, act, wup_b, wdown_b = res
    du = _mm_down_bwd(dy, wdown_b, u, name + '_down_da')
    dwdown = _mm_tn(act, dy, name=name + '_down_dw')
    dh = _mm_nt(du, wup_b, name=name + '_up_da')
    dwup = _mm_tn(h, du, name=name + '_up_dw', blocked=True)
    dx, dg = _stage_bwd(_rms_fn, (x, g), _RMS_SPECS, [x.shape[1]], (dh,), name + '_norm', 256, add_to_first=dy)
    return dx, dg, None, dwup, None, dwdown


mlp_block.defvjp(_mlp_fwd, _mlp_bwd)


@functools.partial(jax.custom_vjp, nondiff_argnums=(4, 5, 6))
def _norm_projections(x, g, wbs, wcs, transposed, skip, name):
    return _norm_projections_fwd(x, g, wbs, wcs, transposed, skip, name)[0]


def _norm_projections_fwd(x, g, wbs, wcs, transposed, skip, name):
    h = _stage_fwd(_rms_fn, (x, g), _RMS_SPECS, [x.shape[1]], name + '_norm', 256, out_dtype=BF16)[0]
    mm = _mm_nt if transposed else _mm_nn
    outs = tuple(mm(h, wb, name=f'{name}_f{i}') for i, wb in enumerate(wbs))
    return ((outs, x) if skip else outs), (x, g, h, wbs)


def _norm_projections_bwd(transposed, skip, name, res, cts):
    x, g, h, wbs = res
    dys, dx_skip = cts if skip else (cts, None)
    dh = None
    for i, (dy, wb) in enumerate(zip(dys, wbs)):
        dh = (_mm_nn if transposed else _mm_nt)(dy, wb, add=dh, name=f'{name}_da{i}')
    if transposed:
        dws = tuple(_mm_tn(dy, h, name=f'{name}_dw{i}') for i, dy in enumerate(dys))
    else:
        dws = tuple(_mm_tn(h, dy, name=f'{name}_dw{i}') for i, dy in enumerate(dys))
    dx, dg = _stage_bwd(_rms_fn, (x, g), _RMS_SPECS, [x.shape[1]], (dh,), name + '_norm', 256, add_to_first=dx_skip)
    return dx, dg, None, dws


_norm_projections.defvjp(_norm_projections_fwd, _norm_projections_bwd)


def norm_projections(x, g, ws, name, transposed=False, skip=False):
    return _norm_projections(x, g.reshape(1, -1), tuple(w.b for w in ws), tuple(w.c for w in ws), transposed, skip,
                             name)


def _shift_down(x, rows):
    return jnp.where(rows == 0, 0.0, pltpu.roll(x, 1, 0))


def _shift_up(x, rows, T):
    return jnp.where(rows == T - 1, 0.0, pltpu.roll(x, T - 1, 0))


def _tshift_fwd_call(x, mu, name):
    T, C = x.shape

    def body(x_ref, mu_ref, o_ref):
        xv = x_ref[...]
        rows = lax.broadcasted_iota(jnp.int32, xv.shape, 0)
        o_ref[...] = xv + (_shift_down(xv, rows) - xv) * mu_ref[...]

    return pl.pallas_call(
        body, name=name + '_f', grid=(C // LANES,),
        in_specs=[pl.BlockSpec((T, LANES), lambda j: (0, j)), pl.BlockSpec((1, LANES), lambda j: (0, j))],
        out_specs=pl.BlockSpec((T, LANES), lambda j: (0, j)), out_shape=jax.ShapeDtypeStruct((T, C), F32),
        compiler_params=_cparams(('parallel',)),
    )(x, mu)


def _tshift_bwd_call(x, mu, dy, name):
    T, C = x.shape

    def body(x_ref, mu_ref, dy_ref, dx_ref, dmu_ref):
        xv, d = x_ref[...], dy_ref[...]
        rows = lax.broadcasted_iota(jnp.int32, xv.shape, 0)
        z = d * mu_ref[...]
        dx_ref[...] = d - z + _shift_up(z, rows, T)
        dmu_ref[...] = jnp.sum(d * (_shift_down(xv, rows) - xv), axis=0, keepdims=True)

    return pl.pallas_call(
        body, name=name + '_b', grid=(C // LANES,),
        in_specs=[pl.BlockSpec((T, LANES), lambda j: (0, j)), pl.BlockSpec((1, LANES), lambda j: (0, j)),
                  pl.BlockSpec((T, LANES), lambda j: (0, j))],
        out_specs=[pl.BlockSpec((T, LANES), lambda j: (0, j)), pl.BlockSpec((1, LANES), lambda j: (0, j))],
        out_shape=[jax.ShapeDtypeStruct((T, C), F32), jax.ShapeDtypeStruct((1, C), F32)],
        compiler_params=_cparams(('parallel',)),
    )(x, mu, dy)


@functools.partial(jax.custom_vjp, nondiff_argnums=(2,))
def token_shift_mix(x, mu, name):
    return _tshift_fwd_call(x, mu, name)


def _tsm_fwd(x, mu, name):
    return _tshift_fwd_call(x, mu, name), (x, mu)


def _tsm_bwd(name, res, dy):
    x, mu = res
    dx, dmu = _tshift_bwd_call(x, mu, dy, name)
    return dx, dmu


token_shift_mix.defvjp(_tsm_fwd, _tsm_bwd)


def _conv_specs(T):
    nb = CONV_WIDTH // LANES
    return [pl.BlockSpec((T, LANES), lambda j: (0, j)), pl.BlockSpec((T, LANES), lambda j: (0, nb + j)),
            pl.BlockSpec((T, LANES), lambda j: (0, 2 * nb + j)), pl.BlockSpec((3, LANES), lambda j: (0, j))]


def _conv_fwd_call(cv, w, name):
    T = cv.shape[0]

    def body(b_ref, c_ref, x_ref, w_ref, o_ref):
        u = c_ref[...] * x_ref[...]
        rows = lax.broadcasted_iota(jnp.int32, u.shape, 0)
        u1 = _shift_down(u, rows)
        u2 = _shift_down(u1, rows)
        o_ref[...] = b_ref[...] * (w_ref[0:1, :] * u2 + w_ref[1:2, :] * u1 + w_ref[2:3, :] * u)

    return pl.pallas_call(
        body, name=name + '_f', grid=(CONV_WIDTH // LANES,), in_specs=_conv_specs(T),
        out_specs=pl.BlockSpec((T, LANES), lambda j: (0, j)),
        out_shape=jax.ShapeDtypeStruct((T, CONV_WIDTH), F32), compiler_params=_cparams(('parallel',)),
    )(cv, cv, cv, w)


def _conv_bwd_call(cv, w, do, name):
    T = cv.shape[0]

    def body(b_ref, c_ref, x_ref, w_ref, do_ref, db_ref, dc_ref, dx_ref, dw_ref):
        c, x, d = c_ref[...], x_ref[...], do_ref[...]
        u = c * x
        rows = lax.broadcasted_iota(jnp.int32, u.shape, 0)
        u1 = _shift_down(u, rows)
        u2 = _shift_down(u1, rows)
        w0, w1, w2 = w_ref[0:1, :], w_ref[1:2, :], w_ref[2:3, :]
        db_ref[...] = d * (w0 * u2 + w1 * u1 + w2 * u)
        dy = d * b_ref[...]
        dy1 = _shift_up(dy, rows, T)
        dy2 = _shift_up(dy1, rows, T)
        du = w2 * dy + w1 * dy1 + w0 * dy2
        dc_ref[...] = du * x
        dx_ref[...] = du * c
        dw_ref[0:1, :] = jnp.sum(dy * u2, axis=0, keepdims=True)
        dw_ref[1:2, :] = jnp.sum(dy * u1, axis=0, keepdims=True)
        dw_ref[2:3, :] = jnp.sum(dy * u, axis=0, keepdims=True)

    blk = pl.BlockSpec((T, LANES), lambda j: (0, j))
    sh = jax.ShapeDtypeStruct((T, CONV_WIDTH), F32)
    return pl.pallas_call(
        body, name=name + '_b', grid=(CONV_WIDTH // LANES,), in_specs=_conv_specs(T) + [blk],
        out_specs=[blk, blk, blk, pl.BlockSpec((3, LANES), lambda j: (0, j))],
        out_shape=[sh, sh, sh, jax.ShapeDtypeStruct((3, CONV_WIDTH), F32)],
        compiler_params=_cparams(('parallel',)),
    )(cv, cv, cv, w, do)


@functools.partial(jax.custom_vjp, nondiff_argnums=(2,))
def short_conv(cv, w, name):
    return _conv_fwd_call(cv, w, name)


def _sc_fwd(cv, w, name):
    return _conv_fwd_call(cv, w, name), (cv, w)


def _sc_bwd(name, res, do):
    cv, w = res
    db, dc, dx, dw = _conv_bwd_call(cv, w, do, name)
    return jnp.concatenate([db, dc, dx], axis=1), dw


short_conv.defvjp(_sc_fwd, _sc_bwd)


ATT_SCALE = QK_HEAD ** -0.5
NPAIR = MLA_HEADS // 2


def _att_bq(T):
    return min(256, T)


def _att_masks(pair, j):
    lane = lax.broadcasted_iota(jnp.int32, (1, LANES), 1)
    mask_n = (lane // QK_NOPE) == j
    mask_r = (lane // (QK_ROPE // 2)) == (2 * pair + j)
    return mask_n, mask_r


def _att_probs(qcat, kcat, row0, stop):
    s = lax.dot_general(qcat, kcat, (((1,), (1,)), ((), ())), preferred_element_type=F32) * ATT_SCALE
    r = row0 + lax.broadcasted_iota(jnp.int32, s.shape, 0)
    c = lax.broadcasted_iota(jnp.int32, s.shape, 1)
    s = jnp.where(c <= r, s, -jnp.inf)
    e = jnp.exp(s - jnp.max(s, axis=-1, keepdims=True))
    return e / jnp.sum(e, axis=-1, keepdims=True)


def _att_in_specs(T):
    blk = lambda f: pl.BlockSpec((T, LANES), f)
    return [blk(lambda p: (0, p)), blk(lambda p: (0, 0)), blk(lambda p: (0, 0)),
            blk(lambda p: (0, p)), blk(lambda p: (0, 0)), blk(lambda p: (0, 0)), blk(lambda p: (0, p))]


def _att_fwd_call(qn, q1, q2, kn, k1, k2, v, name):
    T = qn.shape[0]
    bq = _att_bq(T)

    def body(qn_ref, q1_ref, q2_ref, kn_ref, k1_ref, k2_ref, v_ref, o_ref):
        pair = pl.program_id(0)
        for i in range(T // bq):
            r0, stop = i * bq, (i + 1) * bq
            kcat = jnp.concatenate([kn_ref[0:stop, :], k1_ref[0:stop, :], k2_ref[0:stop, :]], axis=1).astype(BF16)
            vb = v_ref[0:stop, :].astype(BF16)
            outs = []
            for j in range(2):
                mask_n, mask_r = _att_masks(pair, j)
                qcat = jnp.concatenate([jnp.where(mask_n, qn_ref[r0:stop, :], 0.0),
                                        jnp.where(mask_r, q1_ref[r0:stop, :], 0.0),
                                        jnp.where(mask_r, q2_ref[r0:stop, :], 0.0)], axis=1).astype(BF16)
                p = _att_probs(qcat, kcat, r0, stop)
                outs.append(jnp.dot(p.astype(BF16), vb, preferred_element_type=F32))
            mask_n0, _ = _att_masks(pair, 0)
            o_ref[r0:stop, :] = jnp.where(mask_n0, outs[0], outs[1])

    return pl.pallas_call(
        body, name=name + '_f', grid=(NPAIR,), in_specs=_att_in_specs(T),
        out_specs=pl.BlockSpec((T, LANES), lambda p: (0, p)),
        out_shape=jax.ShapeDtypeStruct((T, MLA_HEADS * V_HEAD), F32), compiler_params=_cparams(('parallel',)),
    )(qn, q1, q2, kn, k1, k2, v)


def _att_bwd_call(qn, q1, q2, kn, k1, k2, v, o, do, name):
    T = qn.shape[0]
    bq = _att_bq(T)

    def body(qn_ref, q1_ref, q2_ref, kn_ref, k1_ref, k2_ref, v_ref, o_ref, do_ref,
             dqn_ref, dq1_ref, dq2_ref, dkn_ref, dk1_ref, dk2_ref, dv_ref, dk_acc, dv_acc):
        pair = pl.program_id(0)

        @pl.when(pair == 0)
        def _():
            dq1_ref[...] = jnp.zeros(dq1_ref.shape, F32)
            dq2_ref[...] = jnp.zeros(dq2_ref.shape, F32)
            dk1_ref[...] = jnp.zeros(dk1_ref.shape, F32)
            dk2_ref[...] = jnp.zeros(dk2_ref.shape, F32)

        dk_acc[...] = jnp.zeros(dk_acc.shape, F32)
        dv_acc[...] = jnp.zeros(dv_acc.shape, F32)
        for i in range(T // bq):
            r0, stop = i * bq, (i + 1) * bq
            kcat = jnp.concatenate([kn_ref[0:stop, :], k1_ref[0:stop, :], k2_ref[0:stop, :]], axis=1).astype(BF16)
            vb = v_ref[0:stop, :].astype(BF16)
            dqn = jnp.zeros((bq, LANES), F32)
            for j in range(2):
                mask_n, mask_r = _att_masks(pair, j)
                qcat = jnp.concatenate([jnp.where(mask_n, qn_ref[r0:stop, :], 0.0),
                                        jnp.where(mask_r, q1_ref[r0:stop, :], 0.0),
                                        jnp.where(mask_r, q2_ref[r0:stop, :], 0.0)], axis=1).astype(BF16)
                p = _att_probs(qcat, kcat, r0, stop)
                dom = jnp.where(mask_n, do_ref[r0:stop, :], 0.0)
                delta = jnp.sum(dom * o_ref[r0:stop, :], axis=-1, keepdims=True)
                domb = dom.astype(BF16)
                dp = lax.dot_general(domb, vb, (((1,), (1,)), ((), ())), preferred_element_type=F32)
                ds = (p * (dp - delta) * ATT_SCALE).astype(BF16)
                dqc = jnp.dot(ds, kcat, preferred_element_type=F32)
                dqn = dqn + jnp.where(mask_n, dqc[:, 0:LANES], 0.0)
                dq1_ref[r0:stop, :] += jnp.where(mask_r, dqc[:, LANES:2 * LANES], 0.0)
                dq2_ref[r0:stop, :] += jnp.where(mask_r, dqc[:, 2 * LANES:3 * LANES], 0.0)
                dk_acc[0:stop, :] += lax.dot_general(ds, qcat, (((0,), (0,)), ((), ())),
                                                     preferred_element_type=F32)
                dv_acc[0:stop, :] += lax.dot_general(p.astype(BF16), domb, (((0,), (0,)), ((), ())),
                                                     preferred_element_type=F32)
            dqn_ref[r0:stop, :] = dqn
        dkn_ref[...] = dk_acc[:, 0:LANES]
        dk1_ref[...] += dk_acc[:, LANES:2 * LANES]
        dk2_ref[...] += dk_acc[:, 2 * LANES:3 * LANES]
        dv_ref[...] = dv_acc[...]

    per_pair = pl.BlockSpec((T, LANES), lambda p: (0, p))
    shared = pl.BlockSpec((T, LANES), lambda p: (0, 0))
    wide = jax.ShapeDtypeStruct((T, MLA_HEADS * QK_NOPE), F32)
    narrow = jax.ShapeDtypeStruct((T, LANES), F32)
    return pl.pallas_call(
        body, name=name + '_b', grid=(NPAIR,), in_specs=_att_in_specs(T) + [per_pair, per_pair],
        out_specs=[per_pair, shared, shared, per_pair, shared, shared, per_pair],
        out_shape=[wide, narrow, narrow, wide, narrow, narrow, wide],
        scratch_shapes=[pltpu.VMEM((T, 3 * LANES), F32), pltpu.VMEM((T, LANES), F32)],
        compiler_params=_cparams(('arbitrary',)),
    )(qn, q1, q2, kn, k1, k2, v, o, do)


@functools.partial(jax.custom_vjp, nondiff_argnums=(7,))
def attention(qn, q1, q2, kn, k1, k2, v, name):
    return _att_fwd_call(qn, q1, q2, kn, k1, k2, v, name)


def _attn_fwd(qn, q1, q2, kn, k1, k2, v, name):
    o = _att_fwd_call(qn, q1, q2, kn, k1, k2, v, name)
    return o, (qn, q1, q2, kn, k1, k2, v, o)


def _attn_bwd(name, res, do):
    return tuple(_att_bwd_call(*res, do, name))


attention.defvjp(_attn_fwd, _attn_bwd)


SCAN_CHUNK = 64
SCAN_UNROLL = 8


def _block_ones(n, seg):
    i = np.arange(n)
    return (i[:, None] // seg == i[None, :] // seg).astype(np.float32)


def _scan_diag():
    i = np.arange(RW_WIDTH)
    return jnp.asarray((np.arange(RW_N)[:, None] == (i[None, :] % RW_N)).astype(np.float32))


def _head_rowsum(x):
    low = lax.broadcasted_iota(jnp.int32, (1, LANES), 1) < RW_N
    tiles = []
    for j in range(RW_WIDTH // LANES):
        xt = x[:, j * LANES:(j + 1) * LANES]
        x0 = jnp.where(low, xt, 0.0)
        s0 = jnp.sum(x0, axis=-1, keepdims=True)
        s1 = jnp.sum(xt - x0, axis=-1, keepdims=True)
        tiles.append(jnp.where(low, s0, s1))
    return jnp.concatenate(tiles, axis=1)


def _unrolled_loop(n, step, init):
    def body(i, carry):
        for j in range(SCAN_UNROLL):
            carry = step(i * SCAN_UNROLL + j, carry)
        return carry
    return lax.fori_loop(0, n // SCAN_UNROLL, body, init)


def _scan_fwd_call(r, w, k, v, a, b, name):
    T = r.shape[0]
    tc = min(SCAN_CHUNK, T)
    dg = _scan_diag()

    def body(r_ref, w_ref, k_ref, v_ref, a_ref, b_ref, dg_ref, y_ref, st_ref, sa_ref, vc_ref, s_ref):
        @pl.when(pl.program_id(0) == 0)
        def _():
            s_ref[...] = jnp.zeros(s_ref.shape, F32)

        dgv = dg_ref[...]
        readout = lambda s, t: jnp.sum(_head_rowsum(s * r_ref[t]) * dgv, axis=0, keepdims=True)

        def step(t, carry):
            s, vcol = carry
            st_ref[t] = s
            vc_ref[t] = vcol
            sa = _head_rowsum(s * a_ref[t])
            sa_ref[t] = sa
            prev = jnp.maximum(t - 1, 0)
            y_ref[prev] = readout(s, prev)
            vcol_next = _head_rowsum(dgv * v_ref[jnp.minimum(t + 1, tc - 1)])
            sn = s * w_ref[t] + sa * b_ref[t] + vcol * k_ref[t]
            return sn, vcol_next

        s_end, _ = _unrolled_loop(tc, step, (s_ref[...], _head_rowsum(dgv * v_ref[0])))
        y_ref[tc - 1] = readout(s_end, tc - 1)
        s_ref[...] = s_end

    vec = pl.BlockSpec((tc, 1, RW_WIDTH), lambda i: (i, 0, 0))
    mat = pl.BlockSpec((tc, RW_N, RW_WIDTH), lambda i: (i, 0, 0))
    msh = jax.ShapeDtypeStruct((T, RW_N, RW_WIDTH), F32)
    return pl.pallas_call(
        body, name=name + '_f', grid=(T // tc,),
        in_specs=[vec] * 6 + [pl.BlockSpec((RW_N, RW_WIDTH), lambda i: (0, 0))],
        out_specs=[vec, mat, mat, mat],
        out_shape=[jax.ShapeDtypeStruct((T, 1, RW_WIDTH), F32), msh, msh, msh],
        scratch_shapes=[pltpu.VMEM((RW_N, RW_WIDTH), F32)],
        compiler_params=_cparams(('arbitrary',)),
    )(r, w, k, v, a, b, dg)


def _scan_bwd_call(r, w, k, a, b, st, sa_all, vc_all, dy, name):
    T = r.shape[0]
    tc = min(SCAN_CHUNK, T)
    nt = T // tc
    dg = _scan_diag()

    def body(r_ref, w_ref, k_ref, a_ref, b_ref, st_ref, sa_ref, vc_ref, dy_ref, dg_ref,
             dr_ref, dw_ref, dk_ref, dv_ref, da_ref, db_ref, ds_ref):
        @pl.when(pl.program_id(0) == 0)
        def _():
            ds_ref[...] = jnp.zeros(ds_ref.shape, F32)

        dgv = dg_ref[...]
        colsum = lambda x: jnp.sum(x, axis=0, keepdims=True)

        def step(i, carry):
            ds, dycol = carry
            t = tc - 1 - i
            sp = st_ref[t]
            rt, wt, kt, at, bt = r_ref[t], w_ref[t], k_ref[t], a_ref[t], b_ref[t]
            ds = ds + dycol * rt
            dsa = _head_rowsum(ds * bt)
            sa, vcol = sa_ref[t], vc_ref[t]
            dycol_next = _head_rowsum(dgv * dy_ref[jnp.maximum(t - 1, 0)])
            sn = sp * wt + sa * bt + vcol * kt
            dr_ref[t] = colsum(sn * dycol)
            dk_ref[t] = colsum(ds * vcol)
            db_ref[t] = colsum(ds * sa)
            dw_ref[t] = colsum(ds * sp)
            dv_ref[t] = colsum(_head_rowsum(ds * kt) * dgv)
            da_ref[t] = colsum(sp * dsa)
            return ds * wt + dsa * at, dycol_next

        ds_end, _ = _unrolled_loop(tc, step, (ds_ref[...], _head_rowsum(dgv * dy_ref[tc - 1])))
        ds_ref[...] = ds_end

    vec = pl.BlockSpec((tc, 1, RW_WIDTH), lambda i: (nt - 1 - i, 0, 0))
    mat = pl.BlockSpec((tc, RW_N, RW_WIDTH), lambda i: (nt - 1 - i, 0, 0))
    vsh = jax.ShapeDtypeStruct((T, 1, RW_WIDTH), F32)
    return pl.pallas_call(
        body, name=name + '_b', grid=(nt,),
        in_specs=[vec] * 5 + [mat] * 3 + [vec, pl.BlockSpec((RW_N, RW_WIDTH), lambda i: (0, 0))],
        out_specs=[vec] * 6, out_shape=[vsh] * 6,
        scratch_shapes=[pltpu.VMEM((RW_N, RW_WIDTH), F32)],
        compiler_params=_cparams(('arbitrary',)),
    )(r, w, k, a, b, st, sa_all, vc_all, dy, dg)


@functools.partial(jax.custom_vjp, nondiff_argnums=(6,))
def wkv7(r, w, k, v, a, b, name):
    return _scan_fwd_call(r, w, k, v, a, b, name)[0]


def _wkv7_fwd(r, w, k, v, a, b, name):
    y, st, sa_all, vc_all = _scan_fwd_call(r, w, k, v, a, b, name)
    return y, (r, w, k, a, b, st, sa_all, vc_all)


def _wkv7_bwd(name, res, dy):
    return tuple(_scan_bwd_call(*res, dy, name))


wkv7.defvjp(_wkv7_fwd, _wkv7_bwd)


def _np_bf16(a):
    return jnp.asarray(a, BF16)


def _mla_consts():
    seg_n = (np.arange(512)[:, None] // QK_NOPE == np.arange(LANES)[None, :]).astype(np.float32)
    seg_r = (np.arange(LANES)[:, None] // 16 == np.arange(LANES)[None, :]).astype(np.float32)
    e1 = np.zeros((LANES, LANES), np.float32)
    e2 = np.zeros((LANES, LANES), np.float32)
    for h in range(MLA_HEADS):
        for i in range(16):
            e1[i, h * 16 + i] = 1.0
            e2[16 + i, h * 16 + i] = 1.0
    mats = [seg_n, seg_n.T, seg_r, seg_r.T, e1, e1.T, e2, e2.T]
    return [_np_bf16(m) for m in mats]


def _qk_prep_fn(qn, q1, q2, kn, kx, cos, sin, gqn, gq1, gq2, gkn, gk1, gk2,
                seg_n, seg_nt, seg_r, seg_rt, e1, e1t, e2, e2t):
    def normrope(xn, x1, x2, gn, g1, g2):
        ss = sdot(xn * xn, seg_n, seg_nt) + sdot(x1 * x1, seg_r, seg_rt) + sdot(x2 * x2, seg_r, seg_rt)
        inv = lax.rsqrt(ss * (1.0 / QK_HEAD) + NORM_EPS)
        inv_n = sdot(inv, seg_nt, seg_n)
        inv_r = sdot(inv, seg_rt, seg_r)
        y1 = x1 * inv_r * g1
        y2 = x2 * inv_r * g2
        return xn * inv_n * gn, y1 * cos - y2 * sin, y1 * sin + y2 * cos

    k1 = sdot(kx, e1, e1t)
    k2 = sdot(kx, e2, e2t)
    return normrope(qn, q1, q2, gqn, gq1, gq2) + normrope(kn, k1, k2, gkn, gk1, gk2)


def _rwkv_prep_fn(vres):
    def fn(r, k, v, xg, xwa, kx, *rest):
        if vres:
            vfirst, w0, a0, k_k, k_a, w2p, a2p, g2, v0, v2p, bm = rest
        else:
            w0, a0, k_k, k_a, w2p, a2p, g2, bm = rest
        z = w0 + bdot(jnp.tanh(xwa), w2p)
        nz = -z
        softplus = jnp.maximum(nz, 0.0) + jnp.log(1.0 + jnp.exp(-jnp.abs(nz)))
        decay = jnp.exp(-jnp.exp(-softplus - 0.5))
        a = _sigmoid(a0 + bdot(xwa, a2p))
        g = bdot(_sigmoid(xg), g2)
        if vres:
            vv = v + (vfirst - v) * _sigmoid(v0 + bdot(kx, v2p))
        else:
            vv = v
        kkr = k * k_k
        kk = kkr / jnp.maximum(jnp.sqrt(sdot(kkr * kkr, bm, bm)), 1e-12)
        k2 = k * (1.0 + (a - 1.0) * k_a)
        return r * 1.0, decay, k2, vv, -kk, kk * a, g
    return fn


def _rwkv_post_fn(y, r, k2, vv, g, ln_w, ln_b, rk, bm):
    inv_n = 1.0 / RW_N
    mean = sdot(y, bm, bm) * inv_n
    yc = y - mean
    var = sdot(yc * yc, bm, bm) * inv_n
    yn = yc * lax.rsqrt(var + GN_EPS) * ln_w + ln_b
    bonus = sdot(r * k2 * rk, bm, bm) * vv
    return ((yn + bonus) * g,)


def _merge_fn(g0, g1, g2, oa, ob, oc):
    return (_sigmoid(g0) * oa + _sigmoid(g1) * ob + _sigmoid(g2) * oc,)


def _loss_call(y, target):
    T, C = y.shape
    tb = min(256, T)

    def body(y_ref, t_ref, dy_ref, part_ref):
        err = y_ref[...] - t_ref[...]
        dy_ref[...] = err * (1.0 / C)
        sq = jnp.sum(err * err, axis=0, keepdims=True)
        acc = sq[:, 0:LANES]
        for j in range(1, C // LANES):
            acc = acc + sq[:, j * LANES:(j + 1) * LANES]
        part_ref[...] = jnp.zeros(part_ref.shape, F32)
        part_ref[0:1, :] = acc * (0.5 / C)

    return pl.pallas_call(
        body, name='loss', grid=(T // tb,),
        in_specs=[pl.BlockSpec((tb, C), lambda i: (i, 0))] * 2,
        out_specs=[pl.BlockSpec((tb, C), lambda i: (i, 0)), pl.BlockSpec((8, LANES), lambda i: (i, 0))],
        out_shape=[jax.ShapeDtypeStruct((T, C), F32), jax.ShapeDtypeStruct((8 * (T // tb), LANES), F32)],
        compiler_params=_cparams(('parallel',)),
    )(y, target)


def _pad_rows(t, before, total):
    return jnp.pad(t, ((before, total - before - t.shape[0]), (0, 0)))


def _head_tile(g, lo, hi):
    return jnp.tile(g[lo:hi], MLA_HEADS).reshape(1, -1)


IN_WEIGHTS = ('w_in', 'rwkv_v1')


def _layer(l, x, v_first, wd, sp, cos, sin):
    proj, x_skip = _layer_in(l, x, wd, sp)
    return _layer_rest(l, proj, x_skip, v_first, wd, sp, cos, sin)


def _layer_in(l, x, wd, sp):
    nm = f'l{l}'
    vres = l > 0
    w_in = wd['w_in']
    if vres:
        v1t = wd['rwkv_v1'].map(lambda t: t.T)
    else:
        v1t = W(jnp.zeros((MV_LORA, D_MODEL), BF16), jnp.zeros((MV_LORA, D_MODEL), BF16))
    zpad = W(jnp.zeros((64, D_MODEL), BF16), jnp.zeros((64, D_MODEL), BF16))
    w_rw = _wcat([w_in['rkv'], w_in['xg'], w_in['xwa'], w_in['kpe'], v1t, zpad], 0)
    return norm_projections(
        x, sp['attn_norm'], [w_in['gates'], w_in['cq'], w_in['ckv'], w_rw, w_in['conv']], nm + '_win',
        transposed=True, skip=True)


def _layer_rest(l, proj, x_skip, v_first, wd, sp, cos, sin):
    gates, cq, ckv, rw, cv = proj
    T = rw.shape[0]
    nm = f'l{l}'
    vres = l > 0

    v_mu = sp['rwkv_v_mu'] if vres else jnp.zeros((MV_LORA,), F32)
    mu_all = jnp.concatenate([sp['rwkv_mu'][0:768], sp['rwkv_mu'][896:1024], sp['rwkv_mu'][768:896],
                              jnp.zeros((QK_ROPE,), F32), v_mu, jnp.zeros((64,), F32)]).reshape(1, -1)
    rws = token_shift_mix(rw, mu_all, nm + '_shift')

    wq = wd['mla_wq_b'].map(lambda t: jnp.concatenate(
        [t.reshape(Q_LORA, MLA_HEADS, QK_HEAD)[:, :, 0:64].reshape(Q_LORA, 512),
         t.reshape(Q_LORA, MLA_HEADS, QK_HEAD)[:, :, 64:80].reshape(Q_LORA, 128),
         t.reshape(Q_LORA, MLA_HEADS, QK_HEAD)[:, :, 80:96].reshape(Q_LORA, 128)], axis=1))
    wkn = wd['mla_wkv_b'].map(lambda t: t.reshape(KV_LORA, MLA_HEADS, 128)[:, :, 0:64].reshape(KV_LORA, 512))
    wv = wd['mla_wkv_b'].map(lambda t: t.reshape(KV_LORA, MLA_HEADS, 128)[:, :, 64:128].reshape(KV_LORA, 512))
    q, = norm_projections(cq, sp['mla_q_a_norm'], [wq], nm + '_wq')
    kn, vv_att = norm_projections(ckv, sp['mla_kv_a_norm'], [wkn, wv], nm + '_wkv')
    gq, gk = sp['mla_q_norm'], sp['mla_k_norm']
    consts = _mla_consts()
    qk_specs = ([ROW(pieces=((0, 512), (512, 128), (640, 128))), ROW(), ROW(pieces=((1024, 128),)),
                 ROW(False), ROW(False)] + [FULL()] * 6 + [FULL(False)] * 8)
    qk_op = stage_op(_qk_prep_fn, qk_specs, [512, 128, 128, 512, 128, 128], nm + '_qkprep')
    Qn, Q1, Q2, Kn, K1, K2 = qk_op(q, kn, rws, cos, sin,
                                   _head_tile(gq, 0, 64), _head_tile(gq, 64, 80), _head_tile(gq, 80, 96),
                                   _head_tile(gk, 0, 64), _head_tile(gk, 64, 80), _head_tile(gk, 80, 96), *consts)
    o_att = attention(Qn, Q1, Q2, Kn, K1, K2, vv_att, nm + '_att')
    o_a = linear(o_att, wd['mla_w_o'], name=nm + '_wo')

    bm = _np_bf16(_block_ones(RW_WIDTH, RW_N))
    vec = lambda n: sp[n].reshape(1, -1)
    f32w = lambda n: wd[n].c + wd[n].b.astype(F32)
    w2p = _pad_rows(f32w('rwkv_w2'), 0, 128)
    a2p = _pad_rows(f32w('rwkv_a2'), 64, 128)
    g2 = f32w('rwkv_g2')
    rw_pieces = ((0, 256), (256, 256), (512, 256), (768, 128), (896, 128), (1024, 128))
    if vres:
        v2p = _pad_rows(f32w('rwkv_v2'), 32, 128)
        prep_specs = [ROW(pieces=rw_pieces), ROW()] + [FULL()] * 9 + [FULL(False)]
        prep_in = [rws, v_first, vec('rwkv_w0'), vec('rwkv_a0'), vec('rwkv_k_k'), vec('rwkv_k_a'), w2p, a2p, g2,
                   vec('rwkv_v0'), v2p, bm]
    else:
        prep_specs = [ROW(pieces=rw_pieces)] + [FULL()] * 7 + [FULL(False)]
        prep_in = [rws, vec('rwkv_w0'), vec('rwkv_a0'), vec('rwkv_k_k'), vec('rwkv_k_a'), w2p, a2p, g2, bm]
    prep_op = stage_op(_rwkv_prep_fn(vres), prep_specs, [256] * 7, nm + '_rwprep')
    r_, dec, k2, vv, an, bn, g = prep_op(*prep_in)
    if not vres:
        v_first = vv
    t3 = lambda t: t.reshape(T, 1, RW_WIDTH)
    y = wkv7(t3(r_), t3(dec), t3(k2), t3(vv), t3(an), t3(bn), nm + '_scan').reshape(T, RW_WIDTH)
    post_op = stage_op(_rwkv_post_fn, [ROW()] * 5 + [FULL()] * 3 + [FULL(False)], [256], nm + '_rwpost')
    yb = post_op(y, r_, k2, vv, g, vec('rwkv_ln_w'), vec('rwkv_ln_b'), sp['rwkv_r_k'].reshape(1, -1), bm)[0]
    o_b = linear(yb, wd['rwkv_w_o'], name=nm + '_rwo')

    oc_in = short_conv(cv, f32w('conv_w'), nm + '_conv')
    o_c = linear(oc_in, wd['conv_w_o'], name=nm + '_cwo')

    merge_op = stage_op(_merge_fn, [ROW(pieces=((0, 1024), (1024, 1024), (2048, 1024))), ROW(), ROW(), ROW()],
                        [D_MODEL], nm + '_merge')
    merged = merge_op(gates, o_a, o_b, o_c)[0]
    x2 = linear(merged, wd['w_out'], add=x_skip, name=nm + '_wout')
    return x2, v_first


def _mlp(l, x, wd, sp):
    return mlp_block(x, sp['mlp_norm'].reshape(1, -1), wd['w_up'].b, wd['w_up'].c, wd['w_down'].b, wd['w_down'].c,
                     f'l{l}_mlp')


MLP_WEIGHTS = ('w_up', 'w_down')


def _entries(ml):
    layer, part = ml if isinstance(ml, tuple) else (ml, 'all')
    out = []
    for name, (layers, shape, axis) in SHARDED.items():
        l = layer if layers == DEPTH else layer - 1
        if not 0 <= l < layers or name in MLP_WEIGHTS:
            continue
        if part != 'all' and (name in IN_WEIGHTS) != (part == 'in'):
            continue
        n = shape[0] * shape[1] // N_DEV
        if name == 'conv_w':
            out.append(('conv_w_hi', name, l, n))
            out.append(('conv_w_lo', name, l, n))
        else:
            out.append((name, name, l, n))
    return out


def _slot_size(n):
    return -(-n // LANES) * LANES


def _pack_rows(ml):
    total = sum(_slot_size(n) for _, _, _, n in _entries(ml))
    rows = -(-total // LANES)
    return -(-rows // PACK_ROW_MULT) * PACK_ROW_MULT


def _pack_flat(pieces, ml):
    rows = _pack_rows(ml)
    padded = []
    for p, (_, _, _, n) in zip(pieces, _entries(ml)):
        pad = _slot_size(n) - n
        if pad:
            p = jnp.pad(p, [(0, 0)] * (p.ndim - 1) + [(0, pad)])
        padded.append(p)
    flat = jnp.concatenate(padded, axis=-1)
    tail = rows * LANES - flat.shape[-1]
    if tail:
        flat = jnp.pad(flat, [(0, 0)] * (flat.ndim - 1) + [(0, tail)])
    return flat.reshape(flat.shape[:-1] + (rows, LANES))


def _unpack_flat(buf, ml):
    out, row = [], 0
    for _, _, _, n in _entries(ml):
        nrows = _slot_size(n) // LANES
        piece = buf[..., row:row + nrows, :].reshape(buf.shape[:-2] + (-1,))
        out.append(piece[..., :n])
        row += nrows
    return out


def _pack_shards(shards, ml, dtype, split_conv):
    pieces = []
    for slot, name, l, n in _entries(ml):
        a = shards[name][l]
        if name in TRANSPOSED:
            a = a.astype(dtype).T
        a = a.reshape(-1)
        if slot == 'conv_w_hi':
            a = a.astype(BF16).astype(F32) if split_conv else a
        elif slot == 'conv_w_lo':
            a = (a - a.astype(BF16).astype(F32)) if split_conv else jnp.zeros_like(a)
        pieces.append(a.astype(dtype))
    return _pack_flat(pieces, ml)


def _unpack_shards(buf, ml):
    out = {}
    for (slot, name, l, n), v in zip(_entries(ml), _unpack_flat(buf, ml)):
        if slot == 'conv_w_lo':
            continue
        layers, shape, axis = SHARDED[name]
        sshape = (shape[0] // N_DEV, shape[1]) if axis == 0 else (shape[0], shape[1] // N_DEV)
        out[(name, l)] = v.reshape(sshape[::-1]).T if name in TRANSPOSED else v.reshape(sshape)
    return out


def _to_full(blocks, shape, axis):
    if axis == 0:
        return blocks.reshape(shape)
    return blocks.reshape(N_DEV, shape[0], shape[1] // N_DEV).transpose(1, 0, 2).reshape(shape)


def _to_blocks(full, axis):
    r, c = full.shape
    if axis == 0:
        return full.reshape(N_DEV, -1)
    return full.reshape(r, N_DEV, c // N_DEV).transpose(1, 0, 2).reshape(N_DEV, -1)


def _unpack_gathered(gathered, ml):
    out, conv_hi = {}, None
    for (slot, name, l, n), v in zip(_entries(ml), _unpack_flat(gathered, ml)):
        layers, shape, axis = SHARDED[name]
        if name in TRANSPOSED:
            out[name] = v.reshape(-1, shape[0])
            continue
        full = _to_full(v, shape, axis)
        if slot == 'conv_w_hi':
            conv_hi = full
        elif slot == 'conv_w_lo':
            out[name] = conv_hi.astype(F32) + full.astype(F32)
        else:
            out[name] = full
    return out


def _pack_grads(grads, ml):
    pieces = []
    for slot, name, l, n in _entries(ml):
        if name in TRANSPOSED:
            blocks = jnp.concatenate(grads[name], axis=0).reshape(N_DEV, -1)
        else:
            blocks = _to_blocks(grads[name], SHARDED[name][2])
        if slot == 'conv_w_lo':
            blocks = jnp.zeros_like(blocks)
        pieces.append(blocks.astype(BF16))
    return _pack_flat(pieces, ml)


def _my_pos():
    return lax.axis_index('x'), lax.axis_index('y'), lax.axis_index('c')


def _flip(v, bit):
    return 1 - v if bit else v


def all_gather_blocks(x):
    rows = x.shape[0]

    def body(x_ref, out_ref, send_sems, recv_sems, local_sem):
        mx, my, mc = _my_pos()
        me, sibling = (mx, my, mc), (mx, my, 1 - mc)
        chips = [(1 - mx, my), (mx, 1 - my), (1 - mx, 1 - my)]

        def block(px, py, pc):
            return out_ref.at[4 * px + 2 * py + pc]

        def copy(k, blk, to, src=None):
            return pltpu.make_async_remote_copy(
                src_ref=block(*blk) if src is None else src, dst_ref=block(*blk),
                send_sem=send_sems.at[k], recv_sem=recv_sems.at[k], device_id=to, device_id_type=MESH)

        mine = pltpu.make_async_copy(x_ref, block(*me), local_sem)
        mine.start()
        first = [copy(0, me, sibling, src=x_ref)]
        first += [copy(1 + j, me, (*chip, mc), src=x_ref) for j, chip in enumerate(chips)]
        for cp in first:
            cp.start()
        passed = [copy(4 + j, (*chip, mc), sibling) for j, chip in enumerate(chips)]
        for j, chip in enumerate(chips):
            copy(1 + j, (*chip, mc), me).wait_recv()
            passed[j].start()
        copy(0, sibling, me).wait_recv()
        for j, chip in enumerate(chips):
            copy(4 + j, (*chip, 1 - mc), me).wait_recv()
        for cp in first + passed:
            cp.wait_send()
        mine.wait()

    return pl.pallas_call(
        body, name='all_gather_weights',
        out_shape=jax.ShapeDtypeStruct((N_DEV, rows, LANES), x.dtype),
        in_specs=[pl.BlockSpec(memory_space=pl.ANY)], out_specs=pl.BlockSpec(memory_space=pl.ANY),
        scratch_shapes=[pltpu.SemaphoreType.DMA((7,)), pltpu.SemaphoreType.DMA((7,)), pltpu.SemaphoreType.DMA],
    )(x)


HBM_SPEC = pl.BlockSpec(memory_space=pltpu.HBM)
SEM_SPEC = pl.BlockSpec(memory_space=pltpu.SEMAPHORE)
DATAFLOW_EFFECT = pltpu.SideEffectType.DATAFLOW_SIDE_EFFECTING


def _direct_copies(src_ref, land_ref, send_sems, recv_sems, per_peer):
    mx, my, mc = _my_pos()
    me = 4 * mx + 2 * my + mc
    copies = []
    for k in range(1, N_DEV):
        peer = (_flip(mx, k & 4), _flip(my, k & 2), _flip(mc, k & 1))
        pidx = 4 * peer[0] + 2 * peer[1] + peer[2]
        copies.append(pltpu.make_async_remote_copy(
            src_ref=src_ref.at[pidx] if per_peer else src_ref, dst_ref=land_ref.at[me],
            send_sem=send_sems.at[k - 1], recv_sem=recv_sems.at[k - 1], device_id=peer, device_id_type=MESH))
    return copies


def send_start(src, per_peer, name):
    block = src.shape[1:] if per_peer else src.shape
    land_shape = (N_DEV,) + tuple(block)

    def body(src_ref, land_ref, send_sems, recv_sems, src_thru, land_thru, token):
        for cp in _direct_copies(src_ref, land_ref, send_sems, recv_sems, per_peer):
            cp.start()
        token[...] = jnp.zeros(token.shape, F32)

    send_sems, recv_sems, src_thru, land_thru, token = pl.pallas_call(
        body, name=name,
        out_shape=(pltpu.SemaphoreType.DMA((N_DEV - 1,)), pltpu.SemaphoreType.DMA((N_DEV - 1,)),
                   pltpu.HBM(src.shape, src.dtype), pltpu.HBM(land_shape, src.dtype),
                   jax.ShapeDtypeStruct((8, LANES), F32)),
        in_specs=(HBM_SPEC, HBM_SPEC),
        out_specs=(SEM_SPEC, SEM_SPEC, HBM_SPEC, HBM_SPEC, pl.BlockSpec(memory_space=pltpu.VMEM)),
        input_output_aliases={0: 2, 1: 3},
        compiler_params=pltpu.CompilerParams(has_side_effects=DATAFLOW_EFFECT),
    )(pltpu.with_memory_space_constraint(src, pltpu.HBM),
      pltpu.with_memory_space_constraint(lax.empty(land_shape, src.dtype), pltpu.HBM))
    return (send_sems, recv_sems, src_thru, land_thru), token[0, 0]


def send_wait(handles, after, per_peer, name):
    send_sems, recv_sems, src_thru, land_thru = handles

    def body(src_ref, land_ref, send_sems, recv_sems, after_ref, src_dead, got_ref):
        for cp in _direct_copies(src_ref, land_ref, send_sems, recv_sems, per_peer):
            cp.wait_send()
            cp.wait_recv()

    return pl.pallas_call(
        body, name=name,
        out_shape=(pltpu.HBM(src_thru.shape, src_thru.dtype), pltpu.HBM(land_thru.shape, land_thru.dtype)),
        in_specs=(HBM_SPEC, HBM_SPEC, SEM_SPEC, SEM_SPEC, pl.BlockSpec(memory_space=pl.ANY)),
        out_specs=(HBM_SPEC, HBM_SPEC), input_output_aliases={0: 0, 1: 1},
        compiler_params=pltpu.CompilerParams(has_side_effects=DATAFLOW_EFFECT),
    )(src_thru, land_thru, send_sems, recv_sems, after)[1]


def _adamw_math(w, g, m, v):
    m2 = ADAM_B1 * m + (1.0 - ADAM_B1) * g
    v2 = ADAM_B2 * v + (1.0 - ADAM_B2) * (g * g)
    m_hat = m2 / (1.0 - ADAM_B1 ** ADAM_STEP)
    v_hat = v2 / (1.0 - ADAM_B2 ** ADAM_STEP)
    delta = -ADAM_LR * (m_hat / (jnp.sqrt(v_hat) + ADAM_EPS) + ADAM_WD * w)
    return delta, m2, v2


def sum_parts(parts, mine, name):
    _, rows, cols = parts.shape
    rb = rows
    while N_DEV * rb * cols * 2 > (2 << 20) and rb % 32 == 0:
        rb //= 2

    def body(p_ref, mine_ref, g_ref):
        mx, my, mc = _my_pos()
        me = 4 * mx + 2 * my + mc
        own = mine_ref[0].astype(F32)
        g = jnp.where(me == 0, own, p_ref[0].astype(F32))
        for j in range(1, N_DEV):
            g = g + jnp.where(me == j, own, p_ref[j].astype(F32))
        g_ref[...] = g

    return pl.pallas_call(
        body, name=name, grid=(rows // rb,),
        in_specs=[pl.BlockSpec((N_DEV, rb, cols), lambda i: (0, i, 0)), pl.BlockSpec((1, rb, cols), lambda i: (0, i, 0))],
        out_specs=pl.BlockSpec((rb, cols), lambda i: (i, 0)),
        out_shape=jax.ShapeDtypeStruct((rows, cols), F32), compiler_params=_cparams(('parallel',)),
    )(parts, mine)


ADAMW_BLOCK_BYTES = 1 << 20


def adamw_weight(name, w, m, v, grads):
    layers, a, b = w.shape
    ra = a
    while ra * b * 4 > ADAMW_BLOCK_BYTES and ra % 16 == 0:
        ra //= 2

    def body(*refs):
        w_ref, m_ref, v_ref = refs[:3]
        g_refs = refs[3:3 + layers]
        g_ref, d_ref, m2_ref, v2_ref = refs[3 + layers:]
        g = g_refs[0][...]
        for l in range(1, layers):
            g = jnp.where(pl.program_id(0) == l, g_refs[l][...], g)
        delta, m2, v2 = _adamw_math(w_ref[0], g, m_ref[0], v_ref[0])
        g_ref[0] = g
        d_ref[0] = delta
        m2_ref[0] = m2
        v2_ref[0] = v2

    blk = pl.BlockSpec((1, ra, b), lambda l, i: (l, i, 0))
    gblk = pl.BlockSpec((ra, b), lambda l, i: (i, 0))
    sh = jax.ShapeDtypeStruct(w.shape, F32)
    return pl.pallas_call(
        body, name='adamw_' + name, grid=(layers, a // ra), in_specs=[blk] * 3 + [gblk] * layers,
        out_specs=[blk] * 4, out_shape=[sh] * 4, compiler_params=_cparams(('parallel', 'parallel')),
    )(w, m, v, *grads)


def allreduce_adamw_small(g, w, m, v):
    rows = g.shape[0]

    def body(g_ref, w_ref, m_ref, v_ref, gs_ref, d_ref, m2_ref, v2_ref, all_ref, send_sems, recv_sems):
        mx, my, mc = _my_pos()
        me, sibling = (mx, my, mc), (mx, my, 1 - mc)
        chips = [(1 - mx, my), (mx, 1 - my), (1 - mx, 1 - my)]

        def block(px, py, pc):
            return all_ref.at[4 * px + 2 * py + pc]

        def copy(k, blk, to, src=None):
            return pltpu.make_async_remote_copy(
                src_ref=block(*blk) if src is None else src, dst_ref=block(*blk),
                send_sem=send_sems.at[k], recv_sem=recv_sems.at[k], device_id=to, device_id_type=MESH)

        first = [copy(0, me, sibling, src=g_ref)]
        first += [copy(1 + j, me, (*chip, mc), src=g_ref) for j, chip in enumerate(chips)]
        for cp in first:
            cp.start()
        passed = [copy(4 + j, (*chip, mc), sibling) for j, chip in enumerate(chips)]
        for j, chip in enumerate(chips):
            copy(1 + j, (*chip, mc), me).wait_recv()
            passed[j].start()
        copy(0, sibling, me).wait_recv()
        for j, chip in enumerate(chips):
            copy(4 + j, (*chip, 1 - mc), me).wait_recv()
        for cp in first + passed:
            cp.wait_send()
        my_idx = 4 * mx + 2 * my + mc
        total = jnp.zeros((rows, LANES), F32)
        for j in range(N_DEV):
            total = total + jnp.where(my_idx == j, g_ref[...], all_ref[j])
        delta, m2, v2 = _adamw_math(w_ref[...], total, m_ref[...], v_ref[...])
        gs_ref[...] = total
        d_ref[...] = delta
        m2_ref[...] = m2
        v2_ref[...] = v2

    vm = pl.BlockSpec(memory_space=pltpu.VMEM)
    sh = jax.ShapeDtypeStruct((rows, LANES), F32)
    return pl.pallas_call(
        body, name='allreduce_adamw_small', in_specs=[vm] * 4, out_specs=[vm] * 4, out_shape=[sh] * 4,
        scratch_shapes=[pltpu.VMEM((N_DEV, rows, LANES), F32), pltpu.SemaphoreType.DMA((7,)),
                        pltpu.SemaphoreType.DMA((7,))],
    )(g, w, m, v)


SMALL_COUNT = 11680


def _small_pack(d, extra=None):
    assert sum(d[n].size for n in SMALL_NAMES) == SMALL_COUNT
    flat = jnp.concatenate([d[n].reshape(-1) for n in SMALL_NAMES] + ([] if extra is None else [extra.reshape(1)]))
    rows = -(-(SMALL_COUNT + 1) // (8 * LANES)) * 8
    return jnp.pad(flat, (0, rows * LANES - flat.shape[0])).reshape(rows, LANES)


def _small_unpack(buf, like):
    flat = buf.reshape(-1)
    out, off = {}, 0
    for n in SMALL_NAMES:
        sz = int(np.prod(like[n].shape))
        out[n] = flat[off:off + sz].reshape(like[n].shape)
        off += sz
    return out


def _rope_tables(positions):
    freqs = ROPE_THETA ** (-(jnp.arange(QK_ROPE // 2, dtype=F32) * 2.0 / QK_ROPE))
    ang = positions.astype(F32)[:, None] * freqs
    return jnp.tile(jnp.cos(ang), (1, MLA_HEADS)), jnp.tile(jnp.sin(ang), (1, MLA_HEADS))


def _layer_weights(gathered, carriers):
    wd = {}
    for name, full in gathered.items():
        if name == 'w_in':
            wd[name] = {seg: W(full[lo:hi], c) for (seg, lo, hi), c in zip(WIN_SEGS, carriers[name])}
        else:
            wd[name] = W(full, carriers[name])
    return wd


F32_GRAD_WEIGHTS = ('rwkv_w2', 'rwkv_a2', 'rwkv_g2', 'rwkv_v2', 'conv_w')


def _make_carriers(gathered):
    gathered, carriers = dict(gathered), {}
    for name, full in gathered.items():
        if name == 'w_in':
            carriers[name] = tuple(jnp.zeros((hi - lo, D_MODEL), BF16) for _, lo, hi in WIN_SEGS)
        elif name == 'conv_w':
            gathered[name] = full.astype(BF16)
            carriers[name] = full - full.astype(BF16).astype(F32)
        else:
            carriers[name] = jnp.zeros(full.shape, F32 if name in F32_GRAD_WEIGHTS else BF16)
    return gathered, carriers


def _layer_small(small, ml):
    out = {}
    for n in SMALL_NAMES:
        l = ml if small[n].shape[0] == DEPTH else ml - 1
        if 0 <= l < small[n].shape[0]:
            out[n] = small[n][l]
    return out


def _after(value, tokens):
    return value + sum(tokens[1:], tokens[0])


def _train_step(x, positions, loss_target, weights, moms_m, moms_v):
    shards = {n: weights[n] for n in SHARDED}
    small = {n: weights[n] for n in SMALL_NAMES}
    me = 4 * lax.axis_index('x') + 2 * lax.axis_index('y') + lax.axis_index('c')
    cos, sin = _rope_tables(positions[0])

    units = [('pack', (0, 'in')), ('pack', (0, 'rest')), ('w_up', 0), ('w_down', 0),
             ('pack', 1), ('w_up', 1), ('w_down', 1)]
    first, later = units[0], units[1:]

    def tag(u):
        kind, key = u
        return f'{kind}_l{key[0]}{key[1]}' if isinstance(key, tuple) else f'{kind}_l{key}'

    def own_block(u):
        kind, key = u
        return _pack_shards(shards, key, BF16, True) if kind == 'pack' else shards[kind][key].astype(BF16)

    own = {u: own_block(u) for u in units}
    held = lax.optimization_barrier((all_gather_blocks(own[first]), *[own[u] for u in later]))
    blocks, gathers, tokens = {first: held[0]}, {}, []
    for u, mine in zip(later, held[1:]):
        own[u] = mine
        gathers[u], token = send_start(mine, False, f'gather_{tag(u)}_start')
        tokens.append(token)
    x0 = _after(x[0], tokens)

    def gathered_block(u, after):
        if u not in blocks:
            landed = send_wait(gathers[u], after, False, f'gather_{tag(u)}_wait')
            blocks[u] = lax.dynamic_update_slice(landed, own[u][None], (me,) + (0,) * own[u].ndim)
        return blocks[u]

    def packed_weights(key, after):
        return _make_carriers(_unpack_gathered(gathered_block(('pack', key), after), key))

    def mlp_weights(l, after):
        up = gathered_block(('w_up', l), after)
        down = gathered_block(('w_down', l), after).reshape(D_FF, D_MODEL)
        return _make_carriers({'w_up': up, 'w_down': down})

    def small_of(layer, part):
        owner = lambda n: 'mlp' if n == 'mlp_norm' else ('in' if n == 'attn_norm' else 'rest')
        parts = ('in', 'rest') if part == 'mix' else (part,)
        return {n: v for n, v in _layer_small(small, layer).items() if owner(n) in parts}

    def mlp(layer, gathered):
        return lambda c, s, xx: _mlp(layer, xx, _layer_weights(gathered, c), s)

    gathered, carriers = packed_weights((0, 'in'), None)
    (proj, x_skip), vjp_in0 = jax.vjp(lambda c, s, xx: _layer_in(0, xx, _layer_weights(gathered, c), s),
                                      carriers, small_of(0, 'in'), x0)
    gathered, carriers = packed_weights((0, 'rest'), proj[2])
    (h, v_first), vjp_rest0 = jax.vjp(
        lambda c, s, p, xs, w=gathered: _layer_rest(0, p, xs, None, _layer_weights(w, c), s, cos, sin),
        carriers, small_of(0, 'rest'), proj, x_skip)
    gathered, carriers = mlp_weights(0, h)
    h, vjp_mlp0 = jax.vjp(mlp(0, gathered), carriers, small_of(0, 'mlp'), h)
    gathered, carriers = packed_weights(1, h)
    h, vjp_mix1 = jax.vjp(
        lambda c, s, xx, vf, w=gathered: _layer(1, xx, vf, _layer_weights(w, c), s, cos, sin)[0],
        carriers, small_of(1, 'mix'), h, v_first)
    gathered, carriers = mlp_weights(1, h)
    y, vjp_mlp1 = jax.vjp(mlp(1, gathered), carriers, small_of(1, 'mlp'), h)

    dy, loss_parts = _loss_call(y, loss_target[0])

    shipped = {}

    def ship(u, to_send):
        handles, token = send_start(to_send, True, f'grads_{tag(u)}_start')
        shipped[u] = (handles, to_send)
        return token

    def ship_mlp(l, gw):
        down = gw['w_down'].reshape(N_DEV, D_FF // N_DEV, D_MODEL)
        return [ship(('w_up', l), gw['w_up']), ship(('w_down', l), down)]

    gw, gs_mlp1, d = vjp_mlp1(dy)
    d = _after(d, ship_mlp(1, gw))
    gw, gs_mix1, d, dvf = vjp_mix1(d)
    d = _after(d, [ship(('pack', 1), _pack_grads(gw, 1))])
    gw, gs_mlp0, d = vjp_mlp0(d)
    d = _after(d, ship_mlp(0, gw))
    gw, gs_rest0, dproj, dx_skip = vjp_rest0((d, dvf))
    token = ship(('pack', (0, 'rest')), _pack_grads(gw, (0, 'rest')))
    dproj = (dproj[0], dproj[1], _after(dproj[2], [token]), dproj[3], dproj[4])
    gw, gs_in0, g_x = vjp_in0((dproj, dx_skip))
    last_token = ship(first, _pack_grads(gw, (0, 'in')))
    gs0, gs1 = {**gs_in0, **gs_rest0, **gs_mlp0}, {**gs_mix1, **gs_mlp1}

    def arrived(u, after):
        handles, sent = shipped[u]
        landed = send_wait(handles, after, True, f'grads_{tag(u)}_wait')
        return sum_parts(landed, lax.dynamic_slice_in_dim(sent, me, 1, axis=0), f'sum_grads_{tag(u)}')

    after = _after(g_x, [last_token])
    grads = {}
    for u in later:
        if u[0] == 'pack':
            grads.update(_unpack_shards(arrived(u, after), u[1]))
        else:
            grads[u] = arrived(u, after)
    sharded_out = [{}, {}, {}, {}]

    def update(n):
        outs = adamw_weight(n, weights[n], moms_m[n], moms_v[n], [grads[(n, l)] for l in range(SHARDED[n][0])])
        for i in range(4):
            sharded_out[i][n] = outs[i]

    last_weight = 'w_in'
    for n in SHARDED:
        if n != last_weight:
            update(n)

    g_small = {}
    for n in SMALL_NAMES:
        per = [g[n] for g in (gs0, gs1) if n in g]
        g_small[n] = jnp.stack(per)
    small_grads = _after(_small_pack(g_small, jnp.sum(loss_parts)), [sharded_out[1][MLP_WEIGHTS[-1]][0, 0, 0] * 0.0])
    small_bufs = allreduce_adamw_small(
        small_grads, _small_pack(small), _small_pack({n: moms_m[n] for n in SMALL_NAMES}),
        _small_pack({n: moms_v[n] for n in SMALL_NAMES}))
    loss = small_bufs[0].reshape(-1)[SMALL_COUNT]
    small_out = [_small_unpack(b, small) for b in small_bufs]

    grads.update(_unpack_shards(arrived(first, small_bufs[0]), first[1]))
    update(last_weight)

    pick = lambda i: [sharded_out[i][n] if n in SHARDED else small_out[i][n] for n in WEIGHT_NAMES]
    return (loss, g_x[None], *pick(0), *pick(1), *pick(2), *pick(3))


def kernel(x, positions, attn_norm, w_in, mla_q_a_norm, mla_wq_b, mla_kv_a_norm, mla_wkv_b, mla_q_norm, mla_k_norm, mla_w_o, rwkv_mu, rwkv_w0, rwkv_w2, rwkv_a0, rwkv_a2, rwkv_g2, rwkv_k_k, rwkv_k_a, rwkv_r_k, rwkv_ln_w, rwkv_ln_b, rwkv_w_o, rwkv_v1, rwkv_v_mu, rwkv_v0, rwkv_v2, conv_w, conv_w_o, w_out, mlp_norm, w_up, w_down, loss_target, m_attn_norm, m_w_in, m_mla_q_a_norm, m_mla_wq_b, m_mla_kv_a_norm, m_mla_wkv_b, m_mla_q_norm, m_mla_k_norm, m_mla_w_o, m_rwkv_mu, m_rwkv_w0, m_rwkv_w2, m_rwkv_a0, m_rwkv_a2, m_rwkv_g2, m_rwkv_k_k, m_rwkv_k_a, m_rwkv_r_k, m_rwkv_ln_w, m_rwkv_ln_b, m_rwkv_w_o, m_rwkv_v1, m_rwkv_v_mu, m_rwkv_v0, m_rwkv_v2, m_conv_w, m_conv_w_o, m_w_out, m_mlp_norm, m_w_up, m_w_down, v_attn_norm, v_w_in, v_mla_q_a_norm, v_mla_wq_b, v_mla_kv_a_norm, v_mla_wkv_b, v_mla_q_norm, v_mla_k_norm, v_mla_w_o, v_rwkv_mu, v_rwkv_w0, v_rwkv_w2, v_rwkv_a0, v_rwkv_a2, v_rwkv_g2, v_rwkv_k_k, v_rwkv_k_a, v_rwkv_r_k, v_rwkv_ln_w, v_rwkv_ln_b, v_rwkv_w_o, v_rwkv_v1, v_rwkv_v_mu, v_rwkv_v0, v_rwkv_v2, v_conv_w, v_conv_w_o, v_w_out, v_mlp_norm, v_w_up, v_w_down):
    args = locals()
    weights = {n: args[n] for n in WEIGHT_NAMES}
    moms_m = {n: args['m_' + n] for n in WEIGHT_NAMES}
    moms_v = {n: args['v_' + n] for n in WEIGHT_NAMES}
    return _train_step(x, positions, loss_target, weights, moms_m, moms_v)
```

```python
import functools

import numpy as np
import jax
import jax.numpy as jnp
from jax import lax
from jax.experimental import pallas as pl
from jax.experimental.pallas import tpu as pltpu

F32 = jnp.float32
BF16 = jnp.bfloat16

N_DEV = 8
LANES = 128
D_MODEL = 1024
DEPTH = 2
MLA_HEADS = 8
QK_NOPE = 64
QK_ROPE = 32
QK_HEAD = QK_NOPE + QK_ROPE
V_HEAD = 64
Q_LORA = 384
KV_LORA = 256
ROPE_THETA = 10000.0
RW_HEADS = 4
RW_N = 64
RW_WIDTH = RW_HEADS * RW_N
MV_LORA = 32
GN_EPS = 64e-5
CONV_WIDTH = 256
D_FF = 4 * D_MODEL
NORM_EPS = 1e-6
ADAM_LR = 0.001
ADAM_B1 = 0.9
ADAM_B2 = 0.999
ADAM_EPS = 1e-08
ADAM_WD = 0.01
ADAM_STEP = 10

VMEM_LIMIT = 56 * 1024 * 1024
MESH = pl.DeviceIdType.MESH

WEIGHT_NAMES = ['attn_norm', 'w_in', 'mla_q_a_norm', 'mla_wq_b', 'mla_kv_a_norm', 'mla_wkv_b', 'mla_q_norm',
                'mla_k_norm', 'mla_w_o', 'rwkv_mu', 'rwkv_w0', 'rwkv_w2', 'rwkv_a0', 'rwkv_a2', 'rwkv_g2',
                'rwkv_k_k', 'rwkv_k_a', 'rwkv_r_k', 'rwkv_ln_w', 'rwkv_ln_b', 'rwkv_w_o', 'rwkv_v1',
                'rwkv_v_mu', 'rwkv_v0', 'rwkv_v2', 'conv_w', 'conv_w_o', 'w_out', 'mlp_norm', 'w_up', 'w_down']

SHARDED = {
    'w_in': (2, (1024, 5536), 1), 'mla_wq_b': (2, (384, 768), 1), 'mla_wkv_b': (2, (256, 1024), 1),
    'mla_w_o': (2, (512, 1024), 1), 'rwkv_w2': (2, (64, 256), 1), 'rwkv_a2': (2, (64, 256), 1),
    'rwkv_g2': (2, (128, 256), 1), 'rwkv_w_o': (2, (256, 1024), 1), 'conv_w': (2, (3, 256), 1),
    'conv_w_o': (2, (256, 1024), 1), 'w_out': (2, (1024, 1024), 0), 'w_up': (2, (1024, 4096), 1),
    'w_down': (2, (4096, 1024), 0), 'rwkv_v1': (1, (1024, 32), 0), 'rwkv_v2': (1, (32, 256), 1),
}
SMALL_NAMES = [n for n in WEIGHT_NAMES if n not in SHARDED]
TRANSPOSED = ('w_in',)
WIN_SEGS = (('gates', 0, 3072), ('cq', 3072, 3456), ('ckv', 3456, 3712), ('kpe', 3712, 3744), ('rkv', 3744, 4512),
            ('xwa', 4512, 4640), ('xg', 4640, 4768), ('conv', 4768, 5536))
PACK_ROW_MULT = 512


def _cparams(sem=None, **kw):
    if sem is not None:
        kw['dimension_semantics'] = sem
    return pltpu.CompilerParams(vmem_limit_bytes=VMEM_LIMIT, **kw)


def _pick(n, cands):
    for c in cands:
        if n % c == 0:
            return c
    raise ValueError(f'no tile for {n}')


def _mm_nn(a, b, add=None, name='mm_nn'):
    M, K = a.shape
    N = b.shape[1]
    tm = _pick(M, (1024, 512, 256, 128))
    tn = _pick(N, (512, 384, 256, 128))
    tk = _pick(K, (1024, 512, 384, 256, 128))
    nk = K // tk
    has_add = add is not None

    def body(*refs):
        if has_add:
            a_ref, b_ref, add_ref, o_ref, acc_ref = refs
        else:
            a_ref, b_ref, o_ref, acc_ref = refs
        kk = pl.program_id(2)
        part = jnp.dot(a_ref[...].astype(BF16), b_ref[...].astype(BF16), preferred_element_type=F32)

        @pl.when(kk == 0)
        def _():
            acc_ref[...] = part

        @pl.when(kk > 0)
        def _():
            acc_ref[...] += part

        @pl.when(kk == nk - 1)
        def _():
            if has_add:
                o_ref[...] = acc_ref[...] + add_ref[...]
            else:
                o_ref[...] = acc_ref[...]

    in_specs = [pl.BlockSpec((tm, tk), lambda i, j, k: (i, k)), pl.BlockSpec((tk, tn), lambda i, j, k: (k, j))]
    args = [a, b]
    if has_add:
        in_specs.append(pl.BlockSpec((tm, tn), lambda i, j, k: (i, j)))
        args.append(add)
    return pl.pallas_call(
        body, name=name, grid=(M // tm, N // tn, nk), in_specs=in_specs,
        out_specs=pl.BlockSpec((tm, tn), lambda i, j, k: (i, j)),
        out_shape=jax.ShapeDtypeStruct((M, N), F32),
        scratch_shapes=[pltpu.VMEM((tm, tn), F32)],
        compiler_params=_cparams(('parallel', 'parallel', 'arbitrary')),
    )(*args)


def _mm_nt(a, b, add=None, name='mm_nt'):
    M, N = a.shape
    blocked = b.ndim == 3
    K = b.shape[-2]
    tm = _pick(M, (1024, 512, 256, 128))
    tk = _pick(K, (512, 384, 256, 128))
    tn = N // N_DEV if blocked else _pick(N, (1024, 512, 384, 256, 128))
    nn = N // tn
    has_add = add is not None

    def body(*refs):
        if has_add:
            a_ref, b_ref, add_ref, o_ref, acc_ref = refs
        else:
            a_ref, b_ref, o_ref, acc_ref = refs
        kk = pl.program_id(2)
        part = lax.dot_general(a_ref[...].astype(BF16), b_ref[...].astype(BF16), (((1,), (1,)), ((), ())),
                               preferred_element_type=F32)

        @pl.when(kk == 0)
        def _():
            acc_ref[...] = part

        @pl.when(kk > 0)
        def _():
            acc_ref[...] += part

        @pl.when(kk == nn - 1)
        def _():
            if has_add:
                o_ref[...] = acc_ref[...] + add_ref[...]
            else:
                o_ref[...] = acc_ref[...]

    if blocked:
        b_spec = pl.BlockSpec((None, tk, tn), lambda i, j, k: (k, j, 0))
    else:
        b_spec = pl.BlockSpec((tk, tn), lambda i, j, k: (j, k))
    in_specs = [pl.BlockSpec((tm, tn), lambda i, j, k: (i, k)), b_spec]
    args = [a, b]
    if has_add:
        in_specs.append(pl.BlockSpec((tm, tk), lambda i, j, k: (i, j)))
        args.append(add)
    return pl.pallas_call(
        body, name=name, grid=(M // tm, K // tk, nn), in_specs=in_specs,
        out_specs=pl.BlockSpec((tm, tk), lambda i, j, k: (i, j)),
        out_shape=jax.ShapeDtypeStruct((M, K), F32),
        scratch_shapes=[pltpu.VMEM((tm, tk), F32)],
        compiler_params=_cparams(('parallel', 'parallel', 'arbitrary')),
    )(*args)


def _mm_tn(a, b, name='mm_tn', blocked=False):
    M, K = a.shape
    N = b.shape[1]
    tm = _pick(M, (1024, 512, 256, 128))
    tk = _pick(K, (512, 384, 256, 128))
    tn = N // N_DEV if blocked else _pick(N, (512, 384, 256, 128))
    nm = M // tm

    def body(a_ref, b_ref, o_ref, acc_ref):
        mm = pl.program_id(2)
        part = lax.dot_general(a_ref[...].astype(BF16), b_ref[...].astype(BF16), (((0,), (0,)), ((), ())),
                               preferred_element_type=F32)

        @pl.when(mm == 0)
        def _():
            acc_ref[...] = part

        @pl.when(mm > 0)
        def _():
            acc_ref[...] += part

        @pl.when(mm == nm - 1)
        def _():
            o_ref[...] = acc_ref[...].astype(BF16)

    if blocked:
        out_spec = pl.BlockSpec((None, tk, tn), lambda i, j, m: (j, i, 0))
        out_shape = jax.ShapeDtypeStruct((N_DEV, K, tn), BF16)
    else:
        out_spec = pl.BlockSpec((tk, tn), lambda i, j, m: (i, j))
        out_shape = jax.ShapeDtypeStruct((K, N), BF16)
    return pl.pallas_call(
        body, name=name, grid=(K // tk, N // tn, nm),
        in_specs=[pl.BlockSpec((tm, tk), lambda i, j, m: (m, i)), pl.BlockSpec((tm, tn), lambda i, j, m: (m, j))],
        out_specs=out_spec, out_shape=out_shape,
        scratch_shapes=[pltpu.VMEM((tk, tn), F32)],
        compiler_params=_cparams(('parallel', 'parallel', 'arbitrary')),
    )(a, b)


@functools.partial(jax.custom_vjp, nondiff_argnums=(4,))
def _linear_add(a, wb, wc, add, name):
    return _mm_nn(a, wb, add, name=name + '_f')


def _linear_add_fwd(a, wb, wc, add, name):
    return _mm_nn(a, wb, add, name=name + '_f'), (a, wb)


def _linear_add_bwd(name, res, dy):
    a, wb = res
    return _mm_nt(dy, wb, name=name + '_da'), None, _mm_tn(a, dy, name=name + '_dw'), dy


_linear_add.defvjp(_linear_add_fwd, _linear_add_bwd)


@functools.partial(jax.custom_vjp, nondiff_argnums=(3,))
def _multi_linear(a, wbs, wcs, name):
    return tuple(_mm_nn(a, wb, name=f'{name}_f{i}') for i, wb in enumerate(wbs))


def _multi_linear_fwd(a, wbs, wcs, name):
    return _multi_linear(a, wbs, wcs, name), (a, wbs)


def _multi_linear_bwd(name, res, dys):
    a, wbs = res
    da = None
    for i, (dy, wb) in enumerate(zip(dys, wbs)):
        da = _mm_nt(dy, wb, add=da, name=f'{name}_da{i}')
    dws = tuple(_mm_tn(a, dy, name=f'{name}_dw{i}') for i, dy in enumerate(dys))
    return da, None, dws


_multi_linear.defvjp(_multi_linear_fwd, _multi_linear_bwd)


class W:
    def __init__(self, b, c):
        self.b, self.c = b, c

    def map(self, fn):
        return W(fn(self.b), fn(self.c))


def _wcat(ws, axis):
    return W(jnp.concatenate([w.b for w in ws], axis), jnp.concatenate([w.c for w in ws], axis))


def linear(a, w, add=None, name='lin'):
    if add is None:
        return _multi_linear(a, (w.b,), (w.c,), name)[0]
    return _linear_add(a, w.b, w.c, add, name)


def ROW(diff=True, pieces=None):
    return ('row', diff, pieces)


def FULL(diff=True):
    return ('full', diff, None)


def _load_args(refs, specs):
    args, amap = [], []
    for i, (ref, (kind, diff, pieces)) in enumerate(zip(refs, specs)):
        if pieces is None:
            args.append(ref[...])
            amap.append((i, None))
        else:
            for (s, w) in pieces:
                args.append(ref[:, s:s + w])
                amap.append((i, (s, w)))
    return args, amap


def _stage_in_specs(ins, specs, tb):
    out = []
    for a, (kind, _, _) in zip(ins, specs):
        if kind == 'row':
            out.append(pl.BlockSpec((tb, a.shape[1]), lambda i: (i, 0)))
        else:
            out.append(pl.BlockSpec(a.shape, lambda i: (0, 0)))
    return out


def _stage_fwd(fn, ins, specs, out_widths, name, tb, out_dtype=F32):
    T = [a for a, s in zip(ins, specs) if s[0] == 'row'][0].shape[0]
    tb = min(tb, T)
    n_in = len(ins)

    def body(*refs):
        args, _ = _load_args(refs[:n_in], specs)
        outs = fn(*args)
        for o_ref, o in zip(refs[n_in:], outs):
            o_ref[...] = o.astype(out_dtype)

    return pl.pallas_call(
        body, name=name + '_f', grid=(T // tb,), in_specs=_stage_in_specs(ins, specs, tb),
        out_specs=[pl.BlockSpec((tb, w), lambda i: (i, 0)) for w in out_widths],
        out_shape=[jax.ShapeDtypeStruct((T, w), out_dtype) for w in out_widths],
        compiler_params=_cparams(('parallel',)),
    )(*ins)


def _stage_bwd(fn, ins, specs, out_widths, douts, name, tb, add_to_first=None):
    T = [a for a, s in zip(ins, specs) if s[0] == 'row'][0].shape[0]
    tb = min(tb, T)
    n_in, n_out = len(ins), len(out_widths)
    diff_inputs = [i for i, s in enumerate(specs) if s[1]]
    n_add = 0 if add_to_first is None else 1

    def body(*refs):
        in_refs, dout_refs = refs[:n_in], refs[n_in:n_in + n_out]
        g_refs = refs[n_in + n_out + n_add:]
        args, amap = _load_args(in_refs, specs)
        didx = [j for j, (i, _) in enumerate(amap) if specs[i][1]]

        def f(*dv):
            full = list(args)
            for j, v in zip(didx, dv):
                full[j] = v
            return tuple(fn(*full))

        _, vjp = jax.vjp(f, *[args[j] for j in didx])
        gs = vjp(tuple(d[...] for d in dout_refs))
        gmap = {j: g for j, g in zip(didx, gs)}
        first = pl.program_id(0) == 0
        for g_ref, i in zip(g_refs, diff_inputs):
            kind, _, pieces = specs[i]
            js = [j for j, (ii, _) in enumerate(amap) if ii == i]
            if kind == 'row':
                if pieces is None:
                    if n_add and i == diff_inputs[0]:
                        g_ref[...] = gmap[js[0]] + refs[n_in + n_out][...]
                    else:
                        g_ref[...] = gmap[js[0]]
                else:
                    if sum(w for _, w in pieces) != ins[i].shape[1]:
                        g_ref[...] = jnp.zeros(g_ref.shape, F32)
                    for j in js:
                        s, w = amap[j][1]
                        g_ref[:, s:s + w] = gmap[j]
            else:
                @pl.when(first)
                def _(g_ref=g_ref):
                    g_ref[...] = jnp.zeros(g_ref.shape, F32)

                g_ref[...] += gmap[js[0]]

    in_specs = _stage_in_specs(ins, specs, tb) + [pl.BlockSpec((tb, w), lambda i: (i, 0)) for w in out_widths]
    extra = []
    if n_add:
        in_specs.append(pl.BlockSpec((tb, add_to_first.shape[1]), lambda i: (i, 0)))
        extra.append(add_to_first)
    out_specs, out_shape = [], []
    for i in diff_inputs:
        a = ins[i]
        if specs[i][0] == 'row':
            out_specs.append(pl.BlockSpec((tb, a.shape[1]), lambda i: (i, 0)))
        else:
            out_specs.append(pl.BlockSpec(a.shape, lambda i: (0, 0)))
        out_shape.append(jax.ShapeDtypeStruct(a.shape, F32))
    return pl.pallas_call(
        body, name=name + '_b', grid=(T // tb,), in_specs=in_specs, out_specs=out_specs, out_shape=out_shape,
        compiler_params=_cparams(('arbitrary',)),
    )(*ins, *douts, *extra)


def stage_op(fn, specs, out_widths, name, tb=256):
    n = len(specs)
    diff_inputs = [i for i, s in enumerate(specs) if s[1]]

    @jax.custom_vjp
    def op(*ins):
        return tuple(_stage_fwd(fn, ins, specs, out_widths, name, tb))

    def op_fwd(*ins):
        return op(*ins), ins

    def op_bwd(ins, douts):
        gs = _stage_bwd(fn, ins, specs, out_widths, douts, name, tb)
        res = [None] * n
        for i, g in zip(diff_inputs, gs):
            res[i] = g
        return tuple(res)

    op.defvjp(op_fwd, op_bwd)
    return op


@jax.custom_vjp
def bdot(x, w):
    return jnp.dot(x.astype(BF16), w.astype(BF16), preferred_element_type=F32)


def _bdot_fwd(x, w):
    return bdot(x, w), (x, w)


def _bdot_bwd(res, dy):
    x, w = res
    dyb = dy.astype(BF16)
    dx = lax.dot_general(dyb, w.astype(BF16), (((1,), (1,)), ((), ())), preferred_element_type=F32)
    dw = lax.dot_general(x.astype(BF16), dyb, (((0,), (0,)), ((), ())), preferred_element_type=F32)
    return dx, dw


bdot.defvjp(_bdot_fwd, _bdot_bwd)


def _sdot_raw(x, c):
    hi = x.astype(BF16)
    r1 = x - hi.astype(F32)
    mid = r1.astype(BF16)
    lo = (r1 - mid.astype(F32)).astype(BF16)
    d = lambda u: jnp.dot(u, c, preferred_element_type=F32)
    return d(hi) + d(mid) + d(lo)


@jax.custom_vjp
def sdot(x, c, ct):
    return _sdot_raw(x, c)


def _sdot_fwd(x, c, ct):
    return _sdot_raw(x, c), (c, ct)


def _sdot_bwd(res, dy):
    c, ct = res
    return _sdot_raw(dy, ct), None, None


sdot.defvjp(_sdot_fwd, _sdot_bwd)


def _sigmoid(x):
    return 1.0 / (1.0 + jnp.exp(-x))


def _rms(x, g):
    return x * lax.rsqrt(jnp.mean(x * x, axis=-1, keepdims=True) + NORM_EPS) * g


def _mm_up_relu2(a, b, name):
    M, K = a.shape
    tn = b.shape[2]
    N = N_DEV * tn
    tm = _pick(M, (1024, 512, 256, 128))

    def body(a_ref, b_ref, u_ref, act_ref):
        u = jnp.dot(a_ref[...], b_ref[...], preferred_element_type=F32)
        r = jnp.maximum(u, 0.0)
        u_ref[...] = u.astype(BF16)
        act_ref[...] = (r * r).astype(BF16)

    out = pl.BlockSpec((tm, tn), lambda i, j: (i, j))
    sh = jax.ShapeDtypeStruct((M, N), BF16)
    return pl.pallas_call(
        body, name=name, grid=(M // tm, N // tn),
        in_specs=[pl.BlockSpec((tm, K), lambda i, j: (i, 0)), pl.BlockSpec((None, K, tn), lambda i, j: (j, 0, 0))],
        out_specs=[out, out], out_shape=[sh, sh], compiler_params=_cparams(('parallel', 'parallel')),
    )(a, b)


def _mm_down_bwd(dy, b, u, name):
    M, N = dy.shape
    K = b.shape[0]
    tm = _pick(M, (1024, 512, 256, 128))
    tk = _pick(K, (512, 256, 128))

    def body(dy_ref, b_ref, u_ref, du_ref):
        d = lax.dot_general(dy_ref[...].astype(BF16), b_ref[...], (((1,), (1,)), ((), ())),
                            preferred_element_type=F32)
        du_ref[...] = (d * (2.0 * jnp.maximum(u_ref[...].astype(F32), 0.0))).astype(BF16)

    blk = pl.BlockSpec((tm, tk), lambda i, j: (i, j))
    return pl.pallas_call(
        body, name=name, grid=(M // tm, K // tk),
        in_specs=[pl.BlockSpec((tm, N), lambda i, j: (i, 0)), pl.BlockSpec((tk, N), lambda i, j: (j, 0)), blk],
        out_specs=blk, out_shape=jax.ShapeDtypeStruct((M, K), BF16),
        compiler_params=_cparams(('parallel', 'parallel')),
    )(dy, b, u)


_RMS_SPECS = [ROW(), FULL()]
_rms_fn = lambda xv, gv: (_rms(xv, gv),)


@functools.partial(jax.custom_vjp, nondiff_argnums=(6,))
def mlp_block(x, g, wup_b, wup_c, wdown_b, wdown_c, name):
    return _mlp_fwd(x, g, wup_b, wup_c, wdown_b, wdown_c, name)[0]


def _mlp_fwd(x, g, wup_b, wup_c, wdown_b, wdown_c, name):
    h = _stage_fwd(_rms_fn, (x, g), _RMS_SPECS, [x.shape[1]], name + '_norm', 256, out_dtype=BF16)[0]
    u, act = _mm_up_relu2(h, wup_b, name + '_up')
    y = _mm_nn(act, wdown_b, add=x, name=name + '_down')
    return y, (x, g, h, u, act, wup_b, wdown_b)


def _mlp_bwd(name, res, dy):
    x, g, h, u, act, wup_b, wdown_b = res
    du = _mm_down_bwd(dy, wdown_b, u, name + '_down_da')
    dwdown = _mm_tn(act, dy, name=name + '_down_dw')
    dh = _mm_nt(du, wup_b, name=name + '_up_da')
    dwup = _mm_tn(h, du, name=name + '_up_dw', blocked=True)
    dx, dg = _stage_bwd(_rms_fn, (x, g), _RMS_SPECS, [x.shape[1]], (dh,), name + '_norm', 256, add_to_first=dy)
    return dx, dg, None, dwup, None, dwdown


mlp_block.defvjp(_mlp_fwd, _mlp_bwd)


@functools.partial(jax.custom_vjp, nondiff_argnums=(4, 5, 6))
def _norm_projections(x, g, wbs, wcs, transposed, skip, name):
    return _norm_projections_fwd(x, g, wbs, wcs, transposed, skip, name)[0]


def _norm_projections_fwd(x, g, wbs, wcs, transposed, skip, name):
    h = _stage_fwd(_rms_fn, (x, g), _RMS_SPECS, [x.shape[1]], name + '_norm', 256, out_dtype=BF16)[0]
    mm = _mm_nt if transposed else _mm_nn
    outs = tuple(mm(h, wb, name=f'{name}_f{i}') for i, wb in enumerate(wbs))
    return ((outs, x) if skip else outs), (x, g, h, wbs)


def _norm_projections_bwd(transposed, skip, name, res, cts):
    x, g, h, wbs = res
    dys, dx_skip = cts if skip else (cts, None)
    dh = None
    for i, (dy, wb) in enumerate(zip(dys, wbs)):
        dh = (_mm_nn if transposed else _mm_nt)(dy, wb, add=dh, name=f'{name}_da{i}')
    if transposed:
        dws = tuple(_mm_tn(dy, h, name=f'{name}_dw{i}') for i, dy in enumerate(dys))
    else:
        dws = tuple(_mm_tn(h, dy, name=f'{name}_dw{i}') for i, dy in enumerate(dys))
    dx, dg = _stage_bwd(_rms_fn, (x, g), _RMS_SPECS, [x.shape[1]], (dh,), name + '_norm', 256, add_to_first=dx_skip)
    return dx, dg, None, dws


_norm_projections.defvjp(_norm_projections_fwd, _norm_projections_bwd)


def norm_projections(x, g, ws, name, transposed=False, skip=False):
    return _norm_projections(x, g.reshape(1, -1), tuple(w.b for w in ws), tuple(w.c for w in ws), transposed, skip,
                             name)


def _shift_down(x, rows):
    return jnp.where(rows == 0, 0.0, pltpu.roll(x, 1, 0))


def _shift_up(x, rows, T):
    return jnp.where(rows == T - 1, 0.0, pltpu.roll(x, T - 1, 0))


def _tshift_fwd_call(x, mu, name):
    T, C = x.shape

    def body(x_ref, mu_ref, o_ref):
        xv = x_ref[...]
        rows = lax.broadcasted_iota(jnp.int32, xv.shape, 0)
        o_ref[...] = xv + (_shift_down(xv, rows) - xv) * mu_ref[...]

    return pl.pallas_call(
        body, name=name + '_f', grid=(C // LANES,),
        in_specs=[pl.BlockSpec((T, LANES), lambda j: (0, j)), pl.BlockSpec((1, LANES), lambda j: (0, j))],
        out_specs=pl.BlockSpec((T, LANES), lambda j: (0, j)), out_shape=jax.ShapeDtypeStruct((T, C), F32),
        compiler_params=_cparams(('parallel',)),
    )(x, mu)


def _tshift_bwd_call(x, mu, dy, name):
    T, C = x.shape

    def body(x_ref, mu_ref, dy_ref, dx_ref, dmu_ref):
        xv, d = x_ref[...], dy_ref[...]
        rows = lax.broadcasted_iota(jnp.int32, xv.shape, 0)
        z = d * mu_ref[...]
        dx_ref[...] = d - z + _shift_up(z, rows, T)
        dmu_ref[...] = jnp.sum(d * (_shift_down(xv, rows) - xv), axis=0, keepdims=True)

    return pl.pallas_call(
        body, name=name + '_b', grid=(C // LANES,),
        in_specs=[pl.BlockSpec((T, LANES), lambda j: (0, j)), pl.BlockSpec((1, LANES), lambda j: (0, j)),
                  pl.BlockSpec((T, LANES), lambda j: (0, j))],
        out_specs=[pl.BlockSpec((T, LANES), lambda j: (0, j)), pl.BlockSpec((1, LANES), lambda j: (0, j))],
        out_shape=[jax.ShapeDtypeStruct((T, C), F32), jax.ShapeDtypeStruct((1, C), F32)],
        compiler_params=_cparams(('parallel',)),
    )(x, mu, dy)


@functools.partial(jax.custom_vjp, nondiff_argnums=(2,))
def token_shift_mix(x, mu, name):
    return _tshift_fwd_call(x, mu, name)


def _tsm_fwd(x, mu, name):
    return _tshift_fwd_call(x, mu, name), (x, mu)


def _tsm_bwd(name, res, dy):
    x, mu = res
    dx, dmu = _tshift_bwd_call(x, mu, dy, name)
    return dx, dmu


token_shift_mix.defvjp(_tsm_fwd, _tsm_bwd)


def _conv_specs(T):
    nb = CONV_WIDTH // LANES
    return [pl.BlockSpec((T, LANES), lambda j: (0, j)), pl.BlockSpec((T, LANES), lambda j: (0, nb + j)),
            pl.BlockSpec((T, LANES), lambda j: (0, 2 * nb + j)), pl.BlockSpec((3, LANES), lambda j: (0, j))]


def _conv_fwd_call(cv, w, name):
    T = cv.shape[0]

    def body(b_ref, c_ref, x_ref, w_ref, o_ref):
        u = c_ref[...] * x_ref[...]
        rows = lax.broadcasted_iota(jnp.int32, u.shape, 0)
        u1 = _shift_down(u, rows)
        u2 = _shift_down(u1, rows)
        o_ref[...] = b_ref[...] * (w_ref[0:1, :] * u2 + w_ref[1:2, :] * u1 + w_ref[2:3, :] * u)

    return pl.pallas_call(
        body, name=name + '_f', grid=(CONV_WIDTH // LANES,), in_specs=_conv_specs(T),
        out_specs=pl.BlockSpec((T, LANES), lambda j: (0, j)),
        out_shape=jax.ShapeDtypeStruct((T, CONV_WIDTH), F32), compiler_params=_cparams(('parallel',)),
    )(cv, cv, cv, w)


def _conv_bwd_call(cv, w, do, name):
    T = cv.shape[0]

    def body(b_ref, c_ref, x_ref, w_ref, do_ref, db_ref, dc_ref, dx_ref, dw_ref):
        c, x, d = c_ref[...], x_ref[...], do_ref[...]
        u = c * x
        rows = lax.broadcasted_iota(jnp.int32, u.shape, 0)
        u1 = _shift_down(u, rows)
        u2 = _shift_down(u1, rows)
        w0, w1, w2 = w_ref[0:1, :], w_ref[1:2, :], w_ref[2:3, :]
        db_ref[...] = d * (w0 * u2 + w1 * u1 + w2 * u)
        dy = d * b_ref[...]
        dy1 = _shift_up(dy, rows, T)
        dy2 = _shift_up(dy1, rows, T)
        du = w2 * dy + w1 * dy1 + w0 * dy2
        dc_ref[...] = du * x
        dx_ref[...] = du * c
        dw_ref[0:1, :] = jnp.sum(dy * u2, axis=0, keepdims=True)
        dw_ref[1:2, :] = jnp.sum(dy * u1, axis=0, keepdims=True)
        dw_ref[2:3, :] = jnp.sum(dy * u, axis=0, keepdims=True)

    blk = pl.BlockSpec((T, LANES), lambda j: (0, j))
    sh = jax.ShapeDtypeStruct((T, CONV_WIDTH), F32)
    return pl.pallas_call(
        body, name=name + '_b', grid=(CONV_WIDTH // LANES,), in_specs=_conv_specs(T) + [blk],
        out_specs=[blk, blk, blk, pl.BlockSpec((3, LANES), lambda j: (0, j))],
        out_shape=[sh, sh, sh, jax.ShapeDtypeStruct((3, CONV_WIDTH), F32)],
        compiler_params=_cparams(('parallel',)),
    )(cv, cv, cv, w, do)


@functools.partial(jax.custom_vjp, nondiff_argnums=(2,))
def short_conv(cv, w, name):
    return _conv_fwd_call(cv, w, name)


def _sc_fwd(cv, w, name):
    return _conv_fwd_call(cv, w, name), (cv, w)


def _sc_bwd(name, res, do):
    cv, w = res
    db, dc, dx, dw = _conv_bwd_call(cv, w, do, name)
    return jnp.concatenate([db, dc, dx], axis=1), dw


short_conv.defvjp(_sc_fwd, _sc_bwd)


ATT_SCALE = QK_HEAD ** -0.5
NPAIR = MLA_HEADS // 2


def _att_bq(T):
    return min(256, T)


def _att_masks(pair, j):
    lane = lax.broadcasted_iota(jnp.int32, (1, LANES), 1)
    mask_n = (lane // QK_NOPE) == j
    mask_r = (lane // (QK_ROPE // 2)) == (2 * pair + j)
    return mask_n, mask_r


def _att_probs(qcat, kcat, row0, stop):
    s = lax.dot_general(qcat, kcat, (((1,), (1,)), ((), ())), preferred_element_type=F32) * ATT_SCALE
    r = row0 + lax.broadcasted_iota(jnp.int32, s.shape, 0)
    c = lax.broadcasted_iota(jnp.int32, s.shape, 1)
    s = jnp.where(c <= r, s, -jnp.inf)
    e = jnp.exp(s - jnp.max(s, axis=-1, keepdims=True))
    return e / jnp.sum(e, axis=-1, keepdims=True)


def _att_in_specs(T):
    blk = lambda f: pl.BlockSpec((T, LANES), f)
    return [blk(lambda p: (0, p)), blk(lambda p: (0, 0)), blk(lambda p: (0, 0)),
            blk(lambda p: (0, p)), blk(lambda p: (0, 0)), blk(lambda p: (0, 0)), blk(lambda p: (0, p))]


def _att_fwd_call(qn, q1, q2, kn, k1, k2, v, name):
    T = qn.shape[0]
    bq = _att_bq(T)

    def body(qn_ref, q1_ref, q2_ref, kn_ref, k1_ref, k2_ref, v_ref, o_ref):
        pair = pl.program_id(0)
        for i in range(T // bq):
            r0, stop = i * bq, (i + 1) * bq
            kcat = jnp.concatenate([kn_ref[0:stop, :], k1_ref[0:stop, :], k2_ref[0:stop, :]], axis=1).astype(BF16)
            vb = v_ref[0:stop, :].astype(BF16)
            outs = []
            for j in range(2):
                mask_n, mask_r = _att_masks(pair, j)
                qcat = jnp.concatenate([jnp.where(mask_n, qn_ref[r0:stop, :], 0.0),
                                        jnp.where(mask_r, q1_ref[r0:stop, :], 0.0),
                                        jnp.where(mask_r, q2_ref[r0:stop, :], 0.0)], axis=1).astype(BF16)
                p = _att_probs(qcat, kcat, r0, stop)
                outs.append(jnp.dot(p.astype(BF16), vb, preferred_element_type=F32))
            mask_n0, _ = _att_masks(pair, 0)
            o_ref[r0:stop, :] = jnp.where(mask_n0, outs[0], outs[1])

    return pl.pallas_call(
        body, name=name + '_f', grid=(NPAIR,), in_specs=_att_in_specs(T),
        out_specs=pl.BlockSpec((T, LANES), lambda p: (0, p)),
        out_shape=jax.ShapeDtypeStruct((T, MLA_HEADS * V_HEAD), F32), compiler_params=_cparams(('parallel',)),
    )(qn, q1, q2, kn, k1, k2, v)


def _att_bwd_call(qn, q1, q2, kn, k1, k2, v, o, do, name):
    T = qn.shape[0]
    bq = _att_bq(T)

    def body(qn_ref, q1_ref, q2_ref, kn_ref, k1_ref, k2_ref, v_ref, o_ref, do_ref,
             dqn_ref, dq1_ref, dq2_ref, dkn_ref, dk1_ref, dk2_ref, dv_ref, dk_acc, dv_acc):
        pair = pl.program_id(0)

        @pl.when(pair == 0)
        def _():
            dq1_ref[...] = jnp.zeros(dq1_ref.shape, F32)
            dq2_ref[...] = jnp.zeros(dq2_ref.shape, F32)
            dk1_ref[...] = jnp.zeros(dk1_ref.shape, F32)
            dk2_ref[...] = jnp.zeros(dk2_ref.shape, F32)

        dk_acc[...] = jnp.zeros(dk_acc.shape, F32)
        dv_acc[...] = jnp.zeros(dv_acc.shape, F32)
        for i in range(T // bq):
            r0, stop = i * bq, (i + 1) * bq
            kcat = jnp.concatenate([kn_ref[0:stop, :], k1_ref[0:stop, :], k2_ref[0:stop, :]], axis=1).astype(BF16)
            vb = v_ref[0:stop, :].astype(BF16)
            dqn = jnp.zeros((bq, LANES), F32)
            for j in range(2):
                mask_n, mask_r = _att_masks(pair, j)
                qcat = jnp.concatenate([jnp.where(mask_n, qn_ref[r0:stop, :], 0.0),
                                        jnp.where(mask_r, q1_ref[r0:stop, :], 0.0),
                                        jnp.where(mask_r, q2_ref[r0:stop, :], 0.0)], axis=1).astype(BF16)
                p = _att_probs(qcat, kcat, r0, stop)
                dom = jnp.where(mask_n, do_ref[r0:stop, :], 0.0)
                delta = jnp.sum(dom * o_ref[r0:stop, :], axis=-1, keepdims=True)
                domb = dom.astype(BF16)
                dp = lax.dot_general(domb, vb, (((1,), (1,)), ((), ())), preferred_element_type=F32)
                ds = (p * (dp - delta) * ATT_SCALE).astype(BF16)
                dqc = jnp.dot(ds, kcat, preferred_element_type=F32)
                dqn = dqn + jnp.where(mask_n, dqc[:, 0:LANES], 0.0)
                dq1_ref[r0:stop, :] += jnp.where(mask_r, dqc[:, LANES:2 * LANES], 0.0)
                dq2_ref[r0:stop, :] += jnp.where(mask_r, dqc[:, 2 * LANES:3 * LANES], 0.0)
                dk_acc[0:stop, :] += lax.dot_general(ds, qcat, (((0,), (0,)), ((), ())),
                                                     preferred_element_type=F32)
                dv_acc[0:stop, :] += lax.dot_general(p.astype(BF16), domb, (((0,), (0,)), ((), ())),
                                                     preferred_element_type=F32)
            dqn_ref[r0:stop, :] = dqn
        dkn_ref[...] = dk_acc[:, 0:LANES]
        dk1_ref[...] += dk_acc[:, LANES:2 * LANES]
        dk2_ref[...] += dk_acc[:, 2 * LANES:3 * LANES]
        dv_ref[...] = dv_acc[...]

    per_pair = pl.BlockSpec((T, LANES), lambda p: (0, p))
    shared = pl.BlockSpec((T, LANES), lambda p: (0, 0))
    wide = jax.ShapeDtypeStruct((T, MLA_HEADS * QK_NOPE), F32)
    narrow = jax.ShapeDtypeStruct((T, LANES), F32)
    return pl.pallas_call(
        body, name=name + '_b', grid=(NPAIR,), in_specs=_att_in_specs(T) + [per_pair, per_pair],
        out_specs=[per_pair, shared, shared, per_pair, shared, shared, per_pair],
        out_shape=[wide, narrow, narrow, wide, narrow, narrow, wide],
        scratch_shapes=[pltpu.VMEM((T, 3 * LANES), F32), pltpu.VMEM((T, LANES), F32)],
        compiler_params=_cparams(('arbitrary',)),
    )(qn, q1, q2, kn, k1, k2, v, o, do)


@functools.partial(jax.custom_vjp, nondiff_argnums=(7,))
def attention(qn, q1, q2, kn, k1, k2, v, name):
    return _att_fwd_call(qn, q1, q2, kn, k1, k2, v, name)


def _attn_fwd(qn, q1, q2, kn, k1, k2, v, name):
    o = _att_fwd_call(qn, q1, q2, kn, k1, k2, v, name)
    return o, (qn, q1, q2, kn, k1, k2, v, o)


def _attn_bwd(name, res, do):
    return tuple(_att_bwd_call(*res, do, name))


attention.defvjp(_attn_fwd, _attn_bwd)


SCAN_CHUNK = 64
SCAN_UNROLL = 8


def _block_ones(n, seg):
    i = np.arange(n)
    return (i[:, None] // seg == i[None, :] // seg).astype(np.float32)


def _scan_diag():
    i = np.arange(RW_WIDTH)
    return jnp.asarray((np.arange(RW_N)[:, None] == (i[None, :] % RW_N)).astype(np.float32))


def _head_rowsum(x):
    low = lax.broadcasted_iota(jnp.int32, (1, LANES), 1) < RW_N
    tiles = []
    for j in range(RW_WIDTH // LANES):
        xt = x[:, j * LANES:(j + 1) * LANES]
        x0 = jnp.where(low, xt, 0.0)
        s0 = jnp.sum(x0, axis=-1, keepdims=True)
        s1 = jnp.sum(xt - x0, axis=-1, keepdims=True)
        tiles.append(jnp.where(low, s0, s1))
    return jnp.concatenate(tiles, axis=1)


def _unrolled_loop(n, step, init):
    def body(i, carry):
        for j in range(SCAN_UNROLL):
            carry = step(i * SCAN_UNROLL + j, carry)
        return carry
    return lax.fori_loop(0, n // SCAN_UNROLL, body, init)


def _scan_fwd_call(r, w, k, v, a, b, name):
    T = r.shape[0]
    tc = min(SCAN_CHUNK, T)
    dg = _scan_diag()

    def body(r_ref, w_ref, k_ref, v_ref, a_ref, b_ref, dg_ref, y_ref, st_ref, sa_ref, vc_ref, s_ref):
        @pl.when(pl.program_id(0) == 0)
        def _():
            s_ref[...] = jnp.zeros(s_ref.shape, F32)

        dgv = dg_ref[...]
        readout = lambda s, t: jnp.sum(_head_rowsum(s * r_ref[t]) * dgv, axis=0, keepdims=True)

        def step(t, carry):
            s, vcol = carry
            st_ref[t] = s
            vc_ref[t] = vcol
            sa = _head_rowsum(s * a_ref[t])
            sa_ref[t] = sa
            prev = jnp.maximum(t - 1, 0)
            y_ref[prev] = readout(s, prev)
            vcol_next = _head_rowsum(dgv * v_ref[jnp.minimum(t + 1, tc - 1)])
            sn = s * w_ref[t] + sa * b_ref[t] + vcol * k_ref[t]
            return sn, vcol_next

        s_end, _ = _unrolled_loop(tc, step, (s_ref[...], _head_rowsum(dgv * v_ref[0])))
        y_ref[tc - 1] = readout(s_end, tc - 1)
        s_ref[...] = s_end

    vec = pl.BlockSpec((tc, 1, RW_WIDTH), lambda i: (i, 0, 0))
    mat = pl.BlockSpec((tc, RW_N, RW_WIDTH), lambda i: (i, 0, 0))
    msh = jax.ShapeDtypeStruct((T, RW_N, RW_WIDTH), F32)
    return pl.pallas_call(
        body, name=name + '_f', grid=(T // tc,),
        in_specs=[vec] * 6 + [pl.BlockSpec((RW_N, RW_WIDTH), lambda i: (0, 0))],
        out_specs=[vec, mat, mat, mat],
        out_shape=[jax.ShapeDtypeStruct((T, 1, RW_WIDTH), F32), msh, msh, msh],
        scratch_shapes=[pltpu.VMEM((RW_N, RW_WIDTH), F32)],
        compiler_params=_cparams(('arbitrary',)),
    )(r, w, k, v, a, b, dg)


def _scan_bwd_call(r, w, k, a, b, st, sa_all, vc_all, dy, name):
    T = r.shape[0]
    tc = min(SCAN_CHUNK, T)
    nt = T // tc
    dg = _scan_diag()

    def body(r_ref, w_ref, k_ref, a_ref, b_ref, st_ref, sa_ref, vc_ref, dy_ref, dg_ref,
             dr_ref, dw_ref, dk_ref, dv_ref, da_ref, db_ref, ds_ref):
        @pl.when(pl.program_id(0) == 0)
        def _():
            ds_ref[...] = jnp.zeros(ds_ref.shape, F32)

        dgv = dg_ref[...]
        colsum = lambda x: jnp.sum(x, axis=0, keepdims=True)

        def step(i, carry):
            ds, dycol = carry
            t = tc - 1 - i
            sp = st_ref[t]
            rt, wt, kt, at, bt = r_ref[t], w_ref[t], k_ref[t], a_ref[t], b_ref[t]
            ds = ds + dycol * rt
            dsa = _head_rowsum(ds * bt)
            sa, vcol = sa_ref[t], vc_ref[t]
            dycol_next = _head_rowsum(dgv * dy_ref[jnp.maximum(t - 1, 0)])
            sn = sp * wt + sa * bt + vcol * kt
            dr_ref[t] = colsum(sn * dycol)
            dk_ref[t] = colsum(ds * vcol)
            db_ref[t] = colsum(ds * sa)
            dw_ref[t] = colsum(ds * sp)
            dv_ref[t] = colsum(_head_rowsum(ds * kt) * dgv)
            da_ref[t] = colsum(sp * dsa)
            return ds * wt + dsa * at, dycol_next

        ds_end, _ = _unrolled_loop(tc, step, (ds_ref[...], _head_rowsum(dgv * dy_ref[tc - 1])))
        ds_ref[...] = ds_end

    vec = pl.BlockSpec((tc, 1, RW_WIDTH), lambda i: (nt - 1 - i, 0, 0))
    mat = pl.BlockSpec((tc, RW_N, RW_WIDTH), lambda i: (nt - 1 - i, 0, 0))
    vsh = jax.ShapeDtypeStruct((T, 1, RW_WIDTH), F32)
    return pl.pallas_call(
        body, name=name + '_b', grid=(nt,),
        in_specs=[vec] * 5 + [mat] * 3 + [vec, pl.BlockSpec((RW_N, RW_WIDTH), lambda i: (0, 0))],
        out_specs=[vec] * 6, out_shape=[vsh] * 6,
        scratch_shapes=[pltpu.VMEM((RW_N, RW_WIDTH), F32)],
        compiler_params=_cparams(('arbitrary',)),
    )(r, w, k, a, b, st, sa_all, vc_all, dy, dg)


@functools.partial(jax.custom_vjp, nondiff_argnums=(6,))
def wkv7(r, w, k, v, a, b, name):
    return _scan_fwd_call(r, w, k, v, a, b, name)[0]


def _wkv7_fwd(r, w, k, v, a, b, name):
    y, st, sa_all, vc_all = _scan_fwd_call(r, w, k, v, a, b, name)
    return y, (r, w, k, a, b, st, sa_all, vc_all)


def _wkv7_bwd(name, res, dy):
    return tuple(_scan_bwd_call(*res, dy, name))


wkv7.defvjp(_wkv7_fwd, _wkv7_bwd)


def _np_bf16(a):
    return jnp.asarray(a, BF16)


def _mla_consts():
    seg_n = (np.arange(512)[:, None] // QK_NOPE == np.arange(LANES)[None, :]).astype(np.float32)
    seg_r = (np.arange(LANES)[:, None] // 16 == np.arange(LANES)[None, :]).astype(np.float32)
    e1 = np.zeros((LANES, LANES), np.float32)
    e2 = np.zeros((LANES, LANES), np.float32)
    for h in range(MLA_HEADS):
        for i in range(16):
            e1[i, h * 16 + i] = 1.0
            e2[16 + i, h * 16 + i] = 1.0
    mats = [seg_n, seg_n.T, seg_r, seg_r.T, e1, e1.T, e2, e2.T]
    return [_np_bf16(m) for m in mats]


def _qk_prep_fn(qn, q1, q2, kn, kx, cos, sin, gqn, gq1, gq2, gkn, gk1, gk2,
                seg_n, seg_nt, seg_r, seg_rt, e1, e1t, e2, e2t):
    def normrope(xn, x1, x2, gn, g1, g2):
        ss = sdot(xn * xn, seg_n, seg_nt) + sdot(x1 * x1, seg_r, seg_rt) + sdot(x2 * x2, seg_r, seg_rt)
        inv = lax.rsqrt(ss * (1.0 / QK_HEAD) + NORM_EPS)
        inv_n = sdot(inv, seg_nt, seg_n)
        inv_r = sdot(inv, seg_rt, seg_r)
        y1 = x1 * inv_r * g1
        y2 = x2 * inv_r * g2
        return xn * inv_n * gn, y1 * cos - y2 * sin, y1 * sin + y2 * cos

    k1 = sdot(kx, e1, e1t)
    k2 = sdot(kx, e2, e2t)
    return normrope(qn, q1, q2, gqn, gq1, gq2) + normrope(kn, k1, k2, gkn, gk1, gk2)


def _rwkv_prep_fn(vres):
    def fn(r, k, v, xg, xwa, kx, *rest):
        if vres:
            vfirst, w0, a0, k_k, k_a, w2p, a2p, g2, v0, v2p, bm = rest
        else:
            w0, a0, k_k, k_a, w2p, a2p, g2, bm = rest
        z = w0 + bdot(jnp.tanh(xwa), w2p)
        nz = -z
        softplus = jnp.maximum(nz, 0.0) + jnp.log(1.0 + jnp.exp(-jnp.abs(nz)))
        decay = jnp.exp(-jnp.exp(-softplus - 0.5))
        a = _sigmoid(a0 + bdot(xwa, a2p))
        g = bdot(_sigmoid(xg), g2)
        if vres:
            vv = v + (vfirst - v) * _sigmoid(v0 + bdot(kx, v2p))
        else:
            vv = v
        kkr = k * k_k
        kk = kkr / jnp.maximum(jnp.sqrt(sdot(kkr * kkr, bm, bm)), 1e-12)
        k2 = k * (1.0 + (a - 1.0) * k_a)
        return r * 1.0, decay, k2, vv, -kk, kk * a, g
    return fn


def _rwkv_post_fn(y, r, k2, vv, g, ln_w, ln_b, rk, bm):
    inv_n = 1.0 / RW_N
    mean = sdot(y, bm, bm) * inv_n
    yc = y - mean
    var = sdot(yc * yc, bm, bm) * inv_n
    yn = yc * lax.rsqrt(var + GN_EPS) * ln_w + ln_b
    bonus = sdot(r * k2 * rk, bm, bm) * vv
    return ((yn + bonus) * g,)


def _merge_fn(g0, g1, g2, oa, ob, oc):
    return (_sigmoid(g0) * oa + _sigmoid(g1) * ob + _sigmoid(g2) * oc,)


def _loss_call(y, target):
    T, C = y.shape
    tb = min(256, T)

    def body(y_ref, t_ref, dy_ref, part_ref):
        err = y_ref[...] - t_ref[...]
        dy_ref[...] = err * (1.0 / C)
        sq = jnp.sum(err * err, axis=0, keepdims=True)
        acc = sq[:, 0:LANES]
        for j in range(1, C // LANES):
            acc = acc + sq[:, j * LANES:(j + 1) * LANES]
        part_ref[...] = jnp.zeros(part_ref.shape, F32)
        part_ref[0:1, :] = acc * (0.5 / C)

    return pl.pallas_call(
        body, name='loss', grid=(T // tb,),
        in_specs=[pl.BlockSpec((tb, C), lambda i: (i, 0))] * 2,
        out_specs=[pl.BlockSpec((tb, C), lambda i: (i, 0)), pl.BlockSpec((8, LANES), lambda i: (i, 0))],
        out_shape=[jax.ShapeDtypeStruct((T, C), F32), jax.ShapeDtypeStruct((8 * (T // tb), LANES), F32)],
        compiler_params=_cparams(('parallel',)),
    )(y, target)


def _pad_rows(t, before, total):
    return jnp.pad(t, ((before, total - before - t.shape[0]), (0, 0)))


def _head_tile(g, lo, hi):
    return jnp.tile(g[lo:hi], MLA_HEADS).reshape(1, -1)


IN_WEIGHTS = ('w_in', 'rwkv_v1')


def _layer(l, x, v_first, wd, sp, cos, sin):
    proj, x_skip = _layer_in(l, x, wd, sp)
    return _layer_rest(l, proj, x_skip, v_first, wd, sp, cos, sin)


def _layer_in(l, x, wd, sp):
    nm = f'l{l}'
    vres = l > 0
    w_in = wd['w_in']
    if vres:
        v1t = wd['rwkv_v1'].map(lambda t: t.T)
    else:
        v1t = W(jnp.zeros((MV_LORA, D_MODEL), BF16), jnp.zeros((MV_LORA, D_MODEL), BF16))
    zpad = W(jnp.zeros((64, D_MODEL), BF16), jnp.zeros((64, D_MODEL), BF16))
    w_rw = _wcat([w_in['rkv'], w_in['xg'], w_in['xwa'], w_in['kpe'], v1t, zpad], 0)
    return norm_projections(
        x, sp['attn_norm'], [w_in['gates'], w_in['cq'], w_in['ckv'], w_rw, w_in['conv']], nm + '_win',
        transposed=True, skip=True)


def _layer_rest(l, proj, x_skip, v_first, wd, sp, cos, sin):
    gates, cq, ckv, rw, cv = proj
    T = rw.shape[0]
    nm = f'l{l}'
    vres = l > 0

    v_mu = sp['rwkv_v_mu'] if vres else jnp.zeros((MV_LORA,), F32)
    mu_all = jnp.concatenate([sp['rwkv_mu'][0:768], sp['rwkv_mu'][896:1024], sp['rwkv_mu'][768:896],
                              jnp.zeros((QK_ROPE,), F32), v_mu, jnp.zeros((64,), F32)]).reshape(1, -1)
    rws = token_shift_mix(rw, mu_all, nm + '_shift')

    wq = wd['mla_wq_b'].map(lambda t: jnp.concatenate(
        [t.reshape(Q_LORA, MLA_HEADS, QK_HEAD)[:, :, 0:64].reshape(Q_LORA, 512),
         t.reshape(Q_LORA, MLA_HEADS, QK_HEAD)[:, :, 64:80].reshape(Q_LORA, 128),
         t.reshape(Q_LORA, MLA_HEADS, QK_HEAD)[:, :, 80:96].reshape(Q_LORA, 128)], axis=1))
    wkn = wd['mla_wkv_b'].map(lambda t: t.reshape(KV_LORA, MLA_HEADS, 128)[:, :, 0:64].reshape(KV_LORA, 512))
    wv = wd['mla_wkv_b'].map(lambda t: t.reshape(KV_LORA, MLA_HEADS, 128)[:, :, 64:128].reshape(KV_LORA, 512))
    q, = norm_projections(cq, sp['mla_q_a_norm'], [wq], nm + '_wq')
    kn, vv_att = norm_projections(ckv, sp['mla_kv_a_norm'], [wkn, wv], nm + '_wkv')
    gq, gk = sp['mla_q_norm'], sp['mla_k_norm']
    consts = _mla_consts()
    qk_specs = ([ROW(pieces=((0, 512), (512, 128), (640, 128))), ROW(), ROW(pieces=((1024, 128),)),
                 ROW(False), ROW(False)] + [FULL()] * 6 + [FULL(False)] * 8)
    qk_op = stage_op(_qk_prep_fn, qk_specs, [512, 128, 128, 512, 128, 128], nm + '_qkprep')
    Qn, Q1, Q2, Kn, K1, K2 = qk_op(q, kn, rws, cos, sin,
                                   _head_tile(gq, 0, 64), _head_tile(gq, 64, 80), _head_tile(gq, 80, 96),
                                   _head_tile(gk, 0, 64), _head_tile(gk, 64, 80), _head_tile(gk, 80, 96), *consts)
    o_att = attention(Qn, Q1, Q2, Kn, K1, K2, vv_att, nm + '_att')
    o_a = linear(o_att, wd['mla_w_o'], name=nm + '_wo')

    bm = _np_bf16(_block_ones(RW_WIDTH, RW_N))
    vec = lambda n: sp[n].reshape(1, -1)
    f32w = lambda n: wd[n].c + wd[n].b.astype(F32)
    w2p = _pad_rows(f32w('rwkv_w2'), 0, 128)
    a2p = _pad_rows(f32w('rwkv_a2'), 64, 128)
    g2 = f32w('rwkv_g2')
    rw_pieces = ((0, 256), (256, 256), (512, 256), (768, 128), (896, 128), (1024, 128))
    if vres:
        v2p = _pad_rows(f32w('rwkv_v2'), 32, 128)
        prep_specs = [ROW(pieces=rw_pieces), ROW()] + [FULL()] * 9 + [FULL(False)]
        prep_in = [rws, v_first, vec('rwkv_w0'), vec('rwkv_a0'), vec('rwkv_k_k'), vec('rwkv_k_a'), w2p, a2p, g2,
                   vec('rwkv_v0'), v2p, bm]
    else:
        prep_specs = [ROW(pieces=rw_pieces)] + [FULL()] * 7 + [FULL(False)]
        prep_in = [rws, vec('rwkv_w0'), vec('rwkv_a0'), vec('rwkv_k_k'), vec('rwkv_k_a'), w2p, a2p, g2, bm]
    prep_op = stage_op(_rwkv_prep_fn(vres), prep_specs, [256] * 7, nm + '_rwprep')
    r_, dec, k2, vv, an, bn, g = prep_op(*prep_in)
    if not vres:
        v_first = vv
    t3 = lambda t: t.reshape(T, 1, RW_WIDTH)
    y = wkv7(t3(r_), t3(dec), t3(k2), t3(vv), t3(an), t3(bn), nm + '_scan').reshape(T, RW_WIDTH)
    post_op = stage_op(_rwkv_post_fn, [ROW()] * 5 + [FULL()] * 3 + [FULL(False)], [256], nm + '_rwpost')
    yb = post_op(y, r_, k2, vv, g, vec('rwkv_ln_w'), vec('rwkv_ln_b'), sp['rwkv_r_k'].reshape(1, -1), bm)[0]
    o_b = linear(yb, wd['rwkv_w_o'], name=nm + '_rwo')

    oc_in = short_conv(cv, f32w('conv_w'), nm + '_conv')
    o_c = linear(oc_in, wd['conv_w_o'], name=nm + '_cwo')

    merge_op = stage_op(_merge_fn, [ROW(pieces=((0, 1024), (1024, 1024), (2048, 1024))), ROW(), ROW(), ROW()],
                        [D_MODEL], nm + '_merge')
    merged = merge_op(gates, o_a, o_b, o_c)[0]
    x2 = linear(merged, wd['w_out'], add=x_skip, name=nm + '_wout')
    return x2, v_first


def _mlp(l, x, wd, sp):
    return mlp_block(x, sp['mlp_norm'].reshape(1, -1), wd['w_up'].b, wd['w_up'].c, wd['w_down'].b, wd['w_down'].c,
                     f'l{l}_mlp')


MLP_WEIGHTS = ('w_up', 'w_down')


def _entries(ml):
    layer, part = ml if isinstance(ml, tuple) else (ml, 'all')
    out = []
    for name, (layers, shape, axis) in SHARDED.items():
        l = layer if layers == DEPTH else layer - 1
        if not 0 <= l < layers or name in MLP_WEIGHTS:
            continue
        if part != 'all' and (name in IN_WEIGHTS) != (part == 'in'):
            continue
        n = shape[0] * shape[1] // N_DEV
        if name == 'conv_w':
            out.append(('conv_w_hi', name, l, n))
            out.append(('conv_w_lo', name, l, n))
        else:
            out.append((name, name, l, n))
    return out


def _slot_size(n):
    return -(-n // LANES) * LANES


def _pack_rows(ml):
    total = sum(_slot_size(n) for _, _, _, n in _entries(ml))
    rows = -(-total // LANES)
    return -(-rows // PACK_ROW_MULT) * PACK_ROW_MULT


def _pack_flat(pieces, ml):
    rows = _pack_rows(ml)
    padded = []
    for p, (_, _, _, n) in zip(pieces, _entries(ml)):
        pad = _slot_size(n) - n
        if pad:
            p = jnp.pad(p, [(0, 0)] * (p.ndim - 1) + [(0, pad)])
        padded.append(p)
    flat = jnp.concatenate(padded, axis=-1)
    tail = rows * LANES - flat.shape[-1]
    if tail:
        flat = jnp.pad(flat, [(0, 0)] * (flat.ndim - 1) + [(0, tail)])
    return flat.reshape(flat.shape[:-1] + (rows, LANES))


def _unpack_flat(buf, ml):
    out, row = [], 0
    for _, _, _, n in _entries(ml):
        nrows = _slot_size(n) // LANES
        piece = buf[..., row:row + nrows, :].reshape(buf.shape[:-2] + (-1,))
        out.append(piece[..., :n])
        row += nrows
    return out


def _pack_shards(shards, ml, dtype, split_conv):
    pieces = []
    for slot, name, l, n in _entries(ml):
        a = shards[name][l]
        if name in TRANSPOSED:
            a = a.astype(dtype).T
        a = a.reshape(-1)
        if slot == 'conv_w_hi':
            a = a.astype(BF16).astype(F32) if split_conv else a
        elif slot == 'conv_w_lo':
            a = (a - a.astype(BF16).astype(F32)) if split_conv else jnp.zeros_like(a)
        pieces.append(a.astype(dtype))
    return _pack_flat(pieces, ml)


def _unpack_shards(buf, ml):
    out = {}
    for (slot, name, l, n), v in zip(_entries(ml), _unpack_flat(buf, ml)):
        if slot == 'conv_w_lo':
            continue
        layers, shape, axis = SHARDED[name]
        sshape = (shape[0] // N_DEV, shape[1]) if axis == 0 else (shape[0], shape[1] // N_DEV)
        out[(name, l)] = v.reshape(sshape[::-1]).T if name in TRANSPOSED else v.reshape(sshape)
    return out


def _to_full(blocks, shape, axis):
    if axis == 0:
        return blocks.reshape(shape)
    return blocks.reshape(N_DEV, shape[0], shape[1] // N_DEV).transpose(1, 0, 2).reshape(shape)


def _to_blocks(full, axis):
    r, c = full.shape
    if axis == 0:
        return full.reshape(N_DEV, -1)
    return full.reshape(r, N_DEV, c // N_DEV).transpose(1, 0, 2).reshape(N_DEV, -1)


def _unpack_gathered(gathered, ml):
    out, conv_hi = {}, None
    for (slot, name, l, n), v in zip(_entries(ml), _unpack_flat(gathered, ml)):
        layers, shape, axis = SHARDED[name]
        if name in TRANSPOSED:
            out[name] = v.reshape(-1, shape[0])
            continue
        full = _to_full(v, shape, axis)
        if slot == 'conv_w_hi':
            conv_hi = full
        elif slot == 'conv_w_lo':
            out[name] = conv_hi.astype(F32) + full.astype(F32)
        else:
            out[name] = full
    return out


def _pack_grads(grads, ml):
    pieces = []
    for slot, name, l, n in _entries(ml):
        if name in TRANSPOSED:
            blocks = jnp.concatenate(grads[name], axis=0).reshape(N_DEV, -1)
        else:
            blocks = _to_blocks(grads[name], SHARDED[name][2])
        if slot == 'conv_w_lo':
            blocks = jnp.zeros_like(blocks)
        pieces.append(blocks.astype(BF16))
    return _pack_flat(pieces, ml)


def _my_pos():
    return lax.axis_index('x'), lax.axis_index('y'), lax.axis_index('c')


def _flip(v, bit):
    return 1 - v if bit else v


def all_gather_blocks(x):
    rows = x.shape[0]

    def body(x_ref, out_ref, send_sems, recv_sems, local_sem):
        mx, my, mc = _my_pos()
        me, sibling = (mx, my, mc), (mx, my, 1 - mc)
        chips = [(1 - mx, my), (mx, 1 - my), (1 - mx, 1 - my)]

        def block(px, py, pc):
            return out_ref.at[4 * px + 2 * py + pc]

        def copy(k, blk, to, src=None):
            return pltpu.make_async_remote_copy(
                src_ref=block(*blk) if src is None else src, dst_ref=block(*blk),
                send_sem=send_sems.at[k], recv_sem=recv_sems.at[k], device_id=to, device_id_type=MESH)

        mine = pltpu.make_async_copy(x_ref, block(*me), local_sem)
        mine.start()
        first = [copy(0, me, sibling, src=x_ref)]
        first += [copy(1 + j, me, (*chip, mc), src=x_ref) for j, chip in enumerate(chips)]
        for cp in first:
            cp.start()
        passed = [copy(4 + j, (*chip, mc), sibling) for j, chip in enumerate(chips)]
        for j, chip in enumerate(chips):
            copy(1 + j, (*chip, mc), me).wait_recv()
            passed[j].start()
        copy(0, sibling, me).wait_recv()
        for j, chip in enumerate(chips):
            copy(4 + j, (*chip, 1 - mc), me).wait_recv()
        for cp in first + passed:
            cp.wait_send()
        mine.wait()

    return pl.pallas_call(
        body, name='all_gather_weights',
        out_shape=jax.ShapeDtypeStruct((N_DEV, rows, LANES), x.dtype),
        in_specs=[pl.BlockSpec(memory_space=pl.ANY)], out_specs=pl.BlockSpec(memory_space=pl.ANY),
        scratch_shapes=[pltpu.SemaphoreType.DMA((7,)), pltpu.SemaphoreType.DMA((7,)), pltpu.SemaphoreType.DMA],
    )(x)


HBM_SPEC = pl.BlockSpec(memory_space=pltpu.HBM)
SEM_SPEC = pl.BlockSpec(memory_space=pltpu.SEMAPHORE)
DATAFLOW_EFFECT = pltpu.SideEffectType.DATAFLOW_SIDE_EFFECTING


def _direct_copies(src_ref, land_ref, send_sems, recv_sems, per_peer):
    mx, my, mc = _my_pos()
    me = 4 * mx + 2 * my + mc
    copies = []
    for k in range(1, N_DEV):
        peer = (_flip(mx, k & 4), _flip(my, k & 2), _flip(mc, k & 1))
        pidx = 4 * peer[0] + 2 * peer[1] + peer[2]
        copies.append(pltpu.make_async_remote_copy(
            src_ref=src_ref.at[pidx] if per_peer else src_ref, dst_ref=land_ref.at[me],
            send_sem=send_sems.at[k - 1], recv_sem=recv_sems.at[k - 1], device_id=peer, device_id_type=MESH))
    return copies


def send_start(src, per_peer, name):
    block = src.shape[1:] if per_peer else src.shape
    land_shape = (N_DEV,) + tuple(block)

    def body(src_ref, land_ref, send_sems, recv_sems, src_thru, land_thru, token):
        for cp in _direct_copies(src_ref, land_ref, send_sems, recv_sems, per_peer):
            cp.start()
        token[...] = jnp.zeros(token.shape, F32)

    send_sems, recv_sems, src_thru, land_thru, token = pl.pallas_call(
        body, name=name,
        out_shape=(pltpu.SemaphoreType.DMA((N_DEV - 1,)), pltpu.SemaphoreType.DMA((N_DEV - 1,)),
                   pltpu.HBM(src.shape, src.dtype), pltpu.HBM(land_shape, src.dtype),
                   jax.ShapeDtypeStruct((8, LANES), F32)),
        in_specs=(HBM_SPEC, HBM_SPEC),
        out_specs=(SEM_SPEC, SEM_SPEC, HBM_SPEC, HBM_SPEC, pl.BlockSpec(memory_space=pltpu.VMEM)),
        input_output_aliases={0: 2, 1: 3},
        compiler_params=pltpu.CompilerParams(has_side_effects=DATAFLOW_EFFECT),
    )(pltpu.with_memory_space_constraint(src, pltpu.HBM),
      pltpu.with_memory_space_constraint(lax.empty(land_shape, src.dtype), pltpu.HBM))
    return (send_sems, recv_sems, src_thru, land_thru), token[0, 0]


def send_wait(handles, after, per_peer, name):
    send_sems, recv_sems, src_thru, land_thru = handles

    def body(src_ref, land_ref, send_sems, recv_sems, after_ref, src_dead, got_ref):
        for cp in _direct_copies(src_ref, land_ref, send_sems, recv_sems, per_peer):
            cp.wait_send()
            cp.wait_recv()

    return pl.pallas_call(
        body, name=name,
        out_shape=(pltpu.HBM(src_thru.shape, src_thru.dtype), pltpu.HBM(land_thru.shape, land_thru.dtype)),
        in_specs=(HBM_SPEC, HBM_SPEC, SEM_SPEC, SEM_SPEC, pl.BlockSpec(memory_space=pl.ANY)),
        out_specs=(HBM_SPEC, HBM_SPEC), input_output_aliases={0: 0, 1: 1},
        compiler_params=pltpu.CompilerParams(has_side_effects=DATAFLOW_EFFECT),
    )(src_thru, land_thru, send_sems, recv_sems, after)[1]


def _adamw_math(w, g, m, v):
    m2 = ADAM_B1 * m + (1.0 - ADAM_B1) * g
    v2 = ADAM_B2 * v + (1.0 - ADAM_B2) * (g * g)
    m_hat = m2 / (1.0 - ADAM_B1 ** ADAM_STEP)
    v_hat = v2 / (1.0 - ADAM_B2 ** ADAM_STEP)
    delta = -ADAM_LR * (m_hat / (jnp.sqrt(v_hat) + ADAM_EPS) + ADAM_WD * w)
    return delta, m2, v2


def sum_parts(parts, mine, name):
    _, rows, cols = parts.shape
    rb = rows
    while N_DEV * rb * cols * 2 > (2 << 20) and rb % 32 == 0:
        rb //= 2

    def body(p_ref, mine_ref, g_ref):
        mx, my, mc = _my_pos()
        me = 4 * mx + 2 * my + mc
        own = mine_ref[0].astype(F32)
        g = jnp.where(me == 0, own, p_ref[0].astype(F32))
        for j in range(1, N_DEV):
            g = g + jnp.where(me == j, own, p_ref[j].astype(F32))
        g_ref[...] = g

    return pl.pallas_call(
        body, name=name, grid=(rows // rb,),
        in_specs=[pl.BlockSpec((N_DEV, rb, cols), lambda i: (0, i, 0)), pl.BlockSpec((1, rb, cols), lambda i: (0, i, 0))],
        out_specs=pl.BlockSpec((rb, cols), lambda i: (i, 0)),
        out_shape=jax.ShapeDtypeStruct((rows, cols), F32), compiler_params=_cparams(('parallel',)),
    )(parts, mine)


ADAMW_BLOCK_BYTES = 1 << 20


def adamw_weight(name, w, m, v, grads):
    layers, a, b = w.shape
    ra = a
    while ra * b * 4 > ADAMW_BLOCK_BYTES and ra % 16 == 0:
        ra //= 2

    def body(*refs):
        w_ref, m_ref, v_ref = refs[:3]
        g_refs = refs[3:3 + layers]
        g_ref, d_ref, m2_ref, v2_ref = refs[3 + layers:]
        g = g_refs[0][...]
        for l in range(1, layers):
            g = jnp.where(pl.program_id(0) == l, g_refs[l][...], g)
        delta, m2, v2 = _adamw_math(w_ref[0], g, m_ref[0], v_ref[0])
        g_ref[0] = g
        d_ref[0] = delta
        m2_ref[0] = m2
        v2_ref[0] = v2

    blk = pl.BlockSpec((1, ra, b), lambda l, i: (l, i, 0))
    gblk = pl.BlockSpec((ra, b), lambda l, i: (i, 0))
    sh = jax.ShapeDtypeStruct(w.shape, F32)
    return pl.pallas_call(
        body, name='adamw_' + name, grid=(layers, a // ra), in_specs=[blk] * 3 + [gblk] * layers,
        out_specs=[blk] * 4, out_shape=[sh] * 4, compiler_params=_cparams(('parallel', 'parallel')),
    )(w, m, v, *grads)


def allreduce_adamw_small(g, w, m, v):
    rows = g.shape[0]

    def body(g_ref, w_ref, m_ref, v_ref, gs_ref, d_ref, m2_ref, v2_ref, all_ref, send_sems, recv_sems):
        mx, my, mc = _my_pos()
        me, sibling = (mx, my, mc), (mx, my, 1 - mc)
        chips = [(1 - mx, my), (mx, 1 - my), (1 - mx, 1 - my)]

        def block(px, py, pc):
            return all_ref.at[4 * px + 2 * py + pc]

        def copy(k, blk, to, src=None):
            return pltpu.make_async_remote_copy(
                src_ref=block(*blk) if src is None else src, dst_ref=block(*blk),
                send_sem=send_sems.at[k], recv_sem=recv_sems.at[k], device_id=to, device_id_type=MESH)

        first = [copy(0, me, sibling, src=g_ref)]
        first += [copy(1 + j, me, (*chip, mc), src=g_ref) for j, chip in enumerate(chips)]
        for cp in first:
            cp.start()
        passed = [copy(4 + j, (*chip, mc), sibling) for j, chip in enumerate(chips)]
        for j, chip in enumerate(chips):
            copy(1 + j, (*chip, mc), me).wait_recv()
            passed[j].start()
        copy(0, sibling, me).wait_recv()
        for j, chip in enumerate(chips):
            copy(4 + j, (*chip, 1 - mc), me).wait_recv()
        for cp in first + passed:
            cp.wait_send()
        my_idx = 4 * mx + 2 * my + mc
        total = jnp.zeros((rows, LANES), F32)
        for j in range(N_DEV):
            total = total + jnp.where(my_idx == j, g_ref[...], all_ref[j])
        delta, m2, v2 = _adamw_math(w_ref[...], total, m_ref[...], v_ref[...])
        gs_ref[...] = total
        d_ref[...] = delta
        m2_ref[...] = m2
        v2_ref[...] = v2

    vm = pl.BlockSpec(memory_space=pltpu.VMEM)
    sh = jax.ShapeDtypeStruct((rows, LANES), F32)
    return pl.pallas_call(
        body, name='allreduce_adamw_small', in_specs=[vm] * 4, out_specs=[vm] * 4, out_shape=[sh] * 4,
        scratch_shapes=[pltpu.VMEM((N_DEV, rows, LANES), F32), pltpu.SemaphoreType.DMA((7,)),
                        pltpu.SemaphoreType.DMA((7,))],
    )(g, w, m, v)


SMALL_COUNT = 11680


def _small_pack(d, extra=None):
    assert sum(d[n].size for n in SMALL_NAMES) == SMALL_COUNT
    flat = jnp.concatenate([d[n].reshape(-1) for n in SMALL_NAMES] + ([] if extra is None else [extra.reshape(1)]))
    rows = -(-(SMALL_COUNT + 1) // (8 * LANES)) * 8
    return jnp.pad(flat, (0, rows * LANES - flat.shape[0])).reshape(rows, LANES)


def _small_unpack(buf, like):
    flat = buf.reshape(-1)
    out, off = {}, 0
    for n in SMALL_NAMES:
        sz = int(np.prod(like[n].shape))
        out[n] = flat[off:off + sz].reshape(like[n].shape)
        off += sz
    return out


def _rope_tables(positions):
    freqs = ROPE_THETA ** (-(jnp.arange(QK_ROPE // 2, dtype=F32) * 2.0 / QK_ROPE))
    ang = positions.astype(F32)[:, None] * freqs
    return jnp.tile(jnp.cos(ang), (1, MLA_HEADS)), jnp.tile(jnp.sin(ang), (1, MLA_HEADS))


def _layer_weights(gathered, carriers):
    wd = {}
    for name, full in gathered.items():
        if name == 'w_in':
            wd[name] = {seg: W(full[lo:hi], c) for (seg, lo, hi), c in zip(WIN_SEGS, carriers[name])}
        else:
            wd[name] = W(full, carriers[name])
    return wd


F32_GRAD_WEIGHTS = ('rwkv_w2', 'rwkv_a2', 'rwkv_g2', 'rwkv_v2', 'conv_w')


def _make_carriers(gathered):
    gathered, carriers = dict(gathered), {}
    for name, full in gathered.items():
        if name == 'w_in':
            carriers[name] = tuple(jnp.zeros((hi - lo, D_MODEL), BF16) for _, lo, hi in WIN_SEGS)
        elif name == 'conv_w':
            gathered[name] = full.astype(BF16)
            carriers[name] = full - full.astype(BF16).astype(F32)
        else:
            carriers[name] = jnp.zeros(full.shape, F32 if name in F32_GRAD_WEIGHTS else BF16)
    return gathered, carriers


def _layer_small(small, ml):
    out = {}
    for n in SMALL_NAMES:
        l = ml if small[n].shape[0] == DEPTH else ml - 1
        if 0 <= l < small[n].shape[0]:
            out[n] = small[n][l]
    return out


def _after(value, tokens):
    return value + sum(tokens[1:], tokens[0])


def _train_step(x, positions, loss_target, weights, moms_m, moms_v):
    shards = {n: weights[n] for n in SHARDED}
    small = {n: weights[n] for n in SMALL_NAMES}
    me = 4 * lax.axis_index('x') + 2 * lax.axis_index('y') + lax.axis_index('c')
    cos, sin = _rope_tables(positions[0])

    units = [('pack', (0, 'in')), ('pack', (0, 'rest')), ('w_up', 0), ('w_down', 0),
             ('pack', 1), ('w_up', 1), ('w_down', 1)]
    first, later = units[0], units[1:]

    def tag(u):
        kind, key = u
        return f'{kind}_l{key[0]}{key[1]}' if isinstance(key, tuple) else f'{kind}_l{key}'

    def own_block(u):
        kind, key = u
        return _pack_shards(shards, key, BF16, True) if kind == 'pack' else shards[kind][key].astype(BF16)

    own = {u: own_block(u) for u in units}
    held = lax.optimization_barrier((all_gather_blocks(own[first]), *[own[u] for u in later]))
    blocks, gathers = {first: held[0]}, {}
    for u, mine in zip(later, held[1:]):
        own[u] = mine

    def start_gather(u):
        gathers[u], token = send_start(own[u], False, f'gather_{tag(u)}_start')
        return token

    x0 = _after(x[0], [start_gather(later[0])])

    def gathered_block(u, after):
        if u not in blocks:
            landed = send_wait(gathers[u], after, False, f'gather_{tag(u)}_wait')
            blocks[u] = lax.dynamic_update_slice(landed, own[u][None], (me,) + (0,) * own[u].ndim)
        return blocks[u]

    def packed_weights(key, after):
        return _make_carriers(_unpack_gathered(gathered_block(('pack', key), after), key))

    def mlp_weights(l, after):
        up = gathered_block(('w_up', l), after)
        down = gathered_block(('w_down', l), after).reshape(D_FF, D_MODEL)
        return _make_carriers({'w_up': up, 'w_down': down})

    def small_of(layer, part):
        owner = lambda n: 'mlp' if n == 'mlp_norm' else ('in' if n == 'attn_norm' else 'rest')
        parts = ('in', 'rest') if part == 'mix' else (part,)
        return {n: v for n, v in _layer_small(small, layer).items() if owner(n) in parts}

    def mlp(layer, gathered):
        return lambda c, s, xx: _mlp(layer, xx, _layer_weights(gathered, c), s)

    gathered, carriers = packed_weights((0, 'in'), None)
    (proj, x_skip), vjp_in0 = jax.vjp(lambda c, s, xx: _layer_in(0, xx, _layer_weights(gathered, c), s),
                                      carriers, small_of(0, 'in'), x0)
    gathered, carriers = packed_weights((0, 'rest'), proj[2])
    landed = (blocks[later[0]][0, 0, 0] * 0).astype(F32)
    tokens = []
    for u in later[1:]:
        own[u] = own[u] + landed.astype(own[u].dtype)
        tokens.append(start_gather(u))
    proj = (proj[0], proj[1], _after(proj[2], tokens), proj[3], proj[4])
    (h, v_first), vjp_rest0 = jax.vjp(
        lambda c, s, p, xs, w=gathered: _layer_rest(0, p, xs, None, _layer_weights(w, c), s, cos, sin),
        carriers, small_of(0, 'rest'), proj, x_skip)
    gathered, carriers = mlp_weights(0, h)
    h, vjp_mlp0 = jax.vjp(mlp(0, gathered), carriers, small_of(0, 'mlp'), h)
    gathered, carriers = packed_weights(1, h)
    h, vjp_mix1 = jax.vjp(
        lambda c, s, xx, vf, w=gathered: _layer(1, xx, vf, _layer_weights(w, c), s, cos, sin)[0],
        carriers, small_of(1, 'mix'), h, v_first)
    gathered, carriers = mlp_weights(1, h)
    y, vjp_mlp1 = jax.vjp(mlp(1, gathered), carriers, small_of(1, 'mlp'), h)

    dy, loss_parts = _loss_call(y, loss_target[0])

    shipped = {}

    def ship(u, to_send):
        handles, token = send_start(to_send, True, f'grads_{tag(u)}_start')
        shipped[u] = (handles, to_send)
        return token

    def ship_mlp(l, gw):
        down = gw['w_down'].reshape(N_DEV, D_FF // N_DEV, D_MODEL)
        return [ship(('w_up', l), gw['w_up']), ship(('w_down', l), down)]

    gw, gs_mlp1, d = vjp_mlp1(dy)
    d = _after(d, ship_mlp(1, gw))
    gw, gs_mix1, d, dvf = vjp_mix1(d)
    d = _after(d, [ship(('pack', 1), _pack_grads(gw, 1))])
    gw, gs_mlp0, d = vjp_mlp0(d)
    d = _after(d, ship_mlp(0, gw))
    gw, gs_rest0, dproj, dx_skip = vjp_rest0((d, dvf))
    token = ship(('pack', (0, 'rest')), _pack_grads(gw, (0, 'rest')))
    dproj = (dproj[0], dproj[1], _after(dproj[2], [token]), dproj[3], dproj[4])
    gw, gs_in0, g_x = vjp_in0((dproj, dx_skip))
    last_token = ship(first, _pack_grads(gw, (0, 'in')))
    gs0, gs1 = {**gs_in0, **gs_rest0, **gs_mlp0}, {**gs_mix1, **gs_mlp1}

    def arrived(u, after):
        handles, sent = shipped[u]
        landed = send_wait(handles, after, True, f'grads_{tag(u)}_wait')
        return sum_parts(landed, lax.dynamic_slice_in_dim(sent, me, 1, axis=0), f'sum_grads_{tag(u)}')

    after = _after(g_x, [last_token])
    grads = {}
    for u in later:
        if u[0] == 'pack':
            grads.update(_unpack_shards(arrived(u, after), u[1]))
        else:
            grads[u] = arrived(u, after)
    sharded_out = [{}, {}, {}, {}]

    def update(n):
        outs = adamw_weight(n, weights[n], moms_m[n], moms_v[n], [grads[(n, l)] for l in range(SHARDED[n][0])])
        for i in range(4):
            sharded_out[i][n] = outs[i]

    last_weight = 'w_in'
    for n in SHARDED:
        if n != last_weight:
            update(n)

    g_small = {}
    for n in SMALL_NAMES:
        per = [g[n] for g in (gs0, gs1) if n in g]
        g_small[n] = jnp.stack(per)
    small_grads = _after(_small_pack(g_small, jnp.sum(loss_parts)), [sharded_out[1][MLP_WEIGHTS[-1]][0, 0, 0] * 0.0])
    small_bufs = allreduce_adamw_small(
        small_grads, _small_pack(small), _small_pack({n: moms_m[n] for n in SMALL_NAMES}),
        _small_pack({n: moms_v[n] for n in SMALL_NAMES}))
    loss = small_bufs[0].reshape(-1)[SMALL_COUNT]
    small_out = [_small_unpack(b, small) for b in small_bufs]

    grads.update(_unpack_shards(arrived(first, small_bufs[0]), first[1]))
    update(last_weight)

    pick = lambda i: [sharded_out[i][n] if n in SHARDED else small_out[i][n] for n in WEIGHT_NAMES]
    return (loss, g_x[None], *pick(0), *pick(1), *pick(2), *pick(3))


def kernel(x, positions, attn_norm, w_in, mla_q_a_norm, mla_wq_b, mla_kv_a_norm, mla_wkv_b, mla_q_norm, mla_k_norm, mla_w_o, rwkv_mu, rwkv_w0, rwkv_w2, rwkv_a0, rwkv_a2, rwkv_g2, rwkv_k_k, rwkv_k_a, rwkv_r_k, rwkv_ln_w, rwkv_ln_b, rwkv_w_o, rwkv_v1, rwkv_v_mu, rwkv_v0, rwkv_v2, conv_w, conv_w_o, w_out, mlp_norm, w_up, w_down, loss_target, m_attn_norm, m_w_in, m_mla_q_a_norm, m_mla_wq_b, m_mla_kv_a_norm, m_mla_wkv_b, m_mla_q_norm, m_mla_k_norm, m_mla_w_o, m_rwkv_mu, m_rwkv_w0, m_rwkv_w2, m_rwkv_a0, m_rwkv_a2, m_rwkv_g2, m_rwkv_k_k, m_rwkv_k_a, m_rwkv_r_k, m_rwkv_ln_w, m_rwkv_ln_b, m_rwkv_w_o, m_rwkv_v1, m_rwkv_v_mu, m_rwkv_v0, m_rwkv_v2, m_conv_w, m_conv_w_o, m_w_out, m_mlp_norm, m_w_up, m_w_down, v_attn_norm, v_w_in, v_mla_q_a_norm, v_mla_wq_b, v_mla_kv_a_norm, v_mla_wkv_b, v_mla_q_norm, v_mla_k_norm, v_mla_w_o, v_rwkv_mu, v_rwkv_w0, v_rwkv_w2, v_rwkv_a0, v_rwkv_a2, v_rwkv_g2, v_rwkv_k_k, v_rwkv_k_a, v_rwkv_r_k, v_rwkv_ln_w, v_rwkv_ln_b, v_rwkv_w_o, v_rwkv_v1, v_rwkv_v_mu, v_rwkv_v0, v_rwkv_v2, v_conv_w, v_conv_w_o, v_w_out, v_mlp_norm, v_w_up, v_w_down):
    args = locals()
    weights = {n: args[n] for n in WEIGHT_NAMES}
    moms_m = {n: args['m_' + n] for n in WEIGHT_NAMES}
    moms_v = {n: args['v_' + n] for n in WEIGHT_NAMES}
    return _train_step(x, positions, loss_target, weights, moms_m, moms_v)
```

```python
import functools

import numpy as np
import jax
import jax.numpy as jnp
from jax import lax
from jax.experimental import pallas as pl
from jax.experimental.pallas import tpu as pltpu

F32 = jnp.float32
BF16 = jnp.bfloat16

N_DEV = 8
LANES = 128
D_MODEL = 1024
DEPTH = 2
MLA_HEADS = 8
QK_NOPE = 64
QK_ROPE = 32
QK_HEAD = QK_NOPE + QK_ROPE
V_HEAD = 64
Q_LORA = 384
KV_LORA = 256
ROPE_THETA = 10000.0
RW_HEADS = 4
RW_N = 64
RW_WIDTH = RW_HEADS * RW_N
MV_LORA = 32
GN_EPS = 64e-5
CONV_WIDTH = 256
D_FF = 4 * D_MODEL
NORM_EPS = 1e-6
ADAM_LR = 0.001
ADAM_B1 = 0.9
ADAM_B2 = 0.999
ADAM_EPS = 1e-08
ADAM_WD = 0.01
ADAM_STEP = 10

VMEM_LIMIT = 56 * 1024 * 1024
MESH = pl.DeviceIdType.MESH

WEIGHT_NAMES = ['attn_norm', 'w_in', 'mla_q_a_norm', 'mla_wq_b', 'mla_kv_a_norm', 'mla_wkv_b', 'mla_q_norm',
                'mla_k_norm', 'mla_w_o', 'rwkv_mu', 'rwkv_w0', 'rwkv_w2', 'rwkv_a0', 'rwkv_a2', 'rwkv_g2',
                'rwkv_k_k', 'rwkv_k_a', 'rwkv_r_k', 'rwkv_ln_w', 'rwkv_ln_b', 'rwkv_w_o', 'rwkv_v1',
                'rwkv_v_mu', 'rwkv_v0', 'rwkv_v2', 'conv_w', 'conv_w_o', 'w_out', 'mlp_norm', 'w_up', 'w_down']

SHARDED = {
    'w_in': (2, (1024, 5536), 1), 'mla_wq_b': (2, (384, 768), 1), 'mla_wkv_b': (2, (256, 1024), 1),
    'mla_w_o': (2, (512, 1024), 1), 'rwkv_w2': (2, (64, 256), 1), 'rwkv_a2': (2, (64, 256), 1),
    'rwkv_g2': (2, (128, 256), 1), 'rwkv_w_o': (2, (256, 1024), 1), 'conv_w': (2, (3, 256), 1),
    'conv_w_o': (2, (256, 1024), 1), 'w_out': (2, (1024, 1024), 0), 'w_up': (2, (1024, 4096), 1),
    'w_down': (2, (4096, 1024), 0), 'rwkv_v1': (1, (1024, 32), 0), 'rwkv_v2': (1, (32, 256), 1),
}
SMALL_NAMES = [n for n in WEIGHT_NAMES if n not in SHARDED]
TRANSPOSED = ('w_in',)
WIN_SEGS = (('gates', 0, 3072), ('cq', 3072, 3456), ('ckv', 3456, 3712), ('kpe', 3712, 3744), ('rkv', 3744, 4512),
            ('xwa', 4512, 4640), ('xg', 4640, 4768), ('conv', 4768, 5536))
PACK_ROW_MULT = 512


def _cparams(sem=None, **kw):
    if sem is not None:
        kw['dimension_semantics'] = sem
    return pltpu.CompilerParams(vmem_limit_bytes=VMEM_LIMIT, **kw)


def _pick(n, cands):
    for c in cands:
        if n % c == 0:
            return c
    raise ValueError(f'no tile for {n}')


def _mm_nn(a, b, add=None, name='mm_nn'):
    M, K = a.shape
    N = b.shape[1]
    tm = _pick(M, (1024, 512, 256, 128))
    tn = _pick(N, (512, 384, 256, 128))
    tk = _pick(K, (1024, 512, 384, 256, 128))
    nk = K // tk
    has_add = add is not None

    def body(*refs):
        if has_add:
            a_ref, b_ref, add_ref, o_ref, acc_ref = refs
        else:
            a_ref, b_ref, o_ref, acc_ref = refs
        kk = pl.program_id(2)
        part = jnp.dot(a_ref[...].astype(BF16), b_ref[...].astype(BF16), preferred_element_type=F32)

        @pl.when(kk == 0)
        def _():
            acc_ref[...] = part

        @pl.when(kk > 0)
        def _():
            acc_ref[...] += part

        @pl.when(kk == nk - 1)
        def _():
            if has_add:
                o_ref[...] = acc_ref[...] + add_ref[...]
            else:
                o_ref[...] = acc_ref[...]

    in_specs = [pl.BlockSpec((tm, tk), lambda i, j, k: (i, k)), pl.BlockSpec((tk, tn), lambda i, j, k: (k, j))]
    args = [a, b]
    if has_add:
        in_specs.append(pl.BlockSpec((tm, tn), lambda i, j, k: (i, j)))
        args.append(add)
    return pl.pallas_call(
        body, name=name, grid=(M // tm, N // tn, nk), in_specs=in_specs,
        out_specs=pl.BlockSpec((tm, tn), lambda i, j, k: (i, j)),
        out_shape=jax.ShapeDtypeStruct((M, N), F32),
        scratch_shapes=[pltpu.VMEM((tm, tn), F32)],
        compiler_params=_cparams(('parallel', 'parallel', 'arbitrary')),
    )(*args)


def _mm_nt(a, b, add=None, name='mm_nt'):
    M, N = a.shape
    blocked = b.ndim == 3
    K = b.shape[-2]
    tm = _pick(M, (1024, 512, 256, 128))
    tk = _pick(K, (512, 384, 256, 128))
    tn = N // N_DEV if blocked else _pick(N, (1024, 512, 384, 256, 128))
    nn = N // tn
    has_add = add is not None

    def body(*refs):
        if has_add:
            a_ref, b_ref, add_ref, o_ref, acc_ref = refs
        else:
            a_ref, b_ref, o_ref, acc_ref = refs
        kk = pl.program_id(2)
        part = lax.dot_general(a_ref[...].astype(BF16), b_ref[...].astype(BF16), (((1,), (1,)), ((), ())),
                               preferred_element_type=F32)

        @pl.when(kk == 0)
        def _():
            acc_ref[...] = part

        @pl.when(kk > 0)
        def _():
            acc_ref[...] += part

        @pl.when(kk == nn - 1)
        def _():
            if has_add:
                o_ref[...] = acc_ref[...] + add_ref[...]
            else:
                o_ref[...] = acc_ref[...]

    if blocked:
        b_spec = pl.BlockSpec((None, tk, tn), lambda i, j, k: (k, j, 0))
    else:
        b_spec = pl.BlockSpec((tk, tn), lambda i, j, k: (j, k))
    in_specs = [pl.BlockSpec((tm, tn), lambda i, j, k: (i, k)), b_spec]
    args = [a, b]
    if has_add:
        in_specs.append(pl.BlockSpec((tm, tk), lambda i, j, k: (i, j)))
        args.append(add)
    return pl.pallas_call(
        body, name=name, grid=(M // tm, K // tk, nn), in_specs=in_specs,
        out_specs=pl.BlockSpec((tm, tk), lambda i, j, k: (i, j)),
        out_shape=jax.ShapeDtypeStruct((M, K), F32),
        scratch_shapes=[pltpu.VMEM((tm, tk), F32)],
        compiler_params=_cparams(('parallel', 'parallel', 'arbitrary')),
    )(*args)


def _mm_tn(a, b, name='mm_tn', blocked=False):
    M, K = a.shape
    N = b.shape[1]
    tm = _pick(M, (1024, 512, 256, 128))
    tk = _pick(K, (512, 384, 256, 128))
    tn = N // N_DEV if blocked else _pick(N, (512, 384, 256, 128))
    nm = M // tm

    def body(a_ref, b_ref, o_ref, acc_ref):
        mm = pl.program_id(2)
        part = lax.dot_general(a_ref[...].astype(BF16), b_ref[...].astype(BF16), (((0,), (0,)), ((), ())),
                               preferred_element_type=F32)

        @pl.when(mm == 0)
        def _():
            acc_ref[...] = part

        @pl.when(mm > 0)
        def _():
            acc_ref[...] += part

        @pl.when(mm == nm - 1)
        def _():
            o_ref[...] = acc_ref[...].astype(BF16)

    if blocked:
        out_spec = pl.BlockSpec((None, tk, tn), lambda i, j, m: (j, i, 0))
        out_shape = jax.ShapeDtypeStruct((N_DEV, K, tn), BF16)
    else:
        out_spec = pl.BlockSpec((tk, tn), lambda i, j, m: (i, j))
        out_shape = jax.ShapeDtypeStruct((K, N), BF16)
    return pl.pallas_call(
        body, name=name, grid=(K // tk, N // tn, nm),
        in_specs=[pl.BlockSpec((tm, tk), lambda i, j, m: (m, i)), pl.BlockSpec((tm, tn), lambda i, j, m: (m, j))],
        out_specs=out_spec, out_shape=out_shape,
        scratch_shapes=[pltpu.VMEM((tk, tn), F32)],
        compiler_params=_cparams(('parallel', 'parallel', 'arbitrary')),
    )(a, b)


@functools.partial(jax.custom_vjp, nondiff_argnums=(4,))
def _linear_add(a, wb, wc, add, name):
    return _mm_nn(a, wb, add, name=name + '_f')


def _linear_add_fwd(a, wb, wc, add, name):
    return _mm_nn(a, wb, add, name=name + '_f'), (a, wb)


def _linear_add_bwd(name, res, dy):
    a, wb = res
    return _mm_nt(dy, wb, name=name + '_da'), None, _mm_tn(a, dy, name=name + '_dw'), dy


_linear_add.defvjp(_linear_add_fwd, _linear_add_bwd)


@functools.partial(jax.custom_vjp, nondiff_argnums=(3,))
def _multi_linear(a, wbs, wcs, name):
    return tuple(_mm_nn(a, wb, name=f'{name}_f{i}') for i, wb in enumerate(wbs))


def _multi_linear_fwd(a, wbs, wcs, name):
    return _multi_linear(a, wbs, wcs, name), (a, wbs)


def _multi_linear_bwd(name, res, dys):
    a, wbs = res
    da = None
    for i, (dy, wb) in enumerate(zip(dys, wbs)):
        da = _mm_nt(dy, wb, add=da, name=f'{name}_da{i}')
    dws = tuple(_mm_tn(a, dy, name=f'{name}_dw{i}') for i, dy in enumerate(dys))
    return da, None, dws


_multi_linear.defvjp(_multi_linear_fwd, _multi_linear_bwd)


class W:
    def __init__(self, b, c):
        self.b, self.c = b, c

    def map(self, fn):
        return W(fn(self.b), fn(self.c))


def _wcat(ws, axis):
    return W(jnp.concatenate([w.b for w in ws], axis), jnp.concatenate([w.c for w in ws], axis))


def linear(a, w, add=None, name='lin'):
    if add is None:
        return _multi_linear(a, (w.b,), (w.c,), name)[0]
    return _linear_add(a, w.b, w.c, add, name)


def ROW(diff=True, pieces=None):
    return ('row', diff, pieces)


def FULL(diff=True):
    return ('full', diff, None)


def _load_args(refs, specs):
    args, amap = [], []
    for i, (ref, (kind, diff, pieces)) in enumerate(zip(refs, specs)):
        if pieces is None:
            args.append(ref[...])
            amap.append((i, None))
        else:
            for (s, w) in pieces:
                args.append(ref[:, s:s + w])
                amap.append((i, (s, w)))
    return args, amap


def _stage_in_specs(ins, specs, tb):
    out = []
    for a, (kind, _, _) in zip(ins, specs):
        if kind == 'row':
            out.append(pl.BlockSpec((tb, a.shape[1]), lambda i: (i, 0)))
        else:
            out.append(pl.BlockSpec(a.shape, lambda i: (0, 0)))
    return out


def _stage_fwd(fn, ins, specs, out_widths, name, tb, out_dtype=F32):
    T = [a for a, s in zip(ins, specs) if s[0] == 'row'][0].shape[0]
    tb = min(tb, T)
    n_in = len(ins)

    def body(*refs):
        args, _ = _load_args(refs[:n_in], specs)
        outs = fn(*args)
        for o_ref, o in zip(refs[n_in:], outs):
            o_ref[...] = o.astype(out_dtype)

    return pl.pallas_call(
        body, name=name + '_f', grid=(T // tb,), in_specs=_stage_in_specs(ins, specs, tb),
        out_specs=[pl.BlockSpec((tb, w), lambda i: (i, 0)) for w in out_widths],
        out_shape=[jax.ShapeDtypeStruct((T, w), out_dtype) for w in out_widths],
        compiler_params=_cparams(('parallel',)),
    )(*ins)


def _stage_bwd(fn, ins, specs, out_widths, douts, name, tb, add_to_first=None):
    T = [a for a, s in zip(ins, specs) if s[0] == 'row'][0].shape[0]
    tb = min(tb, T)
    n_in, n_out = len(ins), len(out_widths)
    diff_inputs = [i for i, s in enumerate(specs) if s[1]]
    n_add = 0 if add_to_first is None else 1

    def body(*refs):
        in_refs, dout_refs = refs[:n_in], refs[n_in:n_in + n_out]
        g_refs = refs[n_in + n_out + n_add:]
        args, amap = _load_args(in_refs, specs)
        didx = [j for j, (i, _) in enumerate(amap) if specs[i][1]]

        def f(*dv):
            full = list(args)
            for j, v in zip(didx, dv):
                full[j] = v
            return tuple(fn(*full))

        _, vjp = jax.vjp(f, *[args[j] for j in didx])
        gs = vjp(tuple(d[...] for d in dout_refs))
        gmap = {j: g for j, g in zip(didx, gs)}
        first = pl.program_id(0) == 0
        for g_ref, i in zip(g_refs, diff_inputs):
            kind, _, pieces = specs[i]
            js = [j for j, (ii, _) in enumerate(amap) if ii == i]
            if kind == 'row':
                if pieces is None:
                    if n_add and i == diff_inputs[0]:
                        g_ref[...] = gmap[js[0]] + refs[n_in + n_out][...]
                    else:
                        g_ref[...] = gmap[js[0]]
                else:
                    if sum(w for _, w in pieces) != ins[i].shape[1]:
                        g_ref[...] = jnp.zeros(g_ref.shape, F32)
                    for j in js:
                        s, w = amap[j][1]
                        g_ref[:, s:s + w] = gmap[j]
            else:
                @pl.when(first)
                def _(g_ref=g_ref):
                    g_ref[...] = jnp.zeros(g_ref.shape, F32)

                g_ref[...] += gmap[js[0]]

    in_specs = _stage_in_specs(ins, specs, tb) + [pl.BlockSpec((tb, w), lambda i: (i, 0)) for w in out_widths]
    extra = []
    if n_add:
        in_specs.append(pl.BlockSpec((tb, add_to_first.shape[1]), lambda i: (i, 0)))
        extra.append(add_to_first)
    out_specs, out_shape = [], []
    for i in diff_inputs:
        a = ins[i]
        if specs[i][0] == 'row':
            out_specs.append(pl.BlockSpec((tb, a.shape[1]), lambda i: (i, 0)))
        else:
            out_specs.append(pl.BlockSpec(a.shape, lambda i: (0, 0)))
        out_shape.append(jax.ShapeDtypeStruct(a.shape, F32))
    return pl.pallas_call(
        body, name=name + '_b', grid=(T // tb,), in_specs=in_specs, out_specs=out_specs, out_shape=out_shape,
        compiler_params=_cparams(('arbitrary',)),
    )(*ins, *douts, *extra)


def stage_op(fn, specs, out_widths, name, tb=256):
    n = len(specs)
    diff_inputs = [i for i, s in enumerate(specs) if s[1]]

    @jax.custom_vjp
    def op(*ins):
        return tuple(_stage_fwd(fn, ins, specs, out_widths, name, tb))

    def op_fwd(*ins):
        return op(*ins), ins

    def op_bwd(ins, douts):
        gs = _stage_bwd(fn, ins, specs, out_widths, douts, name, tb)
        res = [None] * n
        for i, g in zip(diff_inputs, gs):
            res[i] = g
        return tuple(res)

    op.defvjp(op_fwd, op_bwd)
    return op


@jax.custom_vjp
def bdot(x, w):
    return jnp.dot(x.astype(BF16), w.astype(BF16), preferred_element_type=F32)


def _bdot_fwd(x, w):
    return bdot(x, w), (x, w)


def _bdot_bwd(res, dy):
    x, w = res
    dyb = dy.astype(BF16)
    dx = lax.dot_general(dyb, w.astype(BF16), (((1,), (1,)), ((), ())), preferred_element_type=F32)
    dw = lax.dot_general(x.astype(BF16), dyb, (((0,), (0,)), ((), ())), preferred_element_type=F32)
    return dx, dw


bdot.defvjp(_bdot_fwd, _bdot_bwd)


def _sdot_raw(x, c):
    hi = x.astype(BF16)
    r1 = x - hi.astype(F32)
    mid = r1.astype(BF16)
    lo = (r1 - mid.astype(F32)).astype(BF16)
    d = lambda u: jnp.dot(u, c, preferred_element_type=F32)
    return d(hi) + d(mid) + d(lo)


@jax.custom_vjp
def sdot(x, c, ct):
    return _sdot_raw(x, c)


def _sdot_fwd(x, c, ct):
    return _sdot_raw(x, c), (c, ct)


def _sdot_bwd(res, dy):
    c, ct = res
    return _sdot_raw(dy, ct), None, None


sdot.defvjp(_sdot_fwd, _sdot_bwd)


def _sigmoid(x):
    return 1.0 / (1.0 + jnp.exp(-x))


def _rms(x, g):
    return x * lax.rsqrt(jnp.mean(x * x, axis=-1, keepdims=True) + NORM_EPS) * g


def _mm_up_relu2(a, b, name):
    M, K = a.shape
    tn = b.shape[2]
    N = N_DEV * tn
    tm = _pick(M, (1024, 512, 256, 128))

    def body(a_ref, b_ref, u_ref, act_ref):
        u = jnp.dot(a_ref[...], b_ref[...], preferred_element_type=F32)
        r = jnp.maximum(u, 0.0)
        u_ref[...] = u.astype(BF16)
        act_ref[...] = (r * r).astype(BF16)

    out = pl.BlockSpec((tm, tn), lambda i, j: (i, j))
    sh = jax.ShapeDtypeStruct((M, N), BF16)
    return pl.pallas_call(
        body, name=name, grid=(M // tm, N // tn),
        in_specs=[pl.BlockSpec((tm, K), lambda i, j: (i, 0)), pl.BlockSpec((None, K, tn), lambda i, j: (j, 0, 0))],
        out_specs=[out, out], out_shape=[sh, sh], compiler_params=_cparams(('parallel', 'parallel')),
    )(a, b)


def _mm_down_bwd(dy, b, u, name):
    M, N = dy.shape
    K = b.shape[0]
    tm = _pick(M, (1024, 512, 256, 128))
    tk = _pick(K, (512, 256, 128))

    def body(dy_ref, b_ref, u_ref, du_ref):
        d = lax.dot_general(dy_ref[...].astype(BF16), b_ref[...], (((1,), (1,)), ((), ())),
                            preferred_element_type=F32)
        du_ref[...] = (d * (2.0 * jnp.maximum(u_ref[...].astype(F32), 0.0))).astype(BF16)

    blk = pl.BlockSpec((tm, tk), lambda i, j: (i, j))
    return pl.pallas_call(
        body, name=name, grid=(M // tm, K // tk),
        in_specs=[pl.BlockSpec((tm, N), lambda i, j: (i, 0)), pl.BlockSpec((tk, N), lambda i, j: (j, 0)), blk],
        out_specs=blk, out_shape=jax.ShapeDtypeStruct((M, K), BF16),
        compiler_params=_cparams(('parallel', 'parallel')),
    )(dy, b, u)


_RMS_SPECS = [ROW(), FULL()]
_rms_fn = lambda xv, gv: (_rms(xv, gv),)


@functools.partial(jax.custom_vjp, nondiff_argnums=(6,))
def mlp_block(x, g, wup_b, wup_c, wdown_b, wdown_c, name):
    return _mlp_fwd(x, g, wup_b, wup_c, wdown_b, wdown_c, name)[0]


def _mlp_fwd(x, g, wup_b, wup_c, wdown_b, wdown_c, name):
    h = _stage_fwd(_rms_fn, (x, g), _RMS_SPECS, [x.shape[1]], name + '_norm', 256, out_dtype=BF16)[0]
    u, act = _mm_up_relu2(h, wup_b, name + '_up')
    y = _mm_nn(act, wdown_b, add=x, name=name + '_down')
    return y, (x, g, h, u, act, wup_b, wdown_b)


def _mlp_bwd(name, res, dy):
    x, g, h, u, act, wup_b, wdown_b = res
    du = _mm_down_bwd(dy, wdown_b, u, name + '_down_da')
    dwdown = _mm_tn(act, dy, name=name + '_down_dw')
    dh = _mm_nt(du, wup_b, name=name + '_up_da')
    dwup = _mm_tn(h, du, name=name + '_up_dw', blocked=True)
    dx, dg = _stage_bwd(_rms_fn, (x, g), _RMS_SPECS, [x.shape[1]], (dh,), name + '_norm', 256, add_to_first=dy)
    return dx, dg, None, dwup, None, dwdown


mlp_block.defvjp(_mlp_fwd, _mlp_bwd)


@functools.partial(jax.custom_vjp, nondiff_argnums=(4, 5, 6))
def _norm_projections(x, g, wbs, wcs, transposed, skip, name):
    return _norm_projections_fwd(x, g, wbs, wcs, transposed, skip, name)[0]


def _norm_projections_fwd(x, g, wbs, wcs, transposed, skip, name):
    h = _stage_fwd(_rms_fn, (x, g), _RMS_SPECS, [x.shape[1]], name + '_norm', 256, out_dtype=BF16)[0]
    mm = _mm_nt if transposed else _mm_nn
    outs = tuple(mm(h, wb, name=f'{name}_f{i}') for i, wb in enumerate(wbs))
    return ((outs, x) if skip else outs), (x, g, h, wbs)


def _norm_projections_bwd(transposed, skip, name, res, cts):
    x, g, h, wbs = res
    dys, dx_skip = cts if skip else (cts, None)
    dh = None
    for i, (dy, wb) in enumerate(zip(dys, wbs)):
        dh = (_mm_nn if transposed else _mm_nt)(dy, wb, add=dh, name=f'{name}_da{i}')
    if transposed:
        dws = tuple(_mm_tn(dy, h, name=f'{name}_dw{i}') for i, dy in enumerate(dys))
    else:
        dws = tuple(_mm_tn(h, dy, name=f'{name}_dw{i}') for i, dy in enumerate(dys))
    dx, dg = _stage_bwd(_rms_fn, (x, g), _RMS_SPECS, [x.shape[1]], (dh,), name + '_norm', 256, add_to_first=dx_skip)
    return dx, dg, None, dws


_norm_projections.defvjp(_norm_projections_fwd, _norm_projections_bwd)


def norm_projections(x, g, ws, name, transposed=False, skip=False):
    return _norm_projections(x, g.reshape(1, -1), tuple(w.b for w in ws), tuple(w.c for w in ws), transposed, skip,
                             name)


def _shift_down(x, rows):
    return jnp.where(rows == 0, 0.0, pltpu.roll(x, 1, 0))


def _shift_up(x, rows, T):
    return jnp.where(rows == T - 1, 0.0, pltpu.roll(x, T - 1, 0))


def _tshift_fwd_call(x, mu, name):
    T, C = x.shape

    def body(x_ref, mu_ref, o_ref):
        xv = x_ref[...]
        rows = lax.broadcasted_iota(jnp.int32, xv.shape, 0)
        o_ref[...] = xv + (_shift_down(xv, rows) - xv) * mu_ref[...]

    return pl.pallas_call(
        body, name=name + '_f', grid=(C // LANES,),
        in_specs=[pl.BlockSpec((T, LANES), lambda j: (0, j)), pl.BlockSpec((1, LANES), lambda j: (0, j))],
        out_specs=pl.BlockSpec((T, LANES), lambda j: (0, j)), out_shape=jax.ShapeDtypeStruct((T, C), F32),
        compiler_params=_cparams(('parallel',)),
    )(x, mu)


def _tshift_bwd_call(x, mu, dy, name):
    T, C = x.shape

    def body(x_ref, mu_ref, dy_ref, dx_ref, dmu_ref):
        xv, d = x_ref[...], dy_ref[...]
        rows = lax.broadcasted_iota(jnp.int32, xv.shape, 0)
        z = d * mu_ref[...]
        dx_ref[...] = d - z + _shift_up(z, rows, T)
        dmu_ref[...] = jnp.sum(d * (_shift_down(xv, rows) - xv), axis=0, keepdims=True)

    return pl.pallas_call(
        body, name=name + '_b', grid=(C // LANES,),
        in_specs=[pl.BlockSpec((T, LANES), lambda j: (0, j)), pl.BlockSpec((1, LANES), lambda j: (0, j)),
                  pl.BlockSpec((T, LANES), lambda j: (0, j))],
        out_specs=[pl.BlockSpec((T, LANES), lambda j: (0, j)), pl.BlockSpec((1, LANES), lambda j: (0, j))],
        out_shape=[jax.ShapeDtypeStruct((T, C), F32), jax.ShapeDtypeStruct((1, C), F32)],
        compiler_params=_cparams(('parallel',)),
    )(x, mu, dy)


@functools.partial(jax.custom_vjp, nondiff_argnums=(2,))
def token_shift_mix(x, mu, name):
    return _tshift_fwd_call(x, mu, name)


def _tsm_fwd(x, mu, name):
    return _tshift_fwd_call(x, mu, name), (x, mu)


def _tsm_bwd(name, res, dy):
    x, mu = res
    dx, dmu = _tshift_bwd_call(x, mu, dy, name)
    return dx, dmu


token_shift_mix.defvjp(_tsm_fwd, _tsm_bwd)


def _conv_specs(T):
    nb = CONV_WIDTH // LANES
    return [pl.BlockSpec((T, LANES), lambda j: (0, j)), pl.BlockSpec((T, LANES), lambda j: (0, nb + j)),
            pl.BlockSpec((T, LANES), lambda j: (0, 2 * nb + j)), pl.BlockSpec((3, LANES), lambda j: (0, j))]


def _conv_fwd_call(cv, w, name):
    T = cv.shape[0]

    def body(b_ref, c_ref, x_ref, w_ref, o_ref):
        u = c_ref[...] * x_ref[...]
        rows = lax.broadcasted_iota(jnp.int32, u.shape, 0)
        u1 = _shift_down(u, rows)
        u2 = _shift_down(u1, rows)
        o_ref[...] = b_ref[...] * (w_ref[0:1, :] * u2 + w_ref[1:2, :] * u1 + w_ref[2:3, :] * u)

    return pl.pallas_call(
        body, name=name + '_f', grid=(CONV_WIDTH // LANES,), in_specs=_conv_specs(T),
        out_specs=pl.BlockSpec((T, LANES), lambda j: (0, j)),
        out_shape=jax.ShapeDtypeStruct((T, CONV_WIDTH), F32), compiler_params=_cparams(('parallel',)),
    )(cv, cv, cv, w)


def _conv_bwd_call(cv, w, do, name):
    T = cv.shape[0]

    def body(b_ref, c_ref, x_ref, w_ref, do_ref, db_ref, dc_ref, dx_ref, dw_ref):
        c, x, d = c_ref[...], x_ref[...], do_ref[...]
        u = c * x
        rows = lax.broadcasted_iota(jnp.int32, u.shape, 0)
        u1 = _shift_down(u, rows)
        u2 = _shift_down(u1, rows)
        w0, w1, w2 = w_ref[0:1, :], w_ref[1:2, :], w_ref[2:3, :]
        db_ref[...] = d * (w0 * u2 + w1 * u1 + w2 * u)
        dy = d * b_ref[...]
        dy1 = _shift_up(dy, rows, T)
        dy2 = _shift_up(dy1, rows, T)
        du = w2 * dy + w1 * dy1 + w0 * dy2
        dc_ref[...] = du * x
        dx_ref[...] = du * c
        dw_ref[0:1, :] = jnp.sum(dy * u2, axis=0, keepdims=True)
        dw_ref[1:2, :] = jnp.sum(dy * u1, axis=0, keepdims=True)
        dw_ref[2:3, :] = jnp.sum(dy * u, axis=0, keepdims=True)

    blk = pl.BlockSpec((T, LANES), lambda j: (0, j))
    sh = jax.ShapeDtypeStruct((T, CONV_WIDTH), F32)
    return pl.pallas_call(
        body, name=name + '_b', grid=(CONV_WIDTH // LANES,), in_specs=_conv_specs(T) + [blk],
        out_specs=[blk, blk, blk, pl.BlockSpec((3, LANES), lambda j: (0, j))],
        out_shape=[sh, sh, sh, jax.ShapeDtypeStruct((3, CONV_WIDTH), F32)],
        compiler_params=_cparams(('parallel',)),
    )(cv, cv, cv, w, do)


@functools.partial(jax.custom_vjp, nondiff_argnums=(2,))
def short_conv(cv, w, name):
    return _conv_fwd_call(cv, w, name)


def _sc_fwd(cv, w, name):
    return _conv_fwd_call(cv, w, name), (cv, w)


def _sc_bwd(name, res, do):
    cv, w = res
    db, dc, dx, dw = _conv_bwd_call(cv, w, do, name)
    return jnp.concatenate([db, dc, dx], axis=1), dw


short_conv.defvjp(_sc_fwd, _sc_bwd)


ATT_SCALE = QK_HEAD ** -0.5
NPAIR = MLA_HEADS // 2


def _att_bq(T):
    return min(256, T)


def _att_masks(pair, j):
    lane = lax.broadcasted_iota(jnp.int32, (1, LANES), 1)
    mask_n = (lane // QK_NOPE) == j
    mask_r = (lane // (QK_ROPE // 2)) == (2 * pair + j)
    return mask_n, mask_r


def _att_probs(qcat, kcat, row0, stop):
    s = lax.dot_general(qcat, kcat, (((1,), (1,)), ((), ())), preferred_element_type=F32) * ATT_SCALE
    r = row0 + lax.broadcasted_iota(jnp.int32, s.shape, 0)
    c = lax.broadcasted_iota(jnp.int32, s.shape, 1)
    s = jnp.where(c <= r, s, -jnp.inf)
    e = jnp.exp(s - jnp.max(s, axis=-1, keepdims=True))
    return e / jnp.sum(e, axis=-1, keepdims=True)


def _att_in_specs(T):
    blk = lambda f: pl.BlockSpec((T, LANES), f)
    return [blk(lambda p: (0, p)), blk(lambda p: (0, 0)), blk(lambda p: (0, 0)),
            blk(lambda p: (0, p)), blk(lambda p: (0, 0)), blk(lambda p: (0, 0)), blk(lambda p: (0, p))]


def _att_fwd_call(qn, q1, q2, kn, k1, k2, v, name):
    T = qn.shape[0]
    bq = _att_bq(T)

    def body(qn_ref, q1_ref, q2_ref, kn_ref, k1_ref, k2_ref, v_ref, o_ref):
        pair = pl.program_id(0)
        for i in range(T // bq):
            r0, stop = i * bq, (i + 1) * bq
            kcat = jnp.concatenate([kn_ref[0:stop, :], k1_ref[0:stop, :], k2_ref[0:stop, :]], axis=1).astype(BF16)
            vb = v_ref[0:stop, :].astype(BF16)
            outs = []
            for j in range(2):
                mask_n, mask_r = _att_masks(pair, j)
                qcat = jnp.concatenate([jnp.where(mask_n, qn_ref[r0:stop, :], 0.0),
                                        jnp.where(mask_r, q1_ref[r0:stop, :], 0.0),
                                        jnp.where(mask_r, q2_ref[r0:stop, :], 0.0)], axis=1).astype(BF16)
                p = _att_probs(qcat, kcat, r0, stop)
                outs.append(jnp.dot(p.astype(BF16), vb, preferred_element_type=F32))
            mask_n0, _ = _att_masks(pair, 0)
            o_ref[r0:stop, :] = jnp.where(mask_n0, outs[0], outs[1])

    return pl.pallas_call(
        body, name=name + '_f', grid=(NPAIR,), in_specs=_att_in_specs(T),
        out_specs=pl.BlockSpec((T, LANES), lambda p: (0, p)),
        out_shape=jax.ShapeDtypeStruct((T, MLA_HEADS * V_HEAD), F32), compiler_params=_cparams(('parallel',)),
    )(qn, q1, q2, kn, k1, k2, v)


def _att_bwd_call(qn, q1, q2, kn, k1, k2, v, o, do, name):
    T = qn.shape[0]
    bq = _att_bq(T)

    def body(qn_ref, q1_ref, q2_ref, kn_ref, k1_ref, k2_ref, v_ref, o_ref, do_ref,
             dqn_ref, dq1_ref, dq2_ref, dkn_ref, dk1_ref, dk2_ref, dv_ref, dk_acc, dv_acc):
        pair = pl.program_id(0)

        @pl.when(pair == 0)
        def _():
            dq1_ref[...] = jnp.zeros(dq1_ref.shape, F32)
            dq2_ref[...] = jnp.zeros(dq2_ref.shape, F32)
            dk1_ref[...] = jnp.zeros(dk1_ref.shape, F32)
            dk2_ref[...] = jnp.zeros(dk2_ref.shape, F32)

        dk_acc[...] = jnp.zeros(dk_acc.shape, F32)
        dv_acc[...] = jnp.zeros(dv_acc.shape, F32)
        for i in range(T // bq):
            r0, stop = i * bq, (i + 1) * bq
            kcat = jnp.concatenate([kn_ref[0:stop, :], k1_ref[0:stop, :], k2_ref[0:stop, :]], axis=1).astype(BF16)
            vb = v_ref[0:stop, :].astype(BF16)
            dqn = jnp.zeros((bq, LANES), F32)
            for j in range(2):
                mask_n, mask_r = _att_masks(pair, j)
                qcat = jnp.concatenate([jnp.where(mask_n, qn_ref[r0:stop, :], 0.0),
                                        jnp.where(mask_r, q1_ref[r0:stop, :], 0.0),
                                        jnp.where(mask_r, q2_ref[r0:stop, :], 0.0)], axis=1).astype(BF16)
                p = _att_probs(qcat, kcat, r0, stop)
                dom = jnp.where(mask_n, do_ref[r0:stop, :], 0.0)
                delta = jnp.sum(dom * o_ref[r0:stop, :], axis=-1, keepdims=True)
                domb = dom.astype(BF16)
                dp = lax.dot_general(domb, vb, (((1,), (1,)), ((), ())), preferred_element_type=F32)
                ds = (p * (dp - delta) * ATT_SCALE).astype(BF16)
                dqc = jnp.dot(ds, kcat, preferred_element_type=F32)
                dqn = dqn + jnp.where(mask_n, dqc[:, 0:LANES], 0.0)
                dq1_ref[r0:stop, :] += jnp.where(mask_r, dqc[:, LANES:2 * LANES], 0.0)
                dq2_ref[r0:stop, :] += jnp.where(mask_r, dqc[:, 2 * LANES:3 * LANES], 0.0)
                dk_acc[0:stop, :] += lax.dot_general(ds, qcat, (((0,), (0,)), ((), ())),
                                                     preferred_element_type=F32)
                dv_acc[0:stop, :] += lax.dot_general(p.astype(BF16), domb, (((0,), (0,)), ((), ())),
                                                     preferred_element_type=F32)
            dqn_ref[r0:stop, :] = dqn
        dkn_ref[...] = dk_acc[:, 0:LANES]
        dk1_ref[...] += dk_acc[:, LANES:2 * LANES]
        dk2_ref[...] += dk_acc[:, 2 * LANES:3 * LANES]
        dv_ref[...] = dv_acc[...]

    per_pair = pl.BlockSpec((T, LANES), lambda p: (0, p))
    shared = pl.BlockSpec((T, LANES), lambda p: (0, 0))
    wide = jax.ShapeDtypeStruct((T, MLA_HEADS * QK_NOPE), F32)
    narrow = jax.ShapeDtypeStruct((T, LANES), F32)
    return pl.pallas_call(
        body, name=name + '_b', grid=(NPAIR,), in_specs=_att_in_specs(T) + [per_pair, per_pair],
        out_specs=[per_pair, shared, shared, per_pair, shared, shared, per_pair],
        out_shape=[wide, narrow, narrow, wide, narrow, narrow, wide],
        scratch_shapes=[pltpu.VMEM((T, 3 * LANES), F32), pltpu.VMEM((T, LANES), F32)],
        compiler_params=_cparams(('arbitrary',)),
    )(qn, q1, q2, kn, k1, k2, v, o, do)


@functools.partial(jax.custom_vjp, nondiff_argnums=(7,))
def attention(qn, q1, q2, kn, k1, k2, v, name):
    return _att_fwd_call(qn, q1, q2, kn, k1, k2, v, name)


def _attn_fwd(qn, q1, q2, kn, k1, k2, v, name):
    o = _att_fwd_call(qn, q1, q2, kn, k1, k2, v, name)
    return o, (qn, q1, q2, kn, k1, k2, v, o)


def _attn_bwd(name, res, do):
    return tuple(_att_bwd_call(*res, do, name))


attention.defvjp(_attn_fwd, _attn_bwd)


SCAN_CHUNK = 64
SCAN_UNROLL = 8


def _block_ones(n, seg):
    i = np.arange(n)
    return (i[:, None] // seg == i[None, :] // seg).astype(np.float32)


def _scan_diag():
    i = np.arange(RW_WIDTH)
    return jnp.asarray((np.arange(RW_N)[:, None] == (i[None, :] % RW_N)).astype(np.float32))


def _head_rowsum(x):
    low = lax.broadcasted_iota(jnp.int32, (1, LANES), 1) < RW_N
    tiles = []
    for j in range(RW_WIDTH // LANES):
        xt = x[:, j * LANES:(j + 1) * LANES]
        x0 = jnp.where(low, xt, 0.0)
        s0 = jnp.sum(x0, axis=-1, keepdims=True)
        s1 = jnp.sum(xt - x0, axis=-1, keepdims=True)
        tiles.append(jnp.where(low, s0, s1))
    return jnp.concatenate(tiles, axis=1)


def _unrolled_loop(n, step, init):
    def body(i, carry):
        for j in range(SCAN_UNROLL):
            carry = step(i * SCAN_UNROLL + j, carry)
        return carry
    return lax.fori_loop(0, n // SCAN_UNROLL, body, init)


def _scan_fwd_call(r, w, k, v, a, b, name):
    T = r.shape[0]
    tc = min(SCAN_CHUNK, T)
    dg = _scan_diag()

    def body(r_ref, w_ref, k_ref, v_ref, a_ref, b_ref, dg_ref, y_ref, st_ref, sa_ref, vc_ref, s_ref):
        @pl.when(pl.program_id(0) == 0)
        def _():
            s_ref[...] = jnp.zeros(s_ref.shape, F32)

        dgv = dg_ref[...]
        readout = lambda s, t: jnp.sum(_head_rowsum(s * r_ref[t]) * dgv, axis=0, keepdims=True)

        def step(t, carry):
            s, vcol = carry
            st_ref[t] = s
            vc_ref[t] = vcol
            sa = _head_rowsum(s * a_ref[t])
            sa_ref[t] = sa
            prev = jnp.maximum(t - 1, 0)
            y_ref[prev] = readout(s, prev)
            vcol_next = _head_rowsum(dgv * v_ref[jnp.minimum(t + 1, tc - 1)])
            sn = s * w_ref[t] + sa * b_ref[t] + vcol * k_ref[t]
            return sn, vcol_next

        s_end, _ = _unrolled_loop(tc, step, (s_ref[...], _head_rowsum(dgv * v_ref[0])))
        y_ref[tc - 1] = readout(s_end, tc - 1)
        s_ref[...] = s_end

    vec = pl.BlockSpec((tc, 1, RW_WIDTH), lambda i: (i, 0, 0))
    mat = pl.BlockSpec((tc, RW_N, RW_WIDTH), lambda i: (i, 0, 0))
    msh = jax.ShapeDtypeStruct((T, RW_N, RW_WIDTH), F32)
    return pl.pallas_call(
        body, name=name + '_f', grid=(T // tc,),
        in_specs=[vec] * 6 + [pl.BlockSpec((RW_N, RW_WIDTH), lambda i: (0, 0))],
        out_specs=[vec, mat, mat, mat],
        out_shape=[jax.ShapeDtypeStruct((T, 1, RW_WIDTH), F32), msh, msh, msh],
        scratch_shapes=[pltpu.VMEM((RW_N, RW_WIDTH), F32)],
        compiler_params=_cparams(('arbitrary',)),
    )(r, w, k, v, a, b, dg)


def _scan_bwd_call(r, w, k, a, b, st, sa_all, vc_all, dy, name):
    T = r.shape[0]
    tc = min(SCAN_CHUNK, T)
    nt = T // tc
    dg = _scan_diag()

    def body(r_ref, w_ref, k_ref, a_ref, b_ref, st_ref, sa_ref, vc_ref, dy_ref, dg_ref,
             dr_ref, dw_ref, dk_ref, dv_ref, da_ref, db_ref, ds_ref):
        @pl.when(pl.program_id(0) == 0)
        def _():
            ds_ref[...] = jnp.zeros(ds_ref.shape, F32)

        dgv = dg_ref[...]
        colsum = lambda x: jnp.sum(x, axis=0, keepdims=True)

        def step(i, carry):
            ds, dycol = carry
            t = tc - 1 - i
            sp = st_ref[t]
            rt, wt, kt, at, bt = r_ref[t], w_ref[t], k_ref[t], a_ref[t], b_ref[t]
            ds = ds + dycol * rt
            dsa = _head_rowsum(ds * bt)
            sa, vcol = sa_ref[t], vc_ref[t]
            dycol_next = _head_rowsum(dgv * dy_ref[jnp.maximum(t - 1, 0)])
            sn = sp * wt + sa * bt + vcol * kt
            dr_ref[t] = colsum(sn * dycol)
            dk_ref[t] = colsum(ds * vcol)
            db_ref[t] = colsum(ds * sa)
            dw_ref[t] = colsum(ds * sp)
            dv_ref[t] = colsum(_head_rowsum(ds * kt) * dgv)
            da_ref[t] = colsum(sp * dsa)
            return ds * wt + dsa * at, dycol_next

        ds_end, _ = _unrolled_loop(tc, step, (ds_ref[...], _head_rowsum(dgv * dy_ref[tc - 1])))
        ds_ref[...] = ds_end

    vec = pl.BlockSpec((tc, 1, RW_WIDTH), lambda i: (nt - 1 - i, 0, 0))
    mat = pl.BlockSpec((tc, RW_N, RW_WIDTH), lambda i: (nt - 1 - i, 0, 0))
    vsh = jax.ShapeDtypeStruct((T, 1, RW_WIDTH), F32)
    return pl.pallas_call(
        body, name=name + '_b', grid=(nt,),
        in_specs=[vec] * 5 + [mat] * 3 + [vec, pl.BlockSpec((RW_N, RW_WIDTH), lambda i: (0, 0))],
        out_specs=[vec] * 6, out_shape=[vsh] * 6,
        scratch_shapes=[pltpu.VMEM((RW_N, RW_WIDTH), F32)],
        compiler_params=_cparams(('arbitrary',)),
    )(r, w, k, a, b, st, sa_all, vc_all, dy, dg)


@functools.partial(jax.custom_vjp, nondiff_argnums=(6,))
def wkv7(r, w, k, v, a, b, name):
    return _scan_fwd_call(r, w, k, v, a, b, name)[0]


def _wkv7_fwd(r, w, k, v, a, b, name):
    y, st, sa_all, vc_all = _scan_fwd_call(r, w, k, v, a, b, name)
    return y, (r, w, k, a, b, st, sa_all, vc_all)


def _wkv7_bwd(name, res, dy):
    return tuple(_scan_bwd_call(*res, dy, name))


wkv7.defvjp(_wkv7_fwd, _wkv7_bwd)


def _np_bf16(a):
    return jnp.asarray(a, BF16)


def _mla_consts():
    seg_n = (np.arange(512)[:, None] // QK_NOPE == np.arange(LANES)[None, :]).astype(np.float32)
    seg_r = (np.arange(LANES)[:, None] // 16 == np.arange(LANES)[None, :]).astype(np.float32)
    e1 = np.zeros((LANES, LANES), np.float32)
    e2 = np.zeros((LANES, LANES), np.float32)
    for h in range(MLA_HEADS):
        for i in range(16):
            e1[i, h * 16 + i] = 1.0
            e2[16 + i, h * 16 + i] = 1.0
    mats = [seg_n, seg_n.T, seg_r, seg_r.T, e1, e1.T, e2, e2.T]
    return [_np_bf16(m) for m in mats]


def _qk_prep_fn(qn, q1, q2, kn, kx, cos, sin, gqn, gq1, gq2, gkn, gk1, gk2,
                seg_n, seg_nt, seg_r, seg_rt, e1, e1t, e2, e2t):
    def normrope(xn, x1, x2, gn, g1, g2):
        ss = sdot(xn * xn, seg_n, seg_nt) + sdot(x1 * x1, seg_r, seg_rt) + sdot(x2 * x2, seg_r, seg_rt)
        inv = lax.rsqrt(ss * (1.0 / QK_HEAD) + NORM_EPS)
        inv_n = sdot(inv, seg_nt, seg_n)
        inv_r = sdot(inv, seg_rt, seg_r)
        y1 = x1 * inv_r * g1
        y2 = x2 * inv_r * g2
        return xn * inv_n * gn, y1 * cos - y2 * sin, y1 * sin + y2 * cos

    k1 = sdot(kx, e1, e1t)
    k2 = sdot(kx, e2, e2t)
    return normrope(qn, q1, q2, gqn, gq1, gq2) + normrope(kn, k1, k2, gkn, gk1, gk2)


def _rwkv_prep_fn(vres):
    def fn(r, k, v, xg, xwa, kx, *rest):
        if vres:
            vfirst, w0, a0, k_k, k_a, w2p, a2p, g2, v0, v2p, bm = rest
        else:
            w0, a0, k_k, k_a, w2p, a2p, g2, bm = rest
        z = w0 + bdot(jnp.tanh(xwa), w2p)
        nz = -z
        softplus = jnp.maximum(nz, 0.0) + jnp.log(1.0 + jnp.exp(-jnp.abs(nz)))
        decay = jnp.exp(-jnp.exp(-softplus - 0.5))
        a = _sigmoid(a0 + bdot(xwa, a2p))
        g = bdot(_sigmoid(xg), g2)
        if vres:
            vv = v + (vfirst - v) * _sigmoid(v0 + bdot(kx, v2p))
        else:
            vv = v
        kkr = k * k_k
        kk = kkr / jnp.maximum(jnp.sqrt(sdot(kkr * kkr, bm, bm)), 1e-12)
        k2 = k * (1.0 + (a - 1.0) * k_a)
        return r * 1.0, decay, k2, vv, -kk, kk * a, g
    return fn


def _rwkv_post_fn(y, r, k2, vv, g, ln_w, ln_b, rk, bm):
    inv_n = 1.0 / RW_N
    mean = sdot(y, bm, bm) * inv_n
    yc = y - mean
    var = sdot(yc * yc, bm, bm) * inv_n
    yn = yc * lax.rsqrt(var + GN_EPS) * ln_w + ln_b
    bonus = sdot(r * k2 * rk, bm, bm) * vv
    return ((yn + bonus) * g,)


def _merge_fn(g0, g1, g2, oa, ob, oc):
    return (_sigmoid(g0) * oa + _sigmoid(g1) * ob + _sigmoid(g2) * oc,)


def _loss_call(y, target):
    T, C = y.shape
    tb = min(256, T)

    def body(y_ref, t_ref, dy_ref, part_ref):
        err = y_ref[...] - t_ref[...]
        dy_ref[...] = err * (1.0 / C)
        sq = jnp.sum(err * err, axis=0, keepdims=True)
        acc = sq[:, 0:LANES]
        for j in range(1, C // LANES):
            acc = acc + sq[:, j * LANES:(j + 1) * LANES]
        part_ref[...] = jnp.zeros(part_ref.shape, F32)
        part_ref[0:1, :] = acc * (0.5 / C)

    return pl.pallas_call(
        body, name='loss', grid=(T // tb,),
        in_specs=[pl.BlockSpec((tb, C), lambda i: (i, 0))] * 2,
        out_specs=[pl.BlockSpec((tb, C), lambda i: (i, 0)), pl.BlockSpec((8, LANES), lambda i: (i, 0))],
        out_shape=[jax.ShapeDtypeStruct((T, C), F32), jax.ShapeDtypeStruct((8 * (T // tb), LANES), F32)],
        compiler_params=_cparams(('parallel',)),
    )(y, target)


def _pad_rows(t, before, total):
    return jnp.pad(t, ((before, total - before - t.shape[0]), (0, 0)))


def _head_tile(g, lo, hi):
    return jnp.tile(g[lo:hi], MLA_HEADS).reshape(1, -1)


IN_WEIGHTS = ('w_in', 'rwkv_v1')


def _layer(l, x, v_first, wd, sp, cos, sin):
    proj, x_skip = _layer_in(l, x, wd, sp)
    return _layer_rest(l, proj, x_skip, v_first, wd, sp, cos, sin)


def _layer_in(l, x, wd, sp):
    nm = f'l{l}'
    vres = l > 0
    w_in = wd['w_in']
    if vres:
        v1t = wd['rwkv_v1'].map(lambda t: t.T)
    else:
        v1t = W(jnp.zeros((MV_LORA, D_MODEL), BF16), jnp.zeros((MV_LORA, D_MODEL), BF16))
    zpad = W(jnp.zeros((64, D_MODEL), BF16), jnp.zeros((64, D_MODEL), BF16))
    w_rw = _wcat([w_in['rkv'], w_in['xg'], w_in['xwa'], w_in['kpe'], v1t, zpad], 0)
    return norm_projections(
        x, sp['attn_norm'], [w_in['gates'], w_in['cq'], w_in['ckv'], w_rw, w_in['conv']], nm + '_win',
        transposed=True, skip=True)


def _layer_rest(l, proj, x_skip, v_first, wd, sp, cos, sin):
    gates, cq, ckv, rw, cv = proj
    T = rw.shape[0]
    nm = f'l{l}'
    vres = l > 0

    v_mu = sp['rwkv_v_mu'] if vres else jnp.zeros((MV_LORA,), F32)
    mu_all = jnp.concatenate([sp['rwkv_mu'][0:768], sp['rwkv_mu'][896:1024], sp['rwkv_mu'][768:896],
                              jnp.zeros((QK_ROPE,), F32), v_mu, jnp.zeros((64,), F32)]).reshape(1, -1)
    rws = token_shift_mix(rw, mu_all, nm + '_shift')

    wq = wd['mla_wq_b'].map(lambda t: jnp.concatenate(
        [t.reshape(Q_LORA, MLA_HEADS, QK_HEAD)[:, :, 0:64].reshape(Q_LORA, 512),
         t.reshape(Q_LORA, MLA_HEADS, QK_HEAD)[:, :, 64:80].reshape(Q_LORA, 128),
         t.reshape(Q_LORA, MLA_HEADS, QK_HEAD)[:, :, 80:96].reshape(Q_LORA, 128)], axis=1))
    wkn = wd['mla_wkv_b'].map(lambda t: t.reshape(KV_LORA, MLA_HEADS, 128)[:, :, 0:64].reshape(KV_LORA, 512))
    wv = wd['mla_wkv_b'].map(lambda t: t.reshape(KV_LORA, MLA_HEADS, 128)[:, :, 64:128].reshape(KV_LORA, 512))
    q, = norm_projections(cq, sp['mla_q_a_norm'], [wq], nm + '_wq')
    kn, vv_att = norm_projections(ckv, sp['mla_kv_a_norm'], [wkn, wv], nm + '_wkv')
    gq, gk = sp['mla_q_norm'], sp['mla_k_norm']
    consts = _mla_consts()
    qk_specs = ([ROW(pieces=((0, 512), (512, 128), (640, 128))), ROW(), ROW(pieces=((1024, 128),)),
                 ROW(False), ROW(False)] + [FULL()] * 6 + [FULL(False)] * 8)
    qk_op = stage_op(_qk_prep_fn, qk_specs, [512, 128, 128, 512, 128, 128], nm + '_qkprep')
    Qn, Q1, Q2, Kn, K1, K2 = qk_op(q, kn, rws, cos, sin,
                                   _head_tile(gq, 0, 64), _head_tile(gq, 64, 80), _head_tile(gq, 80, 96),
                                   _head_tile(gk, 0, 64), _head_tile(gk, 64, 80), _head_tile(gk, 80, 96), *consts)
    o_att = attention(Qn, Q1, Q2, Kn, K1, K2, vv_att, nm + '_att')
    o_a = linear(o_att, wd['mla_w_o'], name=nm + '_wo')

    bm = _np_bf16(_block_ones(RW_WIDTH, RW_N))
    vec = lambda n: sp[n].reshape(1, -1)
    f32w = lambda n: wd[n].c + wd[n].b.astype(F32)
    w2p = _pad_rows(f32w('rwkv_w2'), 0, 128)
    a2p = _pad_rows(f32w('rwkv_a2'), 64, 128)
    g2 = f32w('rwkv_g2')
    rw_pieces = ((0, 256), (256, 256), (512, 256), (768, 128), (896, 128), (1024, 128))
    if vres:
        v2p = _pad_rows(f32w('rwkv_v2'), 32, 128)
        prep_specs = [ROW(pieces=rw_pieces), ROW()] + [FULL()] * 9 + [FULL(False)]
        prep_in = [rws, v_first, vec('rwkv_w0'), vec('rwkv_a0'), vec('rwkv_k_k'), vec('rwkv_k_a'), w2p, a2p, g2,
                   vec('rwkv_v0'), v2p, bm]
    else:
        prep_specs = [ROW(pieces=rw_pieces)] + [FULL()] * 7 + [FULL(False)]
        prep_in = [rws, vec('rwkv_w0'), vec('rwkv_a0'), vec('rwkv_k_k'), vec('rwkv_k_a'), w2p, a2p, g2, bm]
    prep_op = stage_op(_rwkv_prep_fn(vres), prep_specs, [256] * 7, nm + '_rwprep')
    r_, dec, k2, vv, an, bn, g = prep_op(*prep_in)
    if not vres:
        v_first = vv
    t3 = lambda t: t.reshape(T, 1, RW_WIDTH)
    y = wkv7(t3(r_), t3(dec), t3(k2), t3(vv), t3(an), t3(bn), nm + '_scan').reshape(T, RW_WIDTH)
    post_op = stage_op(_rwkv_post_fn, [ROW()] * 5 + [FULL()] * 3 + [FULL(False)], [256], nm + '_rwpost')
    yb = post_op(y, r_, k2, vv, g, vec('rwkv_ln_w'), vec('rwkv_ln_b'), sp['rwkv_r_k'].reshape(1, -1), bm)[0]
    o_b = linear(yb, wd['rwkv_w_o'], name=nm + '_rwo')

    oc_in = short_conv(cv, f32w('conv_w'), nm + '_conv')
    o_c = linear(oc_in, wd['conv_w_o'], name=nm + '_cwo')

    merge_op = stage_op(_merge_fn, [ROW(pieces=((0, 1024), (1024, 1024), (2048, 1024))), ROW(), ROW(), ROW()],
                        [D_MODEL], nm + '_merge')
    merged = merge_op(gates, o_a, o_b, o_c)[0]
    x2 = linear(merged, wd['w_out'], add=x_skip, name=nm + '_wout')
    return x2, v_first


def _mlp(l, x, wd, sp):
    return mlp_block(x, sp['mlp_norm'].reshape(1, -1), wd['w_up'].b, wd['w_up'].c, wd['w_down'].b, wd['w_down'].c,
                     f'l{l}_mlp')


MLP_WEIGHTS = ('w_up', 'w_down')


def _entries(ml):
    layer, part = ml if isinstance(ml, tuple) else (ml, 'all')
    out = []
    for name, (layers, shape, axis) in SHARDED.items():
        l = layer if layers == DEPTH else layer - 1
        if not 0 <= l < layers or name in MLP_WEIGHTS:
            continue
        if part != 'all' and (name in IN_WEIGHTS) != (part == 'in'):
            continue
        n = shape[0] * shape[1] // N_DEV
        if name == 'conv_w':
            out.append(('conv_w_hi', name, l, n))
            out.append(('conv_w_lo', name, l, n))
        else:
            out.append((name, name, l, n))
    return out


def _slot_size(n):
    return -(-n // LANES) * LANES


def _pack_rows(ml):
    total = sum(_slot_size(n) for _, _, _, n in _entries(ml))
    rows = -(-total // LANES)
    return -(-rows // PACK_ROW_MULT) * PACK_ROW_MULT


def _pack_flat(pieces, ml):
    rows = _pack_rows(ml)
    padded = []
    for p, (_, _, _, n) in zip(pieces, _entries(ml)):
        pad = _slot_size(n) - n
        if pad:
            p = jnp.pad(p, [(0, 0)] * (p.ndim - 1) + [(0, pad)])
        padded.append(p)
    flat = jnp.concatenate(padded, axis=-1)
    tail = rows * LANES - flat.shape[-1]
    if tail:
        flat = jnp.pad(flat, [(0, 0)] * (flat.ndim - 1) + [(0, tail)])
    return flat.reshape(flat.shape[:-1] + (rows, LANES))


def _unpack_flat(buf, ml):
    out, row = [], 0
    for _, _, _, n in _entries(ml):
        nrows = _slot_size(n) // LANES
        piece = buf[..., row:row + nrows, :].reshape(buf.shape[:-2] + (-1,))
        out.append(piece[..., :n])
        row += nrows
    return out


def _pack_shards(shards, ml, dtype, split_conv):
    pieces = []
    for slot, name, l, n in _entries(ml):
        a = shards[name][l]
        if name in TRANSPOSED:
            a = a.astype(dtype).T
        a = a.reshape(-1)
        if slot == 'conv_w_hi':
            a = a.astype(BF16).astype(F32) if split_conv else a
        elif slot == 'conv_w_lo':
            a = (a - a.astype(BF16).astype(F32)) if split_conv else jnp.zeros_like(a)
        pieces.append(a.astype(dtype))
    return _pack_flat(pieces, ml)


def _unpack_shards(buf, ml):
    out = {}
    for (slot, name, l, n), v in zip(_entries(ml), _unpack_flat(buf, ml)):
        if slot == 'conv_w_lo':
            continue
        layers, shape, axis = SHARDED[name]
        sshape = (shape[0] // N_DEV, shape[1]) if axis == 0 else (shape[0], shape[1] // N_DEV)
        out[(name, l)] = v.reshape(sshape[::-1]).T if name in TRANSPOSED else v.reshape(sshape)
    return out


def _to_full(blocks, shape, axis):
    if axis == 0:
        return blocks.reshape(shape)
    return blocks.reshape(N_DEV, shape[0], shape[1] // N_DEV).transpose(1, 0, 2).reshape(shape)


def _to_blocks(full, axis):
    r, c = full.shape
    if axis == 0:
        return full.reshape(N_DEV, -1)
    return full.reshape(r, N_DEV, c // N_DEV).transpose(1, 0, 2).reshape(N_DEV, -1)


def _unpack_gathered(gathered, ml):
    out, conv_hi = {}, None
    for (slot, name, l, n), v in zip(_entries(ml), _unpack_flat(gathered, ml)):
        layers, shape, axis = SHARDED[name]
        if name in TRANSPOSED:
            out[name] = v.reshape(-1, shape[0])
            continue
        full = _to_full(v, shape, axis)
        if slot == 'conv_w_hi':
            conv_hi = full
        elif slot == 'conv_w_lo':
            out[name] = conv_hi.astype(F32) + full.astype(F32)
        else:
            out[name] = full
    return out


def _pack_grads(grads, ml):
    pieces = []
    for slot, name, l, n in _entries(ml):
        if name in TRANSPOSED:
            blocks = jnp.concatenate(grads[name], axis=0).reshape(N_DEV, -1)
        else:
            blocks = _to_blocks(grads[name], SHARDED[name][2])
        if slot == 'conv_w_lo':
            blocks = jnp.zeros_like(blocks)
        pieces.append(blocks.astype(BF16))
    return _pack_flat(pieces, ml)


def _my_pos():
    return lax.axis_index('x'), lax.axis_index('y'), lax.axis_index('c')


def _flip(v, bit):
    return 1 - v if bit else v


def all_gather_blocks(x):
    rows = x.shape[0]

    def body(x_ref, out_ref, send_sems, recv_sems, local_sem):
        mx, my, mc = _my_pos()
        me, sibling = (mx, my, mc), (mx, my, 1 - mc)
        chips = [(1 - mx, my), (mx, 1 - my), (1 - mx, 1 - my)]

        def block(px, py, pc):
            return out_ref.at[4 * px + 2 * py + pc]

        def copy(k, blk, to, src=None):
            return pltpu.make_async_remote_copy(
                src_ref=block(*blk) if src is None else src, dst_ref=block(*blk),
                send_sem=send_sems.at[k], recv_sem=recv_sems.at[k], device_id=to, device_id_type=MESH)

        mine = pltpu.make_async_copy(x_ref, block(*me), local_sem)
        mine.start()
        first = [copy(0, me, sibling, src=x_ref)]
        first += [copy(1 + j, me, (*chip, mc), src=x_ref) for j, chip in enumerate(chips)]
        for cp in first:
            cp.start()
        passed = [copy(4 + j, (*chip, mc), sibling) for j, chip in enumerate(chips)]
        for j, chip in enumerate(chips):
            copy(1 + j, (*chip, mc), me).wait_recv()
            passed[j].start()
        copy(0, sibling, me).wait_recv()
        for j, chip in enumerate(chips):
            copy(4 + j, (*chip, 1 - mc), me).wait_recv()
        for cp in first + passed:
            cp.wait_send()
        mine.wait()

    return pl.pallas_call(
        body, name='all_gather_weights',
        out_shape=jax.ShapeDtypeStruct((N_DEV, rows, LANES), x.dtype),
        in_specs=[pl.BlockSpec(memory_space=pl.ANY)], out_specs=pl.BlockSpec(memory_space=pl.ANY),
        scratch_shapes=[pltpu.SemaphoreType.DMA((7,)), pltpu.SemaphoreType.DMA((7,)), pltpu.SemaphoreType.DMA],
    )(x)


HBM_SPEC = pl.BlockSpec(memory_space=pltpu.HBM)
SEM_SPEC = pl.BlockSpec(memory_space=pltpu.SEMAPHORE)
DATAFLOW_EFFECT = pltpu.SideEffectType.DATAFLOW_SIDE_EFFECTING


def _direct_copies(src_ref, land_ref, send_sems, recv_sems, per_peer):
    mx, my, mc = _my_pos()
    me = 4 * mx + 2 * my + mc
    copies = []
    for k in range(1, N_DEV):
        peer = (_flip(mx, k & 4), _flip(my, k & 2), _flip(mc, k & 1))
        pidx = 4 * peer[0] + 2 * peer[1] + peer[2]
        copies.append(pltpu.make_async_remote_copy(
            src_ref=src_ref.at[pidx] if per_peer else src_ref, dst_ref=land_ref.at[me],
            send_sem=send_sems.at[k - 1], recv_sem=recv_sems.at[k - 1], device_id=peer, device_id_type=MESH))
    return copies


def send_start(src, per_peer, name):
    block = src.shape[1:] if per_peer else src.shape
    land_shape = (N_DEV,) + tuple(block)

    def body(src_ref, land_ref, send_sems, recv_sems, src_thru, land_thru, token):
        for cp in _direct_copies(src_ref, land_ref, send_sems, recv_sems, per_peer):
            cp.start()
        token[...] = jnp.zeros(token.shape, F32)

    send_sems, recv_sems, src_thru, land_thru, token = pl.pallas_call(
        body, name=name,
        out_shape=(pltpu.SemaphoreType.DMA((N_DEV - 1,)), pltpu.SemaphoreType.DMA((N_DEV - 1,)),
                   pltpu.HBM(src.shape, src.dtype), pltpu.HBM(land_shape, src.dtype),
                   jax.ShapeDtypeStruct((8, LANES), F32)),
        in_specs=(HBM_SPEC, HBM_SPEC),
        out_specs=(SEM_SPEC, SEM_SPEC, HBM_SPEC, HBM_SPEC, pl.BlockSpec(memory_space=pltpu.VMEM)),
        input_output_aliases={0: 2, 1: 3},
        compiler_params=pltpu.CompilerParams(has_side_effects=DATAFLOW_EFFECT),
    )(pltpu.with_memory_space_constraint(src, pltpu.HBM),
      pltpu.with_memory_space_constraint(lax.empty(land_shape, src.dtype), pltpu.HBM))
    return (send_sems, recv_sems, src_thru, land_thru), token[0, 0]


def send_wait(handles, after, per_peer, name):
    send_sems, recv_sems, src_thru, land_thru = handles

    def body(src_ref, land_ref, send_sems, recv_sems, after_ref, src_dead, got_ref):
        for cp in _direct_copies(src_ref, land_ref, send_sems, recv_sems, per_peer):
            cp.wait_send()
            cp.wait_recv()

    return pl.pallas_call(
        body, name=name,
        out_shape=(pltpu.HBM(src_thru.shape, src_thru.dtype), pltpu.HBM(land_thru.shape, land_thru.dtype)),
        in_specs=(HBM_SPEC, HBM_SPEC, SEM_SPEC, SEM_SPEC, pl.BlockSpec(memory_space=pl.ANY)),
        out_specs=(HBM_SPEC, HBM_SPEC), input_output_aliases={0: 0, 1: 1},
        compiler_params=pltpu.CompilerParams(has_side_effects=DATAFLOW_EFFECT),
    )(src_thru, land_thru, send_sems, recv_sems, after)[1]


def _adamw_math(w, g, m, v):
    m2 = ADAM_B1 * m + (1.0 - ADAM_B1) * g
    v2 = ADAM_B2 * v + (1.0 - ADAM_B2) * (g * g)
    m_hat = m2 / (1.0 - ADAM_B1 ** ADAM_STEP)
    v_hat = v2 / (1.0 - ADAM_B2 ** ADAM_STEP)
    delta = -ADAM_LR * (m_hat / (jnp.sqrt(v_hat) + ADAM_EPS) + ADAM_WD * w)
    return delta, m2, v2


def sum_parts(parts, mine, name):
    _, rows, cols = parts.shape
    rb = rows
    while N_DEV * rb * cols * 2 > (2 << 20) and rb % 32 == 0:
        rb //= 2

    def body(p_ref, mine_ref, g_ref):
        mx, my, mc = _my_pos()
        me = 4 * mx + 2 * my + mc
        own = mine_ref[0].astype(F32)
        g = jnp.where(me == 0, own, p_ref[0].astype(F32))
        for j in range(1, N_DEV):
            g = g + jnp.where(me == j, own, p_ref[j].astype(F32))
        g_ref[...] = g

    return pl.pallas_call(
        body, name=name, grid=(rows // rb,),
        in_specs=[pl.BlockSpec((N_DEV, rb, cols), lambda i: (0, i, 0)), pl.BlockSpec((1, rb, cols), lambda i: (0, i, 0))],
        out_specs=pl.BlockSpec((rb, cols), lambda i: (i, 0)),
        out_shape=jax.ShapeDtypeStruct((rows, cols), F32), compiler_params=_cparams(('parallel',)),
    )(parts, mine)


ADAMW_BLOCK_BYTES = 1 << 20


def adamw_weight(name, w, m, v, grads):
    shape = w.shape
    if shape[2] % LANES and (shape[1] * shape[2]) % LANES == 0 and shape[2] < 4 * LANES:
        flat = lambda t: t.reshape(t.shape[:-2] + (-1, LANES))
        outs = adamw_weight(name, flat(w), flat(m), flat(v), [flat(g) for g in grads])
        return [o.reshape(shape) for o in outs]
    layers, a, b = w.shape
    ra = a
    while ra * b * 4 > ADAMW_BLOCK_BYTES and ra % 16 == 0:
        ra //= 2

    def body(*refs):
        w_ref, m_ref, v_ref = refs[:3]
        g_refs = refs[3:3 + layers]
        g_ref, d_ref, m2_ref, v2_ref = refs[3 + layers:]
        g = g_refs[0][...]
        for l in range(1, layers):
            g = jnp.where(pl.program_id(0) == l, g_refs[l][...], g)
        delta, m2, v2 = _adamw_math(w_ref[0], g, m_ref[0], v_ref[0])
        g_ref[0] = g
        d_ref[0] = delta
        m2_ref[0] = m2
        v2_ref[0] = v2

    blk = pl.BlockSpec((1, ra, b), lambda l, i: (l, i, 0))
    gblk = pl.BlockSpec((ra, b), lambda l, i: (i, 0))
    sh = jax.ShapeDtypeStruct(w.shape, F32)
    return pl.pallas_call(
        body, name='adamw_' + name, grid=(layers, a // ra), in_specs=[blk] * 3 + [gblk] * layers,
        out_specs=[blk] * 4, out_shape=[sh] * 4, compiler_params=_cparams(('parallel', 'parallel')),
    )(w, m, v, *grads)


def allreduce_adamw_small(g, w, m, v):
    rows = g.shape[0]

    def body(g_ref, w_ref, m_ref, v_ref, gs_ref, d_ref, m2_ref, v2_ref, all_ref, send_sems, recv_sems):
        mx, my, mc = _my_pos()
        me, sibling = (mx, my, mc), (mx, my, 1 - mc)
        chips = [(1 - mx, my), (mx, 1 - my), (1 - mx, 1 - my)]

        def block(px, py, pc):
            return all_ref.at[4 * px + 2 * py + pc]

        def copy(k, blk, to, src=None):
            return pltpu.make_async_remote_copy(
                src_ref=block(*blk) if src is None else src, dst_ref=block(*blk),
                send_sem=send_sems.at[k], recv_sem=recv_sems.at[k], device_id=to, device_id_type=MESH)

        first = [copy(0, me, sibling, src=g_ref)]
        first += [copy(1 + j, me, (*chip, mc), src=g_ref) for j, chip in enumerate(chips)]
        for cp in first:
            cp.start()
        passed = [copy(4 + j, (*chip, mc), sibling) for j, chip in enumerate(chips)]
        for j, chip in enumerate(chips):
            copy(1 + j, (*chip, mc), me).wait_recv()
            passed[j].start()
        copy(0, sibling, me).wait_recv()
        for j, chip in enumerate(chips):
            copy(4 + j, (*chip, 1 - mc), me).wait_recv()
        for cp in first + passed:
            cp.wait_send()
        my_idx = 4 * mx + 2 * my + mc
        total = jnp.zeros((rows, LANES), F32)
        for j in range(N_DEV):
            total = total + jnp.where(my_idx == j, g_ref[...], all_ref[j])
        delta, m2, v2 = _adamw_math(w_ref[...], total, m_ref[...], v_ref[...])
        gs_ref[...] = total
        d_ref[...] = delta
        m2_ref[...] = m2
        v2_ref[...] = v2

    vm = pl.BlockSpec(memory_space=pltpu.VMEM)
    sh = jax.ShapeDtypeStruct((rows, LANES), F32)
    return pl.pallas_call(
        body, name='allreduce_adamw_small', in_specs=[vm] * 4, out_specs=[vm] * 4, out_shape=[sh] * 4,
        scratch_shapes=[pltpu.VMEM((N_DEV, rows, LANES), F32), pltpu.SemaphoreType.DMA((7,)),
                        pltpu.SemaphoreType.DMA((7,))],
    )(g, w, m, v)


SMALL_COUNT = 11680


def _small_pack(d, extra=None):
    assert sum(d[n].size for n in SMALL_NAMES) == SMALL_COUNT
    flat = jnp.concatenate([d[n].reshape(-1) for n in SMALL_NAMES] + ([] if extra is None else [extra.reshape(1)]))
    rows = -(-(SMALL_COUNT + 1) // (8 * LANES)) * 8
    return jnp.pad(flat, (0, rows * LANES - flat.shape[0])).reshape(rows, LANES)


def _small_unpack(buf, like):
    flat = buf.reshape(-1)
    out, off = {}, 0
    for n in SMALL_NAMES:
        sz = int(np.prod(like[n].shape))
        out[n] = flat[off:off + sz].reshape(like[n].shape)
        off += sz
    return out


def _rope_tables(positions):
    freqs = ROPE_THETA ** (-(jnp.arange(QK_ROPE // 2, dtype=F32) * 2.0 / QK_ROPE))
    ang = positions.astype(F32)[:, None] * freqs
    return jnp.tile(jnp.cos(ang), (1, MLA_HEADS)), jnp.tile(jnp.sin(ang), (1, MLA_HEADS))


def _layer_weights(gathered, carriers):
    wd = {}
    for name, full in gathered.items():
        if name == 'w_in':
            wd[name] = {seg: W(full[lo:hi], c) for (seg, lo, hi), c in zip(WIN_SEGS, carriers[name])}
        else:
            wd[name] = W(full, carriers[name])
    return wd


F32_GRAD_WEIGHTS = ('rwkv_w2', 'rwkv_a2', 'rwkv_g2', 'rwkv_v2', 'conv_w')


def _make_carriers(gathered):
    gathered, carriers = dict(gathered), {}
    for name, full in gathered.items():
        if name == 'w_in':
            carriers[name] = tuple(jnp.zeros((hi - lo, D_MODEL), BF16) for _, lo, hi in WIN_SEGS)
        elif name == 'conv_w':
            gathered[name] = full.astype(BF16)
            carriers[name] = full - full.astype(BF16).astype(F32)
        else:
            carriers[name] = jnp.zeros(full.shape, F32 if name in F32_GRAD_WEIGHTS else BF16)
    return gathered, carriers


def _layer_small(small, ml):
    out = {}
    for n in SMALL_NAMES:
        l = ml if small[n].shape[0] == DEPTH else ml - 1
        if 0 <= l < small[n].shape[0]:
            out[n] = small[n][l]
    return out


def _after(value, tokens):
    return value + sum(tokens[1:], tokens[0])


def _train_step(x, positions, loss_target, weights, moms_m, moms_v):
    shards = {n: weights[n] for n in SHARDED}
    small = {n: weights[n] for n in SMALL_NAMES}
    me = 4 * lax.axis_index('x') + 2 * lax.axis_index('y') + lax.axis_index('c')
    cos, sin = _rope_tables(positions[0])

    units = [('pack', (0, 'in')), ('pack', (0, 'rest')), ('w_up', 0), ('w_down', 0),
             ('pack', 1), ('w_up', 1), ('w_down', 1)]
    first, later = units[0], units[1:]

    def tag(u):
        kind, key = u
        return f'{kind}_l{key[0]}{key[1]}' if isinstance(key, tuple) else f'{kind}_l{key}'

    def own_block(u):
        kind, key = u
        return _pack_shards(shards, key, BF16, True) if kind == 'pack' else shards[kind][key].astype(BF16)

    own = {u: own_block(u) for u in units}
    held = lax.optimization_barrier((all_gather_blocks(own[first]), *[own[u] for u in later]))
    blocks, gathers = {first: held[0]}, {}
    for u, mine in zip(later, held[1:]):
        own[u] = mine

    def start_gather(u):
        gathers[u], token = send_start(own[u], False, f'gather_{tag(u)}_start')
        return token

    first_token = start_gather(later[0])

    def gathered_block(u, after):
        if u not in blocks:
            landed = send_wait(gathers[u], after, False, f'gather_{tag(u)}_wait')
            blocks[u] = lax.dynamic_update_slice(landed, own[u][None], (me,) + (0,) * own[u].ndim)
        return blocks[u]

    def packed_weights(key, after):
        return _make_carriers(_unpack_gathered(gathered_block(('pack', key), after), key))

    def mlp_weights(l, after):
        up = gathered_block(('w_up', l), after)
        down = gathered_block(('w_down', l), after).reshape(D_FF, D_MODEL)
        return _make_carriers({'w_up': up, 'w_down': down})

    def small_of(layer, part):
        owner = lambda n: 'mlp' if n == 'mlp_norm' else ('in' if n == 'attn_norm' else 'rest')
        parts = ('in', 'rest') if part == 'mix' else (part,)
        return {n: v for n, v in _layer_small(small, layer).items() if owner(n) in parts}

    def mlp(layer, gathered):
        return lambda c, s, xx: _mlp(layer, xx, _layer_weights(gathered, c), s)

    gathered, carriers = packed_weights((0, 'in'), None)
    (proj, x_skip), vjp_in0 = jax.vjp(lambda c, s, xx: _layer_in(0, xx, _layer_weights(gathered, c), s),
                                      carriers, {n: _after(v, [first_token]) for n, v in small_of(0, 'in').items()},
                                      x[0])
    gathered, carriers = packed_weights((0, 'rest'), proj[2])
    landed = (blocks[later[0]][0, 0, 0] * 0).astype(F32)
    tokens = []
    for u in later[1:]:
        own[u] = own[u] + landed.astype(own[u].dtype)
        tokens.append(start_gather(u))
    proj = (proj[0], proj[1], _after(proj[2], tokens), proj[3], proj[4])
    (h, v_first), vjp_rest0 = jax.vjp(
        lambda c, s, p, xs, w=gathered: _layer_rest(0, p, xs, None, _layer_weights(w, c), s, cos, sin),
        carriers, small_of(0, 'rest'), proj, x_skip)
    gathered, carriers = mlp_weights(0, h)
    h, vjp_mlp0 = jax.vjp(mlp(0, gathered), carriers, small_of(0, 'mlp'), h)
    gathered, carriers = packed_weights(1, h)
    h, vjp_mix1 = jax.vjp(
        lambda c, s, xx, vf, w=gathered: _layer(1, xx, vf, _layer_weights(w, c), s, cos, sin)[0],
        carriers, small_of(1, 'mix'), h, v_first)
    gathered, carriers = mlp_weights(1, h)
    y, vjp_mlp1 = jax.vjp(mlp(1, gathered), carriers, small_of(1, 'mlp'), h)

    dy, loss_parts = _loss_call(y, loss_target[0])

    shipped = {}

    def ship(u, to_send):
        handles, token = send_start(to_send, True, f'grads_{tag(u)}_start')
        shipped[u] = (handles, to_send)
        return token

    def ship_mlp(l, gw):
        down = gw['w_down'].reshape(N_DEV, D_FF // N_DEV, D_MODEL)
        return [ship(('w_up', l), gw['w_up']), ship(('w_down', l), down)]

    gw, gs_mlp1, d = vjp_mlp1(dy)
    d = _after(d, ship_mlp(1, gw))
    gw, gs_mix1, d, dvf = vjp_mix1(d)
    d = _after(d, [ship(('pack', 1), _pack_grads(gw, 1))])
    gw, gs_mlp0, d = vjp_mlp0(d)
    d = _after(d, ship_mlp(0, gw))
    gw, gs_rest0, dproj, dx_skip = vjp_rest0((d, dvf))
    token = ship(('pack', (0, 'rest')), _pack_grads(gw, (0, 'rest')))
    dproj = (dproj[0], dproj[1], _after(dproj[2], [token]), dproj[3], dproj[4])
    gw, gs_in0, g_x = vjp_in0((dproj, dx_skip))
    last_token = ship(first, _pack_grads(gw, (0, 'in')))
    gs0, gs1 = {**gs_in0, **gs_rest0, **gs_mlp0}, {**gs_mix1, **gs_mlp1}

    def arrived(u, after):
        handles, sent = shipped[u]
        landed = send_wait(handles, after, True, f'grads_{tag(u)}_wait')
        return sum_parts(landed, lax.dynamic_slice_in_dim(sent, me, 1, axis=0), f'sum_grads_{tag(u)}')

    after = _after(g_x, [last_token])
    grads = {}
    for u in later:
        if u[0] == 'pack':
            grads.update(_unpack_shards(arrived(u, after), u[1]))
        else:
            grads[u] = arrived(u, after)
    sharded_out = [{}, {}, {}, {}]

    def update(n):
        outs = adamw_weight(n, weights[n], moms_m[n], moms_v[n], [grads[(n, l)] for l in range(SHARDED[n][0])])
        for i in range(4):
            sharded_out[i][n] = outs[i]

    last_weight = 'w_in'
    for n in SHARDED:
        if n != last_weight:
            update(n)

    g_small = {}
    for n in SMALL_NAMES:
        per = [g[n] for g in (gs0, gs1) if n in g]
        g_small[n] = jnp.stack(per)
    small_grads = _after(_small_pack(g_small, jnp.sum(loss_parts)), [sharded_out[1][MLP_WEIGHTS[-1]][0, 0, 0] * 0.0])
    small_bufs = allreduce_adamw_small(
        small_grads, _small_pack(small), _small_pack({n: moms_m[n] for n in SMALL_NAMES}),
        _small_pack({n: moms_v[n] for n in SMALL_NAMES}))
    loss = small_bufs[0].reshape(-1)[SMALL_COUNT]
    small_out = [_small_unpack(b, small) for b in small_bufs]

    grads.update(_unpack_shards(arrived(first, small_bufs[0]), first[1]))
    update(last_weight)

    pick = lambda i: [sharded_out[i][n] if n in SHARDED else small_out[i][n] for n in WEIGHT_NAMES]
    return (loss, g_x[None], *pick(0), *pick(1), *pick(2), *pick(3))


def kernel(x, positions, attn_norm, w_in, mla_q_a_norm, mla_wq_b, mla_kv_a_norm, mla_wkv_b, mla_q_norm, mla_k_norm, mla_w_o, rwkv_mu, rwkv_w0, rwkv_w2, rwkv_a0, rwkv_a2, rwkv_g2, rwkv_k_k, rwkv_k_a, rwkv_r_k, rwkv_ln_w, rwkv_ln_b, rwkv_w_o, rwkv_v1, rwkv_v_mu, rwkv_v0, rwkv_v2, conv_w, conv_w_o, w_out, mlp_norm, w_up, w_down, loss_target, m_attn_norm, m_w_in, m_mla_q_a_norm, m_mla_wq_b, m_mla_kv_a_norm, m_mla_wkv_b, m_mla_q_norm, m_mla_k_norm, m_mla_w_o, m_rwkv_mu, m_rwkv_w0, m_rwkv_w2, m_rwkv_a0, m_rwkv_a2, m_rwkv_g2, m_rwkv_k_k, m_rwkv_k_a, m_rwkv_r_k, m_rwkv_ln_w, m_rwkv_ln_b, m_rwkv_w_o, m_rwkv_v1, m_rwkv_v_mu, m_rwkv_v0, m_rwkv_v2, m_conv_w, m_conv_w_o, m_w_out, m_mlp_norm, m_w_up, m_w_down, v_attn_norm, v_w_in, v_mla_q_a_norm, v_mla_wq_b, v_mla_kv_a_norm, v_mla_wkv_b, v_mla_q_norm, v_mla_k_norm, v_mla_w_o, v_rwkv_mu, v_rwkv_w0, v_rwkv_w2, v_rwkv_a0, v_rwkv_a2, v_rwkv_g2, v_rwkv_k_k, v_rwkv_k_a, v_rwkv_r_k, v_rwkv_ln_w, v_rwkv_ln_b, v_rwkv_w_o, v_rwkv_v1, v_rwkv_v_mu, v_rwkv_v0, v_rwkv_v2, v_conv_w, v_conv_w_o, v_w_out, v_mlp_norm, v_w_up, v_w_down):
    args = locals()
    weights = {n: args[n] for n in WEIGHT_NAMES}
    moms_m = {n: args['m_' + n] for n in WEIGHT_NAMES}
    moms_v = {n: args['v_' + n] for n in WEIGHT_NAMES}
    return _train_step(x, positions, loss_target, weights, moms_m, moms_v)
```

```python
import functools

import numpy as np
import jax
import jax.numpy as jnp
from jax import lax
from jax.experimental import pallas as pl
from jax.experimental.pallas import tpu as pltpu

F32 = jnp.float32
BF16 = jnp.bfloat16

N_DEV = 8
LANES = 128
D_MODEL = 1024
DEPTH = 2
MLA_HEADS = 8
QK_NOPE = 64
QK_ROPE = 32
QK_HEAD = QK_NOPE + QK_ROPE
V_HEAD = 64
Q_LORA = 384
KV_LORA = 256
ROPE_THETA = 10000.0
RW_HEADS = 4
RW_N = 64
RW_WIDTH = RW_HEADS * RW_N
MV_LORA = 32
GN_EPS = 64e-5
CONV_WIDTH = 256
D_FF = 4 * D_MODEL
NORM_EPS = 1e-6
ADAM_LR = 0.001
ADAM_B1 = 0.9
ADAM_B2 = 0.999
ADAM_EPS = 1e-08
ADAM_WD = 0.01
ADAM_STEP = 10

VMEM_LIMIT = 56 * 1024 * 1024
MESH = pl.DeviceIdType.MESH

WEIGHT_NAMES = ['attn_norm', 'w_in', 'mla_q_a_norm', 'mla_wq_b', 'mla_kv_a_norm', 'mla_wkv_b', 'mla_q_norm',
                'mla_k_norm', 'mla_w_o', 'rwkv_mu', 'rwkv_w0', 'rwkv_w2', 'rwkv_a0', 'rwkv_a2', 'rwkv_g2',
                'rwkv_k_k', 'rwkv_k_a', 'rwkv_r_k', 'rwkv_ln_w', 'rwkv_ln_b', 'rwkv_w_o', 'rwkv_v1',
                'rwkv_v_mu', 'rwkv_v0', 'rwkv_v2', 'conv_w', 'conv_w_o', 'w_out', 'mlp_norm', 'w_up', 'w_down']

SHARDED = {
    'w_in': (2, (1024, 5536), 1), 'mla_wq_b': (2, (384, 768), 1), 'mla_wkv_b': (2, (256, 1024), 1),
    'mla_w_o': (2, (512, 1024), 1), 'rwkv_w2': (2, (64, 256), 1), 'rwkv_a2': (2, (64, 256), 1),
    'rwkv_g2': (2, (128, 256), 1), 'rwkv_w_o': (2, (256, 1024), 1), 'conv_w': (2, (3, 256), 1),
    'conv_w_o': (2, (256, 1024), 1), 'w_out': (2, (1024, 1024), 0), 'w_up': (2, (1024, 4096), 1),
    'w_down': (2, (4096, 1024), 0), 'rwkv_v1': (1, (1024, 32), 0), 'rwkv_v2': (1, (32, 256), 1),
}
SMALL_NAMES = [n for n in WEIGHT_NAMES if n not in SHARDED]
TRANSPOSED = ('w_in',)
WIN_SEGS = (('gates', 0, 3072), ('cq', 3072, 3456), ('ckv', 3456, 3712), ('kpe', 3712, 3744), ('rkv', 3744, 4512),
            ('xwa', 4512, 4640), ('xg', 4640, 4768), ('conv', 4768, 5536))
PACK_ROW_MULT = 512


def _cparams(sem=None, **kw):
    if sem is not None:
        kw['dimension_semantics'] = sem
    return pltpu.CompilerParams(vmem_limit_bytes=VMEM_LIMIT, **kw)


def _pick(n, cands):
    for c in cands:
        if n % c == 0:
            return c
    raise ValueError(f'no tile for {n}')


def _mm_nn(a, b, add=None, name='mm_nn'):
    M, K = a.shape
    N = b.shape[1]
    tm = _pick(M, (1024, 512, 256, 128))
    tn = _pick(N, (512, 384, 256, 128))
    tk = _pick(K, (1024, 512, 384, 256, 128))
    nk = K // tk
    has_add = add is not None

    def body(*refs):
        if has_add:
            a_ref, b_ref, add_ref, o_ref, acc_ref = refs
        else:
            a_ref, b_ref, o_ref, acc_ref = refs
        kk = pl.program_id(2)
        part = jnp.dot(a_ref[...].astype(BF16), b_ref[...].astype(BF16), preferred_element_type=F32)

        @pl.when(kk == 0)
        def _():
            acc_ref[...] = part

        @pl.when(kk > 0)
        def _():
            acc_ref[...] += part

        @pl.when(kk == nk - 1)
        def _():
            if has_add:
                o_ref[...] = acc_ref[...] + add_ref[...]
            else:
                o_ref[...] = acc_ref[...]

    in_specs = [pl.BlockSpec((tm, tk), lambda i, j, k: (i, k)), pl.BlockSpec((tk, tn), lambda i, j, k: (k, j))]
    args = [a, b]
    if has_add:
        in_specs.append(pl.BlockSpec((tm, tn), lambda i, j, k: (i, j)))
        args.append(add)
    return pl.pallas_call(
        body, name=name, grid=(M // tm, N // tn, nk), in_specs=in_specs,
        out_specs=pl.BlockSpec((tm, tn), lambda i, j, k: (i, j)),
        out_shape=jax.ShapeDtypeStruct((M, N), F32),
        scratch_shapes=[pltpu.VMEM((tm, tn), F32)],
        compiler_params=_cparams(('parallel', 'parallel', 'arbitrary')),
    )(*args)


def _mm_nt(a, b, add=None, name='mm_nt'):
    M, N = a.shape
    blocked = b.ndim == 3
    K = b.shape[-2]
    tm = _pick(M, (1024, 512, 256, 128))
    tk = _pick(K, (512, 384, 256, 128))
    tn = N // N_DEV if blocked else _pick(N, (1024, 512, 384, 256, 128))
    nn = N // tn
    has_add = add is not None

    def body(*refs):
        if has_add:
            a_ref, b_ref, add_ref, o_ref, acc_ref = refs
        else:
            a_ref, b_ref, o_ref, acc_ref = refs
        kk = pl.program_id(2)
        part = lax.dot_general(a_ref[...].astype(BF16), b_ref[...].astype(BF16), (((1,), (1,)), ((), ())),
                               preferred_element_type=F32)

        @pl.when(kk == 0)
        def _():
            acc_ref[...] = part

        @pl.when(kk > 0)
        def _():
            acc_ref[...] += part

        @pl.when(kk == nn - 1)
        def _():
            if has_add:
                o_ref[...] = acc_ref[...] + add_ref[...]
            else:
                o_ref[...] = acc_ref[...]

    if blocked:
        b_spec = pl.BlockSpec((None, tk, tn), lambda i, j, k: (k, j, 0))
    else:
        b_spec = pl.BlockSpec((tk, tn), lambda i, j, k: (j, k))
    in_specs = [pl.BlockSpec((tm, tn), lambda i, j, k: (i, k)), b_spec]
    args = [a, b]
    if has_add:
        in_specs.append(pl.BlockSpec((tm, tk), lambda i, j, k: (i, j)))
        args.append(add)
    return pl.pallas_call(
        body, name=name, grid=(M // tm, K // tk, nn), in_specs=in_specs,
        out_specs=pl.BlockSpec((tm, tk), lambda i, j, k: (i, j)),
        out_shape=jax.ShapeDtypeStruct((M, K), F32),
        scratch_shapes=[pltpu.VMEM((tm, tk), F32)],
        compiler_params=_cparams(('parallel', 'parallel', 'arbitrary')),
    )(*args)


def _mm_tn(a, b, name='mm_tn', blocked=False):
    M, K = a.shape
    N = b.shape[1]
    tm = _pick(M, (1024, 512, 256, 128))
    tk = _pick(K, (512, 384, 256, 128))
    tn = N // N_DEV if blocked else _pick(N, (512, 384, 256, 128))
    nm = M // tm

    def body(a_ref, b_ref, o_ref, acc_ref):
        mm = pl.program_id(2)
        part = lax.dot_general(a_ref[...].astype(BF16), b_ref[...].astype(BF16), (((0,), (0,)), ((), ())),
                               preferred_element_type=F32)

        @pl.when(mm == 0)
        def _():
            acc_ref[...] = part

        @pl.when(mm > 0)
        def _():
            acc_ref[...] += part

        @pl.when(mm == nm - 1)
        def _():
            o_ref[...] = acc_ref[...].astype(BF16)

    if blocked:
        out_spec = pl.BlockSpec((None, tk, tn), lambda i, j, m: (j, i, 0))
        out_shape = jax.ShapeDtypeStruct((N_DEV, K, tn), BF16)
    else:
        out_spec = pl.BlockSpec((tk, tn), lambda i, j, m: (i, j))
        out_shape = jax.ShapeDtypeStruct((K, N), BF16)
    return pl.pallas_call(
        body, name=name, grid=(K // tk, N // tn, nm),
        in_specs=[pl.BlockSpec((tm, tk), lambda i, j, m: (m, i)), pl.BlockSpec((tm, tn), lambda i, j, m: (m, j))],
        out_specs=out_spec, out_shape=out_shape,
        scratch_shapes=[pltpu.VMEM((tk, tn), F32)],
        compiler_params=_cparams(('parallel', 'parallel', 'arbitrary')),
    )(a, b)


@functools.partial(jax.custom_vjp, nondiff_argnums=(4,))
def _linear_add(a, wb, wc, add, name):
    return _mm_nn(a, wb, add, name=name + '_f')


def _linear_add_fwd(a, wb, wc, add, name):
    return _mm_nn(a, wb, add, name=name + '_f'), (a, wb)


def _linear_add_bwd(name, res, dy):
    a, wb = res
    return _mm_nt(dy, wb, name=name + '_da'), None, _mm_tn(a, dy, name=name + '_dw'), dy


_linear_add.defvjp(_linear_add_fwd, _linear_add_bwd)


@functools.partial(jax.custom_vjp, nondiff_argnums=(3,))
def _multi_linear(a, wbs, wcs, name):
    return tuple(_mm_nn(a, wb, name=f'{name}_f{i}') for i, wb in enumerate(wbs))


def _multi_linear_fwd(a, wbs, wcs, name):
    return _multi_linear(a, wbs, wcs, name), (a, wbs)


def _multi_linear_bwd(name, res, dys):
    a, wbs = res
    da = None
    for i, (dy, wb) in enumerate(zip(dys, wbs)):
        da = _mm_nt(dy, wb, add=da, name=f'{name}_da{i}')
    dws = tuple(_mm_tn(a, dy, name=f'{name}_dw{i}') for i, dy in enumerate(dys))
    return da, None, dws


_multi_linear.defvjp(_multi_linear_fwd, _multi_linear_bwd)


class W:
    def __init__(self, b, c):
        self.b, self.c = b, c

    def map(self, fn):
        return W(fn(self.b), fn(self.c))


def _wcat(ws, axis):
    return W(jnp.concatenate([w.b for w in ws], axis), jnp.concatenate([w.c for w in ws], axis))


def linear(a, w, add=None, name='lin'):
    if add is None:
        return _multi_linear(a, (w.b,), (w.c,), name)[0]
    return _linear_add(a, w.b, w.c, add, name)


def ROW(diff=True, pieces=None):
    return ('row', diff, pieces)


def FULL(diff=True):
    return ('full', diff, None)


def _load_args(refs, specs):
    args, amap = [], []
    for i, (ref, (kind, diff, pieces)) in enumerate(zip(refs, specs)):
        if pieces is None:
            args.append(ref[...])
            amap.append((i, None))
        else:
            for (s, w) in pieces:
                args.append(ref[:, s:s + w])
                amap.append((i, (s, w)))
    return args, amap


def _stage_in_specs(ins, specs, tb):
    out = []
    for a, (kind, _, _) in zip(ins, specs):
        if kind == 'row':
            out.append(pl.BlockSpec((tb, a.shape[1]), lambda i: (i, 0)))
        else:
            out.append(pl.BlockSpec(a.shape, lambda i: (0, 0)))
    return out


def _stage_fwd(fn, ins, specs, out_widths, name, tb, out_dtype=F32):
    T = [a for a, s in zip(ins, specs) if s[0] == 'row'][0].shape[0]
    tb = min(tb, T)
    n_in = len(ins)

    def body(*refs):
        args, _ = _load_args(refs[:n_in], specs)
        outs = fn(*args)
        for o_ref, o in zip(refs[n_in:], outs):
            o_ref[...] = o.astype(out_dtype)

    return pl.pallas_call(
        body, name=name + '_f', grid=(T // tb,), in_specs=_stage_in_specs(ins, specs, tb),
        out_specs=[pl.BlockSpec((tb, w), lambda i: (i, 0)) for w in out_widths],
        out_shape=[jax.ShapeDtypeStruct((T, w), out_dtype) for w in out_widths],
        compiler_params=_cparams(('parallel',)),
    )(*ins)


def _stage_bwd(fn, ins, specs, out_widths, douts, name, tb, add_to_first=None):
    T = [a for a, s in zip(ins, specs) if s[0] == 'row'][0].shape[0]
    tb = min(tb, T)
    n_in, n_out = len(ins), len(out_widths)
    diff_inputs = [i for i, s in enumerate(specs) if s[1]]
    n_add = 0 if add_to_first is None else 1

    def body(*refs):
        in_refs, dout_refs = refs[:n_in], refs[n_in:n_in + n_out]
        g_refs = refs[n_in + n_out + n_add:]
        args, amap = _load_args(in_refs, specs)
        didx = [j for j, (i, _) in enumerate(amap) if specs[i][1]]

        def f(*dv):
            full = list(args)
            for j, v in zip(didx, dv):
                full[j] = v
            return tuple(fn(*full))

        _, vjp = jax.vjp(f, *[args[j] for j in didx])
        gs = vjp(tuple(d[...] for d in dout_refs))
        gmap = {j: g for j, g in zip(didx, gs)}
        first = pl.program_id(0) == 0
        for g_ref, i in zip(g_refs, diff_inputs):
            kind, _, pieces = specs[i]
            js = [j for j, (ii, _) in enumerate(amap) if ii == i]
            if kind == 'row':
                if pieces is None:
                    if n_add and i == diff_inputs[0]:
                        g_ref[...] = gmap[js[0]] + refs[n_in + n_out][...]
                    else:
                        g_ref[...] = gmap[js[0]]
                else:
                    if sum(w for _, w in pieces) != ins[i].shape[1]:
                        g_ref[...] = jnp.zeros(g_ref.shape, F32)
                    for j in js:
                        s, w = amap[j][1]
                        g_ref[:, s:s + w] = gmap[j]
            else:
                @pl.when(first)
                def _(g_ref=g_ref):
                    g_ref[...] = jnp.zeros(g_ref.shape, F32)

                g_ref[...] += gmap[js[0]]

    in_specs = _stage_in_specs(ins, specs, tb) + [pl.BlockSpec((tb, w), lambda i: (i, 0)) for w in out_widths]
    extra = []
    if n_add:
        in_specs.append(pl.BlockSpec((tb, add_to_first.shape[1]), lambda i: (i, 0)))
        extra.append(add_to_first)
    out_specs, out_shape = [], []
    for i in diff_inputs:
        a = ins[i]
        if specs[i][0] == 'row':
            out_specs.append(pl.BlockSpec((tb, a.shape[1]), lambda i: (i, 0)))
        else:
            out_specs.append(pl.BlockSpec(a.shape, lambda i: (0, 0)))
        out_shape.append(jax.ShapeDtypeStruct(a.shape, F32))
    return pl.pallas_call(
        body, name=name + '_b', grid=(T // tb,), in_specs=in_specs, out_specs=out_specs, out_shape=out_shape,
        compiler_params=_cparams(('arbitrary',)),
    )(*ins, *douts, *extra)


def stage_op(fn, specs, out_widths, name, tb=256):
    n = len(specs)
    diff_inputs = [i for i, s in enumerate(specs) if s[1]]

    @jax.custom_vjp
    def op(*ins):
        return tuple(_stage_fwd(fn, ins, specs, out_widths, name, tb))

    def op_fwd(*ins):
        return op(*ins), ins

    def op_bwd(ins, douts):
        gs = _stage_bwd(fn, ins, specs, out_widths, douts, name, tb)
        res = [None] * n
        for i, g in zip(diff_inputs, gs):
            res[i] = g
        return tuple(res)

    op.defvjp(op_fwd, op_bwd)
    return op


@jax.custom_vjp
def bdot(x, w):
    return jnp.dot(x.astype(BF16), w.astype(BF16), preferred_element_type=F32)


def _bdot_fwd(x, w):
    return bdot(x, w), (x, w)


def _bdot_bwd(res, dy):
    x, w = res
    dyb = dy.astype(BF16)
    dx = lax.dot_general(dyb, w.astype(BF16), (((1,), (1,)), ((), ())), preferred_element_type=F32)
    dw = lax.dot_general(x.astype(BF16), dyb, (((0,), (0,)), ((), ())), preferred_element_type=F32)
    return dx, dw


bdot.defvjp(_bdot_fwd, _bdot_bwd)


def _sdot_raw(x, c):
    hi = x.astype(BF16)
    r1 = x - hi.astype(F32)
    mid = r1.astype(BF16)
    lo = (r1 - mid.astype(F32)).astype(BF16)
    d = lambda u: jnp.dot(u, c, preferred_element_type=F32)
    return d(hi) + d(mid) + d(lo)


@jax.custom_vjp
def sdot(x, c, ct):
    return _sdot_raw(x, c)


def _sdot_fwd(x, c, ct):
    return _sdot_raw(x, c), (c, ct)


def _sdot_bwd(res, dy):
    c, ct = res
    return _sdot_raw(dy, ct), None, None


sdot.defvjp(_sdot_fwd, _sdot_bwd)


def _sigmoid(x):
    return 1.0 / (1.0 + jnp.exp(-x))


def _rms(x, g):
    return x * lax.rsqrt(jnp.mean(x * x, axis=-1, keepdims=True) + NORM_EPS) * g


def _mm_up_relu2(a, b, name):
    M, K = a.shape
    tn = b.shape[2]
    N = N_DEV * tn
    tm = _pick(M, (1024, 512, 256, 128))

    def body(a_ref, b_ref, u_ref, act_ref):
        u = jnp.dot(a_ref[...], b_ref[...], preferred_element_type=F32)
        r = jnp.maximum(u, 0.0)
        u_ref[...] = u.astype(BF16)
        act_ref[...] = (r * r).astype(BF16)

    out = pl.BlockSpec((tm, tn), lambda i, j: (i, j))
    sh = jax.ShapeDtypeStruct((M, N), BF16)
    return pl.pallas_call(
        body, name=name, grid=(M // tm, N // tn),
        in_specs=[pl.BlockSpec((tm, K), lambda i, j: (i, 0)), pl.BlockSpec((None, K, tn), lambda i, j: (j, 0, 0))],
        out_specs=[out, out], out_shape=[sh, sh], compiler_params=_cparams(('parallel', 'parallel')),
    )(a, b)


def _mm_down_bwd(dy, b, u, name):
    M, N = dy.shape
    K = b.shape[0]
    tm = _pick(M, (1024, 512, 256, 128))
    tk = _pick(K, (512, 256, 128))

    def body(dy_ref, b_ref, u_ref, du_ref):
        d = lax.dot_general(dy_ref[...].astype(BF16), b_ref[...], (((1,), (1,)), ((), ())),
                            preferred_element_type=F32)
        du_ref[...] = (d * (2.0 * jnp.maximum(u_ref[...].astype(F32), 0.0))).astype(BF16)

    blk = pl.BlockSpec((tm, tk), lambda i, j: (i, j))
    return pl.pallas_call(
        body, name=name, grid=(M // tm, K // tk),
        in_specs=[pl.BlockSpec((tm, N), lambda i, j: (i, 0)), pl.BlockSpec((tk, N), lambda i, j: (j, 0)), blk],
        out_specs=blk, out_shape=jax.ShapeDtypeStruct((M, K), BF16),
        compiler_params=_cparams(('parallel', 'parallel')),
    )(dy, b, u)


_RMS_SPECS = [ROW(), FULL()]
_rms_fn = lambda xv, gv: (_rms(xv, gv),)


@functools.partial(jax.custom_vjp, nondiff_argnums=(6,))
def mlp_block(x, g, wup_b, wup_c, wdown_b, wdown_c, name):
    return _mlp_fwd(x, g, wup_b, wup_c, wdown_b, wdown_c, name)[0]


def _mlp_fwd(x, g, wup_b, wup_c, wdown_b, wdown_c, name):
    h = _stage_fwd(_rms_fn, (x, g), _RMS_SPECS, [x.shape[1]], name + '_norm', 256, out_dtype=BF16)[0]
    u, act = _mm_up_relu2(h, wup_b, name + '_up')
    y = _mm_nn(act, wdown_b, add=x, name=name + '_down')
    return y, (x, g, h, u, act, wup_b, wdown_b)


def _mlp_bwd(name, res, dy):
    x, g, h, u, act, wup_b, wdown_b = res
    du = _mm_down_bwd(dy, wdown_b, u, name + '_down_da')
    dwdown = _mm_tn(act, dy, name=name + '_down_dw')
    dh = _mm_nt(du, wup_b, name=name + '_up_da')
    dwup = _mm_tn(h, du, name=name + '_up_dw', blocked=True)
    dx, dg = _stage_bwd(_rms_fn, (x, g), _RMS_SPECS, [x.shape[1]], (dh,), name + '_norm', 256, add_to_first=dy)
    return dx, dg, None, dwup, None, dwdown


mlp_block.defvjp(_mlp_fwd, _mlp_bwd)


@functools.partial(jax.custom_vjp, nondiff_argnums=(4, 5, 6))
def _norm_projections(x, g, wbs, wcs, transposed, skip, name):
    return _norm_projections_fwd(x, g, wbs, wcs, transposed, skip, name)[0]


def _norm_projections_fwd(x, g, wbs, wcs, transposed, skip, name):
    h = _stage_fwd(_rms_fn, (x, g), _RMS_SPECS, [x.shape[1]], name + '_norm', 256, out_dtype=BF16)[0]
    mm = _mm_nt if transposed else _mm_nn
    outs = tuple(mm(h, wb, name=f'{name}_f{i}') for i, wb in enumerate(wbs))
    return ((outs, x) if skip else outs), (x, g, h, wbs)


def _norm_projections_bwd(transposed, skip, name, res, cts):
    x, g, h, wbs = res
    dys, dx_skip = cts if skip else (cts, None)
    dh = None
    for i, (dy, wb) in enumerate(zip(dys, wbs)):
        dh = (_mm_nn if transposed else _mm_nt)(dy, wb, add=dh, name=f'{name}_da{i}')
    if transposed:
        dws = tuple(_mm_tn(dy, h, name=f'{name}_dw{i}') for i, dy in enumerate(dys))
    else:
        dws = tuple(_mm_tn(h, dy, name=f'{name}_dw{i}') for i, dy in enumerate(dys))
    dx, dg = _stage_bwd(_rms_fn, (x, g), _RMS_SPECS, [x.shape[1]], (dh,), name + '_norm', 256, add_to_first=dx_skip)
    return dx, dg, None, dws


_norm_projections.defvjp(_norm_projections_fwd, _norm_projections_bwd)


def norm_projections(x, g, ws, name, transposed=False, skip=False):
    return _norm_projections(x, g.reshape(1, -1), tuple(w.b for w in ws), tuple(w.c for w in ws), transposed, skip,
                             name)


def _shift_down(x, rows):
    return jnp.where(rows == 0, 0.0, pltpu.roll(x, 1, 0))


def _shift_up(x, rows, T):
    return jnp.where(rows == T - 1, 0.0, pltpu.roll(x, T - 1, 0))


def _tshift_fwd_call(x, mu, name):
    T, C = x.shape

    def body(x_ref, mu_ref, o_ref):
        xv = x_ref[...]
        rows = lax.broadcasted_iota(jnp.int32, xv.shape, 0)
        o_ref[...] = xv + (_shift_down(xv, rows) - xv) * mu_ref[...]

    return pl.pallas_call(
        body, name=name + '_f', grid=(C // LANES,),
        in_specs=[pl.BlockSpec((T, LANES), lambda j: (0, j)), pl.BlockSpec((1, LANES), lambda j: (0, j))],
        out_specs=pl.BlockSpec((T, LANES), lambda j: (0, j)), out_shape=jax.ShapeDtypeStruct((T, C), F32),
        compiler_params=_cparams(('parallel',)),
    )(x, mu)


def _tshift_bwd_call(x, mu, dy, name):
    T, C = x.shape

    def body(x_ref, mu_ref, dy_ref, dx_ref, dmu_ref):
        xv, d = x_ref[...], dy_ref[...]
        rows = lax.broadcasted_iota(jnp.int32, xv.shape, 0)
        z = d * mu_ref[...]
        dx_ref[...] = d - z + _shift_up(z, rows, T)
        dmu_ref[...] = jnp.sum(d * (_shift_down(xv, rows) - xv), axis=0, keepdims=True)

    return pl.pallas_call(
        body, name=name + '_b', grid=(C // LANES,),
        in_specs=[pl.BlockSpec((T, LANES), lambda j: (0, j)), pl.BlockSpec((1, LANES), lambda j: (0, j)),
                  pl.BlockSpec((T, LANES), lambda j: (0, j))],
        out_specs=[pl.BlockSpec((T, LANES), lambda j: (0, j)), pl.BlockSpec((1, LANES), lambda j: (0, j))],
        out_shape=[jax.ShapeDtypeStruct((T, C), F32), jax.ShapeDtypeStruct((1, C), F32)],
        compiler_params=_cparams(('parallel',)),
    )(x, mu, dy)


@functools.partial(jax.custom_vjp, nondiff_argnums=(2,))
def token_shift_mix(x, mu, name):
    return _tshift_fwd_call(x, mu, name)


def _tsm_fwd(x, mu, name):
    return _tshift_fwd_call(x, mu, name), (x, mu)


def _tsm_bwd(name, res, dy):
    x, mu = res
    dx, dmu = _tshift_bwd_call(x, mu, dy, name)
    return dx, dmu


token_shift_mix.defvjp(_tsm_fwd, _tsm_bwd)


def _conv_specs(T):
    nb = CONV_WIDTH // LANES
    return [pl.BlockSpec((T, LANES), lambda j: (0, j)), pl.BlockSpec((T, LANES), lambda j: (0, nb + j)),
            pl.BlockSpec((T, LANES), lambda j: (0, 2 * nb + j)), pl.BlockSpec((3, LANES), lambda j: (0, j))]


def _conv_fwd_call(cv, w, name):
    T = cv.shape[0]

    def body(b_ref, c_ref, x_ref, w_ref, o_ref):
        u = c_ref[...] * x_ref[...]
        rows = lax.broadcasted_iota(jnp.int32, u.shape, 0)
        u1 = _shift_down(u, rows)
        u2 = _shift_down(u1, rows)
        o_ref[...] = b_ref[...] * (w_ref[0:1, :] * u2 + w_ref[1:2, :] * u1 + w_ref[2:3, :] * u)

    return pl.pallas_call(
        body, name=name + '_f', grid=(CONV_WIDTH // LANES,), in_specs=_conv_specs(T),
        out_specs=pl.BlockSpec((T, LANES), lambda j: (0, j)),
        out_shape=jax.ShapeDtypeStruct((T, CONV_WIDTH), F32), compiler_params=_cparams(('parallel',)),
    )(cv, cv, cv, w)


def _conv_bwd_call(cv, w, do, name):
    T = cv.shape[0]

    def body(b_ref, c_ref, x_ref, w_ref, do_ref, db_ref, dc_ref, dx_ref, dw_ref):
        c, x, d = c_ref[...], x_ref[...], do_ref[...]
        u = c * x
        rows = lax.broadcasted_iota(jnp.int32, u.shape, 0)
        u1 = _shift_down(u, rows)
        u2 = _shift_down(u1, rows)
        w0, w1, w2 = w_ref[0:1, :], w_ref[1:2, :], w_ref[2:3, :]
        db_ref[...] = d * (w0 * u2 + w1 * u1 + w2 * u)
        dy = d * b_ref[...]
        dy1 = _shift_up(dy, rows, T)
        dy2 = _shift_up(dy1, rows, T)
        du = w2 * dy + w1 * dy1 + w0 * dy2
        dc_ref[...] = du * x
        dx_ref[...] = du * c
        dw_ref[0:1, :] = jnp.sum(dy * u2, axis=0, keepdims=True)
        dw_ref[1:2, :] = jnp.sum(dy * u1, axis=0, keepdims=True)
        dw_ref[2:3, :] = jnp.sum(dy * u, axis=0, keepdims=True)

    blk = pl.BlockSpec((T, LANES), lambda j: (0, j))
    sh = jax.ShapeDtypeStruct((T, CONV_WIDTH), F32)
    return pl.pallas_call(
        body, name=name + '_b', grid=(CONV_WIDTH // LANES,), in_specs=_conv_specs(T) + [blk],
        out_specs=[blk, blk, blk, pl.BlockSpec((3, LANES), lambda j: (0, j))],
        out_shape=[sh, sh, sh, jax.ShapeDtypeStruct((3, CONV_WIDTH), F32)],
        compiler_params=_cparams(('parallel',)),
    )(cv, cv, cv, w, do)


@functools.partial(jax.custom_vjp, nondiff_argnums=(2,))
def short_conv(cv, w, name):
    return _conv_fwd_call(cv, w, name)


def _sc_fwd(cv, w, name):
    return _conv_fwd_call(cv, w, name), (cv, w)


def _sc_bwd(name, res, do):
    cv, w = res
    db, dc, dx, dw = _conv_bwd_call(cv, w, do, name)
    return jnp.concatenate([db, dc, dx], axis=1), dw


short_conv.defvjp(_sc_fwd, _sc_bwd)


ATT_SCALE = QK_HEAD ** -0.5
NPAIR = MLA_HEADS // 2
ROPE_HALF = QK_ROPE // 2
GROUP_HEADS = LANES // QK_ROPE


def _att_bq(T):
    return min(256, T)


def _att_masks(pair, j):
    lane = lax.broadcasted_iota(jnp.int32, (1, LANES), 1)
    mask_n = (lane // QK_NOPE) == j
    mask_r = ((lane % (LANES // 2)) // ROPE_HALF) == ((2 * pair + j) % GROUP_HEADS)
    return mask_n, mask_r


def _att_probs(qcat, kcat, row0, stop):
    s = lax.dot_general(qcat, kcat, (((1,), (1,)), ((), ())), preferred_element_type=F32) * ATT_SCALE
    r = row0 + lax.broadcasted_iota(jnp.int32, s.shape, 0)
    c = lax.broadcasted_iota(jnp.int32, s.shape, 1)
    s = jnp.where(c <= r, s, -jnp.inf)
    e = jnp.exp(s - jnp.max(s, axis=-1, keepdims=True))
    return e / jnp.sum(e, axis=-1, keepdims=True)


def _att_in_specs(T):
    per_pair = pl.BlockSpec((T, LANES), lambda p: (0, p))
    per_group = pl.BlockSpec((T, LANES), lambda p: (0, p // 2))
    return [per_pair, per_group, per_pair, per_group, per_pair]


def _att_fwd_call(qn, qr, kn, kr, v, name):
    T = qn.shape[0]
    bq = _att_bq(T)

    def body(qn_ref, qr_ref, kn_ref, kr_ref, v_ref, o_ref):
        pair = pl.program_id(0)
        for i in range(T // bq):
            r0, stop = i * bq, (i + 1) * bq
            kcat = jnp.concatenate([kn_ref[0:stop, :], kr_ref[0:stop, :]], axis=1).astype(BF16)
            vb = v_ref[0:stop, :].astype(BF16)
            outs = []
            for j in range(2):
                mask_n, mask_r = _att_masks(pair, j)
                qcat = jnp.concatenate([jnp.where(mask_n, qn_ref[r0:stop, :], 0.0),
                                        jnp.where(mask_r, qr_ref[r0:stop, :], 0.0)], axis=1).astype(BF16)
                p = _att_probs(qcat, kcat, r0, stop)
                outs.append(jnp.dot(p.astype(BF16), vb, preferred_element_type=F32))
            mask_n0, _ = _att_masks(pair, 0)
            o_ref[r0:stop, :] = jnp.where(mask_n0, outs[0], outs[1])

    return pl.pallas_call(
        body, name=name + '_f', grid=(NPAIR,), in_specs=_att_in_specs(T),
        out_specs=pl.BlockSpec((T, LANES), lambda p: (0, p)),
        out_shape=jax.ShapeDtypeStruct((T, MLA_HEADS * V_HEAD), F32), compiler_params=_cparams(('parallel',)),
    )(qn, qr, kn, kr, v)


def _att_bwd_call(qn, qr, kn, kr, v, o, do, name):
    T = qn.shape[0]
    bq = _att_bq(T)

    def body(qn_ref, qr_ref, kn_ref, kr_ref, v_ref, o_ref, do_ref,
             dqn_ref, dqr_ref, dkn_ref, dkr_ref, dv_ref, dk_acc, dv_acc):
        pair = pl.program_id(0)

        @pl.when(pair % 2 == 0)
        def _():
            dqr_ref[...] = jnp.zeros(dqr_ref.shape, F32)
            dkr_ref[...] = jnp.zeros(dkr_ref.shape, F32)

        dk_acc[...] = jnp.zeros(dk_acc.shape, F32)
        dv_acc[...] = jnp.zeros(dv_acc.shape, F32)
        for i in range(T // bq):
            r0, stop = i * bq, (i + 1) * bq
            kcat = jnp.concatenate([kn_ref[0:stop, :], kr_ref[0:stop, :]], axis=1).astype(BF16)
            vb = v_ref[0:stop, :].astype(BF16)
            dqn = jnp.zeros((bq, LANES), F32)
            for j in range(2):
                mask_n, mask_r = _att_masks(pair, j)
                qcat = jnp.concatenate([jnp.where(mask_n, qn_ref[r0:stop, :], 0.0),
                                        jnp.where(mask_r, qr_ref[r0:stop, :], 0.0)], axis=1).astype(BF16)
                p = _att_probs(qcat, kcat, r0, stop)
                dom = jnp.where(mask_n, do_ref[r0:stop, :], 0.0)
                delta = jnp.sum(dom * o_ref[r0:stop, :], axis=-1, keepdims=True)
                domb = dom.astype(BF16)
                dp = lax.dot_general(domb, vb, (((1,), (1,)), ((), ())), preferred_element_type=F32)
                ds = (p * (dp - delta) * ATT_SCALE).astype(BF16)
                dqc = jnp.dot(ds, kcat, preferred_element_type=F32)
                dqn = dqn + jnp.where(mask_n, dqc[:, 0:LANES], 0.0)
                dqr_ref[r0:stop, :] += jnp.where(mask_r, dqc[:, LANES:2 * LANES], 0.0)
                dk_acc[0:stop, :] += lax.dot_general(ds, qcat, (((0,), (0,)), ((), ())),
                                                     preferred_element_type=F32)
                dv_acc[0:stop, :] += lax.dot_general(p.astype(BF16), domb, (((0,), (0,)), ((), ())),
                                                     preferred_element_type=F32)
            dqn_ref[r0:stop, :] = dqn
        dkn_ref[...] = dk_acc[:, 0:LANES]
        dkr_ref[...] += dk_acc[:, LANES:2 * LANES]
        dv_ref[...] = dv_acc[...]

    per_pair = pl.BlockSpec((T, LANES), lambda p: (0, p))
    per_group = pl.BlockSpec((T, LANES), lambda p: (0, p // 2))
    wide = jax.ShapeDtypeStruct((T, MLA_HEADS * QK_NOPE), F32)
    rope = jax.ShapeDtypeStruct((T, 2 * LANES), F32)
    return pl.pallas_call(
        body, name=name + '_b', grid=(NPAIR,), in_specs=_att_in_specs(T) + [per_pair, per_pair],
        out_specs=[per_pair, per_group, per_pair, per_group, per_pair],
        out_shape=[wide, rope, wide, rope, wide],
        scratch_shapes=[pltpu.VMEM((T, 2 * LANES), F32), pltpu.VMEM((T, LANES), F32)],
        compiler_params=_cparams(('arbitrary',)),
    )(qn, qr, kn, kr, v, o, do)


@functools.partial(jax.custom_vjp, nondiff_argnums=(5,))
def attention(qn, qr, kn, kr, v, name):
    return _att_fwd_call(qn, qr, kn, kr, v, name)


def _attn_fwd(qn, qr, kn, kr, v, name):
    o = _att_fwd_call(qn, qr, kn, kr, v, name)
    return o, (qn, qr, kn, kr, v, o)


def _attn_bwd(name, res, do):
    return tuple(_att_bwd_call(*res, do, name))


attention.defvjp(_attn_fwd, _attn_bwd)


SCAN_CHUNK = 64
SCAN_UNROLL = 8


def _block_ones(n, seg):
    i = np.arange(n)
    return (i[:, None] // seg == i[None, :] // seg).astype(np.float32)


def _scan_diag():
    i = np.arange(RW_WIDTH)
    return jnp.asarray((np.arange(RW_N)[:, None] == (i[None, :] % RW_N)).astype(np.float32))


def _head_rowsum(x):
    low = lax.broadcasted_iota(jnp.int32, (1, LANES), 1) < RW_N
    tiles = []
    for j in range(RW_WIDTH // LANES):
        xt = x[:, j * LANES:(j + 1) * LANES]
        x0 = jnp.where(low, xt, 0.0)
        s0 = jnp.sum(x0, axis=-1, keepdims=True)
        s1 = jnp.sum(xt - x0, axis=-1, keepdims=True)
        tiles.append(jnp.where(low, s0, s1))
    return jnp.concatenate(tiles, axis=1)


def _unrolled_loop(n, step, init):
    def body(i, carry):
        for j in range(SCAN_UNROLL):
            carry = step(i * SCAN_UNROLL + j, carry)
        return carry
    return lax.fori_loop(0, n // SCAN_UNROLL, body, init)


def _scan_fwd_call(r, w, k, v, a, b, name):
    T = r.shape[0]
    tc = min(SCAN_CHUNK, T)
    dg = _scan_diag()

    def body(r_ref, w_ref, k_ref, v_ref, a_ref, b_ref, dg_ref, y_ref, st_ref, sa_ref, vc_ref, s_ref):
        @pl.when(pl.program_id(0) == 0)
        def _():
            s_ref[...] = jnp.zeros(s_ref.shape, F32)

        dgv = dg_ref[...]
        readout = lambda s, t: jnp.sum(_head_rowsum(s * r_ref[t]) * dgv, axis=0, keepdims=True)

        def step(t, carry):
            s, vcol = carry
            st_ref[t] = s
            vc_ref[t] = vcol
            sa = _head_rowsum(s * a_ref[t])
            sa_ref[t] = sa
            prev = jnp.maximum(t - 1, 0)
            y_ref[prev] = readout(s, prev)
            vcol_next = _head_rowsum(dgv * v_ref[jnp.minimum(t + 1, tc - 1)])
            sn = s * w_ref[t] + sa * b_ref[t] + vcol * k_ref[t]
            return sn, vcol_next

        s_end, _ = _unrolled_loop(tc, step, (s_ref[...], _head_rowsum(dgv * v_ref[0])))
        y_ref[tc - 1] = readout(s_end, tc - 1)
        s_ref[...] = s_end

    vec = pl.BlockSpec((tc, 1, RW_WIDTH), lambda i: (i, 0, 0))
    mat = pl.BlockSpec((tc, RW_N, RW_WIDTH), lambda i: (i, 0, 0))
    msh = jax.ShapeDtypeStruct((T, RW_N, RW_WIDTH), F32)
    return pl.pallas_call(
        body, name=name + '_f', grid=(T // tc,),
        in_specs=[vec] * 6 + [pl.BlockSpec((RW_N, RW_WIDTH), lambda i: (0, 0))],
        out_specs=[vec, mat, mat, mat],
        out_shape=[jax.ShapeDtypeStruct((T, 1, RW_WIDTH), F32), msh, msh, msh],
        scratch_shapes=[pltpu.VMEM((RW_N, RW_WIDTH), F32)],
        compiler_params=_cparams(('arbitrary',)),
    )(r, w, k, v, a, b, dg)


def _scan_bwd_call(r, w, k, a, b, st, sa_all, vc_all, dy, name):
    T = r.shape[0]
    tc = min(SCAN_CHUNK, T)
    nt = T // tc
    dg = _scan_diag()

    def body(r_ref, w_ref, k_ref, a_ref, b_ref, st_ref, sa_ref, vc_ref, dy_ref, dg_ref,
             dr_ref, dw_ref, dk_ref, dv_ref, da_ref, db_ref, ds_ref):
        @pl.when(pl.program_id(0) == 0)
        def _():
            ds_ref[...] = jnp.zeros(ds_ref.shape, F32)

        dgv = dg_ref[...]
        colsum = lambda x: jnp.sum(x, axis=0, keepdims=True)

        def step(i, carry):
            ds, dycol = carry
            t = tc - 1 - i
            sp = st_ref[t]
            rt, wt, kt, at, bt = r_ref[t], w_ref[t], k_ref[t], a_ref[t], b_ref[t]
            ds = ds + dycol * rt
            dsa = _head_rowsum(ds * bt)
            sa, vcol = sa_ref[t], vc_ref[t]
            dycol_next = _head_rowsum(dgv * dy_ref[jnp.maximum(t - 1, 0)])
            sn = sp * wt + sa * bt + vcol * kt
            dr_ref[t] = colsum(sn * dycol)
            dk_ref[t] = colsum(ds * vcol)
            db_ref[t] = colsum(ds * sa)
            dw_ref[t] = colsum(ds * sp)
            dv_ref[t] = colsum(_head_rowsum(ds * kt) * dgv)
            da_ref[t] = colsum(sp * dsa)
            return ds * wt + dsa * at, dycol_next

        ds_end, _ = _unrolled_loop(tc, step, (ds_ref[...], _head_rowsum(dgv * dy_ref[tc - 1])))
        ds_ref[...] = ds_end

    vec = pl.BlockSpec((tc, 1, RW_WIDTH), lambda i: (nt - 1 - i, 0, 0))
    mat = pl.BlockSpec((tc, RW_N, RW_WIDTH), lambda i: (nt - 1 - i, 0, 0))
    vsh = jax.ShapeDtypeStruct((T, 1, RW_WIDTH), F32)
    return pl.pallas_call(
        body, name=name + '_b', grid=(nt,),
        in_specs=[vec] * 5 + [mat] * 3 + [vec, pl.BlockSpec((RW_N, RW_WIDTH), lambda i: (0, 0))],
        out_specs=[vec] * 6, out_shape=[vsh] * 6,
        scratch_shapes=[pltpu.VMEM((RW_N, RW_WIDTH), F32)],
        compiler_params=_cparams(('arbitrary',)),
    )(r, w, k, a, b, st, sa_all, vc_all, dy, dg)


@functools.partial(jax.custom_vjp, nondiff_argnums=(6,))
def wkv7(r, w, k, v, a, b, name):
    return _scan_fwd_call(r, w, k, v, a, b, name)[0]


def _wkv7_fwd(r, w, k, v, a, b, name):
    y, st, sa_all, vc_all = _scan_fwd_call(r, w, k, v, a, b, name)
    return y, (r, w, k, a, b, st, sa_all, vc_all)


def _wkv7_bwd(name, res, dy):
    return tuple(_scan_bwd_call(*res, dy, name))


wkv7.defvjp(_wkv7_fwd, _wkv7_bwd)


def _np_bf16(a):
    return jnp.asarray(a, BF16)


def _rope_lane(head, half, i):
    return (head // GROUP_HEADS) * LANES + half * (LANES // 2) + (head % GROUP_HEADS) * ROPE_HALF + i


def _mla_consts():
    seg_n = (np.arange(512)[:, None] // QK_NOPE == np.arange(LANES)[None, :]).astype(np.float32)
    seg_r = np.zeros((2 * LANES, LANES), np.float32)
    er = np.zeros((LANES, 2 * LANES), np.float32)
    es = np.zeros((LANES, 2 * LANES), np.float32)
    for h in range(MLA_HEADS):
        for half in range(2):
            for i in range(ROPE_HALF):
                seg_r[_rope_lane(h, half, i), h] = 1.0
                er[half * ROPE_HALF + i, _rope_lane(h, half, i)] = 1.0
                es[half * ROPE_HALF + i, _rope_lane(h, 1 - half, i)] = 1.0
    mats = [seg_n, seg_n.T, seg_r, seg_r.T, er, er.T, es, es.T]
    return [_np_bf16(m) for m in mats]


def _qk_prep_fn(qn, qr, qs, kn, kx, cos, sin, gqn, gqr, gqs, gkn, gkr, gks,
                seg_n, seg_nt, seg_r, seg_rt, er, ert, es, est):
    cos2 = jnp.concatenate([cos, cos], axis=1)
    sin2 = jnp.concatenate([sin, sin], axis=1)

    def normrope(xn, xr, xs, gn, gr, gs):
        ss = sdot(xn * xn, seg_n, seg_nt) + sdot(xr * xr, seg_r, seg_rt)
        inv = lax.rsqrt(ss * (1.0 / QK_HEAD) + NORM_EPS)
        inv_n = sdot(inv, seg_nt, seg_n)
        inv_r = sdot(inv, seg_rt, seg_r)
        return xn * inv_n * gn, (xr * gr * cos2 + xs * gs * sin2) * inv_r

    kr = sdot(kx, er, ert)
    ks = sdot(kx, es, est)
    return normrope(qn, qr, qs, gqn, gqr, gqs) + normrope(kn, kr, ks, gkn, gkr, gks)


def _rwkv_prep_fn(vres):
    def fn(r, k, v, xg, xwa, kx, *rest):
        if vres:
            vfirst, w0, a0, k_k, k_a, w2p, a2p, g2, v0, v2p, bm = rest
        else:
            w0, a0, k_k, k_a, w2p, a2p, g2, bm = rest
        z = w0 + bdot(jnp.tanh(xwa), w2p)
        nz = -z
        softplus = jnp.maximum(nz, 0.0) + jnp.log(1.0 + jnp.exp(-jnp.abs(nz)))
        decay = jnp.exp(-jnp.exp(-softplus - 0.5))
        a = _sigmoid(a0 + bdot(xwa, a2p))
        g = bdot(_sigmoid(xg), g2)
        if vres:
            vv = v + (vfirst - v) * _sigmoid(v0 + bdot(kx, v2p))
        else:
            vv = v
        kkr = k * k_k
        kk = kkr / jnp.maximum(jnp.sqrt(sdot(kkr * kkr, bm, bm)), 1e-12)
        k2 = k * (1.0 + (a - 1.0) * k_a)
        return r * 1.0, decay, k2, vv, -kk, kk * a, g
    return fn


def _rwkv_post_fn(y, r, k2, vv, g, ln_w, ln_b, rk, bm):
    inv_n = 1.0 / RW_N
    mean = sdot(y, bm, bm) * inv_n
    yc = y - mean
    var = sdot(yc * yc, bm, bm) * inv_n
    yn = yc * lax.rsqrt(var + GN_EPS) * ln_w + ln_b
    bonus = sdot(r * k2 * rk, bm, bm) * vv
    return ((yn + bonus) * g,)


def _merge_fn(g0, g1, g2, oa, ob, oc):
    return (_sigmoid(g0) * oa + _sigmoid(g1) * ob + _sigmoid(g2) * oc,)


def _loss_call(y, target):
    T, C = y.shape
    tb = min(256, T)

    def body(y_ref, t_ref, dy_ref, part_ref):
        err = y_ref[...] - t_ref[...]
        dy_ref[...] = err * (1.0 / C)
        sq = jnp.sum(err * err, axis=0, keepdims=True)
        acc = sq[:, 0:LANES]
        for j in range(1, C // LANES):
            acc = acc + sq[:, j * LANES:(j + 1) * LANES]
        part_ref[...] = jnp.zeros(part_ref.shape, F32)
        part_ref[0:1, :] = acc * (0.5 / C)

    return pl.pallas_call(
        body, name='loss', grid=(T // tb,),
        in_specs=[pl.BlockSpec((tb, C), lambda i: (i, 0))] * 2,
        out_specs=[pl.BlockSpec((tb, C), lambda i: (i, 0)), pl.BlockSpec((8, LANES), lambda i: (i, 0))],
        out_shape=[jax.ShapeDtypeStruct((T, C), F32), jax.ShapeDtypeStruct((8 * (T // tb), LANES), F32)],
        compiler_params=_cparams(('parallel',)),
    )(y, target)


def _pad_rows(t, before, total):
    return jnp.pad(t, ((before, total - before - t.shape[0]), (0, 0)))


def _head_gains(g):
    g1, g2 = jnp.tile(g[64:80], GROUP_HEADS), jnp.tile(g[80:96], GROUP_HEADS)
    return (jnp.tile(g[0:64], MLA_HEADS).reshape(1, -1), jnp.concatenate([g1, g2, g1, g2]).reshape(1, -1),
            jnp.concatenate([g2, g1, g2, g1]).reshape(1, -1))


IN_WEIGHTS = ('w_in', 'rwkv_v1')


def _layer(l, x, v_first, wd, sp, cos, sin):
    proj, x_skip = _layer_in(l, x, wd, sp)
    return _layer_rest(l, proj, x_skip, v_first, wd, sp, cos, sin)


def _layer_in(l, x, wd, sp):
    nm = f'l{l}'
    vres = l > 0
    w_in = wd['w_in']
    if vres:
        v1t = wd['rwkv_v1'].map(lambda t: t.T)
    else:
        v1t = W(jnp.zeros((MV_LORA, D_MODEL), BF16), jnp.zeros((MV_LORA, D_MODEL), BF16))
    zpad = W(jnp.zeros((64, D_MODEL), BF16), jnp.zeros((64, D_MODEL), BF16))
    w_rw = _wcat([w_in['rkv'], w_in['xg'], w_in['xwa'], w_in['kpe'], v1t, zpad], 0)
    return norm_projections(
        x, sp['attn_norm'], [w_in['gates'], w_in['cq'], w_in['ckv'], w_rw, w_in['conv']], nm + '_win',
        transposed=True, skip=True)


def _layer_rest(l, proj, x_skip, v_first, wd, sp, cos, sin):
    gates, cq, ckv, rw, cv = proj
    T = rw.shape[0]
    nm = f'l{l}'
    vres = l > 0

    v_mu = sp['rwkv_v_mu'] if vres else jnp.zeros((MV_LORA,), F32)
    mu_all = jnp.concatenate([sp['rwkv_mu'][0:768], sp['rwkv_mu'][896:1024], sp['rwkv_mu'][768:896],
                              jnp.zeros((QK_ROPE,), F32), v_mu, jnp.zeros((64,), F32)]).reshape(1, -1)
    rws = token_shift_mix(rw, mu_all, nm + '_shift')

    def q_columns(t):
        t3 = t.reshape(Q_LORA, MLA_HEADS, QK_HEAD)
        x1 = t3[:, :, 64:80].reshape(Q_LORA, 2, LANES // 2)
        x2 = t3[:, :, 80:96].reshape(Q_LORA, 2, LANES // 2)
        return jnp.concatenate([t3[:, :, 0:64].reshape(Q_LORA, 512), x1[:, 0], x2[:, 0], x1[:, 1], x2[:, 1],
                                x2[:, 0], x1[:, 0], x2[:, 1], x1[:, 1]], axis=1)

    wq = wd['mla_wq_b'].map(q_columns)
    wkn = wd['mla_wkv_b'].map(lambda t: t.reshape(KV_LORA, MLA_HEADS, 128)[:, :, 0:64].reshape(KV_LORA, 512))
    wv = wd['mla_wkv_b'].map(lambda t: t.reshape(KV_LORA, MLA_HEADS, 128)[:, :, 64:128].reshape(KV_LORA, 512))
    q, = norm_projections(cq, sp['mla_q_a_norm'], [wq], nm + '_wq')
    kn, vv_att = norm_projections(ckv, sp['mla_kv_a_norm'], [wkn, wv], nm + '_wkv')
    gq, gk = sp['mla_q_norm'], sp['mla_k_norm']
    consts = _mla_consts()
    qk_specs = ([ROW(pieces=((0, 512), (512, 256), (768, 256))), ROW(), ROW(pieces=((1024, 128),)),
                 ROW(False), ROW(False)] + [FULL()] * 6 + [FULL(False)] * 8)
    qk_op = stage_op(_qk_prep_fn, qk_specs, [512, 256, 512, 256], nm + '_qkprep')
    Qn, Qr, Kn, Kr = qk_op(q, kn, rws, cos, sin, *_head_gains(gq), *_head_gains(gk), *consts)
    o_att = attention(Qn, Qr, Kn, Kr, vv_att, nm + '_att')
    o_a = linear(o_att, wd['mla_w_o'], name=nm + '_wo')

    bm = _np_bf16(_block_ones(RW_WIDTH, RW_N))
    vec = lambda n: sp[n].reshape(1, -1)
    f32w = lambda n: wd[n].c + wd[n].b.astype(F32)
    w2p = _pad_rows(f32w('rwkv_w2'), 0, 128)
    a2p = _pad_rows(f32w('rwkv_a2'), 64, 128)
    g2 = f32w('rwkv_g2')
    rw_pieces = ((0, 256), (256, 256), (512, 256), (768, 128), (896, 128), (1024, 128))
    if vres:
        v2p = _pad_rows(f32w('rwkv_v2'), 32, 128)
        prep_specs = [ROW(pieces=rw_pieces), ROW()] + [FULL()] * 9 + [FULL(False)]
        prep_in = [rws, v_first, vec('rwkv_w0'), vec('rwkv_a0'), vec('rwkv_k_k'), vec('rwkv_k_a'), w2p, a2p, g2,
                   vec('rwkv_v0'), v2p, bm]
    else:
        prep_specs = [ROW(pieces=rw_pieces)] + [FULL()] * 7 + [FULL(False)]
        prep_in = [rws, vec('rwkv_w0'), vec('rwkv_a0'), vec('rwkv_k_k'), vec('rwkv_k_a'), w2p, a2p, g2, bm]
    prep_op = stage_op(_rwkv_prep_fn(vres), prep_specs, [256] * 7, nm + '_rwprep')
    r_, dec, k2, vv, an, bn, g = prep_op(*prep_in)
    if not vres:
        v_first = vv
    t3 = lambda t: t.reshape(T, 1, RW_WIDTH)
    y = wkv7(t3(r_), t3(dec), t3(k2), t3(vv), t3(an), t3(bn), nm + '_scan').reshape(T, RW_WIDTH)
    post_op = stage_op(_rwkv_post_fn, [ROW()] * 5 + [FULL()] * 3 + [FULL(False)], [256], nm + '_rwpost')
    yb = post_op(y, r_, k2, vv, g, vec('rwkv_ln_w'), vec('rwkv_ln_b'), sp['rwkv_r_k'].reshape(1, -1), bm)[0]
    o_b = linear(yb, wd['rwkv_w_o'], name=nm + '_rwo')

    oc_in = short_conv(cv, f32w('conv_w'), nm + '_conv')
    o_c = linear(oc_in, wd['conv_w_o'], name=nm + '_cwo')

    merge_op = stage_op(_merge_fn, [ROW(pieces=((0, 1024), (1024, 1024), (2048, 1024))), ROW(), ROW(), ROW()],
                        [D_MODEL], nm + '_merge')
    merged = merge_op(gates, o_a, o_b, o_c)[0]
    x2 = linear(merged, wd['w_out'], add=x_skip, name=nm + '_wout')
    return x2, v_first


def _mlp(l, x, wd, sp):
    return mlp_block(x, sp['mlp_norm'].reshape(1, -1), wd['w_up'].b, wd['w_up'].c, wd['w_down'].b, wd['w_down'].c,
                     f'l{l}_mlp')


MLP_WEIGHTS = ('w_up', 'w_down')


def _entries(ml):
    layer, part = ml if isinstance(ml, tuple) else (ml, 'all')
    out = []
    for name, (layers, shape, axis) in SHARDED.items():
        l = layer if layers == DEPTH else layer - 1
        if not 0 <= l < layers or name in MLP_WEIGHTS:
            continue
        if part != 'all' and (name in IN_WEIGHTS) != (part == 'in'):
            continue
        n = shape[0] * shape[1] // N_DEV
        if name == 'conv_w':
            out.append(('conv_w_hi', name, l, n))
            out.append(('conv_w_lo', name, l, n))
        else:
            out.append((name, name, l, n))
    return out


def _slot_size(n):
    return -(-n // LANES) * LANES


def _pack_rows(ml):
    total = sum(_slot_size(n) for _, _, _, n in _entries(ml))
    rows = -(-total // LANES)
    return -(-rows // PACK_ROW_MULT) * PACK_ROW_MULT


def _pack_flat(pieces, ml):
    rows = _pack_rows(ml)
    padded = []
    for p, (_, _, _, n) in zip(pieces, _entries(ml)):
        pad = _slot_size(n) - n
        if pad:
            p = jnp.pad(p, [(0, 0)] * (p.ndim - 1) + [(0, pad)])
        padded.append(p)
    flat = jnp.concatenate(padded, axis=-1)
    tail = rows * LANES - flat.shape[-1]
    if tail:
        flat = jnp.pad(flat, [(0, 0)] * (flat.ndim - 1) + [(0, tail)])
    return flat.reshape(flat.shape[:-1] + (rows, LANES))


def _unpack_flat(buf, ml):
    out, row = [], 0
    for _, _, _, n in _entries(ml):
        nrows = _slot_size(n) // LANES
        piece = buf[..., row:row + nrows, :].reshape(buf.shape[:-2] + (-1,))
        out.append(piece[..., :n])
        row += nrows
    return out


def _pack_shards(shards, ml, dtype, split_conv):
    pieces = []
    for slot, name, l, n in _entries(ml):
        a = shards[name][l]
        if name in TRANSPOSED:
            a = a.astype(dtype).T
        a = a.reshape(-1)
        if slot == 'conv_w_hi':
            a = a.astype(BF16).astype(F32) if split_conv else a
        elif slot == 'conv_w_lo':
            a = (a - a.astype(BF16).astype(F32)) if split_conv else jnp.zeros_like(a)
        pieces.append(a.astype(dtype))
    return _pack_flat(pieces, ml)


def _unpack_shards(buf, ml):
    out = {}
    for (slot, name, l, n), v in zip(_entries(ml), _unpack_flat(buf, ml)):
        if slot == 'conv_w_lo':
            continue
        layers, shape, axis = SHARDED[name]
        sshape = (shape[0] // N_DEV, shape[1]) if axis == 0 else (shape[0], shape[1] // N_DEV)
        out[(name, l)] = v.reshape(sshape[::-1]).T if name in TRANSPOSED else v.reshape(sshape)
    return out


def _to_full(blocks, shape, axis):
    if axis == 0:
        return blocks.reshape(shape)
    return blocks.reshape(N_DEV, shape[0], shape[1] // N_DEV).transpose(1, 0, 2).reshape(shape)


def _to_blocks(full, axis):
    r, c = full.shape
    if axis == 0:
        return full.reshape(N_DEV, -1)
    return full.reshape(r, N_DEV, c // N_DEV).transpose(1, 0, 2).reshape(N_DEV, -1)


def _unpack_gathered(gathered, ml):
    out, conv_hi = {}, None
    for (slot, name, l, n), v in zip(_entries(ml), _unpack_flat(gathered, ml)):
        layers, shape, axis = SHARDED[name]
        if name in TRANSPOSED:
            out[name] = v.reshape(-1, shape[0])
            continue
        full = _to_full(v, shape, axis)
        if slot == 'conv_w_hi':
            conv_hi = full
        elif slot == 'conv_w_lo':
            out[name] = conv_hi.astype(F32) + full.astype(F32)
        else:
            out[name] = full
    return out


def _pack_grads(grads, ml):
    pieces = []
    for slot, name, l, n in _entries(ml):
        if name in TRANSPOSED:
            blocks = jnp.concatenate(grads[name], axis=0).reshape(N_DEV, -1)
        else:
            blocks = _to_blocks(grads[name], SHARDED[name][2])
        if slot == 'conv_w_lo':
            blocks = jnp.zeros_like(blocks)
        pieces.append(blocks.astype(BF16))
    return _pack_flat(pieces, ml)


def _my_pos():
    return lax.axis_index('x'), lax.axis_index('y'), lax.axis_index('c')


def _flip(v, bit):
    return 1 - v if bit else v


def all_gather_blocks(x):
    rows = x.shape[0]

    def body(x_ref, out_ref, send_sems, recv_sems, local_sem):
        mx, my, mc = _my_pos()
        me, sibling = (mx, my, mc), (mx, my, 1 - mc)
        chips = [(1 - mx, my), (mx, 1 - my), (1 - mx, 1 - my)]

        def block(px, py, pc):
            return out_ref.at[4 * px + 2 * py + pc]

        def copy(k, blk, to, src=None):
            return pltpu.make_async_remote_copy(
                src_ref=block(*blk) if src is None else src, dst_ref=block(*blk),
                send_sem=send_sems.at[k], recv_sem=recv_sems.at[k], device_id=to, device_id_type=MESH)

        mine = pltpu.make_async_copy(x_ref, block(*me), local_sem)
        mine.start()
        first = [copy(0, me, sibling, src=x_ref)]
        first += [copy(1 + j, me, (*chip, mc), src=x_ref) for j, chip in enumerate(chips)]
        for cp in first:
            cp.start()
        passed = [copy(4 + j, (*chip, mc), sibling) for j, chip in enumerate(chips)]
        for j, chip in enumerate(chips):
            copy(1 + j, (*chip, mc), me).wait_recv()
            passed[j].start()
        copy(0, sibling, me).wait_recv()
        for j, chip in enumerate(chips):
            copy(4 + j, (*chip, 1 - mc), me).wait_recv()
        for cp in first + passed:
            cp.wait_send()
        mine.wait()

    return pl.pallas_call(
        body, name='all_gather_weights',
        out_shape=jax.ShapeDtypeStruct((N_DEV, rows, LANES), x.dtype),
        in_specs=[pl.BlockSpec(memory_space=pl.ANY)], out_specs=pl.BlockSpec(memory_space=pl.ANY),
        scratch_shapes=[pltpu.SemaphoreType.DMA((7,)), pltpu.SemaphoreType.DMA((7,)), pltpu.SemaphoreType.DMA],
    )(x)


HBM_SPEC = pl.BlockSpec(memory_space=pltpu.HBM)
SEM_SPEC = pl.BlockSpec(memory_space=pltpu.SEMAPHORE)
DATAFLOW_EFFECT = pltpu.SideEffectType.DATAFLOW_SIDE_EFFECTING


def _direct_copies(src_ref, land_ref, send_sems, recv_sems, per_peer):
    mx, my, mc = _my_pos()
    me = 4 * mx + 2 * my + mc
    copies = []
    for k in range(1, N_DEV):
        peer = (_flip(mx, k & 4), _flip(my, k & 2), _flip(mc, k & 1))
        pidx = 4 * peer[0] + 2 * peer[1] + peer[2]
        copies.append(pltpu.make_async_remote_copy(
            src_ref=src_ref.at[pidx] if per_peer else src_ref, dst_ref=land_ref.at[me],
            send_sem=send_sems.at[k - 1], recv_sem=recv_sems.at[k - 1], device_id=peer, device_id_type=MESH))
    return copies


def send_start(src, per_peer, name):
    block = src.shape[1:] if per_peer else src.shape
    land_shape = (N_DEV,) + tuple(block)

    def body(src_ref, land_ref, send_sems, recv_sems, src_thru, land_thru, token):
        for cp in _direct_copies(src_ref, land_ref, send_sems, recv_sems, per_peer):
            cp.start()
        token[...] = jnp.zeros(token.shape, F32)

    send_sems, recv_sems, src_thru, land_thru, token = pl.pallas_call(
        body, name=name,
        out_shape=(pltpu.SemaphoreType.DMA((N_DEV - 1,)), pltpu.SemaphoreType.DMA((N_DEV - 1,)),
                   pltpu.HBM(src.shape, src.dtype), pltpu.HBM(land_shape, src.dtype),
                   jax.ShapeDtypeStruct((8, LANES), F32)),
        in_specs=(HBM_SPEC, HBM_SPEC),
        out_specs=(SEM_SPEC, SEM_SPEC, HBM_SPEC, HBM_SPEC, pl.BlockSpec(memory_space=pltpu.VMEM)),
        input_output_aliases={0: 2, 1: 3},
        compiler_params=pltpu.CompilerParams(has_side_effects=DATAFLOW_EFFECT),
    )(pltpu.with_memory_space_constraint(src, pltpu.HBM),
      pltpu.with_memory_space_constraint(lax.empty(land_shape, src.dtype), pltpu.HBM))
    return (send_sems, recv_sems, src_thru, land_thru), token[0, 0]


def send_wait(handles, after, per_peer, name):
    send_sems, recv_sems, src_thru, land_thru = handles

    def body(src_ref, land_ref, send_sems, recv_sems, after_ref, src_dead, got_ref):
        for cp in _direct_copies(src_ref, land_ref, send_sems, recv_sems, per_peer):
            cp.wait_send()
            cp.wait_recv()

    return pl.pallas_call(
        body, name=name,
        out_shape=(pltpu.HBM(src_thru.shape, src_thru.dtype), pltpu.HBM(land_thru.shape, land_thru.dtype)),
        in_specs=(HBM_SPEC, HBM_SPEC, SEM_SPEC, SEM_SPEC, pl.BlockSpec(memory_space=pl.ANY)),
        out_specs=(HBM_SPEC, HBM_SPEC), input_output_aliases={0: 0, 1: 1},
        compiler_params=pltpu.CompilerParams(has_side_effects=DATAFLOW_EFFECT),
    )(src_thru, land_thru, send_sems, recv_sems, after)[1]


def _adamw_math(w, g, m, v):
    m2 = ADAM_B1 * m + (1.0 - ADAM_B1) * g
    v2 = ADAM_B2 * v + (1.0 - ADAM_B2) * (g * g)
    m_hat = m2 / (1.0 - ADAM_B1 ** ADAM_STEP)
    v_hat = v2 / (1.0 - ADAM_B2 ** ADAM_STEP)
    delta = -ADAM_LR * (m_hat / (jnp.sqrt(v_hat) + ADAM_EPS) + ADAM_WD * w)
    return delta, m2, v2


def sum_parts(parts, mine, name):
    _, rows, cols = parts.shape
    rb = rows
    while N_DEV * rb * cols * 2 > (2 << 20) and rb % 32 == 0:
        rb //= 2

    def body(p_ref, mine_ref, g_ref):
        mx, my, mc = _my_pos()
        me = 4 * mx + 2 * my + mc
        own = mine_ref[0].astype(F32)
        g = jnp.where(me == 0, own, p_ref[0].astype(F32))
        for j in range(1, N_DEV):
            g = g + jnp.where(me == j, own, p_ref[j].astype(F32))
        g_ref[...] = g

    return pl.pallas_call(
        body, name=name, grid=(rows // rb,),
        in_specs=[pl.BlockSpec((N_DEV, rb, cols), lambda i: (0, i, 0)), pl.BlockSpec((1, rb, cols), lambda i: (0, i, 0))],
        out_specs=pl.BlockSpec((rb, cols), lambda i: (i, 0)),
        out_shape=jax.ShapeDtypeStruct((rows, cols), F32), compiler_params=_cparams(('parallel',)),
    )(parts, mine)


ADAMW_BLOCK_BYTES = 1 << 20


def adamw_weight(name, w, m, v, grads):
    shape = w.shape
    if shape[2] % LANES and (shape[1] * shape[2]) % LANES == 0 and shape[2] < 4 * LANES:
        flat = lambda t: t.reshape(t.shape[:-2] + (-1, LANES))
        outs = adamw_weight(name, flat(w), flat(m), flat(v), [flat(g) for g in grads])
        return [o.reshape(shape) for o in outs]
    layers, a, b = w.shape
    ra = a
    while ra * b * 4 > ADAMW_BLOCK_BYTES and ra % 16 == 0:
        ra //= 2

    def body(*refs):
        w_ref, m_ref, v_ref = refs[:3]
        g_refs = refs[3:3 + layers]
        g_ref, d_ref, m2_ref, v2_ref = refs[3 + layers:]
        g = g_refs[0][...]
        for l in range(1, layers):
            g = jnp.where(pl.program_id(0) == l, g_refs[l][...], g)
        delta, m2, v2 = _adamw_math(w_ref[0], g, m_ref[0], v_ref[0])
        g_ref[0] = g
        d_ref[0] = delta
        m2_ref[0] = m2
        v2_ref[0] = v2

    blk = pl.BlockSpec((1, ra, b), lambda l, i: (l, i, 0))
    gblk = pl.BlockSpec((ra, b), lambda l, i: (i, 0))
    sh = jax.ShapeDtypeStruct(w.shape, F32)
    return pl.pallas_call(
        body, name='adamw_' + name, grid=(layers, a // ra), in_specs=[blk] * 3 + [gblk] * layers,
        out_specs=[blk] * 4, out_shape=[sh] * 4, compiler_params=_cparams(('parallel', 'parallel')),
    )(w, m, v, *grads)


def allreduce_adamw_small(g, w, m, v):
    rows = g.shape[0]

    def body(g_ref, w_ref, m_ref, v_ref, gs_ref, d_ref, m2_ref, v2_ref, all_ref, send_sems, recv_sems):
        mx, my, mc = _my_pos()
        me, sibling = (mx, my, mc), (mx, my, 1 - mc)
        chips = [(1 - mx, my), (mx, 1 - my), (1 - mx, 1 - my)]

        def block(px, py, pc):
            return all_ref.at[4 * px + 2 * py + pc]

        def copy(k, blk, to, src=None):
            return pltpu.make_async_remote_copy(
                src_ref=block(*blk) if src is None else src, dst_ref=block(*blk),
                send_sem=send_sems.at[k], recv_sem=recv_sems.at[k], device_id=to, device_id_type=MESH)

        first = [copy(0, me, sibling, src=g_ref)]
        first += [copy(1 + j, me, (*chip, mc), src=g_ref) for j, chip in enumerate(chips)]
        for cp in first:
            cp.start()
        passed = [copy(4 + j, (*chip, mc), sibling) for j, chip in enumerate(chips)]
        for j, chip in enumerate(chips):
            copy(1 + j, (*chip, mc), me).wait_recv()
            passed[j].start()
        copy(0, sibling, me).wait_recv()
        for j, chip in enumerate(chips):
            copy(4 + j, (*chip, 1 - mc), me).wait_recv()
        for cp in first + passed:
            cp.wait_send()
        my_idx = 4 * mx + 2 * my + mc
        total = jnp.zeros((rows, LANES), F32)
        for j in range(N_DEV):
            total = total + jnp.where(my_idx == j, g_ref[...], all_ref[j])
        delta, m2, v2 = _adamw_math(w_ref[...], total, m_ref[...], v_ref[...])
        gs_ref[...] = total
        d_ref[...] = delta
        m2_ref[...] = m2
        v2_ref[...] = v2

    vm = pl.BlockSpec(memory_space=pltpu.VMEM)
    sh = jax.ShapeDtypeStruct((rows, LANES), F32)
    return pl.pallas_call(
        body, name='allreduce_adamw_small', in_specs=[vm] * 4, out_specs=[vm] * 4, out_shape=[sh] * 4,
        scratch_shapes=[pltpu.VMEM((N_DEV, rows, LANES), F32), pltpu.SemaphoreType.DMA((7,)),
                        pltpu.SemaphoreType.DMA((7,))],
    )(g, w, m, v)


SMALL_COUNT = 11680


def _small_pack(d, extra=None):
    assert sum(d[n].size for n in SMALL_NAMES) == SMALL_COUNT
    flat = jnp.concatenate([d[n].reshape(-1) for n in SMALL_NAMES] + ([] if extra is None else [extra.reshape(1)]))
    rows = -(-(SMALL_COUNT + 1) // (8 * LANES)) * 8
    return jnp.pad(flat, (0, rows * LANES - flat.shape[0])).reshape(rows, LANES)


def _small_unpack(buf, like):
    flat = buf.reshape(-1)
    out, off = {}, 0
    for n in SMALL_NAMES:
        sz = int(np.prod(like[n].shape))
        out[n] = flat[off:off + sz].reshape(like[n].shape)
        off += sz
    return out


def _rope_tables(positions):
    freqs = ROPE_THETA ** (-(jnp.arange(ROPE_HALF, dtype=F32) * 2.0 / QK_ROPE))
    ang = positions.astype(F32)[:, None] * freqs
    sin = jnp.tile(jnp.sin(ang), (1, GROUP_HEADS))
    return jnp.tile(jnp.cos(ang), (1, 2 * GROUP_HEADS)), jnp.concatenate([-sin, sin], axis=1)


def _layer_weights(gathered, carriers):
    wd = {}
    for name, full in gathered.items():
        if name == 'w_in':
            wd[name] = {seg: W(full[lo:hi], c) for (seg, lo, hi), c in zip(WIN_SEGS, carriers[name])}
        else:
            wd[name] = W(full, carriers[name])
    return wd


F32_GRAD_WEIGHTS = ('rwkv_w2', 'rwkv_a2', 'rwkv_g2', 'rwkv_v2', 'conv_w')


def _make_carriers(gathered):
    gathered, carriers = dict(gathered), {}
    for name, full in gathered.items():
        if name == 'w_in':
            carriers[name] = tuple(jnp.zeros((hi - lo, D_MODEL), BF16) for _, lo, hi in WIN_SEGS)
        elif name == 'conv_w':
            gathered[name] = full.astype(BF16)
            carriers[name] = full - full.astype(BF16).astype(F32)
        else:
            carriers[name] = jnp.zeros(full.shape, F32 if name in F32_GRAD_WEIGHTS else BF16)
    return gathered, carriers


def _layer_small(small, ml):
    out = {}
    for n in SMALL_NAMES:
        l = ml if small[n].shape[0] == DEPTH else ml - 1
        if 0 <= l < small[n].shape[0]:
            out[n] = small[n][l]
    return out


def _after(value, tokens):
    return value + sum(tokens[1:], tokens[0])


def _train_step(x, positions, loss_target, weights, moms_m, moms_v):
    shards = {n: weights[n] for n in SHARDED}
    small = {n: weights[n] for n in SMALL_NAMES}
    me = 4 * lax.axis_index('x') + 2 * lax.axis_index('y') + lax.axis_index('c')
    cos, sin = _rope_tables(positions[0])

    units = [('pack', (0, 'in')), ('pack', (0, 'rest')), ('w_up', 0), ('w_down', 0),
             ('pack', 1), ('w_up', 1), ('w_down', 1)]
    first, later = units[0], units[1:]

    def tag(u):
        kind, key = u
        return f'{kind}_l{key[0]}{key[1]}' if isinstance(key, tuple) else f'{kind}_l{key}'

    def own_block(u):
        kind, key = u
        return _pack_shards(shards, key, BF16, True) if kind == 'pack' else shards[kind][key].astype(BF16)

    own = {u: own_block(u) for u in units}
    held = lax.optimization_barrier((all_gather_blocks(own[first]), *[own[u] for u in later]))
    blocks, gathers = {first: held[0]}, {}
    for u, mine in zip(later, held[1:]):
        own[u] = mine

    def start_gather(u):
        gathers[u], token = send_start(own[u], False, f'gather_{tag(u)}_start')
        return token

    first_token = start_gather(later[0])

    def gathered_block(u, after):
        if u not in blocks:
            landed = send_wait(gathers[u], after, False, f'gather_{tag(u)}_wait')
            blocks[u] = lax.dynamic_update_slice(landed, own[u][None], (me,) + (0,) * own[u].ndim)
        return blocks[u]

    def packed_weights(key, after):
        return _make_carriers(_unpack_gathered(gathered_block(('pack', key), after), key))

    def mlp_weights(l, after):
        up = gathered_block(('w_up', l), after)
        down = gathered_block(('w_down', l), after).reshape(D_FF, D_MODEL)
        return _make_carriers({'w_up': up, 'w_down': down})

    def small_of(layer, part):
        owner = lambda n: 'mlp' if n == 'mlp_norm' else ('in' if n == 'attn_norm' else 'rest')
        parts = ('in', 'rest') if part == 'mix' else (part,)
        return {n: v for n, v in _layer_small(small, layer).items() if owner(n) in parts}

    def mlp(layer, gathered):
        return lambda c, s, xx: _mlp(layer, xx, _layer_weights(gathered, c), s)

    gathered, carriers = packed_weights((0, 'in'), None)
    (proj, x_skip), vjp_in0 = jax.vjp(lambda c, s, xx: _layer_in(0, xx, _layer_weights(gathered, c), s),
                                      carriers, {n: _after(v, [first_token]) for n, v in small_of(0, 'in').items()},
                                      x[0])
    gathered, carriers = packed_weights((0, 'rest'), proj[2])
    landed = (blocks[later[0]][0, 0, 0] * 0).astype(F32)
    tokens = []
    for u in later[1:]:
        own[u] = own[u] + landed.astype(own[u].dtype)
        tokens.append(start_gather(u))
    proj = (proj[0], proj[1], _after(proj[2], tokens), proj[3], proj[4])
    (h, v_first), vjp_rest0 = jax.vjp(
        lambda c, s, p, xs, w=gathered: _layer_rest(0, p, xs, None, _layer_weights(w, c), s, cos, sin),
        carriers, small_of(0, 'rest'), proj, x_skip)
    gathered, carriers = mlp_weights(0, h)
    h, vjp_mlp0 = jax.vjp(mlp(0, gathered), carriers, small_of(0, 'mlp'), h)
    gathered, carriers = packed_weights(1, h)
    h, vjp_mix1 = jax.vjp(
        lambda c, s, xx, vf, w=gathered: _layer(1, xx, vf, _layer_weights(w, c), s, cos, sin)[0],
        carriers, small_of(1, 'mix'), h, v_first)
    gathered, carriers = mlp_weights(1, h)
    y, vjp_mlp1 = jax.vjp(mlp(1, gathered), carriers, small_of(1, 'mlp'), h)

    dy, loss_parts = _loss_call(y, loss_target[0])

    shipped = {}

    def ship(u, to_send):
        handles, token = send_start(to_send, True, f'grads_{tag(u)}_start')
        shipped[u] = (handles, to_send)
        return token

    def ship_mlp(l, gw):
        down = gw['w_down'].reshape(N_DEV, D_FF // N_DEV, D_MODEL)
        return [ship(('w_up', l), gw['w_up']), ship(('w_down', l), down)]

    gw, gs_mlp1, d = vjp_mlp1(dy)
    d = _after(d, ship_mlp(1, gw))
    gw, gs_mix1, d, dvf = vjp_mix1(d)
    d = _after(d, [ship(('pack', 1), _pack_grads(gw, 1))])
    gw, gs_mlp0, d = vjp_mlp0(d)
    d = _after(d, ship_mlp(0, gw))
    gw, gs_rest0, dproj, dx_skip = vjp_rest0((d, dvf))
    token = ship(('pack', (0, 'rest')), _pack_grads(gw, (0, 'rest')))
    dproj = (dproj[0], dproj[1], _after(dproj[2], [token]), dproj[3], dproj[4])
    gw, gs_in0, g_x = vjp_in0((dproj, dx_skip))
    last_token = ship(first, _pack_grads(gw, (0, 'in')))
    gs0, gs1 = {**gs_in0, **gs_rest0, **gs_mlp0}, {**gs_mix1, **gs_mlp1}

    def arrived(u, after):
        handles, sent = shipped[u]
        landed = send_wait(handles, after, True, f'grads_{tag(u)}_wait')
        return sum_parts(landed, lax.dynamic_slice_in_dim(sent, me, 1, axis=0), f'sum_grads_{tag(u)}')

    after = _after(g_x, [last_token])
    grads = {}
    for u in later:
        if u[0] == 'pack':
            grads.update(_unpack_shards(arrived(u, after), u[1]))
        else:
            grads[u] = arrived(u, after)
    sharded_out = [{}, {}, {}, {}]

    def update(n):
        outs = adamw_weight(n, weights[n], moms_m[n], moms_v[n], [grads[(n, l)] for l in range(SHARDED[n][0])])
        for i in range(4):
            sharded_out[i][n] = outs[i]

    last_weight = 'w_in'
    for n in SHARDED:
        if n != last_weight:
            update(n)

    g_small = {}
    for n in SMALL_NAMES:
        per = [g[n] for g in (gs0, gs1) if n in g]
        g_small[n] = jnp.stack(per)
    small_grads = _after(_small_pack(g_small, jnp.sum(loss_parts)), [sharded_out[1][MLP_WEIGHTS[-1]][0, 0, 0] * 0.0])
    small_bufs = allreduce_adamw_small(
        small_grads, _small_pack(small), _small_pack({n: moms_m[n] for n in SMALL_NAMES}),
        _small_pack({n: moms_v[n] for n in SMALL_NAMES}))
    loss = small_bufs[0].reshape(-1)[SMALL_COUNT]
    small_out = [_small_unpack(b, small) for b in small_bufs]

    grads.update(_unpack_shards(arrived(first, small_bufs[0]), first[1]))
    update(last_weight)

    pick = lambda i: [sharded_out[i][n] if n in SHARDED else small_out[i][n] for n in WEIGHT_NAMES]
    return (loss, g_x[None], *pick(0), *pick(1), *pick(2), *pick(3))


def kernel(x, positions, attn_norm, w_in, mla_q_a_norm, mla_wq_b, mla_kv_a_norm, mla_wkv_b, mla_q_norm, mla_k_norm, mla_w_o, rwkv_mu, rwkv_w0, rwkv_w2, rwkv_a0, rwkv_a2, rwkv_g2, rwkv_k_k, rwkv_k_a, rwkv_r_k, rwkv_ln_w, rwkv_ln_b, rwkv_w_o, rwkv_v1, rwkv_v_mu, rwkv_v0, rwkv_v2, conv_w, conv_w_o, w_out, mlp_norm, w_up, w_down, loss_target, m_attn_norm, m_w_in, m_mla_q_a_norm, m_mla_wq_b, m_mla_kv_a_norm, m_mla_wkv_b, m_mla_q_norm, m_mla_k_norm, m_mla_w_o, m_rwkv_mu, m_rwkv_w0, m_rwkv_w2, m_rwkv_a0, m_rwkv_a2, m_rwkv_g2, m_rwkv_k_k, m_rwkv_k_a, m_rwkv_r_k, m_rwkv_ln_w, m_rwkv_ln_b, m_rwkv_w_o, m_rwkv_v1, m_rwkv_v_mu, m_rwkv_v0, m_rwkv_v2, m_conv_w, m_conv_w_o, m_w_out, m_mlp_norm, m_w_up, m_w_down, v_attn_norm, v_w_in, v_mla_q_a_norm, v_mla_wq_b, v_mla_kv_a_norm, v_mla_wkv_b, v_mla_q_norm, v_mla_k_norm, v_mla_w_o, v_rwkv_mu, v_rwkv_w0, v_rwkv_w2, v_rwkv_a0, v_rwkv_a2, v_rwkv_g2, v_rwkv_k_k, v_rwkv_k_a, v_rwkv_r_k, v_rwkv_ln_w, v_rwkv_ln_b, v_rwkv_w_o, v_rwkv_v1, v_rwkv_v_mu, v_rwkv_v0, v_rwkv_v2, v_conv_w, v_conv_w_o, v_w_out, v_mlp_norm, v_w_up, v_w_down):
    args = locals()
    weights = {n: args[n] for n in WEIGHT_NAMES}
    moms_m = {n: args['m_' + n] for n in WEIGHT_NAMES}
    moms_v = {n: args['v_' + n] for n in WEIGHT_NAMES}
    return _train_step(x, positions, loss_target, weights, moms_m, moms_v)
```
